```python
import math
import jax, jax.numpy as jnp
from jax import lax
import numpy as np

D_MODEL = 1024
BATCH = 8
SEQ = 8192
DEPTH = 1

N_META = 16
GDN_HEADS = 4
GDN_DK = 128
GDN_DV = 128
GDN_CONV = 4
GDN_CHUNK = 64
GDN_QK = GDN_HEADS * GDN_DK
D_GDN = GDN_HEADS * GDN_DV
GDN_CONV_CH = 2 * GDN_QK + D_GDN
SB_HEADS = 8
SB_DH = 64
SB_BLOCK = 128
D_SB = SB_HEADS * SB_DH
D_MIX = D_GDN + D_SB
IN_SPLIT_SIZES = (GDN_QK, GDN_QK, D_GDN, GDN_HEADS, GDN_HEADS, D_GDN, D_SB, D_SB, D_SB)
D_IN = 2 * GDN_QK + 2 * D_GDN + 2 * GDN_HEADS + 3 * D_SB
D_FF = 2816
FFN_CONV = 3
NORM_EPS = 1e-6

kernel_name = "hymba_gdn_stickbreak_convffn_layer"


def rmsnorm(x, w, eps=NORM_EPS):
    xf = x.astype(jnp.float32)
    y = xf * lax.rsqrt(jnp.mean(xf * xf, axis=-1, keepdims=True) + eps)
    return (y * w.astype(jnp.float32)).astype(x.dtype)


def l2norm(x, eps=1e-6):
    return x * lax.rsqrt(jnp.sum(x * x, axis=-1, keepdims=True) + eps)


def causal_dwconv(x, w):
    width, ch = w.shape
    return lax.conv_general_dilated(
        x, w[:, None, :].astype(x.dtype), window_strides=(1,), padding=[(width - 1, 0)],
        dimension_numbers=("NWC", "WIO", "NWC"), feature_group_count=ch)


def gated_deltanet(q, k, v, a, b, z, conv_w, A_log, dt_bias, norm_w):
    Bsz, L, _ = q.shape
    H, C = GDN_HEADS, GDN_CHUNK
    qkv = jax.nn.silu(causal_dwconv(jnp.concatenate([q, k, v], axis=-1), conv_w)).astype(jnp.float32)
    q, k, v = jnp.split(qkv, [GDN_QK, 2 * GDN_QK], axis=-1)
    q = l2norm(q.reshape(Bsz, L, H, GDN_DK)) * (GDN_DK ** -0.5)
    k = l2norm(k.reshape(Bsz, L, H, GDN_DK))
    v = v.reshape(Bsz, L, H, GDN_DV)
    beta = jax.nn.sigmoid(b.astype(jnp.float32))
    g = -jnp.exp(A_log.astype(jnp.float32)) * jax.nn.softplus(a.astype(jnp.float32) + dt_bias.astype(jnp.float32))

    pad = (-L) % C
    pad4 = ((0, 0), (pad, 0), (0, 0), (0, 0))
    pad3 = ((0, 0), (pad, 0), (0, 0))
    q, k, v = jnp.pad(q, pad4), jnp.pad(k, pad4), jnp.pad(v, pad4)
    beta, g = jnp.pad(beta, pad3), jnp.pad(g, pad3)
    Lp = L + pad
    N = Lp // C
    chunk4 = lambda t: t.reshape(Bsz, N, C, H, t.shape[-1]).transpose(0, 3, 1, 2, 4)
    chunk3 = lambda t: t.reshape(Bsz, N, C, H).transpose(0, 3, 1, 2)
    qc, kc, vc = chunk4(q), chunk4(k), chunk4(v)
    bc, G = chunk3(beta), jnp.cumsum(chunk3(g), axis=-1)

    incl = jnp.tril(jnp.ones((C, C), bool))
    strict = jnp.tril(jnp.ones((C, C), bool), -1)
    decay = jnp.exp(jnp.where(incl, G[..., :, None] - G[..., None, :], -jnp.inf))
    kb = kc * bc[..., None]
    A = jnp.where(strict, jnp.einsum("bhnid,bhnjd->bhnij", kb, kc) * decay, 0.0)
    eye = jnp.broadcast_to(jnp.eye(C, dtype=jnp.float32), A.shape)
    T = lax.linalg.triangular_solve(eye + A, eye, left_side=True, lower=True, unit_diagonal=True)
    u = jnp.einsum("bhnij,bhnjd->bhnid", T, vc * bc[..., None])
    w = jnp.einsum("bhnij,bhnjd->bhnid", T, kb * jnp.exp(G)[..., None])
    qk = jnp.einsum("bhnid,bhnjd->bhnij", qc, kc) * decay
    q_dec = qc * jnp.exp(G)[..., None]
    k_dec = kc * jnp.exp(G[..., -1:] - G)[..., None]
    g_last = jnp.exp(G[..., -1])

    def step(S, inp):
        qk_n, u_n, w_n, qd_n, kd_n, gl_n = inp
        v_new = u_n - jnp.einsum("bhcd,bhde->bhce", w_n, S)
        o = jnp.einsum("bhcd,bhde->bhce", qd_n, S) + jnp.einsum("bhcs,bhse->bhce", qk_n, v_new)
        S = S * gl_n[..., None, None] + jnp.einsum("bhcd,bhce->bhde", kd_n, v_new)
        return S, o

    mv = lambda t: jnp.moveaxis(t, 2, 0)
    S0 = jnp.zeros((Bsz, H, GDN_DK, GDN_DV), jnp.float32)
    _, o = lax.scan(step, S0, (mv(qk), mv(u), mv(w), mv(q_dec), mv(k_dec), jnp.moveaxis(g_last, 2, 0)))
    o = o.transpose(1, 0, 3, 2, 4).reshape(Bsz, Lp, H, GDN_DV)[:, pad:]
    zg = jax.nn.silu(z.astype(jnp.float32).reshape(Bsz, L, H, GDN_DV))
    o = rmsnorm(o, norm_w) * zg
    return o.reshape(Bsz, L, D_GDN).astype(z.dtype)


def stick_breaking(q, k, v, norm_w):
    Bsz, L, _ = q.shape
    out_dtype = v.dtype
    H, Dh, BLK = SB_HEADS, SB_DH, SB_BLOCK
    pad = (-L) % BLK
    heads = lambda t: jnp.pad(t.astype(jnp.float32).reshape(Bsz, L, H, Dh), ((0, 0), (pad, 0), (0, 0), (0, 0))).transpose(0, 2, 1, 3)
    q, k, v = heads(q), heads(k), heads(v)
    Lp = L + pad
    nblk = Lp // BLK
    key_pos = jnp.arange(Lp)
    key_real = key_pos >= pad
    qb = q.reshape(Bsz, H, nblk, BLK, Dh).transpose(2, 0, 1, 3, 4)
    scale = Dh ** -0.5

    def block(args):
        q_blk, bi = args
        q_pos = bi * BLK + jnp.arange(BLK)
        zs = jnp.einsum("bhqd,bhkd->bhqk", q_blk, k) * scale
        mask = (key_pos[None, :] < q_pos[:, None]) & key_real[None, :]
        log_1m = jnp.where(mask, jax.nn.log_sigmoid(-zs), 0.0)
        log_surv = lax.cumsum(log_1m, axis=3, reverse=True) - log_1m
        att = jnp.exp(jnp.where(mask, jax.nn.log_sigmoid(zs) + log_surv, -jnp.inf))
        return jnp.einsum("bhqk,bhkd->bhqd", att, v)

    o = lax.map(block, (qb, jnp.arange(nblk)))
    o = o.transpose(1, 0, 3, 2, 4).reshape(Bsz, Lp, H, Dh)[:, pad:]
    o = rmsnorm(o, norm_w)
    return o.reshape(Bsz, L, D_SB).astype(out_dtype)


def conv_glu_ffn(x, w_up, conv_w, conv_b, w_down):
    h = causal_dwconv(x @ w_up, conv_w) + conv_b
    gate, up = jnp.split(h, 2, axis=-1)
    return (jax.nn.gelu(gate, approximate=True) * up) @ w_down


def _fwd_setup_inputs(seed: int = 0) -> dict:
    key = jax.random.key(seed)
    ks = jax.random.split(key, 20)
    f32 = jnp.float32
    nrm = lambda k, shape, s: jax.random.normal(k, shape, f32) * s
    dt = jnp.exp(jax.random.uniform(ks[5], (DEPTH, GDN_HEADS), f32, math.log(1e-3), math.log(1e-1)))
    return {
        "x": nrm(ks[0], (BATCH, SEQ, D_MODEL), 1.0),
        "meta_tokens": nrm(ks[1], (N_META, D_MODEL), 1.0),
        "attn_pre_norm": 1.0 + nrm(ks[2], (DEPTH, D_MODEL), 0.02),
        "w_in": nrm(ks[3], (DEPTH, D_MODEL, D_IN), D_MODEL ** -0.5),
        "gdn_conv_w": nrm(ks[4], (DEPTH, GDN_CONV, GDN_CONV_CH), GDN_CONV ** -0.5),
        "gdn_A_log": jnp.log(jax.random.uniform(ks[6], (DEPTH, GDN_HEADS), f32, 1.0, 16.0)),
        "gdn_dt_bias": dt + jnp.log(-jnp.expm1(-dt)),
        "gdn_norm_w": 1.0 + nrm(ks[7], (DEPTH, GDN_DV), 0.02),
        "sb_norm_w": 1.0 + nrm(ks[8], (DEPTH, SB_HEADS, SB_DH), 0.02),
        "w_out": nrm(ks[9], (DEPTH, D_MIX, D_MODEL), D_MIX ** -0.5),
        "attn_post_norm": 1.0 + nrm(ks[10], (DEPTH, D_MODEL), 0.02),
        "ffn_pre_norm": 1.0 + nrm(ks[11], (DEPTH, D_MODEL), 0.02),
        "w_ffn_up": nrm(ks[12], (DEPTH, D_MODEL, 2 * D_FF), D_MODEL ** -0.5),
        "ffn_conv_w": nrm(ks[13], (DEPTH, FFN_CONV, 2 * D_FF), FFN_CONV ** -0.5),
        "ffn_conv_b": nrm(ks[14], (DEPTH, 2 * D_FF), 0.01),
        "w_ffn_down": nrm(ks[15], (DEPTH, D_FF, D_MODEL), D_FF ** -0.5),
        "ffn_post_norm": 1.0 + nrm(ks[16], (DEPTH, D_MODEL), 0.02),
    }


def _fwd_reference(x, meta_tokens, attn_pre_norm, w_in, gdn_conv_w, gdn_A_log, gdn_dt_bias, gdn_norm_w,
              sb_norm_w, w_out, attn_post_norm, ffn_pre_norm, w_ffn_up, ffn_conv_w, ffn_conv_b,
              w_ffn_down, ffn_post_norm):
    Bsz = x.shape[0]
    meta = jnp.broadcast_to(meta_tokens[None].astype(x.dtype), (Bsz, N_META, D_MODEL))
    h = jnp.concatenate([meta, x], axis=1)
    split_idx = np.cumsum(IN_SPLIT_SIZES)[:-1].tolist()
    for l in range(DEPTH):
        u = rmsnorm(h, attn_pre_norm[l])
        q_g, k_g, v_g, a_g, b_g, z_g, q_s, k_s, v_s = jnp.split(u @ w_in[l], split_idx, axis=-1)
        y_g = gated_deltanet(q_g, k_g, v_g, a_g, b_g, z_g, gdn_conv_w[l], gdn_A_log[l], gdn_dt_bias[l], gdn_norm_w[l])
        y_s = stick_breaking(q_s, k_s, v_s, sb_norm_w[l])
        mix = jnp.concatenate([y_g, y_s], axis=-1) @ w_out[l]
        h = h + rmsnorm(mix, attn_post_norm[l])
        f = conv_glu_ffn(rmsnorm(h, ffn_pre_norm[l]), w_ffn_up[l], ffn_conv_w[l], ffn_conv_b[l], w_ffn_down[l])
        h = h + rmsnorm(f, ffn_post_norm[l])
    return h[:, N_META:]


import jax as _jax
import jax.numpy as _jnp

TWIN_FORMAT = 'train_step'
FWD_PARAMS = ['x', 'meta_tokens', 'attn_pre_norm', 'w_in', 'gdn_conv_w', 'gdn_A_log', 'gdn_dt_bias', 'gdn_norm_w', 'sb_norm_w', 'w_out', 'attn_post_norm', 'ffn_pre_norm', 'w_ffn_up', 'ffn_conv_w', 'ffn_conv_b', 'w_ffn_down', 'ffn_post_norm']
TWIN_WEIGHTS = ['meta_tokens', 'attn_pre_norm', 'w_in', 'gdn_conv_w', 'gdn_A_log', 'gdn_dt_bias', 'gdn_norm_w', 'sb_norm_w', 'w_out', 'attn_post_norm', 'ffn_pre_norm', 'w_ffn_up', 'ffn_conv_w', 'ffn_conv_b', 'w_ffn_down', 'ffn_post_norm']
TWIN_DIFF_INPUT = 'x'
TWIN_INPUTS = ['x', 'meta_tokens', 'attn_pre_norm', 'w_in', 'gdn_conv_w', 'gdn_A_log', 'gdn_dt_bias', 'gdn_norm_w', 'sb_norm_w', 'w_out', 'attn_post_norm', 'ffn_pre_norm', 'w_ffn_up', 'ffn_conv_w', 'ffn_conv_b', 'w_ffn_down', 'ffn_post_norm', 'loss_target', 'm_meta_tokens', 'm_attn_pre_norm', 'm_w_in', 'm_gdn_conv_w', 'm_gdn_A_log', 'm_gdn_dt_bias', 'm_gdn_norm_w', 'm_sb_norm_w', 'm_w_out', 'm_attn_post_norm', 'm_ffn_pre_norm', 'm_w_ffn_up', 'm_ffn_conv_w', 'm_ffn_conv_b', 'm_w_ffn_down', 'm_ffn_post_norm', 'v_meta_tokens', 'v_attn_pre_norm', 'v_w_in', 'v_gdn_conv_w', 'v_gdn_A_log', 'v_gdn_dt_bias', 'v_gdn_norm_w', 'v_sb_norm_w', 'v_w_out', 'v_attn_post_norm', 'v_ffn_pre_norm', 'v_w_ffn_up', 'v_ffn_conv_w', 'v_ffn_conv_b', 'v_w_ffn_down', 'v_ffn_post_norm']
TWIN_OUTPUTS = ['loss', 'grad_x', 'grad_meta_tokens', 'grad_attn_pre_norm', 'grad_w_in', 'grad_gdn_conv_w', 'grad_gdn_A_log', 'grad_gdn_dt_bias', 'grad_gdn_norm_w', 'grad_sb_norm_w', 'grad_w_out', 'grad_attn_post_norm', 'grad_ffn_pre_norm', 'grad_w_ffn_up', 'grad_ffn_conv_w', 'grad_ffn_conv_b', 'grad_w_ffn_down', 'grad_ffn_post_norm', 'delta_meta_tokens', 'delta_attn_pre_norm', 'delta_w_in', 'delta_gdn_conv_w', 'delta_gdn_A_log', 'delta_gdn_dt_bias', 'delta_gdn_norm_w', 'delta_sb_norm_w', 'delta_w_out', 'delta_attn_post_norm', 'delta_ffn_pre_norm', 'delta_w_ffn_up', 'delta_ffn_conv_w', 'delta_ffn_conv_b', 'delta_w_ffn_down', 'delta_ffn_post_norm', 'new_m_meta_tokens', 'new_m_attn_pre_norm', 'new_m_w_in', 'new_m_gdn_conv_w', 'new_m_gdn_A_log', 'new_m_gdn_dt_bias', 'new_m_gdn_norm_w', 'new_m_sb_norm_w', 'new_m_w_out', 'new_m_attn_post_norm', 'new_m_ffn_pre_norm', 'new_m_w_ffn_up', 'new_m_ffn_conv_w', 'new_m_ffn_conv_b', 'new_m_w_ffn_down', 'new_m_ffn_post_norm', 'new_v_meta_tokens', 'new_v_attn_pre_norm', 'new_v_w_in', 'new_v_gdn_conv_w', 'new_v_gdn_A_log', 'new_v_gdn_dt_bias', 'new_v_gdn_norm_w', 'new_v_sb_norm_w', 'new_v_w_out', 'new_v_attn_post_norm', 'new_v_ffn_pre_norm', 'new_v_w_ffn_up', 'new_v_ffn_conv_w', 'new_v_ffn_conv_b', 'new_v_w_ffn_down', 'new_v_ffn_post_norm']
TWIN_LEAF_KINDS = {'loss': 'loss', 'grad_x': 'grad_x', 'grad_meta_tokens': 'grad_w', 'grad_attn_pre_norm': 'grad_w', 'grad_w_in': 'grad_w', 'grad_gdn_conv_w': 'grad_w', 'grad_gdn_A_log': 'grad_w', 'grad_gdn_dt_bias': 'grad_w', 'grad_gdn_norm_w': 'grad_w', 'grad_sb_norm_w': 'grad_w', 'grad_w_out': 'grad_w', 'grad_attn_post_norm': 'grad_w', 'grad_ffn_pre_norm': 'grad_w', 'grad_w_ffn_up': 'grad_w', 'grad_ffn_conv_w': 'grad_w', 'grad_ffn_conv_b': 'grad_w', 'grad_w_ffn_down': 'grad_w', 'grad_ffn_post_norm': 'grad_w', 'delta_meta_tokens': 'delta_w', 'delta_attn_pre_norm': 'delta_w', 'delta_w_in': 'delta_w', 'delta_gdn_conv_w': 'delta_w', 'delta_gdn_A_log': 'delta_w', 'delta_gdn_dt_bias': 'delta_w', 'delta_gdn_norm_w': 'delta_w', 'delta_sb_norm_w': 'delta_w', 'delta_w_out': 'delta_w', 'delta_attn_post_norm': 'delta_w', 'delta_ffn_pre_norm': 'delta_w', 'delta_w_ffn_up': 'delta_w', 'delta_ffn_conv_w': 'delta_w', 'delta_ffn_conv_b': 'delta_w', 'delta_w_ffn_down': 'delta_w', 'delta_ffn_post_norm': 'delta_w', 'new_m_meta_tokens': 'new_m', 'new_m_attn_pre_norm': 'new_m', 'new_m_w_in': 'new_m', 'new_m_gdn_conv_w': 'new_m', 'new_m_gdn_A_log': 'new_m', 'new_m_gdn_dt_bias': 'new_m', 'new_m_gdn_norm_w': 'new_m', 'new_m_sb_norm_w': 'new_m', 'new_m_w_out': 'new_m', 'new_m_attn_post_norm': 'new_m', 'new_m_ffn_pre_norm': 'new_m', 'new_m_w_ffn_up': 'new_m', 'new_m_ffn_conv_w': 'new_m', 'new_m_ffn_conv_b': 'new_m', 'new_m_w_ffn_down': 'new_m', 'new_m_ffn_post_norm': 'new_m', 'new_v_meta_tokens': 'new_v', 'new_v_attn_pre_norm': 'new_v', 'new_v_w_in': 'new_v', 'new_v_gdn_conv_w': 'new_v', 'new_v_gdn_A_log': 'new_v', 'new_v_gdn_dt_bias': 'new_v', 'new_v_gdn_norm_w': 'new_v', 'new_v_sb_norm_w': 'new_v', 'new_v_w_out': 'new_v', 'new_v_attn_post_norm': 'new_v', 'new_v_ffn_pre_norm': 'new_v', 'new_v_w_ffn_up': 'new_v', 'new_v_ffn_conv_w': 'new_v', 'new_v_ffn_conv_b': 'new_v', 'new_v_w_ffn_down': 'new_v', 'new_v_ffn_post_norm': 'new_v'}


def _forward(args):
    return _fwd_reference(*[args[k] for k in FWD_PARAMS])


def _output_shape():
    def fwd():
        inp = _fwd_setup_inputs(0)
        return _fwd_reference(*[inp[k] for k in FWD_PARAMS])
    out = _jax.eval_shape(fwd)
    return out.shape, out.dtype

N_MICROBATCH = 1
ADAM_LR = 0.001
ADAM_B1 = 0.9
ADAM_B2 = 0.999
ADAM_EPS = 1e-08
ADAM_WD = 0.01
ADAM_STEP = 10
PER_EXAMPLE_BATCH_AXIS = {'x': 0, 'loss_target': 0}
SHARED_INPUTS = []
_WEIGHT_DTYPES = {'meta_tokens': _jnp.float32, 'attn_pre_norm': _jnp.float32, 'w_in': _jnp.float32, 'gdn_conv_w': _jnp.float32, 'gdn_A_log': _jnp.float32, 'gdn_dt_bias': _jnp.float32, 'gdn_norm_w': _jnp.float32, 'sb_norm_w': _jnp.float32, 'w_out': _jnp.float32, 'attn_post_norm': _jnp.float32, 'ffn_pre_norm': _jnp.float32, 'w_ffn_up': _jnp.float32, 'ffn_conv_w': _jnp.float32, 'ffn_conv_b': _jnp.float32, 'w_ffn_down': _jnp.float32, 'ffn_post_norm': _jnp.float32}
MOMENT_SCALE = {'meta_tokens': 2.476700e-02, 'attn_pre_norm': 9.153708e-01, 'w_in': 4.935038e-01, 'gdn_conv_w': 3.934178e-01, 'gdn_A_log': 1.487451e+00, 'gdn_dt_bias': 1.415363e+00, 'gdn_norm_w': 1.323132e+00, 'sb_norm_w': 1.278929e+00, 'w_out': 8.507643e-01, 'attn_post_norm': 6.405084e+01, 'ffn_pre_norm': 9.454653e-01, 'w_ffn_up': 3.689361e-01, 'ffn_conv_w': 4.113844e-01, 'ffn_conv_b': 1.134686e+00, 'w_ffn_down': 7.562719e-01, 'ffn_post_norm': 6.403083e+01}


def _to_microbatches(a, axis):
    t = _jnp.moveaxis(a, axis, 0)
    t = t.reshape((N_MICROBATCH, t.shape[0] // N_MICROBATCH) + t.shape[1:])
    return _jnp.moveaxis(t, 1, axis + 1)


def setup_inputs(seed: int = 0) -> dict:
    inp = _fwd_setup_inputs(seed)
    key = _jax.random.fold_in(_jax.random.key(seed), 7919)
    shape, _ = _output_shape()
    out = dict(inp)
    out["loss_target"] = _jax.random.normal(_jax.random.fold_in(key, 0), shape, _jnp.float32)
    for i, name in enumerate(TWIN_WEIGHTS):
        w = inp[name].astype(_jnp.float32)
        if MOMENT_SCALE is None:
            s = _jnp.sqrt(_jnp.mean(_jnp.square(w)) + 1e-30)
        else:
            s = MOMENT_SCALE[name]
        km, kv = _jax.random.split(_jax.random.fold_in(key, i + 1))
        out[name] = w
        out["m_" + name] = s * _jax.random.normal(km, w.shape, _jnp.float32)
        out["v_" + name] = (s * s) * _jax.random.uniform(kv, w.shape, _jnp.float32, 0.5, 1.5)
    if N_MICROBATCH > 1:
        for name, axis in PER_EXAMPLE_BATCH_AXIS.items():
            out[name] = _to_microbatches(out[name], axis)
    return {'x': out['x'], 'meta_tokens': out['meta_tokens'], 'attn_pre_norm': out['attn_pre_norm'], 'w_in': out['w_in'], 'gdn_conv_w': out['gdn_conv_w'], 'gdn_A_log': out['gdn_A_log'], 'gdn_dt_bias': out['gdn_dt_bias'], 'gdn_norm_w': out['gdn_norm_w'], 'sb_norm_w': out['sb_norm_w'], 'w_out': out['w_out'], 'attn_post_norm': out['attn_post_norm'], 'ffn_pre_norm': out['ffn_pre_norm'], 'w_ffn_up': out['w_ffn_up'], 'ffn_conv_w': out['ffn_conv_w'], 'ffn_conv_b': out['ffn_conv_b'], 'w_ffn_down': out['w_ffn_down'], 'ffn_post_norm': out['ffn_post_norm'], 'loss_target': out['loss_target'], 'm_meta_tokens': out['m_meta_tokens'], 'm_attn_pre_norm': out['m_attn_pre_norm'], 'm_w_in': out['m_w_in'], 'm_gdn_conv_w': out['m_gdn_conv_w'], 'm_gdn_A_log': out['m_gdn_A_log'], 'm_gdn_dt_bias': out['m_gdn_dt_bias'], 'm_gdn_norm_w': out['m_gdn_norm_w'], 'm_sb_norm_w': out['m_sb_norm_w'], 'm_w_out': out['m_w_out'], 'm_attn_post_norm': out['m_attn_post_norm'], 'm_ffn_pre_norm': out['m_ffn_pre_norm'], 'm_w_ffn_up': out['m_w_ffn_up'], 'm_ffn_conv_w': out['m_ffn_conv_w'], 'm_ffn_conv_b': out['m_ffn_conv_b'], 'm_w_ffn_down': out['m_w_ffn_down'], 'm_ffn_post_norm': out['m_ffn_post_norm'], 'v_meta_tokens': out['v_meta_tokens'], 'v_attn_pre_norm': out['v_attn_pre_norm'], 'v_w_in': out['v_w_in'], 'v_gdn_conv_w': out['v_gdn_conv_w'], 'v_gdn_A_log': out['v_gdn_A_log'], 'v_gdn_dt_bias': out['v_gdn_dt_bias'], 'v_gdn_norm_w': out['v_gdn_norm_w'], 'v_sb_norm_w': out['v_sb_norm_w'], 'v_w_out': out['v_w_out'], 'v_attn_post_norm': out['v_attn_post_norm'], 'v_ffn_pre_norm': out['v_ffn_pre_norm'], 'v_w_ffn_up': out['v_w_ffn_up'], 'v_ffn_conv_w': out['v_ffn_conv_w'], 'v_ffn_conv_b': out['v_ffn_conv_b'], 'v_w_ffn_down': out['v_w_ffn_down'], 'v_ffn_post_norm': out['v_ffn_post_norm']}


def _loss(weights, diff, rest, loss_target):
    with _jax.named_scope("forward"):
        args = {**rest, TWIN_DIFF_INPUT: diff, **{k: w.astype(_WEIGHT_DTYPES[k]) for k, w in weights.items()}}
        y = _forward(args)
    with _jax.named_scope("loss_head"):
        err = _jnp.square(y.astype(_jnp.float32) - loss_target)
        return 0.5 * _jnp.sum(_jnp.mean(err, axis=-1)) if err.ndim else 0.5 * err


def _adamw(w, g, m, v):
    m = ADAM_B1 * m + (1.0 - ADAM_B1) * g
    v = ADAM_B2 * v + (1.0 - ADAM_B2) * _jnp.square(g)
    m_hat = m / (1.0 - ADAM_B1 ** ADAM_STEP)
    v_hat = v / (1.0 - ADAM_B2 ** ADAM_STEP)
    delta = -ADAM_LR * (m_hat / (_jnp.sqrt(v_hat) + ADAM_EPS) + ADAM_WD * w)
    return delta, m, v


def reference(x, meta_tokens, attn_pre_norm, w_in, gdn_conv_w, gdn_A_log, gdn_dt_bias, gdn_norm_w, sb_norm_w, w_out, attn_post_norm, ffn_pre_norm, w_ffn_up, ffn_conv_w, ffn_conv_b, w_ffn_down, ffn_post_norm, loss_target, m_meta_tokens, m_attn_pre_norm, m_w_in, m_gdn_conv_w, m_gdn_A_log, m_gdn_dt_bias, m_gdn_norm_w, m_sb_norm_w, m_w_out, m_attn_post_norm, m_ffn_pre_norm, m_w_ffn_up, m_ffn_conv_w, m_ffn_conv_b, m_w_ffn_down, m_ffn_post_norm, v_meta_tokens, v_attn_pre_norm, v_w_in, v_gdn_conv_w, v_gdn_A_log, v_gdn_dt_bias, v_gdn_norm_w, v_sb_norm_w, v_w_out, v_attn_post_norm, v_ffn_pre_norm, v_w_ffn_up, v_ffn_conv_w, v_ffn_conv_b, v_w_ffn_down, v_ffn_post_norm):
    given = dict(x=x, meta_tokens=meta_tokens, attn_pre_norm=attn_pre_norm, w_in=w_in, gdn_conv_w=gdn_conv_w, gdn_A_log=gdn_A_log, gdn_dt_bias=gdn_dt_bias, gdn_norm_w=gdn_norm_w, sb_norm_w=sb_norm_w, w_out=w_out, attn_post_norm=attn_post_norm, ffn_pre_norm=ffn_pre_norm, w_ffn_up=w_ffn_up, ffn_conv_w=ffn_conv_w, ffn_conv_b=ffn_conv_b, w_ffn_down=w_ffn_down, ffn_post_norm=ffn_post_norm, loss_target=loss_target, m_meta_tokens=m_meta_tokens, m_attn_pre_norm=m_attn_pre_norm, m_w_in=m_w_in, m_gdn_conv_w=m_gdn_conv_w, m_gdn_A_log=m_gdn_A_log, m_gdn_dt_bias=m_gdn_dt_bias, m_gdn_norm_w=m_gdn_norm_w, m_sb_norm_w=m_sb_norm_w, m_w_out=m_w_out, m_attn_post_norm=m_attn_post_norm, m_ffn_pre_norm=m_ffn_pre_norm, m_w_ffn_up=m_w_ffn_up, m_ffn_conv_w=m_ffn_conv_w, m_ffn_conv_b=m_ffn_conv_b, m_w_ffn_down=m_w_ffn_down, m_ffn_post_norm=m_ffn_post_norm, v_meta_tokens=v_meta_tokens, v_attn_pre_norm=v_attn_pre_norm, v_w_in=v_w_in, v_gdn_conv_w=v_gdn_conv_w, v_gdn_A_log=v_gdn_A_log, v_gdn_dt_bias=v_gdn_dt_bias, v_gdn_norm_w=v_gdn_norm_w, v_sb_norm_w=v_sb_norm_w, v_w_out=v_w_out, v_attn_post_norm=v_attn_post_norm, v_ffn_pre_norm=v_ffn_pre_norm, v_w_ffn_up=v_w_ffn_up, v_ffn_conv_w=v_ffn_conv_w, v_ffn_conv_b=v_ffn_conv_b, v_w_ffn_down=v_w_ffn_down, v_ffn_post_norm=v_ffn_post_norm)
    weights = {n: given[n] for n in TWIN_WEIGHTS}
    shared = {n: given[n] for n in SHARED_INPUTS}
    per_example = {n: given[n] for n in ['x']}
    grad_fn = _jax.value_and_grad(_loss, argnums=(0, 1))

    def one_microbatch(ex, loss_target):
        ex = dict(ex)
        diff = ex.pop(TWIN_DIFF_INPUT)
        return grad_fn(weights, diff, {**shared, **ex}, loss_target)

    if N_MICROBATCH == 1:
        loss, (grad_w, grad_x) = one_microbatch(per_example, given["loss_target"])
    else:
        def body(carry, xs):
            loss_sum, grad_sum = carry
            l_k, (gw_k, gx_k) = one_microbatch(xs[0], xs[1])
            with _jax.named_scope("update"):
                return (loss_sum + l_k, _jax.tree.map(_jnp.add, grad_sum, gw_k)), gx_k

        init = (_jnp.zeros((), _jnp.float32), _jax.tree.map(_jnp.zeros_like, weights))
        (loss, grad_w), grad_x = _jax.lax.scan(body, init, (per_example, given["loss_target"]))
    with _jax.named_scope("update"):
        delta_w, new_m, new_v = {}, {}, {}
        for n in TWIN_WEIGHTS:
            delta_w[n], new_m[n], new_v[n] = _adamw(weights[n], grad_w[n], given["m_" + n], given["v_" + n])
    return (loss, grad_x, *[grad_w[n] for n in TWIN_WEIGHTS], *[delta_w[n] for n in TWIN_WEIGHTS],
            *[new_m[n] for n in TWIN_WEIGHTS], *[new_v[n] for n in TWIN_WEIGHTS])
```

```python
import functools

import jax
import jax.numpy as jnp
from jax import lax
from jax.experimental import pallas as pl
from jax.experimental.pallas import tpu as pltpu

F32 = jnp.float32
BF16 = jnp.bfloat16

D_MODEL = 1024
N_META = 16
GDN_HEADS = 4
GDN_D = 128
GDN_CHUNK = 64
GDN_CONV = 4
SB_HEADS = 8
SB_DH = 64
SB_BLOCK = 128
D_FF = 2816
FFN_CONV = 3
NORM_EPS = 1e-6
L2_EPS = 1e-6
N_DEV = 8

PAD_ROWS = SB_BLOCK - N_META
ROW0 = SB_BLOCK
SB_SPAN = 256
LP_ALIGN = 256

C_QKV = 3 * GDN_HEADS * GDN_D
C_Z = GDN_HEADS * GDN_D
C_SB = 3 * SB_HEADS * SB_DH
C_AB = 256
OFF_Z = C_QKV
OFF_SB = OFF_Z + C_Z
OFF_AB = OFF_SB + C_SB
D_INP = OFF_AB + C_AB
D_IN = C_QKV + 2 * GDN_HEADS + C_Z + C_SB

ADAM_LR = 0.001
ADAM_B1 = 0.9
ADAM_B2 = 0.999
ADAM_EPS = 1e-08
ADAM_WD = 0.01
ADAM_STEP = 10

VMEM_LIMIT = 56 * 1024 * 1024
MESH = pl.DeviceIdType.MESH


def _cp(sem=None):
    kw = dict(vmem_limit_bytes=VMEM_LIMIT)
    if sem is not None:
        kw["dimension_semantics"] = sem
    return pltpu.CompilerParams(**kw)


def _tile(n, cap, unit=128):
    best = None
    t = unit
    while t <= min(n, cap):
        if n % t == 0:
            best = t
        t += unit
    assert best is not None, (n, cap, unit)
    return best


def _dot(a, b):
    return jnp.dot(a, b, preferred_element_type=F32)


def _dot_nt(a, b):
    return lax.dot_general(a, b, (((1,), (1,)), ((), ())), preferred_element_type=F32)


def _dot_tn(a, b):
    return lax.dot_general(a, b, (((0,), (0,)), ((), ())), preferred_element_type=F32)


def _split(x):
    hi = x.astype(BF16)
    lo = (x - hi.astype(F32)).astype(BF16)
    return hi, lo


def _dot3(a, b, f=_dot):
    ah, al = _split(a)
    bh, bl = _split(b)
    return f(ah, bh) + (f(ah, bl) + f(al, bh))


def _dot_exact_l(m_bf16, x, f=_dot):
    xh, xl = _split(x)
    return f(m_bf16, xh) + f(m_bf16, xl)


def _dot_exact_r(x, m_bf16, f=_dot):
    xh, xl = _split(x)
    return f(xh, m_bf16) + f(xl, m_bf16)


def _iota2(shape, dim):
    return lax.broadcasted_iota(jnp.int32, shape, dim)


def _sigmoid(x):
    return 1.0 / (1.0 + jnp.exp(-x))


def _softplus(x):
    return jnp.maximum(x, 0.0) + jnp.log(1.0 + jnp.exp(-jnp.abs(x)))


def _colsum(x):
    return jnp.sum(x, axis=0, keepdims=True)


def _rowsum(x):
    return jnp.sum(x, axis=-1, keepdims=True)


def _mm(a, b, out_dtype, name):
    M, K = a.shape
    K2, N = b.shape
    assert K == K2
    tm = _tile(M, 768)
    tn = _tile(N, max(128, (6 * 1024 * 1024) // (2 * K)))

    def body(a_ref, b_ref, o_ref):
        o_ref[...] = _dot(a_ref[...].astype(BF16), b_ref[...].astype(BF16)).astype(o_ref.dtype)

    return pl.pallas_call(
        body, name=name, grid=(N // tn, M // tm),
        in_specs=[pl.BlockSpec((tm, K), lambda j, i: (i, 0)), pl.BlockSpec((K, tn), lambda j, i: (0, j))],
        out_specs=pl.BlockSpec((tm, tn), lambda j, i: (i, j)),
        out_shape=jax.ShapeDtypeStruct((M, N), out_dtype),
        compiler_params=_cp(("parallel", "parallel")),
    )(a, b)


def _mm_tn(a, b, name):
    M, K = a.shape
    M2, N = b.shape
    assert M == M2
    tm = _tile(M, 768)
    tk = _tile(K, 1024)
    tn = _tile(N, 1408)

    def body(a_ref, b_ref, o_ref):
        @pl.when(pl.program_id(2) == 0)
        def _():
            o_ref[...] = jnp.zeros_like(o_ref)
        o_ref[...] += _dot_tn(a_ref[...].astype(BF16), b_ref[...].astype(BF16))

    return pl.pallas_call(
        body, name=name, grid=(K // tk, N // tn, M // tm),
        in_specs=[pl.BlockSpec((tm, tk), lambda i, j, m: (m, i)), pl.BlockSpec((tm, tn), lambda i, j, m: (m, j))],
        out_specs=pl.BlockSpec((tk, tn), lambda i, j, m: (i, j)),
        out_shape=jax.ShapeDtypeStruct((K, N), F32),
        compiler_params=_cp(("parallel", "parallel", "arbitrary")),
    )(a, b)


def _rms(x):
    return lax.rsqrt(jnp.mean(x * x, axis=-1, keepdims=True) + NORM_EPS)


def _rms_bwd(x, w, dy):
    r = _rms(x)
    n = x * r
    dyw = dy * w
    dx = r * (dyw - n * jnp.mean(dyw * n, axis=-1, keepdims=True))
    return dx, dy * n


def _prenorm_fwd(h0, w):
    LP, D = h0.shape
    T = _tile(LP, 512)

    def body(h_ref, w_ref, u_ref):
        h = h_ref[...]
        u_ref[...] = (h * _rms(h) * w_ref[...]).astype(BF16)

    return pl.pallas_call(
        body, name="prenorm_fwd", grid=(LP // T,),
        in_specs=[pl.BlockSpec((T, D), lambda i: (i, 0)), pl.BlockSpec((1, D), lambda i: (0, 0))],
        out_specs=pl.BlockSpec((T, D), lambda i: (i, 0)),
        out_shape=jax.ShapeDtypeStruct((LP, D), BF16),
        compiler_params=_cp(("parallel",)),
    )(h0, w)


def _prenorm_bwd(h0, w, du, dh1):
    LP, D = h0.shape
    T = _tile(LP, 512)

    def body(h_ref, w_ref, du_ref, dh1_ref, dh0_ref, dw_ref):
        @pl.when(pl.program_id(0) == 0)
        def _():
            dw_ref[...] = jnp.zeros_like(dw_ref)
        dx, dwn = _rms_bwd(h_ref[...], w_ref[...], du_ref[...])
        dh0_ref[...] = dh1_ref[...] + dx
        dw_ref[...] += _colsum(dwn)

    row = pl.BlockSpec((T, D), lambda i: (i, 0))
    vec = pl.BlockSpec((1, D), lambda i: (0, 0))
    return pl.pallas_call(
        body, name="prenorm_bwd", grid=(LP // T,),
        in_specs=[row, vec, row, row], out_specs=[row, vec],
        out_shape=[jax.ShapeDtypeStruct((LP, D), F32), jax.ShapeDtypeStruct((1, D), F32)],
        compiler_params=_cp(("arbitrary",)),
    )(h0, w, du, dh1)


def _gdn_gate_consts(alog_ref, dtb_ref, h):
    a_coef = -jnp.exp(alog_ref[0:1, h:h + 1])
    return a_coef, dtb_ref[0:1, h:h + 1]


def _gdn_pre_fwd(proj, conv_w, a_log, dt_bias):
    LP = proj.shape[0]
    T = _tile(LP, 256)
    C = C_QKV
    H = GDN_HEADS

    def body(x_ref, halo_ref, ab_ref, cw_ref, alog_ref, dtb_ref, q_ref, k_ref, v_ref, beta_ref, g_ref, xs):
        i = pl.program_id(0)
        xs[0:8, :] = jnp.where(i > 0, halo_ref[...], 0.0)
        xs[8:, :] = x_ref[...]
        y = jnp.zeros((T, C), F32)
        for j in range(GDN_CONV):
            y = y + cw_ref[j:j + 1, :] * xs[pl.ds(8 - (GDN_CONV - 1) + j, T), :]
        c = y * _sigmoid(y)
        for h in range(H):
            sl = slice(h * GDN_D, (h + 1) * GDN_D)
            cq = c[:, sl]
            q_ref[:, sl] = cq * lax.rsqrt(_rowsum(cq * cq) + L2_EPS) * (GDN_D ** -0.5)
            ck = c[:, 512 + h * GDN_D:512 + (h + 1) * GDN_D]
            k_ref[:, sl] = ck * lax.rsqrt(_rowsum(ck * ck) + L2_EPS)
        v_ref[...] = c[:, 1024:]
        ab = ab_ref[...]
        valid = (i * T + _iota2((T, 1), 0)) >= PAD_ROWS
        for h in range(H):
            sl = slice(h * GDN_D, (h + 1) * GDN_D)
            a_coef, dtb = _gdn_gate_consts(alog_ref, dtb_ref, h)
            g = jnp.where(valid, a_coef * _softplus(ab[:, h:h + 1] + dtb), 0.0)
            beta = jnp.where(valid, _sigmoid(ab[:, H + h:H + h + 1]), 0.0)
            g_ref[:, sl] = jnp.broadcast_to(g, (T, GDN_D))
            beta_ref[:, sl] = jnp.broadcast_to(beta, (T, GDN_D))

    t8 = T // 8
    row512 = pl.BlockSpec((T, 512), lambda i: (i, 0))
    small = lambda r, c: pl.BlockSpec((r, c), lambda i: (0, 0))
    out = jax.ShapeDtypeStruct((LP, 512), F32)
    return pl.pallas_call(
        body, name="gdn_pre_fwd", grid=(LP // T,),
        in_specs=[pl.BlockSpec((T, C), lambda i: (i, 0)),
                  pl.BlockSpec((8, C), lambda i: (jnp.maximum(i * t8 - 1, 0), 0)),
                  pl.BlockSpec((T, C_AB), lambda i: (i, OFF_AB // C_AB)),
                  small(GDN_CONV, C), small(1, H), small(1, H)],
        out_specs=[row512] * 5, out_shape=[out] * 5,
        scratch_shapes=[pltpu.VMEM((T + 8, C), F32)],
        compiler_params=_cp(("parallel",)),
    )(proj, proj, proj, conv_w, a_log, dt_bias)


def _gdn_pre_bwd(proj, conv_w, a_log, dt_bias, dq, dk, dv, dbeta, dg):
    LP = proj.shape[0]
    T = _tile(LP, 256)
    C = C_QKV
    H = GDN_HEADS
    TE = T + 8
    nt = LP // T

    def body(x_ref, xp_ref, xn_ref, ab_ref, cw_ref, alog_ref, dtb_ref,
             dq_ref, dqn_ref, dk_ref, dkn_ref, dv_ref, dvn_ref, dbeta_ref, dg_ref,
             dx_ref, dab_ref, dcw_ref, dsc_ref, xs, dys):
        i = pl.program_id(0)

        @pl.when(i == 0)
        def _():
            dcw_ref[...] = jnp.zeros_like(dcw_ref)
            dsc_ref[...] = jnp.zeros_like(dsc_ref)

        last = i == nt - 1
        xs[0:8, :] = jnp.where(i > 0, xp_ref[...], 0.0)
        xs[8:8 + T, :] = x_ref[...]
        xs[8 + T:, :] = jnp.where(last, 0.0, xn_ref[...])
        y = jnp.zeros((TE, C), F32)
        for j in range(GDN_CONV):
            y = y + cw_ref[j:j + 1, :] * xs[pl.ds(8 - (GDN_CONV - 1) + j, TE), :]
        sg = _sigmoid(y)
        c = y * sg
        nxt = lambda a_ref, b_ref: jnp.concatenate([a_ref[...], jnp.where(last, 0.0, b_ref[...])], axis=0)
        dqn = nxt(dq_ref, dqn_ref)
        dkn = nxt(dk_ref, dkn_ref)
        dvv = nxt(dv_ref, dvn_ref)
        for h in range(H):
            sl = slice(h * GDN_D, (h + 1) * GDN_D)
            cq = c[:, sl]
            rq = lax.rsqrt(_rowsum(cq * cq) + L2_EPS)
            nq = cq * rq
            dqh = dqn[:, sl]
            dys[:, sl] = (GDN_D ** -0.5) * rq * (dqh - nq * _rowsum(dqh * nq))
            sk = slice(512 + h * GDN_D, 512 + (h + 1) * GDN_D)
            ck = c[:, sk]
            rk = lax.rsqrt(_rowsum(ck * ck) + L2_EPS)
            nk = ck * rk
            dkh = dkn[:, sl]
            dys[:, sk] = rk * (dkh - nk * _rowsum(dkh * nk))
        dys[:, 1024:] = dvv
        dy = dys[...] * (sg * (1.0 + y * (1.0 - sg)))
        dys[...] = dy
        for j in range(GDN_CONV):
            dcw_ref[j:j + 1, :] += _colsum(dy[0:T, :] * xs[pl.ds(8 - (GDN_CONV - 1) + j, T), :])
        dx = jnp.zeros((T, C), F32)
        for j in range(GDN_CONV):
            dx = dx + cw_ref[j:j + 1, :] * dys[pl.ds(GDN_CONV - 1 - j, T), :]
        dx_ref[...] = dx
        ab = ab_ref[...]
        valid = (i * T + _iota2((T, 1), 0)) >= PAD_ROWS
        lane = _iota2((T, C_AB), 1)
        lane1 = _iota2((1, 128), 1)
        dab = jnp.zeros((T, C_AB), F32)
        dsc_a = jnp.zeros((1, 128), F32)
        dsc_d = jnp.zeros((1, 128), F32)
        for h in range(H):
            a_coef, dtb = _gdn_gate_consts(alog_ref, dtb_ref, h)
            pre = ab[:, h:h + 1] + dtb
            dgh = jnp.where(valid, dg_ref[:, h * GDN_D:h * GDN_D + 1], 0.0)
            da = dgh * a_coef * _sigmoid(pre)
            beta = _sigmoid(ab[:, H + h:H + h + 1])
            db = jnp.where(valid, dbeta_ref[:, h * GDN_D:h * GDN_D + 1], 0.0) * beta * (1.0 - beta)
            dab = dab + jnp.where(lane == h, da, 0.0) + jnp.where(lane == H + h, db, 0.0)
            dsc_a = dsc_a + jnp.where(lane1 == h, _colsum(dgh * a_coef * _softplus(pre)), 0.0)
            dsc_d = dsc_d + jnp.where(lane1 == h, _colsum(da), 0.0)
        dab_ref[...] = dab
        dsc_ref[0:1, :] += dsc_a
        dsc_ref[1:2, :] += dsc_d

    t8 = T // 8
    nb8 = LP // 8
    prev8 = lambda w: pl.BlockSpec((8, w), lambda i: (jnp.maximum(i * t8 - 1, 0), 0))
    next8 = lambda w: pl.BlockSpec((8, w), lambda i: (jnp.minimum((i + 1) * t8, nb8 - 1), 0))
    row = lambda w: pl.BlockSpec((T, w), lambda i: (i, 0))
    small = lambda r, c: pl.BlockSpec((r, c), lambda i: (0, 0))
    return pl.pallas_call(
        body, name="gdn_pre_bwd", grid=(nt,),
        in_specs=[row(C), prev8(C), next8(C), pl.BlockSpec((T, C_AB), lambda i: (i, OFF_AB // C_AB)),
                  small(GDN_CONV, C), small(1, H), small(1, H),
                  row(512), next8(512), row(512), next8(512), row(512), next8(512), row(512), row(512)],
        out_specs=[row(C), row(C_AB), small(GDN_CONV, C), small(2, 128)],
        out_shape=[jax.ShapeDtypeStruct((LP, C), F32), jax.ShapeDtypeStruct((LP, C_AB), F32),
                   jax.ShapeDtypeStruct((GDN_CONV, C), F32), jax.ShapeDtypeStruct((2, 128), F32)],
        scratch_shapes=[pltpu.VMEM((T + 16, C), F32), pltpu.VMEM((TE, C), F32)],
        compiler_params=_cp(("arbitrary",)),
    )(proj, proj, proj, proj, conv_w, a_log, dt_bias, dq, dq, dk, dk, dv, dv, dbeta, dg)


def _tri_masks():
    r = _iota2((GDN_CHUNK, GDN_CHUNK), 0)
    c = _iota2((GDN_CHUNK, GDN_CHUNK), 1)
    return r >= c, r > c


def _gdn_chunk_common(q, k, v, beta, gb):
    incl, strict = _tri_masks()
    l_incl = incl.astype(BF16)
    gd = _dot_exact_l(l_incl, jnp.where(strict, gb[:, :GDN_CHUNK], 0.0))
    gc = _dot_exact_l(l_incl, gb)
    decay = jnp.where(incl, jnp.exp(jnp.where(incl, gd, 0.0)), 0.0)
    exp_g = jnp.exp(gc)
    g_last = gc[GDN_CHUNK - 1:GDN_CHUNK, :]
    kd_fac = jnp.exp(g_last - gc)
    gl = jnp.exp(g_last)
    kb = k * beta
    kk = _dot3(kb, k, _dot_nt)
    return dict(incl=incl, strict=strict, decay=decay, exp_g=exp_g, kd_fac=kd_fac, gl=gl, kb=kb, kk=kk,
                vb=v * beta, kbg=kb * exp_g)


def _gdn_chunk_fwd(qn, kn, v, beta_b, g_b):
    LP = qn.shape[0]
    R = 128
    H = GDN_HEADS
    CH = GDN_CHUNK

    def body(q_ref, k_ref, v_ref, b_ref, g_ref, u_ref, w_ref, qd_ref, kd_ref, qk_ref, t_ref, gl_ref):
        for cc in range(R // CH):
            rs = slice(cc * CH, (cc + 1) * CH)
            for h in range(H):
                sl = slice(h * GDN_D, (h + 1) * GDN_D)
                s64 = slice(h * CH, (h + 1) * CH)
                q, k = q_ref[rs, sl], k_ref[rs, sl]
                m = _gdn_chunk_common(q, k, v_ref[rs, sl], b_ref[rs, sl], g_ref[rs, sl])
                a = jnp.where(m["strict"], m["kk"] * m["decay"], 0.0)
                eye = (_iota2((CH, CH), 0) == _iota2((CH, CH), 1)).astype(F32)
                p = -a
                t = eye + p
                for _ in range(5):
                    p = _dot3(p, p)
                    t = t + _dot3(t, p)
                u_ref[rs, sl] = _dot3(t, m["vb"])
                w_ref[rs, sl] = _dot3(t, m["kbg"])
                qk_ref[rs, s64] = _dot3(q, k, _dot_nt) * m["decay"]
                t_ref[rs, s64] = t
                qd_ref[rs, sl] = q * m["exp_g"]
                kd_ref[rs, sl] = k * m["kd_fac"]
                gl_ref[cc * 8:(cc + 1) * 8, sl] = jnp.broadcast_to(m["gl"], (8, GDN_D))

    row = lambda w: pl.BlockSpec((R, w), lambda i: (i, 0))
    o512 = jax.ShapeDtypeStruct((LP, 512), F32)
    o256 = jax.ShapeDtypeStruct((LP, 256), F32)
    return pl.pallas_call(
        body, name="gdn_chunk_fwd", grid=(LP // R,),
        in_specs=[row(512)] * 5,
        out_specs=[row(512)] * 4 + [row(256)] * 2 + [pl.BlockSpec((R // 8, 512), lambda i: (i, 0))],
        out_shape=[o512] * 4 + [o256] * 2 + [jax.ShapeDtypeStruct((LP // 8, 512), F32)],
        compiler_params=_cp(("parallel",)),
    )(qn, kn, v, beta_b, g_b)


def _gdn_chunk_bwd(qn, kn, v, beta_b, g_b, t_all, du, dw, dqd, dkd, dqk, dgl):
    LP = qn.shape[0]
    R = 128
    H = GDN_HEADS
    CH = GDN_CHUNK

    def body(q_ref, k_ref, v_ref, b_ref, g_ref, t_ref, du_ref, dw_ref, dqd_ref, dkd_ref, dqk_ref, dgl_ref,
             dq_ref, dk_ref, dv_ref, db_ref, dg_ref):
        ones = jnp.ones((CH, GDN_D), BF16)
        for cc in range(R // CH):
            rs = slice(cc * CH, (cc + 1) * CH)
            for h in range(H):
                sl = slice(h * GDN_D, (h + 1) * GDN_D)
                s64 = slice(h * CH, (h + 1) * CH)
                q, k, vv, beta = q_ref[rs, sl], k_ref[rs, sl], v_ref[rs, sl], b_ref[rs, sl]
                m = _gdn_chunk_common(q, k, vv, beta, g_ref[rs, sl])
                incl, strict, decay = m["incl"], m["strict"], m["decay"]
                t = t_ref[rs, s64]
                du_, dw_ = du_ref[rs, sl], dw_ref[rs, sl]
                dqd_, dkd_ = dqd_ref[rs, sl], dkd_ref[rs, sl]
                d_t = _dot3(du_, m["vb"], _dot_nt) + _dot3(dw_, m["kbg"], _dot_nt)
                dvb = _dot3(t, du_, _dot_tn)
                dkbg = _dot3(t, dw_, _dot_tn)
                dkb = dkbg * m["exp_g"]
                d_gi = _rowsum(dkbg * m["kbg"])
                x1 = _dot3(d_t, t, _dot_nt)
                d_a = jnp.where(strict, -_dot3(t, x1, _dot_tn), 0.0)
                d_kk = d_a * decay
                d_dec = d_a * m["kk"]
                dkb = dkb + _dot3(d_kk, k)
                dk_ = _dot3(d_kk, m["kb"], _dot_tn)
                dqk_m = jnp.where(incl, dqk_ref[rs, s64], 0.0)
                qk_raw = _dot3(q, k, _dot_nt)
                dqk_raw = dqk_m * decay
                d_dec = d_dec + dqk_m * qk_raw
                dq_ = _dot3(dqk_raw, k) + dqd_ * m["exp_g"]
                dk_ = dk_ + _dot3(dqk_raw, q, _dot_tn)
                mm = d_dec * decay
                d_gi = d_gi + (_dot_exact_r(mm, ones) - _dot_exact_r(mm, ones, _dot_tn))
                d_gi = d_gi + _rowsum(dqd_ * q * m["exp_g"])
                kd = k * m["kd_fac"]
                e = _rowsum(dkd_ * kd)
                d_gi = d_gi - e
                d_glast = _colsum(jnp.broadcast_to(e, (CH, GDN_D))) + dgl_ref[cc * 8:cc * 8 + 1, sl] * m["gl"]
                dk_ = dk_ + dkd_ * m["kd_fac"] + dkb * beta
                d_gi = d_gi + jnp.where(_iota2((CH, GDN_D), 0) == CH - 1, d_glast, 0.0)
                u_incl = (_iota2((CH, CH), 1) >= _iota2((CH, CH), 0)).astype(BF16)
                dq_ref[rs, sl] = dq_
                dk_ref[rs, sl] = dk_
                dv_ref[rs, sl] = dvb * beta
                db_ref[rs, sl] = jnp.broadcast_to(_rowsum(dvb * vv) + _rowsum(dkb * k), (CH, GDN_D))
                dg_ref[rs, sl] = _dot_exact_l(u_incl, d_gi)

    row = lambda w: pl.BlockSpec((R, w), lambda i: (i, 0))
    o512 = jax.ShapeDtypeStruct((LP, 512), F32)
    gl_spec = pl.BlockSpec((R // 8, 512), lambda i: (i, 0))
    return pl.pallas_call(
        body, name="gdn_chunk_bwd", grid=(LP // R,),
        in_specs=[row(512)] * 5 + [row(256)] + [row(512)] * 4 + [row(256), gl_spec],
        out_specs=[row(512)] * 5, out_shape=[o512] * 5,
        compiler_params=_cp(("parallel",)),
    )(qn, kn, v, beta_b, g_b, t_all, du, dw, dqd, dkd, dqk, dgl)


def _gdn_scan_fwd(u, w, qd, kd, qk, gl):
    LP = u.shape[0]
    CH = GDN_CHUNK
    N = LP // CH
    H = GDN_HEADS

    def body(u_ref, w_ref, qd_ref, kd_ref, qk_ref, gl_ref, o_ref, ssave_ref, s_sc):
        @pl.when(pl.program_id(0) == 0)
        def _():
            s_sc[...] = jnp.zeros_like(s_sc)
        ssave_ref[...] = s_sc[...]
        for h in range(H):
            sl = slice(h * GDN_D, (h + 1) * GDN_D)
            s = s_sc[:, sl]
            v_new = u_ref[:, sl] - _dot3(w_ref[:, sl], s)
            o_ref[:, sl] = _dot3(qd_ref[:, sl], s) + _dot3(qk_ref[:, h * CH:(h + 1) * CH], v_new)
            s_sc[:, sl] = s * gl_ref[0:1, sl] + _dot3(kd_ref[:, sl], v_new, _dot_tn)

    row = lambda w_: pl.BlockSpec((CH, w_), lambda n: (n, 0))
    return pl.pallas_call(
        body, name="gdn_scan_fwd", grid=(N,),
        in_specs=[row(512)] * 4 + [row(256), pl.BlockSpec((8, 512), lambda n: (n, 0))],
        out_specs=[row(512), pl.BlockSpec((GDN_D, 512), lambda n: (n, 0))],
        out_shape=[jax.ShapeDtypeStruct((LP, 512), F32), jax.ShapeDtypeStruct((N * GDN_D, 512), F32)],
        scratch_shapes=[pltpu.VMEM((GDN_D, 512), F32)],
        compiler_params=_cp(("arbitrary",)),
    )(u, w, qd, kd, qk, gl)


def _gdn_scan_bwd(u, w, qd, kd, qk, gl, ssave, do):
    LP = u.shape[0]
    CH = GDN_CHUNK
    N = LP // CH
    H = GDN_HEADS

    def body(u_ref, w_ref, qd_ref, kd_ref, qk_ref, gl_ref, s_ref, do_ref,
             du_ref, dw_ref, dqd_ref, dkd_ref, dqk_ref, dgl_ref, ds_sc):
        @pl.when(pl.program_id(0) == 0)
        def _():
            ds_sc[...] = jnp.zeros_like(ds_sc)
        for h in range(H):
            sl = slice(h * GDN_D, (h + 1) * GDN_D)
            s64 = slice(h * CH, (h + 1) * CH)
            s = s_ref[:, sl]
            ds = ds_sc[:, sl]
            do_ = do_ref[:, sl]
            w_, qd_, kd_, qk_ = w_ref[:, sl], qd_ref[:, sl], kd_ref[:, sl], qk_ref[:, s64]
            v_new = u_ref[:, sl] - _dot3(w_, s)
            d_vnew = _dot3(qk_, do_, _dot_tn) + _dot3(kd_, ds)
            du_ref[:, sl] = d_vnew
            dw_ref[:, sl] = -_dot3(d_vnew, s, _dot_nt)
            dqd_ref[:, sl] = _dot3(do_, s, _dot_nt)
            dkd_ref[:, sl] = _dot3(v_new, ds, _dot_nt)
            dqk_ref[:, s64] = _dot3(do_, v_new, _dot_nt)
            dgl_ref[:, sl] = jnp.broadcast_to(jnp.sum(_colsum(ds * s), axis=-1, keepdims=True), (8, GDN_D))
            ds_sc[:, sl] = ds * gl_ref[0:1, sl] + _dot3(qd_, do_, _dot_tn) - _dot3(w_, d_vnew, _dot_tn)

    rev = lambda w_: pl.BlockSpec((CH, w_), lambda n: (N - 1 - n, 0))
    rev8 = pl.BlockSpec((8, 512), lambda n: (N - 1 - n, 0))
    o512 = jax.ShapeDtypeStruct((LP, 512), F32)
    return pl.pallas_call(
        body, name="gdn_scan_bwd", grid=(N,),
        in_specs=[rev(512)] * 4 + [rev(256), rev8, pl.BlockSpec((GDN_D, 512), lambda n: (N - 1 - n, 0)), rev(512)],
        out_specs=[rev(512)] * 4 + [rev(256), rev8],
        out_shape=[o512] * 4 + [jax.ShapeDtypeStruct((LP, 256), F32), jax.ShapeDtypeStruct((LP // 8, 512), F32)],
        scratch_shapes=[pltpu.VMEM((GDN_D, 512), F32)],
        compiler_params=_cp(("arbitrary",)),
    )(u, w, qd, kd, qk, gl, ssave, do)


def _sb_scores(qh, kblk, mask):
    z = _dot_nt(qh, kblk)
    e = jnp.exp(-jnp.abs(z))
    sp = jnp.maximum(z, 0.0) + jnp.log(1.0 + e)
    return z, e, jnp.where(mask, -sp, 0.0), z - sp


def _sb_fwd(proj):
    LP = proj.shape[0]
    B = SB_BLOCK
    W = SB_SPAN
    nq = LP // B
    nsub = W // B
    scale = SB_DH ** -0.5
    qcol, kcol, vcol = OFF_SB // B, (OFF_SB + 512) // B, (OFF_SB + 1024) // B

    def body(q_ref, k_ref, v_ref, o_ref, c_ref):
        i = pl.program_id(1)
        lane = _iota2((B, B), 1)
        head_a = lane < SB_DH
        qs = q_ref[...] * scale
        qh = [jnp.where(head_a, qs, 0.0).astype(BF16), jnp.where(head_a, 0.0, qs).astype(BF16)]
        u_strict = (_iota2((B, B), 0) > lane).astype(BF16)
        qpos = i * B + _iota2((B, W), 0)
        nspan = (i * B + B + W - 1) // W

        def span(sr, carry):
            s = nspan - 1 - sr
            k0 = pl.multiple_of(s * W, W)
            kblk = k_ref[pl.ds(k0, W), :].astype(BF16)
            vblk = v_ref[pl.ds(k0, W), :].astype(BF16)
            kpos = k0 + _iota2((B, W), 1)
            mask = (kpos < qpos) & (kpos >= PAD_ROWS)
            new = []
            for h in range(2):
                o_acc, c = carry[2 * h], carry[2 * h + 1]
                z, e, l1m, lsg = _sb_scores(qh[h], kblk, mask)
                parts = [None] * nsub
                for b in reversed(range(nsub)):
                    bs = slice(b * B, (b + 1) * B)
                    suf = _dot_exact_r(l1m[:, bs], u_strict)
                    parts[b] = jnp.where(mask[:, bs], jnp.exp(lsg[:, bs] + suf + c), 0.0)
                    c = c + _rowsum(l1m[:, bs])
                att = jnp.concatenate(parts, axis=1).astype(BF16)
                new += [o_acc + _dot(att, vblk), c]
            return tuple(new)

        zero_o = jnp.zeros((B, B), F32)
        zero_c = jnp.zeros((B, 1), F32)
        o_a, c_a, o_b, c_b = lax.fori_loop(0, nspan, span, (zero_o, zero_c, zero_o, zero_c))
        o_ref[...] = jnp.where(head_a, o_a, o_b)
        c_ref[...] = jnp.where(head_a, c_a, c_b)

    blk = pl.BlockSpec((B, B), lambda p, i: (i, p))
    out = jax.ShapeDtypeStruct((LP, 512), F32)
    return pl.pallas_call(
        body, name="sb_fwd", grid=(SB_HEADS // 2, nq),
        in_specs=[pl.BlockSpec((B, B), lambda p, i: (i, qcol + p)),
                  pl.BlockSpec((LP, B), lambda p, i: (0, kcol + p)),
                  pl.BlockSpec((LP, B), lambda p, i: (0, vcol + p))],
        out_specs=[blk, blk], out_shape=[out, out],
        compiler_params=_cp(("parallel", "arbitrary")),
    )(proj, proj, proj)


def _sb_bwd(proj, ctot, do):
    LP = proj.shape[0]
    B = SB_BLOCK
    W = SB_SPAN
    nq = LP // B
    nsub = W // B
    scale = SB_DH ** -0.5
    qcol, kcol, vcol = OFF_SB // B, (OFF_SB + 512) // B, (OFF_SB + 1024) // B

    def body(q_ref, k_ref, v_ref, c_ref, do_ref, dq_ref, dk_ref, dv_ref):
        i = pl.program_id(1)

        @pl.when(i == 0)
        def _():
            dk_ref[...] = jnp.zeros_like(dk_ref)
            dv_ref[...] = jnp.zeros_like(dv_ref)

        lane = _iota2((B, B), 1)
        head_a = lane < SB_DH
        qs = q_ref[...] * scale
        qh = [jnp.where(head_a, qs, 0.0).astype(BF16), jnp.where(head_a, 0.0, qs).astype(BF16)]
        dof = do_ref[...]
        doh = [jnp.where(head_a, dof, 0.0).astype(BF16), jnp.where(head_a, 0.0, dof).astype(BF16)]
        cfull = c_ref[...]
        ctot_h = [cfull[:, 0:1], cfull[:, SB_DH:SB_DH + 1]]
        u_strict = (_iota2((B, B), 0) > lane).astype(BF16)
        l_strict = (_iota2((B, B), 0) < lane).astype(BF16)
        qpos = i * B + _iota2((B, W), 0)
        nspan = (i * B + B + W - 1) // W

        def span(s, carry):
            k0 = pl.multiple_of(s * W, W)
            kblk = k_ref[pl.ds(k0, W), :].astype(BF16)
            vblk = v_ref[pl.ds(k0, W), :].astype(BF16)
            kpos = k0 + _iota2((B, W), 1)
            mask = (kpos < qpos) & (kpos >= PAD_ROWS)
            new = []
            dk_add = jnp.zeros((W, B), F32)
            dv_add = jnp.zeros((W, B), F32)
            for h in range(2):
                dq_acc, pre, ecar = carry[3 * h], carry[3 * h + 1], carry[3 * h + 2]
                z, e, l1m, lsg = _sb_scores(qh[h], kblk, mask)
                d_att = _dot_nt(doh[h], vblk)
                sig = jnp.where(z >= 0.0, 1.0, e) / (1.0 + e)
                att_parts, dz_parts = [None] * nsub, [None] * nsub
                for b in range(nsub):
                    bs = slice(b * B, (b + 1) * B)
                    suf = _dot_exact_r(l1m[:, bs], u_strict)
                    pre = pre + _rowsum(l1m[:, bs])
                    att = jnp.where(mask[:, bs], jnp.exp(lsg[:, bs] + suf + (ctot_h[h] - pre)), 0.0)
                    p = att * d_att[:, bs]
                    dcum = ecar + _dot_exact_r(p, l_strict)
                    ecar = ecar + _rowsum(p)
                    sg = sig[:, bs]
                    dz_parts[b] = jnp.where(mask[:, bs], p * (1.0 - sg) - sg * dcum, 0.0)
                    att_parts[b] = att
                att = jnp.concatenate(att_parts, axis=1).astype(BF16)
                dz = jnp.concatenate(dz_parts, axis=1).astype(BF16)
                dq_acc = dq_acc + _dot(dz, kblk)
                dk_add = dk_add + _dot_tn(dz, qh[h])
                dv_add = dv_add + _dot_tn(att, doh[h])
                new += [dq_acc, pre, ecar]
            dk_ref[pl.ds(k0, W), :] += dk_add
            dv_ref[pl.ds(k0, W), :] += dv_add
            return tuple(new)

        zero_o = jnp.zeros((B, B), F32)
        zero_c = jnp.zeros((B, 1), F32)
        res = lax.fori_loop(0, nspan, span, (zero_o, zero_c, zero_c, zero_o, zero_c, zero_c))
        dq_ref[...] = jnp.where(head_a, res[0], res[3]) * scale

    blk = pl.BlockSpec((B, B), lambda p, i: (i, p))
    col = pl.BlockSpec((LP, B), lambda p, i: (0, p))
    out = jax.ShapeDtypeStruct((LP, 512), F32)
    return pl.pallas_call(
        body, name="sb_bwd", grid=(SB_HEADS // 2, nq),
        in_specs=[pl.BlockSpec((B, B), lambda p, i: (i, qcol + p)),
                  pl.BlockSpec((LP, B), lambda p, i: (0, kcol + p)),
                  pl.BlockSpec((LP, B), lambda p, i: (0, vcol + p)),
                  blk, blk],
        out_specs=[blk, col, col], out_shape=[out, out, out],
        compiler_params=_cp(("parallel", "arbitrary")),
    )(proj, proj, proj, ctot, do)


def _sb_group_mean():
    r = jnp.right_shift(_iota2((512, 512), 0), 6)
    c = jnp.right_shift(_iota2((512, 512), 1), 6)
    return jnp.where(r == c, 1.0 / SB_DH, 0.0).astype(BF16)


def _attn_norm_fwd(og, proj, osb, gnw, snw):
    LP = og.shape[0]
    T = _tile(LP, 256)

    def body(og_ref, z_ref, os_ref, gnw_ref, snw_ref, y_ref):
        valid = (pl.program_id(0) * T + _iota2((T, 1), 0)) >= PAD_ROWS
        z = z_ref[...]
        zg = z * _sigmoid(z)
        for h in range(GDN_HEADS):
            sl = slice(h * GDN_D, (h + 1) * GDN_D)
            o = og_ref[:, sl]
            y = o * _rms(o) * gnw_ref[...] * zg[:, sl]
            y_ref[:, sl] = jnp.where(valid, y, 0.0).astype(BF16)
        o = os_ref[...]
        msq = _dot_exact_r(o * o, _sb_group_mean())
        y = o * lax.rsqrt(msq + NORM_EPS) * snw_ref[...]
        y_ref[:, 512:] = jnp.where(valid, y, 0.0).astype(BF16)

    row = pl.BlockSpec((T, 512), lambda i: (i, 0))
    return pl.pallas_call(
        body, name="attn_norm_fwd", grid=(LP // T,),
        in_specs=[row, pl.BlockSpec((T, 512), lambda i: (i, OFF_Z // 512)), row,
                  pl.BlockSpec((1, GDN_D), lambda i: (0, 0)), pl.BlockSpec((1, 512), lambda i: (0, 0))],
        out_specs=pl.BlockSpec((T, 1024), lambda i: (i, 0)),
        out_shape=jax.ShapeDtypeStruct((LP, 1024), BF16),
        compiler_params=_cp(("parallel",)),
    )(og, proj, osb, gnw, snw)


def _attn_norm_bwd(og, proj, osb, gnw, snw, dy):
    LP = og.shape[0]
    T = _tile(LP, 256)

    def body(og_ref, z_ref, os_ref, gnw_ref, snw_ref, dy_ref, dog_ref, dz_ref, dos_ref, dgw_ref, dsw_ref):
        @pl.when(pl.program_id(0) == 0)
        def _():
            dgw_ref[...] = jnp.zeros_like(dgw_ref)
            dsw_ref[...] = jnp.zeros_like(dsw_ref)
        valid = (pl.program_id(0) * T + _iota2((T, 1), 0)) >= PAD_ROWS
        dy = jnp.where(valid, dy_ref[...], 0.0)
        z = z_ref[...]
        sg = _sigmoid(z)
        zg = z * sg
        dgw = jnp.zeros((1, GDN_D), F32)
        for h in range(GDN_HEADS):
            sl = slice(h * GDN_D, (h + 1) * GDN_D)
            o = og_ref[:, sl]
            dyh = dy[:, sl]
            dx, dwn = _rms_bwd(o, gnw_ref[...], dyh * zg[:, sl])
            dog_ref[:, sl] = dx
            dgw = dgw + _colsum(dwn)
            yn = o * _rms(o) * gnw_ref[...]
            dz_ref[:, sl] = dyh * yn * (sg[:, sl] * (1.0 + z[:, sl] * (1.0 - sg[:, sl])))
        dgw_ref[...] += dgw
        o = os_ref[...]
        gm = _sb_group_mean()
        r = lax.rsqrt(_dot_exact_r(o * o, gm) + NORM_EPS)
        n = o * r
        dys = dy[:, 512:]
        dyw = dys * snw_ref[...]
        dos_ref[...] = r * (dyw - n * _dot_exact_r(dyw * n, gm))
        dsw_ref[...] += _colsum(dys * n)

    row = pl.BlockSpec((T, 512), lambda i: (i, 0))
    gw = pl.BlockSpec((1, GDN_D), lambda i: (0, 0))
    sw = pl.BlockSpec((1, 512), lambda i: (0, 0))
    o512 = jax.ShapeDtypeStruct((LP, 512), F32)
    return pl.pallas_call(
        body, name="attn_norm_bwd", grid=(LP // T,),
        in_specs=[row, pl.BlockSpec((T, 512), lambda i: (i, OFF_Z // 512)), row, gw, sw,
                  pl.BlockSpec((T, 1024), lambda i: (i, 0))],
        out_specs=[row, row, row, gw, sw],
        out_shape=[o512, o512, o512, jax.ShapeDtypeStruct((1, GDN_D), F32), jax.ShapeDtypeStruct((1, 512), F32)],
        compiler_params=_cp(("arbitrary",)),
    )(og, proj, osb, gnw, snw, dy)


def _resid_fwd(h0, mix, w_post, w_pre):
    LP, D = h0.shape
    T = _tile(LP, 512)

    def body(h0_ref, mix_ref, wp_ref, wf_ref, h1_ref, n2_ref):
        mix = mix_ref[...]
        h1 = h0_ref[...] + mix * _rms(mix) * wp_ref[...]
        h1_ref[...] = h1
        n2_ref[...] = (h1 * _rms(h1) * wf_ref[...]).astype(BF16)

    row = pl.BlockSpec((T, D), lambda i: (i, 0))
    vec = pl.BlockSpec((1, D), lambda i: (0, 0))
    return pl.pallas_call(
        body, name="resid_fwd", grid=(LP // T,),
        in_specs=[row, row, vec, vec], out_specs=[row, row],
        out_shape=[jax.ShapeDtypeStruct((LP, D), F32), jax.ShapeDtypeStruct((LP, D), BF16)],
        compiler_params=_cp(("parallel",)),
    )(h0, mix, w_post, w_pre)


def _resid_bwd(h1, mix, w_post, w_pre, dout, dn2):
    LP, D = h1.shape
    T = _tile(LP, 512)

    def body(h1_ref, mix_ref, wp_ref, wf_ref, dout_ref, dn2_ref, dh1_ref, dmix_ref, dwf_ref, dwp_ref):
        @pl.when(pl.program_id(0) == 0)
        def _():
            dwf_ref[...] = jnp.zeros_like(dwf_ref)
            dwp_ref[...] = jnp.zeros_like(dwp_ref)
        dx, dwn = _rms_bwd(h1_ref[...], wf_ref[...], dn2_ref[...])
        dh1 = dout_ref[...] + dx
        dh1_ref[...] = dh1
        dwf_ref[...] += _colsum(dwn)
        dmix, dwn2 = _rms_bwd(mix_ref[...], wp_ref[...], dh1)
        dmix_ref[...] = dmix.astype(BF16)
        dwp_ref[...] += _colsum(dwn2)

    row = pl.BlockSpec((T, D), lambda i: (i, 0))
    vec = pl.BlockSpec((1, D), lambda i: (0, 0))
    v = jax.ShapeDtypeStruct((1, D), F32)
    return pl.pallas_call(
        body, name="resid_bwd", grid=(LP // T,),
        in_specs=[row, row, vec, vec, row, row], out_specs=[row, row, vec, vec],
        out_shape=[jax.ShapeDtypeStruct((LP, D), F32), jax.ShapeDtypeStruct((LP, D), BF16), v, v],
        compiler_params=_cp(("arbitrary",)),
    )(h1, mix, w_post, w_pre, dout, dn2)


GELU_C = 0.7978845608028654
GELU_A = 0.044715


def _gelu_parts(x):
    t = jnp.tanh(GELU_C * (x + GELU_A * x * x * x))
    return 0.5 * x * (1.0 + t), t


def _convglu_fwd(up, conv_w, conv_b):
    LP, C = up.shape
    T = _tile(LP, 128)

    def body(x_ref, halo_ref, cw_ref, cb_ref, act_ref, xs):
        i = pl.program_id(0)
        xs[0:8, :] = jnp.where(i > 0, halo_ref[...], 0.0)
        xs[8:, :] = x_ref[...]
        y = jnp.broadcast_to(cb_ref[...], (T, C))
        for j in range(FFN_CONV):
            y = y + cw_ref[j:j + 1, :] * xs[pl.ds(8 - (FFN_CONV - 1) + j, T), :]
        g, _ = _gelu_parts(y[:, :D_FF])
        act_ref[...] = (g * y[:, D_FF:]).astype(BF16)

    t8 = T // 8
    return pl.pallas_call(
        body, name="convglu_fwd", grid=(LP // T,),
        in_specs=[pl.BlockSpec((T, C), lambda i: (i, 0)),
                  pl.BlockSpec((8, C), lambda i: (jnp.maximum(i * t8 - 1, 0), 0)),
                  pl.BlockSpec((FFN_CONV, C), lambda i: (0, 0)), pl.BlockSpec((1, C), lambda i: (0, 0))],
        out_specs=pl.BlockSpec((T, D_FF), lambda i: (i, 0)),
        out_shape=jax.ShapeDtypeStruct((LP, D_FF), BF16),
        scratch_shapes=[pltpu.VMEM((T + 8, C), F32)],
        compiler_params=_cp(("parallel",)),
    )(up, up, conv_w, conv_b)


def _convglu_bwd(up, conv_w, conv_b, dact):
    LP, C = up.shape
    T = _tile(LP, 128)
    TE = T + 8
    nt = LP // T

    def body(x_ref, xp_ref, xn_ref, cw_ref, cb_ref, da_ref, dan_ref, dx_ref, dcw_ref, dcb_ref, xs, dys):
        i = pl.program_id(0)

        @pl.when(i == 0)
        def _():
            dcw_ref[...] = jnp.zeros_like(dcw_ref)
            dcb_ref[...] = jnp.zeros_like(dcb_ref)

        last = i == nt - 1
        xs[0:8, :] = jnp.where(i > 0, xp_ref[...], 0.0)
        xs[8:8 + T, :] = x_ref[...]
        xs[8 + T:, :] = jnp.where(last, 0.0, xn_ref[...])
        y = jnp.broadcast_to(cb_ref[...], (TE, C))
        for j in range(FFN_CONV):
            y = y + cw_ref[j:j + 1, :] * xs[pl.ds(8 - (FFN_CONV - 1) + j, TE), :]
        gate, val = y[:, :D_FF], y[:, D_FF:]
        g, t = _gelu_parts(gate)
        dg_dx = 0.5 * (1.0 + t) + 0.5 * gate * (1.0 - t * t) * GELU_C * (1.0 + 3.0 * GELU_A * gate * gate)
        da = jnp.concatenate([da_ref[...], jnp.where(last, 0.0, dan_ref[...])], axis=0)
        dys[:, :D_FF] = da * val * dg_dx
        dys[:, D_FF:] = da * g
        dy_t = dys[0:T, :]
        dcb_ref[...] += _colsum(dy_t)
        for j in range(FFN_CONV):
            dcw_ref[j:j + 1, :] += _colsum(dy_t * xs[pl.ds(8 - (FFN_CONV - 1) + j, T), :])
        dx = jnp.zeros((T, C), F32)
        for j in range(FFN_CONV):
            dx = dx + cw_ref[j:j + 1, :] * dys[pl.ds(FFN_CONV - 1 - j, T), :]
        dx_ref[...] = dx.astype(BF16)

    t8 = T // 8
    nb8 = LP // 8
    prev8 = lambda w: pl.BlockSpec((8, w), lambda i: (jnp.maximum(i * t8 - 1, 0), 0))
    next8 = lambda w: pl.BlockSpec((8, w), lambda i: (jnp.minimum((i + 1) * t8, nb8 - 1), 0))
    row = lambda w: pl.BlockSpec((T, w), lambda i: (i, 0))
    small = lambda r: pl.BlockSpec((r, C), lambda i: (0, 0))
    return pl.pallas_call(
        body, name="convglu_bwd", grid=(nt,),
        in_specs=[row(C), prev8(C), next8(C), small(FFN_CONV), small(1), row(D_FF), next8(D_FF)],
        out_specs=[row(C), small(FFN_CONV), small(1)],
        out_shape=[jax.ShapeDtypeStruct((LP, C), BF16), jax.ShapeDtypeStruct((FFN_CONV, C), F32),
                   jax.ShapeDtypeStruct((1, C), F32)],
        scratch_shapes=[pltpu.VMEM((T + 16, C), F32), pltpu.VMEM((TE, C), F32)],
        compiler_params=_cp(("arbitrary",)),
    )(up, up, up, conv_w, conv_b, dact, dact)


def _final(h1, f, w_post, target, n_real):
    LP, D = h1.shape
    T = _tile(LP, 256)

    def body(h1_ref, f_ref, w_ref, t_ref, loss_ref, dout_ref, df_ref, dw_ref):
        @pl.when(pl.program_id(0) == 0)
        def _():
            loss_ref[...] = jnp.zeros_like(loss_ref)
            dw_ref[...] = jnp.zeros_like(dw_ref)
        rows = pl.program_id(0) * T + _iota2((T, 1), 0)
        real = (rows >= ROW0) & (rows < ROW0 + n_real)
        f = f_ref[...]
        out = h1_ref[...] + f * _rms(f) * w_ref[...]
        err = jnp.where(real, out - t_ref[...], 0.0)
        loss_ref[...] += 0.5 * jnp.sum(_colsum(jnp.mean(err * err, axis=-1, keepdims=True)), axis=-1, keepdims=True)
        dout = err * (1.0 / D)
        dout_ref[...] = dout
        dx, dwn = _rms_bwd(f, w_ref[...], dout)
        df_ref[...] = dx.astype(BF16)
        dw_ref[...] += _colsum(dwn)

    row = pl.BlockSpec((T, D), lambda i: (i, 0))
    vec = pl.BlockSpec((1, D), lambda i: (0, 0))
    return pl.pallas_call(
        body, name="final_loss", grid=(LP // T,),
        in_specs=[row, row, vec, row],
        out_specs=[pl.BlockSpec((1, 128), lambda i: (0, 0)), row, row, vec],
        out_shape=[jax.ShapeDtypeStruct((1, 128), F32), jax.ShapeDtypeStruct((LP, D), F32),
                   jax.ShapeDtypeStruct((LP, D), BF16), jax.ShapeDtypeStruct((1, D), F32)],
        compiler_params=_cp(("arbitrary",)),
    )(h1, f, w_post, target)


def _exchange(arrs, gather, name):
    n = len(arrs)
    npeer = N_DEV - 1

    def body(*refs):
        ins, outs = refs[:n], refs[n:2 * n]
        send_sems, recv_sems, loc_sems = refs[2 * n:]
        x, y, c = lax.axis_index("x"), lax.axis_index("y"), lax.axis_index("c")
        me = 4 * x + 2 * y + c
        copies = []
        for a in range(n):
            for kk in range(1, N_DEV):
                px = 1 - x if kk & 4 else x
                py = 1 - y if kk & 2 else y
                pc = 1 - c if kk & 1 else c
                src = ins[a] if gather else ins[a].at[4 * px + 2 * py + pc]
                s = a * npeer + kk - 1
                cp = pltpu.make_async_remote_copy(src_ref=src, dst_ref=outs[a].at[me], send_sem=send_sems.at[s],
                                                  recv_sem=recv_sems.at[s], device_id=(px, py, pc), device_id_type=MESH)
                cp.start()
                copies.append(cp)
            own = pltpu.make_async_copy(ins[a] if gather else ins[a].at[me], outs[a].at[me], loc_sems.at[a])
            own.start()
            copies.append(own)
        for cp in copies:
            cp.wait()

    any_spec = pl.BlockSpec(memory_space=pl.ANY)
    shapes = [jax.ShapeDtypeStruct((N_DEV,) + tuple(a.shape) if gather else tuple(a.shape), a.dtype) for a in arrs]
    return pl.pallas_call(
        body, name=name, in_specs=[any_spec] * n, out_specs=[any_spec] * n, out_shape=shapes,
        scratch_shapes=[pltpu.SemaphoreType.DMA((n * npeer,)), pltpu.SemaphoreType.DMA((n * npeer,)),
                        pltpu.SemaphoreType.DMA((n,))],
        compiler_params=pltpu.CompilerParams(has_side_effects=True),
    )(*arrs)


def _adamw(parts, w, m, v, name):
    R, C = w.shape
    cap = max(8, ((2 * 1024 * 1024) // (4 * C * 12)) // 8 * 8)
    T = R if R <= cap else _tile(R, cap, 8)

    def body(p_ref, w_ref, m_ref, v_ref, g_ref, d_ref, nm_ref, nv_ref):
        g = p_ref[0]
        for k in range(1, N_DEV):
            g = g + p_ref[k]
        mm = ADAM_B1 * m_ref[...] + (1.0 - ADAM_B1) * g
        vv = ADAM_B2 * v_ref[...] + (1.0 - ADAM_B2) * (g * g)
        m_hat = mm / (1.0 - ADAM_B1 ** ADAM_STEP)
        v_hat = vv / (1.0 - ADAM_B2 ** ADAM_STEP)
        g_ref[...] = g
        d_ref[...] = -ADAM_LR * (m_hat / (jnp.sqrt(v_hat) + ADAM_EPS) + ADAM_WD * w_ref[...])
        nm_ref[...] = mm
        nv_ref[...] = vv

    row = pl.BlockSpec((T, C), lambda i: (i, 0))
    out = jax.ShapeDtypeStruct((R, C), F32)
    return pl.pallas_call(
        body, name=name, grid=(R // T,),
        in_specs=[pl.BlockSpec((N_DEV, T, C), lambda i: (0, i, 0)), row, row, row],
        out_specs=[row] * 4, out_shape=[out] * 4,
        compiler_params=_cp(("parallel",)),
    )(parts, w, m, v)


SMALL = ("attn_pre_norm", "gdn_A_log", "gdn_dt_bias", "gdn_norm_w", "sb_norm_w", "attn_post_norm",
         "ffn_pre_norm", "ffn_conv_b", "ffn_post_norm")


def _pack_small(arrs):
    rows = []
    for a in arrs:
        flat = a.reshape(-1).astype(F32)
        n = -(-flat.shape[0] // 128) * 128
        rows.append(jnp.pad(flat, (0, n - flat.shape[0])).reshape(-1, 128))
    slab = jnp.concatenate(rows, axis=0)
    pad = (-slab.shape[0]) % 8
    return jnp.pad(slab, ((0, pad), (0, 0)))


def _unpack_small(slab, shapes):
    out, r = [], 0
    for shp in shapes:
        size = 1
        for s in shp:
            size *= s
        nr = -(-size // 128)
        out.append(slab[r:r + nr].reshape(-1)[:size].reshape(shp))
        r += nr
    return out


def _to_blocks_cols(a):
    R, C = a.shape
    return a.reshape(R, N_DEV, C // N_DEV).transpose(1, 0, 2)


def _from_blocks_cols(a):
    n, R, c = a.shape
    return a.transpose(1, 0, 2).reshape(R, n * c)


def kernel(x, meta_tokens, attn_pre_norm, w_in, gdn_conv_w, gdn_A_log, gdn_dt_bias, gdn_norm_w, sb_norm_w, w_out, attn_post_norm, ffn_pre_norm, w_ffn_up, ffn_conv_w, ffn_conv_b, w_ffn_down, ffn_post_norm, loss_target, m_meta_tokens, m_attn_pre_norm, m_w_in, m_gdn_conv_w, m_gdn_A_log, m_gdn_dt_bias, m_gdn_norm_w, m_sb_norm_w, m_w_out, m_attn_post_norm, m_ffn_pre_norm, m_w_ffn_up, m_ffn_conv_w, m_ffn_conv_b, m_w_ffn_down, m_ffn_post_norm, v_meta_tokens, v_attn_pre_norm, v_w_in, v_gdn_conv_w, v_gdn_A_log, v_gdn_dt_bias, v_gdn_norm_w, v_sb_norm_w, v_w_out, v_attn_post_norm, v_ffn_pre_norm, v_w_ffn_up, v_ffn_conv_w, v_ffn_conv_b, v_w_ffn_down, v_ffn_post_norm):
    args = dict(locals())
    seq = x.shape[1]
    LP = -(-(ROW0 + seq) // LP_ALIGN) * LP_ALIGN
    tail = LP - ROW0 - seq

    gathered = _exchange(
        [w_in[0].astype(BF16), w_out[0].astype(BF16), w_ffn_up[0].astype(BF16), w_ffn_down[0].astype(BF16),
         gdn_conv_w[0], ffn_conv_w[0], meta_tokens], gather=True, name="gather_weights")
    win_o = _from_blocks_cols(gathered[0])
    o_ab = C_QKV
    o_z = o_ab + 2 * GDN_HEADS
    w_inp = jnp.concatenate([win_o[:, :C_QKV], win_o[:, o_z:o_z + C_Z], win_o[:, o_z + C_Z:],
                             win_o[:, o_ab:o_z], jnp.zeros((D_MODEL, C_AB - 2 * GDN_HEADS), BF16)], axis=1)
    w_out_f = gathered[1].reshape(D_MODEL, D_MODEL)
    w_up_f = _from_blocks_cols(gathered[2])
    w_down_f = gathered[3].reshape(D_FF, D_MODEL)
    gconv_f = _from_blocks_cols(gathered[4])
    fconv_f = _from_blocks_cols(gathered[5])
    meta_f = _from_blocks_cols(gathered[6])

    h0 = jnp.concatenate([jnp.zeros((PAD_ROWS, D_MODEL), F32), meta_f, x[0], jnp.zeros((tail, D_MODEL), F32)], axis=0)
    target = jnp.concatenate([jnp.zeros((ROW0, D_MODEL), F32), loss_target[0], jnp.zeros((tail, D_MODEL), F32)], axis=0)
    u = _prenorm_fwd(h0, attn_pre_norm)
    proj = _mm(u, w_inp, F32, "mm_in")
    qn, kn, vg, beta_b, g_b = _gdn_pre_fwd(proj, gconv_f, gdn_A_log, gdn_dt_bias)
    cu, cw, cqd, ckd, cqk, ct, cgl = _gdn_chunk_fwd(qn, kn, vg, beta_b, g_b)
    og, ssave = _gdn_scan_fwd(cu, cw, cqd, ckd, cqk, cgl)
    osb, ctot = _sb_fwd(proj)
    snw = sb_norm_w.reshape(1, SB_HEADS * SB_DH)
    y = _attn_norm_fwd(og, proj, osb, gdn_norm_w, snw)
    mix = _mm(y, w_out_f, F32, "mm_out")
    h1, n2 = _resid_fwd(h0, mix, attn_post_norm, ffn_pre_norm)
    up = _mm(n2, w_up_f, F32, "mm_up")
    act = _convglu_fwd(up, fconv_f, ffn_conv_b)
    f = _mm(act, w_down_f, F32, "mm_down")
    loss_part, dout, df, d_fpost = _final(h1, f, ffn_post_norm, target, seq)
    loss = lax.psum(loss_part[0, 0], ("x", "y", "c"))

    d_wdown = _mm_tn(act, df, "mm_dw_down")
    dact = _mm(df, w_down_f.T, F32, "mm_dact")
    dup, d_fconv, d_fconvb = _convglu_bwd(up, fconv_f, ffn_conv_b, dact)
    d_wup = _mm_tn(n2, dup, "mm_dw_up")
    dn2 = _mm(dup, w_up_f.T, F32, "mm_dn2")
    dh1, dmix, d_fpre, d_apost = _resid_bwd(h1, mix, attn_post_norm, ffn_pre_norm, dout, dn2)
    d_wout = _mm_tn(y, dmix, "mm_dw_out")
    dy = _mm(dmix, w_out_f.T, F32, "mm_dy")
    dog, dz, dos, d_gnw, d_snw = _attn_norm_bwd(og, proj, osb, gdn_norm_w, snw, dy)
    dqs, dks, dvs = _sb_bwd(proj, ctot, dos)
    du_, dw_, dqd_, dkd_, dqk_, dgl_ = _gdn_scan_bwd(cu, cw, cqd, ckd, cqk, cgl, ssave, dog)
    dqn, dkn, dvg, dbeta, dg = _gdn_chunk_bwd(qn, kn, vg, beta_b, g_b, ct, du_, dw_, dqd_, dkd_, dqk_, dgl_)
    dqkv, dab, d_gconv, d_gsc = _gdn_pre_bwd(proj, gconv_f, gdn_A_log, gdn_dt_bias, dqn, dkn, dvg, dbeta, dg)
    dproj = jnp.concatenate([dqkv.astype(BF16), dz.astype(BF16), dqs.astype(BF16), dks.astype(BF16),
                             dvs.astype(BF16), dab.astype(BF16)], axis=1)
    d_winp = _mm_tn(u, dproj, "mm_dw_in")
    du0 = _mm(dproj, w_inp.T, F32, "mm_du")
    dh0, d_apre = _prenorm_bwd(h0, attn_pre_norm, du0, dh1)
    grad_x = dh0[ROW0:ROW0 + seq][None]
    d_meta = dh0[PAD_ROWS:ROW0]

    d_win = jnp.concatenate([d_winp[:, :C_QKV], d_winp[:, OFF_AB:OFF_AB + 2 * GDN_HEADS],
                             d_winp[:, OFF_Z:OFF_Z + C_Z], d_winp[:, OFF_SB:OFF_SB + C_SB]], axis=1)
    small_grads = [d_apre, d_gsc[0:1, :GDN_HEADS], d_gsc[1:2, :GDN_HEADS], d_gnw, d_snw.reshape(1, SB_HEADS, SB_DH),
                   d_apost, d_fpre, d_fconvb, d_fpost]
    big_names = ("w_in", "w_out", "w_ffn_up", "w_ffn_down", "gdn_conv_w", "ffn_conv_w", "meta_tokens")
    sends = [_to_blocks_cols(d_win), d_wout.reshape(N_DEV, D_MODEL // N_DEV, D_MODEL), _to_blocks_cols(d_wup),
             d_wdown.reshape(N_DEV, D_FF // N_DEV, D_MODEL), _to_blocks_cols(d_gconv), _to_blocks_cols(d_fconv),
             _to_blocks_cols(d_meta)]
    recv = _exchange(sends, gather=False, name="scatter_grads")
    slab_parts = _exchange([_pack_small(small_grads)], gather=True, name="gather_small_grads")[0]

    res = {}
    for nm, parts in zip(big_names, recv):
        wloc = args[nm]
        shp = wloc.shape
        w2 = wloc.reshape(shp[-2], shp[-1])
        outs = _adamw(parts, w2, args["m_" + nm].reshape(w2.shape), args["v_" + nm].reshape(w2.shape), "adamw_" + nm)
        res[nm] = [o.reshape(shp) for o in outs]
    small_shapes = [args[nm].shape for nm in SMALL]
    outs = _adamw(slab_parts, _pack_small([args[nm] for nm in SMALL]), _pack_small([args["m_" + nm] for nm in SMALL]),
                  _pack_small([args["v_" + nm] for nm in SMALL]), "adamw_small")
    for k in range(4):
        for nm, val in zip(SMALL, _unpack_small(outs[k], small_shapes)):
            res.setdefault(nm, [None] * 4)[k] = val

    order = ("meta_tokens", "attn_pre_norm", "w_in", "gdn_conv_w", "gdn_A_log", "gdn_dt_bias", "gdn_norm_w",
             "sb_norm_w", "w_out", "attn_post_norm", "ffn_pre_norm", "w_ffn_up", "ffn_conv_w", "ffn_conv_b",
             "w_ffn_down", "ffn_post_norm")
    return (loss, grad_x, *[res[nm][0] for nm in order], *[res[nm][1] for nm in order],
            *[res[nm][2] for nm in order], *[res[nm][3] for nm in order])
```

```python
import functools

import jax
import jax.numpy as jnp
from jax import lax
from jax.experimental import pallas as pl
from jax.experimental.pallas import tpu as pltpu

F32 = jnp.float32
BF16 = jnp.bfloat16

D_MODEL = 1024
N_META = 16
GDN_HEADS = 4
GDN_D = 128
GDN_CHUNK = 64
GDN_CONV = 4
SB_HEADS = 8
SB_DH = 64
SB_BLOCK = 128
D_FF = 2816
FFN_CONV = 3
NORM_EPS = 1e-6
L2_EPS = 1e-6
N_DEV = 8

PAD_ROWS = SB_BLOCK - N_META
ROW0 = SB_BLOCK
SB_SPAN = 512
SB_DEAD = -104.0
SB_SUB = 256
LP_ALIGN = 256

C_QKV = 3 * GDN_HEADS * GDN_D
C_Z = GDN_HEADS * GDN_D
C_SB = 3 * SB_HEADS * SB_DH
C_AB = 256
OFF_Z = C_QKV
OFF_SB = OFF_Z + C_Z
OFF_AB = OFF_SB + C_SB
D_INP = OFF_AB + C_AB
D_IN = C_QKV + 2 * GDN_HEADS + C_Z + C_SB

ADAM_LR = 0.001
ADAM_B1 = 0.9
ADAM_B2 = 0.999
ADAM_EPS = 1e-08
ADAM_WD = 0.01
ADAM_STEP = 10

VMEM_LIMIT = 56 * 1024 * 1024
MESH = pl.DeviceIdType.MESH


def _cp(sem=None):
    kw = dict(vmem_limit_bytes=VMEM_LIMIT)
    if sem is not None:
        kw["dimension_semantics"] = sem
    return pltpu.CompilerParams(**kw)


def _tile(n, cap, unit=128):
    best = None
    t = unit
    while t <= min(n, cap):
        if n % t == 0:
            best = t
        t += unit
    assert best is not None, (n, cap, unit)
    return best


def _dot(a, b):
    return jnp.dot(a, b, preferred_element_type=F32)


def _dot_nt(a, b):
    return lax.dot_general(a, b, (((1,), (1,)), ((), ())), preferred_element_type=F32)


def _dot_tn(a, b):
    return lax.dot_general(a, b, (((0,), (0,)), ((), ())), preferred_element_type=F32)


def _split(x):
    hi = x.astype(BF16)
    lo = (x - hi.astype(F32)).astype(BF16)
    return hi, lo


def _dot3(a, b, f=_dot):
    ah, al = _split(a)
    bh, bl = _split(b)
    return f(ah, bh) + (f(ah, bl) + f(al, bh))


def _dot_exact_l(m_bf16, x, f=_dot):
    xh, xl = _split(x)
    return f(m_bf16, xh) + f(m_bf16, xl)


def _dot_exact_r(x, m_bf16, f=_dot):
    xh, xl = _split(x)
    return f(xh, m_bf16) + f(xl, m_bf16)


def _iota2(shape, dim):
    return lax.broadcasted_iota(jnp.int32, shape, dim)


def _sigmoid(x):
    return 1.0 / (1.0 + jnp.exp(-x))


def _softplus(x):
    return jnp.maximum(x, 0.0) + jnp.log(1.0 + jnp.exp(-jnp.abs(x)))


def _colsum(x):
    return jnp.sum(x, axis=0, keepdims=True)


def _rowsum(x):
    return jnp.sum(x, axis=-1, keepdims=True)


def _mm(a, b, out_dtype, name):
    M, K = a.shape
    K2, N = b.shape
    assert K == K2
    tm = _tile(M, 768)
    tn = _tile(N, max(128, (6 * 1024 * 1024) // (2 * K)))

    def body(a_ref, b_ref, o_ref):
        o_ref[...] = _dot(a_ref[...].astype(BF16), b_ref[...].astype(BF16)).astype(o_ref.dtype)

    return pl.pallas_call(
        body, name=name, grid=(N // tn, M // tm),
        in_specs=[pl.BlockSpec((tm, K), lambda j, i: (i, 0)), pl.BlockSpec((K, tn), lambda j, i: (0, j))],
        out_specs=pl.BlockSpec((tm, tn), lambda j, i: (i, j)),
        out_shape=jax.ShapeDtypeStruct((M, N), out_dtype),
        compiler_params=_cp(("parallel", "parallel")),
    )(a, b)


def _mm_tn(a, b, name):
    M, K = a.shape
    M2, N = b.shape
    assert M == M2
    tm = _tile(M, 768)
    tk = _tile(K, 1024)
    tn = _tile(N, 1408)

    def body(a_ref, b_ref, o_ref):
        @pl.when(pl.program_id(2) == 0)
        def _():
            o_ref[...] = jnp.zeros_like(o_ref)
        o_ref[...] += _dot_tn(a_ref[...].astype(BF16), b_ref[...].astype(BF16))

    return pl.pallas_call(
        body, name=name, grid=(K // tk, N // tn, M // tm),
        in_specs=[pl.BlockSpec((tm, tk), lambda i, j, m: (m, i)), pl.BlockSpec((tm, tn), lambda i, j, m: (m, j))],
        out_specs=pl.BlockSpec((tk, tn), lambda i, j, m: (i, j)),
        out_shape=jax.ShapeDtypeStruct((K, N), F32),
        compiler_params=_cp(("parallel", "parallel", "arbitrary")),
    )(a, b)


def _rms(x):
    return lax.rsqrt(jnp.mean(x * x, axis=-1, keepdims=True) + NORM_EPS)


def _rms_bwd(x, w, dy):
    r = _rms(x)
    n = x * r
    dyw = dy * w
    dx = r * (dyw - n * jnp.mean(dyw * n, axis=-1, keepdims=True))
    return dx, dy * n


def _prenorm_fwd(h0, w):
    LP, D = h0.shape
    T = _tile(LP, 512)

    def body(h_ref, w_ref, u_ref):
        h = h_ref[...]
        u_ref[...] = (h * _rms(h) * w_ref[...]).astype(BF16)

    return pl.pallas_call(
        body, name="prenorm_fwd", grid=(LP // T,),
        in_specs=[pl.BlockSpec((T, D), lambda i: (i, 0)), pl.BlockSpec((1, D), lambda i: (0, 0))],
        out_specs=pl.BlockSpec((T, D), lambda i: (i, 0)),
        out_shape=jax.ShapeDtypeStruct((LP, D), BF16),
        compiler_params=_cp(("parallel",)),
    )(h0, w)


def _prenorm_bwd(h0, w, du, dh1):
    LP, D = h0.shape
    T = _tile(LP, 512)

    def body(h_ref, w_ref, du_ref, dh1_ref, dh0_ref, dw_ref):
        @pl.when(pl.program_id(0) == 0)
        def _():
            dw_ref[...] = jnp.zeros_like(dw_ref)
        dx, dwn = _rms_bwd(h_ref[...], w_ref[...], du_ref[...])
        dh0_ref[...] = dh1_ref[...] + dx
        dw_ref[...] += _colsum(dwn)

    row = pl.BlockSpec((T, D), lambda i: (i, 0))
    vec = pl.BlockSpec((1, D), lambda i: (0, 0))
    return pl.pallas_call(
        body, name="prenorm_bwd", grid=(LP // T,),
        in_specs=[row, vec, row, row], out_specs=[row, vec],
        out_shape=[jax.ShapeDtypeStruct((LP, D), F32), jax.ShapeDtypeStruct((1, D), F32)],
        compiler_params=_cp(("arbitrary",)),
    )(h0, w, du, dh1)


def _gdn_gate_consts(alog_ref, dtb_ref, h):
    a_coef = -jnp.exp(alog_ref[0:1, h:h + 1])
    return a_coef, dtb_ref[0:1, h:h + 1]


def _gdn_pre_fwd(proj, conv_w, a_log, dt_bias):
    LP = proj.shape[0]
    T = _tile(LP, 256)
    C = C_QKV
    H = GDN_HEADS

    def body(x_ref, halo_ref, ab_ref, cw_ref, alog_ref, dtb_ref, q_ref, k_ref, v_ref, beta_ref, g_ref, xs):
        i = pl.program_id(0)
        xs[0:8, :] = jnp.where(i > 0, halo_ref[...], 0.0)
        xs[8:, :] = x_ref[...]
        y = jnp.zeros((T, C), F32)
        for j in range(GDN_CONV):
            y = y + cw_ref[j:j + 1, :] * xs[pl.ds(8 - (GDN_CONV - 1) + j, T), :]
        c = y * _sigmoid(y)
        for h in range(H):
            sl = slice(h * GDN_D, (h + 1) * GDN_D)
            cq = c[:, sl]
            q_ref[:, sl] = cq * lax.rsqrt(_rowsum(cq * cq) + L2_EPS) * (GDN_D ** -0.5)
            ck = c[:, 512 + h * GDN_D:512 + (h + 1) * GDN_D]
            k_ref[:, sl] = ck * lax.rsqrt(_rowsum(ck * ck) + L2_EPS)
        v_ref[...] = c[:, 1024:]
        ab = ab_ref[...]
        valid = (i * T + _iota2((T, 1), 0)) >= PAD_ROWS
        for h in range(H):
            sl = slice(h * GDN_D, (h + 1) * GDN_D)
            a_coef, dtb = _gdn_gate_consts(alog_ref, dtb_ref, h)
            g = jnp.where(valid, a_coef * _softplus(ab[:, h:h + 1] + dtb), 0.0)
            beta = jnp.where(valid, _sigmoid(ab[:, H + h:H + h + 1]), 0.0)
            g_ref[:, sl] = jnp.broadcast_to(g, (T, GDN_D))
            beta_ref[:, sl] = jnp.broadcast_to(beta, (T, GDN_D))

    t8 = T // 8
    row512 = pl.BlockSpec((T, 512), lambda i: (i, 0))
    small = lambda r, c: pl.BlockSpec((r, c), lambda i: (0, 0))
    out = jax.ShapeDtypeStruct((LP, 512), F32)
    return pl.pallas_call(
        body, name="gdn_pre_fwd", grid=(LP // T,),
        in_specs=[pl.BlockSpec((T, C), lambda i: (i, 0)),
                  pl.BlockSpec((8, C), lambda i: (jnp.maximum(i * t8 - 1, 0), 0)),
                  pl.BlockSpec((T, C_AB), lambda i: (i, OFF_AB // C_AB)),
                  small(GDN_CONV, C), small(1, H), small(1, H)],
        out_specs=[row512] * 5, out_shape=[out] * 5,
        scratch_shapes=[pltpu.VMEM((T + 8, C), F32)],
        compiler_params=_cp(("parallel",)),
    )(proj, proj, proj, conv_w, a_log, dt_bias)


def _gdn_pre_bwd(proj, conv_w, a_log, dt_bias, dq, dk, dv, dbeta, dg):
    LP = proj.shape[0]
    T = _tile(LP, 256)
    C = C_QKV
    H = GDN_HEADS
    TE = T + 8
    nt = LP // T

    def body(x_ref, xp_ref, xn_ref, ab_ref, cw_ref, alog_ref, dtb_ref,
             dq_ref, dqn_ref, dk_ref, dkn_ref, dv_ref, dvn_ref, dbeta_ref, dg_ref,
             dx_ref, dab_ref, dcw_ref, dsc_ref, xs, dys):
        i = pl.program_id(0)

        @pl.when(i == 0)
        def _():
            dcw_ref[...] = jnp.zeros_like(dcw_ref)
            dsc_ref[...] = jnp.zeros_like(dsc_ref)

        last = i == nt - 1
        xs[0:8, :] = jnp.where(i > 0, xp_ref[...], 0.0)
        xs[8:8 + T, :] = x_ref[...]
        xs[8 + T:, :] = jnp.where(last, 0.0, xn_ref[...])
        y = jnp.zeros((TE, C), F32)
        for j in range(GDN_CONV):
            y = y + cw_ref[j:j + 1, :] * xs[pl.ds(8 - (GDN_CONV - 1) + j, TE), :]
        sg = _sigmoid(y)
        c = y * sg
        nxt = lambda a_ref, b_ref: jnp.concatenate([a_ref[...], jnp.where(last, 0.0, b_ref[...])], axis=0)
        dqn = nxt(dq_ref, dqn_ref)
        dkn = nxt(dk_ref, dkn_ref)
        dvv = nxt(dv_ref, dvn_ref)
        for h in range(H):
            sl = slice(h * GDN_D, (h + 1) * GDN_D)
            cq = c[:, sl]
            rq = lax.rsqrt(_rowsum(cq * cq) + L2_EPS)
            nq = cq * rq
            dqh = dqn[:, sl]
            dys[:, sl] = (GDN_D ** -0.5) * rq * (dqh - nq * _rowsum(dqh * nq))
            sk = slice(512 + h * GDN_D, 512 + (h + 1) * GDN_D)
            ck = c[:, sk]
            rk = lax.rsqrt(_rowsum(ck * ck) + L2_EPS)
            nk = ck * rk
            dkh = dkn[:, sl]
            dys[:, sk] = rk * (dkh - nk * _rowsum(dkh * nk))
        dys[:, 1024:] = dvv
        dy = dys[...] * (sg * (1.0 + y * (1.0 - sg)))
        dys[...] = dy
        for j in range(GDN_CONV):
            dcw_ref[j:j + 1, :] += _colsum(dy[0:T, :] * xs[pl.ds(8 - (GDN_CONV - 1) + j, T), :])
        dx = jnp.zeros((T, C), F32)
        for j in range(GDN_CONV):
            dx = dx + cw_ref[j:j + 1, :] * dys[pl.ds(GDN_CONV - 1 - j, T), :]
        dx_ref[...] = dx
        ab = ab_ref[...]
        valid = (i * T + _iota2((T, 1), 0)) >= PAD_ROWS
        lane = _iota2((T, C_AB), 1)
        lane1 = _iota2((1, 128), 1)
        dab = jnp.zeros((T, C_AB), F32)
        dsc_a = jnp.zeros((1, 128), F32)
        dsc_d = jnp.zeros((1, 128), F32)
        for h in range(H):
            a_coef, dtb = _gdn_gate_consts(alog_ref, dtb_ref, h)
            pre = ab[:, h:h + 1] + dtb
            dgh = jnp.where(valid, dg_ref[:, h * GDN_D:h * GDN_D + 1], 0.0)
            da = dgh * a_coef * _sigmoid(pre)
            beta = _sigmoid(ab[:, H + h:H + h + 1])
            db = jnp.where(valid, dbeta_ref[:, h * GDN_D:h * GDN_D + 1], 0.0) * beta * (1.0 - beta)
            dab = dab + jnp.where(lane == h, da, 0.0) + jnp.where(lane == H + h, db, 0.0)
            dsc_a = dsc_a + jnp.where(lane1 == h, _colsum(dgh * a_coef * _softplus(pre)), 0.0)
            dsc_d = dsc_d + jnp.where(lane1 == h, _colsum(da), 0.0)
        dab_ref[...] = dab
        dsc_ref[0:1, :] += dsc_a
        dsc_ref[1:2, :] += dsc_d

    t8 = T // 8
    nb8 = LP // 8
    prev8 = lambda w: pl.BlockSpec((8, w), lambda i: (jnp.maximum(i * t8 - 1, 0), 0))
    next8 = lambda w: pl.BlockSpec((8, w), lambda i: (jnp.minimum((i + 1) * t8, nb8 - 1), 0))
    row = lambda w: pl.BlockSpec((T, w), lambda i: (i, 0))
    small = lambda r, c: pl.BlockSpec((r, c), lambda i: (0, 0))
    return pl.pallas_call(
        body, name="gdn_pre_bwd", grid=(nt,),
        in_specs=[row(C), prev8(C), next8(C), pl.BlockSpec((T, C_AB), lambda i: (i, OFF_AB // C_AB)),
                  small(GDN_CONV, C), small(1, H), small(1, H),
                  row(512), next8(512), row(512), next8(512), row(512), next8(512), row(512), row(512)],
        out_specs=[row(C), row(C_AB), small(GDN_CONV, C), small(2, 128)],
        out_shape=[jax.ShapeDtypeStruct((LP, C), F32), jax.ShapeDtypeStruct((LP, C_AB), F32),
                   jax.ShapeDtypeStruct((GDN_CONV, C), F32), jax.ShapeDtypeStruct((2, 128), F32)],
        scratch_shapes=[pltpu.VMEM((T + 16, C), F32), pltpu.VMEM((TE, C), F32)],
        compiler_params=_cp(("arbitrary",)),
    )(proj, proj, proj, proj, conv_w, a_log, dt_bias, dq, dq, dk, dk, dv, dv, dbeta, dg)


def _tri_masks():
    r = _iota2((GDN_CHUNK, GDN_CHUNK), 0)
    c = _iota2((GDN_CHUNK, GDN_CHUNK), 1)
    return r >= c, r > c


def _gdn_chunk_common(q, k, v, beta, gb):
    incl, strict = _tri_masks()
    l_incl = incl.astype(BF16)
    gd = _dot_exact_l(l_incl, jnp.where(strict, gb[:, :GDN_CHUNK], 0.0))
    gc = _dot_exact_l(l_incl, gb)
    decay = jnp.where(incl, jnp.exp(jnp.where(incl, gd, 0.0)), 0.0)
    exp_g = jnp.exp(gc)
    g_last = gc[GDN_CHUNK - 1:GDN_CHUNK, :]
    kd_fac = jnp.exp(g_last - gc)
    gl = jnp.exp(g_last)
    kb = k * beta
    kk = _dot3(kb, k, _dot_nt)
    return dict(incl=incl, strict=strict, decay=decay, exp_g=exp_g, kd_fac=kd_fac, gl=gl, kb=kb, kk=kk,
                vb=v * beta, kbg=kb * exp_g)


def _gdn_chunk_fwd(qn, kn, v, beta_b, g_b):
    LP = qn.shape[0]
    R = 128
    H = GDN_HEADS
    CH = GDN_CHUNK

    def body(q_ref, k_ref, v_ref, b_ref, g_ref, u_ref, w_ref, qd_ref, kd_ref, qk_ref, t_ref, gl_ref):
        for cc in range(R // CH):
            rs = slice(cc * CH, (cc + 1) * CH)
            for h in range(H):
                sl = slice(h * GDN_D, (h + 1) * GDN_D)
                s64 = slice(h * CH, (h + 1) * CH)
                q, k = q_ref[rs, sl], k_ref[rs, sl]
                m = _gdn_chunk_common(q, k, v_ref[rs, sl], b_ref[rs, sl], g_ref[rs, sl])
                a = jnp.where(m["strict"], m["kk"] * m["decay"], 0.0)
                eye = (_iota2((CH, CH), 0) == _iota2((CH, CH), 1)).astype(F32)
                p = -a
                t = eye + p
                for _ in range(5):
                    p = _dot3(p, p)
                    t = t + _dot3(t, p)
                u_ref[rs, sl] = _dot3(t, m["vb"])
                w_ref[rs, sl] = _dot3(t, m["kbg"])
                qk_ref[rs, s64] = _dot3(q, k, _dot_nt) * m["decay"]
                t_ref[rs, s64] = t
                qd_ref[rs, sl] = q * m["exp_g"]
                kd_ref[rs, sl] = k * m["kd_fac"]
                gl_ref[cc * 8:(cc + 1) * 8, sl] = jnp.broadcast_to(m["gl"], (8, GDN_D))

    row = lambda w: pl.BlockSpec((R, w), lambda i: (i, 0))
    o512 = jax.ShapeDtypeStruct((LP, 512), F32)
    o256 = jax.ShapeDtypeStruct((LP, 256), F32)
    return pl.pallas_call(
        body, name="gdn_chunk_fwd", grid=(LP // R,),
        in_specs=[row(512)] * 5,
        out_specs=[row(512)] * 4 + [row(256)] * 2 + [pl.BlockSpec((R // 8, 512), lambda i: (i, 0))],
        out_shape=[o512] * 4 + [o256] * 2 + [jax.ShapeDtypeStruct((LP // 8, 512), F32)],
        compiler_params=_cp(("parallel",)),
    )(qn, kn, v, beta_b, g_b)


def _gdn_chunk_bwd(qn, kn, v, beta_b, g_b, t_all, du, dw, dqd, dkd, dqk, dgl):
    LP = qn.shape[0]
    R = 128
    H = GDN_HEADS
    CH = GDN_CHUNK

    def body(q_ref, k_ref, v_ref, b_ref, g_ref, t_ref, du_ref, dw_ref, dqd_ref, dkd_ref, dqk_ref, dgl_ref,
             dq_ref, dk_ref, dv_ref, db_ref, dg_ref):
        ones = jnp.ones((CH, GDN_D), BF16)
        for cc in range(R // CH):
            rs = slice(cc * CH, (cc + 1) * CH)
            for h in range(H):
                sl = slice(h * GDN_D, (h + 1) * GDN_D)
                s64 = slice(h * CH, (h + 1) * CH)
                q, k, vv, beta = q_ref[rs, sl], k_ref[rs, sl], v_ref[rs, sl], b_ref[rs, sl]
                m = _gdn_chunk_common(q, k, vv, beta, g_ref[rs, sl])
                incl, strict, decay = m["incl"], m["strict"], m["decay"]
                t = t_ref[rs, s64]
                du_, dw_ = du_ref[rs, sl], dw_ref[rs, sl]
                dqd_, dkd_ = dqd_ref[rs, sl], dkd_ref[rs, sl]
                d_t = _dot3(du_, m["vb"], _dot_nt) + _dot3(dw_, m["kbg"], _dot_nt)
                dvb = _dot3(t, du_, _dot_tn)
                dkbg = _dot3(t, dw_, _dot_tn)
                dkb = dkbg * m["exp_g"]
                d_gi = _rowsum(dkbg * m["kbg"])
                x1 = _dot3(d_t, t, _dot_nt)
                d_a = jnp.where(strict, -_dot3(t, x1, _dot_tn), 0.0)
                d_kk = d_a * decay
                d_dec = d_a * m["kk"]
                dkb = dkb + _dot3(d_kk, k)
                dk_ = _dot3(d_kk, m["kb"], _dot_tn)
                dqk_m = jnp.where(incl, dqk_ref[rs, s64], 0.0)
                qk_raw = _dot3(q, k, _dot_nt)
                dqk_raw = dqk_m * decay
                d_dec = d_dec + dqk_m * qk_raw
                dq_ = _dot3(dqk_raw, k) + dqd_ * m["exp_g"]
                dk_ = dk_ + _dot3(dqk_raw, q, _dot_tn)
                mm = d_dec * decay
                d_gi = d_gi + (_dot_exact_r(mm, ones) - _dot_exact_r(mm, ones, _dot_tn))
                d_gi = d_gi + _rowsum(dqd_ * q * m["exp_g"])
                kd = k * m["kd_fac"]
                e = _rowsum(dkd_ * kd)
                d_gi = d_gi - e
                d_glast = _colsum(jnp.broadcast_to(e, (CH, GDN_D))) + dgl_ref[cc * 8:cc * 8 + 1, sl] * m["gl"]
                dk_ = dk_ + dkd_ * m["kd_fac"] + dkb * beta
                d_gi = d_gi + jnp.where(_iota2((CH, GDN_D), 0) == CH - 1, d_glast, 0.0)
                u_incl = (_iota2((CH, CH), 1) >= _iota2((CH, CH), 0)).astype(BF16)
                dq_ref[rs, sl] = dq_
                dk_ref[rs, sl] = dk_
                dv_ref[rs, sl] = dvb * beta
                db_ref[rs, sl] = jnp.broadcast_to(_rowsum(dvb * vv) + _rowsum(dkb * k), (CH, GDN_D))
                dg_ref[rs, sl] = _dot_exact_l(u_incl, d_gi)

    row = lambda w: pl.BlockSpec((R, w), lambda i: (i, 0))
    o512 = jax.ShapeDtypeStruct((LP, 512), F32)
    gl_spec = pl.BlockSpec((R // 8, 512), lambda i: (i, 0))
    return pl.pallas_call(
        body, name="gdn_chunk_bwd", grid=(LP // R,),
        in_specs=[row(512)] * 5 + [row(256)] + [row(512)] * 4 + [row(256), gl_spec],
        out_specs=[row(512)] * 5, out_shape=[o512] * 5,
        compiler_params=_cp(("parallel",)),
    )(qn, kn, v, beta_b, g_b, t_all, du, dw, dqd, dkd, dqk, dgl)


def _gdn_scan_fwd(u, w, qd, kd, qk, gl):
    LP = u.shape[0]
    CH = GDN_CHUNK
    N = LP // CH
    H = GDN_HEADS

    def body(u_ref, w_ref, qd_ref, kd_ref, qk_ref, gl_ref, o_ref, ssave_ref, s_sc):
        @pl.when(pl.program_id(0) == 0)
        def _():
            s_sc[...] = jnp.zeros_like(s_sc)
        ssave_ref[...] = s_sc[...]
        for h in range(H):
            sl = slice(h * GDN_D, (h + 1) * GDN_D)
            s = s_sc[:, sl]
            v_new = u_ref[:, sl] - _dot3(w_ref[:, sl], s)
            o_ref[:, sl] = _dot3(qd_ref[:, sl], s) + _dot3(qk_ref[:, h * CH:(h + 1) * CH], v_new)
            s_sc[:, sl] = s * gl_ref[0:1, sl] + _dot3(kd_ref[:, sl], v_new, _dot_tn)

    row = lambda w_: pl.BlockSpec((CH, w_), lambda n: (n, 0))
    return pl.pallas_call(
        body, name="gdn_scan_fwd", grid=(N,),
        in_specs=[row(512)] * 4 + [row(256), pl.BlockSpec((8, 512), lambda n: (n, 0))],
        out_specs=[row(512), pl.BlockSpec((GDN_D, 512), lambda n: (n, 0))],
        out_shape=[jax.ShapeDtypeStruct((LP, 512), F32), jax.ShapeDtypeStruct((N * GDN_D, 512), F32)],
        scratch_shapes=[pltpu.VMEM((GDN_D, 512), F32)],
        compiler_params=_cp(("arbitrary",)),
    )(u, w, qd, kd, qk, gl)


def _gdn_scan_bwd(u, w, qd, kd, qk, gl, ssave, do):
    LP = u.shape[0]
    CH = GDN_CHUNK
    N = LP // CH
    H = GDN_HEADS

    def body(u_ref, w_ref, qd_ref, kd_ref, qk_ref, gl_ref, s_ref, do_ref,
             du_ref, dw_ref, dqd_ref, dkd_ref, dqk_ref, dgl_ref, ds_sc):
        @pl.when(pl.program_id(0) == 0)
        def _():
            ds_sc[...] = jnp.zeros_like(ds_sc)
        for h in range(H):
            sl = slice(h * GDN_D, (h + 1) * GDN_D)
            s64 = slice(h * CH, (h + 1) * CH)
            s = s_ref[:, sl]
            ds = ds_sc[:, sl]
            do_ = do_ref[:, sl]
            w_, qd_, kd_, qk_ = w_ref[:, sl], qd_ref[:, sl], kd_ref[:, sl], qk_ref[:, s64]
            v_new = u_ref[:, sl] - _dot3(w_, s)
            d_vnew = _dot3(qk_, do_, _dot_tn) + _dot3(kd_, ds)
            du_ref[:, sl] = d_vnew
            dw_ref[:, sl] = -_dot3(d_vnew, s, _dot_nt)
            dqd_ref[:, sl] = _dot3(do_, s, _dot_nt)
            dkd_ref[:, sl] = _dot3(v_new, ds, _dot_nt)
            dqk_ref[:, s64] = _dot3(do_, v_new, _dot_nt)
            dgl_ref[:, sl] = jnp.broadcast_to(jnp.sum(_colsum(ds * s), axis=-1, keepdims=True), (8, GDN_D))
            ds_sc[:, sl] = ds * gl_ref[0:1, sl] + _dot3(qd_, do_, _dot_tn) - _dot3(w_, d_vnew, _dot_tn)

    rev = lambda w_: pl.BlockSpec((CH, w_), lambda n: (N - 1 - n, 0))
    rev8 = pl.BlockSpec((8, 512), lambda n: (N - 1 - n, 0))
    o512 = jax.ShapeDtypeStruct((LP, 512), F32)
    return pl.pallas_call(
        body, name="gdn_scan_bwd", grid=(N,),
        in_specs=[rev(512)] * 4 + [rev(256), rev8, pl.BlockSpec((GDN_D, 512), lambda n: (N - 1 - n, 0)), rev(512)],
        out_specs=[rev(512)] * 4 + [rev(256), rev8],
        out_shape=[o512] * 4 + [jax.ShapeDtypeStruct((LP, 256), F32), jax.ShapeDtypeStruct((LP // 8, 512), F32)],
        scratch_shapes=[pltpu.VMEM((GDN_D, 512), F32)],
        compiler_params=_cp(("arbitrary",)),
    )(u, w, qd, kd, qk, gl, ssave, do)


def _sb_scores(qh, kblk, mask):
    z = _dot_nt(qh, kblk)
    e = jnp.exp(-jnp.abs(z))
    sp = jnp.maximum(z, 0.0) + jnp.log(1.0 + e)
    return z, e, jnp.where(mask, -sp, 0.0), z - sp


def _sb_fwd(proj):
    LP = proj.shape[0]
    B = SB_BLOCK
    W = min(SB_SPAN, LP)
    SUB = SB_SUB
    nq = LP // B
    nsub = W // SUB
    scale = SB_DH ** -0.5
    qcol, kcol, vcol = OFF_SB // B, (OFF_SB + 512) // B, (OFF_SB + 1024) // B

    def body(q_ref, k_ref, v_ref, o_ref, c_ref, n_ref):
        i = pl.program_id(1)
        lane = _iota2((B, B), 1)
        head_a = lane < SB_DH
        qs = q_ref[...] * scale
        qh = [jnp.where(head_a, qs, 0.0).astype(BF16), jnp.where(head_a, 0.0, qs).astype(BF16)]
        u_strict = (_iota2((SUB, SUB), 0) > _iota2((SUB, SUB), 1)).astype(BF16)
        qpos = i * B + _iota2((B, W), 0)
        hi0 = (i + 1) * B
        nspan = (hi0 + W - 1) // W

        def live(st):
            return (st[0] < nspan) & (st[1] > 0)

        def span(st):
            r, carry = st[0], st[2:]
            hi = hi0 - r * W
            k0 = pl.multiple_of(jnp.maximum(hi - W, 0), B)
            kblk = k_ref[pl.ds(k0, W), :].astype(BF16)
            vblk = v_ref[pl.ds(k0, W), :].astype(BF16)
            kpos = k0 + _iota2((B, W), 1)
            mask = (kpos < qpos) & (kpos >= PAD_ROWS) & (kpos < hi)
            new = []
            for h in range(2):
                o_acc, c = carry[2 * h], carry[2 * h + 1]
                z, e, l1m, lsg = _sb_scores(qh[h], kblk, mask)
                parts = [None] * nsub
                for b in reversed(range(nsub)):
                    bs = slice(b * SUB, (b + 1) * SUB)
                    suf = _dot_exact_r(l1m[:, bs], u_strict)
                    parts[b] = jnp.where(mask[:, bs], jnp.exp(lsg[:, bs] + suf + c), 0.0)
                    c = c + _rowsum(l1m[:, bs])
                att = jnp.concatenate(parts, axis=1).astype(BF16)
                new += [o_acc + _dot(att, vblk), c]
            more = (jnp.maximum(jnp.max(new[1]), jnp.max(new[3])) > SB_DEAD).astype(jnp.int32)
            return (r + 1, more, *new)

        zero_o = jnp.zeros((B, B), F32)
        zero_c = jnp.zeros((B, 1), F32)
        nrun, _, o_a, c_a, o_b, c_b = lax.while_loop(
            live, span, (jnp.int32(0), jnp.int32(1), zero_o, zero_c, zero_o, zero_c))
        o_ref[...] = jnp.where(head_a, o_a, o_b)
        c_ref[...] = jnp.where(head_a, c_a, c_b)
        n_ref[pl.program_id(0), i] = nrun

    blk = pl.BlockSpec((B, B), lambda p, i: (i, p))
    out = jax.ShapeDtypeStruct((LP, 512), F32)
    return pl.pallas_call(
        body, name="sb_fwd", grid=(SB_HEADS // 2, nq),
        in_specs=[pl.BlockSpec((B, B), lambda p, i: (i, qcol + p)),
                  pl.BlockSpec((LP, B), lambda p, i: (0, kcol + p)),
                  pl.BlockSpec((LP, B), lambda p, i: (0, vcol + p))],
        out_specs=[blk, blk, pl.BlockSpec(memory_space=pltpu.SMEM)],
        out_shape=[out, out, jax.ShapeDtypeStruct((SB_HEADS // 2, nq), jnp.int32)],
        compiler_params=_cp(("arbitrary", "arbitrary")),
    )(proj, proj, proj)


def _sb_bwd(proj, ctot, nrun_all, do):
    LP = proj.shape[0]
    B = SB_BLOCK
    W = min(SB_SPAN, LP)
    SUB = SB_SUB
    nq = LP // B
    nsub = W // SUB
    scale = SB_DH ** -0.5
    qcol, kcol, vcol = OFF_SB // B, (OFF_SB + 512) // B, (OFF_SB + 1024) // B

    def body(n_ref, q_ref, k_ref, v_ref, c_ref, do_ref, dq_ref, dk_ref, dv_ref):
        i = pl.program_id(1)

        @pl.when(i == 0)
        def _():
            dk_ref[...] = jnp.zeros_like(dk_ref)
            dv_ref[...] = jnp.zeros_like(dv_ref)

        lane = _iota2((B, B), 1)
        head_a = lane < SB_DH
        qs = q_ref[...] * scale
        qh = [jnp.where(head_a, qs, 0.0).astype(BF16), jnp.where(head_a, 0.0, qs).astype(BF16)]
        dof = do_ref[...]
        doh = [jnp.where(head_a, dof, 0.0).astype(BF16), jnp.where(head_a, 0.0, dof).astype(BF16)]
        cfull = c_ref[...]
        ctot_h = [cfull[:, 0:1], cfull[:, SB_DH:SB_DH + 1]]
        sub_r, sub_c = _iota2((SUB, SUB), 0), _iota2((SUB, SUB), 1)
        u_strict = (sub_r > sub_c).astype(BF16)
        l_strict = (sub_r < sub_c).astype(BF16)
        qpos = i * B + _iota2((B, W), 0)
        hi0 = (i + 1) * B
        nrun = n_ref[pl.program_id(0), i]

        def span(t, carry):
            r = nrun - 1 - t
            hi = hi0 - r * W
            k0 = pl.multiple_of(jnp.maximum(hi - W, 0), B)
            kblk = k_ref[pl.ds(k0, W), :].astype(BF16)
            vblk = v_ref[pl.ds(k0, W), :].astype(BF16)
            kpos = k0 + _iota2((B, W), 1)
            mask = (kpos < qpos) & (kpos >= PAD_ROWS) & (kpos < hi)
            new = []
            dk_add = jnp.zeros((W, B), F32)
            dv_add = jnp.zeros((W, B), F32)
            for h in range(2):
                dq_acc, pre, ecar = carry[3 * h], carry[3 * h + 1], carry[3 * h + 2]
                z, e, l1m, lsg = _sb_scores(qh[h], kblk, mask)
                d_att = _dot_nt(doh[h], vblk)
                sig = jnp.where(z >= 0.0, 1.0, e) / (1.0 + e)
                att_parts, dz_parts = [None] * nsub, [None] * nsub
                for b in range(nsub):
                    bs = slice(b * SUB, (b + 1) * SUB)
                    suf = _dot_exact_r(l1m[:, bs], u_strict)
                    pre = pre + _rowsum(l1m[:, bs])
                    att = jnp.where(mask[:, bs], jnp.exp(lsg[:, bs] + suf + (ctot_h[h] - pre)), 0.0)
                    p = att * d_att[:, bs]
                    dcum = ecar + _dot_exact_r(p, l_strict)
                    ecar = ecar + _rowsum(p)
                    sg = sig[:, bs]
                    dz_parts[b] = jnp.where(mask[:, bs], p * (1.0 - sg) - sg * dcum, 0.0)
                    att_parts[b] = att
                att = jnp.concatenate(att_parts, axis=1).astype(BF16)
                dz = jnp.concatenate(dz_parts, axis=1).astype(BF16)
                dq_acc = dq_acc + _dot(dz, kblk)
                dk_add = dk_add + _dot_tn(dz, qh[h])
                dv_add = dv_add + _dot_tn(att, doh[h])
                new += [dq_acc, pre, ecar]
            dk_ref[pl.ds(k0, W), :] += dk_add
            dv_ref[pl.ds(k0, W), :] += dv_add
            return tuple(new)

        zero_o = jnp.zeros((B, B), F32)
        zero_c = jnp.zeros((B, 1), F32)
        res = lax.fori_loop(0, nrun, span, (zero_o, zero_c, zero_c, zero_o, zero_c, zero_c))
        dq_ref[...] = jnp.where(head_a, res[0], res[3]) * scale

    blk = pl.BlockSpec((B, B), lambda p, i: (i, p))
    col = pl.BlockSpec((LP, B), lambda p, i: (0, p))
    out = jax.ShapeDtypeStruct((LP, 512), F32)
    return pl.pallas_call(
        body, name="sb_bwd", grid=(SB_HEADS // 2, nq),
        in_specs=[pl.BlockSpec(memory_space=pltpu.SMEM),
                  pl.BlockSpec((B, B), lambda p, i: (i, qcol + p)),
                  pl.BlockSpec((LP, B), lambda p, i: (0, kcol + p)),
                  pl.BlockSpec((LP, B), lambda p, i: (0, vcol + p)),
                  blk, blk],
        out_specs=[blk, col, col], out_shape=[out, out, out],
        compiler_params=_cp(("arbitrary", "arbitrary")),
    )(nrun_all, proj, proj, proj, ctot, do)


def _sb_group_mean():
    r = jnp.right_shift(_iota2((512, 512), 0), 6)
    c = jnp.right_shift(_iota2((512, 512), 1), 6)
    return jnp.where(r == c, 1.0 / SB_DH, 0.0).astype(BF16)


def _attn_norm_fwd(og, proj, osb, gnw, snw):
    LP = og.shape[0]
    T = _tile(LP, 256)

    def body(og_ref, z_ref, os_ref, gnw_ref, snw_ref, y_ref):
        valid = (pl.program_id(0) * T + _iota2((T, 1), 0)) >= PAD_ROWS
        z = z_ref[...]
        zg = z * _sigmoid(z)
        for h in range(GDN_HEADS):
            sl = slice(h * GDN_D, (h + 1) * GDN_D)
            o = og_ref[:, sl]
            y = o * _rms(o) * gnw_ref[...] * zg[:, sl]
            y_ref[:, sl] = jnp.where(valid, y, 0.0).astype(BF16)
        o = os_ref[...]
        msq = _dot_exact_r(o * o, _sb_group_mean())
        y = o * lax.rsqrt(msq + NORM_EPS) * snw_ref[...]
        y_ref[:, 512:] = jnp.where(valid, y, 0.0).astype(BF16)

    row = pl.BlockSpec((T, 512), lambda i: (i, 0))
    return pl.pallas_call(
        body, name="attn_norm_fwd", grid=(LP // T,),
        in_specs=[row, pl.BlockSpec((T, 512), lambda i: (i, OFF_Z // 512)), row,
                  pl.BlockSpec((1, GDN_D), lambda i: (0, 0)), pl.BlockSpec((1, 512), lambda i: (0, 0))],
        out_specs=pl.BlockSpec((T, 1024), lambda i: (i, 0)),
        out_shape=jax.ShapeDtypeStruct((LP, 1024), BF16),
        compiler_params=_cp(("parallel",)),
    )(og, proj, osb, gnw, snw)


def _attn_norm_bwd(og, proj, osb, gnw, snw, dy):
    LP = og.shape[0]
    T = _tile(LP, 256)

    def body(og_ref, z_ref, os_ref, gnw_ref, snw_ref, dy_ref, dog_ref, dz_ref, dos_ref, dgw_ref, dsw_ref):
        @pl.when(pl.program_id(0) == 0)
        def _():
            dgw_ref[...] = jnp.zeros_like(dgw_ref)
            dsw_ref[...] = jnp.zeros_like(dsw_ref)
        valid = (pl.program_id(0) * T + _iota2((T, 1), 0)) >= PAD_ROWS
        dy = jnp.where(valid, dy_ref[...], 0.0)
        z = z_ref[...]
        sg = _sigmoid(z)
        zg = z * sg
        dgw = jnp.zeros((1, GDN_D), F32)
        for h in range(GDN_HEADS):
            sl = slice(h * GDN_D, (h + 1) * GDN_D)
            o = og_ref[:, sl]
            dyh = dy[:, sl]
            dx, dwn = _rms_bwd(o, gnw_ref[...], dyh * zg[:, sl])
            dog_ref[:, sl] = dx
            dgw = dgw + _colsum(dwn)
            yn = o * _rms(o) * gnw_ref[...]
            dz_ref[:, sl] = dyh * yn * (sg[:, sl] * (1.0 + z[:, sl] * (1.0 - sg[:, sl])))
        dgw_ref[...] += dgw
        o = os_ref[...]
        gm = _sb_group_mean()
        r = lax.rsqrt(_dot_exact_r(o * o, gm) + NORM_EPS)
        n = o * r
        dys = dy[:, 512:]
        dyw = dys * snw_ref[...]
        dos_ref[...] = r * (dyw - n * _dot_exact_r(dyw * n, gm))
        dsw_ref[...] += _colsum(dys * n)

    row = pl.BlockSpec((T, 512), lambda i: (i, 0))
    gw = pl.BlockSpec((1, GDN_D), lambda i: (0, 0))
    sw = pl.BlockSpec((1, 512), lambda i: (0, 0))
    o512 = jax.ShapeDtypeStruct((LP, 512), F32)
    return pl.pallas_call(
        body, name="attn_norm_bwd", grid=(LP // T,),
        in_specs=[row, pl.BlockSpec((T, 512), lambda i: (i, OFF_Z // 512)), row, gw, sw,
                  pl.BlockSpec((T, 1024), lambda i: (i, 0))],
        out_specs=[row, row, row, gw, sw],
        out_shape=[o512, o512, o512, jax.ShapeDtypeStruct((1, GDN_D), F32), jax.ShapeDtypeStruct((1, 512), F32)],
        compiler_params=_cp(("arbitrary",)),
    )(og, proj, osb, gnw, snw, dy)


def _resid_fwd(h0, mix, w_post, w_pre):
    LP, D = h0.shape
    T = _tile(LP, 512)

    def body(h0_ref, mix_ref, wp_ref, wf_ref, h1_ref, n2_ref):
        mix = mix_ref[...]
        h1 = h0_ref[...] + mix * _rms(mix) * wp_ref[...]
        h1_ref[...] = h1
        n2_ref[...] = (h1 * _rms(h1) * wf_ref[...]).astype(BF16)

    row = pl.BlockSpec((T, D), lambda i: (i, 0))
    vec = pl.BlockSpec((1, D), lambda i: (0, 0))
    return pl.pallas_call(
        body, name="resid_fwd", grid=(LP // T,),
        in_specs=[row, row, vec, vec], out_specs=[row, row],
        out_shape=[jax.ShapeDtypeStruct((LP, D), F32), jax.ShapeDtypeStruct((LP, D), BF16)],
        compiler_params=_cp(("parallel",)),
    )(h0, mix, w_post, w_pre)


def _resid_bwd(h1, mix, w_post, w_pre, dout, dn2):
    LP, D = h1.shape
    T = _tile(LP, 512)

    def body(h1_ref, mix_ref, wp_ref, wf_ref, dout_ref, dn2_ref, dh1_ref, dmix_ref, dwf_ref, dwp_ref):
        @pl.when(pl.program_id(0) == 0)
        def _():
            dwf_ref[...] = jnp.zeros_like(dwf_ref)
            dwp_ref[...] = jnp.zeros_like(dwp_ref)
        dx, dwn = _rms_bwd(h1_ref[...], wf_ref[...], dn2_ref[...])
        dh1 = dout_ref[...] + dx
        dh1_ref[...] = dh1
        dwf_ref[...] += _colsum(dwn)
        dmix, dwn2 = _rms_bwd(mix_ref[...], wp_ref[...], dh1)
        dmix_ref[...] = dmix.astype(BF16)
        dwp_ref[...] += _colsum(dwn2)

    row = pl.BlockSpec((T, D), lambda i: (i, 0))
    vec = pl.BlockSpec((1, D), lambda i: (0, 0))
    v = jax.ShapeDtypeStruct((1, D), F32)
    return pl.pallas_call(
        body, name="resid_bwd", grid=(LP // T,),
        in_specs=[row, row, vec, vec, row, row], out_specs=[row, row, vec, vec],
        out_shape=[jax.ShapeDtypeStruct((LP, D), F32), jax.ShapeDtypeStruct((LP, D), BF16), v, v],
        compiler_params=_cp(("arbitrary",)),
    )(h1, mix, w_post, w_pre, dout, dn2)


GELU_C = 0.7978845608028654
GELU_A = 0.044715


def _gelu_parts(x):
    t = jnp.tanh(GELU_C * (x + GELU_A * x * x * x))
    return 0.5 * x * (1.0 + t), t


def _convglu_fwd(up, conv_w, conv_b):
    LP, C = up.shape
    T = _tile(LP, 128)

    def body(x_ref, halo_ref, cw_ref, cb_ref, act_ref, xs):
        i = pl.program_id(0)
        xs[0:8, :] = jnp.where(i > 0, halo_ref[...], 0.0)
        xs[8:, :] = x_ref[...]
        y = jnp.broadcast_to(cb_ref[...], (T, C))
        for j in range(FFN_CONV):
            y = y + cw_ref[j:j + 1, :] * xs[pl.ds(8 - (FFN_CONV - 1) + j, T), :]
        g, _ = _gelu_parts(y[:, :D_FF])
        act_ref[...] = (g * y[:, D_FF:]).astype(BF16)

    t8 = T // 8
    return pl.pallas_call(
        body, name="convglu_fwd", grid=(LP // T,),
        in_specs=[pl.BlockSpec((T, C), lambda i: (i, 0)),
                  pl.BlockSpec((8, C), lambda i: (jnp.maximum(i * t8 - 1, 0), 0)),
                  pl.BlockSpec((FFN_CONV, C), lambda i: (0, 0)), pl.BlockSpec((1, C), lambda i: (0, 0))],
        out_specs=pl.BlockSpec((T, D_FF), lambda i: (i, 0)),
        out_shape=jax.ShapeDtypeStruct((LP, D_FF), BF16),
        scratch_shapes=[pltpu.VMEM((T + 8, C), F32)],
        compiler_params=_cp(("parallel",)),
    )(up, up, conv_w, conv_b)


def _convglu_bwd(up, conv_w, conv_b, dact):
    LP, C = up.shape
    T = _tile(LP, 128)
    TE = T + 8
    nt = LP // T

    def body(x_ref, xp_ref, xn_ref, cw_ref, cb_ref, da_ref, dan_ref, dx_ref, dcw_ref, dcb_ref, xs, dys):
        i = pl.program_id(0)

        @pl.when(i == 0)
        def _():
            dcw_ref[...] = jnp.zeros_like(dcw_ref)
            dcb_ref[...] = jnp.zeros_like(dcb_ref)

        last = i == nt - 1
        xs[0:8, :] = jnp.where(i > 0, xp_ref[...], 0.0)
        xs[8:8 + T, :] = x_ref[...]
        xs[8 + T:, :] = jnp.where(last, 0.0, xn_ref[...])
        y = jnp.broadcast_to(cb_ref[...], (TE, C))
        for j in range(FFN_CONV):
            y = y + cw_ref[j:j + 1, :] * xs[pl.ds(8 - (FFN_CONV - 1) + j, TE), :]
        gate, val = y[:, :D_FF], y[:, D_FF:]
        g, t = _gelu_parts(gate)
        dg_dx = 0.5 * (1.0 + t) + 0.5 * gate * (1.0 - t * t) * GELU_C * (1.0 + 3.0 * GELU_A * gate * gate)
        da = jnp.concatenate([da_ref[...], jnp.where(last, 0.0, dan_ref[...])], axis=0)
        dys[:, :D_FF] = da * val * dg_dx
        dys[:, D_FF:] = da * g
        dy_t = dys[0:T, :]
        dcb_ref[...] += _colsum(dy_t)
        for j in range(FFN_CONV):
            dcw_ref[j:j + 1, :] += _colsum(dy_t * xs[pl.ds(8 - (FFN_CONV - 1) + j, T), :])
        dx = jnp.zeros((T, C), F32)
        for j in range(FFN_CONV):
            dx = dx + cw_ref[j:j + 1, :] * dys[pl.ds(FFN_CONV - 1 - j, T), :]
        dx_ref[...] = dx.astype(BF16)

    t8 = T // 8
    nb8 = LP // 8
    prev8 = lambda w: pl.BlockSpec((8, w), lambda i: (jnp.maximum(i * t8 - 1, 0), 0))
    next8 = lambda w: pl.BlockSpec((8, w), lambda i: (jnp.minimum((i + 1) * t8, nb8 - 1), 0))
    row = lambda w: pl.BlockSpec((T, w), lambda i: (i, 0))
    small = lambda r: pl.BlockSpec((r, C), lambda i: (0, 0))
    return pl.pallas_call(
        body, name="convglu_bwd", grid=(nt,),
        in_specs=[row(C), prev8(C), next8(C), small(FFN_CONV), small(1), row(D_FF), next8(D_FF)],
        out_specs=[row(C), small(FFN_CONV), small(1)],
        out_shape=[jax.ShapeDtypeStruct((LP, C), BF16), jax.ShapeDtypeStruct((FFN_CONV, C), F32),
                   jax.ShapeDtypeStruct((1, C), F32)],
        scratch_shapes=[pltpu.VMEM((T + 16, C), F32), pltpu.VMEM((TE, C), F32)],
        compiler_params=_cp(("arbitrary",)),
    )(up, up, up, conv_w, conv_b, dact, dact)


def _final(h1, f, w_post, target, n_real):
    LP, D = h1.shape
    T = _tile(LP, 256)

    def body(h1_ref, f_ref, w_ref, t_ref, loss_ref, dout_ref, df_ref, dw_ref):
        @pl.when(pl.program_id(0) == 0)
        def _():
            loss_ref[...] = jnp.zeros_like(loss_ref)
            dw_ref[...] = jnp.zeros_like(dw_ref)
        rows = pl.program_id(0) * T + _iota2((T, 1), 0)
        real = (rows >= ROW0) & (rows < ROW0 + n_real)
        f = f_ref[...]
        out = h1_ref[...] + f * _rms(f) * w_ref[...]
        err = jnp.where(real, out - t_ref[...], 0.0)
        loss_ref[...] += 0.5 * jnp.sum(_colsum(jnp.mean(err * err, axis=-1, keepdims=True)), axis=-1, keepdims=True)
        dout = err * (1.0 / D)
        dout_ref[...] = dout
        dx, dwn = _rms_bwd(f, w_ref[...], dout)
        df_ref[...] = dx.astype(BF16)
        dw_ref[...] += _colsum(dwn)

    row = pl.BlockSpec((T, D), lambda i: (i, 0))
    vec = pl.BlockSpec((1, D), lambda i: (0, 0))
    return pl.pallas_call(
        body, name="final_loss", grid=(LP // T,),
        in_specs=[row, row, vec, row],
        out_specs=[pl.BlockSpec((1, 128), lambda i: (0, 0)), row, row, vec],
        out_shape=[jax.ShapeDtypeStruct((1, 128), F32), jax.ShapeDtypeStruct((LP, D), F32),
                   jax.ShapeDtypeStruct((LP, D), BF16), jax.ShapeDtypeStruct((1, D), F32)],
        compiler_params=_cp(("arbitrary",)),
    )(h1, f, w_post, target)


def _exchange(arrs, gather, name):
    n = len(arrs)
    npeer = N_DEV - 1

    def body(*refs):
        ins, outs = refs[:n], refs[n:2 * n]
        send_sems, recv_sems, loc_sems = refs[2 * n:]
        x, y, c = lax.axis_index("x"), lax.axis_index("y"), lax.axis_index("c")
        me = 4 * x + 2 * y + c
        copies = []
        for a in range(n):
            for kk in range(1, N_DEV):
                px = 1 - x if kk & 4 else x
                py = 1 - y if kk & 2 else y
                pc = 1 - c if kk & 1 else c
                src = ins[a] if gather else ins[a].at[4 * px + 2 * py + pc]
                s = a * npeer + kk - 1
                cp = pltpu.make_async_remote_copy(src_ref=src, dst_ref=outs[a].at[me], send_sem=send_sems.at[s],
                                                  recv_sem=recv_sems.at[s], device_id=(px, py, pc), device_id_type=MESH)
                cp.start()
                copies.append(cp)
            own = pltpu.make_async_copy(ins[a] if gather else ins[a].at[me], outs[a].at[me], loc_sems.at[a])
            own.start()
            copies.append(own)
        for cp in copies:
            cp.wait()

    any_spec = pl.BlockSpec(memory_space=pl.ANY)
    shapes = [jax.ShapeDtypeStruct((N_DEV,) + tuple(a.shape) if gather else tuple(a.shape), a.dtype) for a in arrs]
    return pl.pallas_call(
        body, name=name, in_specs=[any_spec] * n, out_specs=[any_spec] * n, out_shape=shapes,
        scratch_shapes=[pltpu.SemaphoreType.DMA((n * npeer,)), pltpu.SemaphoreType.DMA((n * npeer,)),
                        pltpu.SemaphoreType.DMA((n,))],
        compiler_params=pltpu.CompilerParams(has_side_effects=True),
    )(*arrs)


def _adamw(parts, w, m, v, name):
    R, C = w.shape
    cap = max(8, ((2 * 1024 * 1024) // (4 * C * 12)) // 8 * 8)
    T = R if R <= cap else _tile(R, cap, 8)

    def body(p_ref, w_ref, m_ref, v_ref, g_ref, d_ref, nm_ref, nv_ref):
        g = p_ref[0]
        for k in range(1, N_DEV):
            g = g + p_ref[k]
        mm = ADAM_B1 * m_ref[...] + (1.0 - ADAM_B1) * g
        vv = ADAM_B2 * v_ref[...] + (1.0 - ADAM_B2) * (g * g)
        m_hat = mm / (1.0 - ADAM_B1 ** ADAM_STEP)
        v_hat = vv / (1.0 - ADAM_B2 ** ADAM_STEP)
        g_ref[...] = g
        d_ref[...] = -ADAM_LR * (m_hat / (jnp.sqrt(v_hat) + ADAM_EPS) + ADAM_WD * w_ref[...])
        nm_ref[...] = mm
        nv_ref[...] = vv

    row = pl.BlockSpec((T, C), lambda i: (i, 0))
    out = jax.ShapeDtypeStruct((R, C), F32)
    return pl.pallas_call(
        body, name=name, grid=(R // T,),
        in_specs=[pl.BlockSpec((N_DEV, T, C), lambda i: (0, i, 0)), row, row, row],
        out_specs=[row] * 4, out_shape=[out] * 4,
        compiler_params=_cp(("parallel",)),
    )(parts, w, m, v)


SMALL = ("attn_pre_norm", "gdn_A_log", "gdn_dt_bias", "gdn_norm_w", "sb_norm_w", "attn_post_norm",
         "ffn_pre_norm", "ffn_conv_b", "ffn_post_norm")


def _pack_small(arrs):
    rows = []
    for a in arrs:
        flat = a.reshape(-1).astype(F32)
        n = -(-flat.shape[0] // 128) * 128
        rows.append(jnp.pad(flat, (0, n - flat.shape[0])).reshape(-1, 128))
    slab = jnp.concatenate(rows, axis=0)
    pad = (-slab.shape[0]) % 8
    return jnp.pad(slab, ((0, pad), (0, 0)))


def _unpack_small(slab, shapes):
    out, r = [], 0
    for shp in shapes:
        size = 1
        for s in shp:
            size *= s
        nr = -(-size // 128)
        out.append(slab[r:r + nr].reshape(-1)[:size].reshape(shp))
        r += nr
    return out


def _to_blocks_cols(a):
    R, C = a.shape
    return a.reshape(R, N_DEV, C // N_DEV).transpose(1, 0, 2)


def _from_blocks_cols(a):
    n, R, c = a.shape
    return a.transpose(1, 0, 2).reshape(R, n * c)


def kernel(x, meta_tokens, attn_pre_norm, w_in, gdn_conv_w, gdn_A_log, gdn_dt_bias, gdn_norm_w, sb_norm_w, w_out, attn_post_norm, ffn_pre_norm, w_ffn_up, ffn_conv_w, ffn_conv_b, w_ffn_down, ffn_post_norm, loss_target, m_meta_tokens, m_attn_pre_norm, m_w_in, m_gdn_conv_w, m_gdn_A_log, m_gdn_dt_bias, m_gdn_norm_w, m_sb_norm_w, m_w_out, m_attn_post_norm, m_ffn_pre_norm, m_w_ffn_up, m_ffn_conv_w, m_ffn_conv_b, m_w_ffn_down, m_ffn_post_norm, v_meta_tokens, v_attn_pre_norm, v_w_in, v_gdn_conv_w, v_gdn_A_log, v_gdn_dt_bias, v_gdn_norm_w, v_sb_norm_w, v_w_out, v_attn_post_norm, v_ffn_pre_norm, v_w_ffn_up, v_ffn_conv_w, v_ffn_conv_b, v_w_ffn_down, v_ffn_post_norm):
    args = dict(locals())
    seq = x.shape[1]
    LP = -(-(ROW0 + seq) // LP_ALIGN) * LP_ALIGN
    tail = LP - ROW0 - seq

    gathered = _exchange(
        [w_in[0].astype(BF16), w_out[0].astype(BF16), w_ffn_up[0].astype(BF16), w_ffn_down[0].astype(BF16),
         gdn_conv_w[0], ffn_conv_w[0], meta_tokens], gather=True, name="gather_weights")
    win_o = _from_blocks_cols(gathered[0])
    o_ab = C_QKV
    o_z = o_ab + 2 * GDN_HEADS
    w_inp = jnp.concatenate([win_o[:, :C_QKV], win_o[:, o_z:o_z + C_Z], win_o[:, o_z + C_Z:],
                             win_o[:, o_ab:o_z], jnp.zeros((D_MODEL, C_AB - 2 * GDN_HEADS), BF16)], axis=1)
    w_out_f = gathered[1].reshape(D_MODEL, D_MODEL)
    w_up_f = _from_blocks_cols(gathered[2])
    w_down_f = gathered[3].reshape(D_FF, D_MODEL)
    gconv_f = _from_blocks_cols(gathered[4])
    fconv_f = _from_blocks_cols(gathered[5])
    meta_f = _from_blocks_cols(gathered[6])

    h0 = jnp.concatenate([jnp.zeros((PAD_ROWS, D_MODEL), F32), meta_f, x[0], jnp.zeros((tail, D_MODEL), F32)], axis=0)
    target = jnp.concatenate([jnp.zeros((ROW0, D_MODEL), F32), loss_target[0], jnp.zeros((tail, D_MODEL), F32)], axis=0)
    u = _prenorm_fwd(h0, attn_pre_norm)
    proj = _mm(u, w_inp, F32, "mm_in")
    qn, kn, vg, beta_b, g_b = _gdn_pre_fwd(proj, gconv_f, gdn_A_log, gdn_dt_bias)
    cu, cw, cqd, ckd, cqk, ct, cgl = _gdn_chunk_fwd(qn, kn, vg, beta_b, g_b)
    og, ssave = _gdn_scan_fwd(cu, cw, cqd, ckd, cqk, cgl)
    osb, ctot, sb_nrun = _sb_fwd(proj)
    snw = sb_norm_w.reshape(1, SB_HEADS * SB_DH)
    y = _attn_norm_fwd(og, proj, osb, gdn_norm_w, snw)
    mix = _mm(y, w_out_f, F32, "mm_out")
    h1, n2 = _resid_fwd(h0, mix, attn_post_norm, ffn_pre_norm)
    up = _mm(n2, w_up_f, F32, "mm_up")
    act = _convglu_fwd(up, fconv_f, ffn_conv_b)
    f = _mm(act, w_down_f, F32, "mm_down")
    loss_part, dout, df, d_fpost = _final(h1, f, ffn_post_norm, target, seq)
    loss = lax.psum(loss_part[0, 0], ("x", "y", "c"))

    d_wdown = _mm_tn(act, df, "mm_dw_down")
    dact = _mm(df, w_down_f.T, F32, "mm_dact")
    dup, d_fconv, d_fconvb = _convglu_bwd(up, fconv_f, ffn_conv_b, dact)
    d_wup = _mm_tn(n2, dup, "mm_dw_up")
    dn2 = _mm(dup, w_up_f.T, F32, "mm_dn2")
    dh1, dmix, d_fpre, d_apost = _resid_bwd(h1, mix, attn_post_norm, ffn_pre_norm, dout, dn2)
    d_wout = _mm_tn(y, dmix, "mm_dw_out")
    dy = _mm(dmix, w_out_f.T, F32, "mm_dy")
    dog, dz, dos, d_gnw, d_snw = _attn_norm_bwd(og, proj, osb, gdn_norm_w, snw, dy)
    dqs, dks, dvs = _sb_bwd(proj, ctot, sb_nrun, dos)
    du_, dw_, dqd_, dkd_, dqk_, dgl_ = _gdn_scan_bwd(cu, cw, cqd, ckd, cqk, cgl, ssave, dog)
    dqn, dkn, dvg, dbeta, dg = _gdn_chunk_bwd(qn, kn, vg, beta_b, g_b, ct, du_, dw_, dqd_, dkd_, dqk_, dgl_)
    dqkv, dab, d_gconv, d_gsc = _gdn_pre_bwd(proj, gconv_f, gdn_A_log, gdn_dt_bias, dqn, dkn, dvg, dbeta, dg)
    dproj = jnp.concatenate([dqkv.astype(BF16), dz.astype(BF16), dqs.astype(BF16), dks.astype(BF16),
                             dvs.astype(BF16), dab.astype(BF16)], axis=1)
    d_winp = _mm_tn(u, dproj, "mm_dw_in")
    du0 = _mm(dproj, w_inp.T, F32, "mm_du")
    dh0, d_apre = _prenorm_bwd(h0, attn_pre_norm, du0, dh1)
    grad_x = dh0[ROW0:ROW0 + seq][None]
    d_meta = dh0[PAD_ROWS:ROW0]

    d_win = jnp.concatenate([d_winp[:, :C_QKV], d_winp[:, OFF_AB:OFF_AB + 2 * GDN_HEADS],
                             d_winp[:, OFF_Z:OFF_Z + C_Z], d_winp[:, OFF_SB:OFF_SB + C_SB]], axis=1)
    small_grads = [d_apre, d_gsc[0:1, :GDN_HEADS], d_gsc[1:2, :GDN_HEADS], d_gnw, d_snw.reshape(1, SB_HEADS, SB_DH),
                   d_apost, d_fpre, d_fconvb, d_fpost]
    big_names = ("w_in", "w_out", "w_ffn_up", "w_ffn_down", "gdn_conv_w", "ffn_conv_w", "meta_tokens")
    sends = [_to_blocks_cols(d_win), d_wout.reshape(N_DEV, D_MODEL // N_DEV, D_MODEL), _to_blocks_cols(d_wup),
             d_wdown.reshape(N_DEV, D_FF // N_DEV, D_MODEL), _to_blocks_cols(d_gconv), _to_blocks_cols(d_fconv),
             _to_blocks_cols(d_meta)]
    recv = _exchange(sends, gather=False, name="scatter_grads")
    slab_parts = _exchange([_pack_small(small_grads)], gather=True, name="gather_small_grads")[0]

    res = {}
    for nm, parts in zip(big_names, recv):
        wloc = args[nm]
        shp = wloc.shape
        w2 = wloc.reshape(shp[-2], shp[-1])
        outs = _adamw(parts, w2, args["m_" + nm].reshape(w2.shape), args["v_" + nm].reshape(w2.shape), "adamw_" + nm)
        res[nm] = [o.reshape(shp) for o in outs]
    small_shapes = [args[nm].shape for nm in SMALL]
    outs = _adamw(slab_parts, _pack_small([args[nm] for nm in SMALL]), _pack_small([args["m_" + nm] for nm in SMALL]),
                  _pack_small([args["v_" + nm] for nm in SMALL]), "adamw_small")
    for k in range(4):
        for nm, val in zip(SMALL, _unpack_small(outs[k], small_shapes)):
            res.setdefault(nm, [None] * 4)[k] = val

    order = ("meta_tokens", "attn_pre_norm", "w_in", "gdn_conv_w", "gdn_A_log", "gdn_dt_bias", "gdn_norm_w",
             "sb_norm_w", "w_out", "attn_post_norm", "ffn_pre_norm", "w_ffn_up", "ffn_conv_w", "ffn_conv_b",
             "w_ffn_down", "ffn_post_norm")
    return (loss, grad_x, *[res[nm][0] for nm in order], *[res[nm][1] for nm in order],
            *[res[nm][2] for nm in order], *[res[nm][3] for nm in order])
```

```python
import functools

import jax
import jax.numpy as jnp
from jax import lax
from jax.experimental import pallas as pl
from jax.experimental.pallas import tpu as pltpu

F32 = jnp.float32
BF16 = jnp.bfloat16

D_MODEL = 1024
N_META = 16
GDN_HEADS = 4
GDN_D = 128
GDN_CHUNK = 64
GDN_CONV = 4
GDN_ROWS = 128
SB_HEADS = 8
SB_DH = 64
SB_BLOCK = 128
D_FF = 2816
FFN_CONV = 3
NORM_EPS = 1e-6
L2_EPS = 1e-6
N_DEV = 8

PAD_ROWS = SB_BLOCK - N_META
ROW0 = SB_BLOCK
SB_SPAN = 512
SB_DEAD = -104.0
SB_SUB = 256
LP_ALIGN = 256

C_QKV = 3 * GDN_HEADS * GDN_D
C_Z = GDN_HEADS * GDN_D
C_SB = 3 * SB_HEADS * SB_DH
C_AB = 256
OFF_Z = C_QKV
OFF_SB = OFF_Z + C_Z
OFF_AB = OFF_SB + C_SB
D_INP = OFF_AB + C_AB
D_IN = C_QKV + 2 * GDN_HEADS + C_Z + C_SB

ADAM_LR = 0.001
ADAM_B1 = 0.9
ADAM_B2 = 0.999
ADAM_EPS = 1e-08
ADAM_WD = 0.01
ADAM_STEP = 10

VMEM_LIMIT = 56 * 1024 * 1024
MESH = pl.DeviceIdType.MESH


def _cp(sem=None):
    kw = dict(vmem_limit_bytes=VMEM_LIMIT)
    if sem is not None:
        kw["dimension_semantics"] = sem
    return pltpu.CompilerParams(**kw)


def _tile(n, cap, unit=128):
    best = None
    t = unit
    while t <= min(n, cap):
        if n % t == 0:
            best = t
        t += unit
    assert best is not None, (n, cap, unit)
    return best


def _dot(a, b):
    return jnp.dot(a, b, preferred_element_type=F32)


def _dot_nt(a, b):
    return lax.dot_general(a, b, (((1,), (1,)), ((), ())), preferred_element_type=F32)


def _dot_tn(a, b):
    return lax.dot_general(a, b, (((0,), (0,)), ((), ())), preferred_element_type=F32)


def _split(x):
    hi = x.astype(BF16)
    lo = (x - hi.astype(F32)).astype(BF16)
    return hi, lo


def _dot3(a, b, f=_dot):
    ah, al = _split(a)
    bh, bl = _split(b)
    return f(ah, bh) + (f(ah, bl) + f(al, bh))


def _dot_exact_l(m_bf16, x, f=_dot):
    xh, xl = _split(x)
    return f(m_bf16, xh) + f(m_bf16, xl)


def _dot_exact_r(x, m_bf16, f=_dot):
    xh, xl = _split(x)
    return f(xh, m_bf16) + f(xl, m_bf16)


def _iota2(shape, dim):
    return lax.broadcasted_iota(jnp.int32, shape, dim)


def _sigmoid(x):
    return 1.0 / (1.0 + jnp.exp(-x))


def _softplus(x):
    return jnp.maximum(x, 0.0) + jnp.log(1.0 + jnp.exp(-jnp.abs(x)))


def _colsum(x):
    return jnp.sum(x, axis=0, keepdims=True)


def _rowsum(x):
    return jnp.sum(x, axis=-1, keepdims=True)


def _mm(a, b, out_dtype, name):
    M, K = a.shape
    K2, N = b.shape
    assert K == K2
    tm = _tile(M, 768)
    tn = _tile(N, max(128, (6 * 1024 * 1024) // (2 * K)))

    def body(a_ref, b_ref, o_ref):
        o_ref[...] = _dot(a_ref[...].astype(BF16), b_ref[...].astype(BF16)).astype(o_ref.dtype)

    return pl.pallas_call(
        body, name=name, grid=(N // tn, M // tm),
        in_specs=[pl.BlockSpec((tm, K), lambda j, i: (i, 0)), pl.BlockSpec((K, tn), lambda j, i: (0, j))],
        out_specs=pl.BlockSpec((tm, tn), lambda j, i: (i, j)),
        out_shape=jax.ShapeDtypeStruct((M, N), out_dtype),
        compiler_params=_cp(("parallel", "parallel")),
    )(a, b)


def _mm_tn(a, b, name):
    M, K = a.shape
    M2, N = b.shape
    assert M == M2
    tm = _tile(M, 768)
    tk = _tile(K, 1024)
    tn = _tile(N, 1408)

    def body(a_ref, b_ref, o_ref):
        @pl.when(pl.program_id(2) == 0)
        def _():
            o_ref[...] = jnp.zeros_like(o_ref)
        o_ref[...] += _dot_tn(a_ref[...].astype(BF16), b_ref[...].astype(BF16))

    return pl.pallas_call(
        body, name=name, grid=(K // tk, N // tn, M // tm),
        in_specs=[pl.BlockSpec((tm, tk), lambda i, j, m: (m, i)), pl.BlockSpec((tm, tn), lambda i, j, m: (m, j))],
        out_specs=pl.BlockSpec((tk, tn), lambda i, j, m: (i, j)),
        out_shape=jax.ShapeDtypeStruct((K, N), F32),
        compiler_params=_cp(("parallel", "parallel", "arbitrary")),
    )(a, b)


def _rms(x):
    return lax.rsqrt(jnp.mean(x * x, axis=-1, keepdims=True) + NORM_EPS)


def _rms_bwd(x, w, dy):
    r = _rms(x)
    n = x * r
    dyw = dy * w
    dx = r * (dyw - n * jnp.mean(dyw * n, axis=-1, keepdims=True))
    return dx, dy * n


def _prenorm_fwd(h0, w):
    LP, D = h0.shape
    T = _tile(LP, 512)

    def body(h_ref, w_ref, u_ref):
        h = h_ref[...]
        u_ref[...] = (h * _rms(h) * w_ref[...]).astype(BF16)

    return pl.pallas_call(
        body, name="prenorm_fwd", grid=(LP // T,),
        in_specs=[pl.BlockSpec((T, D), lambda i: (i, 0)), pl.BlockSpec((1, D), lambda i: (0, 0))],
        out_specs=pl.BlockSpec((T, D), lambda i: (i, 0)),
        out_shape=jax.ShapeDtypeStruct((LP, D), BF16),
        compiler_params=_cp(("parallel",)),
    )(h0, w)


def _prenorm_bwd(h0, w, du, dh1):
    LP, D = h0.shape
    T = _tile(LP, 512)

    def body(h_ref, w_ref, du_ref, dh1_ref, dh0_ref, dw_ref):
        @pl.when(pl.program_id(0) == 0)
        def _():
            dw_ref[...] = jnp.zeros_like(dw_ref)
        dx, dwn = _rms_bwd(h_ref[...], w_ref[...], du_ref[...])
        dh0_ref[...] = dh1_ref[...] + dx
        dw_ref[...] += _colsum(dwn)

    row = pl.BlockSpec((T, D), lambda i: (i, 0))
    vec = pl.BlockSpec((1, D), lambda i: (0, 0))
    return pl.pallas_call(
        body, name="prenorm_bwd", grid=(LP // T,),
        in_specs=[row, vec, row, row], out_specs=[row, vec],
        out_shape=[jax.ShapeDtypeStruct((LP, D), F32), jax.ShapeDtypeStruct((1, D), F32)],
        compiler_params=_cp(("arbitrary",)),
    )(h0, w, du, dh1)


def _gdn_gate_consts(alog_ref, dtb_ref, h):
    a_coef = -jnp.exp(alog_ref[0:1, h:h + 1])
    return a_coef, dtb_ref[0:1, h:h + 1]


def _gdn_pre_fwd(proj, conv_w, a_log, dt_bias):
    LP = proj.shape[0]
    T = _tile(LP, 256)
    C = C_QKV
    H = GDN_HEADS

    def body(x_ref, halo_ref, ab_ref, cw_ref, alog_ref, dtb_ref, q_ref, k_ref, v_ref, beta_ref, g_ref, xs):
        i = pl.program_id(0)
        xs[0:8, :] = jnp.where(i > 0, halo_ref[...], 0.0)
        xs[8:, :] = x_ref[...]
        y = jnp.zeros((T, C), F32)
        for j in range(GDN_CONV):
            y = y + cw_ref[j:j + 1, :] * xs[pl.ds(8 - (GDN_CONV - 1) + j, T), :]
        c = y * _sigmoid(y)
        for h in range(H):
            sl = slice(h * GDN_D, (h + 1) * GDN_D)
            cq = c[:, sl]
            q_ref[:, sl] = cq * lax.rsqrt(_rowsum(cq * cq) + L2_EPS) * (GDN_D ** -0.5)
            ck = c[:, 512 + h * GDN_D:512 + (h + 1) * GDN_D]
            k_ref[:, sl] = ck * lax.rsqrt(_rowsum(ck * ck) + L2_EPS)
        v_ref[...] = c[:, 1024:]
        ab = ab_ref[...]
        valid = (i * T + _iota2((T, 1), 0)) >= PAD_ROWS
        for h in range(H):
            sl = slice(h * GDN_D, (h + 1) * GDN_D)
            a_coef, dtb = _gdn_gate_consts(alog_ref, dtb_ref, h)
            g = jnp.where(valid, a_coef * _softplus(ab[:, h:h + 1] + dtb), 0.0)
            beta = jnp.where(valid, _sigmoid(ab[:, H + h:H + h + 1]), 0.0)
            g_ref[:, sl] = jnp.broadcast_to(g, (T, GDN_D))
            beta_ref[:, sl] = jnp.broadcast_to(beta, (T, GDN_D))

    t8 = T // 8
    row512 = pl.BlockSpec((T, 512), lambda i: (i, 0))
    small = lambda r, c: pl.BlockSpec((r, c), lambda i: (0, 0))
    out = jax.ShapeDtypeStruct((LP, 512), F32)
    return pl.pallas_call(
        body, name="gdn_pre_fwd", grid=(LP // T,),
        in_specs=[pl.BlockSpec((T, C), lambda i: (i, 0)),
                  pl.BlockSpec((8, C), lambda i: (jnp.maximum(i * t8 - 1, 0), 0)),
                  pl.BlockSpec((T, C_AB), lambda i: (i, OFF_AB // C_AB)),
                  small(GDN_CONV, C), small(1, H), small(1, H)],
        out_specs=[row512] * 5, out_shape=[out] * 5,
        scratch_shapes=[pltpu.VMEM((T + 8, C), F32)],
        compiler_params=_cp(("parallel",)),
    )(proj, proj, proj, conv_w, a_log, dt_bias)


def _gdn_pre_bwd(proj, conv_w, a_log, dt_bias, dq, dk, dv, dbeta, dg):
    LP = proj.shape[0]
    T = _tile(LP, 256)
    C = C_QKV
    H = GDN_HEADS
    TE = T + 8
    nt = LP // T

    def body(x_ref, xp_ref, xn_ref, ab_ref, cw_ref, alog_ref, dtb_ref,
             dq_ref, dqn_ref, dk_ref, dkn_ref, dv_ref, dvn_ref, dbeta_ref, dg_ref,
             dx_ref, dab_ref, dcw_ref, dsc_ref, xs, dys):
        i = pl.program_id(0)

        @pl.when(i == 0)
        def _():
            dcw_ref[...] = jnp.zeros_like(dcw_ref)
            dsc_ref[...] = jnp.zeros_like(dsc_ref)

        last = i == nt - 1
        xs[0:8, :] = jnp.where(i > 0, xp_ref[...], 0.0)
        xs[8:8 + T, :] = x_ref[...]
        xs[8 + T:, :] = jnp.where(last, 0.0, xn_ref[...])
        y = jnp.zeros((TE, C), F32)
        for j in range(GDN_CONV):
            y = y + cw_ref[j:j + 1, :] * xs[pl.ds(8 - (GDN_CONV - 1) + j, TE), :]
        sg = _sigmoid(y)
        c = y * sg
        nxt = lambda a_ref, b_ref: jnp.concatenate([a_ref[...], jnp.where(last, 0.0, b_ref[...])], axis=0)
        dqn = nxt(dq_ref, dqn_ref)
        dkn = nxt(dk_ref, dkn_ref)
        dvv = nxt(dv_ref, dvn_ref)
        for h in range(H):
            sl = slice(h * GDN_D, (h + 1) * GDN_D)
            cq = c[:, sl]
            rq = lax.rsqrt(_rowsum(cq * cq) + L2_EPS)
            nq = cq * rq
            dqh = dqn[:, sl]
            dys[:, sl] = (GDN_D ** -0.5) * rq * (dqh - nq * _rowsum(dqh * nq))
            sk = slice(512 + h * GDN_D, 512 + (h + 1) * GDN_D)
            ck = c[:, sk]
            rk = lax.rsqrt(_rowsum(ck * ck) + L2_EPS)
            nk = ck * rk
            dkh = dkn[:, sl]
            dys[:, sk] = rk * (dkh - nk * _rowsum(dkh * nk))
        dys[:, 1024:] = dvv
        dy = dys[...] * (sg * (1.0 + y * (1.0 - sg)))
        dys[...] = dy
        for j in range(GDN_CONV):
            dcw_ref[j:j + 1, :] += _colsum(dy[0:T, :] * xs[pl.ds(8 - (GDN_CONV - 1) + j, T), :])
        dx = jnp.zeros((T, C), F32)
        for j in range(GDN_CONV):
            dx = dx + cw_ref[j:j + 1, :] * dys[pl.ds(GDN_CONV - 1 - j, T), :]
        dx_ref[...] = dx
        ab = ab_ref[...]
        valid = (i * T + _iota2((T, 1), 0)) >= PAD_ROWS
        lane = _iota2((T, C_AB), 1)
        lane1 = _iota2((1, 128), 1)
        dab = jnp.zeros((T, C_AB), F32)
        dsc_a = jnp.zeros((1, 128), F32)
        dsc_d = jnp.zeros((1, 128), F32)
        for h in range(H):
            a_coef, dtb = _gdn_gate_consts(alog_ref, dtb_ref, h)
            pre = ab[:, h:h + 1] + dtb
            dgh = jnp.where(valid, dg_ref[:, h * GDN_D:h * GDN_D + 1], 0.0)
            da = dgh * a_coef * _sigmoid(pre)
            beta = _sigmoid(ab[:, H + h:H + h + 1])
            db = jnp.where(valid, dbeta_ref[:, h * GDN_D:h * GDN_D + 1], 0.0) * beta * (1.0 - beta)
            dab = dab + jnp.where(lane == h, da, 0.0) + jnp.where(lane == H + h, db, 0.0)
            dsc_a = dsc_a + jnp.where(lane1 == h, _colsum(dgh * a_coef * _softplus(pre)), 0.0)
            dsc_d = dsc_d + jnp.where(lane1 == h, _colsum(da), 0.0)
        dab_ref[...] = dab
        dsc_ref[0:1, :] += dsc_a
        dsc_ref[1:2, :] += dsc_d

    t8 = T // 8
    nb8 = LP // 8
    prev8 = lambda w: pl.BlockSpec((8, w), lambda i: (jnp.maximum(i * t8 - 1, 0), 0))
    next8 = lambda w: pl.BlockSpec((8, w), lambda i: (jnp.minimum((i + 1) * t8, nb8 - 1), 0))
    row = lambda w: pl.BlockSpec((T, w), lambda i: (i, 0))
    small = lambda r, c: pl.BlockSpec((r, c), lambda i: (0, 0))
    return pl.pallas_call(
        body, name="gdn_pre_bwd", grid=(nt,),
        in_specs=[row(C), prev8(C), next8(C), pl.BlockSpec((T, C_AB), lambda i: (i, OFF_AB // C_AB)),
                  small(GDN_CONV, C), small(1, H), small(1, H),
                  row(512), next8(512), row(512), next8(512), row(512), next8(512), row(512), row(512)],
        out_specs=[row(C), row(C_AB), small(GDN_CONV, C), small(2, 128)],
        out_shape=[jax.ShapeDtypeStruct((LP, C), F32), jax.ShapeDtypeStruct((LP, C_AB), F32),
                   jax.ShapeDtypeStruct((GDN_CONV, C), F32), jax.ShapeDtypeStruct((2, 128), F32)],
        scratch_shapes=[pltpu.VMEM((T + 16, C), F32), pltpu.VMEM((TE, C), F32)],
        compiler_params=_cp(("arbitrary",)),
    )(proj, proj, proj, proj, conv_w, a_log, dt_bias, dq, dq, dk, dk, dv, dv, dbeta, dg)


def _tri_masks():
    r = _iota2((GDN_CHUNK, GDN_CHUNK), 0)
    c = _iota2((GDN_CHUNK, GDN_CHUNK), 1)
    return r >= c, r > c


def _gdn_chunk_common(q, k, v, beta, gb):
    incl, strict = _tri_masks()
    l_incl = incl.astype(BF16)
    gd = _dot_exact_l(l_incl, jnp.where(strict, gb[:, :GDN_CHUNK], 0.0))
    gc = _dot_exact_l(l_incl, gb)
    decay = jnp.where(incl, jnp.exp(jnp.where(incl, gd, 0.0)), 0.0)
    exp_g = jnp.exp(gc)
    g_last = gc[GDN_CHUNK - 1:GDN_CHUNK, :]
    kd_fac = jnp.exp(g_last - gc)
    gl = jnp.exp(g_last)
    kb = k * beta
    kk = _dot3(kb, k, _dot_nt)
    return dict(incl=incl, strict=strict, decay=decay, exp_g=exp_g, kd_fac=kd_fac, gl=gl, kb=kb, kk=kk,
                vb=v * beta, kbg=kb * exp_g)


def _interleave(gens):
    gens = list(gens)
    while gens:
        alive = []
        for g in gens:
            try:
                next(g)
                alive.append(g)
            except StopIteration:
                pass
        gens = alive


def _gdn_chunk_fwd(qn, kn, v, beta_b, g_b):
    LP = qn.shape[0]
    R = GDN_ROWS
    H = GDN_HEADS
    CH = GDN_CHUNK

    def body(q_ref, k_ref, v_ref, b_ref, g_ref, u_ref, w_ref, qd_ref, kd_ref, qk_ref, t_ref, gl_ref):
        def item(cc, h):
            rs = slice(cc * CH, (cc + 1) * CH)
            sl = slice(h * GDN_D, (h + 1) * GDN_D)
            s64 = slice(h * CH, (h + 1) * CH)
            q, k = q_ref[rs, sl], k_ref[rs, sl]
            m = _gdn_chunk_common(q, k, v_ref[rs, sl], b_ref[rs, sl], g_ref[rs, sl])
            qk_raw = _dot3(q, k, _dot_nt)
            yield
            a = jnp.where(m["strict"], m["kk"] * m["decay"], 0.0)
            eye = (_iota2((CH, CH), 0) == _iota2((CH, CH), 1)).astype(F32)
            t = eye - a
            p = _dot3(a, a)
            yield
            for _ in range(4):
                t = t + _dot3(t, p)
                p = _dot3(p, p)
                yield
            t = t + _dot3(t, p)
            yield
            u_ref[rs, sl] = _dot3(t, m["vb"])
            w_ref[rs, sl] = _dot3(t, m["kbg"])
            qk_ref[rs, s64] = qk_raw * m["decay"]
            t_ref[rs, s64] = t
            qd_ref[rs, sl] = q * m["exp_g"]
            kd_ref[rs, sl] = k * m["kd_fac"]
            gl_ref[cc * 8:(cc + 1) * 8, sl] = jnp.broadcast_to(m["gl"], (8, GDN_D))

        _interleave(item(cc, h) for cc in range(R // CH) for h in range(H))

    row = lambda w: pl.BlockSpec((R, w), lambda i: (i, 0))
    o512 = jax.ShapeDtypeStruct((LP, 512), F32)
    o256 = jax.ShapeDtypeStruct((LP, 256), F32)
    return pl.pallas_call(
        body, name="gdn_chunk_fwd", grid=(LP // R,),
        in_specs=[row(512)] * 5,
        out_specs=[row(512)] * 4 + [row(256)] * 2 + [pl.BlockSpec((R // 8, 512), lambda i: (i, 0))],
        out_shape=[o512] * 4 + [o256] * 2 + [jax.ShapeDtypeStruct((LP // 8, 512), F32)],
        compiler_params=_cp(("parallel",)),
    )(qn, kn, v, beta_b, g_b)


def _gdn_chunk_bwd(qn, kn, v, beta_b, g_b, t_all, du, dw, dqd, dkd, dqk, dgl):
    LP = qn.shape[0]
    R = GDN_ROWS
    H = GDN_HEADS
    CH = GDN_CHUNK

    def body(q_ref, k_ref, v_ref, b_ref, g_ref, t_ref, du_ref, dw_ref, dqd_ref, dkd_ref, dqk_ref, dgl_ref,
             dq_ref, dk_ref, dv_ref, db_ref, dg_ref):
        ones = jnp.ones((CH, GDN_D), BF16)

        def item(cc, h):
            rs = slice(cc * CH, (cc + 1) * CH)
            sl = slice(h * GDN_D, (h + 1) * GDN_D)
            s64 = slice(h * CH, (h + 1) * CH)
            q, k, vv, beta = q_ref[rs, sl], k_ref[rs, sl], v_ref[rs, sl], b_ref[rs, sl]
            m = _gdn_chunk_common(q, k, vv, beta, g_ref[rs, sl])
            incl, strict, decay = m["incl"], m["strict"], m["decay"]
            t = t_ref[rs, s64]
            du_, dw_ = du_ref[rs, sl], dw_ref[rs, sl]
            dqd_, dkd_ = dqd_ref[rs, sl], dkd_ref[rs, sl]
            d_t = _dot3(du_, m["vb"], _dot_nt) + _dot3(dw_, m["kbg"], _dot_nt)
            dvb = _dot3(t, du_, _dot_tn)
            dkbg = _dot3(t, dw_, _dot_tn)
            qk_raw = _dot3(q, k, _dot_nt)
            yield
            x1 = _dot3(d_t, t, _dot_nt)
            dkb = dkbg * m["exp_g"]
            d_gi = _rowsum(dkbg * m["kbg"])
            yield
            d_a = jnp.where(strict, -_dot3(t, x1, _dot_tn), 0.0)
            yield
            d_kk = d_a * decay
            dqk_m = jnp.where(incl, dqk_ref[rs, s64], 0.0)
            dqk_raw = dqk_m * decay
            mm = (d_a * m["kk"] + dqk_m * qk_raw) * decay
            dkb = dkb + _dot3(d_kk, k)
            dk_ = _dot3(d_kk, m["kb"], _dot_tn) + _dot3(dqk_raw, q, _dot_tn)
            dq_ = _dot3(dqk_raw, k) + dqd_ * m["exp_g"]
            d_gi = d_gi + (_dot_exact_r(mm, ones) - _dot_exact_r(mm, ones, _dot_tn))
            yield
            d_gi = d_gi + _rowsum(dqd_ * q * m["exp_g"])
            e = _rowsum(dkd_ * k * m["kd_fac"])
            d_gi = d_gi - e
            d_glast = _colsum(jnp.broadcast_to(e, (CH, GDN_D))) + dgl_ref[cc * 8:cc * 8 + 1, sl] * m["gl"]
            dk_ = dk_ + dkd_ * m["kd_fac"] + dkb * beta
            d_gi = d_gi + jnp.where(_iota2((CH, GDN_D), 0) == CH - 1, d_glast, 0.0)
            u_incl = (_iota2((CH, CH), 1) >= _iota2((CH, CH), 0)).astype(BF16)
            dq_ref[rs, sl] = dq_
            dk_ref[rs, sl] = dk_
            dv_ref[rs, sl] = dvb * beta
            db_ref[rs, sl] = jnp.broadcast_to(_rowsum(dvb * vv) + _rowsum(dkb * k), (CH, GDN_D))
            dg_ref[rs, sl] = _dot_exact_l(u_incl, d_gi)

        _interleave(item(cc, h) for cc in range(R // CH) for h in range(H))

    row = lambda w: pl.BlockSpec((R, w), lambda i: (i, 0))
    o512 = jax.ShapeDtypeStruct((LP, 512), F32)
    gl_spec = pl.BlockSpec((R // 8, 512), lambda i: (i, 0))
    return pl.pallas_call(
        body, name="gdn_chunk_bwd", grid=(LP // R,),
        in_specs=[row(512)] * 5 + [row(256)] + [row(512)] * 4 + [row(256), gl_spec],
        out_specs=[row(512)] * 5, out_shape=[o512] * 5,
        compiler_params=_cp(("parallel",)),
    )(qn, kn, v, beta_b, g_b, t_all, du, dw, dqd, dkd, dqk, dgl)


def _gdn_scan_fwd(u, w, qd, kd, qk, gl):
    LP = u.shape[0]
    CH = GDN_CHUNK
    N = LP // CH
    H = GDN_HEADS

    def body(u_ref, w_ref, qd_ref, kd_ref, qk_ref, gl_ref, o_ref, ssave_ref, s_sc):
        @pl.when(pl.program_id(0) == 0)
        def _():
            s_sc[...] = jnp.zeros_like(s_sc)
        ssave_ref[...] = s_sc[...]

        def item(h):
            sl = slice(h * GDN_D, (h + 1) * GDN_D)
            s = s_sc[:, sl]
            v_new = u_ref[:, sl] - _dot3(w_ref[:, sl], s)
            o_s = _dot3(qd_ref[:, sl], s)
            yield
            o_ref[:, sl] = o_s + _dot3(qk_ref[:, h * CH:(h + 1) * CH], v_new)
            s_sc[:, sl] = s * gl_ref[0:1, sl] + _dot3(kd_ref[:, sl], v_new, _dot_tn)

        _interleave(item(h) for h in range(H))

    row = lambda w_: pl.BlockSpec((CH, w_), lambda n: (n, 0))
    return pl.pallas_call(
        body, name="gdn_scan_fwd", grid=(N,),
        in_specs=[row(512)] * 4 + [row(256), pl.BlockSpec((8, 512), lambda n: (n, 0))],
        out_specs=[row(512), pl.BlockSpec((GDN_D, 512), lambda n: (n, 0))],
        out_shape=[jax.ShapeDtypeStruct((LP, 512), F32), jax.ShapeDtypeStruct((N * GDN_D, 512), F32)],
        scratch_shapes=[pltpu.VMEM((GDN_D, 512), F32)],
        compiler_params=_cp(("arbitrary",)),
    )(u, w, qd, kd, qk, gl)


def _gdn_scan_bwd(u, w, qd, kd, qk, gl, ssave, do):
    LP = u.shape[0]
    CH = GDN_CHUNK
    N = LP // CH
    H = GDN_HEADS

    def body(u_ref, w_ref, qd_ref, kd_ref, qk_ref, gl_ref, s_ref, do_ref,
             du_ref, dw_ref, dqd_ref, dkd_ref, dqk_ref, dgl_ref, ds_sc):
        @pl.when(pl.program_id(0) == 0)
        def _():
            ds_sc[...] = jnp.zeros_like(ds_sc)
        def item(h):
            sl = slice(h * GDN_D, (h + 1) * GDN_D)
            s64 = slice(h * CH, (h + 1) * CH)
            s = s_ref[:, sl]
            ds = ds_sc[:, sl]
            do_ = do_ref[:, sl]
            w_, qd_, kd_, qk_ = w_ref[:, sl], qd_ref[:, sl], kd_ref[:, sl], qk_ref[:, s64]
            v_new = u_ref[:, sl] - _dot3(w_, s)
            d_vnew = _dot3(qk_, do_, _dot_tn) + _dot3(kd_, ds)
            dqd_ref[:, sl] = _dot3(do_, s, _dot_nt)
            ds_new = ds * gl_ref[0:1, sl] + _dot3(qd_, do_, _dot_tn)
            dgl_ref[:, sl] = jnp.broadcast_to(jnp.sum(_colsum(ds * s), axis=-1, keepdims=True), (8, GDN_D))
            yield
            du_ref[:, sl] = d_vnew
            dw_ref[:, sl] = -_dot3(d_vnew, s, _dot_nt)
            dkd_ref[:, sl] = _dot3(v_new, ds, _dot_nt)
            dqk_ref[:, s64] = _dot3(do_, v_new, _dot_nt)
            ds_sc[:, sl] = ds_new - _dot3(w_, d_vnew, _dot_tn)

        _interleave(item(h) for h in range(H))

    rev = lambda w_: pl.BlockSpec((CH, w_), lambda n: (N - 1 - n, 0))
    rev8 = pl.BlockSpec((8, 512), lambda n: (N - 1 - n, 0))
    o512 = jax.ShapeDtypeStruct((LP, 512), F32)
    return pl.pallas_call(
        body, name="gdn_scan_bwd", grid=(N,),
        in_specs=[rev(512)] * 4 + [rev(256), rev8, pl.BlockSpec((GDN_D, 512), lambda n: (N - 1 - n, 0)), rev(512)],
        out_specs=[rev(512)] * 4 + [rev(256), rev8],
        out_shape=[o512] * 4 + [jax.ShapeDtypeStruct((LP, 256), F32), jax.ShapeDtypeStruct((LP // 8, 512), F32)],
        scratch_shapes=[pltpu.VMEM((GDN_D, 512), F32)],
        compiler_params=_cp(("arbitrary",)),
    )(u, w, qd, kd, qk, gl, ssave, do)


def _sb_scores(qh, kblk, mask):
    z = _dot_nt(qh, kblk)
    e = jnp.exp(-jnp.abs(z))
    sp = jnp.maximum(z, 0.0) + jnp.log(1.0 + e)
    return z, e, jnp.where(mask, -sp, 0.0), z - sp


def _sb_fwd(proj):
    LP = proj.shape[0]
    B = SB_BLOCK
    W = min(SB_SPAN, LP)
    SUB = SB_SUB
    nq = LP // B
    nsub = W // SUB
    scale = SB_DH ** -0.5
    qcol, kcol, vcol = OFF_SB // B, (OFF_SB + 512) // B, (OFF_SB + 1024) // B

    def body(q_ref, k_ref, v_ref, o_ref, c_ref, n_ref):
        i = pl.program_id(1)
        lane = _iota2((B, B), 1)
        head_a = lane < SB_DH
        qs = q_ref[...] * scale
        qh = [jnp.where(head_a, qs, 0.0).astype(BF16), jnp.where(head_a, 0.0, qs).astype(BF16)]
        u_strict = (_iota2((SUB, SUB), 0) > _iota2((SUB, SUB), 1)).astype(BF16)
        qpos = i * B + _iota2((B, W), 0)
        hi0 = (i + 1) * B
        nspan = (hi0 + W - 1) // W

        def live(st):
            return (st[0] < nspan) & (st[1] > 0)

        def span(st):
            r, carry = st[0], st[2:]
            hi = hi0 - r * W
            k0 = pl.multiple_of(jnp.maximum(hi - W, 0), B)
            kblk = k_ref[pl.ds(k0, W), :].astype(BF16)
            vblk = v_ref[pl.ds(k0, W), :].astype(BF16)
            kpos = k0 + _iota2((B, W), 1)
            mask = (kpos < qpos) & (kpos >= PAD_ROWS) & (kpos < hi)
            new = []
            for h in range(2):
                o_acc, c = carry[2 * h], carry[2 * h + 1]
                z, e, l1m, lsg = _sb_scores(qh[h], kblk, mask)
                parts = [None] * nsub
                for b in reversed(range(nsub)):
                    bs = slice(b * SUB, (b + 1) * SUB)
                    suf = _dot_exact_r(l1m[:, bs], u_strict)
                    parts[b] = jnp.where(mask[:, bs], jnp.exp(lsg[:, bs] + suf + c), 0.0)
                    c = c + _rowsum(l1m[:, bs])
                att = jnp.concatenate(parts, axis=1).astype(BF16)
                new += [o_acc + _dot(att, vblk), c]
            more = (jnp.maximum(jnp.max(new[1]), jnp.max(new[3])) > SB_DEAD).astype(jnp.int32)
            return (r + 1, more, *new)

        zero_o = jnp.zeros((B, B), F32)
        zero_c = jnp.zeros((B, 1), F32)
        nrun, _, o_a, c_a, o_b, c_b = lax.while_loop(
            live, span, (jnp.int32(0), jnp.int32(1), zero_o, zero_c, zero_o, zero_c))
        o_ref[...] = jnp.where(head_a, o_a, o_b)
        c_ref[...] = jnp.where(head_a, c_a, c_b)
        n_ref[pl.program_id(0), i] = nrun

    blk = pl.BlockSpec((B, B), lambda p, i: (i, p))
    out = jax.ShapeDtypeStruct((LP, 512), F32)
    return pl.pallas_call(
        body, name="sb_fwd", grid=(SB_HEADS // 2, nq),
        in_specs=[pl.BlockSpec((B, B), lambda p, i: (i, qcol + p)),
                  pl.BlockSpec((LP, B), lambda p, i: (0, kcol + p)),
                  pl.BlockSpec((LP, B), lambda p, i: (0, vcol + p))],
        out_specs=[blk, blk, pl.BlockSpec(memory_space=pltpu.SMEM)],
        out_shape=[out, out, jax.ShapeDtypeStruct((SB_HEADS // 2, nq), jnp.int32)],
        compiler_params=_cp(("arbitrary", "arbitrary")),
    )(proj, proj, proj)


def _sb_bwd(proj, ctot, nrun_all, do):
    LP = proj.shape[0]
    B = SB_BLOCK
    W = min(SB_SPAN, LP)
    SUB = SB_SUB
    nq = LP // B
    nsub = W // SUB
    scale = SB_DH ** -0.5
    qcol, kcol, vcol = OFF_SB // B, (OFF_SB + 512) // B, (OFF_SB + 1024) // B

    def body(n_ref, q_ref, k_ref, v_ref, c_ref, do_ref, dq_ref, dk_ref, dv_ref):
        i = pl.program_id(1)

        @pl.when(i == 0)
        def _():
            dk_ref[...] = jnp.zeros_like(dk_ref)
            dv_ref[...] = jnp.zeros_like(dv_ref)

        lane = _iota2((B, B), 1)
        head_a = lane < SB_DH
        qs = q_ref[...] * scale
        qh = [jnp.where(head_a, qs, 0.0).astype(BF16), jnp.where(head_a, 0.0, qs).astype(BF16)]
        dof = do_ref[...]
        doh = [jnp.where(head_a, dof, 0.0).astype(BF16), jnp.where(head_a, 0.0, dof).astype(BF16)]
        cfull = c_ref[...]
        ctot_h = [cfull[:, 0:1], cfull[:, SB_DH:SB_DH + 1]]
        sub_r, sub_c = _iota2((SUB, SUB), 0), _iota2((SUB, SUB), 1)
        u_strict = (sub_r > sub_c).astype(BF16)
        l_strict = (sub_r < sub_c).astype(BF16)
        qpos = i * B + _iota2((B, W), 0)
        hi0 = (i + 1) * B
        nrun = n_ref[pl.program_id(0), i]

        def span(t, carry):
            r = nrun - 1 - t
            hi = hi0 - r * W
            k0 = pl.multiple_of(jnp.maximum(hi - W, 0), B)
            kblk = k_ref[pl.ds(k0, W), :].astype(BF16)
            vblk = v_ref[pl.ds(k0, W), :].astype(BF16)
            kpos = k0 + _iota2((B, W), 1)
            mask = (kpos < qpos) & (kpos >= PAD_ROWS) & (kpos < hi)
            new = []
            dk_add = jnp.zeros((W, B), F32)
            dv_add = jnp.zeros((W, B), F32)
            for h in range(2):
                dq_acc, pre, ecar = carry[3 * h], carry[3 * h + 1], carry[3 * h + 2]
                z, e, l1m, lsg = _sb_scores(qh[h], kblk, mask)
                d_att = _dot_nt(doh[h], vblk)
                sig = jnp.where(z >= 0.0, 1.0, e) / (1.0 + e)
                att_parts, dz_parts = [None] * nsub, [None] * nsub
                for b in range(nsub):
                    bs = slice(b * SUB, (b + 1) * SUB)
                    suf = _dot_exact_r(l1m[:, bs], u_strict)
                    pre = pre + _rowsum(l1m[:, bs])
                    att = jnp.where(mask[:, bs], jnp.exp(lsg[:, bs] + suf + (ctot_h[h] - pre)), 0.0)
                    p = att * d_att[:, bs]
                    dcum = ecar + _dot_exact_r(p, l_strict)
                    ecar = ecar + _rowsum(p)
                    sg = sig[:, bs]
                    dz_parts[b] = jnp.where(mask[:, bs], p * (1.0 - sg) - sg * dcum, 0.0)
                    att_parts[b] = att
                att = jnp.concatenate(att_parts, axis=1).astype(BF16)
                dz = jnp.concatenate(dz_parts, axis=1).astype(BF16)
                dq_acc = dq_acc + _dot(dz, kblk)
                dk_add = dk_add + _dot_tn(dz, qh[h])
                dv_add = dv_add + _dot_tn(att, doh[h])
                new += [dq_acc, pre, ecar]
            dk_ref[pl.ds(k0, W), :] += dk_add
            dv_ref[pl.ds(k0, W), :] += dv_add
            return tuple(new)

        zero_o = jnp.zeros((B, B), F32)
        zero_c = jnp.zeros((B, 1), F32)
        res = lax.fori_loop(0, nrun, span, (zero_o, zero_c, zero_c, zero_o, zero_c, zero_c))
        dq_ref[...] = jnp.where(head_a, res[0], res[3]) * scale

    blk = pl.BlockSpec((B, B), lambda p, i: (i, p))
    col = pl.BlockSpec((LP, B), lambda p, i: (0, p))
    out = jax.ShapeDtypeStruct((LP, 512), F32)
    return pl.pallas_call(
        body, name="sb_bwd", grid=(SB_HEADS // 2, nq),
        in_specs=[pl.BlockSpec(memory_space=pltpu.SMEM),
                  pl.BlockSpec((B, B), lambda p, i: (i, qcol + p)),
                  pl.BlockSpec((LP, B), lambda p, i: (0, kcol + p)),
                  pl.BlockSpec((LP, B), lambda p, i: (0, vcol + p)),
                  blk, blk],
        out_specs=[blk, col, col], out_shape=[out, out, out],
        compiler_params=_cp(("arbitrary", "arbitrary")),
    )(nrun_all, proj, proj, proj, ctot, do)


def _sb_group_mean():
    r = jnp.right_shift(_iota2((512, 512), 0), 6)
    c = jnp.right_shift(_iota2((512, 512), 1), 6)
    return jnp.where(r == c, 1.0 / SB_DH, 0.0).astype(BF16)


def _attn_norm_fwd(og, proj, osb, gnw, snw):
    LP = og.shape[0]
    T = _tile(LP, 256)

    def body(og_ref, z_ref, os_ref, gnw_ref, snw_ref, y_ref):
        valid = (pl.program_id(0) * T + _iota2((T, 1), 0)) >= PAD_ROWS
        z = z_ref[...]
        zg = z * _sigmoid(z)
        for h in range(GDN_HEADS):
            sl = slice(h * GDN_D, (h + 1) * GDN_D)
            o = og_ref[:, sl]
            y = o * _rms(o) * gnw_ref[...] * zg[:, sl]
            y_ref[:, sl] = jnp.where(valid, y, 0.0).astype(BF16)
        o = os_ref[...]
        msq = _dot_exact_r(o * o, _sb_group_mean())
        y = o * lax.rsqrt(msq + NORM_EPS) * snw_ref[...]
        y_ref[:, 512:] = jnp.where(valid, y, 0.0).astype(BF16)

    row = pl.BlockSpec((T, 512), lambda i: (i, 0))
    return pl.pallas_call(
        body, name="attn_norm_fwd", grid=(LP // T,),
        in_specs=[row, pl.BlockSpec((T, 512), lambda i: (i, OFF_Z // 512)), row,
                  pl.BlockSpec((1, GDN_D), lambda i: (0, 0)), pl.BlockSpec((1, 512), lambda i: (0, 0))],
        out_specs=pl.BlockSpec((T, 1024), lambda i: (i, 0)),
        out_shape=jax.ShapeDtypeStruct((LP, 1024), BF16),
        compiler_params=_cp(("parallel",)),
    )(og, proj, osb, gnw, snw)


def _attn_norm_bwd(og, proj, osb, gnw, snw, dy):
    LP = og.shape[0]
    T = _tile(LP, 256)

    def body(og_ref, z_ref, os_ref, gnw_ref, snw_ref, dy_ref, dog_ref, dz_ref, dos_ref, dgw_ref, dsw_ref):
        @pl.when(pl.program_id(0) == 0)
        def _():
            dgw_ref[...] = jnp.zeros_like(dgw_ref)
            dsw_ref[...] = jnp.zeros_like(dsw_ref)
        valid = (pl.program_id(0) * T + _iota2((T, 1), 0)) >= PAD_ROWS
        dy = jnp.where(valid, dy_ref[...], 0.0)
        z = z_ref[...]
        sg = _sigmoid(z)
        zg = z * sg
        dgw = jnp.zeros((1, GDN_D), F32)
        for h in range(GDN_HEADS):
            sl = slice(h * GDN_D, (h + 1) * GDN_D)
            o = og_ref[:, sl]
            dyh = dy[:, sl]
            dx, dwn = _rms_bwd(o, gnw_ref[...], dyh * zg[:, sl])
            dog_ref[:, sl] = dx
            dgw = dgw + _colsum(dwn)
            yn = o * _rms(o) * gnw_ref[...]
            dz_ref[:, sl] = dyh * yn * (sg[:, sl] * (1.0 + z[:, sl] * (1.0 - sg[:, sl])))
        dgw_ref[...] += dgw
        o = os_ref[...]
        gm = _sb_group_mean()
        r = lax.rsqrt(_dot_exact_r(o * o, gm) + NORM_EPS)
        n = o * r
        dys = dy[:, 512:]
        dyw = dys * snw_ref[...]
        dos_ref[...] = r * (dyw - n * _dot_exact_r(dyw * n, gm))
        dsw_ref[...] += _colsum(dys * n)

    row = pl.BlockSpec((T, 512), lambda i: (i, 0))
    gw = pl.BlockSpec((1, GDN_D), lambda i: (0, 0))
    sw = pl.BlockSpec((1, 512), lambda i: (0, 0))
    o512 = jax.ShapeDtypeStruct((LP, 512), F32)
    return pl.pallas_call(
        body, name="attn_norm_bwd", grid=(LP // T,),
        in_specs=[row, pl.BlockSpec((T, 512), lambda i: (i, OFF_Z // 512)), row, gw, sw,
                  pl.BlockSpec((T, 1024), lambda i: (i, 0))],
        out_specs=[row, row, row, gw, sw],
        out_shape=[o512, o512, o512, jax.ShapeDtypeStruct((1, GDN_D), F32), jax.ShapeDtypeStruct((1, 512), F32)],
        compiler_params=_cp(("arbitrary",)),
    )(og, proj, osb, gnw, snw, dy)


def _resid_fwd(h0, mix, w_post, w_pre):
    LP, D = h0.shape
    T = _tile(LP, 512)

    def body(h0_ref, mix_ref, wp_ref, wf_ref, h1_ref, n2_ref):
        mix = mix_ref[...]
        h1 = h0_ref[...] + mix * _rms(mix) * wp_ref[...]
        h1_ref[...] = h1
        n2_ref[...] = (h1 * _rms(h1) * wf_ref[...]).astype(BF16)

    row = pl.BlockSpec((T, D), lambda i: (i, 0))
    vec = pl.BlockSpec((1, D), lambda i: (0, 0))
    return pl.pallas_call(
        body, name="resid_fwd", grid=(LP // T,),
        in_specs=[row, row, vec, vec], out_specs=[row, row],
        out_shape=[jax.ShapeDtypeStruct((LP, D), F32), jax.ShapeDtypeStruct((LP, D), BF16)],
        compiler_params=_cp(("parallel",)),
    )(h0, mix, w_post, w_pre)


def _resid_bwd(h1, mix, w_post, w_pre, dout, dn2):
    LP, D = h1.shape
    T = _tile(LP, 512)

    def body(h1_ref, mix_ref, wp_ref, wf_ref, dout_ref, dn2_ref, dh1_ref, dmix_ref, dwf_ref, dwp_ref):
        @pl.when(pl.program_id(0) == 0)
        def _():
            dwf_ref[...] = jnp.zeros_like(dwf_ref)
            dwp_ref[...] = jnp.zeros_like(dwp_ref)
        dx, dwn = _rms_bwd(h1_ref[...], wf_ref[...], dn2_ref[...])
        dh1 = dout_ref[...] + dx
        dh1_ref[...] = dh1
        dwf_ref[...] += _colsum(dwn)
        dmix, dwn2 = _rms_bwd(mix_ref[...], wp_ref[...], dh1)
        dmix_ref[...] = dmix.astype(BF16)
        dwp_ref[...] += _colsum(dwn2)

    row = pl.BlockSpec((T, D), lambda i: (i, 0))
    vec = pl.BlockSpec((1, D), lambda i: (0, 0))
    v = jax.ShapeDtypeStruct((1, D), F32)
    return pl.pallas_call(
        body, name="resid_bwd", grid=(LP // T,),
        in_specs=[row, row, vec, vec, row, row], out_specs=[row, row, vec, vec],
        out_shape=[jax.ShapeDtypeStruct((LP, D), F32), jax.ShapeDtypeStruct((LP, D), BF16), v, v],
        compiler_params=_cp(("arbitrary",)),
    )(h1, mix, w_post, w_pre, dout, dn2)


GELU_C = 0.7978845608028654
GELU_A = 0.044715


def _gelu_parts(x):
    t = jnp.tanh(GELU_C * (x + GELU_A * x * x * x))
    return 0.5 * x * (1.0 + t), t


def _convglu_fwd(up, conv_w, conv_b):
    LP, C = up.shape
    T = _tile(LP, 128)

    def body(x_ref, halo_ref, cw_ref, cb_ref, act_ref, xs):
        i = pl.program_id(0)
        xs[0:8, :] = jnp.where(i > 0, halo_ref[...], 0.0)
        xs[8:, :] = x_ref[...]
        y = jnp.broadcast_to(cb_ref[...], (T, C))
        for j in range(FFN_CONV):
            y = y + cw_ref[j:j + 1, :] * xs[pl.ds(8 - (FFN_CONV - 1) + j, T), :]
        g, _ = _gelu_parts(y[:, :D_FF])
        act_ref[...] = (g * y[:, D_FF:]).astype(BF16)

    t8 = T // 8
    return pl.pallas_call(
        body, name="convglu_fwd", grid=(LP // T,),
        in_specs=[pl.BlockSpec((T, C), lambda i: (i, 0)),
                  pl.BlockSpec((8, C), lambda i: (jnp.maximum(i * t8 - 1, 0), 0)),
                  pl.BlockSpec((FFN_CONV, C), lambda i: (0, 0)), pl.BlockSpec((1, C), lambda i: (0, 0))],
        out_specs=pl.BlockSpec((T, D_FF), lambda i: (i, 0)),
        out_shape=jax.ShapeDtypeStruct((LP, D_FF), BF16),
        scratch_shapes=[pltpu.VMEM((T + 8, C), F32)],
        compiler_params=_cp(("parallel",)),
    )(up, up, conv_w, conv_b)


def _convglu_bwd(up, conv_w, conv_b, dact):
    LP, C = up.shape
    T = _tile(LP, 128)
    TE = T + 8
    nt = LP // T

    def body(x_ref, xp_ref, xn_ref, cw_ref, cb_ref, da_ref, dan_ref, dx_ref, dcw_ref, dcb_ref, xs, dys):
        i = pl.program_id(0)

        @pl.when(i == 0)
        def _():
            dcw_ref[...] = jnp.zeros_like(dcw_ref)
            dcb_ref[...] = jnp.zeros_like(dcb_ref)

        last = i == nt - 1
        xs[0:8, :] = jnp.where(i > 0, xp_ref[...], 0.0)
        xs[8:8 + T, :] = x_ref[...]
        xs[8 + T:, :] = jnp.where(last, 0.0, xn_ref[...])
        y = jnp.broadcast_to(cb_ref[...], (TE, C))
        for j in range(FFN_CONV):
            y = y + cw_ref[j:j + 1, :] * xs[pl.ds(8 - (FFN_CONV - 1) + j, TE), :]
        gate, val = y[:, :D_FF], y[:, D_FF:]
        g, t = _gelu_parts(gate)
        dg_dx = 0.5 * (1.0 + t) + 0.5 * gate * (1.0 - t * t) * GELU_C * (1.0 + 3.0 * GELU_A * gate * gate)
        da = jnp.concatenate([da_ref[...], jnp.where(last, 0.0, dan_ref[...])], axis=0)
        dys[:, :D_FF] = da * val * dg_dx
        dys[:, D_FF:] = da * g
        dy_t = dys[0:T, :]
        dcb_ref[...] += _colsum(dy_t)
        for j in range(FFN_CONV):
            dcw_ref[j:j + 1, :] += _colsum(dy_t * xs[pl.ds(8 - (FFN_CONV - 1) + j, T), :])
        dx = jnp.zeros((T, C), F32)
        for j in range(FFN_CONV):
            dx = dx + cw_ref[j:j + 1, :] * dys[pl.ds(FFN_CONV - 1 - j, T), :]
        dx_ref[...] = dx.astype(BF16)

    t8 = T // 8
    nb8 = LP // 8
    prev8 = lambda w: pl.BlockSpec((8, w), lambda i: (jnp.maximum(i * t8 - 1, 0), 0))
    next8 = lambda w: pl.BlockSpec((8, w), lambda i: (jnp.minimum((i + 1) * t8, nb8 - 1), 0))
    row = lambda w: pl.BlockSpec((T, w), lambda i: (i, 0))
    small = lambda r: pl.BlockSpec((r, C), lambda i: (0, 0))
    return pl.pallas_call(
        body, name="convglu_bwd", grid=(nt,),
        in_specs=[row(C), prev8(C), next8(C), small(FFN_CONV), small(1), row(D_FF), next8(D_FF)],
        out_specs=[row(C), small(FFN_CONV), small(1)],
        out_shape=[jax.ShapeDtypeStruct((LP, C), BF16), jax.ShapeDtypeStruct((FFN_CONV, C), F32),
                   jax.ShapeDtypeStruct((1, C), F32)],
        scratch_shapes=[pltpu.VMEM((T + 16, C), F32), pltpu.VMEM((TE, C), F32)],
        compiler_params=_cp(("arbitrary",)),
    )(up, up, up, conv_w, conv_b, dact, dact)


def _final(h1, f, w_post, target, n_real):
    LP, D = h1.shape
    T = _tile(LP, 256)

    def body(h1_ref, f_ref, w_ref, t_ref, loss_ref, dout_ref, df_ref, dw_ref):
        @pl.when(pl.program_id(0) == 0)
        def _():
            loss_ref[...] = jnp.zeros_like(loss_ref)
            dw_ref[...] = jnp.zeros_like(dw_ref)
        rows = pl.program_id(0) * T + _iota2((T, 1), 0)
        real = (rows >= ROW0) & (rows < ROW0 + n_real)
        f = f_ref[...]
        out = h1_ref[...] + f * _rms(f) * w_ref[...]
        err = jnp.where(real, out - t_ref[...], 0.0)
        loss_ref[...] += 0.5 * jnp.sum(_colsum(jnp.mean(err * err, axis=-1, keepdims=True)), axis=-1, keepdims=True)
        dout = err * (1.0 / D)
        dout_ref[...] = dout
        dx, dwn = _rms_bwd(f, w_ref[...], dout)
        df_ref[...] = dx.astype(BF16)
        dw_ref[...] += _colsum(dwn)

    row = pl.BlockSpec((T, D), lambda i: (i, 0))
    vec = pl.BlockSpec((1, D), lambda i: (0, 0))
    return pl.pallas_call(
        body, name="final_loss", grid=(LP // T,),
        in_specs=[row, row, vec, row],
        out_specs=[pl.BlockSpec((1, 128), lambda i: (0, 0)), row, row, vec],
        out_shape=[jax.ShapeDtypeStruct((1, 128), F32), jax.ShapeDtypeStruct((LP, D), F32),
                   jax.ShapeDtypeStruct((LP, D), BF16), jax.ShapeDtypeStruct((1, D), F32)],
        compiler_params=_cp(("arbitrary",)),
    )(h1, f, w_post, target)


ANY_SPEC = pl.BlockSpec(memory_space=pl.ANY)
N_CHIP = 4


def _other_chips(x, y):
    return [(1 - x, y), (x, 1 - y), (1 - x, 1 - y)]


def _gather_direct(arrs, name):
    n = len(arrs)
    npeer = N_DEV - 1

    def body(*refs):
        ins, outs = refs[:n], refs[n:2 * n]
        send_sems, recv_sems, loc_sems = refs[2 * n:]
        x, y, c = lax.axis_index("x"), lax.axis_index("y"), lax.axis_index("c")
        me = 4 * x + 2 * y + c
        copies = []
        for a in range(n):
            for kk in range(1, N_DEV):
                px = 1 - x if kk & 4 else x
                py = 1 - y if kk & 2 else y
                pc = 1 - c if kk & 1 else c
                s = a * npeer + kk - 1
                cp = pltpu.make_async_remote_copy(src_ref=ins[a], dst_ref=outs[a].at[me], send_sem=send_sems.at[s],
                                                  recv_sem=recv_sems.at[s], device_id=(px, py, pc), device_id_type=MESH)
                cp.start()
                copies.append(cp)
            own = pltpu.make_async_copy(ins[a], outs[a].at[me], loc_sems.at[a])
            own.start()
            copies.append(own)
        for cp in copies:
            cp.wait()

    shapes = [jax.ShapeDtypeStruct((N_DEV,) + tuple(a.shape), a.dtype) for a in arrs]
    return pl.pallas_call(
        body, name=name, in_specs=[ANY_SPEC] * n, out_specs=[ANY_SPEC] * n, out_shape=shapes,
        scratch_shapes=[pltpu.SemaphoreType.DMA((n * npeer,)), pltpu.SemaphoreType.DMA((n * npeer,)),
                        pltpu.SemaphoreType.DMA((n,))],
        compiler_params=pltpu.CompilerParams(has_side_effects=True),
    )(*arrs)


def _gather_two_level(arrs, name):
    n = len(arrs)
    K = 7

    def body(*refs):
        ins, outs = refs[:n], refs[n:2 * n]
        send_sems, recv_sems, loc_sems = refs[2 * n:]
        x, y, c = lax.axis_index("x"), lax.axis_index("y"), lax.axis_index("c")
        me = 4 * x + 2 * y + c
        sib = (x, y, 1 - c)
        chips = _other_chips(x, y)

        def cp(a, k, src, slot, to):
            return pltpu.make_async_remote_copy(src_ref=src, dst_ref=outs[a].at[slot], send_sem=send_sems.at[a * K + k],
                                                recv_sem=recv_sems.at[a * K + k], device_id=to, device_id_type=MESH)

        owns, first, passed = [], [], []
        for a in range(n):
            own = pltpu.make_async_copy(ins[a], outs[a].at[me], loc_sems.at[a])
            own.start()
            owns.append(own)
            first.append(cp(a, 0, ins[a], me, sib))
            for j, (px, py) in enumerate(chips):
                first.append(cp(a, 1 + j, ins[a], me, (px, py, c)))
        for f in first:
            f.start()
        for j, (px, py) in enumerate(chips):
            slot = 4 * px + 2 * py + c
            for a in range(n):
                cp(a, 1 + j, ins[a], slot, (px, py, c)).wait_recv()
                fwd = cp(a, 4 + j, outs[a].at[slot], slot, sib)
                fwd.start()
                passed.append(fwd)
        for a in range(n):
            cp(a, 0, ins[a], 4 * x + 2 * y + (1 - c), sib).wait_recv()
            for j, (px, py) in enumerate(chips):
                cp(a, 4 + j, ins[a], 4 * px + 2 * py + (1 - c), sib).wait_recv()
        for f in first + passed:
            f.wait_send()
        for own in owns:
            own.wait()

    shapes = [jax.ShapeDtypeStruct((N_DEV,) + tuple(a.shape), a.dtype) for a in arrs]
    return pl.pallas_call(
        body, name=name, in_specs=[ANY_SPEC] * n, out_specs=[ANY_SPEC] * n, out_shape=shapes,
        scratch_shapes=[pltpu.SemaphoreType.DMA((n * K,)), pltpu.SemaphoreType.DMA((n * K,)),
                        pltpu.SemaphoreType.DMA((n,))],
        compiler_params=pltpu.CompilerParams(has_side_effects=True),
    )(*arrs)


def _swap_sibling(arrs, name):
    n = len(arrs)

    def body(*refs):
        ins, outs = refs[:n], refs[n:2 * n]
        send_sems, recv_sems = refs[2 * n:]
        x, y, c = lax.axis_index("x"), lax.axis_index("y"), lax.axis_index("c")
        copies = [pltpu.make_async_remote_copy(src_ref=ins[a], dst_ref=outs[a], send_sem=send_sems.at[a],
                                               recv_sem=recv_sems.at[a], device_id=(x, y, 1 - c), device_id_type=MESH)
                  for a in range(n)]
        for cp in copies:
            cp.start()
        for cp in copies:
            cp.wait()

    shapes = [jax.ShapeDtypeStruct(tuple(a.shape), a.dtype) for a in arrs]
    return pl.pallas_call(
        body, name=name, in_specs=[ANY_SPEC] * n, out_specs=[ANY_SPEC] * n, out_shape=shapes,
        scratch_shapes=[pltpu.SemaphoreType.DMA((n,)), pltpu.SemaphoreType.DMA((n,))],
        compiler_params=pltpu.CompilerParams(has_side_effects=True),
    )(*arrs)


def _exchange_chips(arrs, name):
    n = len(arrs)
    K = N_CHIP - 1

    def body(*refs):
        ins, outs = refs[:n], refs[n:2 * n]
        send_sems, recv_sems, loc_sems = refs[2 * n:]
        x, y, c = lax.axis_index("x"), lax.axis_index("y"), lax.axis_index("c")
        mine = 2 * x + y
        copies = []
        for a in range(n):
            for j, (px, py) in enumerate(_other_chips(x, y)):
                cp = pltpu.make_async_remote_copy(src_ref=ins[a].at[2 * px + py], dst_ref=outs[a].at[mine],
                                                  send_sem=send_sems.at[a * K + j], recv_sem=recv_sems.at[a * K + j],
                                                  device_id=(px, py, c), device_id_type=MESH)
                cp.start()
                copies.append(cp)
            own = pltpu.make_async_copy(ins[a].at[mine], outs[a].at[mine], loc_sems.at[a])
            own.start()
            copies.append(own)
        for cp in copies:
            cp.wait()

    shapes = [jax.ShapeDtypeStruct(tuple(a.shape), a.dtype) for a in arrs]
    return pl.pallas_call(
        body, name=name, in_specs=[ANY_SPEC] * n, out_specs=[ANY_SPEC] * n, out_shape=shapes,
        scratch_shapes=[pltpu.SemaphoreType.DMA((n * K,)), pltpu.SemaphoreType.DMA((n * K,)),
                        pltpu.SemaphoreType.DMA((n,))],
        compiler_params=pltpu.CompilerParams(has_side_effects=True),
    )(*arrs)


def _add_halves(mine, theirs, name):
    _, R, C = mine.shape
    cap = max(16, ((2 * 1024 * 1024) // (4 * C * 10)) // 16 * 16)
    T = R if R <= cap else _tile(R, cap, 16)

    def body(a_ref, b_ref, o_ref):
        o_ref[...] = (a_ref[...] + b_ref[...].astype(F32)).astype(BF16)

    blk = pl.BlockSpec((N_CHIP, T, C), lambda i: (0, i, 0))
    return pl.pallas_call(
        body, name=name, grid=(R // T,), in_specs=[blk, blk], out_specs=blk,
        out_shape=jax.ShapeDtypeStruct(mine.shape, BF16), compiler_params=_cp(("parallel",)),
    )(mine, theirs)


def _adamw(parts, w, m, v, name):
    R, C = w.shape
    npart = parts.shape[0]
    cap = max(16, ((2 * 1024 * 1024) // (4 * C * 12)) // 16 * 16)
    T = R if R <= cap else _tile(R, cap, 16)

    def body(p_ref, w_ref, m_ref, v_ref, g_ref, d_ref, nm_ref, nv_ref):
        g = p_ref[0].astype(F32)
        for k in range(1, npart):
            g = g + p_ref[k].astype(F32)
        mm = ADAM_B1 * m_ref[...] + (1.0 - ADAM_B1) * g
        vv = ADAM_B2 * v_ref[...] + (1.0 - ADAM_B2) * (g * g)
        m_hat = mm / (1.0 - ADAM_B1 ** ADAM_STEP)
        v_hat = vv / (1.0 - ADAM_B2 ** ADAM_STEP)
        g_ref[...] = g
        d_ref[...] = -ADAM_LR * (m_hat / (jnp.sqrt(v_hat) + ADAM_EPS) + ADAM_WD * w_ref[...])
        nm_ref[...] = mm
        nv_ref[...] = vv

    row = pl.BlockSpec((T, C), lambda i: (i, 0))
    out = jax.ShapeDtypeStruct((R, C), F32)
    return pl.pallas_call(
        body, name=name, grid=(R // T,),
        in_specs=[pl.BlockSpec((npart, T, C), lambda i: (0, i, 0)), row, row, row],
        out_specs=[row] * 4, out_shape=[out] * 4,
        compiler_params=_cp(("parallel",)),
    )(parts, w, m, v)


SMALL = ("attn_pre_norm", "gdn_A_log", "gdn_dt_bias", "gdn_norm_w", "sb_norm_w", "attn_post_norm",
         "ffn_pre_norm", "ffn_conv_b", "ffn_post_norm")


def _pack_small(arrs):
    rows = []
    for a in arrs:
        flat = a.reshape(-1).astype(F32)
        n = -(-flat.shape[0] // 128) * 128
        rows.append(jnp.pad(flat, (0, n - flat.shape[0])).reshape(-1, 128))
    slab = jnp.concatenate(rows, axis=0)
    pad = (-slab.shape[0]) % 8
    return jnp.pad(slab, ((0, pad), (0, 0)))


def _unpack_small(slab, shapes):
    out, r = [], 0
    for shp in shapes:
        size = 1
        for s in shp:
            size *= s
        nr = -(-size // 128)
        out.append(slab[r:r + nr].reshape(-1)[:size].reshape(shp))
        r += nr
    return out


def _to_blocks_cols(a):
    R, C = a.shape
    return a.reshape(R, N_DEV, C // N_DEV).transpose(1, 0, 2)


def _from_blocks_cols(a):
    n, R, c = a.shape
    return a.transpose(1, 0, 2).reshape(R, n * c)


def kernel(x, meta_tokens, attn_pre_norm, w_in, gdn_conv_w, gdn_A_log, gdn_dt_bias, gdn_norm_w, sb_norm_w, w_out, attn_post_norm, ffn_pre_norm, w_ffn_up, ffn_conv_w, ffn_conv_b, w_ffn_down, ffn_post_norm, loss_target, m_meta_tokens, m_attn_pre_norm, m_w_in, m_gdn_conv_w, m_gdn_A_log, m_gdn_dt_bias, m_gdn_norm_w, m_sb_norm_w, m_w_out, m_attn_post_norm, m_ffn_pre_norm, m_w_ffn_up, m_ffn_conv_w, m_ffn_conv_b, m_w_ffn_down, m_ffn_post_norm, v_meta_tokens, v_attn_pre_norm, v_w_in, v_gdn_conv_w, v_gdn_A_log, v_gdn_dt_bias, v_gdn_norm_w, v_sb_norm_w, v_w_out, v_attn_post_norm, v_ffn_pre_norm, v_w_ffn_up, v_ffn_conv_w, v_ffn_conv_b, v_w_ffn_down, v_ffn_post_norm):
    args = dict(locals())
    seq = x.shape[1]
    LP = -(-(ROW0 + seq) // LP_ALIGN) * LP_ALIGN
    tail = LP - ROW0 - seq

    gathered = _gather_two_level(
        [w_in[0].astype(BF16), w_out[0].astype(BF16), w_ffn_up[0].astype(BF16), w_ffn_down[0].astype(BF16),
         gdn_conv_w[0], ffn_conv_w[0], meta_tokens], name="gather_weights")
    win_o = _from_blocks_cols(gathered[0])
    o_ab = C_QKV
    o_z = o_ab + 2 * GDN_HEADS
    w_inp = jnp.concatenate([win_o[:, :C_QKV], win_o[:, o_z:o_z + C_Z], win_o[:, o_z + C_Z:],
                             win_o[:, o_ab:o_z], jnp.zeros((D_MODEL, C_AB - 2 * GDN_HEADS), BF16)], axis=1)
    w_out_f = gathered[1].reshape(D_MODEL, D_MODEL)
    w_up_f = _from_blocks_cols(gathered[2])
    w_down_f = gathered[3].reshape(D_FF, D_MODEL)
    gconv_f = _from_blocks_cols(gathered[4])
    fconv_f = _from_blocks_cols(gathered[5])
    meta_f = _from_blocks_cols(gathered[6])

    h0 = jnp.concatenate([jnp.zeros((PAD_ROWS, D_MODEL), F32), meta_f, x[0], jnp.zeros((tail, D_MODEL), F32)], axis=0)
    target = jnp.concatenate([jnp.zeros((ROW0, D_MODEL), F32), loss_target[0], jnp.zeros((tail, D_MODEL), F32)], axis=0)
    u = _prenorm_fwd(h0, attn_pre_norm)
    proj = _mm(u, w_inp, F32, "mm_in")
    qn, kn, vg, beta_b, g_b = _gdn_pre_fwd(proj, gconv_f, gdn_A_log, gdn_dt_bias)
    cu, cw, cqd, ckd, cqk, ct, cgl = _gdn_chunk_fwd(qn, kn, vg, beta_b, g_b)
    og, ssave = _gdn_scan_fwd(cu, cw, cqd, ckd, cqk, cgl)
    osb, ctot, sb_nrun = _sb_fwd(proj)
    snw = sb_norm_w.reshape(1, SB_HEADS * SB_DH)
    y = _attn_norm_fwd(og, proj, osb, gdn_norm_w, snw)
    mix = _mm(y, w_out_f, F32, "mm_out")
    h1, n2 = _resid_fwd(h0, mix, attn_post_norm, ffn_pre_norm)
    up = _mm(n2, w_up_f, F32, "mm_up")
    act = _convglu_fwd(up, fconv_f, ffn_conv_b)
    f = _mm(act, w_down_f, F32, "mm_down")
    loss_part, dout, df, d_fpost = _final(h1, f, ffn_post_norm, target, seq)
    loss = lax.psum(loss_part[0, 0], ("x", "y", "c"))

    d_wdown = _mm_tn(act, df, "mm_dw_down")
    dact = _mm(df, w_down_f.T, F32, "mm_dact")
    dup, d_fconv, d_fconvb = _convglu_bwd(up, fconv_f, ffn_conv_b, dact)
    d_wup = _mm_tn(n2, dup, "mm_dw_up")
    dn2 = _mm(dup, w_up_f.T, F32, "mm_dn2")
    dh1, dmix, d_fpre, d_apost = _resid_bwd(h1, mix, attn_post_norm, ffn_pre_norm, dout, dn2)
    d_wout = _mm_tn(y, dmix, "mm_dw_out")
    dy = _mm(dmix, w_out_f.T, F32, "mm_dy")
    dog, dz, dos, d_gnw, d_snw = _attn_norm_bwd(og, proj, osb, gdn_norm_w, snw, dy)
    dqs, dks, dvs = _sb_bwd(proj, ctot, sb_nrun, dos)
    du_, dw_, dqd_, dkd_, dqk_, dgl_ = _gdn_scan_bwd(cu, cw, cqd, ckd, cqk, cgl, ssave, dog)
    dqn, dkn, dvg, dbeta, dg = _gdn_chunk_bwd(qn, kn, vg, beta_b, g_b, ct, du_, dw_, dqd_, dkd_, dqk_, dgl_)
    dqkv, dab, d_gconv, d_gsc = _gdn_pre_bwd(proj, gconv_f, gdn_A_log, gdn_dt_bias, dqn, dkn, dvg, dbeta, dg)
    dproj = jnp.concatenate([dqkv.astype(BF16), dz.astype(BF16), dqs.astype(BF16), dks.astype(BF16),
                             dvs.astype(BF16), dab.astype(BF16)], axis=1)
    d_winp = _mm_tn(u, dproj, "mm_dw_in")
    du0 = _mm(dproj, w_inp.T, F32, "mm_du")
    dh0, d_apre = _prenorm_bwd(h0, attn_pre_norm, du0, dh1)
    grad_x = dh0[ROW0:ROW0 + seq][None]
    d_meta = dh0[PAD_ROWS:ROW0]

    d_win = jnp.concatenate([d_winp[:, :C_QKV], d_winp[:, OFF_AB:OFF_AB + 2 * GDN_HEADS],
                             d_winp[:, OFF_Z:OFF_Z + C_Z], d_winp[:, OFF_SB:OFF_SB + C_SB]], axis=1)
    small_grads = [d_apre, d_gsc[0:1, :GDN_HEADS], d_gsc[1:2, :GDN_HEADS], d_gnw, d_snw.reshape(1, SB_HEADS, SB_DH),
                   d_apost, d_fpre, d_fconvb, d_fpost]
    big_names = ("w_in", "w_out", "w_ffn_up", "w_ffn_down", "gdn_conv_w", "ffn_conv_w", "meta_tokens")
    sends = [_to_blocks_cols(d_win), d_wout.reshape(N_DEV, D_MODEL // N_DEV, D_MODEL), _to_blocks_cols(d_wup),
             d_wdown.reshape(N_DEV, D_FF // N_DEV, D_MODEL), _to_blocks_cols(d_gconv), _to_blocks_cols(d_fconv),
             _to_blocks_cols(d_meta)]
    my_c = lax.axis_index("c")
    halves = [s.reshape((N_CHIP, 2) + s.shape[1:]) for s in sends]
    mine = [lax.dynamic_index_in_dim(h, my_c, axis=1, keepdims=False) for h in halves]
    theirs = _swap_sibling([lax.dynamic_index_in_dim(h, 1 - my_c, axis=1, keepdims=False).astype(BF16) for h in halves],
                           name="grads_swap_sibling")
    chip_sums = [_add_halves(a, b, "grads_add_" + nm) for nm, a, b in zip(big_names, mine, theirs)]
    recv = _exchange_chips(chip_sums, name="grads_exchange_chips")
    slab_parts = _gather_direct([_pack_small(small_grads)], name="gather_small_grads")[0]

    res = {}
    for nm, parts in zip(big_names, recv):
        wloc = args[nm]
        shp = wloc.shape
        w2 = wloc.reshape(shp[-2], shp[-1])
        outs = _adamw(parts, w2, args["m_" + nm].reshape(w2.shape), args["v_" + nm].reshape(w2.shape), "adamw_" + nm)
        res[nm] = [o.reshape(shp) for o in outs]
    small_shapes = [args[nm].shape for nm in SMALL]
    outs = _adamw(slab_parts, _pack_small([args[nm] for nm in SMALL]), _pack_small([args["m_" + nm] for nm in SMALL]),
                  _pack_small([args["v_" + nm] for nm in SMALL]), "adamw_small")
    for k in range(4):
        for nm, val in zip(SMALL, _unpack_small(outs[k], small_shapes)):
            res.setdefault(nm, [None] * 4)[k] = val

    order = ("meta_tokens", "attn_pre_norm", "w_in", "gdn_conv_w", "gdn_A_log", "gdn_dt_bias", "gdn_norm_w",
             "sb_norm_w", "w_out", "attn_post_norm", "ffn_pre_norm", "w_ffn_up", "ffn_conv_w", "ffn_conv_b",
             "w_ffn_down", "ffn_post_norm")
    return (loss, grad_x, *[res[nm][0] for nm in order], *[res[nm][1] for nm in order],
            *[res[nm][2] for nm in order], *[res[nm][3] for nm in order])
```

```python
import functools

import jax
import jax.numpy as jnp
from jax import lax
from jax.experimental import pallas as pl
from jax.experimental.pallas import tpu as pltpu

F32 = jnp.float32
BF16 = jnp.bfloat16

D_MODEL = 1024
N_META = 16
GDN_HEADS = 4
GDN_D = 128
GDN_CHUNK = 64
GDN_CONV = 4
GDN_ROWS = 128
SB_HEADS = 8
SB_DH = 64
SB_BLOCK = 128
D_FF = 2816
FFN_CONV = 3
NORM_EPS = 1e-6
L2_EPS = 1e-6
N_DEV = 8

PAD_ROWS = SB_BLOCK - N_META
ROW0 = SB_BLOCK
SB_SPAN = 512
SB_DEAD = -104.0
SB_SUB = 256
LP_ALIGN = 256

C_QKV = 3 * GDN_HEADS * GDN_D
C_Z = GDN_HEADS * GDN_D
C_SB = 3 * SB_HEADS * SB_DH
C_AB = 256
OFF_Z = C_QKV
OFF_SB = OFF_Z + C_Z
OFF_AB = OFF_SB + C_SB
D_INP = OFF_AB + C_AB
D_IN = C_QKV + 2 * GDN_HEADS + C_Z + C_SB

ADAM_LR = 0.001
ADAM_B1 = 0.9
ADAM_B2 = 0.999
ADAM_EPS = 1e-08
ADAM_WD = 0.01
ADAM_STEP = 10

VMEM_LIMIT = 56 * 1024 * 1024
MESH = pl.DeviceIdType.MESH


def _cp(sem=None):
    kw = dict(vmem_limit_bytes=VMEM_LIMIT)
    if sem is not None:
        kw["dimension_semantics"] = sem
    return pltpu.CompilerParams(**kw)


def _tile(n, cap, unit=128):
    best = None
    t = unit
    while t <= min(n, cap):
        if n % t == 0:
            best = t
        t += unit
    assert best is not None, (n, cap, unit)
    return best


def _dot(a, b):
    return jnp.dot(a, b, preferred_element_type=F32)


def _dot_nt(a, b):
    return lax.dot_general(a, b, (((1,), (1,)), ((), ())), preferred_element_type=F32)


def _dot_tn(a, b):
    return lax.dot_general(a, b, (((0,), (0,)), ((), ())), preferred_element_type=F32)


def _split(x):
    hi = x.astype(BF16)
    lo = (x - hi.astype(F32)).astype(BF16)
    return hi, lo


def _dot3(a, b, f=_dot):
    ah, al = _split(a)
    bh, bl = _split(b)
    return f(ah, bh) + (f(ah, bl) + f(al, bh))


def _dot_exact_l(m_bf16, x, f=_dot):
    xh, xl = _split(x)
    return f(m_bf16, xh) + f(m_bf16, xl)


def _dot_exact_r(x, m_bf16, f=_dot):
    xh, xl = _split(x)
    return f(xh, m_bf16) + f(xl, m_bf16)


def _iota2(shape, dim):
    return lax.broadcasted_iota(jnp.int32, shape, dim)


def _sigmoid(x):
    return 1.0 / (1.0 + jnp.exp(-x))


def _softplus(x):
    return jnp.maximum(x, 0.0) + jnp.log(1.0 + jnp.exp(-jnp.abs(x)))


def _colsum(x):
    return jnp.sum(x, axis=0, keepdims=True)


def _rowsum(x):
    return jnp.sum(x, axis=-1, keepdims=True)


def _mm(a, b, out_dtype, name):
    M, K = a.shape
    K2, N = b.shape
    assert K == K2
    tm = _tile(M, 768)
    tn = _tile(N, max(128, (6 * 1024 * 1024) // (2 * K)))

    def body(a_ref, b_ref, o_ref):
        o_ref[...] = _dot(a_ref[...].astype(BF16), b_ref[...].astype(BF16)).astype(o_ref.dtype)

    return pl.pallas_call(
        body, name=name, grid=(N // tn, M // tm),
        in_specs=[pl.BlockSpec((tm, K), lambda j, i: (i, 0)), pl.BlockSpec((K, tn), lambda j, i: (0, j))],
        out_specs=pl.BlockSpec((tm, tn), lambda j, i: (i, j)),
        out_shape=jax.ShapeDtypeStruct((M, N), out_dtype),
        compiler_params=_cp(("parallel", "parallel")),
    )(a, b)


def _mm_tn(a, b, name):
    M, K = a.shape
    M2, N = b.shape
    assert M == M2
    tm = _tile(M, 768)
    tk = _tile(K, 1408)
    tn = _tile(N, 1408)

    def body(a_ref, b_ref, o_ref):
        @pl.when(pl.program_id(2) == 0)
        def _():
            o_ref[...] = jnp.zeros_like(o_ref)
        o_ref[...] += _dot_tn(a_ref[...].astype(BF16), b_ref[...].astype(BF16))

    return pl.pallas_call(
        body, name=name, grid=(K // tk, N // tn, M // tm),
        in_specs=[pl.BlockSpec((tm, tk), lambda i, j, m: (m, i)), pl.BlockSpec((tm, tn), lambda i, j, m: (m, j))],
        out_specs=pl.BlockSpec((tk, tn), lambda i, j, m: (i, j)),
        out_shape=jax.ShapeDtypeStruct((K, N), F32),
        compiler_params=_cp(("parallel", "parallel", "arbitrary")),
    )(a, b)


def _rms(x):
    return lax.rsqrt(jnp.mean(x * x, axis=-1, keepdims=True) + NORM_EPS)


def _rms_bwd(x, w, dy):
    r = _rms(x)
    n = x * r
    dyw = dy * w
    dx = r * (dyw - n * jnp.mean(dyw * n, axis=-1, keepdims=True))
    return dx, dy * n


def _prenorm_fwd(h0, w):
    LP, D = h0.shape
    T = _tile(LP, 512)

    def body(h_ref, w_ref, u_ref):
        h = h_ref[...]
        u_ref[...] = (h * _rms(h) * w_ref[...]).astype(BF16)

    return pl.pallas_call(
        body, name="prenorm_fwd", grid=(LP // T,),
        in_specs=[pl.BlockSpec((T, D), lambda i: (i, 0)), pl.BlockSpec((1, D), lambda i: (0, 0))],
        out_specs=pl.BlockSpec((T, D), lambda i: (i, 0)),
        out_shape=jax.ShapeDtypeStruct((LP, D), BF16),
        compiler_params=_cp(("parallel",)),
    )(h0, w)


def _prenorm_bwd(h0, w, du, dh1):
    LP, D = h0.shape
    T = _tile(LP, 512)

    def body(h_ref, w_ref, du_ref, dh1_ref, dh0_ref, dw_ref):
        @pl.when(pl.program_id(0) == 0)
        def _():
            dw_ref[...] = jnp.zeros_like(dw_ref)
        dx, dwn = _rms_bwd(h_ref[...], w_ref[...], du_ref[...])
        dh0_ref[...] = dh1_ref[...] + dx
        dw_ref[...] += _colsum(dwn)

    row = pl.BlockSpec((T, D), lambda i: (i, 0))
    vec = pl.BlockSpec((1, D), lambda i: (0, 0))
    return pl.pallas_call(
        body, name="prenorm_bwd", grid=(LP // T,),
        in_specs=[row, vec, row, row], out_specs=[row, vec],
        out_shape=[jax.ShapeDtypeStruct((LP, D), F32), jax.ShapeDtypeStruct((1, D), F32)],
        compiler_params=_cp(("arbitrary",)),
    )(h0, w, du, dh1)


def _causal_taps(ext, w_ref, width, start, rows):
    y = w_ref[width - 1:width, :] * ext[start:start + rows]
    for j in range(width - 1):
        y = y + w_ref[j:j + 1, :] * pltpu.roll(ext, width - 1 - j, 0)[start:start + rows]
    return y


def _shifted_rows(ext, shift, start, rows):
    return ext[start:start + rows] if shift == 0 else pltpu.roll(ext, shift, 0)[start:start + rows]


def _anticausal_taps(dy_ext, w_ref, width, rows):
    n = dy_ext.shape[0]
    dx = w_ref[width - 1:width, :] * dy_ext[0:rows]
    for j in range(width - 1):
        dx = dx + w_ref[j:j + 1, :] * pltpu.roll(dy_ext, n - (width - 1 - j), 0)[0:rows]
    return dx


def _gdn_gate_consts(alog_ref, dtb_ref, h):
    a_coef = -jnp.exp(alog_ref[0:1, h:h + 1])
    return a_coef, dtb_ref[0:1, h:h + 1]


def _gdn_pre_fwd(proj, conv_w, a_log, dt_bias):
    LP = proj.shape[0]
    T = _tile(LP, 256)
    C = C_QKV
    H = GDN_HEADS

    def body(x_ref, halo_ref, ab_ref, cw_ref, alog_ref, dtb_ref, q_ref, k_ref, v_ref, beta_ref, g_ref):
        i = pl.program_id(0)
        ext = jnp.concatenate([jnp.where(i > 0, halo_ref[...], 0.0), x_ref[...]], axis=0)
        y = _causal_taps(ext, cw_ref, GDN_CONV, 8, T)
        c = y * _sigmoid(y)
        for h in range(H):
            sl = slice(h * GDN_D, (h + 1) * GDN_D)
            cq = c[:, sl]
            q_ref[:, sl] = cq * lax.rsqrt(_rowsum(cq * cq) + L2_EPS) * (GDN_D ** -0.5)
            ck = c[:, 512 + h * GDN_D:512 + (h + 1) * GDN_D]
            k_ref[:, sl] = ck * lax.rsqrt(_rowsum(ck * ck) + L2_EPS)
        v_ref[...] = c[:, 1024:]
        ab = ab_ref[...]
        valid = (i * T + _iota2((T, 1), 0)) >= PAD_ROWS
        for h in range(H):
            sl = slice(h * GDN_D, (h + 1) * GDN_D)
            a_coef, dtb = _gdn_gate_consts(alog_ref, dtb_ref, h)
            g = jnp.where(valid, a_coef * _softplus(ab[:, h:h + 1] + dtb), 0.0)
            beta = jnp.where(valid, _sigmoid(ab[:, H + h:H + h + 1]), 0.0)
            g_ref[:, sl] = jnp.broadcast_to(g, (T, GDN_D))
            beta_ref[:, sl] = jnp.broadcast_to(beta, (T, GDN_D))

    t8 = T // 8
    row512 = pl.BlockSpec((T, 512), lambda i: (i, 0))
    small = lambda r, c: pl.BlockSpec((r, c), lambda i: (0, 0))
    out = jax.ShapeDtypeStruct((LP, 512), F32)
    return pl.pallas_call(
        body, name="gdn_pre_fwd", grid=(LP // T,),
        in_specs=[pl.BlockSpec((T, C), lambda i: (i, 0)),
                  pl.BlockSpec((8, C), lambda i: (jnp.maximum(i * t8 - 1, 0), 0)),
                  pl.BlockSpec((T, C_AB), lambda i: (i, OFF_AB // C_AB)),
                  small(GDN_CONV, C), small(1, H), small(1, H)],
        out_specs=[row512] * 5, out_shape=[out] * 5,
        compiler_params=_cp(("parallel",)),
    )(proj, proj, proj, conv_w, a_log, dt_bias)


def _gdn_pre_bwd(proj, conv_w, a_log, dt_bias, dq, dk, dv, dbeta, dg):
    LP = proj.shape[0]
    T = _tile(LP, 256)
    C = C_QKV
    H = GDN_HEADS
    TE = T + 8
    nt = LP // T

    def body(x_ref, xp_ref, xn_ref, ab_ref, cw_ref, alog_ref, dtb_ref,
             dq_ref, dqn_ref, dk_ref, dkn_ref, dv_ref, dvn_ref, dbeta_ref, dg_ref,
             dx_ref, dab_ref, dcw_ref, dsc_ref, dys):
        i = pl.program_id(0)

        @pl.when(i == 0)
        def _():
            dcw_ref[...] = jnp.zeros_like(dcw_ref)
            dsc_ref[...] = jnp.zeros_like(dsc_ref)

        last = i == nt - 1
        ext = jnp.concatenate([jnp.where(i > 0, xp_ref[...], 0.0), x_ref[...], jnp.where(last, 0.0, xn_ref[...])],
                              axis=0)
        y = _causal_taps(ext, cw_ref, GDN_CONV, 8, TE)
        sg = _sigmoid(y)
        c = y * sg
        nxt = lambda a_ref, b_ref: jnp.concatenate([a_ref[...], jnp.where(last, 0.0, b_ref[...])], axis=0)
        dqn = nxt(dq_ref, dqn_ref)
        dkn = nxt(dk_ref, dkn_ref)
        dvv = nxt(dv_ref, dvn_ref)
        for h in range(H):
            sl = slice(h * GDN_D, (h + 1) * GDN_D)
            cq = c[:, sl]
            rq = lax.rsqrt(_rowsum(cq * cq) + L2_EPS)
            nq = cq * rq
            dqh = dqn[:, sl]
            dys[:, sl] = (GDN_D ** -0.5) * rq * (dqh - nq * _rowsum(dqh * nq))
            sk = slice(512 + h * GDN_D, 512 + (h + 1) * GDN_D)
            ck = c[:, sk]
            rk = lax.rsqrt(_rowsum(ck * ck) + L2_EPS)
            nk = ck * rk
            dkh = dkn[:, sl]
            dys[:, sk] = rk * (dkh - nk * _rowsum(dkh * nk))
        dys[:, 1024:] = dvv
        dy = dys[...] * (sg * (1.0 + y * (1.0 - sg)))
        for j in range(GDN_CONV):
            dcw_ref[j:j + 1, :] += _colsum(dy[0:T, :] * _shifted_rows(ext, GDN_CONV - 1 - j, 8, T))
        dx_ref[...] = _anticausal_taps(dy, cw_ref, GDN_CONV, T)
        ab = ab_ref[...]
        valid = (i * T + _iota2((T, 1), 0)) >= PAD_ROWS
        lane = _iota2((T, C_AB), 1)
        lane1 = _iota2((1, 128), 1)
        dab = jnp.zeros((T, C_AB), F32)
        dsc_a = jnp.zeros((1, 128), F32)
        dsc_d = jnp.zeros((1, 128), F32)
        for h in range(H):
            a_coef, dtb = _gdn_gate_consts(alog_ref, dtb_ref, h)
            pre = ab[:, h:h + 1] + dtb
            dgh = jnp.where(valid, dg_ref[:, h * GDN_D:h * GDN_D + 1], 0.0)
            da = dgh * a_coef * _sigmoid(pre)
            beta = _sigmoid(ab[:, H + h:H + h + 1])
            db = jnp.where(valid, dbeta_ref[:, h * GDN_D:h * GDN_D + 1], 0.0) * beta * (1.0 - beta)
            dab = dab + jnp.where(lane == h, da, 0.0) + jnp.where(lane == H + h, db, 0.0)
            dsc_a = dsc_a + jnp.where(lane1 == h, _colsum(dgh * a_coef * _softplus(pre)), 0.0)
            dsc_d = dsc_d + jnp.where(lane1 == h, _colsum(da), 0.0)
        dab_ref[...] = dab
        dsc_ref[0:1, :] += dsc_a
        dsc_ref[1:2, :] += dsc_d

    t8 = T // 8
    nb8 = LP // 8
    prev8 = lambda w: pl.BlockSpec((8, w), lambda i: (jnp.maximum(i * t8 - 1, 0), 0))
    next8 = lambda w: pl.BlockSpec((8, w), lambda i: (jnp.minimum((i + 1) * t8, nb8 - 1), 0))
    row = lambda w: pl.BlockSpec((T, w), lambda i: (i, 0))
    small = lambda r, c: pl.BlockSpec((r, c), lambda i: (0, 0))
    return pl.pallas_call(
        body, name="gdn_pre_bwd", grid=(nt,),
        in_specs=[row(C), prev8(C), next8(C), pl.BlockSpec((T, C_AB), lambda i: (i, OFF_AB // C_AB)),
                  small(GDN_CONV, C), small(1, H), small(1, H),
                  row(512), next8(512), row(512), next8(512), row(512), next8(512), row(512), row(512)],
        out_specs=[row(C), row(C_AB), small(GDN_CONV, C), small(2, 128)],
        out_shape=[jax.ShapeDtypeStruct((LP, C), F32), jax.ShapeDtypeStruct((LP, C_AB), F32),
                   jax.ShapeDtypeStruct((GDN_CONV, C), F32), jax.ShapeDtypeStruct((2, 128), F32)],
        scratch_shapes=[pltpu.VMEM((TE, C), F32)],
        compiler_params=_cp(("arbitrary",)),
    )(proj, proj, proj, proj, conv_w, a_log, dt_bias, dq, dq, dk, dk, dv, dv, dbeta, dg)


def _tri_masks():
    r = _iota2((GDN_CHUNK, GDN_CHUNK), 0)
    c = _iota2((GDN_CHUNK, GDN_CHUNK), 1)
    return r >= c, r > c


def _gdn_chunk_common(q, k, v, beta, gb):
    incl, strict = _tri_masks()
    l_incl = incl.astype(BF16)
    gd = _dot_exact_l(l_incl, jnp.where(strict, gb[:, :GDN_CHUNK], 0.0))
    gc = _dot_exact_l(l_incl, gb)
    decay = jnp.where(incl, jnp.exp(jnp.where(incl, gd, 0.0)), 0.0)
    exp_g = jnp.exp(gc)
    g_last = gc[GDN_CHUNK - 1:GDN_CHUNK, :]
    kd_fac = jnp.exp(g_last - gc)
    gl = jnp.exp(g_last)
    kb = k * beta
    kk = _dot3(kb, k, _dot_nt)
    return dict(incl=incl, strict=strict, decay=decay, exp_g=exp_g, kd_fac=kd_fac, gl=gl, kb=kb, kk=kk,
                vb=v * beta, kbg=kb * exp_g)


def _interleave(gens):
    gens = list(gens)
    while gens:
        alive = []
        for g in gens:
            try:
                next(g)
                alive.append(g)
            except StopIteration:
                pass
        gens = alive


def _gdn_chunk_fwd(qn, kn, v, beta_b, g_b):
    LP = qn.shape[0]
    R = GDN_ROWS
    H = GDN_HEADS
    CH = GDN_CHUNK

    def body(q_ref, k_ref, v_ref, b_ref, g_ref, u_ref, w_ref, qd_ref, kd_ref, qk_ref, t_ref, gl_ref):
        def item(cc, h):
            rs = slice(cc * CH, (cc + 1) * CH)
            sl = slice(h * GDN_D, (h + 1) * GDN_D)
            s64 = slice(h * CH, (h + 1) * CH)
            q, k = q_ref[rs, sl], k_ref[rs, sl]
            m = _gdn_chunk_common(q, k, v_ref[rs, sl], b_ref[rs, sl], g_ref[rs, sl])
            qk_raw = _dot3(q, k, _dot_nt)
            yield
            a = jnp.where(m["strict"], m["kk"] * m["decay"], 0.0)
            eye = (_iota2((CH, CH), 0) == _iota2((CH, CH), 1)).astype(F32)
            t = eye - a
            p = _dot3(a, a)
            yield
            for _ in range(4):
                t = t + _dot3(t, p)
                p = _dot3(p, p)
                yield
            t = t + _dot3(t, p)
            yield
            u_ref[rs, sl] = _dot3(t, m["vb"])
            w_ref[rs, sl] = _dot3(t, m["kbg"])
            qk_ref[rs, s64] = qk_raw * m["decay"]
            t_ref[rs, s64] = t
            qd_ref[rs, sl] = q * m["exp_g"]
            kd_ref[rs, sl] = k * m["kd_fac"]
            gl_ref[cc * 8:(cc + 1) * 8, sl] = jnp.broadcast_to(m["gl"], (8, GDN_D))

        _interleave(item(cc, h) for cc in range(R // CH) for h in range(H))

    row = lambda w: pl.BlockSpec((R, w), lambda i: (i, 0))
    o512 = jax.ShapeDtypeStruct((LP, 512), F32)
    o256 = jax.ShapeDtypeStruct((LP, 256), F32)
    return pl.pallas_call(
        body, name="gdn_chunk_fwd", grid=(LP // R,),
        in_specs=[row(512)] * 5,
        out_specs=[row(512)] * 4 + [row(256)] * 2 + [pl.BlockSpec((R // 8, 512), lambda i: (i, 0))],
        out_shape=[o512] * 4 + [o256] * 2 + [jax.ShapeDtypeStruct((LP // 8, 512), F32)],
        compiler_params=_cp(("parallel",)),
    )(qn, kn, v, beta_b, g_b)


def _gdn_chunk_bwd(qn, kn, v, beta_b, g_b, t_all, du, dw, dqd, dkd, dqk, dgl):
    LP = qn.shape[0]
    R = GDN_ROWS
    H = GDN_HEADS
    CH = GDN_CHUNK

    def body(q_ref, k_ref, v_ref, b_ref, g_ref, t_ref, du_ref, dw_ref, dqd_ref, dkd_ref, dqk_ref, dgl_ref,
             dq_ref, dk_ref, dv_ref, db_ref, dg_ref):
        ones = jnp.ones((CH, GDN_D), BF16)

        def item(cc, h):
            rs = slice(cc * CH, (cc + 1) * CH)
            sl = slice(h * GDN_D, (h + 1) * GDN_D)
            s64 = slice(h * CH, (h + 1) * CH)
            q, k, vv, beta = q_ref[rs, sl], k_ref[rs, sl], v_ref[rs, sl], b_ref[rs, sl]
            m = _gdn_chunk_common(q, k, vv, beta, g_ref[rs, sl])
            incl, strict, decay = m["incl"], m["strict"], m["decay"]
            t = t_ref[rs, s64]
            du_, dw_ = du_ref[rs, sl], dw_ref[rs, sl]
            dqd_, dkd_ = dqd_ref[rs, sl], dkd_ref[rs, sl]
            d_t = _dot3(du_, m["vb"], _dot_nt) + _dot3(dw_, m["kbg"], _dot_nt)
            dvb = _dot3(t, du_, _dot_tn)
            dkbg = _dot3(t, dw_, _dot_tn)
            qk_raw = _dot3(q, k, _dot_nt)
            yield
            x1 = _dot3(d_t, t, _dot_nt)
            dkb = dkbg * m["exp_g"]
            d_gi = _rowsum(dkbg * m["kbg"])
            yield
            d_a = jnp.where(strict, -_dot3(t, x1, _dot_tn), 0.0)
            yield
            d_kk = d_a * decay
            dqk_m = jnp.where(incl, dqk_ref[rs, s64], 0.0)
            dqk_raw = dqk_m * decay
            mm = (d_a * m["kk"] + dqk_m * qk_raw) * decay
            dkb = dkb + _dot3(d_kk, k)
            dk_ = _dot3(d_kk, m["kb"], _dot_tn) + _dot3(dqk_raw, q, _dot_tn)
            dq_ = _dot3(dqk_raw, k) + dqd_ * m["exp_g"]
            d_gi = d_gi + (_dot_exact_r(mm, ones) - _dot_exact_r(mm, ones, _dot_tn))
            yield
            d_gi = d_gi + _rowsum(dqd_ * q * m["exp_g"])
            e = _rowsum(dkd_ * k * m["kd_fac"])
            d_gi = d_gi - e
            d_glast = _colsum(jnp.broadcast_to(e, (CH, GDN_D))) + dgl_ref[cc * 8:cc * 8 + 1, sl] * m["gl"]
            dk_ = dk_ + dkd_ * m["kd_fac"] + dkb * beta
            d_gi = d_gi + jnp.where(_iota2((CH, GDN_D), 0) == CH - 1, d_glast, 0.0)
            u_incl = (_iota2((CH, CH), 1) >= _iota2((CH, CH), 0)).astype(BF16)
            dq_ref[rs, sl] = dq_
            dk_ref[rs, sl] = dk_
            dv_ref[rs, sl] = dvb * beta
            db_ref[rs, sl] = jnp.broadcast_to(_rowsum(dvb * vv) + _rowsum(dkb * k), (CH, GDN_D))
            dg_ref[rs, sl] = _dot_exact_l(u_incl, d_gi)

        _interleave(item(cc, h) for cc in range(R // CH) for h in range(H))

    row = lambda w: pl.BlockSpec((R, w), lambda i: (i, 0))
    o512 = jax.ShapeDtypeStruct((LP, 512), F32)
    gl_spec = pl.BlockSpec((R // 8, 512), lambda i: (i, 0))
    return pl.pallas_call(
        body, name="gdn_chunk_bwd", grid=(LP // R,),
        in_specs=[row(512)] * 5 + [row(256)] + [row(512)] * 4 + [row(256), gl_spec],
        out_specs=[row(512)] * 5, out_shape=[o512] * 5,
        compiler_params=_cp(("parallel",)),
    )(qn, kn, v, beta_b, g_b, t_all, du, dw, dqd, dkd, dqk, dgl)


def _gdn_scan_fwd(u, w, qd, kd, qk, gl):
    LP = u.shape[0]
    CH = GDN_CHUNK
    N = LP // CH
    H = GDN_HEADS

    def body(u_ref, w_ref, qd_ref, kd_ref, qk_ref, gl_ref, o_ref, ssave_ref, s_sc):
        @pl.when(pl.program_id(0) == 0)
        def _():
            s_sc[...] = jnp.zeros_like(s_sc)
        ssave_ref[...] = s_sc[...]

        def item(h):
            sl = slice(h * GDN_D, (h + 1) * GDN_D)
            s = s_sc[:, sl]
            v_new = u_ref[:, sl] - _dot3(w_ref[:, sl], s)
            o_s = _dot3(qd_ref[:, sl], s)
            yield
            o_ref[:, sl] = o_s + _dot3(qk_ref[:, h * CH:(h + 1) * CH], v_new)
            s_sc[:, sl] = s * gl_ref[0:1, sl] + _dot3(kd_ref[:, sl], v_new, _dot_tn)

        _interleave(item(h) for h in range(H))

    row = lambda w_: pl.BlockSpec((CH, w_), lambda n: (n, 0))
    return pl.pallas_call(
        body, name="gdn_scan_fwd", grid=(N,),
        in_specs=[row(512)] * 4 + [row(256), pl.BlockSpec((8, 512), lambda n: (n, 0))],
        out_specs=[row(512), pl.BlockSpec((GDN_D, 512), lambda n: (n, 0))],
        out_shape=[jax.ShapeDtypeStruct((LP, 512), F32), jax.ShapeDtypeStruct((N * GDN_D, 512), F32)],
        scratch_shapes=[pltpu.VMEM((GDN_D, 512), F32)],
        compiler_params=_cp(("arbitrary",)),
    )(u, w, qd, kd, qk, gl)


def _gdn_scan_bwd(u, w, qd, kd, qk, gl, ssave, do):
    LP = u.shape[0]
    CH = GDN_CHUNK
    N = LP // CH
    H = GDN_HEADS

    def body(u_ref, w_ref, qd_ref, kd_ref, qk_ref, gl_ref, s_ref, do_ref,
             du_ref, dw_ref, dqd_ref, dkd_ref, dqk_ref, dgl_ref, ds_sc):
        @pl.when(pl.program_id(0) == 0)
        def _():
            ds_sc[...] = jnp.zeros_like(ds_sc)
        def item(h):
            sl = slice(h * GDN_D, (h + 1) * GDN_D)
            s64 = slice(h * CH, (h + 1) * CH)
            s = s_ref[:, sl]
            ds = ds_sc[:, sl]
            do_ = do_ref[:, sl]
            w_, qd_, kd_, qk_ = w_ref[:, sl], qd_ref[:, sl], kd_ref[:, sl], qk_ref[:, s64]
            v_new = u_ref[:, sl] - _dot3(w_, s)
            d_vnew = _dot3(qk_, do_, _dot_tn) + _dot3(kd_, ds)
            dqd_ref[:, sl] = _dot3(do_, s, _dot_nt)
            ds_new = ds * gl_ref[0:1, sl] + _dot3(qd_, do_, _dot_tn)
            dgl_ref[:, sl] = jnp.broadcast_to(jnp.sum(_colsum(ds * s), axis=-1, keepdims=True), (8, GDN_D))
            yield
            du_ref[:, sl] = d_vnew
            dw_ref[:, sl] = -_dot3(d_vnew, s, _dot_nt)
            dkd_ref[:, sl] = _dot3(v_new, ds, _dot_nt)
            dqk_ref[:, s64] = _dot3(do_, v_new, _dot_nt)
            ds_sc[:, sl] = ds_new - _dot3(w_, d_vnew, _dot_tn)

        _interleave(item(h) for h in range(H))

    rev = lambda w_: pl.BlockSpec((CH, w_), lambda n: (N - 1 - n, 0))
    rev8 = pl.BlockSpec((8, 512), lambda n: (N - 1 - n, 0))
    o512 = jax.ShapeDtypeStruct((LP, 512), F32)
    return pl.pallas_call(
        body, name="gdn_scan_bwd", grid=(N,),
        in_specs=[rev(512)] * 4 + [rev(256), rev8, pl.BlockSpec((GDN_D, 512), lambda n: (N - 1 - n, 0)), rev(512)],
        out_specs=[rev(512)] * 4 + [rev(256), rev8],
        out_shape=[o512] * 4 + [jax.ShapeDtypeStruct((LP, 256), F32), jax.ShapeDtypeStruct((LP // 8, 512), F32)],
        scratch_shapes=[pltpu.VMEM((GDN_D, 512), F32)],
        compiler_params=_cp(("arbitrary",)),
    )(u, w, qd, kd, qk, gl, ssave, do)


def _sb_scores(qh, kblk, mask):
    z = _dot_nt(qh, kblk)
    e = jnp.exp(-jnp.abs(z))
    sp = jnp.maximum(z, 0.0) + jnp.log(1.0 + e)
    return z, e, jnp.where(mask, -sp, 0.0), z - sp


def _sb_fwd(proj):
    LP = proj.shape[0]
    B = SB_BLOCK
    W = min(SB_SPAN, LP)
    SUB = SB_SUB
    nq = LP // B
    nsub = W // SUB
    scale = SB_DH ** -0.5
    qcol, kcol, vcol = OFF_SB // B, (OFF_SB + 512) // B, (OFF_SB + 1024) // B

    def body(q_ref, k_ref, v_ref, o_ref, c_ref, n_ref):
        i = pl.program_id(1)
        lane = _iota2((B, B), 1)
        head_a = lane < SB_DH
        qs = q_ref[...] * scale
        qh = [jnp.where(head_a, qs, 0.0).astype(BF16), jnp.where(head_a, 0.0, qs).astype(BF16)]
        u_strict = (_iota2((SUB, SUB), 0) > _iota2((SUB, SUB), 1)).astype(BF16)
        qpos = i * B + _iota2((B, W), 0)
        hi0 = (i + 1) * B
        nspan = (hi0 + W - 1) // W

        def live(st):
            return (st[0] < nspan) & (st[1] > 0)

        def span(st):
            r, carry = st[0], st[2:]
            hi = hi0 - r * W
            k0 = pl.multiple_of(jnp.maximum(hi - W, 0), B)
            kblk = k_ref[pl.ds(k0, W), :].astype(BF16)
            vblk = v_ref[pl.ds(k0, W), :].astype(BF16)
            kpos = k0 + _iota2((B, W), 1)
            mask = (kpos < qpos) & (kpos >= PAD_ROWS) & (kpos < hi)
            new = [None] * 4

            def head(h):
                o_acc, c = carry[2 * h], carry[2 * h + 1]
                z, e, l1m, lsg = _sb_scores(qh[h], kblk, mask)
                yield
                subs = [slice(b * SUB, (b + 1) * SUB) for b in range(nsub)]
                suf = [_dot_exact_r(l1m[:, bs], u_strict) for bs in subs]
                yield
                parts = [None] * nsub
                for b in reversed(range(nsub)):
                    parts[b] = jnp.where(mask[:, subs[b]], jnp.exp(lsg[:, subs[b]] + suf[b] + c), 0.0)
                    c = c + _rowsum(l1m[:, subs[b]])
                att = jnp.concatenate(parts, axis=1).astype(BF16)
                new[2 * h], new[2 * h + 1] = o_acc + _dot(att, vblk), c

            _interleave(head(h) for h in range(2))
            more = (jnp.maximum(jnp.max(new[1]), jnp.max(new[3])) > SB_DEAD).astype(jnp.int32)
            return (r + 1, more, *new)

        zero_o = jnp.zeros((B, B), F32)
        zero_c = jnp.zeros((B, 1), F32)
        nrun, _, o_a, c_a, o_b, c_b = lax.while_loop(
            live, span, (jnp.int32(0), jnp.int32(1), zero_o, zero_c, zero_o, zero_c))
        o_ref[...] = jnp.where(head_a, o_a, o_b)
        c_ref[...] = jnp.where(head_a, c_a, c_b)
        n_ref[pl.program_id(0), i] = nrun

    blk = pl.BlockSpec((B, B), lambda p, i: (i, p))
    out = jax.ShapeDtypeStruct((LP, 512), F32)
    return pl.pallas_call(
        body, name="sb_fwd", grid=(SB_HEADS // 2, nq),
        in_specs=[pl.BlockSpec((B, B), lambda p, i: (i, qcol + p)),
                  pl.BlockSpec((LP, B), lambda p, i: (0, kcol + p)),
                  pl.BlockSpec((LP, B), lambda p, i: (0, vcol + p))],
        out_specs=[blk, blk, pl.BlockSpec(memory_space=pltpu.SMEM)],
        out_shape=[out, out, jax.ShapeDtypeStruct((SB_HEADS // 2, nq), jnp.int32)],
        compiler_params=_cp(("arbitrary", "arbitrary")),
    )(proj, proj, proj)


def _sb_bwd(proj, ctot, nrun_all, do):
    LP = proj.shape[0]
    B = SB_BLOCK
    W = min(SB_SPAN, LP)
    SUB = SB_SUB
    nq = LP // B
    nsub = W // SUB
    scale = SB_DH ** -0.5
    qcol, kcol, vcol = OFF_SB // B, (OFF_SB + 512) // B, (OFF_SB + 1024) // B

    def body(n_ref, q_ref, k_ref, v_ref, c_ref, do_ref, dq_ref, dk_ref, dv_ref):
        i = pl.program_id(1)

        @pl.when(i == 0)
        def _():
            dk_ref[...] = jnp.zeros_like(dk_ref)
            dv_ref[...] = jnp.zeros_like(dv_ref)

        lane = _iota2((B, B), 1)
        head_a = lane < SB_DH
        qs = q_ref[...] * scale
        qh = [jnp.where(head_a, qs, 0.0).astype(BF16), jnp.where(head_a, 0.0, qs).astype(BF16)]
        dof = do_ref[...]
        doh = [jnp.where(head_a, dof, 0.0).astype(BF16), jnp.where(head_a, 0.0, dof).astype(BF16)]
        cfull = c_ref[...]
        ctot_h = [cfull[:, 0:1], cfull[:, SB_DH:SB_DH + 1]]
        sub_r, sub_c = _iota2((SUB, SUB), 0), _iota2((SUB, SUB), 1)
        u_strict = (sub_r > sub_c).astype(BF16)
        l_strict = (sub_r < sub_c).astype(BF16)
        qpos = i * B + _iota2((B, W), 0)
        hi0 = (i + 1) * B
        nrun = n_ref[pl.program_id(0), i]

        def span(t, carry):
            r = nrun - 1 - t
            hi = hi0 - r * W
            k0 = pl.multiple_of(jnp.maximum(hi - W, 0), B)
            kblk = k_ref[pl.ds(k0, W), :].astype(BF16)
            vblk = v_ref[pl.ds(k0, W), :].astype(BF16)
            kpos = k0 + _iota2((B, W), 1)
            mask = (kpos < qpos) & (kpos >= PAD_ROWS) & (kpos < hi)
            new = [None] * 6
            dk_add, dv_add = [None, None], [None, None]
            subs = [slice(b * SUB, (b + 1) * SUB) for b in range(nsub)]

            def head(h):
                dq_acc, pre, ecar = carry[3 * h], carry[3 * h + 1], carry[3 * h + 2]
                z, e, l1m, lsg = _sb_scores(qh[h], kblk, mask)
                d_att = _dot_nt(doh[h], vblk)
                yield
                sig = jnp.where(z >= 0.0, 1.0, e) / (1.0 + e)
                suf = [_dot_exact_r(l1m[:, bs], u_strict) for bs in subs]
                yield
                att_parts, p_parts = [None] * nsub, [None] * nsub
                for b, bs in enumerate(subs):
                    pre = pre + _rowsum(l1m[:, bs])
                    att_parts[b] = jnp.where(mask[:, bs], jnp.exp(lsg[:, bs] + suf[b] + (ctot_h[h] - pre)), 0.0)
                    p_parts[b] = att_parts[b] * d_att[:, bs]
                pcum = [_dot_exact_r(p, l_strict) for p in p_parts]
                yield
                dz_parts = [None] * nsub
                for b, bs in enumerate(subs):
                    sg = sig[:, bs]
                    dz_parts[b] = jnp.where(mask[:, bs], p_parts[b] * (1.0 - sg) - sg * (ecar + pcum[b]), 0.0)
                    ecar = ecar + _rowsum(p_parts[b])
                att = jnp.concatenate(att_parts, axis=1).astype(BF16)
                dz = jnp.concatenate(dz_parts, axis=1).astype(BF16)
                new[3 * h:3 * h + 3] = [dq_acc + _dot(dz, kblk), pre, ecar]
                dk_add[h] = _dot_tn(dz, qh[h])
                dv_add[h] = _dot_tn(att, doh[h])

            _interleave(head(h) for h in range(2))
            dk_ref[pl.ds(k0, W), :] += dk_add[0] + dk_add[1]
            dv_ref[pl.ds(k0, W), :] += dv_add[0] + dv_add[1]
            return tuple(new)

        zero_o = jnp.zeros((B, B), F32)
        zero_c = jnp.zeros((B, 1), F32)
        res = lax.fori_loop(0, nrun, span, (zero_o, zero_c, zero_c, zero_o, zero_c, zero_c))
        dq_ref[...] = jnp.where(head_a, res[0], res[3]) * scale

    blk = pl.BlockSpec((B, B), lambda p, i: (i, p))
    col = pl.BlockSpec((LP, B), lambda p, i: (0, p))
    out = jax.ShapeDtypeStruct((LP, 512), F32)
    return pl.pallas_call(
        body, name="sb_bwd", grid=(SB_HEADS // 2, nq),
        in_specs=[pl.BlockSpec(memory_space=pltpu.SMEM),
                  pl.BlockSpec((B, B), lambda p, i: (i, qcol + p)),
                  pl.BlockSpec((LP, B), lambda p, i: (0, kcol + p)),
                  pl.BlockSpec((LP, B), lambda p, i: (0, vcol + p)),
                  blk, blk],
        out_specs=[blk, col, col], out_shape=[out, out, out],
        compiler_params=_cp(("arbitrary", "arbitrary")),
    )(nrun_all, proj, proj, proj, ctot, do)


def _sb_group_mean():
    r = jnp.right_shift(_iota2((512, 512), 0), 6)
    c = jnp.right_shift(_iota2((512, 512), 1), 6)
    return jnp.where(r == c, 1.0 / SB_DH, 0.0).astype(BF16)


def _attn_norm_fwd(og, proj, osb, gnw, snw):
    LP = og.shape[0]
    T = _tile(LP, 256)

    def body(og_ref, z_ref, os_ref, gnw_ref, snw_ref, y_ref):
        valid = (pl.program_id(0) * T + _iota2((T, 1), 0)) >= PAD_ROWS
        z = z_ref[...]
        zg = z * _sigmoid(z)
        for h in range(GDN_HEADS):
            sl = slice(h * GDN_D, (h + 1) * GDN_D)
            o = og_ref[:, sl]
            y = o * _rms(o) * gnw_ref[...] * zg[:, sl]
            y_ref[:, sl] = jnp.where(valid, y, 0.0).astype(BF16)
        o = os_ref[...]
        msq = _dot_exact_r(o * o, _sb_group_mean())
        y = o * lax.rsqrt(msq + NORM_EPS) * snw_ref[...]
        y_ref[:, 512:] = jnp.where(valid, y, 0.0).astype(BF16)

    row = pl.BlockSpec((T, 512), lambda i: (i, 0))
    return pl.pallas_call(
        body, name="attn_norm_fwd", grid=(LP // T,),
        in_specs=[row, pl.BlockSpec((T, 512), lambda i: (i, OFF_Z // 512)), row,
                  pl.BlockSpec((1, GDN_D), lambda i: (0, 0)), pl.BlockSpec((1, 512), lambda i: (0, 0))],
        out_specs=pl.BlockSpec((T, 1024), lambda i: (i, 0)),
        out_shape=jax.ShapeDtypeStruct((LP, 1024), BF16),
        compiler_params=_cp(("parallel",)),
    )(og, proj, osb, gnw, snw)


def _attn_norm_bwd(og, proj, osb, gnw, snw, dy):
    LP = og.shape[0]
    T = _tile(LP, 256)

    def body(og_ref, z_ref, os_ref, gnw_ref, snw_ref, dy_ref, dog_ref, dz_ref, dos_ref, dgw_ref, dsw_ref):
        @pl.when(pl.program_id(0) == 0)
        def _():
            dgw_ref[...] = jnp.zeros_like(dgw_ref)
            dsw_ref[...] = jnp.zeros_like(dsw_ref)
        valid = (pl.program_id(0) * T + _iota2((T, 1), 0)) >= PAD_ROWS
        dy = jnp.where(valid, dy_ref[...], 0.0)
        z = z_ref[...]
        sg = _sigmoid(z)
        zg = z * sg
        dgw = jnp.zeros((1, GDN_D), F32)
        for h in range(GDN_HEADS):
            sl = slice(h * GDN_D, (h + 1) * GDN_D)
            o = og_ref[:, sl]
            dyh = dy[:, sl]
            dx, dwn = _rms_bwd(o, gnw_ref[...], dyh * zg[:, sl])
            dog_ref[:, sl] = dx
            dgw = dgw + _colsum(dwn)
            yn = o * _rms(o) * gnw_ref[...]
            dz_ref[:, sl] = dyh * yn * (sg[:, sl] * (1.0 + z[:, sl] * (1.0 - sg[:, sl])))
        dgw_ref[...] += dgw
        o = os_ref[...]
        gm = _sb_group_mean()
        r = lax.rsqrt(_dot_exact_r(o * o, gm) + NORM_EPS)
        n = o * r
        dys = dy[:, 512:]
        dyw = dys * snw_ref[...]
        dos_ref[...] = r * (dyw - n * _dot_exact_r(dyw * n, gm))
        dsw_ref[...] += _colsum(dys * n)

    row = pl.BlockSpec((T, 512), lambda i: (i, 0))
    gw = pl.BlockSpec((1, GDN_D), lambda i: (0, 0))
    sw = pl.BlockSpec((1, 512), lambda i: (0, 0))
    o512 = jax.ShapeDtypeStruct((LP, 512), F32)
    return pl.pallas_call(
        body, name="attn_norm_bwd", grid=(LP // T,),
        in_specs=[row, pl.BlockSpec((T, 512), lambda i: (i, OFF_Z // 512)), row, gw, sw,
                  pl.BlockSpec((T, 1024), lambda i: (i, 0))],
        out_specs=[row, row, row, gw, sw],
        out_shape=[o512, o512, o512, jax.ShapeDtypeStruct((1, GDN_D), F32), jax.ShapeDtypeStruct((1, 512), F32)],
        compiler_params=_cp(("arbitrary",)),
    )(og, proj, osb, gnw, snw, dy)


def _resid_fwd(h0, mix, w_post, w_pre):
    LP, D = h0.shape
    T = _tile(LP, 512)

    def body(h0_ref, mix_ref, wp_ref, wf_ref, h1_ref, n2_ref):
        mix = mix_ref[...]
        h1 = h0_ref[...] + mix * _rms(mix) * wp_ref[...]
        h1_ref[...] = h1
        n2_ref[...] = (h1 * _rms(h1) * wf_ref[...]).astype(BF16)

    row = pl.BlockSpec((T, D), lambda i: (i, 0))
    vec = pl.BlockSpec((1, D), lambda i: (0, 0))
    return pl.pallas_call(
        body, name="resid_fwd", grid=(LP // T,),
        in_specs=[row, row, vec, vec], out_specs=[row, row],
        out_shape=[jax.ShapeDtypeStruct((LP, D), F32), jax.ShapeDtypeStruct((LP, D), BF16)],
        compiler_params=_cp(("parallel",)),
    )(h0, mix, w_post, w_pre)


def _resid_bwd(h1, mix, w_post, w_pre, dout, dn2):
    LP, D = h1.shape
    T = _tile(LP, 512)

    def body(h1_ref, mix_ref, wp_ref, wf_ref, dout_ref, dn2_ref, dh1_ref, dmix_ref, dwf_ref, dwp_ref):
        @pl.when(pl.program_id(0) == 0)
        def _():
            dwf_ref[...] = jnp.zeros_like(dwf_ref)
            dwp_ref[...] = jnp.zeros_like(dwp_ref)
        dx, dwn = _rms_bwd(h1_ref[...], wf_ref[...], dn2_ref[...])
        dh1 = dout_ref[...] + dx
        dh1_ref[...] = dh1
        dwf_ref[...] += _colsum(dwn)
        dmix, dwn2 = _rms_bwd(mix_ref[...], wp_ref[...], dh1)
        dmix_ref[...] = dmix.astype(BF16)
        dwp_ref[...] += _colsum(dwn2)

    row = pl.BlockSpec((T, D), lambda i: (i, 0))
    vec = pl.BlockSpec((1, D), lambda i: (0, 0))
    v = jax.ShapeDtypeStruct((1, D), F32)
    return pl.pallas_call(
        body, name="resid_bwd", grid=(LP // T,),
        in_specs=[row, row, vec, vec, row, row], out_specs=[row, row, vec, vec],
        out_shape=[jax.ShapeDtypeStruct((LP, D), F32), jax.ShapeDtypeStruct((LP, D), BF16), v, v],
        compiler_params=_cp(("arbitrary",)),
    )(h1, mix, w_post, w_pre, dout, dn2)


GELU_C = 0.7978845608028654
GELU_A = 0.044715


def _gelu_parts(x):
    t = jnp.tanh(GELU_C * (x + GELU_A * x * x * x))
    return 0.5 * x * (1.0 + t), t


def _convglu_fwd(up, conv_w, conv_b):
    LP, C = up.shape
    T = _tile(LP, 128)

    def body(x_ref, halo_ref, cw_ref, cb_ref, act_ref):
        i = pl.program_id(0)
        ext = jnp.concatenate([jnp.where(i > 0, halo_ref[...], 0.0), x_ref[...]], axis=0)
        y = _causal_taps(ext, cw_ref, FFN_CONV, 8, T) + cb_ref[...]
        g, _ = _gelu_parts(y[:, :D_FF])
        act_ref[...] = (g * y[:, D_FF:]).astype(BF16)

    t8 = T // 8
    return pl.pallas_call(
        body, name="convglu_fwd", grid=(LP // T,),
        in_specs=[pl.BlockSpec((T, C), lambda i: (i, 0)),
                  pl.BlockSpec((8, C), lambda i: (jnp.maximum(i * t8 - 1, 0), 0)),
                  pl.BlockSpec((FFN_CONV, C), lambda i: (0, 0)), pl.BlockSpec((1, C), lambda i: (0, 0))],
        out_specs=pl.BlockSpec((T, D_FF), lambda i: (i, 0)),
        out_shape=jax.ShapeDtypeStruct((LP, D_FF), BF16),
        compiler_params=_cp(("parallel",)),
    )(up, up, conv_w, conv_b)


def _convglu_bwd(up, conv_w, conv_b, dact):
    LP, C = up.shape
    T = _tile(LP, 128)
    TE = T + 8
    nt = LP // T

    def body(x_ref, xp_ref, xn_ref, cw_ref, cb_ref, da_ref, dan_ref, dx_ref, dcw_ref, dcb_ref):
        i = pl.program_id(0)

        @pl.when(i == 0)
        def _():
            dcw_ref[...] = jnp.zeros_like(dcw_ref)
            dcb_ref[...] = jnp.zeros_like(dcb_ref)

        last = i == nt - 1
        ext = jnp.concatenate([jnp.where(i > 0, xp_ref[...], 0.0), x_ref[...], jnp.where(last, 0.0, xn_ref[...])],
                              axis=0)
        y = _causal_taps(ext, cw_ref, FFN_CONV, 8, TE) + cb_ref[...]
        gate, val = y[:, :D_FF], y[:, D_FF:]
        g, t = _gelu_parts(gate)
        dg_dx = 0.5 * (1.0 + t) + 0.5 * gate * (1.0 - t * t) * GELU_C * (1.0 + 3.0 * GELU_A * gate * gate)
        da = jnp.concatenate([da_ref[...], jnp.where(last, 0.0, dan_ref[...])], axis=0)
        dy = jnp.concatenate([da * val * dg_dx, da * g], axis=1)
        dy_t = dy[0:T, :]
        dcb_ref[...] += _colsum(dy_t)
        for j in range(FFN_CONV):
            dcw_ref[j:j + 1, :] += _colsum(dy_t * _shifted_rows(ext, FFN_CONV - 1 - j, 8, T))
        dx_ref[...] = _anticausal_taps(dy, cw_ref, FFN_CONV, T).astype(BF16)

    t8 = T // 8
    nb8 = LP // 8
    prev8 = lambda w: pl.BlockSpec((8, w), lambda i: (jnp.maximum(i * t8 - 1, 0), 0))
    next8 = lambda w: pl.BlockSpec((8, w), lambda i: (jnp.minimum((i + 1) * t8, nb8 - 1), 0))
    row = lambda w: pl.BlockSpec((T, w), lambda i: (i, 0))
    small = lambda r: pl.BlockSpec((r, C), lambda i: (0, 0))
    return pl.pallas_call(
        body, name="convglu_bwd", grid=(nt,),
        in_specs=[row(C), prev8(C), next8(C), small(FFN_CONV), small(1), row(D_FF), next8(D_FF)],
        out_specs=[row(C), small(FFN_CONV), small(1)],
        out_shape=[jax.ShapeDtypeStruct((LP, C), BF16), jax.ShapeDtypeStruct((FFN_CONV, C), F32),
                   jax.ShapeDtypeStruct((1, C), F32)],
        compiler_params=_cp(("arbitrary",)),
    )(up, up, up, conv_w, conv_b, dact, dact)


def _final(h1, f, w_post, target, n_real):
    LP, D = h1.shape
    T = _tile(LP, 256)

    def body(h1_ref, f_ref, w_ref, t_ref, loss_ref, dout_ref, df_ref, dw_ref):
        @pl.when(pl.program_id(0) == 0)
        def _():
            loss_ref[...] = jnp.zeros_like(loss_ref)
            dw_ref[...] = jnp.zeros_like(dw_ref)
        rows = pl.program_id(0) * T + _iota2((T, 1), 0)
        real = (rows >= ROW0) & (rows < ROW0 + n_real)
        f = f_ref[...]
        out = h1_ref[...] + f * _rms(f) * w_ref[...]
        err = jnp.where(real, out - t_ref[...], 0.0)
        loss_ref[...] += 0.5 * jnp.sum(_colsum(jnp.mean(err * err, axis=-1, keepdims=True)), axis=-1, keepdims=True)
        dout = err * (1.0 / D)
        dout_ref[...] = dout
        dx, dwn = _rms_bwd(f, w_ref[...], dout)
        df_ref[...] = dx.astype(BF16)
        dw_ref[...] += _colsum(dwn)

    row = pl.BlockSpec((T, D), lambda i: (i, 0))
    vec = pl.BlockSpec((1, D), lambda i: (0, 0))
    return pl.pallas_call(
        body, name="final_loss", grid=(LP // T,),
        in_specs=[row, row, vec, row],
        out_specs=[pl.BlockSpec((1, 128), lambda i: (0, 0)), row, row, vec],
        out_shape=[jax.ShapeDtypeStruct((1, 128), F32), jax.ShapeDtypeStruct((LP, D), F32),
                   jax.ShapeDtypeStruct((LP, D), BF16), jax.ShapeDtypeStruct((1, D), F32)],
        compiler_params=_cp(("arbitrary",)),
    )(h1, f, w_post, target)


ANY_SPEC = pl.BlockSpec(memory_space=pl.ANY)
N_CHIP = 4


def _other_chips(x, y):
    return [(1 - x, y), (x, 1 - y), (1 - x, 1 - y)]


def _gather_direct(arrs, name):
    n = len(arrs)
    npeer = N_DEV - 1

    def body(*refs):
        ins, outs = refs[:n], refs[n:2 * n]
        send_sems, recv_sems, loc_sems = refs[2 * n:]
        x, y, c = lax.axis_index("x"), lax.axis_index("y"), lax.axis_index("c")
        me = 4 * x + 2 * y + c
        copies = []
        for a in range(n):
            for kk in range(1, N_DEV):
                px = 1 - x if kk & 4 else x
                py = 1 - y if kk & 2 else y
                pc = 1 - c if kk & 1 else c
                s = a * npeer + kk - 1
                cp = pltpu.make_async_remote_copy(src_ref=ins[a], dst_ref=outs[a].at[me], send_sem=send_sems.at[s],
                                                  recv_sem=recv_sems.at[s], device_id=(px, py, pc), device_id_type=MESH)
                cp.start()
                copies.append(cp)
            own = pltpu.make_async_copy(ins[a], outs[a].at[me], loc_sems.at[a])
            own.start()
            copies.append(own)
        for cp in copies:
            cp.wait()

    shapes = [jax.ShapeDtypeStruct((N_DEV,) + tuple(a.shape), a.dtype) for a in arrs]
    return pl.pallas_call(
        body, name=name, in_specs=[ANY_SPEC] * n, out_specs=[ANY_SPEC] * n, out_shape=shapes,
        scratch_shapes=[pltpu.SemaphoreType.DMA((n * npeer,)), pltpu.SemaphoreType.DMA((n * npeer,)),
                        pltpu.SemaphoreType.DMA((n,))],
        compiler_params=pltpu.CompilerParams(has_side_effects=True),
    )(*arrs)


def _gather_two_level(arrs, name):
    n = len(arrs)
    K = 7

    def body(*refs):
        ins, outs = refs[:n], refs[n:2 * n]
        send_sems, recv_sems, loc_sems = refs[2 * n:]
        x, y, c = lax.axis_index("x"), lax.axis_index("y"), lax.axis_index("c")
        me = 4 * x + 2 * y + c
        sib = (x, y, 1 - c)
        chips = _other_chips(x, y)

        def cp(a, k, src, slot, to):
            return pltpu.make_async_remote_copy(src_ref=src, dst_ref=outs[a].at[slot], send_sem=send_sems.at[a * K + k],
                                                recv_sem=recv_sems.at[a * K + k], device_id=to, device_id_type=MESH)

        owns, first, passed = [], [], []
        for a in range(n):
            own = pltpu.make_async_copy(ins[a], outs[a].at[me], loc_sems.at[a])
            own.start()
            owns.append(own)
            first.append(cp(a, 0, ins[a], me, sib))
            for j, (px, py) in enumerate(chips):
                first.append(cp(a, 1 + j, ins[a], me, (px, py, c)))
        for f in first:
            f.start()
        for j, (px, py) in enumerate(chips):
            slot = 4 * px + 2 * py + c
            for a in range(n):
                cp(a, 1 + j, ins[a], slot, (px, py, c)).wait_recv()
                fwd = cp(a, 4 + j, outs[a].at[slot], slot, sib)
                fwd.start()
                passed.append(fwd)
        for a in range(n):
            cp(a, 0, ins[a], 4 * x + 2 * y + (1 - c), sib).wait_recv()
            for j, (px, py) in enumerate(chips):
                cp(a, 4 + j, ins[a], 4 * px + 2 * py + (1 - c), sib).wait_recv()
        for f in first + passed:
            f.wait_send()
        for own in owns:
            own.wait()

    shapes = [jax.ShapeDtypeStruct((N_DEV,) + tuple(a.shape), a.dtype) for a in arrs]
    return pl.pallas_call(
        body, name=name, in_specs=[ANY_SPEC] * n, out_specs=[ANY_SPEC] * n, out_shape=shapes,
        scratch_shapes=[pltpu.SemaphoreType.DMA((n * K,)), pltpu.SemaphoreType.DMA((n * K,)),
                        pltpu.SemaphoreType.DMA((n,))],
        compiler_params=pltpu.CompilerParams(has_side_effects=True),
    )(*arrs)


def _swap_sibling(arrs, name):
    n = len(arrs)

    def body(*refs):
        ins, outs = refs[:n], refs[n:2 * n]
        send_sems, recv_sems = refs[2 * n:]
        x, y, c = lax.axis_index("x"), lax.axis_index("y"), lax.axis_index("c")
        copies = [pltpu.make_async_remote_copy(src_ref=ins[a], dst_ref=outs[a], send_sem=send_sems.at[a],
                                               recv_sem=recv_sems.at[a], device_id=(x, y, 1 - c), device_id_type=MESH)
                  for a in range(n)]
        for cp in copies:
            cp.start()
        for cp in copies:
            cp.wait()

    shapes = [jax.ShapeDtypeStruct(tuple(a.shape), a.dtype) for a in arrs]
    return pl.pallas_call(
        body, name=name, in_specs=[ANY_SPEC] * n, out_specs=[ANY_SPEC] * n, out_shape=shapes,
        scratch_shapes=[pltpu.SemaphoreType.DMA((n,)), pltpu.SemaphoreType.DMA((n,))],
        compiler_params=pltpu.CompilerParams(has_side_effects=True),
    )(*arrs)


def _exchange_chips(arrs, name):
    n = len(arrs)
    K = N_CHIP - 1

    def body(*refs):
        ins, outs = refs[:n], refs[n:2 * n]
        send_sems, recv_sems, loc_sems = refs[2 * n:]
        x, y, c = lax.axis_index("x"), lax.axis_index("y"), lax.axis_index("c")
        mine = 2 * x + y
        copies = []
        for a in range(n):
            for j, (px, py) in enumerate(_other_chips(x, y)):
                cp = pltpu.make_async_remote_copy(src_ref=ins[a].at[2 * px + py], dst_ref=outs[a].at[mine],
                                                  send_sem=send_sems.at[a * K + j], recv_sem=recv_sems.at[a * K + j],
                                                  device_id=(px, py, c), device_id_type=MESH)
                cp.start()
                copies.append(cp)
            own = pltpu.make_async_copy(ins[a].at[mine], outs[a].at[mine], loc_sems.at[a])
            own.start()
            copies.append(own)
        for cp in copies:
            cp.wait()

    shapes = [jax.ShapeDtypeStruct(tuple(a.shape), a.dtype) for a in arrs]
    return pl.pallas_call(
        body, name=name, in_specs=[ANY_SPEC] * n, out_specs=[ANY_SPEC] * n, out_shape=shapes,
        scratch_shapes=[pltpu.SemaphoreType.DMA((n * K,)), pltpu.SemaphoreType.DMA((n * K,)),
                        pltpu.SemaphoreType.DMA((n,))],
        compiler_params=pltpu.CompilerParams(has_side_effects=True),
    )(*arrs)


def _add_halves(mine, theirs, name):
    _, R, C = mine.shape
    cap = max(16, ((2 * 1024 * 1024) // (4 * C * 10)) // 16 * 16)
    T = R if R <= cap else _tile(R, cap, 16)

    def body(a_ref, b_ref, o_ref):
        o_ref[...] = (a_ref[...] + b_ref[...].astype(F32)).astype(BF16)

    blk = pl.BlockSpec((N_CHIP, T, C), lambda i: (0, i, 0))
    return pl.pallas_call(
        body, name=name, grid=(R // T,), in_specs=[blk, blk], out_specs=blk,
        out_shape=jax.ShapeDtypeStruct(mine.shape, BF16), compiler_params=_cp(("parallel",)),
    )(mine, theirs)


def _adamw(parts, w, m, v, name):
    R, C = w.shape
    npart = parts.shape[0]
    cap = max(16, ((2 * 1024 * 1024) // (4 * C * 12)) // 16 * 16)
    T = R if R <= cap else _tile(R, cap, 16)

    def body(p_ref, w_ref, m_ref, v_ref, g_ref, d_ref, nm_ref, nv_ref):
        g = p_ref[0].astype(F32)
        for k in range(1, npart):
            g = g + p_ref[k].astype(F32)
        mm = ADAM_B1 * m_ref[...] + (1.0 - ADAM_B1) * g
        vv = ADAM_B2 * v_ref[...] + (1.0 - ADAM_B2) * (g * g)
        m_hat = mm / (1.0 - ADAM_B1 ** ADAM_STEP)
        v_hat = vv / (1.0 - ADAM_B2 ** ADAM_STEP)
        g_ref[...] = g
        d_ref[...] = -ADAM_LR * (m_hat / (jnp.sqrt(v_hat) + ADAM_EPS) + ADAM_WD * w_ref[...])
        nm_ref[...] = mm
        nv_ref[...] = vv

    row = pl.BlockSpec((T, C), lambda i: (i, 0))
    out = jax.ShapeDtypeStruct((R, C), F32)
    return pl.pallas_call(
        body, name=name, grid=(R // T,),
        in_specs=[pl.BlockSpec((npart, T, C), lambda i: (0, i, 0)), row, row, row],
        out_specs=[row] * 4, out_shape=[out] * 4,
        compiler_params=_cp(("parallel",)),
    )(parts, w, m, v)


SMALL = ("attn_pre_norm", "gdn_A_log", "gdn_dt_bias", "gdn_norm_w", "sb_norm_w", "attn_post_norm",
         "ffn_pre_norm", "ffn_conv_b", "ffn_post_norm")


def _pack_small(arrs):
    rows = []
    for a in arrs:
        flat = a.reshape(-1).astype(F32)
        n = -(-flat.shape[0] // 128) * 128
        rows.append(jnp.pad(flat, (0, n - flat.shape[0])).reshape(-1, 128))
    slab = jnp.concatenate(rows, axis=0)
    pad = (-slab.shape[0]) % 8
    return jnp.pad(slab, ((0, pad), (0, 0)))


def _unpack_small(slab, shapes):
    out, r = [], 0
    for shp in shapes:
        size = 1
        for s in shp:
            size *= s
        nr = -(-size // 128)
        out.append(slab[r:r + nr].reshape(-1)[:size].reshape(shp))
        r += nr
    return out


def _to_blocks_cols(a):
    R, C = a.shape
    return a.reshape(R, N_DEV, C // N_DEV).transpose(1, 0, 2)


def _from_blocks_cols(a):
    n, R, c = a.shape
    return a.transpose(1, 0, 2).reshape(R, n * c)


def kernel(x, meta_tokens, attn_pre_norm, w_in, gdn_conv_w, gdn_A_log, gdn_dt_bias, gdn_norm_w, sb_norm_w, w_out, attn_post_norm, ffn_pre_norm, w_ffn_up, ffn_conv_w, ffn_conv_b, w_ffn_down, ffn_post_norm, loss_target, m_meta_tokens, m_attn_pre_norm, m_w_in, m_gdn_conv_w, m_gdn_A_log, m_gdn_dt_bias, m_gdn_norm_w, m_sb_norm_w, m_w_out, m_attn_post_norm, m_ffn_pre_norm, m_w_ffn_up, m_ffn_conv_w, m_ffn_conv_b, m_w_ffn_down, m_ffn_post_norm, v_meta_tokens, v_attn_pre_norm, v_w_in, v_gdn_conv_w, v_gdn_A_log, v_gdn_dt_bias, v_gdn_norm_w, v_sb_norm_w, v_w_out, v_attn_post_norm, v_ffn_pre_norm, v_w_ffn_up, v_ffn_conv_w, v_ffn_conv_b, v_w_ffn_down, v_ffn_post_norm):
    args = dict(locals())
    seq = x.shape[1]
    LP = -(-(ROW0 + seq) // LP_ALIGN) * LP_ALIGN
    tail = LP - ROW0 - seq

    gathered = _gather_two_level(
        [w_in[0].astype(BF16), w_out[0].astype(BF16), w_ffn_up[0].astype(BF16), w_ffn_down[0].astype(BF16),
         gdn_conv_w[0], ffn_conv_w[0], meta_tokens], name="gather_weights")
    win_o = _from_blocks_cols(gathered[0])
    o_ab = C_QKV
    o_z = o_ab + 2 * GDN_HEADS
    w_inp = jnp.concatenate([win_o[:, :C_QKV], win_o[:, o_z:o_z + C_Z], win_o[:, o_z + C_Z:],
                             win_o[:, o_ab:o_z], jnp.zeros((D_MODEL, C_AB - 2 * GDN_HEADS), BF16)], axis=1)
    w_out_f = gathered[1].reshape(D_MODEL, D_MODEL)
    w_up_f = _from_blocks_cols(gathered[2])
    w_down_f = gathered[3].reshape(D_FF, D_MODEL)
    gconv_f = _from_blocks_cols(gathered[4])
    fconv_f = _from_blocks_cols(gathered[5])
    meta_f = _from_blocks_cols(gathered[6])

    h0 = jnp.concatenate([jnp.zeros((PAD_ROWS, D_MODEL), F32), meta_f, x[0], jnp.zeros((tail, D_MODEL), F32)], axis=0)
    target = jnp.concatenate([jnp.zeros((ROW0, D_MODEL), F32), loss_target[0], jnp.zeros((tail, D_MODEL), F32)], axis=0)
    u = _prenorm_fwd(h0, attn_pre_norm)
    proj = _mm(u, w_inp, F32, "mm_in")
    qn, kn, vg, beta_b, g_b = _gdn_pre_fwd(proj, gconv_f, gdn_A_log, gdn_dt_bias)
    cu, cw, cqd, ckd, cqk, ct, cgl = _gdn_chunk_fwd(qn, kn, vg, beta_b, g_b)
    og, ssave = _gdn_scan_fwd(cu, cw, cqd, ckd, cqk, cgl)
    osb, ctot, sb_nrun = _sb_fwd(proj)
    snw = sb_norm_w.reshape(1, SB_HEADS * SB_DH)
    y = _attn_norm_fwd(og, proj, osb, gdn_norm_w, snw)
    mix = _mm(y, w_out_f, F32, "mm_out")
    h1, n2 = _resid_fwd(h0, mix, attn_post_norm, ffn_pre_norm)
    up = _mm(n2, w_up_f, F32, "mm_up")
    act = _convglu_fwd(up, fconv_f, ffn_conv_b)
    f = _mm(act, w_down_f, F32, "mm_down")
    loss_part, dout, df, d_fpost = _final(h1, f, ffn_post_norm, target, seq)
    loss = lax.psum(loss_part[0, 0], ("x", "y", "c"))

    d_wdown = _mm_tn(act, df, "mm_dw_down")
    dact = _mm(df, w_down_f.T, F32, "mm_dact")
    dup, d_fconv, d_fconvb = _convglu_bwd(up, fconv_f, ffn_conv_b, dact)
    d_wup = _mm_tn(n2, dup, "mm_dw_up")
    dn2 = _mm(dup, w_up_f.T, F32, "mm_dn2")
    dh1, dmix, d_fpre, d_apost = _resid_bwd(h1, mix, attn_post_norm, ffn_pre_norm, dout, dn2)
    d_wout = _mm_tn(y, dmix, "mm_dw_out")
    dy = _mm(dmix, w_out_f.T, F32, "mm_dy")
    dog, dz, dos, d_gnw, d_snw = _attn_norm_bwd(og, proj, osb, gdn_norm_w, snw, dy)
    dqs, dks, dvs = _sb_bwd(proj, ctot, sb_nrun, dos)
    du_, dw_, dqd_, dkd_, dqk_, dgl_ = _gdn_scan_bwd(cu, cw, cqd, ckd, cqk, cgl, ssave, dog)
    dqn, dkn, dvg, dbeta, dg = _gdn_chunk_bwd(qn, kn, vg, beta_b, g_b, ct, du_, dw_, dqd_, dkd_, dqk_, dgl_)
    dqkv, dab, d_gconv, d_gsc = _gdn_pre_bwd(proj, gconv_f, gdn_A_log, gdn_dt_bias, dqn, dkn, dvg, dbeta, dg)
    dproj = jnp.concatenate([dqkv.astype(BF16), dz.astype(BF16), dqs.astype(BF16), dks.astype(BF16),
                             dvs.astype(BF16), dab.astype(BF16)], axis=1)
    d_winp = _mm_tn(u, dproj, "mm_dw_in")
    du0 = _mm(dproj, w_inp.T, F32, "mm_du")
    dh0, d_apre = _prenorm_bwd(h0, attn_pre_norm, du0, dh1)
    grad_x = dh0[ROW0:ROW0 + seq][None]
    d_meta = dh0[PAD_ROWS:ROW0]

    d_win = jnp.concatenate([d_winp[:, :C_QKV], d_winp[:, OFF_AB:OFF_AB + 2 * GDN_HEADS],
                             d_winp[:, OFF_Z:OFF_Z + C_Z], d_winp[:, OFF_SB:OFF_SB + C_SB]], axis=1)
    small_grads = [d_apre, d_gsc[0:1, :GDN_HEADS], d_gsc[1:2, :GDN_HEADS], d_gnw, d_snw.reshape(1, SB_HEADS, SB_DH),
                   d_apost, d_fpre, d_fconvb, d_fpost]
    big_names = ("w_in", "w_out", "w_ffn_up", "w_ffn_down", "gdn_conv_w", "ffn_conv_w", "meta_tokens")
    sends = [_to_blocks_cols(d_win), d_wout.reshape(N_DEV, D_MODEL // N_DEV, D_MODEL), _to_blocks_cols(d_wup),
             d_wdown.reshape(N_DEV, D_FF // N_DEV, D_MODEL), _to_blocks_cols(d_gconv), _to_blocks_cols(d_fconv),
             _to_blocks_cols(d_meta)]
    my_c = lax.axis_index("c")
    halves = [s.reshape((N_CHIP, 2) + s.shape[1:]) for s in sends]
    mine = [lax.dynamic_index_in_dim(h, my_c, axis=1, keepdims=False) for h in halves]
    theirs = _swap_sibling([lax.dynamic_index_in_dim(h, 1 - my_c, axis=1, keepdims=False).astype(BF16) for h in halves],
                           name="grads_swap_sibling")
    chip_sums = [_add_halves(a, b, "grads_add_" + nm) for nm, a, b in zip(big_names, mine, theirs)]
    recv = _exchange_chips(chip_sums, name="grads_exchange_chips")
    slab_parts = _gather_direct([_pack_small(small_grads)], name="gather_small_grads")[0]

    res = {}
    for nm, parts in zip(big_names, recv):
        wloc = args[nm]
        shp = wloc.shape
        w2 = wloc.reshape(shp[-2], shp[-1])
        outs = _adamw(parts, w2, args["m_" + nm].reshape(w2.shape), args["v_" + nm].reshape(w2.shape), "adamw_" + nm)
        res[nm] = [o.reshape(shp) for o in outs]
    small_shapes = [args[nm].shape for nm in SMALL]
    outs = _adamw(slab_parts, _pack_small([args[nm] for nm in SMALL]), _pack_small([args["m_" + nm] for nm in SMALL]),
                  _pack_small([args["v_" + nm] for nm in SMALL]), "adamw_small")
    for k in range(4):
        for nm, val in zip(SMALL, _unpack_small(outs[k], small_shapes)):
            res.setdefault(nm, [None] * 4)[k] = val

    order = ("meta_tokens", "attn_pre_norm", "w_in", "gdn_conv_w", "gdn_A_log", "gdn_dt_bias", "gdn_norm_w",
             "sb_norm_w", "w_out", "attn_post_norm", "ffn_pre_norm", "w_ffn_up", "ffn_conv_w", "ffn_conv_b",
             "w_ffn_down", "ffn_post_norm")
    return (loss, grad_x, *[res[nm][0] for nm in order], *[res[nm][1] for nm in order],
            *[res[nm][2] for nm in order], *[res[nm][3] for nm in order])
```

```python
import functools

import jax
import jax.numpy as jnp
from jax import lax
from jax.experimental import pallas as pl
from jax.experimental.pallas import tpu as pltpu

F32 = jnp.float32
BF16 = jnp.bfloat16

D_MODEL = 1024
N_META = 16
GDN_HEADS = 4
GDN_D = 128
GDN_CHUNK = 64
GDN_CONV = 4
GDN_ROWS = 256
SB_HEADS = 8
SB_DH = 64
SB_BLOCK = 128
D_FF = 2816
FFN_CONV = 3
NORM_EPS = 1e-6
L2_EPS = 1e-6
N_DEV = 8

PAD_ROWS = SB_BLOCK - N_META
ROW0 = SB_BLOCK
SB_SPAN = 512
SB_DEAD = -104.0
SB_SUB = 256
SB_QTILE = 256
LP_ALIGN = 256

C_QKV = 3 * GDN_HEADS * GDN_D
C_Z = GDN_HEADS * GDN_D
C_SB = 3 * SB_HEADS * SB_DH
C_AB = 256
OFF_Z = C_QKV
OFF_SB = OFF_Z + C_Z
OFF_AB = OFF_SB + C_SB
D_INP = OFF_AB + C_AB
D_IN = C_QKV + 2 * GDN_HEADS + C_Z + C_SB

ADAM_LR = 0.001
ADAM_B1 = 0.9
ADAM_B2 = 0.999
ADAM_EPS = 1e-08
ADAM_WD = 0.01
ADAM_STEP = 10

VMEM_LIMIT = 56 * 1024 * 1024
MESH = pl.DeviceIdType.MESH


def _cp(sem=None):
    kw = dict(vmem_limit_bytes=VMEM_LIMIT)
    if sem is not None:
        kw["dimension_semantics"] = sem
    return pltpu.CompilerParams(**kw)


def _tile(n, cap, unit=128):
    best = None
    t = unit
    while t <= min(n, cap):
        if n % t == 0:
            best = t
        t += unit
    assert best is not None, (n, cap, unit)
    return best


def _dot(a, b):
    return jnp.dot(a, b, preferred_element_type=F32)


def _dot_nt(a, b):
    return lax.dot_general(a, b, (((1,), (1,)), ((), ())), preferred_element_type=F32)


def _dot_tn(a, b):
    return lax.dot_general(a, b, (((0,), (0,)), ((), ())), preferred_element_type=F32)


def _split(x):
    hi = x.astype(BF16)
    lo = (x - hi.astype(F32)).astype(BF16)
    return hi, lo


def _dot1(a, b, f=_dot):
    return f(a.astype(BF16), b.astype(BF16))


def _dot3(a, b, f=_dot):
    ah, al = _split(a)
    bh, bl = _split(b)
    return f(ah, bh) + (f(ah, bl) + f(al, bh))


def _dot_exact_l(m_bf16, x, f=_dot):
    xh, xl = _split(x)
    return f(m_bf16, xh) + f(m_bf16, xl)


def _dot_exact_r(x, m_bf16, f=_dot):
    xh, xl = _split(x)
    return f(xh, m_bf16) + f(xl, m_bf16)


def _iota2(shape, dim):
    return lax.broadcasted_iota(jnp.int32, shape, dim)


def _sigmoid(x):
    return 1.0 / (1.0 + jnp.exp(-x))


def _softplus(x):
    return jnp.maximum(x, 0.0) + jnp.log(1.0 + jnp.exp(-jnp.abs(x)))


def _colsum(x):
    return jnp.sum(x, axis=0, keepdims=True)


def _rowsum(x):
    return jnp.sum(x, axis=-1, keepdims=True)


def _mm(a, b, out_dtype, name):
    M, K = a.shape
    K2, N = b.shape
    assert K == K2
    tm = _tile(M, 768)
    tn = _tile(N, max(128, (6 * 1024 * 1024) // (2 * K)))

    def body(a_ref, b_ref, o_ref):
        o_ref[...] = _dot(a_ref[...].astype(BF16), b_ref[...].astype(BF16)).astype(o_ref.dtype)

    return pl.pallas_call(
        body, name=name, grid=(N // tn, M // tm),
        in_specs=[pl.BlockSpec((tm, K), lambda j, i: (i, 0)), pl.BlockSpec((K, tn), lambda j, i: (0, j))],
        out_specs=pl.BlockSpec((tm, tn), lambda j, i: (i, j)),
        out_shape=jax.ShapeDtypeStruct((M, N), out_dtype),
        compiler_params=_cp(("parallel", "parallel")),
    )(a, b)


def _mm_tn(a, b, name):
    M, K = a.shape
    M2, N = b.shape
    assert M == M2
    tm = _tile(M, 768)
    tk = _tile(K, 1408)
    tn = _tile(N, 1408)

    def body(a_ref, b_ref, o_ref):
        @pl.when(pl.program_id(2) == 0)
        def _():
            o_ref[...] = jnp.zeros_like(o_ref)
        o_ref[...] += _dot_tn(a_ref[...].astype(BF16), b_ref[...].astype(BF16))

    return pl.pallas_call(
        body, name=name, grid=(K // tk, N // tn, M // tm),
        in_specs=[pl.BlockSpec((tm, tk), lambda i, j, m: (m, i)), pl.BlockSpec((tm, tn), lambda i, j, m: (m, j))],
        out_specs=pl.BlockSpec((tk, tn), lambda i, j, m: (i, j)),
        out_shape=jax.ShapeDtypeStruct((K, N), F32),
        compiler_params=_cp(("parallel", "parallel", "arbitrary")),
    )(a, b)


def _rms(x):
    return lax.rsqrt(jnp.mean(x * x, axis=-1, keepdims=True) + NORM_EPS)


def _rms_bwd(x, w, dy):
    r = _rms(x)
    n = x * r
    dyw = dy * w
    dx = r * (dyw - n * jnp.mean(dyw * n, axis=-1, keepdims=True))
    return dx, dy * n


def _prenorm_fwd(h0, w):
    LP, D = h0.shape
    T = _tile(LP, 512)

    def body(h_ref, w_ref, u_ref):
        h = h_ref[...]
        u_ref[...] = (h * _rms(h) * w_ref[...]).astype(BF16)

    return pl.pallas_call(
        body, name="prenorm_fwd", grid=(LP // T,),
        in_specs=[pl.BlockSpec((T, D), lambda i: (i, 0)), pl.BlockSpec((1, D), lambda i: (0, 0))],
        out_specs=pl.BlockSpec((T, D), lambda i: (i, 0)),
        out_shape=jax.ShapeDtypeStruct((LP, D), BF16),
        compiler_params=_cp(("parallel",)),
    )(h0, w)


def _prenorm_bwd(h0, w, du, dh1):
    LP, D = h0.shape
    T = _tile(LP, 512)

    def body(h_ref, w_ref, du_ref, dh1_ref, dh0_ref, dw_ref):
        @pl.when(pl.program_id(0) == 0)
        def _():
            dw_ref[...] = jnp.zeros_like(dw_ref)
        dx, dwn = _rms_bwd(h_ref[...], w_ref[...], du_ref[...])
        dh0_ref[...] = dh1_ref[...] + dx
        dw_ref[...] += _colsum(dwn)

    row = pl.BlockSpec((T, D), lambda i: (i, 0))
    vec = pl.BlockSpec((1, D), lambda i: (0, 0))
    return pl.pallas_call(
        body, name="prenorm_bwd", grid=(LP // T,),
        in_specs=[row, vec, row, row], out_specs=[row, vec],
        out_shape=[jax.ShapeDtypeStruct((LP, D), F32), jax.ShapeDtypeStruct((1, D), F32)],
        compiler_params=_cp(("arbitrary",)),
    )(h0, w, du, dh1)


def _causal_taps(ext, w_ref, width, start, rows):
    y = w_ref[width - 1:width, :] * ext[start:start + rows]
    for j in range(width - 1):
        y = y + w_ref[j:j + 1, :] * pltpu.roll(ext, width - 1 - j, 0)[start:start + rows]
    return y


def _shifted_rows(ext, shift, start, rows):
    return ext[start:start + rows] if shift == 0 else pltpu.roll(ext, shift, 0)[start:start + rows]


def _anticausal_taps(dy_ext, w_ref, width, rows):
    n = dy_ext.shape[0]
    dx = w_ref[width - 1:width, :] * dy_ext[0:rows]
    for j in range(width - 1):
        dx = dx + w_ref[j:j + 1, :] * pltpu.roll(dy_ext, n - (width - 1 - j), 0)[0:rows]
    return dx


def _gdn_gate_consts(alog_ref, dtb_ref, h):
    a_coef = -jnp.exp(alog_ref[0:1, h:h + 1])
    return a_coef, dtb_ref[0:1, h:h + 1]


def _gdn_pre_fwd(proj, conv_w, a_log, dt_bias):
    LP = proj.shape[0]
    T = _tile(LP, 256)
    C = C_QKV
    H = GDN_HEADS

    def body(x_ref, halo_ref, ab_ref, cw_ref, alog_ref, dtb_ref, q_ref, k_ref, v_ref, beta_ref, g_ref):
        i = pl.program_id(0)
        ext = jnp.concatenate([jnp.where(i > 0, halo_ref[...], 0.0), x_ref[...]], axis=0)
        y = _causal_taps(ext, cw_ref, GDN_CONV, 8, T)
        c = y * _sigmoid(y)
        for h in range(H):
            sl = slice(h * GDN_D, (h + 1) * GDN_D)
            cq = c[:, sl]
            q_ref[:, sl] = cq * lax.rsqrt(_rowsum(cq * cq) + L2_EPS) * (GDN_D ** -0.5)
            ck = c[:, 512 + h * GDN_D:512 + (h + 1) * GDN_D]
            k_ref[:, sl] = ck * lax.rsqrt(_rowsum(ck * ck) + L2_EPS)
        v_ref[...] = c[:, 1024:]
        ab = ab_ref[...]
        valid = (i * T + _iota2((T, 1), 0)) >= PAD_ROWS
        for h in range(H):
            sl = slice(h * GDN_D, (h + 1) * GDN_D)
            a_coef, dtb = _gdn_gate_consts(alog_ref, dtb_ref, h)
            g = jnp.where(valid, a_coef * _softplus(ab[:, h:h + 1] + dtb), 0.0)
            beta = jnp.where(valid, _sigmoid(ab[:, H + h:H + h + 1]), 0.0)
            g_ref[:, sl] = jnp.broadcast_to(g, (T, GDN_D))
            beta_ref[:, sl] = jnp.broadcast_to(beta, (T, GDN_D))

    t8 = T // 8
    row512 = pl.BlockSpec((T, 512), lambda i: (i, 0))
    small = lambda r, c: pl.BlockSpec((r, c), lambda i: (0, 0))
    out = jax.ShapeDtypeStruct((LP, 512), F32)
    return pl.pallas_call(
        body, name="gdn_pre_fwd", grid=(LP // T,),
        in_specs=[pl.BlockSpec((T, C), lambda i: (i, 0)),
                  pl.BlockSpec((8, C), lambda i: (jnp.maximum(i * t8 - 1, 0), 0)),
                  pl.BlockSpec((T, C_AB), lambda i: (i, OFF_AB // C_AB)),
                  small(GDN_CONV, C), small(1, H), small(1, H)],
        out_specs=[row512] * 5, out_shape=[out] * 5,
        compiler_params=_cp(("parallel",)),
    )(proj, proj, proj, conv_w, a_log, dt_bias)


def _gdn_pre_bwd(proj, conv_w, a_log, dt_bias, dq, dk, dv, dbeta, dg):
    LP = proj.shape[0]
    T = _tile(LP, 256)
    C = C_QKV
    H = GDN_HEADS
    TE = T + 8
    nt = LP // T

    def body(x_ref, xp_ref, xn_ref, ab_ref, cw_ref, alog_ref, dtb_ref,
             dq_ref, dqn_ref, dk_ref, dkn_ref, dv_ref, dvn_ref, dbeta_ref, dg_ref,
             dx_ref, dab_ref, dcw_ref, dsc_ref, dys):
        i = pl.program_id(0)

        @pl.when(i == 0)
        def _():
            dcw_ref[...] = jnp.zeros_like(dcw_ref)
            dsc_ref[...] = jnp.zeros_like(dsc_ref)

        last = i == nt - 1
        ext = jnp.concatenate([jnp.where(i > 0, xp_ref[...], 0.0), x_ref[...], jnp.where(last, 0.0, xn_ref[...])],
                              axis=0)
        y = _causal_taps(ext, cw_ref, GDN_CONV, 8, TE)
        sg = _sigmoid(y)
        c = y * sg
        nxt = lambda a_ref, b_ref: jnp.concatenate([a_ref[...], jnp.where(last, 0.0, b_ref[...])], axis=0)
        dqn = nxt(dq_ref, dqn_ref)
        dkn = nxt(dk_ref, dkn_ref)
        dvv = nxt(dv_ref, dvn_ref)
        for h in range(H):
            sl = slice(h * GDN_D, (h + 1) * GDN_D)
            cq = c[:, sl]
            rq = lax.rsqrt(_rowsum(cq * cq) + L2_EPS)
            nq = cq * rq
            dqh = dqn[:, sl]
            dys[:, sl] = (GDN_D ** -0.5) * rq * (dqh - nq * _rowsum(dqh * nq))
            sk = slice(512 + h * GDN_D, 512 + (h + 1) * GDN_D)
            ck = c[:, sk]
            rk = lax.rsqrt(_rowsum(ck * ck) + L2_EPS)
            nk = ck * rk
            dkh = dkn[:, sl]
            dys[:, sk] = rk * (dkh - nk * _rowsum(dkh * nk))
        dys[:, 1024:] = dvv
        dy = dys[...] * (sg * (1.0 + y * (1.0 - sg)))
        for j in range(GDN_CONV):
            dcw_ref[j:j + 1, :] += _colsum(dy[0:T, :] * _shifted_rows(ext, GDN_CONV - 1 - j, 8, T))
        dx_ref[...] = _anticausal_taps(dy, cw_ref, GDN_CONV, T)
        ab = ab_ref[...]
        valid = (i * T + _iota2((T, 1), 0)) >= PAD_ROWS
        lane = _iota2((T, C_AB), 1)
        lane1 = _iota2((1, 128), 1)
        dab = jnp.zeros((T, C_AB), F32)
        dsc_a = jnp.zeros((1, 128), F32)
        dsc_d = jnp.zeros((1, 128), F32)
        for h in range(H):
            a_coef, dtb = _gdn_gate_consts(alog_ref, dtb_ref, h)
            pre = ab[:, h:h + 1] + dtb
            dgh = jnp.where(valid, dg_ref[:, h * GDN_D:h * GDN_D + 1], 0.0)
            da = dgh * a_coef * _sigmoid(pre)
            beta = _sigmoid(ab[:, H + h:H + h + 1])
            db = jnp.where(valid, dbeta_ref[:, h * GDN_D:h * GDN_D + 1], 0.0) * beta * (1.0 - beta)
            dab = dab + jnp.where(lane == h, da, 0.0) + jnp.where(lane == H + h, db, 0.0)
            dsc_a = dsc_a + jnp.where(lane1 == h, _colsum(dgh * a_coef * _softplus(pre)), 0.0)
            dsc_d = dsc_d + jnp.where(lane1 == h, _colsum(da), 0.0)
        dab_ref[...] = dab
        dsc_ref[0:1, :] += dsc_a
        dsc_ref[1:2, :] += dsc_d

    t8 = T // 8
    nb8 = LP // 8
    prev8 = lambda w: pl.BlockSpec((8, w), lambda i: (jnp.maximum(i * t8 - 1, 0), 0))
    next8 = lambda w: pl.BlockSpec((8, w), lambda i: (jnp.minimum((i + 1) * t8, nb8 - 1), 0))
    row = lambda w: pl.BlockSpec((T, w), lambda i: (i, 0))
    small = lambda r, c: pl.BlockSpec((r, c), lambda i: (0, 0))
    return pl.pallas_call(
        body, name="gdn_pre_bwd", grid=(nt,),
        in_specs=[row(C), prev8(C), next8(C), pl.BlockSpec((T, C_AB), lambda i: (i, OFF_AB // C_AB)),
                  small(GDN_CONV, C), small(1, H), small(1, H),
                  row(512), next8(512), row(512), next8(512), row(512), next8(512), row(512), row(512)],
        out_specs=[row(C), row(C_AB), small(GDN_CONV, C), small(2, 128)],
        out_shape=[jax.ShapeDtypeStruct((LP, C), F32), jax.ShapeDtypeStruct((LP, C_AB), F32),
                   jax.ShapeDtypeStruct((GDN_CONV, C), F32), jax.ShapeDtypeStruct((2, 128), F32)],
        scratch_shapes=[pltpu.VMEM((TE, C), F32)],
        compiler_params=_cp(("arbitrary",)),
    )(proj, proj, proj, proj, conv_w, a_log, dt_bias, dq, dq, dk, dk, dv, dv, dbeta, dg)


def _tri_masks():
    r = _iota2((GDN_CHUNK, GDN_CHUNK), 0)
    c = _iota2((GDN_CHUNK, GDN_CHUNK), 1)
    return r >= c, r > c


def _gdn_chunk_common(q, k, v, beta, gb):
    incl, strict = _tri_masks()
    l_incl = incl.astype(BF16)
    gd = _dot_exact_l(l_incl, jnp.where(strict, gb[:, :GDN_CHUNK], 0.0))
    gc = _dot_exact_l(l_incl, gb)
    decay = jnp.where(incl, jnp.exp(jnp.where(incl, gd, 0.0)), 0.0)
    exp_g = jnp.exp(gc)
    g_last = gc[GDN_CHUNK - 1:GDN_CHUNK, :]
    kd_fac = jnp.exp(g_last - gc)
    gl = jnp.exp(g_last)
    kb = k * beta
    kk = _dot1(kb, k, _dot_nt)
    return dict(incl=incl, strict=strict, decay=decay, exp_g=exp_g, kd_fac=kd_fac, gl=gl, kb=kb, kk=kk,
                vb=v * beta, kbg=kb * exp_g)


def _interleave(gens):
    gens = list(gens)
    while gens:
        alive = []
        for g in gens:
            try:
                next(g)
                alive.append(g)
            except StopIteration:
                pass
        gens = alive


def _gdn_chunk_fwd(qn, kn, v, beta_b, g_b):
    LP = qn.shape[0]
    R = GDN_ROWS
    H = GDN_HEADS
    CH = GDN_CHUNK

    def body(q_ref, k_ref, v_ref, b_ref, g_ref, u_ref, w_ref, qd_ref, kd_ref, qk_ref, t_ref, gl_ref):
        def item(cc, h):
            rs = slice(cc * CH, (cc + 1) * CH)
            sl = slice(h * GDN_D, (h + 1) * GDN_D)
            s64 = slice(h * CH, (h + 1) * CH)
            q, k = q_ref[rs, sl], k_ref[rs, sl]
            m = _gdn_chunk_common(q, k, v_ref[rs, sl], b_ref[rs, sl], g_ref[rs, sl])
            qk_raw = _dot1(q, k, _dot_nt)
            yield
            a = jnp.where(m["strict"], m["kk"] * m["decay"], 0.0)
            eye = (_iota2((CH, CH), 0) == _iota2((CH, CH), 1)).astype(F32)
            t = eye - a
            p = _dot3(a, a)
            yield
            for _ in range(4):
                t = t + _dot3(t, p)
                p = _dot3(p, p)
                yield
            t = t + _dot3(t, p)
            yield
            u_ref[rs, sl] = _dot1(t, m["vb"])
            w_ref[rs, sl] = _dot1(t, m["kbg"])
            qk_ref[rs, s64] = qk_raw * m["decay"]
            t_ref[rs, s64] = t
            qd_ref[rs, sl] = q * m["exp_g"]
            kd_ref[rs, sl] = k * m["kd_fac"]
            gl_ref[cc * 8:(cc + 1) * 8, sl] = jnp.broadcast_to(m["gl"], (8, GDN_D))

        _interleave(item(cc, h) for cc in range(R // CH) for h in range(H))

    row = lambda w: pl.BlockSpec((R, w), lambda i: (i, 0))
    o512 = jax.ShapeDtypeStruct((LP, 512), F32)
    o256 = jax.ShapeDtypeStruct((LP, 256), F32)
    return pl.pallas_call(
        body, name="gdn_chunk_fwd", grid=(LP // R,),
        in_specs=[row(512)] * 5,
        out_specs=[row(512)] * 4 + [row(256)] * 2 + [pl.BlockSpec((R // 8, 512), lambda i: (i, 0))],
        out_shape=[o512] * 4 + [o256] * 2 + [jax.ShapeDtypeStruct((LP // 8, 512), F32)],
        compiler_params=_cp(("parallel",)),
    )(qn, kn, v, beta_b, g_b)


def _gdn_chunk_bwd(qn, kn, v, beta_b, g_b, t_all, du, dw, dqd, dkd, dqk, dgl):
    LP = qn.shape[0]
    R = GDN_ROWS
    H = GDN_HEADS
    CH = GDN_CHUNK

    def body(q_ref, k_ref, v_ref, b_ref, g_ref, t_ref, du_ref, dw_ref, dqd_ref, dkd_ref, dqk_ref, dgl_ref,
             dq_ref, dk_ref, dv_ref, db_ref, dg_ref):
        ones = jnp.ones((CH, GDN_D), BF16)

        def item(cc, h):
            rs = slice(cc * CH, (cc + 1) * CH)
            sl = slice(h * GDN_D, (h + 1) * GDN_D)
            s64 = slice(h * CH, (h + 1) * CH)
            q, k, vv, beta = q_ref[rs, sl], k_ref[rs, sl], v_ref[rs, sl], b_ref[rs, sl]
            m = _gdn_chunk_common(q, k, vv, beta, g_ref[rs, sl])
            incl, strict, decay = m["incl"], m["strict"], m["decay"]
            t = t_ref[rs, s64]
            du_, dw_ = du_ref[rs, sl], dw_ref[rs, sl]
            dqd_, dkd_ = dqd_ref[rs, sl], dkd_ref[rs, sl]
            d_t = _dot1(du_, m["vb"], _dot_nt) + _dot1(dw_, m["kbg"], _dot_nt)
            dvb = _dot1(t, du_, _dot_tn)
            dkbg = _dot1(t, dw_, _dot_tn)
            qk_raw = _dot1(q, k, _dot_nt)
            yield
            x1 = _dot3(d_t, t, _dot_nt)
            dkb = dkbg * m["exp_g"]
            d_gi = _rowsum(dkbg * m["kbg"])
            yield
            d_a = jnp.where(strict, -_dot3(t, x1, _dot_tn), 0.0)
            yield
            d_kk = d_a * decay
            dqk_m = jnp.where(incl, dqk_ref[rs, s64], 0.0)
            dqk_raw = dqk_m * decay
            mm = (d_a * m["kk"] + dqk_m * qk_raw) * decay
            dkb = dkb + _dot1(d_kk, k)
            dk_ = _dot1(d_kk, m["kb"], _dot_tn) + _dot1(dqk_raw, q, _dot_tn)
            dq_ = _dot1(dqk_raw, k) + dqd_ * m["exp_g"]
            d_gi = d_gi + (_dot_exact_r(mm, ones) - _dot_exact_r(mm, ones, _dot_tn))
            yield
            d_gi = d_gi + _rowsum(dqd_ * q * m["exp_g"])
            e = _rowsum(dkd_ * k * m["kd_fac"])
            d_gi = d_gi - e
            d_glast = _colsum(jnp.broadcast_to(e, (CH, GDN_D))) + dgl_ref[cc * 8:cc * 8 + 1, sl] * m["gl"]
            dk_ = dk_ + dkd_ * m["kd_fac"] + dkb * beta
            d_gi = d_gi + jnp.where(_iota2((CH, GDN_D), 0) == CH - 1, d_glast, 0.0)
            u_incl = (_iota2((CH, CH), 1) >= _iota2((CH, CH), 0)).astype(BF16)
            dq_ref[rs, sl] = dq_
            dk_ref[rs, sl] = dk_
            dv_ref[rs, sl] = dvb * beta
            db_ref[rs, sl] = jnp.broadcast_to(_rowsum(dvb * vv) + _rowsum(dkb * k), (CH, GDN_D))
            dg_ref[rs, sl] = _dot_exact_l(u_incl, d_gi)

        _interleave(item(cc, h) for cc in range(R // CH) for h in range(H))

    row = lambda w: pl.BlockSpec((R, w), lambda i: (i, 0))
    o512 = jax.ShapeDtypeStruct((LP, 512), F32)
    gl_spec = pl.BlockSpec((R // 8, 512), lambda i: (i, 0))
    return pl.pallas_call(
        body, name="gdn_chunk_bwd", grid=(LP // R,),
        in_specs=[row(512)] * 5 + [row(256)] + [row(512)] * 4 + [row(256), gl_spec],
        out_specs=[row(512)] * 5, out_shape=[o512] * 5,
        compiler_params=_cp(("parallel",)),
    )(qn, kn, v, beta_b, g_b, t_all, du, dw, dqd, dkd, dqk, dgl)


def _gdn_scan_fwd(u, w, qd, kd, qk, gl):
    LP = u.shape[0]
    CH = GDN_CHUNK
    N = LP // CH
    H = GDN_HEADS

    def body(u_ref, w_ref, qd_ref, kd_ref, qk_ref, gl_ref, o_ref, ssave_ref, s_sc):
        @pl.when(pl.program_id(0) == 0)
        def _():
            s_sc[...] = jnp.zeros_like(s_sc)
        ssave_ref[...] = s_sc[...]

        def item(h):
            sl = slice(h * GDN_D, (h + 1) * GDN_D)
            s = s_sc[:, sl]
            v_new = u_ref[:, sl] - _dot1(w_ref[:, sl], s)
            o_s = _dot1(qd_ref[:, sl], s)
            yield
            o_ref[:, sl] = o_s + _dot1(qk_ref[:, h * CH:(h + 1) * CH], v_new)
            s_sc[:, sl] = s * gl_ref[0:1, sl] + _dot1(kd_ref[:, sl], v_new, _dot_tn)

        _interleave(item(h) for h in range(H))

    row = lambda w_: pl.BlockSpec((CH, w_), lambda n: (n, 0))
    return pl.pallas_call(
        body, name="gdn_scan_fwd", grid=(N,),
        in_specs=[row(512)] * 4 + [row(256), pl.BlockSpec((8, 512), lambda n: (n, 0))],
        out_specs=[row(512), pl.BlockSpec((GDN_D, 512), lambda n: (n, 0))],
        out_shape=[jax.ShapeDtypeStruct((LP, 512), F32), jax.ShapeDtypeStruct((N * GDN_D, 512), F32)],
        scratch_shapes=[pltpu.VMEM((GDN_D, 512), F32)],
        compiler_params=_cp(("arbitrary",)),
    )(u, w, qd, kd, qk, gl)


def _gdn_scan_bwd(u, w, qd, kd, qk, gl, ssave, do):
    LP = u.shape[0]
    CH = GDN_CHUNK
    N = LP // CH
    H = GDN_HEADS

    def body(u_ref, w_ref, qd_ref, kd_ref, qk_ref, gl_ref, s_ref, do_ref,
             du_ref, dw_ref, dqd_ref, dkd_ref, dqk_ref, dgl_ref, ds_sc):
        @pl.when(pl.program_id(0) == 0)
        def _():
            ds_sc[...] = jnp.zeros_like(ds_sc)
        def item(h):
            sl = slice(h * GDN_D, (h + 1) * GDN_D)
            s64 = slice(h * CH, (h + 1) * CH)
            s = s_ref[:, sl]
            ds = ds_sc[:, sl]
            do_ = do_ref[:, sl]
            w_, qd_, kd_, qk_ = w_ref[:, sl], qd_ref[:, sl], kd_ref[:, sl], qk_ref[:, s64]
            v_new = u_ref[:, sl] - _dot1(w_, s)
            d_vnew = _dot1(qk_, do_, _dot_tn) + _dot1(kd_, ds)
            dqd_ref[:, sl] = _dot1(do_, s, _dot_nt)
            ds_new = ds * gl_ref[0:1, sl] + _dot1(qd_, do_, _dot_tn)
            dgl_ref[:, sl] = jnp.broadcast_to(jnp.sum(_colsum(ds * s), axis=-1, keepdims=True), (8, GDN_D))
            yield
            du_ref[:, sl] = d_vnew
            dw_ref[:, sl] = -_dot1(d_vnew, s, _dot_nt)
            dkd_ref[:, sl] = _dot1(v_new, ds, _dot_nt)
            dqk_ref[:, s64] = _dot1(do_, v_new, _dot_nt)
            ds_sc[:, sl] = ds_new - _dot1(w_, d_vnew, _dot_tn)

        _interleave(item(h) for h in range(H))

    rev = lambda w_: pl.BlockSpec((CH, w_), lambda n: (N - 1 - n, 0))
    rev8 = pl.BlockSpec((8, 512), lambda n: (N - 1 - n, 0))
    o512 = jax.ShapeDtypeStruct((LP, 512), F32)
    return pl.pallas_call(
        body, name="gdn_scan_bwd", grid=(N,),
        in_specs=[rev(512)] * 4 + [rev(256), rev8, pl.BlockSpec((GDN_D, 512), lambda n: (N - 1 - n, 0)), rev(512)],
        out_specs=[rev(512)] * 4 + [rev(256), rev8],
        out_shape=[o512] * 4 + [jax.ShapeDtypeStruct((LP, 256), F32), jax.ShapeDtypeStruct((LP // 8, 512), F32)],
        scratch_shapes=[pltpu.VMEM((GDN_D, 512), F32)],
        compiler_params=_cp(("arbitrary",)),
    )(u, w, qd, kd, qk, gl, ssave, do)


def _sb_scores(qh, kblk, mask):
    z = _dot_nt(qh, kblk)
    e = jnp.exp(-jnp.abs(z))
    sp = jnp.maximum(z, 0.0) + jnp.log(1.0 + e)
    return z, e, jnp.where(mask, -sp, 0.0), z - sp


def _sb_fwd(proj):
    LP = proj.shape[0]
    B = SB_BLOCK
    W = min(SB_SPAN, LP)
    SUB = SB_SUB
    Q = min(SB_QTILE, LP)
    nq = LP // Q
    nsub = W // SUB
    scale = SB_DH ** -0.5
    qcol, kcol, vcol = OFF_SB // B, (OFF_SB + 512) // B, (OFF_SB + 1024) // B

    def body(q_ref, k_ref, v_ref, o_ref, c_ref, n_ref):
        i = pl.program_id(1)
        lane = _iota2((Q, B), 1)
        head_a = lane < SB_DH
        qs = q_ref[...] * scale
        qh = [jnp.where(head_a, qs, 0.0).astype(BF16), jnp.where(head_a, 0.0, qs).astype(BF16)]
        u_strict = (_iota2((SUB, SUB), 0) > _iota2((SUB, SUB), 1)).astype(BF16)
        qpos = i * Q + _iota2((Q, W), 0)
        hi0 = (i + 1) * Q
        nspan = (hi0 + W - 1) // W

        def live(st):
            return (st[0] < nspan) & (st[1] > 0)

        def span(st):
            r, carry = st[0], st[2:]
            hi = hi0 - r * W
            k0 = pl.multiple_of(jnp.maximum(hi - W, 0), B)
            kblk = k_ref[pl.ds(k0, W), :].astype(BF16)
            vblk = v_ref[pl.ds(k0, W), :].astype(BF16)
            kpos = k0 + _iota2((Q, W), 1)
            mask = (kpos < qpos) & (kpos >= PAD_ROWS) & (kpos < hi)
            new = [None] * 4

            def head(h):
                o_acc, c = carry[2 * h], carry[2 * h + 1]
                z, e, l1m, lsg = _sb_scores(qh[h], kblk, mask)
                yield
                subs = [slice(b * SUB, (b + 1) * SUB) for b in range(nsub)]
                suf = [_dot_exact_r(l1m[:, bs], u_strict) for bs in subs]
                yield
                parts = [None] * nsub
                for b in reversed(range(nsub)):
                    parts[b] = jnp.where(mask[:, subs[b]], jnp.exp(lsg[:, subs[b]] + suf[b] + c), 0.0)
                    c = c + _rowsum(l1m[:, subs[b]])
                att = jnp.concatenate(parts, axis=1).astype(BF16)
                new[2 * h], new[2 * h + 1] = o_acc + _dot(att, vblk), c

            _interleave(head(h) for h in range(2))
            more = (jnp.maximum(jnp.max(new[1]), jnp.max(new[3])) > SB_DEAD).astype(jnp.int32)
            return (r + 1, more, *new)

        zero_o = jnp.zeros((Q, B), F32)
        zero_c = jnp.zeros((Q, 1), F32)
        nrun, _, o_a, c_a, o_b, c_b = lax.while_loop(
            live, span, (jnp.int32(0), jnp.int32(1), zero_o, zero_c, zero_o, zero_c))
        o_ref[...] = jnp.where(head_a, o_a, o_b)
        c_ref[...] = jnp.where(head_a, c_a, c_b)
        n_ref[pl.program_id(0), i] = nrun

    blk = pl.BlockSpec((Q, B), lambda p, i: (i, p))
    out = jax.ShapeDtypeStruct((LP, 512), F32)
    return pl.pallas_call(
        body, name="sb_fwd", grid=(SB_HEADS // 2, nq),
        in_specs=[pl.BlockSpec((Q, B), lambda p, i: (i, qcol + p)),
                  pl.BlockSpec((LP, B), lambda p, i: (0, kcol + p)),
                  pl.BlockSpec((LP, B), lambda p, i: (0, vcol + p))],
        out_specs=[blk, blk, pl.BlockSpec(memory_space=pltpu.SMEM)],
        out_shape=[out, out, jax.ShapeDtypeStruct((SB_HEADS // 2, nq), jnp.int32)],
        compiler_params=_cp(("arbitrary", "arbitrary")),
    )(proj, proj, proj)


def _sb_bwd(proj, ctot, nrun_all, do):
    LP = proj.shape[0]
    B = SB_BLOCK
    W = min(SB_SPAN, LP)
    SUB = SB_SUB
    Q = min(SB_QTILE, LP)
    nq = LP // Q
    nsub = W // SUB
    scale = SB_DH ** -0.5
    qcol, kcol, vcol = OFF_SB // B, (OFF_SB + 512) // B, (OFF_SB + 1024) // B

    def body(n_ref, q_ref, k_ref, v_ref, c_ref, do_ref, dq_ref, dk_ref, dv_ref):
        i = pl.program_id(1)

        @pl.when(i == 0)
        def _():
            dk_ref[...] = jnp.zeros_like(dk_ref)
            dv_ref[...] = jnp.zeros_like(dv_ref)

        lane = _iota2((Q, B), 1)
        head_a = lane < SB_DH
        qs = q_ref[...] * scale
        qh = [jnp.where(head_a, qs, 0.0).astype(BF16), jnp.where(head_a, 0.0, qs).astype(BF16)]
        dof = do_ref[...]
        doh = [jnp.where(head_a, dof, 0.0).astype(BF16), jnp.where(head_a, 0.0, dof).astype(BF16)]
        cfull = c_ref[...]
        ctot_h = [cfull[:, 0:1], cfull[:, SB_DH:SB_DH + 1]]
        sub_r, sub_c = _iota2((SUB, SUB), 0), _iota2((SUB, SUB), 1)
        u_strict = (sub_r > sub_c).astype(BF16)
        l_strict = (sub_r < sub_c).astype(BF16)
        qpos = i * Q + _iota2((Q, W), 0)
        hi0 = (i + 1) * Q
        nrun = n_ref[pl.program_id(0), i]

        def span(t, carry):
            r = nrun - 1 - t
            hi = hi0 - r * W
            k0 = pl.multiple_of(jnp.maximum(hi - W, 0), B)
            kblk = k_ref[pl.ds(k0, W), :].astype(BF16)
            vblk = v_ref[pl.ds(k0, W), :].astype(BF16)
            kpos = k0 + _iota2((Q, W), 1)
            mask = (kpos < qpos) & (kpos >= PAD_ROWS) & (kpos < hi)
            new = [None] * 6
            dk_add, dv_add = [None, None], [None, None]
            subs = [slice(b * SUB, (b + 1) * SUB) for b in range(nsub)]

            def head(h):
                dq_acc, pre, ecar = carry[3 * h], carry[3 * h + 1], carry[3 * h + 2]
                z, e, l1m, lsg = _sb_scores(qh[h], kblk, mask)
                d_att = _dot_nt(doh[h], vblk)
                yield
                sig = jnp.where(z >= 0.0, 1.0, e) / (1.0 + e)
                suf = [_dot_exact_r(l1m[:, bs], u_strict) for bs in subs]
                yield
                att_parts, p_parts = [None] * nsub, [None] * nsub
                for b, bs in enumerate(subs):
                    pre = pre + _rowsum(l1m[:, bs])
                    att_parts[b] = jnp.where(mask[:, bs], jnp.exp(lsg[:, bs] + suf[b] + (ctot_h[h] - pre)), 0.0)
                    p_parts[b] = att_parts[b] * d_att[:, bs]
                pcum = [_dot_exact_r(p, l_strict) for p in p_parts]
                yield
                dz_parts = [None] * nsub
                for b, bs in enumerate(subs):
                    sg = sig[:, bs]
                    dz_parts[b] = jnp.where(mask[:, bs], p_parts[b] * (1.0 - sg) - sg * (ecar + pcum[b]), 0.0)
                    ecar = ecar + _rowsum(p_parts[b])
                att = jnp.concatenate(att_parts, axis=1).astype(BF16)
                dz = jnp.concatenate(dz_parts, axis=1).astype(BF16)
                new[3 * h:3 * h + 3] = [dq_acc + _dot(dz, kblk), pre, ecar]
                dk_add[h] = _dot_tn(dz, qh[h])
                dv_add[h] = _dot_tn(att, doh[h])

            _interleave(head(h) for h in range(2))
            dk_ref[pl.ds(k0, W), :] += dk_add[0] + dk_add[1]
            dv_ref[pl.ds(k0, W), :] += dv_add[0] + dv_add[1]
            return tuple(new)

        zero_o = jnp.zeros((Q, B), F32)
        zero_c = jnp.zeros((Q, 1), F32)
        res = lax.fori_loop(0, nrun, span, (zero_o, zero_c, zero_c, zero_o, zero_c, zero_c))
        dq_ref[...] = jnp.where(head_a, res[0], res[3]) * scale

    blk = pl.BlockSpec((Q, B), lambda p, i: (i, p))
    col = pl.BlockSpec((LP, B), lambda p, i: (0, p))
    out = jax.ShapeDtypeStruct((LP, 512), F32)
    return pl.pallas_call(
        body, name="sb_bwd", grid=(SB_HEADS // 2, nq),
        in_specs=[pl.BlockSpec(memory_space=pltpu.SMEM),
                  pl.BlockSpec((Q, B), lambda p, i: (i, qcol + p)),
                  pl.BlockSpec((LP, B), lambda p, i: (0, kcol + p)),
                  pl.BlockSpec((LP, B), lambda p, i: (0, vcol + p)),
                  blk, blk],
        out_specs=[blk, col, col], out_shape=[out, out, out],
        compiler_params=_cp(("arbitrary", "arbitrary")),
    )(nrun_all, proj, proj, proj, ctot, do)


def _sb_group_mean():
    r = jnp.right_shift(_iota2((512, 512), 0), 6)
    c = jnp.right_shift(_iota2((512, 512), 1), 6)
    return jnp.where(r == c, 1.0 / SB_DH, 0.0).astype(BF16)


def _attn_norm_fwd(og, proj, osb, gnw, snw):
    LP = og.shape[0]
    T = _tile(LP, 256)

    def body(og_ref, z_ref, os_ref, gnw_ref, snw_ref, y_ref):
        valid = (pl.program_id(0) * T + _iota2((T, 1), 0)) >= PAD_ROWS
        z = z_ref[...]
        zg = z * _sigmoid(z)
        for h in range(GDN_HEADS):
            sl = slice(h * GDN_D, (h + 1) * GDN_D)
            o = og_ref[:, sl]
            y = o * _rms(o) * gnw_ref[...] * zg[:, sl]
            y_ref[:, sl] = jnp.where(valid, y, 0.0).astype(BF16)
        o = os_ref[...]
        msq = _dot_exact_r(o * o, _sb_group_mean())
        y = o * lax.rsqrt(msq + NORM_EPS) * snw_ref[...]
        y_ref[:, 512:] = jnp.where(valid, y, 0.0).astype(BF16)

    row = pl.BlockSpec((T, 512), lambda i: (i, 0))
    return pl.pallas_call(
        body, name="attn_norm_fwd", grid=(LP // T,),
        in_specs=[row, pl.BlockSpec((T, 512), lambda i: (i, OFF_Z // 512)), row,
                  pl.BlockSpec((1, GDN_D), lambda i: (0, 0)), pl.BlockSpec((1, 512), lambda i: (0, 0))],
        out_specs=pl.BlockSpec((T, 1024), lambda i: (i, 0)),
        out_shape=jax.ShapeDtypeStruct((LP, 1024), BF16),
        compiler_params=_cp(("parallel",)),
    )(og, proj, osb, gnw, snw)


def _attn_norm_bwd(og, proj, osb, gnw, snw, dy):
    LP = og.shape[0]
    T = _tile(LP, 256)

    def body(og_ref, z_ref, os_ref, gnw_ref, snw_ref, dy_ref, dog_ref, dz_ref, dos_ref, dgw_ref, dsw_ref):
        @pl.when(pl.program_id(0) == 0)
        def _():
            dgw_ref[...] = jnp.zeros_like(dgw_ref)
            dsw_ref[...] = jnp.zeros_like(dsw_ref)
        valid = (pl.program_id(0) * T + _iota2((T, 1), 0)) >= PAD_ROWS
        dy = jnp.where(valid, dy_ref[...], 0.0)
        z = z_ref[...]
        sg = _sigmoid(z)
        zg = z * sg
        dgw = jnp.zeros((1, GDN_D), F32)
        for h in range(GDN_HEADS):
            sl = slice(h * GDN_D, (h + 1) * GDN_D)
            o = og_ref[:, sl]
            dyh = dy[:, sl]
            dx, dwn = _rms_bwd(o, gnw_ref[...], dyh * zg[:, sl])
            dog_ref[:, sl] = dx
            dgw = dgw + _colsum(dwn)
            yn = o * _rms(o) * gnw_ref[...]
            dz_ref[:, sl] = dyh * yn * (sg[:, sl] * (1.0 + z[:, sl] * (1.0 - sg[:, sl])))
        dgw_ref[...] += dgw
        o = os_ref[...]
        gm = _sb_group_mean()
        r = lax.rsqrt(_dot_exact_r(o * o, gm) + NORM_EPS)
        n = o * r
        dys = dy[:, 512:]
        dyw = dys * snw_ref[...]
        dos_ref[...] = r * (dyw - n * _dot_exact_r(dyw * n, gm))
        dsw_ref[...] += _colsum(dys * n)

    row = pl.BlockSpec((T, 512), lambda i: (i, 0))
    gw = pl.BlockSpec((1, GDN_D), lambda i: (0, 0))
    sw = pl.BlockSpec((1, 512), lambda i: (0, 0))
    o512 = jax.ShapeDtypeStruct((LP, 512), F32)
    return pl.pallas_call(
        body, name="attn_norm_bwd", grid=(LP // T,),
        in_specs=[row, pl.BlockSpec((T, 512), lambda i: (i, OFF_Z // 512)), row, gw, sw,
                  pl.BlockSpec((T, 1024), lambda i: (i, 0))],
        out_specs=[row, row, row, gw, sw],
        out_shape=[o512, o512, o512, jax.ShapeDtypeStruct((1, GDN_D), F32), jax.ShapeDtypeStruct((1, 512), F32)],
        compiler_params=_cp(("arbitrary",)),
    )(og, proj, osb, gnw, snw, dy)


def _resid_fwd(h0, mix, w_post, w_pre):
    LP, D = h0.shape
    T = _tile(LP, 512)

    def body(h0_ref, mix_ref, wp_ref, wf_ref, h1_ref, n2_ref):
        mix = mix_ref[...]
        h1 = h0_ref[...] + mix * _rms(mix) * wp_ref[...]
        h1_ref[...] = h1
        n2_ref[...] = (h1 * _rms(h1) * wf_ref[...]).astype(BF16)

    row = pl.BlockSpec((T, D), lambda i: (i, 0))
    vec = pl.BlockSpec((1, D), lambda i: (0, 0))
    return pl.pallas_call(
        body, name="resid_fwd", grid=(LP // T,),
        in_specs=[row, row, vec, vec], out_specs=[row, row],
        out_shape=[jax.ShapeDtypeStruct((LP, D), F32), jax.ShapeDtypeStruct((LP, D), BF16)],
        compiler_params=_cp(("parallel",)),
    )(h0, mix, w_post, w_pre)


def _resid_bwd(h1, mix, w_post, w_pre, dout, dn2):
    LP, D = h1.shape
    T = _tile(LP, 512)

    def body(h1_ref, mix_ref, wp_ref, wf_ref, dout_ref, dn2_ref, dh1_ref, dmix_ref, dwf_ref, dwp_ref):
        @pl.when(pl.program_id(0) == 0)
        def _():
            dwf_ref[...] = jnp.zeros_like(dwf_ref)
            dwp_ref[...] = jnp.zeros_like(dwp_ref)
        dx, dwn = _rms_bwd(h1_ref[...], wf_ref[...], dn2_ref[...])
        dh1 = dout_ref[...] + dx
        dh1_ref[...] = dh1
        dwf_ref[...] += _colsum(dwn)
        dmix, dwn2 = _rms_bwd(mix_ref[...], wp_ref[...], dh1)
        dmix_ref[...] = dmix.astype(BF16)
        dwp_ref[...] += _colsum(dwn2)

    row = pl.BlockSpec((T, D), lambda i: (i, 0))
    vec = pl.BlockSpec((1, D), lambda i: (0, 0))
    v = jax.ShapeDtypeStruct((1, D), F32)
    return pl.pallas_call(
        body, name="resid_bwd", grid=(LP // T,),
        in_specs=[row, row, vec, vec, row, row], out_specs=[row, row, vec, vec],
        out_shape=[jax.ShapeDtypeStruct((LP, D), F32), jax.ShapeDtypeStruct((LP, D), BF16), v, v],
        compiler_params=_cp(("arbitrary",)),
    )(h1, mix, w_post, w_pre, dout, dn2)


GELU_C = 0.7978845608028654
GELU_A = 0.044715


def _gelu_parts(x):
    t = jnp.tanh(GELU_C * (x + GELU_A * x * x * x))
    return 0.5 * x * (1.0 + t), t


def _convglu_fwd(up, conv_w, conv_b):
    LP, C = up.shape
    T = _tile(LP, 128)

    def body(x_ref, halo_ref, cw_ref, cb_ref, act_ref):
        i = pl.program_id(0)
        ext = jnp.concatenate([jnp.where(i > 0, halo_ref[...], 0.0), x_ref[...]], axis=0)
        y = _causal_taps(ext, cw_ref, FFN_CONV, 8, T) + cb_ref[...]
        g, _ = _gelu_parts(y[:, :D_FF])
        act_ref[...] = (g * y[:, D_FF:]).astype(BF16)

    t8 = T // 8
    return pl.pallas_call(
        body, name="convglu_fwd", grid=(LP // T,),
        in_specs=[pl.BlockSpec((T, C), lambda i: (i, 0)),
                  pl.BlockSpec((8, C), lambda i: (jnp.maximum(i * t8 - 1, 0), 0)),
                  pl.BlockSpec((FFN_CONV, C), lambda i: (0, 0)), pl.BlockSpec((1, C), lambda i: (0, 0))],
        out_specs=pl.BlockSpec((T, D_FF), lambda i: (i, 0)),
        out_shape=jax.ShapeDtypeStruct((LP, D_FF), BF16),
        compiler_params=_cp(("parallel",)),
    )(up, up, conv_w, conv_b)


def _convglu_bwd(up, conv_w, conv_b, dact):
    LP, C = up.shape
    T = _tile(LP, 128)
    TE = T + 8
    nt = LP // T

    def body(x_ref, xp_ref, xn_ref, cw_ref, cb_ref, da_ref, dan_ref, dx_ref, dcw_ref, dcb_ref):
        i = pl.program_id(0)

        @pl.when(i == 0)
        def _():
            dcw_ref[...] = jnp.zeros_like(dcw_ref)
            dcb_ref[...] = jnp.zeros_like(dcb_ref)

        last = i == nt - 1
        ext = jnp.concatenate([jnp.where(i > 0, xp_ref[...], 0.0), x_ref[...], jnp.where(last, 0.0, xn_ref[...])],
                              axis=0)
        y = _causal_taps(ext, cw_ref, FFN_CONV, 8, TE) + cb_ref[...]
        gate, val = y[:, :D_FF], y[:, D_FF:]
        g, t = _gelu_parts(gate)
        dg_dx = 0.5 * (1.0 + t) + 0.5 * gate * (1.0 - t * t) * GELU_C * (1.0 + 3.0 * GELU_A * gate * gate)
        da = jnp.concatenate([da_ref[...], jnp.where(last, 0.0, dan_ref[...])], axis=0)
        dy = jnp.concatenate([da * val * dg_dx, da * g], axis=1)
        dy_t = dy[0:T, :]
        dcb_ref[...] += _colsum(dy_t)
        for j in range(FFN_CONV):
            dcw_ref[j:j + 1, :] += _colsum(dy_t * _shifted_rows(ext, FFN_CONV - 1 - j, 8, T))
        dx_ref[...] = _anticausal_taps(dy, cw_ref, FFN_CONV, T).astype(BF16)

    t8 = T // 8
    nb8 = LP // 8
    prev8 = lambda w: pl.BlockSpec((8, w), lambda i: (jnp.maximum(i * t8 - 1, 0), 0))
    next8 = lambda w: pl.BlockSpec((8, w), lambda i: (jnp.minimum((i + 1) * t8, nb8 - 1), 0))
    row = lambda w: pl.BlockSpec((T, w), lambda i: (i, 0))
    small = lambda r: pl.BlockSpec((r, C), lambda i: (0, 0))
    return pl.pallas_call(
        body, name="convglu_bwd", grid=(nt,),
        in_specs=[row(C), prev8(C), next8(C), small(FFN_CONV), small(1), row(D_FF), next8(D_FF)],
        out_specs=[row(C), small(FFN_CONV), small(1)],
        out_shape=[jax.ShapeDtypeStruct((LP, C), BF16), jax.ShapeDtypeStruct((FFN_CONV, C), F32),
                   jax.ShapeDtypeStruct((1, C), F32)],
        compiler_params=_cp(("arbitrary",)),
    )(up, up, up, conv_w, conv_b, dact, dact)


def _final(h1, f, w_post, target, n_real):
    LP, D = h1.shape
    T = _tile(LP, 256)

    def body(h1_ref, f_ref, w_ref, t_ref, loss_ref, dout_ref, df_ref, dw_ref):
        @pl.when(pl.program_id(0) == 0)
        def _():
            loss_ref[...] = jnp.zeros_like(loss_ref)
            dw_ref[...] = jnp.zeros_like(dw_ref)
        rows = pl.program_id(0) * T + _iota2((T, 1), 0)
        real = (rows >= ROW0) & (rows < ROW0 + n_real)
        f = f_ref[...]
        out = h1_ref[...] + f * _rms(f) * w_ref[...]
        err = jnp.where(real, out - t_ref[...], 0.0)
        loss_ref[...] += 0.5 * jnp.sum(_colsum(jnp.mean(err * err, axis=-1, keepdims=True)), axis=-1, keepdims=True)
        dout = err * (1.0 / D)
        dout_ref[...] = dout
        dx, dwn = _rms_bwd(f, w_ref[...], dout)
        df_ref[...] = dx.astype(BF16)
        dw_ref[...] += _colsum(dwn)

    row = pl.BlockSpec((T, D), lambda i: (i, 0))
    vec = pl.BlockSpec((1, D), lambda i: (0, 0))
    return pl.pallas_call(
        body, name="final_loss", grid=(LP // T,),
        in_specs=[row, row, vec, row],
        out_specs=[pl.BlockSpec((1, 128), lambda i: (0, 0)), row, row, vec],
        out_shape=[jax.ShapeDtypeStruct((1, 128), F32), jax.ShapeDtypeStruct((LP, D), F32),
                   jax.ShapeDtypeStruct((LP, D), BF16), jax.ShapeDtypeStruct((1, D), F32)],
        compiler_params=_cp(("arbitrary",)),
    )(h1, f, w_post, target)


ANY_SPEC = pl.BlockSpec(memory_space=pl.ANY)
N_CHIP = 4


def _other_chips(x, y):
    return [(1 - x, y), (x, 1 - y), (1 - x, 1 - y)]


def _gather_direct(arrs, name):
    n = len(arrs)
    npeer = N_DEV - 1

    def body(*refs):
        ins, outs = refs[:n], refs[n:2 * n]
        send_sems, recv_sems, loc_sems = refs[2 * n:]
        x, y, c = lax.axis_index("x"), lax.axis_index("y"), lax.axis_index("c")
        me = 4 * x + 2 * y + c
        copies = []
        for a in range(n):
            for kk in range(1, N_DEV):
                px = 1 - x if kk & 4 else x
                py = 1 - y if kk & 2 else y
                pc = 1 - c if kk & 1 else c
                s = a * npeer + kk - 1
                cp = pltpu.make_async_remote_copy(src_ref=ins[a], dst_ref=outs[a].at[me], send_sem=send_sems.at[s],
                                                  recv_sem=recv_sems.at[s], device_id=(px, py, pc), device_id_type=MESH)
                cp.start()
                copies.append(cp)
            own = pltpu.make_async_copy(ins[a], outs[a].at[me], loc_sems.at[a])
            own.start()
            copies.append(own)
        for cp in copies:
            cp.wait()

    shapes = [jax.ShapeDtypeStruct((N_DEV,) + tuple(a.shape), a.dtype) for a in arrs]
    return pl.pallas_call(
        body, name=name, in_specs=[ANY_SPEC] * n, out_specs=[ANY_SPEC] * n, out_shape=shapes,
        scratch_shapes=[pltpu.SemaphoreType.DMA((n * npeer,)), pltpu.SemaphoreType.DMA((n * npeer,)),
                        pltpu.SemaphoreType.DMA((n,))],
        compiler_params=pltpu.CompilerParams(has_side_effects=True),
    )(*arrs)


def _gather_two_level(arrs, name):
    n = len(arrs)
    K = 7

    def body(*refs):
        ins, outs = refs[:n], refs[n:2 * n]
        send_sems, recv_sems, loc_sems = refs[2 * n:]
        x, y, c = lax.axis_index("x"), lax.axis_index("y"), lax.axis_index("c")
        me = 4 * x + 2 * y + c
        sib = (x, y, 1 - c)
        chips = _other_chips(x, y)

        def cp(a, k, src, slot, to):
            return pltpu.make_async_remote_copy(src_ref=src, dst_ref=outs[a].at[slot], send_sem=send_sems.at[a * K + k],
                                                recv_sem=recv_sems.at[a * K + k], device_id=to, device_id_type=MESH)

        owns, first, passed = [], [], []
        for a in range(n):
            own = pltpu.make_async_copy(ins[a], outs[a].at[me], loc_sems.at[a])
            own.start()
            owns.append(own)
            first.append(cp(a, 0, ins[a], me, sib))
            for j, (px, py) in enumerate(chips):
                first.append(cp(a, 1 + j, ins[a], me, (px, py, c)))
        for f in first:
            f.start()
        for j, (px, py) in enumerate(chips):
            slot = 4 * px + 2 * py + c
            for a in range(n):
                cp(a, 1 + j, ins[a], slot, (px, py, c)).wait_recv()
                fwd = cp(a, 4 + j, outs[a].at[slot], slot, sib)
                fwd.start()
                passed.append(fwd)
        for a in range(n):
            cp(a, 0, ins[a], 4 * x + 2 * y + (1 - c), sib).wait_recv()
            for j, (px, py) in enumerate(chips):
                cp(a, 4 + j, ins[a], 4 * px + 2 * py + (1 - c), sib).wait_recv()
        for f in first + passed:
            f.wait_send()
        for own in owns:
            own.wait()

    shapes = [jax.ShapeDtypeStruct((N_DEV,) + tuple(a.shape), a.dtype) for a in arrs]
    return pl.pallas_call(
        body, name=name, in_specs=[ANY_SPEC] * n, out_specs=[ANY_SPEC] * n, out_shape=shapes,
        scratch_shapes=[pltpu.SemaphoreType.DMA((n * K,)), pltpu.SemaphoreType.DMA((n * K,)),
                        pltpu.SemaphoreType.DMA((n,))],
        compiler_params=pltpu.CompilerParams(has_side_effects=True),
    )(*arrs)


def _swap_sibling(arrs, name):
    n = len(arrs)

    def body(*refs):
        ins, outs = refs[:n], refs[n:2 * n]
        send_sems, recv_sems = refs[2 * n:]
        x, y, c = lax.axis_index("x"), lax.axis_index("y"), lax.axis_index("c")
        copies = [pltpu.make_async_remote_copy(src_ref=ins[a], dst_ref=outs[a], send_sem=send_sems.at[a],
                                               recv_sem=recv_sems.at[a], device_id=(x, y, 1 - c), device_id_type=MESH)
                  for a in range(n)]
        for cp in copies:
            cp.start()
        for cp in copies:
            cp.wait()

    shapes = [jax.ShapeDtypeStruct(tuple(a.shape), a.dtype) for a in arrs]
    return pl.pallas_call(
        body, name=name, in_specs=[ANY_SPEC] * n, out_specs=[ANY_SPEC] * n, out_shape=shapes,
        scratch_shapes=[pltpu.SemaphoreType.DMA((n,)), pltpu.SemaphoreType.DMA((n,))],
        compiler_params=pltpu.CompilerParams(has_side_effects=True),
    )(*arrs)


def _exchange_chips(arrs, name):
    n = len(arrs)
    K = N_CHIP - 1

    def body(*refs):
        ins, outs = refs[:n], refs[n:2 * n]
        send_sems, recv_sems, loc_sems = refs[2 * n:]
        x, y, c = lax.axis_index("x"), lax.axis_index("y"), lax.axis_index("c")
        mine = 2 * x + y
        copies = []
        for a in range(n):
            for j, (px, py) in enumerate(_other_chips(x, y)):
                cp = pltpu.make_async_remote_copy(src_ref=ins[a].at[2 * px + py], dst_ref=outs[a].at[mine],
                                                  send_sem=send_sems.at[a * K + j], recv_sem=recv_sems.at[a * K + j],
                                                  device_id=(px, py, c), device_id_type=MESH)
                cp.start()
                copies.append(cp)
            own = pltpu.make_async_copy(ins[a].at[mine], outs[a].at[mine], loc_sems.at[a])
            own.start()
            copies.append(own)
        for cp in copies:
            cp.wait()

    shapes = [jax.ShapeDtypeStruct(tuple(a.shape), a.dtype) for a in arrs]
    return pl.pallas_call(
        body, name=name, in_specs=[ANY_SPEC] * n, out_specs=[ANY_SPEC] * n, out_shape=shapes,
        scratch_shapes=[pltpu.SemaphoreType.DMA((n * K,)), pltpu.SemaphoreType.DMA((n * K,)),
                        pltpu.SemaphoreType.DMA((n,))],
        compiler_params=pltpu.CompilerParams(has_side_effects=True),
    )(*arrs)


def _add_halves(mine, theirs, name):
    _, R, C = mine.shape
    cap = max(16, ((2 * 1024 * 1024) // (4 * C * 10)) // 16 * 16)
    T = R if R <= cap else _tile(R, cap, 16)

    def body(a_ref, b_ref, o_ref):
        o_ref[...] = (a_ref[...] + b_ref[...].astype(F32)).astype(BF16)

    blk = pl.BlockSpec((N_CHIP, T, C), lambda i: (0, i, 0))
    return pl.pallas_call(
        body, name=name, grid=(R // T,), in_specs=[blk, blk], out_specs=blk,
        out_shape=jax.ShapeDtypeStruct(mine.shape, BF16), compiler_params=_cp(("parallel",)),
    )(mine, theirs)


def _adamw(parts, w, m, v, name):
    R, C = w.shape
    npart = parts.shape[0]
    cap = max(16, ((2 * 1024 * 1024) // (4 * C * 12)) // 16 * 16)
    T = R if R <= cap else _tile(R, cap, 16)

    def body(p_ref, w_ref, m_ref, v_ref, g_ref, d_ref, nm_ref, nv_ref):
        g = p_ref[0].astype(F32)
        for k in range(1, npart):
            g = g + p_ref[k].astype(F32)
        mm = ADAM_B1 * m_ref[...] + (1.0 - ADAM_B1) * g
        vv = ADAM_B2 * v_ref[...] + (1.0 - ADAM_B2) * (g * g)
        m_hat = mm / (1.0 - ADAM_B1 ** ADAM_STEP)
        v_hat = vv / (1.0 - ADAM_B2 ** ADAM_STEP)
        g_ref[...] = g
        d_ref[...] = -ADAM_LR * (m_hat / (jnp.sqrt(v_hat) + ADAM_EPS) + ADAM_WD * w_ref[...])
        nm_ref[...] = mm
        nv_ref[...] = vv

    row = pl.BlockSpec((T, C), lambda i: (i, 0))
    out = jax.ShapeDtypeStruct((R, C), F32)
    return pl.pallas_call(
        body, name=name, grid=(R // T,),
        in_specs=[pl.BlockSpec((npart, T, C), lambda i: (0, i, 0)), row, row, row],
        out_specs=[row] * 4, out_shape=[out] * 4,
        compiler_params=_cp(("parallel",)),
    )(parts, w, m, v)


SMALL = ("attn_pre_norm", "gdn_A_log", "gdn_dt_bias", "gdn_norm_w", "sb_norm_w", "attn_post_norm",
         "ffn_pre_norm", "ffn_conv_b", "ffn_post_norm")


def _pack_small(arrs):
    rows = []
    for a in arrs:
        flat = a.reshape(-1).astype(F32)
        n = -(-flat.shape[0] // 128) * 128
        rows.append(jnp.pad(flat, (0, n - flat.shape[0])).reshape(-1, 128))
    slab = jnp.concatenate(rows, axis=0)
    pad = (-slab.shape[0]) % 8
    return jnp.pad(slab, ((0, pad), (0, 0)))


def _unpack_small(slab, shapes):
    out, r = [], 0
    for shp in shapes:
        size = 1
        for s in shp:
            size *= s
        nr = -(-size // 128)
        out.append(slab[r:r + nr].reshape(-1)[:size].reshape(shp))
        r += nr
    return out


def _to_blocks_cols(a):
    R, C = a.shape
    return a.reshape(R, N_DEV, C // N_DEV).transpose(1, 0, 2)


def _from_blocks_cols(a):
    n, R, c = a.shape
    return a.transpose(1, 0, 2).reshape(R, n * c)


def kernel(x, meta_tokens, attn_pre_norm, w_in, gdn_conv_w, gdn_A_log, gdn_dt_bias, gdn_norm_w, sb_norm_w, w_out, attn_post_norm, ffn_pre_norm, w_ffn_up, ffn_conv_w, ffn_conv_b, w_ffn_down, ffn_post_norm, loss_target, m_meta_tokens, m_attn_pre_norm, m_w_in, m_gdn_conv_w, m_gdn_A_log, m_gdn_dt_bias, m_gdn_norm_w, m_sb_norm_w, m_w_out, m_attn_post_norm, m_ffn_pre_norm, m_w_ffn_up, m_ffn_conv_w, m_ffn_conv_b, m_w_ffn_down, m_ffn_post_norm, v_meta_tokens, v_attn_pre_norm, v_w_in, v_gdn_conv_w, v_gdn_A_log, v_gdn_dt_bias, v_gdn_norm_w, v_sb_norm_w, v_w_out, v_attn_post_norm, v_ffn_pre_norm, v_w_ffn_up, v_ffn_conv_w, v_ffn_conv_b, v_w_ffn_down, v_ffn_post_norm):
    args = dict(locals())
    seq = x.shape[1]
    LP = -(-(ROW0 + seq) // LP_ALIGN) * LP_ALIGN
    tail = LP - ROW0 - seq

    gathered = _gather_two_level(
        [w_in[0].astype(BF16), w_out[0].astype(BF16), w_ffn_up[0].astype(BF16), w_ffn_down[0].astype(BF16),
         gdn_conv_w[0], ffn_conv_w[0], meta_tokens], name="gather_weights")
    win_o = _from_blocks_cols(gathered[0])
    o_ab = C_QKV
    o_z = o_ab + 2 * GDN_HEADS
    w_inp = jnp.concatenate([win_o[:, :C_QKV], win_o[:, o_z:o_z + C_Z], win_o[:, o_z + C_Z:],
                             win_o[:, o_ab:o_z], jnp.zeros((D_MODEL, C_AB - 2 * GDN_HEADS), BF16)], axis=1)
    w_out_f = gathered[1].reshape(D_MODEL, D_MODEL)
    w_up_f = _from_blocks_cols(gathered[2])
    w_down_f = gathered[3].reshape(D_FF, D_MODEL)
    gconv_f = _from_blocks_cols(gathered[4])
    fconv_f = _from_blocks_cols(gathered[5])
    meta_f = _from_blocks_cols(gathered[6])

    h0 = jnp.concatenate([jnp.zeros((PAD_ROWS, D_MODEL), F32), meta_f, x[0], jnp.zeros((tail, D_MODEL), F32)], axis=0)
    target = jnp.concatenate([jnp.zeros((ROW0, D_MODEL), F32), loss_target[0], jnp.zeros((tail, D_MODEL), F32)], axis=0)
    u = _prenorm_fwd(h0, attn_pre_norm)
    proj = _mm(u, w_inp, F32, "mm_in")
    qn, kn, vg, beta_b, g_b = _gdn_pre_fwd(proj, gconv_f, gdn_A_log, gdn_dt_bias)
    cu, cw, cqd, ckd, cqk, ct, cgl = _gdn_chunk_fwd(qn, kn, vg, beta_b, g_b)
    og, ssave = _gdn_scan_fwd(cu, cw, cqd, ckd, cqk, cgl)
    osb, ctot, sb_nrun = _sb_fwd(proj)
    snw = sb_norm_w.reshape(1, SB_HEADS * SB_DH)
    y = _attn_norm_fwd(og, proj, osb, gdn_norm_w, snw)
    mix = _mm(y, w_out_f, F32, "mm_out")
    h1, n2 = _resid_fwd(h0, mix, attn_post_norm, ffn_pre_norm)
    up = _mm(n2, w_up_f, F32, "mm_up")
    act = _convglu_fwd(up, fconv_f, ffn_conv_b)
    f = _mm(act, w_down_f, F32, "mm_down")
    loss_part, dout, df, d_fpost = _final(h1, f, ffn_post_norm, target, seq)
    loss = lax.psum(loss_part[0, 0], ("x", "y", "c"))

    d_wdown = _mm_tn(act, df, "mm_dw_down")
    dact = _mm(df, w_down_f.T, F32, "mm_dact")
    dup, d_fconv, d_fconvb = _convglu_bwd(up, fconv_f, ffn_conv_b, dact)
    d_wup = _mm_tn(n2, dup, "mm_dw_up")
    dn2 = _mm(dup, w_up_f.T, F32, "mm_dn2")
    dh1, dmix, d_fpre, d_apost = _resid_bwd(h1, mix, attn_post_norm, ffn_pre_norm, dout, dn2)
    d_wout = _mm_tn(y, dmix, "mm_dw_out")
    dy = _mm(dmix, w_out_f.T, F32, "mm_dy")
    dog, dz, dos, d_gnw, d_snw = _attn_norm_bwd(og, proj, osb, gdn_norm_w, snw, dy)
    dqs, dks, dvs = _sb_bwd(proj, ctot, sb_nrun, dos)
    du_, dw_, dqd_, dkd_, dqk_, dgl_ = _gdn_scan_bwd(cu, cw, cqd, ckd, cqk, cgl, ssave, dog)
    dqn, dkn, dvg, dbeta, dg = _gdn_chunk_bwd(qn, kn, vg, beta_b, g_b, ct, du_, dw_, dqd_, dkd_, dqk_, dgl_)
    dqkv, dab, d_gconv, d_gsc = _gdn_pre_bwd(proj, gconv_f, gdn_A_log, gdn_dt_bias, dqn, dkn, dvg, dbeta, dg)
    dproj = jnp.concatenate([dqkv.astype(BF16), dz.astype(BF16), dqs.astype(BF16), dks.astype(BF16),
                             dvs.astype(BF16), dab.astype(BF16)], axis=1)
    d_winp = _mm_tn(u, dproj, "mm_dw_in")
    du0 = _mm(dproj, w_inp.T, F32, "mm_du")
    dh0, d_apre = _prenorm_bwd(h0, attn_pre_norm, du0, dh1)
    grad_x = dh0[ROW0:ROW0 + seq][None]
    d_meta = dh0[PAD_ROWS:ROW0]

    d_win = jnp.concatenate([d_winp[:, :C_QKV], d_winp[:, OFF_AB:OFF_AB + 2 * GDN_HEADS],
                             d_winp[:, OFF_Z:OFF_Z + C_Z], d_winp[:, OFF_SB:OFF_SB + C_SB]], axis=1)
    small_grads = [d_apre, d_gsc[0:1, :GDN_HEADS], d_gsc[1:2, :GDN_HEADS], d_gnw, d_snw.reshape(1, SB_HEADS, SB_DH),
                   d_apost, d_fpre, d_fconvb, d_fpost]
    big_names = ("w_in", "w_out", "w_ffn_up", "w_ffn_down", "gdn_conv_w", "ffn_conv_w", "meta_tokens")
    sends = [_to_blocks_cols(d_win), d_wout.reshape(N_DEV, D_MODEL // N_DEV, D_MODEL), _to_blocks_cols(d_wup),
             d_wdown.reshape(N_DEV, D_FF // N_DEV, D_MODEL), _to_blocks_cols(d_gconv), _to_blocks_cols(d_fconv),
             _to_blocks_cols(d_meta)]
    my_c = lax.axis_index("c")
    halves = [s.reshape((N_CHIP, 2) + s.shape[1:]) for s in sends]
    mine = [lax.dynamic_index_in_dim(h, my_c, axis=1, keepdims=False) for h in halves]
    theirs = _swap_sibling([lax.dynamic_index_in_dim(h, 1 - my_c, axis=1, keepdims=False).astype(BF16) for h in halves],
                           name="grads_swap_sibling")
    chip_sums = [_add_halves(a, b, "grads_add_" + nm) for nm, a, b in zip(big_names, mine, theirs)]
    recv = _exchange_chips(chip_sums, name="grads_exchange_chips")
    slab_parts = _gather_direct([_pack_small(small_grads)], name="gather_small_grads")[0]

    res = {}
    for nm, parts in zip(big_names, recv):
        wloc = args[nm]
        shp = wloc.shape
        w2 = wloc.reshape(shp[-2], shp[-1])
        outs = _adamw(parts, w2, args["m_" + nm].reshape(w2.shape), args["v_" + nm].reshape(w2.shape), "adamw_" + nm)
        res[nm] = [o.reshape(shp) for o in outs]
    small_shapes = [args[nm].shape for nm in SMALL]
    outs = _adamw(slab_parts, _pack_small([args[nm] for nm in SMALL]), _pack_small([args["m_" + nm] for nm in SMALL]),
                  _pack_small([args["v_" + nm] for nm in SMALL]), "adamw_small")
    for k in range(4):
        for nm, val in zip(SMALL, _unpack_small(outs[k], small_shapes)):
            res.setdefault(nm, [None] * 4)[k] = val

    order = ("meta_tokens", "attn_pre_norm", "w_in", "gdn_conv_w", "gdn_A_log", "gdn_dt_bias", "gdn_norm_w",
             "sb_norm_w", "w_out", "attn_post_norm", "ffn_pre_norm", "w_ffn_up", "ffn_conv_w", "ffn_conv_b",
             "w_ffn_down", "ffn_post_norm")
    return (loss, grad_x, *[res[nm][0] for nm in order], *[res[nm][1] for nm in order],
            *[res[nm][2] for nm in order], *[res[nm][3] for nm in order])
```

```python
import functools

import jax
import jax.numpy as jnp
from jax import lax
from jax.experimental import pallas as pl
from jax.experimental.pallas import tpu as pltpu

F32 = jnp.float32
BF16 = jnp.bfloat16

D_MODEL = 1024
N_META = 16
GDN_HEADS = 4
GDN_D = 128
GDN_CHUNK = 64
GDN_CONV = 4
GDN_ROWS = 256
SB_HEADS = 8
SB_DH = 64
SB_BLOCK = 128
D_FF = 2816
FFN_CONV = 3
NORM_EPS = 1e-6
L2_EPS = 1e-6
LANE = 128
N_DEV = 8

PAD_ROWS = SB_BLOCK - N_META
ROW0 = SB_BLOCK
SB_SPAN = 512
SB_DEAD = -104.0
SB_SUB = 256
SB_QTILE = 256
LP_ALIGN = 256

C_QKV = 3 * GDN_HEADS * GDN_D
C_Z = GDN_HEADS * GDN_D
C_SB = 3 * SB_HEADS * SB_DH
C_AB = 256
OFF_Z = C_QKV
OFF_SB = OFF_Z + C_Z
OFF_AB = OFF_SB + C_SB
D_INP = OFF_AB + C_AB
D_IN = C_QKV + 2 * GDN_HEADS + C_Z + C_SB

ADAM_LR = 0.001
ADAM_B1 = 0.9
ADAM_B2 = 0.999
ADAM_EPS = 1e-08
ADAM_WD = 0.01
ADAM_STEP = 10

VMEM_LIMIT = 56 * 1024 * 1024
MESH = pl.DeviceIdType.MESH


def _cp(sem=None):
    kw = dict(vmem_limit_bytes=VMEM_LIMIT)
    if sem is not None:
        kw["dimension_semantics"] = sem
    return pltpu.CompilerParams(**kw)


def _tile(n, cap, unit=128):
    best = None
    t = unit
    while t <= min(n, cap):
        if n % t == 0:
            best = t
        t += unit
    assert best is not None, (n, cap, unit)
    return best


def _dot(a, b):
    return jnp.dot(a, b, preferred_element_type=F32)


def _dot_nt(a, b):
    return lax.dot_general(a, b, (((1,), (1,)), ((), ())), preferred_element_type=F32)


def _dot_tn(a, b):
    return lax.dot_general(a, b, (((0,), (0,)), ((), ())), preferred_element_type=F32)


def _split(x):
    hi = x.astype(BF16)
    lo = (x - hi.astype(F32)).astype(BF16)
    return hi, lo


def _dot1(a, b, f=_dot):
    return f(a.astype(BF16), b.astype(BF16))


def _dot3(a, b, f=_dot):
    ah, al = _split(a)
    bh, bl = _split(b)
    return f(ah, bh) + (f(ah, bl) + f(al, bh))


def _dot_exact_l(m_bf16, x, f=_dot):
    xh, xl = _split(x)
    return f(m_bf16, xh) + f(m_bf16, xl)


def _dot_exact_r(x, m_bf16, f=_dot):
    xh, xl = _split(x)
    return f(xh, m_bf16) + f(xl, m_bf16)


def _iota2(shape, dim):
    return lax.broadcasted_iota(jnp.int32, shape, dim)


def _sigmoid(x):
    return 1.0 / (1.0 + jnp.exp(-x))


def _softplus(x):
    return jnp.maximum(x, 0.0) + jnp.log(1.0 + jnp.exp(-jnp.abs(x)))


def _colsum(x):
    return jnp.sum(x, axis=0, keepdims=True)


def _rowsum(x):
    return jnp.sum(x, axis=-1, keepdims=True)


def _mm(a, b, out_dtype, name):
    M, K = a.shape
    K2, N = b.shape
    assert K == K2
    tm = _tile(M, 768)
    tn = _tile(N, max(128, (6 * 1024 * 1024) // (2 * K)))

    def body(a_ref, b_ref, o_ref):
        o_ref[...] = _dot(a_ref[...].astype(BF16), b_ref[...].astype(BF16)).astype(o_ref.dtype)

    return pl.pallas_call(
        body, name=name, grid=(N // tn, M // tm),
        in_specs=[pl.BlockSpec((tm, K), lambda j, i: (i, 0)), pl.BlockSpec((K, tn), lambda j, i: (0, j))],
        out_specs=pl.BlockSpec((tm, tn), lambda j, i: (i, j)),
        out_shape=jax.ShapeDtypeStruct((M, N), out_dtype),
        compiler_params=_cp(("parallel", "parallel")),
    )(a, b)


def _mm_tn(a, b, name):
    M, K = a.shape
    M2, N = b.shape
    assert M == M2
    tm = _tile(M, 768)
    tk = _tile(K, 1408)
    tn = _tile(N, 1408)

    def body(a_ref, b_ref, o_ref):
        @pl.when(pl.program_id(2) == 0)
        def _():
            o_ref[...] = jnp.zeros_like(o_ref)
        o_ref[...] += _dot_tn(a_ref[...].astype(BF16), b_ref[...].astype(BF16))

    return pl.pallas_call(
        body, name=name, grid=(K // tk, N // tn, M // tm),
        in_specs=[pl.BlockSpec((tm, tk), lambda i, j, m: (m, i)), pl.BlockSpec((tm, tn), lambda i, j, m: (m, j))],
        out_specs=pl.BlockSpec((tk, tn), lambda i, j, m: (i, j)),
        out_shape=jax.ShapeDtypeStruct((K, N), F32),
        compiler_params=_cp(("parallel", "parallel", "arbitrary")),
    )(a, b)


def _rms(x):
    return lax.rsqrt(jnp.mean(x * x, axis=-1, keepdims=True) + NORM_EPS)


def _rms_bwd(x, w, dy):
    r = _rms(x)
    n = x * r
    dyw = dy * w
    dx = r * (dyw - n * jnp.mean(dyw * n, axis=-1, keepdims=True))
    return dx, dy * n


def _prenorm_fwd(h0, w):
    LP, D = h0.shape
    T = _tile(LP, 512)

    def body(h_ref, w_ref, u_ref):
        h = h_ref[...]
        u_ref[...] = (h * _rms(h) * w_ref[...]).astype(BF16)

    return pl.pallas_call(
        body, name="prenorm_fwd", grid=(LP // T,),
        in_specs=[pl.BlockSpec((T, D), lambda i: (i, 0)), pl.BlockSpec((1, D), lambda i: (0, 0))],
        out_specs=pl.BlockSpec((T, D), lambda i: (i, 0)),
        out_shape=jax.ShapeDtypeStruct((LP, D), BF16),
        compiler_params=_cp(("parallel",)),
    )(h0, w)


def _prenorm_bwd(h0, w, du, dh1):
    LP, D = h0.shape
    T = _tile(LP, 512)

    def body(h_ref, w_ref, du_ref, dh1_ref, dh0_ref, dw_ref):
        @pl.when(pl.program_id(0) == 0)
        def _():
            dw_ref[...] = jnp.zeros_like(dw_ref)
        dx, dwn = _rms_bwd(h_ref[...], w_ref[...], du_ref[...])
        dh0_ref[...] = dh1_ref[...] + dx
        dw_ref[...] += _colsum(dwn)

    row = pl.BlockSpec((T, D), lambda i: (i, 0))
    vec = pl.BlockSpec((1, D), lambda i: (0, 0))
    return pl.pallas_call(
        body, name="prenorm_bwd", grid=(LP // T,),
        in_specs=[row, vec, row, row], out_specs=[row, vec],
        out_shape=[jax.ShapeDtypeStruct((LP, D), F32), jax.ShapeDtypeStruct((1, D), F32)],
        compiler_params=_cp(("arbitrary",)),
    )(h0, w, du, dh1)


def _causal_taps(ext, w_ref, width, start, rows):
    y = w_ref[width - 1:width, :] * ext[start:start + rows]
    for j in range(width - 1):
        y = y + w_ref[j:j + 1, :] * pltpu.roll(ext, width - 1 - j, 0)[start:start + rows]
    return y


def _shifted_rows(ext, shift, start, rows):
    return ext[start:start + rows] if shift == 0 else pltpu.roll(ext, shift, 0)[start:start + rows]


def _anticausal_taps(dy_ext, w_ref, width, rows):
    n = dy_ext.shape[0]
    dx = w_ref[width - 1:width, :] * dy_ext[0:rows]
    for j in range(width - 1):
        dx = dx + w_ref[j:j + 1, :] * pltpu.roll(dy_ext, n - (width - 1 - j), 0)[0:rows]
    return dx


def _gdn_gate_consts(alog_ref, dtb_ref, h):
    a_coef = -jnp.exp(alog_ref[0:1, h:h + 1])
    return a_coef, dtb_ref[0:1, h:h + 1]


def _gdn_pre_fwd(proj, conv_w, a_log, dt_bias):
    LP = proj.shape[0]
    T = _tile(LP, 256)
    C = C_QKV
    H = GDN_HEADS

    def body(x_ref, halo_ref, ab_ref, cw_ref, alog_ref, dtb_ref, q_ref, k_ref, v_ref, beta_ref, g_ref):
        i = pl.program_id(0)
        ext = jnp.concatenate([jnp.where(i > 0, halo_ref[...], 0.0), x_ref[...]], axis=0)
        y = _causal_taps(ext, cw_ref, GDN_CONV, 8, T)
        c = y * _sigmoid(y)
        for h in range(H):
            sl = slice(h * GDN_D, (h + 1) * GDN_D)
            cq = c[:, sl]
            q_ref[:, sl] = cq * lax.rsqrt(_rowsum(cq * cq) + L2_EPS) * (GDN_D ** -0.5)
            ck = c[:, 512 + h * GDN_D:512 + (h + 1) * GDN_D]
            k_ref[:, sl] = ck * lax.rsqrt(_rowsum(ck * ck) + L2_EPS)
        v_ref[...] = c[:, 1024:]
        ab = ab_ref[...]
        valid = (i * T + _iota2((T, 1), 0)) >= PAD_ROWS
        for h in range(H):
            sl = slice(h * GDN_D, (h + 1) * GDN_D)
            a_coef, dtb = _gdn_gate_consts(alog_ref, dtb_ref, h)
            g = jnp.where(valid, a_coef * _softplus(ab[:, h:h + 1] + dtb), 0.0)
            beta = jnp.where(valid, _sigmoid(ab[:, H + h:H + h + 1]), 0.0)
            g_ref[:, sl] = jnp.broadcast_to(g, (T, GDN_D))
            beta_ref[:, sl] = jnp.broadcast_to(beta, (T, GDN_D))

    t8 = T // 8
    row512 = pl.BlockSpec((T, 512), lambda i: (i, 0))
    small = lambda r, c: pl.BlockSpec((r, c), lambda i: (0, 0))
    out = jax.ShapeDtypeStruct((LP, 512), F32)
    return pl.pallas_call(
        body, name="gdn_pre_fwd", grid=(LP // T,),
        in_specs=[pl.BlockSpec((T, C), lambda i: (i, 0)),
                  pl.BlockSpec((8, C), lambda i: (jnp.maximum(i * t8 - 1, 0), 0)),
                  pl.BlockSpec((T, C_AB), lambda i: (i, OFF_AB // C_AB)),
                  small(GDN_CONV, C), small(1, H), small(1, H)],
        out_specs=[row512] * 5, out_shape=[out] * 5,
        compiler_params=_cp(("parallel",)),
    )(proj, proj, proj, conv_w, a_log, dt_bias)


def _gdn_pre_bwd(proj, conv_w, a_log, dt_bias, dq, dk, dv, dbeta, dg):
    LP = proj.shape[0]
    T = _tile(LP, 256)
    C = C_QKV
    H = GDN_HEADS
    TE = T + 8
    nt = LP // T

    def body(x_ref, xp_ref, xn_ref, ab_ref, cw_ref, alog_ref, dtb_ref,
             dq_ref, dqn_ref, dk_ref, dkn_ref, dv_ref, dvn_ref, dbeta_ref, dg_ref,
             dx_ref, dab_ref, dcw_ref, dsc_ref, dys):
        i = pl.program_id(0)

        @pl.when(i == 0)
        def _():
            dcw_ref[...] = jnp.zeros_like(dcw_ref)
            dsc_ref[...] = jnp.zeros_like(dsc_ref)

        last = i == nt - 1
        ext = jnp.concatenate([jnp.where(i > 0, xp_ref[...], 0.0), x_ref[...], jnp.where(last, 0.0, xn_ref[...])],
                              axis=0)
        y = _causal_taps(ext, cw_ref, GDN_CONV, 8, TE)
        sg = _sigmoid(y)
        c = y * sg
        nxt = lambda a_ref, b_ref: jnp.concatenate([a_ref[...], jnp.where(last, 0.0, b_ref[...])], axis=0)
        dqn = nxt(dq_ref, dqn_ref)
        dkn = nxt(dk_ref, dkn_ref)
        dvv = nxt(dv_ref, dvn_ref)
        for h in range(H):
            sl = slice(h * GDN_D, (h + 1) * GDN_D)
            cq = c[:, sl]
            rq = lax.rsqrt(_rowsum(cq * cq) + L2_EPS)
            nq = cq * rq
            dqh = dqn[:, sl]
            dys[:, sl] = (GDN_D ** -0.5) * rq * (dqh - nq * _rowsum(dqh * nq))
            sk = slice(512 + h * GDN_D, 512 + (h + 1) * GDN_D)
            ck = c[:, sk]
            rk = lax.rsqrt(_rowsum(ck * ck) + L2_EPS)
            nk = ck * rk
            dkh = dkn[:, sl]
            dys[:, sk] = rk * (dkh - nk * _rowsum(dkh * nk))
        dys[:, 1024:] = dvv
        dy = dys[...] * (sg * (1.0 + y * (1.0 - sg)))
        for j in range(GDN_CONV):
            dcw_ref[j:j + 1, :] += _colsum(dy[0:T, :] * _shifted_rows(ext, GDN_CONV - 1 - j, 8, T))
        dx_ref[...] = _anticausal_taps(dy, cw_ref, GDN_CONV, T)
        ab = ab_ref[...]
        valid = (i * T + _iota2((T, 1), 0)) >= PAD_ROWS
        lane = _iota2((T, C_AB), 1)
        lane1 = _iota2((1, 128), 1)
        dab = jnp.zeros((T, C_AB), F32)
        dsc_a = jnp.zeros((1, 128), F32)
        dsc_d = jnp.zeros((1, 128), F32)
        for h in range(H):
            a_coef, dtb = _gdn_gate_consts(alog_ref, dtb_ref, h)
            pre = ab[:, h:h + 1] + dtb
            dgh = jnp.where(valid, dg_ref[:, h * GDN_D:h * GDN_D + 1], 0.0)
            da = dgh * a_coef * _sigmoid(pre)
            beta = _sigmoid(ab[:, H + h:H + h + 1])
            db = jnp.where(valid, dbeta_ref[:, h * GDN_D:h * GDN_D + 1], 0.0) * beta * (1.0 - beta)
            dab = dab + jnp.where(lane == h, da, 0.0) + jnp.where(lane == H + h, db, 0.0)
            dsc_a = dsc_a + jnp.where(lane1 == h, _colsum(dgh * a_coef * _softplus(pre)), 0.0)
            dsc_d = dsc_d + jnp.where(lane1 == h, _colsum(da), 0.0)
        dab_ref[...] = dab
        dsc_ref[0:1, :] += dsc_a
        dsc_ref[1:2, :] += dsc_d

    t8 = T // 8
    nb8 = LP // 8
    prev8 = lambda w: pl.BlockSpec((8, w), lambda i: (jnp.maximum(i * t8 - 1, 0), 0))
    next8 = lambda w: pl.BlockSpec((8, w), lambda i: (jnp.minimum((i + 1) * t8, nb8 - 1), 0))
    row = lambda w: pl.BlockSpec((T, w), lambda i: (i, 0))
    small = lambda r, c: pl.BlockSpec((r, c), lambda i: (0, 0))
    return pl.pallas_call(
        body, name="gdn_pre_bwd", grid=(nt,),
        in_specs=[row(C), prev8(C), next8(C), pl.BlockSpec((T, C_AB), lambda i: (i, OFF_AB // C_AB)),
                  small(GDN_CONV, C), small(1, H), small(1, H),
                  row(512), next8(512), row(512), next8(512), row(512), next8(512), row(512), row(512)],
        out_specs=[row(C), row(C_AB), small(GDN_CONV, C), small(2, 128)],
        out_shape=[jax.ShapeDtypeStruct((LP, C), F32), jax.ShapeDtypeStruct((LP, C_AB), F32),
                   jax.ShapeDtypeStruct((GDN_CONV, C), F32), jax.ShapeDtypeStruct((2, 128), F32)],
        scratch_shapes=[pltpu.VMEM((TE, C), F32)],
        compiler_params=_cp(("arbitrary",)),
    )(proj, proj, proj, proj, conv_w, a_log, dt_bias, dq, dq, dk, dk, dv, dv, dbeta, dg)


def _tri_masks():
    r = _iota2((GDN_CHUNK, GDN_CHUNK), 0)
    c = _iota2((GDN_CHUNK, GDN_CHUNK), 1)
    return r >= c, r > c


def _gdn_chunk_common(q, k, v, beta, gb):
    incl, strict = _tri_masks()
    l_incl = incl.astype(BF16)
    gd = _dot_exact_l(l_incl, jnp.where(strict, gb[:, :GDN_CHUNK], 0.0))
    gc = _dot_exact_l(l_incl, gb)
    decay = jnp.where(incl, jnp.exp(jnp.where(incl, gd, 0.0)), 0.0)
    exp_g = jnp.exp(gc)
    g_last = gc[GDN_CHUNK - 1:GDN_CHUNK, :]
    kd_fac = jnp.exp(g_last - gc)
    gl = jnp.exp(g_last)
    kb = k * beta
    kk = _dot1(kb, k, _dot_nt)
    return dict(incl=incl, strict=strict, decay=decay, exp_g=exp_g, kd_fac=kd_fac, gl=gl, kb=kb, kk=kk,
                vb=v * beta, kbg=kb * exp_g)


def _interleave(gens):
    gens = list(gens)
    while gens:
        alive = []
        for g in gens:
            try:
                next(g)
                alive.append(g)
            except StopIteration:
                pass
        gens = alive


def _call_carrying(ex, body, nsteps, *, name, in_specs, out_specs, out_shape, operands, scratch_shapes=()):
    n_in, n_out, n_scr = len(in_specs), len(out_specs), len(scratch_shapes)
    n = ex.n if ex is not None else 0

    def full(*refs):
        o0 = n_in + n
        s0 = o0 + n_out + n
        ex_refs = (refs[n_in:o0], refs[o0 + n_out:s0], refs[s0 + n_scr:])
        step = pl.program_id(0)
        _carry_begin(ex, ex_refs, step, nsteps)
        body(*refs[:n_in], *refs[o0:o0 + n_out], *refs[s0:s0 + n_scr])
        _carry_end(ex, ex_refs, step, nsteps)

    res = pl.pallas_call(
        full, name=name, grid=(nsteps,),
        in_specs=list(in_specs) + [ANY_SPEC] * n, out_specs=list(out_specs) + [ANY_SPEC] * n,
        out_shape=list(out_shape) + (ex.out_shapes if ex is not None else []),
        scratch_shapes=list(scratch_shapes) + (ex.scratch if ex is not None else []),
        compiler_params=pltpu.CompilerParams(dimension_semantics=("arbitrary",), vmem_limit_bytes=VMEM_LIMIT,
                                             has_side_effects=ex is not None),
    )(*operands, *(ex.arrs if ex is not None else []))
    return list(res[:n_out]), list(res[n_out:])


def _gdn_chunk_fwd(qn, kn, v, beta_b, g_b, carry=None):
    LP = qn.shape[0]
    R = GDN_ROWS
    H = GDN_HEADS
    CH = GDN_CHUNK

    def body(q_ref, k_ref, v_ref, b_ref, g_ref, u_ref, w_ref, qd_ref, kd_ref, qk_ref, t_ref, gl_ref):
        def item(cc, h):
            rs = slice(cc * CH, (cc + 1) * CH)
            sl = slice(h * GDN_D, (h + 1) * GDN_D)
            s64 = slice(h * CH, (h + 1) * CH)
            q, k = q_ref[rs, sl], k_ref[rs, sl]
            m = _gdn_chunk_common(q, k, v_ref[rs, sl], b_ref[rs, sl], g_ref[rs, sl])
            qk_raw = _dot1(q, k, _dot_nt)
            yield
            a = jnp.where(m["strict"], m["kk"] * m["decay"], 0.0)
            eye = (_iota2((CH, CH), 0) == _iota2((CH, CH), 1)).astype(F32)
            t = eye - a
            p = _dot3(a, a)
            yield
            for _ in range(4):
                t = t + _dot3(t, p)
                p = _dot3(p, p)
                yield
            t = t + _dot3(t, p)
            yield
            u_ref[rs, sl] = _dot1(t, m["vb"])
            w_ref[rs, sl] = _dot1(t, m["kbg"])
            qk_ref[rs, s64] = qk_raw * m["decay"]
            t_ref[rs, s64] = t
            qd_ref[rs, sl] = q * m["exp_g"]
            kd_ref[rs, sl] = k * m["kd_fac"]
            gl_ref[cc * 8:(cc + 1) * 8, sl] = jnp.broadcast_to(m["gl"], (8, GDN_D))

        _interleave(item(cc, h) for cc in range(R // CH) for h in range(H))

    row = lambda w: pl.BlockSpec((R, w), lambda i: (i, 0))
    o512 = jax.ShapeDtypeStruct((LP, 512), F32)
    o256 = jax.ShapeDtypeStruct((LP, 256), F32)
    return _call_carrying(
        carry, body, LP // R, name="gdn_chunk_fwd",
        in_specs=[row(512)] * 5,
        out_specs=[row(512)] * 4 + [row(256)] * 2 + [pl.BlockSpec((R // 8, 512), lambda i: (i, 0))],
        out_shape=[o512] * 4 + [o256] * 2 + [jax.ShapeDtypeStruct((LP // 8, 512), F32)],
        operands=(qn, kn, v, beta_b, g_b))


def _gdn_chunk_bwd(qn, kn, v, beta_b, g_b, t_all, du, dw, dqd, dkd, dqk, dgl):
    LP = qn.shape[0]
    R = GDN_ROWS
    H = GDN_HEADS
    CH = GDN_CHUNK

    def body(q_ref, k_ref, v_ref, b_ref, g_ref, t_ref, du_ref, dw_ref, dqd_ref, dkd_ref, dqk_ref, dgl_ref,
             dq_ref, dk_ref, dv_ref, db_ref, dg_ref):
        ones = jnp.ones((CH, GDN_D), BF16)

        def item(cc, h):
            rs = slice(cc * CH, (cc + 1) * CH)
            sl = slice(h * GDN_D, (h + 1) * GDN_D)
            s64 = slice(h * CH, (h + 1) * CH)
            q, k, vv, beta = q_ref[rs, sl], k_ref[rs, sl], v_ref[rs, sl], b_ref[rs, sl]
            m = _gdn_chunk_common(q, k, vv, beta, g_ref[rs, sl])
            incl, strict, decay = m["incl"], m["strict"], m["decay"]
            t = t_ref[rs, s64]
            du_, dw_ = du_ref[rs, sl], dw_ref[rs, sl]
            dqd_, dkd_ = dqd_ref[rs, sl], dkd_ref[rs, sl]
            d_t = _dot1(du_, m["vb"], _dot_nt) + _dot1(dw_, m["kbg"], _dot_nt)
            dvb = _dot1(t, du_, _dot_tn)
            dkbg = _dot1(t, dw_, _dot_tn)
            qk_raw = _dot1(q, k, _dot_nt)
            yield
            x1 = _dot3(d_t, t, _dot_nt)
            dkb = dkbg * m["exp_g"]
            d_gi = _rowsum(dkbg * m["kbg"])
            yield
            d_a = jnp.where(strict, -_dot3(t, x1, _dot_tn), 0.0)
            yield
            d_kk = d_a * decay
            dqk_m = jnp.where(incl, dqk_ref[rs, s64], 0.0)
            dqk_raw = dqk_m * decay
            mm = (d_a * m["kk"] + dqk_m * qk_raw) * decay
            dkb = dkb + _dot1(d_kk, k)
            dk_ = _dot1(d_kk, m["kb"], _dot_tn) + _dot1(dqk_raw, q, _dot_tn)
            dq_ = _dot1(dqk_raw, k) + dqd_ * m["exp_g"]
            d_gi = d_gi + (_dot_exact_r(mm, ones) - _dot_exact_r(mm, ones, _dot_tn))
            yield
            d_gi = d_gi + _rowsum(dqd_ * q * m["exp_g"])
            e = _rowsum(dkd_ * k * m["kd_fac"])
            d_gi = d_gi - e
            d_glast = _colsum(jnp.broadcast_to(e, (CH, GDN_D))) + dgl_ref[cc * 8:cc * 8 + 1, sl] * m["gl"]
            dk_ = dk_ + dkd_ * m["kd_fac"] + dkb * beta
            d_gi = d_gi + jnp.where(_iota2((CH, GDN_D), 0) == CH - 1, d_glast, 0.0)
            u_incl = (_iota2((CH, CH), 1) >= _iota2((CH, CH), 0)).astype(BF16)
            dq_ref[rs, sl] = dq_
            dk_ref[rs, sl] = dk_
            dv_ref[rs, sl] = dvb * beta
            db_ref[rs, sl] = jnp.broadcast_to(_rowsum(dvb * vv) + _rowsum(dkb * k), (CH, GDN_D))
            dg_ref[rs, sl] = _dot_exact_l(u_incl, d_gi)

        _interleave(item(cc, h) for cc in range(R // CH) for h in range(H))

    row = lambda w: pl.BlockSpec((R, w), lambda i: (i, 0))
    o512 = jax.ShapeDtypeStruct((LP, 512), F32)
    gl_spec = pl.BlockSpec((R // 8, 512), lambda i: (i, 0))
    return pl.pallas_call(
        body, name="gdn_chunk_bwd", grid=(LP // R,),
        in_specs=[row(512)] * 5 + [row(256)] + [row(512)] * 4 + [row(256), gl_spec],
        out_specs=[row(512)] * 5, out_shape=[o512] * 5,
        compiler_params=_cp(("parallel",)),
    )(qn, kn, v, beta_b, g_b, t_all, du, dw, dqd, dkd, dqk, dgl)


def _gdn_scan_fwd(u, w, qd, kd, qk, gl):
    LP = u.shape[0]
    CH = GDN_CHUNK
    N = LP // CH
    H = GDN_HEADS

    def body(u_ref, w_ref, qd_ref, kd_ref, qk_ref, gl_ref, o_ref, ssave_ref, s_sc):
        @pl.when(pl.program_id(0) == 0)
        def _():
            s_sc[...] = jnp.zeros_like(s_sc)
        ssave_ref[...] = s_sc[...]

        def item(h):
            sl = slice(h * GDN_D, (h + 1) * GDN_D)
            s = s_sc[:, sl]
            v_new = u_ref[:, sl] - _dot1(w_ref[:, sl], s)
            o_s = _dot1(qd_ref[:, sl], s)
            yield
            o_ref[:, sl] = o_s + _dot1(qk_ref[:, h * CH:(h + 1) * CH], v_new)
            s_sc[:, sl] = s * gl_ref[0:1, sl] + _dot1(kd_ref[:, sl], v_new, _dot_tn)

        _interleave(item(h) for h in range(H))

    row = lambda w_: pl.BlockSpec((CH, w_), lambda n: (n, 0))
    return pl.pallas_call(
        body, name="gdn_scan_fwd", grid=(N,),
        in_specs=[row(512)] * 4 + [row(256), pl.BlockSpec((8, 512), lambda n: (n, 0))],
        out_specs=[row(512), pl.BlockSpec((GDN_D, 512), lambda n: (n, 0))],
        out_shape=[jax.ShapeDtypeStruct((LP, 512), F32), jax.ShapeDtypeStruct((N * GDN_D, 512), F32)],
        scratch_shapes=[pltpu.VMEM((GDN_D, 512), F32)],
        compiler_params=_cp(("arbitrary",)),
    )(u, w, qd, kd, qk, gl)


def _gdn_scan_bwd(u, w, qd, kd, qk, gl, ssave, do, carry=None):
    LP = u.shape[0]
    CH = GDN_CHUNK
    N = LP // CH
    H = GDN_HEADS

    def body(u_ref, w_ref, qd_ref, kd_ref, qk_ref, gl_ref, s_ref, do_ref,
             du_ref, dw_ref, dqd_ref, dkd_ref, dqk_ref, dgl_ref, ds_sc):
        @pl.when(pl.program_id(0) == 0)
        def _():
            ds_sc[...] = jnp.zeros_like(ds_sc)
        def item(h):
            sl = slice(h * GDN_D, (h + 1) * GDN_D)
            s64 = slice(h * CH, (h + 1) * CH)
            s = s_ref[:, sl]
            ds = ds_sc[:, sl]
            do_ = do_ref[:, sl]
            w_, qd_, kd_, qk_ = w_ref[:, sl], qd_ref[:, sl], kd_ref[:, sl], qk_ref[:, s64]
            v_new = u_ref[:, sl] - _dot1(w_, s)
            d_vnew = _dot1(qk_, do_, _dot_tn) + _dot1(kd_, ds)
            dqd_ref[:, sl] = _dot1(do_, s, _dot_nt)
            ds_new = ds * gl_ref[0:1, sl] + _dot1(qd_, do_, _dot_tn)
            dgl_ref[:, sl] = jnp.broadcast_to(jnp.sum(_colsum(ds * s), axis=-1, keepdims=True), (8, GDN_D))
            yield
            du_ref[:, sl] = d_vnew
            dw_ref[:, sl] = -_dot1(d_vnew, s, _dot_nt)
            dkd_ref[:, sl] = _dot1(v_new, ds, _dot_nt)
            dqk_ref[:, s64] = _dot1(do_, v_new, _dot_nt)
            ds_sc[:, sl] = ds_new - _dot1(w_, d_vnew, _dot_tn)

        _interleave(item(h) for h in range(H))

    rev = lambda w_: pl.BlockSpec((CH, w_), lambda n: (N - 1 - n, 0))
    rev8 = pl.BlockSpec((8, 512), lambda n: (N - 1 - n, 0))
    o512 = jax.ShapeDtypeStruct((LP, 512), F32)
    return _call_carrying(
        carry, body, N, name="gdn_scan_bwd",
        in_specs=[rev(512)] * 4 + [rev(256), rev8, pl.BlockSpec((GDN_D, 512), lambda n: (N - 1 - n, 0)), rev(512)],
        out_specs=[rev(512)] * 4 + [rev(256), rev8],
        out_shape=[o512] * 4 + [jax.ShapeDtypeStruct((LP, 256), F32), jax.ShapeDtypeStruct((LP // 8, 512), F32)],
        scratch_shapes=[pltpu.VMEM((GDN_D, 512), F32)],
        operands=(u, w, qd, kd, qk, gl, ssave, do))


def _sb_scores(qh, kblk, mask):
    z = _dot_nt(qh, kblk)
    e = jnp.exp(-jnp.abs(z))
    sp = jnp.maximum(z, 0.0) + jnp.log(1.0 + e)
    return z, e, jnp.where(mask, -sp, 0.0), z - sp


def _sb_fwd(proj):
    LP = proj.shape[0]
    B = SB_BLOCK
    W = min(SB_SPAN, LP)
    SUB = SB_SUB
    Q = min(SB_QTILE, LP)
    nq = LP // Q
    nsub = W // SUB
    scale = SB_DH ** -0.5
    qcol, kcol, vcol = OFF_SB // B, (OFF_SB + 512) // B, (OFF_SB + 1024) // B

    def body(q_ref, k_ref, v_ref, o_ref, c_ref, n_ref):
        i = pl.program_id(1)
        lane = _iota2((Q, B), 1)
        head_a = lane < SB_DH
        qs = q_ref[...] * scale
        qh = [jnp.where(head_a, qs, 0.0).astype(BF16), jnp.where(head_a, 0.0, qs).astype(BF16)]
        u_strict = (_iota2((SUB, SUB), 0) > _iota2((SUB, SUB), 1)).astype(BF16)
        qpos = i * Q + _iota2((Q, W), 0)
        hi0 = (i + 1) * Q
        nspan = (hi0 + W - 1) // W

        def live(st):
            return (st[0] < nspan) & (st[1] > 0)

        def span(st):
            r, carry = st[0], st[2:]
            hi = hi0 - r * W
            k0 = pl.multiple_of(jnp.maximum(hi - W, 0), B)
            kblk = k_ref[pl.ds(k0, W), :].astype(BF16)
            vblk = v_ref[pl.ds(k0, W), :].astype(BF16)
            kpos = k0 + _iota2((Q, W), 1)
            mask = (kpos < qpos) & (kpos >= PAD_ROWS) & (kpos < hi)
            new = [None] * 4

            def head(h):
                o_acc, c = carry[2 * h], carry[2 * h + 1]
                z, e, l1m, lsg = _sb_scores(qh[h], kblk, mask)
                yield
                subs = [slice(b * SUB, (b + 1) * SUB) for b in range(nsub)]
                suf = [_dot_exact_r(l1m[:, bs], u_strict) for bs in subs]
                yield
                parts = [None] * nsub
                for b in reversed(range(nsub)):
                    parts[b] = jnp.where(mask[:, subs[b]], jnp.exp(lsg[:, subs[b]] + suf[b] + c), 0.0)
                    c = c + _rowsum(l1m[:, subs[b]])
                att = jnp.concatenate(parts, axis=1).astype(BF16)
                new[2 * h], new[2 * h + 1] = o_acc + _dot(att, vblk), c

            _interleave(head(h) for h in range(2))
            more = (jnp.maximum(jnp.max(new[1]), jnp.max(new[3])) > SB_DEAD).astype(jnp.int32)
            return (r + 1, more, *new)

        zero_o = jnp.zeros((Q, B), F32)
        zero_c = jnp.zeros((Q, 1), F32)
        nrun, _, o_a, c_a, o_b, c_b = lax.while_loop(
            live, span, (jnp.int32(0), jnp.int32(1), zero_o, zero_c, zero_o, zero_c))
        o_ref[...] = jnp.where(head_a, o_a, o_b)
        c_ref[...] = jnp.where(head_a, c_a, c_b)
        n_ref[pl.program_id(0), i] = nrun

    blk = pl.BlockSpec((Q, B), lambda p, i: (i, p))
    out = jax.ShapeDtypeStruct((LP, 512), F32)
    return pl.pallas_call(
        body, name="sb_fwd", grid=(SB_HEADS // 2, nq),
        in_specs=[pl.BlockSpec((Q, B), lambda p, i: (i, qcol + p)),
                  pl.BlockSpec((LP, B), lambda p, i: (0, kcol + p)),
                  pl.BlockSpec((LP, B), lambda p, i: (0, vcol + p))],
        out_specs=[blk, blk, pl.BlockSpec(memory_space=pltpu.SMEM)],
        out_shape=[out, out, jax.ShapeDtypeStruct((SB_HEADS // 2, nq), jnp.int32)],
        compiler_params=_cp(("arbitrary", "arbitrary")),
    )(proj, proj, proj)


def _sb_bwd(proj, ctot, nrun_all, do):
    LP = proj.shape[0]
    B = SB_BLOCK
    W = min(SB_SPAN, LP)
    SUB = SB_SUB
    Q = min(SB_QTILE, LP)
    nq = LP // Q
    nsub = W // SUB
    scale = SB_DH ** -0.5
    qcol, kcol, vcol = OFF_SB // B, (OFF_SB + 512) // B, (OFF_SB + 1024) // B

    def body(n_ref, q_ref, k_ref, v_ref, c_ref, do_ref, dq_ref, dk_ref, dv_ref):
        i = pl.program_id(1)

        @pl.when(i == 0)
        def _():
            dk_ref[...] = jnp.zeros_like(dk_ref)
            dv_ref[...] = jnp.zeros_like(dv_ref)

        lane = _iota2((Q, B), 1)
        head_a = lane < SB_DH
        qs = q_ref[...] * scale
        qh = [jnp.where(head_a, qs, 0.0).astype(BF16), jnp.where(head_a, 0.0, qs).astype(BF16)]
        dof = do_ref[...]
        doh = [jnp.where(head_a, dof, 0.0).astype(BF16), jnp.where(head_a, 0.0, dof).astype(BF16)]
        cfull = c_ref[...]
        ctot_h = [cfull[:, 0:1], cfull[:, SB_DH:SB_DH + 1]]
        sub_r, sub_c = _iota2((SUB, SUB), 0), _iota2((SUB, SUB), 1)
        u_strict = (sub_r > sub_c).astype(BF16)
        l_strict = (sub_r < sub_c).astype(BF16)
        qpos = i * Q + _iota2((Q, W), 0)
        hi0 = (i + 1) * Q
        nrun = n_ref[pl.program_id(0), i]

        def span(t, carry):
            r = nrun - 1 - t
            hi = hi0 - r * W
            k0 = pl.multiple_of(jnp.maximum(hi - W, 0), B)
            kblk = k_ref[pl.ds(k0, W), :].astype(BF16)
            vblk = v_ref[pl.ds(k0, W), :].astype(BF16)
            kpos = k0 + _iota2((Q, W), 1)
            mask = (kpos < qpos) & (kpos >= PAD_ROWS) & (kpos < hi)
            new = [None] * 6
            dk_add, dv_add = [None, None], [None, None]
            subs = [slice(b * SUB, (b + 1) * SUB) for b in range(nsub)]

            def head(h):
                dq_acc, pre, ecar = carry[3 * h], carry[3 * h + 1], carry[3 * h + 2]
                z, e, l1m, lsg = _sb_scores(qh[h], kblk, mask)
                d_att = _dot_nt(doh[h], vblk)
                yield
                sig = jnp.where(z >= 0.0, 1.0, e) / (1.0 + e)
                suf = [_dot_exact_r(l1m[:, bs], u_strict) for bs in subs]
                yield
                att_parts, p_parts = [None] * nsub, [None] * nsub
                for b, bs in enumerate(subs):
                    pre = pre + _rowsum(l1m[:, bs])
                    att_parts[b] = jnp.where(mask[:, bs], jnp.exp(lsg[:, bs] + suf[b] + (ctot_h[h] - pre)), 0.0)
                    p_parts[b] = att_parts[b] * d_att[:, bs]
                pcum = [_dot_exact_r(p, l_strict) for p in p_parts]
                yield
                dz_parts = [None] * nsub
                for b, bs in enumerate(subs):
                    sg = sig[:, bs]
                    dz_parts[b] = jnp.where(mask[:, bs], p_parts[b] * (1.0 - sg) - sg * (ecar + pcum[b]), 0.0)
                    ecar = ecar + _rowsum(p_parts[b])
                att = jnp.concatenate(att_parts, axis=1).astype(BF16)
                dz = jnp.concatenate(dz_parts, axis=1).astype(BF16)
                new[3 * h:3 * h + 3] = [dq_acc + _dot(dz, kblk), pre, ecar]
                dk_add[h] = _dot_tn(dz, qh[h])
                dv_add[h] = _dot_tn(att, doh[h])

            _interleave(head(h) for h in range(2))
            dk_ref[pl.ds(k0, W), :] += dk_add[0] + dk_add[1]
            dv_ref[pl.ds(k0, W), :] += dv_add[0] + dv_add[1]
            return tuple(new)

        zero_o = jnp.zeros((Q, B), F32)
        zero_c = jnp.zeros((Q, 1), F32)
        res = lax.fori_loop(0, nrun, span, (zero_o, zero_c, zero_c, zero_o, zero_c, zero_c))
        dq_ref[...] = jnp.where(head_a, res[0], res[3]) * scale

    blk = pl.BlockSpec((Q, B), lambda p, i: (i, p))
    col = pl.BlockSpec((LP, B), lambda p, i: (0, p))
    out = jax.ShapeDtypeStruct((LP, 512), F32)
    return pl.pallas_call(
        body, name="sb_bwd", grid=(SB_HEADS // 2, nq),
        in_specs=[pl.BlockSpec(memory_space=pltpu.SMEM),
                  pl.BlockSpec((Q, B), lambda p, i: (i, qcol + p)),
                  pl.BlockSpec((LP, B), lambda p, i: (0, kcol + p)),
                  pl.BlockSpec((LP, B), lambda p, i: (0, vcol + p)),
                  blk, blk],
        out_specs=[blk, col, col], out_shape=[out, out, out],
        compiler_params=_cp(("arbitrary", "arbitrary")),
    )(nrun_all, proj, proj, proj, ctot, do)


def _sb_group_mean():
    r = jnp.right_shift(_iota2((512, 512), 0), 6)
    c = jnp.right_shift(_iota2((512, 512), 1), 6)
    return jnp.where(r == c, 1.0 / SB_DH, 0.0).astype(BF16)


def _attn_norm_fwd(og, proj, osb, gnw, snw):
    LP = og.shape[0]
    T = _tile(LP, 256)

    def body(og_ref, z_ref, os_ref, gnw_ref, snw_ref, y_ref):
        valid = (pl.program_id(0) * T + _iota2((T, 1), 0)) >= PAD_ROWS
        z = z_ref[...]
        zg = z * _sigmoid(z)
        for h in range(GDN_HEADS):
            sl = slice(h * GDN_D, (h + 1) * GDN_D)
            o = og_ref[:, sl]
            y = o * _rms(o) * gnw_ref[...] * zg[:, sl]
            y_ref[:, sl] = jnp.where(valid, y, 0.0).astype(BF16)
        o = os_ref[...]
        msq = _dot_exact_r(o * o, _sb_group_mean())
        y = o * lax.rsqrt(msq + NORM_EPS) * snw_ref[...]
        y_ref[:, 512:] = jnp.where(valid, y, 0.0).astype(BF16)

    row = pl.BlockSpec((T, 512), lambda i: (i, 0))
    return pl.pallas_call(
        body, name="attn_norm_fwd", grid=(LP // T,),
        in_specs=[row, pl.BlockSpec((T, 512), lambda i: (i, OFF_Z // 512)), row,
                  pl.BlockSpec((1, GDN_D), lambda i: (0, 0)), pl.BlockSpec((1, 512), lambda i: (0, 0))],
        out_specs=pl.BlockSpec((T, 1024), lambda i: (i, 0)),
        out_shape=jax.ShapeDtypeStruct((LP, 1024), BF16),
        compiler_params=_cp(("parallel",)),
    )(og, proj, osb, gnw, snw)


def _attn_norm_bwd(og, proj, osb, gnw, snw, dy, carry=None):
    LP = og.shape[0]
    T = _tile(LP, 256)

    def body(og_ref, z_ref, os_ref, gnw_ref, snw_ref, dy_ref, dog_ref, dz_ref, dos_ref, dgw_ref, dsw_ref):
        @pl.when(pl.program_id(0) == 0)
        def _():
            dgw_ref[...] = jnp.zeros_like(dgw_ref)
            dsw_ref[...] = jnp.zeros_like(dsw_ref)
        valid = (pl.program_id(0) * T + _iota2((T, 1), 0)) >= PAD_ROWS
        dy = jnp.where(valid, dy_ref[...], 0.0)
        z = z_ref[...]
        sg = _sigmoid(z)
        zg = z * sg
        dgw = jnp.zeros((1, GDN_D), F32)
        for h in range(GDN_HEADS):
            sl = slice(h * GDN_D, (h + 1) * GDN_D)
            o = og_ref[:, sl]
            dyh = dy[:, sl]
            dx, dwn = _rms_bwd(o, gnw_ref[...], dyh * zg[:, sl])
            dog_ref[:, sl] = dx
            dgw = dgw + _colsum(dwn)
            yn = o * _rms(o) * gnw_ref[...]
            dz_ref[:, sl] = dyh * yn * (sg[:, sl] * (1.0 + z[:, sl] * (1.0 - sg[:, sl])))
        dgw_ref[...] += dgw
        o = os_ref[...]
        gm = _sb_group_mean()
        r = lax.rsqrt(_dot_exact_r(o * o, gm) + NORM_EPS)
        n = o * r
        dys = dy[:, 512:]
        dyw = dys * snw_ref[...]
        dos_ref[...] = r * (dyw - n * _dot_exact_r(dyw * n, gm))
        dsw_ref[...] += _colsum(dys * n)

    row = pl.BlockSpec((T, 512), lambda i: (i, 0))
    gw = pl.BlockSpec((1, GDN_D), lambda i: (0, 0))
    sw = pl.BlockSpec((1, 512), lambda i: (0, 0))
    o512 = jax.ShapeDtypeStruct((LP, 512), F32)
    return _call_carrying(
        carry, body, LP // T, name="attn_norm_bwd",
        in_specs=[row, pl.BlockSpec((T, 512), lambda i: (i, OFF_Z // 512)), row, gw, sw,
                  pl.BlockSpec((T, 1024), lambda i: (i, 0))],
        out_specs=[row, row, row, gw, sw],
        out_shape=[o512, o512, o512, jax.ShapeDtypeStruct((1, GDN_D), F32), jax.ShapeDtypeStruct((1, 512), F32)],
        operands=(og, proj, osb, gnw, snw, dy))


def _resid_fwd(h0, mix, w_post, w_pre):
    LP, D = h0.shape
    T = _tile(LP, 512)

    def body(h0_ref, mix_ref, wp_ref, wf_ref, h1_ref, n2_ref):
        mix = mix_ref[...]
        h1 = h0_ref[...] + mix * _rms(mix) * wp_ref[...]
        h1_ref[...] = h1
        n2_ref[...] = (h1 * _rms(h1) * wf_ref[...]).astype(BF16)

    row = pl.BlockSpec((T, D), lambda i: (i, 0))
    vec = pl.BlockSpec((1, D), lambda i: (0, 0))
    return pl.pallas_call(
        body, name="resid_fwd", grid=(LP // T,),
        in_specs=[row, row, vec, vec], out_specs=[row, row],
        out_shape=[jax.ShapeDtypeStruct((LP, D), F32), jax.ShapeDtypeStruct((LP, D), BF16)],
        compiler_params=_cp(("parallel",)),
    )(h0, mix, w_post, w_pre)


def _resid_bwd(h1, mix, w_post, w_pre, dout, dn2):
    LP, D = h1.shape
    T = _tile(LP, 512)

    def body(h1_ref, mix_ref, wp_ref, wf_ref, dout_ref, dn2_ref, dh1_ref, dmix_ref, dwf_ref, dwp_ref):
        @pl.when(pl.program_id(0) == 0)
        def _():
            dwf_ref[...] = jnp.zeros_like(dwf_ref)
            dwp_ref[...] = jnp.zeros_like(dwp_ref)
        dx, dwn = _rms_bwd(h1_ref[...], wf_ref[...], dn2_ref[...])
        dh1 = dout_ref[...] + dx
        dh1_ref[...] = dh1
        dwf_ref[...] += _colsum(dwn)
        dmix, dwn2 = _rms_bwd(mix_ref[...], wp_ref[...], dh1)
        dmix_ref[...] = dmix.astype(BF16)
        dwp_ref[...] += _colsum(dwn2)

    row = pl.BlockSpec((T, D), lambda i: (i, 0))
    vec = pl.BlockSpec((1, D), lambda i: (0, 0))
    v = jax.ShapeDtypeStruct((1, D), F32)
    return pl.pallas_call(
        body, name="resid_bwd", grid=(LP // T,),
        in_specs=[row, row, vec, vec, row, row], out_specs=[row, row, vec, vec],
        out_shape=[jax.ShapeDtypeStruct((LP, D), F32), jax.ShapeDtypeStruct((LP, D), BF16), v, v],
        compiler_params=_cp(("arbitrary",)),
    )(h1, mix, w_post, w_pre, dout, dn2)


GELU_C = 0.7978845608028654
GELU_A = 0.044715


def _gelu_parts(x):
    t = jnp.tanh(GELU_C * (x + GELU_A * x * x * x))
    return 0.5 * x * (1.0 + t), t


def _convglu_fwd(up, conv_w, conv_b):
    LP, C = up.shape
    T = _tile(LP, 128)

    def body(x_ref, halo_ref, cw_ref, cb_ref, act_ref):
        i = pl.program_id(0)

        def conv(cols):
            ext = jnp.concatenate([jnp.where(i > 0, halo_ref[:, cols], 0.0), x_ref[:, cols]], axis=0)
            w = cw_ref[:, cols]
            return (w[2:3] * ext[8:] + w[1:2] * pltpu.roll(ext, 1, 0)[8:] + w[0:1] * pltpu.roll(ext, 2, 0)[8:]
                    + cb_ref[:, cols])

        for s in range(D_FF // LANE):
            gs = slice(s * LANE, (s + 1) * LANE)
            g, _ = _gelu_parts(conv(gs))
            act_ref[:, gs] = (g * conv(slice(D_FF + s * LANE, D_FF + (s + 1) * LANE))).astype(BF16)

    t8 = T // 8
    return pl.pallas_call(
        body, name="convglu_fwd", grid=(LP // T,),
        in_specs=[pl.BlockSpec((T, C), lambda i: (i, 0)),
                  pl.BlockSpec((8, C), lambda i: (jnp.maximum(i * t8 - 1, 0), 0)),
                  pl.BlockSpec((FFN_CONV, C), lambda i: (0, 0)), pl.BlockSpec((1, C), lambda i: (0, 0))],
        out_specs=pl.BlockSpec((T, D_FF), lambda i: (i, 0)),
        out_shape=jax.ShapeDtypeStruct((LP, D_FF), BF16),
        compiler_params=_cp(("parallel",)),
    )(up, up, conv_w, conv_b)


def _convglu_bwd(up, conv_w, conv_b, dact):
    LP, C = up.shape
    T = _tile(LP, 128)
    TE = T + 8
    nt = LP // T

    def body(x_ref, xp_ref, xn_ref, cw_ref, cb_ref, da_ref, dan_ref, dx_ref, dcw_ref, dcb_ref):
        i = pl.program_id(0)

        @pl.when(i == 0)
        def _():
            dcw_ref[...] = jnp.zeros_like(dcw_ref)
            dcb_ref[...] = jnp.zeros_like(dcb_ref)

        last = i == nt - 1

        def conv(cols):
            ext = jnp.concatenate([jnp.where(i > 0, xp_ref[:, cols], 0.0), x_ref[:, cols],
                                   jnp.where(last, 0.0, xn_ref[:, cols])], axis=0)
            sh = [ext[8:8 + TE], pltpu.roll(ext, 1, 0)[8:8 + TE], pltpu.roll(ext, 2, 0)[8:8 + TE]]
            w = cw_ref[:, cols]
            return w[2:3] * sh[0] + w[1:2] * sh[1] + w[0:1] * sh[2] + cb_ref[:, cols], sh, w

        def back(cols, dy, sh, w):
            dy_t = dy[0:T]
            dcb_ref[:, cols] += _colsum(dy_t)
            for j in range(FFN_CONV):
                dcw_ref[j:j + 1, cols] += _colsum(dy_t * sh[FFN_CONV - 1 - j][0:T])
            dx_ref[:, cols] = (w[2:3] * dy_t + w[1:2] * pltpu.roll(dy, TE - 1, 0)[0:T]
                               + w[0:1] * pltpu.roll(dy, TE - 2, 0)[0:T]).astype(BF16)

        for s in range(D_FF // LANE):
            gs = slice(s * LANE, (s + 1) * LANE)
            vs = slice(D_FF + s * LANE, D_FF + (s + 1) * LANE)
            gate, sh_g, w_g = conv(gs)
            val, sh_v, w_v = conv(vs)
            g, t = _gelu_parts(gate)
            dg_dx = 0.5 * (1.0 + t) + 0.5 * gate * (1.0 - t * t) * GELU_C * (1.0 + 3.0 * GELU_A * gate * gate)
            da = jnp.concatenate([da_ref[:, gs], jnp.where(last, 0.0, dan_ref[:, gs])], axis=0)
            back(gs, da * val * dg_dx, sh_g, w_g)
            back(vs, da * g, sh_v, w_v)

    t8 = T // 8
    nb8 = LP // 8
    prev8 = lambda w: pl.BlockSpec((8, w), lambda i: (jnp.maximum(i * t8 - 1, 0), 0))
    next8 = lambda w: pl.BlockSpec((8, w), lambda i: (jnp.minimum((i + 1) * t8, nb8 - 1), 0))
    row = lambda w: pl.BlockSpec((T, w), lambda i: (i, 0))
    small = lambda r: pl.BlockSpec((r, C), lambda i: (0, 0))
    return pl.pallas_call(
        body, name="convglu_bwd", grid=(nt,),
        in_specs=[row(C), prev8(C), next8(C), small(FFN_CONV), small(1), row(D_FF), next8(D_FF)],
        out_specs=[row(C), small(FFN_CONV), small(1)],
        out_shape=[jax.ShapeDtypeStruct((LP, C), BF16), jax.ShapeDtypeStruct((FFN_CONV, C), F32),
                   jax.ShapeDtypeStruct((1, C), F32)],
        compiler_params=_cp(("arbitrary",)),
    )(up, up, up, conv_w, conv_b, dact, dact)


def _final(h1, f, w_post, target, n_real):
    LP, D = h1.shape
    T = _tile(LP, 256)

    def body(h1_ref, f_ref, w_ref, t_ref, loss_ref, dout_ref, df_ref, dw_ref):
        @pl.when(pl.program_id(0) == 0)
        def _():
            loss_ref[...] = jnp.zeros_like(loss_ref)
            dw_ref[...] = jnp.zeros_like(dw_ref)
        rows = pl.program_id(0) * T + _iota2((T, 1), 0)
        real = (rows >= ROW0) & (rows < ROW0 + n_real)
        f = f_ref[...]
        out = h1_ref[...] + f * _rms(f) * w_ref[...]
        err = jnp.where(real, out - t_ref[...], 0.0)
        loss_ref[...] += 0.5 * jnp.sum(_colsum(jnp.mean(err * err, axis=-1, keepdims=True)), axis=-1, keepdims=True)
        dout = err * (1.0 / D)
        dout_ref[...] = dout
        dx, dwn = _rms_bwd(f, w_ref[...], dout)
        df_ref[...] = dx.astype(BF16)
        dw_ref[...] += _colsum(dwn)

    row = pl.BlockSpec((T, D), lambda i: (i, 0))
    vec = pl.BlockSpec((1, D), lambda i: (0, 0))
    return pl.pallas_call(
        body, name="final_loss", grid=(LP // T,),
        in_specs=[row, row, vec, row],
        out_specs=[pl.BlockSpec((1, 128), lambda i: (0, 0)), row, row, vec],
        out_shape=[jax.ShapeDtypeStruct((1, 128), F32), jax.ShapeDtypeStruct((LP, D), F32),
                   jax.ShapeDtypeStruct((LP, D), BF16), jax.ShapeDtypeStruct((1, D), F32)],
        compiler_params=_cp(("arbitrary",)),
    )(h1, f, w_post, target)


ANY_SPEC = pl.BlockSpec(memory_space=pl.ANY)
N_CHIP = 4


def _other_chips(x, y):
    return [(1 - x, y), (x, 1 - y), (1 - x, 1 - y)]


def _gather_direct(arrs, name):
    n = len(arrs)
    npeer = N_DEV - 1

    def body(*refs):
        ins, outs = refs[:n], refs[n:2 * n]
        send_sems, recv_sems, loc_sems = refs[2 * n:]
        x, y, c = lax.axis_index("x"), lax.axis_index("y"), lax.axis_index("c")
        me = 4 * x + 2 * y + c
        copies = []
        for a in range(n):
            for kk in range(1, N_DEV):
                px = 1 - x if kk & 4 else x
                py = 1 - y if kk & 2 else y
                pc = 1 - c if kk & 1 else c
                s = a * npeer + kk - 1
                cp = pltpu.make_async_remote_copy(src_ref=ins[a], dst_ref=outs[a].at[me], send_sem=send_sems.at[s],
                                                  recv_sem=recv_sems.at[s], device_id=(px, py, pc), device_id_type=MESH)
                cp.start()
                copies.append(cp)
            own = pltpu.make_async_copy(ins[a], outs[a].at[me], loc_sems.at[a])
            own.start()
            copies.append(own)
        for cp in copies:
            cp.wait()

    shapes = [jax.ShapeDtypeStruct((N_DEV,) + tuple(a.shape), a.dtype) for a in arrs]
    return pl.pallas_call(
        body, name=name, in_specs=[ANY_SPEC] * n, out_specs=[ANY_SPEC] * n, out_shape=shapes,
        scratch_shapes=[pltpu.SemaphoreType.DMA((n * npeer,)), pltpu.SemaphoreType.DMA((n * npeer,)),
                        pltpu.SemaphoreType.DMA((n,))],
        compiler_params=pltpu.CompilerParams(has_side_effects=True),
    )(*arrs)


class _Exchange:
    def __init__(self, arrs, out_shapes, scratch, start, finish, mid=None):
        self.arrs, self.out_shapes, self.scratch = list(arrs), list(out_shapes), list(scratch)
        self.start, self.finish, self.mid = start, finish, mid

    @property
    def n(self):
        return len(self.arrs)


def _run_exchange(ex, name):
    n = ex.n

    def body(*refs):
        ins, outs, sems = refs[:n], refs[n:2 * n], refs[2 * n:]
        ex.start(ins, outs, sems)
        if ex.mid is not None:
            ex.mid(ins, outs, sems)
        ex.finish(ins, outs, sems)

    return pl.pallas_call(
        body, name=name, in_specs=[ANY_SPEC] * n, out_specs=[ANY_SPEC] * n, out_shape=ex.out_shapes,
        scratch_shapes=ex.scratch, compiler_params=pltpu.CompilerParams(has_side_effects=True),
    )(*ex.arrs)


def _carry_begin(ex, refs, step, nsteps):
    if ex is None:
        return

    @pl.when(step == 0)
    def _():
        ex.start(*refs)

    if ex.mid is not None:
        @pl.when(step == min(nsteps - 1, (3 * nsteps) // 5))
        def _():
            ex.mid(*refs)


def _carry_end(ex, refs, step, nsteps):
    if ex is None:
        return

    @pl.when(step == nsteps - 1)
    def _():
        ex.finish(*refs)


def _gather_two_level(arrs):
    n = len(arrs)
    K = 7

    def env(ins, outs, sems):
        send_sems, recv_sems, loc_sems = sems
        x, y, c = lax.axis_index("x"), lax.axis_index("y"), lax.axis_index("c")

        def cp(a, k, src, slot, to):
            return pltpu.make_async_remote_copy(src_ref=src, dst_ref=outs[a].at[slot], send_sem=send_sems.at[a * K + k],
                                                recv_sem=recv_sems.at[a * K + k], device_id=to, device_id_type=MESH)

        me = 4 * x + 2 * y + c
        owns = [pltpu.make_async_copy(ins[a], outs[a].at[me], loc_sems.at[a]) for a in range(n)]
        first = []
        for a in range(n):
            first.append(cp(a, 0, ins[a], me, (x, y, 1 - c)))
            first += [cp(a, 1 + j, ins[a], me, (px, py, c)) for j, (px, py) in enumerate(_other_chips(x, y))]
        passed = []
        for j, (px, py) in enumerate(_other_chips(x, y)):
            slot = 4 * px + 2 * py + c
            passed += [(cp(a, 1 + j, ins[a], slot, (px, py, c)), cp(a, 4 + j, outs[a].at[slot], slot, (x, y, 1 - c)))
                       for a in range(n)]
        from_sib = []
        for a in range(n):
            from_sib.append(cp(a, 0, ins[a], 4 * x + 2 * y + (1 - c), (x, y, 1 - c)))
            from_sib += [cp(a, 4 + j, ins[a], 4 * px + 2 * py + (1 - c), (x, y, 1 - c))
                         for j, (px, py) in enumerate(_other_chips(x, y))]
        return owns, first, passed, from_sib

    def start(ins, outs, sems):
        owns, first, _, _ = env(ins, outs, sems)
        for cp in owns + first:
            cp.start()

    def mid(ins, outs, sems):
        _, _, passed, _ = env(ins, outs, sems)
        for arrival, fwd in passed:
            arrival.wait_recv()
            fwd.start()

    def finish(ins, outs, sems):
        owns, first, passed, from_sib = env(ins, outs, sems)
        for cp in from_sib:
            cp.wait_recv()
        for cp in first + [fwd for _, fwd in passed]:
            cp.wait_send()
        for cp in owns:
            cp.wait()

    shapes = [jax.ShapeDtypeStruct((N_DEV,) + tuple(a.shape), a.dtype) for a in arrs]
    scratch = [pltpu.SemaphoreType.DMA((n * K,)), pltpu.SemaphoreType.DMA((n * K,)), pltpu.SemaphoreType.DMA((n,))]
    return _Exchange(arrs, shapes, scratch, start, finish, mid)


def _swap_sibling(arrs):
    n = len(arrs)

    def copies(ins, outs, sems):
        send_sems, recv_sems = sems
        x, y, c = lax.axis_index("x"), lax.axis_index("y"), lax.axis_index("c")
        return [pltpu.make_async_remote_copy(src_ref=ins[a], dst_ref=outs[a], send_sem=send_sems.at[a],
                                             recv_sem=recv_sems.at[a], device_id=(x, y, 1 - c), device_id_type=MESH)
                for a in range(n)]

    def start(ins, outs, sems):
        for cp in copies(ins, outs, sems):
            cp.start()

    def finish(ins, outs, sems):
        for cp in copies(ins, outs, sems):
            cp.wait()

    shapes = [jax.ShapeDtypeStruct(tuple(a.shape), a.dtype) for a in arrs]
    return _Exchange(arrs, shapes, [pltpu.SemaphoreType.DMA((n,)), pltpu.SemaphoreType.DMA((n,))], start, finish)


def _exchange_chips(arrs):
    n = len(arrs)
    K = N_CHIP - 1

    def copies(ins, outs, sems):
        send_sems, recv_sems, loc_sems = sems
        x, y, c = lax.axis_index("x"), lax.axis_index("y"), lax.axis_index("c")
        mine = 2 * x + y
        out = []
        for a in range(n):
            out += [pltpu.make_async_remote_copy(src_ref=ins[a].at[2 * px + py], dst_ref=outs[a].at[mine],
                                                 send_sem=send_sems.at[a * K + j], recv_sem=recv_sems.at[a * K + j],
                                                 device_id=(px, py, c), device_id_type=MESH)
                    for j, (px, py) in enumerate(_other_chips(x, y))]
            out.append(pltpu.make_async_copy(ins[a].at[mine], outs[a].at[mine], loc_sems.at[a]))
        return out

    def start(ins, outs, sems):
        for cp in copies(ins, outs, sems):
            cp.start()

    def finish(ins, outs, sems):
        for cp in copies(ins, outs, sems):
            cp.wait()

    shapes = [jax.ShapeDtypeStruct(tuple(a.shape), a.dtype) for a in arrs]
    scratch = [pltpu.SemaphoreType.DMA((n * K,)), pltpu.SemaphoreType.DMA((n * K,)), pltpu.SemaphoreType.DMA((n,))]
    return _Exchange(arrs, shapes, scratch, start, finish)


def _add_halves(mine, theirs, name):
    _, R, C = mine.shape
    cap = max(16, ((2 * 1024 * 1024) // (4 * C * 10)) // 16 * 16)
    T = R if R <= cap else _tile(R, cap, 16)

    def body(a_ref, b_ref, o_ref):
        o_ref[...] = (a_ref[...] + b_ref[...].astype(F32)).astype(BF16)

    blk = pl.BlockSpec((N_CHIP, T, C), lambda i: (0, i, 0))
    return pl.pallas_call(
        body, name=name, grid=(R // T,), in_specs=[blk, blk], out_specs=blk,
        out_shape=jax.ShapeDtypeStruct(mine.shape, BF16), compiler_params=_cp(("parallel",)),
    )(mine, theirs)


def _adamw(parts, w, m, v, name):
    R, C = w.shape
    npart = parts.shape[0]
    cap = max(16, ((2 * 1024 * 1024) // (4 * C * 12)) // 16 * 16)
    T = R if R <= cap else _tile(R, cap, 16)

    def body(p_ref, w_ref, m_ref, v_ref, g_ref, d_ref, nm_ref, nv_ref):
        g = p_ref[0].astype(F32)
        for k in range(1, npart):
            g = g + p_ref[k].astype(F32)
        mm = ADAM_B1 * m_ref[...] + (1.0 - ADAM_B1) * g
        vv = ADAM_B2 * v_ref[...] + (1.0 - ADAM_B2) * (g * g)
        m_hat = mm / (1.0 - ADAM_B1 ** ADAM_STEP)
        v_hat = vv / (1.0 - ADAM_B2 ** ADAM_STEP)
        g_ref[...] = g
        d_ref[...] = -ADAM_LR * (m_hat / (jnp.sqrt(v_hat) + ADAM_EPS) + ADAM_WD * w_ref[...])
        nm_ref[...] = mm
        nv_ref[...] = vv

    row = pl.BlockSpec((T, C), lambda i: (i, 0))
    out = jax.ShapeDtypeStruct((R, C), F32)
    return pl.pallas_call(
        body, name=name, grid=(R // T,),
        in_specs=[pl.BlockSpec((npart, T, C), lambda i: (0, i, 0)), row, row, row],
        out_specs=[row] * 4, out_shape=[out] * 4,
        compiler_params=_cp(("parallel",)),
    )(parts, w, m, v)


SMALL = ("attn_pre_norm", "gdn_A_log", "gdn_dt_bias", "gdn_norm_w", "sb_norm_w", "attn_post_norm",
         "ffn_pre_norm", "ffn_conv_b", "ffn_post_norm")


def _pack_small(arrs):
    rows = []
    for a in arrs:
        flat = a.reshape(-1).astype(F32)
        n = -(-flat.shape[0] // 128) * 128
        rows.append(jnp.pad(flat, (0, n - flat.shape[0])).reshape(-1, 128))
    slab = jnp.concatenate(rows, axis=0)
    pad = (-slab.shape[0]) % 8
    return jnp.pad(slab, ((0, pad), (0, 0)))


def _unpack_small(slab, shapes):
    out, r = [], 0
    for shp in shapes:
        size = 1
        for s in shp:
            size *= s
        nr = -(-size // 128)
        out.append(slab[r:r + nr].reshape(-1)[:size].reshape(shp))
        r += nr
    return out


def _to_blocks_cols(a):
    R, C = a.shape
    return a.reshape(R, N_DEV, C // N_DEV).transpose(1, 0, 2)


def _from_blocks_cols(a):
    n, R, c = a.shape
    return a.transpose(1, 0, 2).reshape(R, n * c)


def kernel(x, meta_tokens, attn_pre_norm, w_in, gdn_conv_w, gdn_A_log, gdn_dt_bias, gdn_norm_w, sb_norm_w, w_out, attn_post_norm, ffn_pre_norm, w_ffn_up, ffn_conv_w, ffn_conv_b, w_ffn_down, ffn_post_norm, loss_target, m_meta_tokens, m_attn_pre_norm, m_w_in, m_gdn_conv_w, m_gdn_A_log, m_gdn_dt_bias, m_gdn_norm_w, m_sb_norm_w, m_w_out, m_attn_post_norm, m_ffn_pre_norm, m_w_ffn_up, m_ffn_conv_w, m_ffn_conv_b, m_w_ffn_down, m_ffn_post_norm, v_meta_tokens, v_attn_pre_norm, v_w_in, v_gdn_conv_w, v_gdn_A_log, v_gdn_dt_bias, v_gdn_norm_w, v_sb_norm_w, v_w_out, v_attn_post_norm, v_ffn_pre_norm, v_w_ffn_up, v_ffn_conv_w, v_ffn_conv_b, v_w_ffn_down, v_ffn_post_norm):
    args = dict(locals())
    seq = x.shape[1]
    LP = -(-(ROW0 + seq) // LP_ALIGN) * LP_ALIGN
    tail = LP - ROW0 - seq

    gathered = _run_exchange(_gather_two_level([w_in[0].astype(BF16), gdn_conv_w[0], meta_tokens]),
                             "gather_weights_first")
    gather_rest = _gather_two_level([w_out[0].astype(BF16), w_ffn_up[0].astype(BF16), w_ffn_down[0].astype(BF16),
                                     ffn_conv_w[0]])
    win_o = _from_blocks_cols(gathered[0])
    o_ab = C_QKV
    o_z = o_ab + 2 * GDN_HEADS
    w_inp = jnp.concatenate([win_o[:, :C_QKV], win_o[:, o_z:o_z + C_Z], win_o[:, o_z + C_Z:],
                             win_o[:, o_ab:o_z], jnp.zeros((D_MODEL, C_AB - 2 * GDN_HEADS), BF16)], axis=1)
    gconv_f = _from_blocks_cols(gathered[1])
    meta_f = _from_blocks_cols(gathered[2])

    h0 = jnp.concatenate([jnp.zeros((PAD_ROWS, D_MODEL), F32), meta_f, x[0], jnp.zeros((tail, D_MODEL), F32)], axis=0)
    target = jnp.concatenate([jnp.zeros((ROW0, D_MODEL), F32), loss_target[0], jnp.zeros((tail, D_MODEL), F32)], axis=0)
    u = _prenorm_fwd(h0, attn_pre_norm)
    proj = _mm(u, w_inp, F32, "mm_in")
    qn, kn, vg, beta_b, g_b = _gdn_pre_fwd(proj, gconv_f, gdn_A_log, gdn_dt_bias)
    (cu, cw, cqd, ckd, cqk, ct, cgl), rest = _gdn_chunk_fwd(qn, kn, vg, beta_b, g_b, carry=gather_rest)
    w_out_f = rest[0].reshape(D_MODEL, D_MODEL)
    w_up_f = _from_blocks_cols(rest[1])
    w_down_f = rest[2].reshape(D_FF, D_MODEL)
    fconv_f = _from_blocks_cols(rest[3])
    og, ssave = _gdn_scan_fwd(cu, cw, cqd, ckd, cqk, cgl)
    osb, ctot, sb_nrun = _sb_fwd(proj)
    snw = sb_norm_w.reshape(1, SB_HEADS * SB_DH)
    y = _attn_norm_fwd(og, proj, osb, gdn_norm_w, snw)
    mix = _mm(y, w_out_f, F32, "mm_out")
    h1, n2 = _resid_fwd(h0, mix, attn_post_norm, ffn_pre_norm)
    up = _mm(n2, w_up_f, F32, "mm_up")
    act = _convglu_fwd(up, fconv_f, ffn_conv_b)
    f = _mm(act, w_down_f, F32, "mm_down")
    loss_part, dout, df, d_fpost = _final(h1, f, ffn_post_norm, target, seq)
    loss = lax.psum(loss_part[0, 0], ("x", "y", "c"))

    d_wdown = _mm_tn(act, df, "mm_dw_down")
    dact = _mm(df, w_down_f.T, F32, "mm_dact")
    dup, d_fconv, d_fconvb = _convglu_bwd(up, fconv_f, ffn_conv_b, dact)
    d_wup = _mm_tn(n2, dup, "mm_dw_up")
    dn2 = _mm(dup, w_up_f.T, F32, "mm_dn2")
    dh1, dmix, d_fpre, d_apost = _resid_bwd(h1, mix, attn_post_norm, ffn_pre_norm, dout, dn2)
    d_wout = _mm_tn(y, dmix, "mm_dw_out")
    dy = _mm(dmix, w_out_f.T, F32, "mm_dy")
    my_c = lax.axis_index("c")

    def core_halves(blocks):
        halves = [s.reshape((N_CHIP, 2) + s.shape[1:]) for s in blocks]
        return ([lax.dynamic_index_in_dim(h, my_c, axis=1, keepdims=False) for h in halves],
                [lax.dynamic_index_in_dim(h, 1 - my_c, axis=1, keepdims=False).astype(BF16) for h in halves])

    early_names = ("w_out", "w_ffn_up", "w_ffn_down", "ffn_conv_w")
    e_mine, e_send = core_halves([d_wout.reshape(N_DEV, D_MODEL // N_DEV, D_MODEL), _to_blocks_cols(d_wup),
                                  d_wdown.reshape(N_DEV, D_FF // N_DEV, D_MODEL), _to_blocks_cols(d_fconv)])
    (dog, dz, dos, d_gnw, d_snw), e_theirs = _attn_norm_bwd(og, proj, osb, gdn_norm_w, snw, dy,
                                                            carry=_swap_sibling(e_send))
    e_sums = [_add_halves(a, b, "grads_add_" + nm) for nm, a, b in zip(early_names, e_mine, e_theirs)]
    dqs, dks, dvs = _sb_bwd(proj, ctot, sb_nrun, dos)
    (du_, dw_, dqd_, dkd_, dqk_, dgl_), e_recv = _gdn_scan_bwd(cu, cw, cqd, ckd, cqk, cgl, ssave, dog,
                                                               carry=_exchange_chips(e_sums))
    dqn, dkn, dvg, dbeta, dg = _gdn_chunk_bwd(qn, kn, vg, beta_b, g_b, ct, du_, dw_, dqd_, dkd_, dqk_, dgl_)
    dqkv, dab, d_gconv, d_gsc = _gdn_pre_bwd(proj, gconv_f, gdn_A_log, gdn_dt_bias, dqn, dkn, dvg, dbeta, dg)
    dproj = jnp.concatenate([dqkv.astype(BF16), dz.astype(BF16), dqs.astype(BF16), dks.astype(BF16),
                             dvs.astype(BF16), dab.astype(BF16)], axis=1)
    d_winp = _mm_tn(u, dproj, "mm_dw_in")
    du0 = _mm(dproj, w_inp.T, F32, "mm_du")
    dh0, d_apre = _prenorm_bwd(h0, attn_pre_norm, du0, dh1)
    grad_x = dh0[ROW0:ROW0 + seq][None]
    d_meta = dh0[PAD_ROWS:ROW0]

    d_win = jnp.concatenate([d_winp[:, :C_QKV], d_winp[:, OFF_AB:OFF_AB + 2 * GDN_HEADS],
                             d_winp[:, OFF_Z:OFF_Z + C_Z], d_winp[:, OFF_SB:OFF_SB + C_SB]], axis=1)
    small_grads = [d_apre, d_gsc[0:1, :GDN_HEADS], d_gsc[1:2, :GDN_HEADS], d_gnw, d_snw.reshape(1, SB_HEADS, SB_DH),
                   d_apost, d_fpre, d_fconvb, d_fpost]
    late_names = ("w_in", "gdn_conv_w", "meta_tokens")
    l_mine, l_send = core_halves([_to_blocks_cols(d_win), _to_blocks_cols(d_gconv), _to_blocks_cols(d_meta)])
    l_theirs = _run_exchange(_swap_sibling(l_send), "grads_swap_sibling")
    l_sums = [_add_halves(a, b, "grads_add_" + nm) for nm, a, b in zip(late_names, l_mine, l_theirs)]
    l_recv = _run_exchange(_exchange_chips(l_sums), "grads_exchange_chips")
    slab_parts = _gather_direct([_pack_small(small_grads)], name="gather_small_grads")[0]

    res = {}
    for nm, parts in zip(early_names + late_names, list(e_recv) + list(l_recv)):
        wloc = args[nm]
        shp = wloc.shape
        w2 = wloc.reshape(shp[-2], shp[-1])
        outs = _adamw(parts, w2, args["m_" + nm].reshape(w2.shape), args["v_" + nm].reshape(w2.shape), "adamw_" + nm)
        res[nm] = [o.reshape(shp) for o in outs]
    small_shapes = [args[nm].shape for nm in SMALL]
    outs = _adamw(slab_parts, _pack_small([args[nm] for nm in SMALL]), _pack_small([args["m_" + nm] for nm in SMALL]),
                  _pack_small([args["v_" + nm] for nm in SMALL]), "adamw_small")
    for k in range(4):
        for nm, val in zip(SMALL, _unpack_small(outs[k], small_shapes)):
            res.setdefault(nm, [None] * 4)[k] = val

    order = ("meta_tokens", "attn_pre_norm", "w_in", "gdn_conv_w", "gdn_A_log", "gdn_dt_bias", "gdn_norm_w",
             "sb_norm_w", "w_out", "attn_post_norm", "ffn_pre_norm", "w_ffn_up", "ffn_conv_w", "ffn_conv_b",
             "w_ffn_down", "ffn_post_norm")
    return (loss, grad_x, *[res[nm][0] for nm in order], *[res[nm][1] for nm in order],
            *[res[nm][2] for nm in order], *[res[nm][3] for nm in order])
```

```python
import functools

import jax
import jax.numpy as jnp
from jax import lax
from jax.experimental import pallas as pl
from jax.experimental.pallas import tpu as pltpu

F32 = jnp.float32
BF16 = jnp.bfloat16

D_MODEL = 1024
N_META = 16
GDN_HEADS = 4
GDN_D = 128
GDN_CHUNK = 64
GDN_CONV = 4
GDN_ROWS = 256
SB_HEADS = 8
SB_DH = 64
SB_BLOCK = 128
D_FF = 2816
FFN_CONV = 3
NORM_EPS = 1e-6
L2_EPS = 1e-6
LANE = 128
N_DEV = 8

PAD_ROWS = SB_BLOCK - N_META
ROW0 = SB_BLOCK
SB_SPAN = 512
SB_DEAD = -104.0
SB_SUB = 256
SB_QTILE = 256
LP_ALIGN = 256

C_QKV = 3 * GDN_HEADS * GDN_D
C_Z = GDN_HEADS * GDN_D
C_SB = 3 * SB_HEADS * SB_DH
C_AB = 256
OFF_Z = C_QKV
OFF_SB = OFF_Z + C_Z
OFF_AB = OFF_SB + C_SB
D_INP = OFF_AB + C_AB
D_IN = C_QKV + 2 * GDN_HEADS + C_Z + C_SB

ADAM_LR = 0.001
ADAM_B1 = 0.9
ADAM_B2 = 0.999
ADAM_EPS = 1e-08
ADAM_WD = 0.01
ADAM_STEP = 10

VMEM_LIMIT = 56 * 1024 * 1024
MESH = pl.DeviceIdType.MESH


def _cp(sem=None):
    kw = dict(vmem_limit_bytes=VMEM_LIMIT)
    if sem is not None:
        kw["dimension_semantics"] = sem
    return pltpu.CompilerParams(**kw)


def _tile(n, cap, unit=128):
    best = None
    t = unit
    while t <= min(n, cap):
        if n % t == 0:
            best = t
        t += unit
    assert best is not None, (n, cap, unit)
    return best


def _dot(a, b):
    return jnp.dot(a, b, preferred_element_type=F32)


def _dot_nt(a, b):
    return lax.dot_general(a, b, (((1,), (1,)), ((), ())), preferred_element_type=F32)


def _dot_tn(a, b):
    return lax.dot_general(a, b, (((0,), (0,)), ((), ())), preferred_element_type=F32)


def _split(x):
    hi = x.astype(BF16)
    lo = (x - hi.astype(F32)).astype(BF16)
    return hi, lo


def _dot1(a, b, f=_dot):
    return f(a.astype(BF16), b.astype(BF16))


def _dot3(a, b, f=_dot):
    ah, al = _split(a)
    bh, bl = _split(b)
    return f(ah, bh) + (f(ah, bl) + f(al, bh))


def _dot_exact_l(m_bf16, x, f=_dot):
    xh, xl = _split(x)
    return f(m_bf16, xh) + f(m_bf16, xl)


def _dot_exact_r(x, m_bf16, f=_dot):
    xh, xl = _split(x)
    return f(xh, m_bf16) + f(xl, m_bf16)


def _iota2(shape, dim):
    return lax.broadcasted_iota(jnp.int32, shape, dim)


def _sigmoid(x):
    return 1.0 / (1.0 + jnp.exp(-x))


def _softplus(x):
    return jnp.maximum(x, 0.0) + jnp.log(1.0 + jnp.exp(-jnp.abs(x)))


def _colsum(x):
    return jnp.sum(x, axis=0, keepdims=True)


def _rowsum(x):
    return jnp.sum(x, axis=-1, keepdims=True)


def _mm(a, b, out_dtype, name):
    M, K = a.shape
    K2, N = b.shape
    assert K == K2
    tm = _tile(M, 768)
    tn = _tile(N, max(128, (6 * 1024 * 1024) // (2 * K)))

    def body(a_ref, b_ref, o_ref):
        o_ref[...] = _dot(a_ref[...].astype(BF16), b_ref[...].astype(BF16)).astype(o_ref.dtype)

    return pl.pallas_call(
        body, name=name, grid=(N // tn, M // tm),
        in_specs=[pl.BlockSpec((tm, K), lambda j, i: (i, 0)), pl.BlockSpec((K, tn), lambda j, i: (0, j))],
        out_specs=pl.BlockSpec((tm, tn), lambda j, i: (i, j)),
        out_shape=jax.ShapeDtypeStruct((M, N), out_dtype),
        compiler_params=_cp(("parallel", "parallel")),
    )(a, b)


def _mm_tn(a, b, name):
    M, K = a.shape
    M2, N = b.shape
    assert M == M2
    tm = _tile(M, 1408)
    tk = _tile(K, 1408)
    tn = _tile(N, 1408)

    def body(a_ref, b_ref, o_ref):
        @pl.when(pl.program_id(2) == 0)
        def _():
            o_ref[...] = jnp.zeros_like(o_ref)
        o_ref[...] += _dot_tn(a_ref[...].astype(BF16), b_ref[...].astype(BF16))

    return pl.pallas_call(
        body, name=name, grid=(K // tk, N // tn, M // tm),
        in_specs=[pl.BlockSpec((tm, tk), lambda i, j, m: (m, i)), pl.BlockSpec((tm, tn), lambda i, j, m: (m, j))],
        out_specs=pl.BlockSpec((tk, tn), lambda i, j, m: (i, j)),
        out_shape=jax.ShapeDtypeStruct((K, N), F32),
        compiler_params=_cp(("parallel", "parallel", "arbitrary")),
    )(a, b)


def _rms(x):
    return lax.rsqrt(jnp.mean(x * x, axis=-1, keepdims=True) + NORM_EPS)


def _rms_bwd(x, w, dy):
    r = _rms(x)
    n = x * r
    dyw = dy * w
    dx = r * (dyw - n * jnp.mean(dyw * n, axis=-1, keepdims=True))
    return dx, dy * n


def _prenorm_fwd(h0, w, carry=None):
    LP, D = h0.shape
    T = _tile(LP, 512)

    def body(h_ref, w_ref, u_ref):
        h = h_ref[...]
        u_ref[...] = (h * _rms(h) * w_ref[...]).astype(BF16)

    return _call_carrying(
        carry, body, LP // T, name="prenorm_fwd",
        in_specs=[pl.BlockSpec((T, D), lambda i: (i, 0)), pl.BlockSpec((1, D), lambda i: (0, 0))],
        out_specs=[pl.BlockSpec((T, D), lambda i: (i, 0))],
        out_shape=[jax.ShapeDtypeStruct((LP, D), BF16)],
        operands=(h0, w))


def _prenorm_bwd(h0, w, du, dh1):
    LP, D = h0.shape
    T = _tile(LP, 512)

    def body(h_ref, w_ref, du_ref, dh1_ref, dh0_ref, dw_ref):
        @pl.when(pl.program_id(0) == 0)
        def _():
            dw_ref[...] = jnp.zeros_like(dw_ref)
        dx, dwn = _rms_bwd(h_ref[...], w_ref[...], du_ref[...])
        dh0_ref[...] = dh1_ref[...] + dx
        dw_ref[...] += _colsum(dwn)

    row = pl.BlockSpec((T, D), lambda i: (i, 0))
    vec = pl.BlockSpec((1, D), lambda i: (0, 0))
    return pl.pallas_call(
        body, name="prenorm_bwd", grid=(LP // T,),
        in_specs=[row, vec, row, row], out_specs=[row, vec],
        out_shape=[jax.ShapeDtypeStruct((LP, D), F32), jax.ShapeDtypeStruct((1, D), F32)],
        compiler_params=_cp(("arbitrary",)),
    )(h0, w, du, dh1)


def _causal_taps(ext, w_ref, width, start, rows):
    y = w_ref[width - 1:width, :] * ext[start:start + rows]
    for j in range(width - 1):
        y = y + w_ref[j:j + 1, :] * pltpu.roll(ext, width - 1 - j, 0)[start:start + rows]
    return y


def _shifted_rows(ext, shift, start, rows):
    return ext[start:start + rows] if shift == 0 else pltpu.roll(ext, shift, 0)[start:start + rows]


def _anticausal_taps(dy_ext, w_ref, width, rows):
    n = dy_ext.shape[0]
    dx = w_ref[width - 1:width, :] * dy_ext[0:rows]
    for j in range(width - 1):
        dx = dx + w_ref[j:j + 1, :] * pltpu.roll(dy_ext, n - (width - 1 - j), 0)[0:rows]
    return dx


def _gdn_gate_consts(alog_ref, dtb_ref, h):
    a_coef = -jnp.exp(alog_ref[0:1, h:h + 1])
    return a_coef, dtb_ref[0:1, h:h + 1]


def _gdn_pre_fwd(proj, conv_w, a_log, dt_bias, carry=None):
    LP = proj.shape[0]
    T = _tile(LP, 256)
    C = C_QKV
    H = GDN_HEADS

    def body(x_ref, halo_ref, ab_ref, cw_ref, alog_ref, dtb_ref, q_ref, k_ref, v_ref, beta_ref, g_ref):
        i = pl.program_id(0)
        ext = jnp.concatenate([jnp.where(i > 0, halo_ref[...], 0.0), x_ref[...]], axis=0)
        y = _causal_taps(ext, cw_ref, GDN_CONV, 8, T)
        c = y * _sigmoid(y)
        for h in range(H):
            sl = slice(h * GDN_D, (h + 1) * GDN_D)
            cq = c[:, sl]
            q_ref[:, sl] = cq * lax.rsqrt(_rowsum(cq * cq) + L2_EPS) * (GDN_D ** -0.5)
            ck = c[:, 512 + h * GDN_D:512 + (h + 1) * GDN_D]
            k_ref[:, sl] = ck * lax.rsqrt(_rowsum(ck * ck) + L2_EPS)
        v_ref[...] = c[:, 1024:]
        ab = ab_ref[...]
        valid = (i * T + _iota2((T, 1), 0)) >= PAD_ROWS
        for h in range(H):
            sl = slice(h * GDN_D, (h + 1) * GDN_D)
            a_coef, dtb = _gdn_gate_consts(alog_ref, dtb_ref, h)
            g = jnp.where(valid, a_coef * _softplus(ab[:, h:h + 1] + dtb), 0.0)
            beta = jnp.where(valid, _sigmoid(ab[:, H + h:H + h + 1]), 0.0)
            g_ref[:, sl] = jnp.broadcast_to(g, (T, GDN_D))
            beta_ref[:, sl] = jnp.broadcast_to(beta, (T, GDN_D))

    t8 = T // 8
    row512 = pl.BlockSpec((T, 512), lambda i: (i, 0))
    small = lambda r, c: pl.BlockSpec((r, c), lambda i: (0, 0))
    out = jax.ShapeDtypeStruct((LP, 512), F32)
    return _call_carrying(
        carry, body, LP // T, name="gdn_pre_fwd",
        in_specs=[pl.BlockSpec((T, C), lambda i: (i, 0)),
                  pl.BlockSpec((8, C), lambda i: (jnp.maximum(i * t8 - 1, 0), 0)),
                  pl.BlockSpec((T, C_AB), lambda i: (i, OFF_AB // C_AB)),
                  small(GDN_CONV, C), small(1, H), small(1, H)],
        out_specs=[row512] * 5, out_shape=[out] * 5,
        operands=(proj, proj, proj, conv_w, a_log, dt_bias))


def _gdn_pre_bwd(proj, conv_w, a_log, dt_bias, dq, dk, dv, dbeta, dg):
    LP = proj.shape[0]
    T = _tile(LP, 256)
    C = C_QKV
    H = GDN_HEADS
    TE = T + 8
    nt = LP // T

    def body(x_ref, xp_ref, xn_ref, ab_ref, cw_ref, alog_ref, dtb_ref,
             dq_ref, dqn_ref, dk_ref, dkn_ref, dv_ref, dvn_ref, dbeta_ref, dg_ref,
             dx_ref, dab_ref, dcw_ref, dsc_ref, dys):
        i = pl.program_id(0)

        @pl.when(i == 0)
        def _():
            dcw_ref[...] = jnp.zeros_like(dcw_ref)
            dsc_ref[...] = jnp.zeros_like(dsc_ref)

        last = i == nt - 1
        ext = jnp.concatenate([jnp.where(i > 0, xp_ref[...], 0.0), x_ref[...], jnp.where(last, 0.0, xn_ref[...])],
                              axis=0)
        y = _causal_taps(ext, cw_ref, GDN_CONV, 8, TE)
        sg = _sigmoid(y)
        c = y * sg
        nxt = lambda a_ref, b_ref: jnp.concatenate([a_ref[...], jnp.where(last, 0.0, b_ref[...])], axis=0)
        dqn = nxt(dq_ref, dqn_ref)
        dkn = nxt(dk_ref, dkn_ref)
        dvv = nxt(dv_ref, dvn_ref)
        for h in range(H):
            sl = slice(h * GDN_D, (h + 1) * GDN_D)
            cq = c[:, sl]
            rq = lax.rsqrt(_rowsum(cq * cq) + L2_EPS)
            nq = cq * rq
            dqh = dqn[:, sl]
            dys[:, sl] = (GDN_D ** -0.5) * rq * (dqh - nq * _rowsum(dqh * nq))
            sk = slice(512 + h * GDN_D, 512 + (h + 1) * GDN_D)
            ck = c[:, sk]
            rk = lax.rsqrt(_rowsum(ck * ck) + L2_EPS)
            nk = ck * rk
            dkh = dkn[:, sl]
            dys[:, sk] = rk * (dkh - nk * _rowsum(dkh * nk))
        dys[:, 1024:] = dvv
        dy = dys[...] * (sg * (1.0 + y * (1.0 - sg)))
        for j in range(GDN_CONV):
            dcw_ref[j:j + 1, :] += _colsum(dy[0:T, :] * _shifted_rows(ext, GDN_CONV - 1 - j, 8, T))
        dx_ref[...] = _anticausal_taps(dy, cw_ref, GDN_CONV, T).astype(BF16)
        ab = ab_ref[...]
        valid = (i * T + _iota2((T, 1), 0)) >= PAD_ROWS
        lane = _iota2((T, C_AB), 1)
        lane1 = _iota2((1, 128), 1)
        dab = jnp.zeros((T, C_AB), F32)
        dsc_a = jnp.zeros((1, 128), F32)
        dsc_d = jnp.zeros((1, 128), F32)
        for h in range(H):
            a_coef, dtb = _gdn_gate_consts(alog_ref, dtb_ref, h)
            pre = ab[:, h:h + 1] + dtb
            dgh = jnp.where(valid, dg_ref[:, h * GDN_D:h * GDN_D + 1], 0.0)
            da = dgh * a_coef * _sigmoid(pre)
            beta = _sigmoid(ab[:, H + h:H + h + 1])
            db = jnp.where(valid, dbeta_ref[:, h * GDN_D:h * GDN_D + 1], 0.0) * beta * (1.0 - beta)
            dab = dab + jnp.where(lane == h, da, 0.0) + jnp.where(lane == H + h, db, 0.0)
            dsc_a = dsc_a + jnp.where(lane1 == h, _colsum(dgh * a_coef * _softplus(pre)), 0.0)
            dsc_d = dsc_d + jnp.where(lane1 == h, _colsum(da), 0.0)
        dab_ref[...] = dab.astype(BF16)
        dsc_ref[0:1, :] += dsc_a
        dsc_ref[1:2, :] += dsc_d

    t8 = T // 8
    nb8 = LP // 8
    prev8 = lambda w: pl.BlockSpec((8, w), lambda i: (jnp.maximum(i * t8 - 1, 0), 0))
    next8 = lambda w: pl.BlockSpec((8, w), lambda i: (jnp.minimum((i + 1) * t8, nb8 - 1), 0))
    row = lambda w: pl.BlockSpec((T, w), lambda i: (i, 0))
    small = lambda r, c: pl.BlockSpec((r, c), lambda i: (0, 0))
    return pl.pallas_call(
        body, name="gdn_pre_bwd", grid=(nt,),
        in_specs=[row(C), prev8(C), next8(C), pl.BlockSpec((T, C_AB), lambda i: (i, OFF_AB // C_AB)),
                  small(GDN_CONV, C), small(1, H), small(1, H),
                  row(512), next8(512), row(512), next8(512), row(512), next8(512), row(512), row(512)],
        out_specs=[row(C), row(C_AB), small(GDN_CONV, C), small(2, 128)],
        out_shape=[jax.ShapeDtypeStruct((LP, C), BF16), jax.ShapeDtypeStruct((LP, C_AB), BF16),
                   jax.ShapeDtypeStruct((GDN_CONV, C), F32), jax.ShapeDtypeStruct((2, 128), F32)],
        scratch_shapes=[pltpu.VMEM((TE, C), F32)],
        compiler_params=_cp(("arbitrary",)),
    )(proj, proj, proj, proj, conv_w, a_log, dt_bias, dq, dq, dk, dk, dv, dv, dbeta, dg)


def _tri_masks():
    r = _iota2((GDN_CHUNK, GDN_CHUNK), 0)
    c = _iota2((GDN_CHUNK, GDN_CHUNK), 1)
    return r >= c, r > c


def _gdn_chunk_common(q, k, v, beta, gb):
    incl, strict = _tri_masks()
    l_incl = incl.astype(BF16)
    gd = _dot_exact_l(l_incl, jnp.where(strict, gb[:, :GDN_CHUNK], 0.0))
    gc = _dot_exact_l(l_incl, gb)
    decay = jnp.where(incl, jnp.exp(jnp.where(incl, gd, 0.0)), 0.0)
    exp_g = jnp.exp(gc)
    g_last = gc[GDN_CHUNK - 1:GDN_CHUNK, :]
    kd_fac = jnp.exp(g_last - gc)
    gl = jnp.exp(g_last)
    kb = k * beta
    kk = _dot1(kb, k, _dot_nt)
    return dict(incl=incl, strict=strict, decay=decay, exp_g=exp_g, kd_fac=kd_fac, gl=gl, kb=kb, kk=kk,
                vb=v * beta, kbg=kb * exp_g)


def _interleave(gens):
    gens = list(gens)
    while gens:
        alive = []
        for g in gens:
            try:
                next(g)
                alive.append(g)
            except StopIteration:
                pass
        gens = alive


def _call_carrying(ex, body, nsteps, *, name, in_specs, out_specs, out_shape, operands, scratch_shapes=()):
    n_in, n_out, n_scr = len(in_specs), len(out_specs), len(scratch_shapes)
    n = ex.n if ex is not None else 0

    def full(*refs):
        o0 = n_in + n
        s0 = o0 + n_out + n
        ex_refs = (refs[n_in:o0], refs[o0 + n_out:s0], refs[s0 + n_scr:])
        step = pl.program_id(0)
        _carry_begin(ex, ex_refs, step, nsteps)
        body(*refs[:n_in], *refs[o0:o0 + n_out], *refs[s0:s0 + n_scr])
        _carry_end(ex, ex_refs, step, nsteps)

    res = pl.pallas_call(
        full, name=name, grid=(nsteps,),
        in_specs=list(in_specs) + [ANY_SPEC] * n, out_specs=list(out_specs) + [ANY_SPEC] * n,
        out_shape=list(out_shape) + (ex.out_shapes if ex is not None else []),
        scratch_shapes=list(scratch_shapes) + (ex.scratch if ex is not None else []),
        compiler_params=pltpu.CompilerParams(dimension_semantics=("arbitrary",), vmem_limit_bytes=VMEM_LIMIT,
                                             has_side_effects=ex is not None),
    )(*operands, *(ex.arrs if ex is not None else []))
    return list(res[:n_out]), list(res[n_out:])


def _gdn_chunk_fwd(qn, kn, v, beta_b, g_b, carry=None):
    LP = qn.shape[0]
    R = GDN_ROWS
    H = GDN_HEADS
    CH = GDN_CHUNK

    def body(q_ref, k_ref, v_ref, b_ref, g_ref, u_ref, w_ref, qd_ref, kd_ref, qk_ref, t_ref, gl_ref):
        def item(cc, h):
            rs = slice(cc * CH, (cc + 1) * CH)
            sl = slice(h * GDN_D, (h + 1) * GDN_D)
            s64 = slice(h * CH, (h + 1) * CH)
            q, k = q_ref[rs, sl], k_ref[rs, sl]
            m = _gdn_chunk_common(q, k, v_ref[rs, sl], b_ref[rs, sl], g_ref[rs, sl])
            qk_raw = _dot1(q, k, _dot_nt)
            yield
            a = jnp.where(m["strict"], m["kk"] * m["decay"], 0.0)
            eye = (_iota2((CH, CH), 0) == _iota2((CH, CH), 1)).astype(F32)
            t = eye - a
            p = _dot3(a, a)
            yield
            for _ in range(4):
                t = t + _dot3(t, p)
                p = _dot3(p, p)
                yield
            t = t + _dot3(t, p)
            yield
            u_ref[rs, sl] = _dot1(t, m["vb"])
            w_ref[rs, sl] = _dot1(t, m["kbg"])
            qk_ref[rs, s64] = qk_raw * m["decay"]
            t_ref[rs, s64] = t
            qd_ref[rs, sl] = q * m["exp_g"]
            kd_ref[rs, sl] = k * m["kd_fac"]
            gl_ref[cc * 8:(cc + 1) * 8, sl] = jnp.broadcast_to(m["gl"], (8, GDN_D))

        _interleave(item(cc, h) for cc in range(R // CH) for h in range(H))

    row = lambda w: pl.BlockSpec((R, w), lambda i: (i, 0))
    o512 = jax.ShapeDtypeStruct((LP, 512), F32)
    o256 = jax.ShapeDtypeStruct((LP, 256), F32)
    return _call_carrying(
        carry, body, LP // R, name="gdn_chunk_fwd",
        in_specs=[row(512)] * 5,
        out_specs=[row(512)] * 4 + [row(256)] * 2 + [pl.BlockSpec((R // 8, 512), lambda i: (i, 0))],
        out_shape=[o512] * 4 + [o256] * 2 + [jax.ShapeDtypeStruct((LP // 8, 512), F32)],
        operands=(qn, kn, v, beta_b, g_b))


def _gdn_chunk_bwd(qn, kn, v, beta_b, g_b, t_all, du, dw, dqd, dkd, dqk, dgl):
    LP = qn.shape[0]
    R = GDN_ROWS
    H = GDN_HEADS
    CH = GDN_CHUNK

    def body(q_ref, k_ref, v_ref, b_ref, g_ref, t_ref, du_ref, dw_ref, dqd_ref, dkd_ref, dqk_ref, dgl_ref,
             dq_ref, dk_ref, dv_ref, db_ref, dg_ref):
        ones = jnp.ones((CH, GDN_D), BF16)

        def item(cc, h):
            rs = slice(cc * CH, (cc + 1) * CH)
            sl = slice(h * GDN_D, (h + 1) * GDN_D)
            s64 = slice(h * CH, (h + 1) * CH)
            q, k, vv, beta = q_ref[rs, sl], k_ref[rs, sl], v_ref[rs, sl], b_ref[rs, sl]
            m = _gdn_chunk_common(q, k, vv, beta, g_ref[rs, sl])
            incl, strict, decay = m["incl"], m["strict"], m["decay"]
            t = t_ref[rs, s64]
            du_, dw_ = du_ref[rs, sl], dw_ref[rs, sl]
            dqd_, dkd_ = dqd_ref[rs, sl], dkd_ref[rs, sl]
            d_t = _dot1(du_, m["vb"], _dot_nt) + _dot1(dw_, m["kbg"], _dot_nt)
            dvb = _dot1(t, du_, _dot_tn)
            dkbg = _dot1(t, dw_, _dot_tn)
            qk_raw = _dot1(q, k, _dot_nt)
            yield
            x1 = _dot3(d_t, t, _dot_nt)
            dkb = dkbg * m["exp_g"]
            d_gi = _rowsum(dkbg * m["kbg"])
            yield
            d_a = jnp.where(strict, -_dot3(t, x1, _dot_tn), 0.0)
            yield
            d_kk = d_a * decay
            dqk_m = jnp.where(incl, dqk_ref[rs, s64], 0.0)
            dqk_raw = dqk_m * decay
            mm = (d_a * m["kk"] + dqk_m * qk_raw) * decay
            dkb = dkb + _dot1(d_kk, k)
            dk_ = _dot1(d_kk, m["kb"], _dot_tn) + _dot1(dqk_raw, q, _dot_tn)
            dq_ = _dot1(dqk_raw, k) + dqd_ * m["exp_g"]
            d_gi = d_gi + (_dot_exact_r(mm, ones) - _dot_exact_r(mm, ones, _dot_tn))
            yield
            d_gi = d_gi + _rowsum(dqd_ * q * m["exp_g"])
            e = _rowsum(dkd_ * k * m["kd_fac"])
            d_gi = d_gi - e
            d_glast = _colsum(jnp.broadcast_to(e, (CH, GDN_D))) + dgl_ref[cc * 8:cc * 8 + 1, sl] * m["gl"]
            dk_ = dk_ + dkd_ * m["kd_fac"] + dkb * beta
            d_gi = d_gi + jnp.where(_iota2((CH, GDN_D), 0) == CH - 1, d_glast, 0.0)
            u_incl = (_iota2((CH, CH), 1) >= _iota2((CH, CH), 0)).astype(BF16)
            dq_ref[rs, sl] = dq_
            dk_ref[rs, sl] = dk_
            dv_ref[rs, sl] = dvb * beta
            db_ref[rs, sl] = jnp.broadcast_to(_rowsum(dvb * vv) + _rowsum(dkb * k), (CH, GDN_D))
            dg_ref[rs, sl] = _dot_exact_l(u_incl, d_gi)

        _interleave(item(cc, h) for cc in range(R // CH) for h in range(H))

    row = lambda w: pl.BlockSpec((R, w), lambda i: (i, 0))
    o512 = jax.ShapeDtypeStruct((LP, 512), F32)
    gl_spec = pl.BlockSpec((R // 8, 512), lambda i: (i, 0))
    return pl.pallas_call(
        body, name="gdn_chunk_bwd", grid=(LP // R,),
        in_specs=[row(512)] * 5 + [row(256)] + [row(512)] * 4 + [row(256), gl_spec],
        out_specs=[row(512)] * 5, out_shape=[o512] * 5,
        compiler_params=_cp(("parallel",)),
    )(qn, kn, v, beta_b, g_b, t_all, du, dw, dqd, dkd, dqk, dgl)


def _gdn_scan_fwd(u, w, qd, kd, qk, gl):
    LP = u.shape[0]
    CH = GDN_CHUNK
    N = LP // CH
    H = GDN_HEADS

    def body(u_ref, w_ref, qd_ref, kd_ref, qk_ref, gl_ref, o_ref, ssave_ref, s_sc):
        @pl.when(pl.program_id(0) == 0)
        def _():
            s_sc[...] = jnp.zeros_like(s_sc)
        ssave_ref[...] = s_sc[...]

        def item(h):
            sl = slice(h * GDN_D, (h + 1) * GDN_D)
            s = s_sc[:, sl]
            v_new = u_ref[:, sl] - _dot1(w_ref[:, sl], s)
            o_s = _dot1(qd_ref[:, sl], s)
            yield
            o_ref[:, sl] = o_s + _dot1(qk_ref[:, h * CH:(h + 1) * CH], v_new)
            s_sc[:, sl] = s * gl_ref[0:1, sl] + _dot1(kd_ref[:, sl], v_new, _dot_tn)

        _interleave(item(h) for h in range(H))

    row = lambda w_: pl.BlockSpec((CH, w_), lambda n: (n, 0))
    return pl.pallas_call(
        body, name="gdn_scan_fwd", grid=(N,),
        in_specs=[row(512)] * 4 + [row(256), pl.BlockSpec((8, 512), lambda n: (n, 0))],
        out_specs=[row(512), pl.BlockSpec((GDN_D, 512), lambda n: (n, 0))],
        out_shape=[jax.ShapeDtypeStruct((LP, 512), F32), jax.ShapeDtypeStruct((N * GDN_D, 512), F32)],
        scratch_shapes=[pltpu.VMEM((GDN_D, 512), F32)],
        compiler_params=_cp(("arbitrary",)),
    )(u, w, qd, kd, qk, gl)


def _gdn_scan_bwd(u, w, qd, kd, qk, gl, ssave, do, carry=None):
    LP = u.shape[0]
    CH = GDN_CHUNK
    N = LP // CH
    H = GDN_HEADS

    def body(u_ref, w_ref, qd_ref, kd_ref, qk_ref, gl_ref, s_ref, do_ref,
             du_ref, dw_ref, dqd_ref, dkd_ref, dqk_ref, dgl_ref, ds_sc):
        @pl.when(pl.program_id(0) == 0)
        def _():
            ds_sc[...] = jnp.zeros_like(ds_sc)
        def item(h):
            sl = slice(h * GDN_D, (h + 1) * GDN_D)
            s64 = slice(h * CH, (h + 1) * CH)
            s = s_ref[:, sl]
            ds = ds_sc[:, sl]
            do_ = do_ref[:, sl]
            w_, qd_, kd_, qk_ = w_ref[:, sl], qd_ref[:, sl], kd_ref[:, sl], qk_ref[:, s64]
            v_new = u_ref[:, sl] - _dot1(w_, s)
            d_vnew = _dot1(qk_, do_, _dot_tn) + _dot1(kd_, ds)
            dqd_ref[:, sl] = _dot1(do_, s, _dot_nt)
            ds_new = ds * gl_ref[0:1, sl] + _dot1(qd_, do_, _dot_tn)
            dgl_ref[:, sl] = jnp.broadcast_to(jnp.sum(_colsum(ds * s), axis=-1, keepdims=True), (8, GDN_D))
            yield
            du_ref[:, sl] = d_vnew
            dw_ref[:, sl] = -_dot1(d_vnew, s, _dot_nt)
            dkd_ref[:, sl] = _dot1(v_new, ds, _dot_nt)
            dqk_ref[:, s64] = _dot1(do_, v_new, _dot_nt)
            ds_sc[:, sl] = ds_new - _dot1(w_, d_vnew, _dot_tn)

        _interleave(item(h) for h in range(H))

    rev = lambda w_: pl.BlockSpec((CH, w_), lambda n: (N - 1 - n, 0))
    rev8 = pl.BlockSpec((8, 512), lambda n: (N - 1 - n, 0))
    o512 = jax.ShapeDtypeStruct((LP, 512), F32)
    return _call_carrying(
        carry, body, N, name="gdn_scan_bwd",
        in_specs=[rev(512)] * 4 + [rev(256), rev8, pl.BlockSpec((GDN_D, 512), lambda n: (N - 1 - n, 0)), rev(512)],
        out_specs=[rev(512)] * 4 + [rev(256), rev8],
        out_shape=[o512] * 4 + [jax.ShapeDtypeStruct((LP, 256), F32), jax.ShapeDtypeStruct((LP // 8, 512), F32)],
        scratch_shapes=[pltpu.VMEM((GDN_D, 512), F32)],
        operands=(u, w, qd, kd, qk, gl, ssave, do))


def _sb_scores(qh, kblk, mask):
    z = _dot_nt(qh, kblk)
    e = jnp.exp(-jnp.abs(z))
    sp = jnp.maximum(z, 0.0) + jnp.log(1.0 + e)
    return z, e, jnp.where(mask, -sp, 0.0), z - sp


def _sb_fwd(proj):
    LP = proj.shape[0]
    B = SB_BLOCK
    W = min(SB_SPAN, LP)
    SUB = SB_SUB
    Q = min(SB_QTILE, LP)
    nq = LP // Q
    nsub = W // SUB
    scale = SB_DH ** -0.5
    qcol, kcol, vcol = OFF_SB // B, (OFF_SB + 512) // B, (OFF_SB + 1024) // B

    def body(q_ref, k_ref, v_ref, o_ref, c_ref, n_ref):
        i = pl.program_id(1)
        lane = _iota2((Q, B), 1)
        head_a = lane < SB_DH
        qs = q_ref[...] * scale
        qh = [jnp.where(head_a, qs, 0.0).astype(BF16), jnp.where(head_a, 0.0, qs).astype(BF16)]
        u_strict = (_iota2((SUB, SUB), 0) > _iota2((SUB, SUB), 1)).astype(BF16)
        qpos = i * Q + _iota2((Q, W), 0)
        hi0 = (i + 1) * Q
        nspan = (hi0 + W - 1) // W

        def live(st):
            return (st[0] < nspan) & (st[1] > 0)

        def span(st):
            r, carry = st[0], st[2:]
            hi = hi0 - r * W
            k0 = pl.multiple_of(jnp.maximum(hi - W, 0), B)
            kblk = k_ref[pl.ds(k0, W), :].astype(BF16)
            vblk = v_ref[pl.ds(k0, W), :].astype(BF16)
            kpos = k0 + _iota2((Q, W), 1)
            mask = (kpos < qpos) & (kpos >= PAD_ROWS) & (kpos < hi)
            new = [None] * 4

            def head(h):
                o_acc, c = carry[2 * h], carry[2 * h + 1]
                z, e, l1m, lsg = _sb_scores(qh[h], kblk, mask)
                yield
                subs = [slice(b * SUB, (b + 1) * SUB) for b in range(nsub)]
                suf = [_dot_exact_r(l1m[:, bs], u_strict) for bs in subs]
                yield
                parts = [None] * nsub
                for b in reversed(range(nsub)):
                    parts[b] = jnp.where(mask[:, subs[b]], jnp.exp(lsg[:, subs[b]] + suf[b] + c), 0.0)
                    c = c + _rowsum(l1m[:, subs[b]])
                att = jnp.concatenate(parts, axis=1).astype(BF16)
                new[2 * h], new[2 * h + 1] = o_acc + _dot(att, vblk), c

            _interleave(head(h) for h in range(2))
            more = (jnp.maximum(jnp.max(new[1]), jnp.max(new[3])) > SB_DEAD).astype(jnp.int32)
            return (r + 1, more, *new)

        zero_o = jnp.zeros((Q, B), F32)
        zero_c = jnp.zeros((Q, 1), F32)
        nrun, _, o_a, c_a, o_b, c_b = lax.while_loop(
            live, span, (jnp.int32(0), jnp.int32(1), zero_o, zero_c, zero_o, zero_c))
        o_ref[...] = jnp.where(head_a, o_a, o_b)
        c_ref[...] = jnp.where(head_a, c_a, c_b)
        n_ref[pl.program_id(0), i] = nrun

    blk = pl.BlockSpec((Q, B), lambda p, i: (i, p))
    out = jax.ShapeDtypeStruct((LP, 512), F32)
    return pl.pallas_call(
        body, name="sb_fwd", grid=(SB_HEADS // 2, nq),
        in_specs=[pl.BlockSpec((Q, B), lambda p, i: (i, qcol + p)),
                  pl.BlockSpec((LP, B), lambda p, i: (0, kcol + p)),
                  pl.BlockSpec((LP, B), lambda p, i: (0, vcol + p))],
        out_specs=[blk, blk, pl.BlockSpec(memory_space=pltpu.SMEM)],
        out_shape=[out, out, jax.ShapeDtypeStruct((SB_HEADS // 2, nq), jnp.int32)],
        compiler_params=_cp(("arbitrary", "arbitrary")),
    )(proj, proj, proj)


def _sb_bwd(proj, ctot, nrun_all, do):
    LP = proj.shape[0]
    B = SB_BLOCK
    W = min(SB_SPAN, LP)
    SUB = SB_SUB
    Q = min(SB_QTILE, LP)
    nq = LP // Q
    nsub = W // SUB
    scale = SB_DH ** -0.5
    qcol, kcol, vcol = OFF_SB // B, (OFF_SB + 512) // B, (OFF_SB + 1024) // B

    def body(n_ref, q_ref, k_ref, v_ref, c_ref, do_ref, dq_ref, dk_ref, dv_ref):
        i = pl.program_id(1)

        @pl.when(i == 0)
        def _():
            dk_ref[...] = jnp.zeros_like(dk_ref)
            dv_ref[...] = jnp.zeros_like(dv_ref)

        lane = _iota2((Q, B), 1)
        head_a = lane < SB_DH
        qs = q_ref[...] * scale
        qh = [jnp.where(head_a, qs, 0.0).astype(BF16), jnp.where(head_a, 0.0, qs).astype(BF16)]
        dof = do_ref[...]
        doh = [jnp.where(head_a, dof, 0.0).astype(BF16), jnp.where(head_a, 0.0, dof).astype(BF16)]
        cfull = c_ref[...]
        ctot_h = [cfull[:, 0:1], cfull[:, SB_DH:SB_DH + 1]]
        sub_r, sub_c = _iota2((SUB, SUB), 0), _iota2((SUB, SUB), 1)
        u_strict = (sub_r > sub_c).astype(BF16)
        l_strict = (sub_r < sub_c).astype(BF16)
        qpos = i * Q + _iota2((Q, W), 0)
        hi0 = (i + 1) * Q
        nrun = n_ref[pl.program_id(0), i]

        def span(t, carry):
            r = nrun - 1 - t
            hi = hi0 - r * W
            k0 = pl.multiple_of(jnp.maximum(hi - W, 0), B)
            kblk = k_ref[pl.ds(k0, W), :].astype(BF16)
            vblk = v_ref[pl.ds(k0, W), :].astype(BF16)
            kpos = k0 + _iota2((Q, W), 1)
            mask = (kpos < qpos) & (kpos >= PAD_ROWS) & (kpos < hi)
            new = [None] * 6
            dk_add, dv_add = [None, None], [None, None]
            subs = [slice(b * SUB, (b + 1) * SUB) for b in range(nsub)]

            def head(h):
                dq_acc, pre, ecar = carry[3 * h], carry[3 * h + 1], carry[3 * h + 2]
                z, e, l1m, lsg = _sb_scores(qh[h], kblk, mask)
                d_att = _dot_nt(doh[h], vblk)
                yield
                sig = jnp.where(z >= 0.0, 1.0, e) / (1.0 + e)
                suf = [_dot_exact_r(l1m[:, bs], u_strict) for bs in subs]
                yield
                att_parts, p_parts = [None] * nsub, [None] * nsub
                for b, bs in enumerate(subs):
                    pre = pre + _rowsum(l1m[:, bs])
                    att_parts[b] = jnp.where(mask[:, bs], jnp.exp(lsg[:, bs] + suf[b] + (ctot_h[h] - pre)), 0.0)
                    p_parts[b] = att_parts[b] * d_att[:, bs]
                pcum = [_dot_exact_r(p, l_strict) for p in p_parts]
                yield
                dz_parts = [None] * nsub
                for b, bs in enumerate(subs):
                    sg = sig[:, bs]
                    dz_parts[b] = jnp.where(mask[:, bs], p_parts[b] * (1.0 - sg) - sg * (ecar + pcum[b]), 0.0)
                    ecar = ecar + _rowsum(p_parts[b])
                att = jnp.concatenate(att_parts, axis=1).astype(BF16)
                dz = jnp.concatenate(dz_parts, axis=1).astype(BF16)
                new[3 * h:3 * h + 3] = [dq_acc + _dot(dz, kblk), pre, ecar]
                dk_add[h] = _dot_tn(dz, qh[h])
                dv_add[h] = _dot_tn(att, doh[h])

            _interleave(head(h) for h in range(2))
            dk_ref[pl.ds(k0, W), :] += dk_add[0] + dk_add[1]
            dv_ref[pl.ds(k0, W), :] += dv_add[0] + dv_add[1]
            return tuple(new)

        zero_o = jnp.zeros((Q, B), F32)
        zero_c = jnp.zeros((Q, 1), F32)
        res = lax.fori_loop(0, nrun, span, (zero_o, zero_c, zero_c, zero_o, zero_c, zero_c))
        dq_ref[...] = (jnp.where(head_a, res[0], res[3]) * scale).astype(BF16)

    blk = pl.BlockSpec((Q, B), lambda p, i: (i, p))
    col = pl.BlockSpec((LP, B), lambda p, i: (0, p))
    out = jax.ShapeDtypeStruct((LP, 512), F32)
    return pl.pallas_call(
        body, name="sb_bwd", grid=(SB_HEADS // 2, nq),
        in_specs=[pl.BlockSpec(memory_space=pltpu.SMEM),
                  pl.BlockSpec((Q, B), lambda p, i: (i, qcol + p)),
                  pl.BlockSpec((LP, B), lambda p, i: (0, kcol + p)),
                  pl.BlockSpec((LP, B), lambda p, i: (0, vcol + p)),
                  blk, blk],
        out_specs=[blk, col, col], out_shape=[jax.ShapeDtypeStruct((LP, 512), BF16), out, out],
        compiler_params=_cp(("arbitrary", "arbitrary")),
    )(nrun_all, proj, proj, proj, ctot, do)


def _sb_group_mean():
    r = jnp.right_shift(_iota2((512, 512), 0), 6)
    c = jnp.right_shift(_iota2((512, 512), 1), 6)
    return jnp.where(r == c, 1.0 / SB_DH, 0.0).astype(BF16)


def _attn_norm_fwd(og, proj, osb, gnw, snw):
    LP = og.shape[0]
    T = _tile(LP, 256)

    def body(og_ref, z_ref, os_ref, gnw_ref, snw_ref, y_ref):
        valid = (pl.program_id(0) * T + _iota2((T, 1), 0)) >= PAD_ROWS
        z = z_ref[...]
        zg = z * _sigmoid(z)
        for h in range(GDN_HEADS):
            sl = slice(h * GDN_D, (h + 1) * GDN_D)
            o = og_ref[:, sl]
            y = o * _rms(o) * gnw_ref[...] * zg[:, sl]
            y_ref[:, sl] = jnp.where(valid, y, 0.0).astype(BF16)
        o = os_ref[...]
        msq = _dot_exact_r(o * o, _sb_group_mean())
        y = o * lax.rsqrt(msq + NORM_EPS) * snw_ref[...]
        y_ref[:, 512:] = jnp.where(valid, y, 0.0).astype(BF16)

    row = pl.BlockSpec((T, 512), lambda i: (i, 0))
    return pl.pallas_call(
        body, name="attn_norm_fwd", grid=(LP // T,),
        in_specs=[row, pl.BlockSpec((T, 512), lambda i: (i, OFF_Z // 512)), row,
                  pl.BlockSpec((1, GDN_D), lambda i: (0, 0)), pl.BlockSpec((1, 512), lambda i: (0, 0))],
        out_specs=pl.BlockSpec((T, 1024), lambda i: (i, 0)),
        out_shape=jax.ShapeDtypeStruct((LP, 1024), BF16),
        compiler_params=_cp(("parallel",)),
    )(og, proj, osb, gnw, snw)


def _attn_norm_bwd(og, proj, osb, gnw, snw, dy, carry=None):
    LP = og.shape[0]
    T = _tile(LP, 256)

    def body(og_ref, z_ref, os_ref, gnw_ref, snw_ref, dy_ref, dog_ref, dz_ref, dos_ref, dgw_ref, dsw_ref):
        @pl.when(pl.program_id(0) == 0)
        def _():
            dgw_ref[...] = jnp.zeros_like(dgw_ref)
            dsw_ref[...] = jnp.zeros_like(dsw_ref)
        valid = (pl.program_id(0) * T + _iota2((T, 1), 0)) >= PAD_ROWS
        dy = jnp.where(valid, dy_ref[...], 0.0)
        z = z_ref[...]
        sg = _sigmoid(z)
        zg = z * sg
        dgw = jnp.zeros((1, GDN_D), F32)
        for h in range(GDN_HEADS):
            sl = slice(h * GDN_D, (h + 1) * GDN_D)
            o = og_ref[:, sl]
            dyh = dy[:, sl]
            dx, dwn = _rms_bwd(o, gnw_ref[...], dyh * zg[:, sl])
            dog_ref[:, sl] = dx
            dgw = dgw + _colsum(dwn)
            yn = o * _rms(o) * gnw_ref[...]
            dz_ref[:, sl] = (dyh * yn * (sg[:, sl] * (1.0 + z[:, sl] * (1.0 - sg[:, sl])))).astype(BF16)
        dgw_ref[...] += dgw
        o = os_ref[...]
        gm = _sb_group_mean()
        r = lax.rsqrt(_dot_exact_r(o * o, gm) + NORM_EPS)
        n = o * r
        dys = dy[:, 512:]
        dyw = dys * snw_ref[...]
        dos_ref[...] = r * (dyw - n * _dot_exact_r(dyw * n, gm))
        dsw_ref[...] += _colsum(dys * n)

    row = pl.BlockSpec((T, 512), lambda i: (i, 0))
    gw = pl.BlockSpec((1, GDN_D), lambda i: (0, 0))
    sw = pl.BlockSpec((1, 512), lambda i: (0, 0))
    o512 = jax.ShapeDtypeStruct((LP, 512), F32)
    return _call_carrying(
        carry, body, LP // T, name="attn_norm_bwd",
        in_specs=[row, pl.BlockSpec((T, 512), lambda i: (i, OFF_Z // 512)), row, gw, sw,
                  pl.BlockSpec((T, 1024), lambda i: (i, 0))],
        out_specs=[row, row, row, gw, sw],
        out_shape=[o512, jax.ShapeDtypeStruct((LP, 512), BF16), o512, jax.ShapeDtypeStruct((1, GDN_D), F32),
                   jax.ShapeDtypeStruct((1, 512), F32)],
        operands=(og, proj, osb, gnw, snw, dy))


def _resid_fwd(h0, mix, w_post, w_pre):
    LP, D = h0.shape
    T = _tile(LP, 512)

    def body(h0_ref, mix_ref, wp_ref, wf_ref, h1_ref, n2_ref):
        mix = mix_ref[...]
        h1 = h0_ref[...] + mix * _rms(mix) * wp_ref[...]
        h1_ref[...] = h1
        n2_ref[...] = (h1 * _rms(h1) * wf_ref[...]).astype(BF16)

    row = pl.BlockSpec((T, D), lambda i: (i, 0))
    vec = pl.BlockSpec((1, D), lambda i: (0, 0))
    return pl.pallas_call(
        body, name="resid_fwd", grid=(LP // T,),
        in_specs=[row, row, vec, vec], out_specs=[row, row],
        out_shape=[jax.ShapeDtypeStruct((LP, D), F32), jax.ShapeDtypeStruct((LP, D), BF16)],
        compiler_params=_cp(("parallel",)),
    )(h0, mix, w_post, w_pre)


def _resid_bwd(h1, mix, w_post, w_pre, dout, dn2):
    LP, D = h1.shape
    T = _tile(LP, 512)

    def body(h1_ref, mix_ref, wp_ref, wf_ref, dout_ref, dn2_ref, dh1_ref, dmix_ref, dwf_ref, dwp_ref):
        @pl.when(pl.program_id(0) == 0)
        def _():
            dwf_ref[...] = jnp.zeros_like(dwf_ref)
            dwp_ref[...] = jnp.zeros_like(dwp_ref)
        dx, dwn = _rms_bwd(h1_ref[...], wf_ref[...], dn2_ref[...])
        dh1 = dout_ref[...] + dx
        dh1_ref[...] = dh1
        dwf_ref[...] += _colsum(dwn)
        dmix, dwn2 = _rms_bwd(mix_ref[...], wp_ref[...], dh1)
        dmix_ref[...] = dmix.astype(BF16)
        dwp_ref[...] += _colsum(dwn2)

    row = pl.BlockSpec((T, D), lambda i: (i, 0))
    vec = pl.BlockSpec((1, D), lambda i: (0, 0))
    v = jax.ShapeDtypeStruct((1, D), F32)
    return pl.pallas_call(
        body, name="resid_bwd", grid=(LP // T,),
        in_specs=[row, row, vec, vec, row, row], out_specs=[row, row, vec, vec],
        out_shape=[jax.ShapeDtypeStruct((LP, D), F32), jax.ShapeDtypeStruct((LP, D), BF16), v, v],
        compiler_params=_cp(("arbitrary",)),
    )(h1, mix, w_post, w_pre, dout, dn2)


GELU_C = 0.7978845608028654
GELU_A = 0.044715


def _gelu_parts(x):
    t = jnp.tanh(GELU_C * (x + GELU_A * x * x * x))
    return 0.5 * x * (1.0 + t), t


def _convglu_fwd(up, conv_w, conv_b):
    LP, C = up.shape
    T = _tile(LP, 128)

    def body(x_ref, halo_ref, cw_ref, cb_ref, act_ref):
        i = pl.program_id(0)

        def conv(cols):
            ext = jnp.concatenate([jnp.where(i > 0, halo_ref[:, cols], 0.0), x_ref[:, cols]], axis=0)
            w = cw_ref[:, cols]
            return (w[2:3] * ext[8:] + w[1:2] * pltpu.roll(ext, 1, 0)[8:] + w[0:1] * pltpu.roll(ext, 2, 0)[8:]
                    + cb_ref[:, cols])

        for s in range(D_FF // LANE):
            gs = slice(s * LANE, (s + 1) * LANE)
            g, _ = _gelu_parts(conv(gs))
            act_ref[:, gs] = (g * conv(slice(D_FF + s * LANE, D_FF + (s + 1) * LANE))).astype(BF16)

    t8 = T // 8
    return pl.pallas_call(
        body, name="convglu_fwd", grid=(LP // T,),
        in_specs=[pl.BlockSpec((T, C), lambda i: (i, 0)),
                  pl.BlockSpec((8, C), lambda i: (jnp.maximum(i * t8 - 1, 0), 0)),
                  pl.BlockSpec((FFN_CONV, C), lambda i: (0, 0)), pl.BlockSpec((1, C), lambda i: (0, 0))],
        out_specs=pl.BlockSpec((T, D_FF), lambda i: (i, 0)),
        out_shape=jax.ShapeDtypeStruct((LP, D_FF), BF16),
        compiler_params=_cp(("parallel",)),
    )(up, up, conv_w, conv_b)


def _convglu_bwd(up, conv_w, conv_b, dact):
    LP, C = up.shape
    T = _tile(LP, 128)
    TE = T + 8
    nt = LP // T

    def body(x_ref, xp_ref, xn_ref, cw_ref, cb_ref, da_ref, dan_ref, dx_ref, dcw_ref, dcb_ref):
        i = pl.program_id(0)

        @pl.when(i == 0)
        def _():
            dcw_ref[...] = jnp.zeros_like(dcw_ref)
            dcb_ref[...] = jnp.zeros_like(dcb_ref)

        last = i == nt - 1

        def conv(cols):
            ext = jnp.concatenate([jnp.where(i > 0, xp_ref[:, cols], 0.0), x_ref[:, cols],
                                   jnp.where(last, 0.0, xn_ref[:, cols])], axis=0)
            sh = [ext[8:8 + TE], pltpu.roll(ext, 1, 0)[8:8 + TE], pltpu.roll(ext, 2, 0)[8:8 + TE]]
            w = cw_ref[:, cols]
            return w[2:3] * sh[0] + w[1:2] * sh[1] + w[0:1] * sh[2] + cb_ref[:, cols], sh, w

        def back(cols, dy, sh, w):
            dy_t = dy[0:T]
            dcb_ref[:, cols] += _colsum(dy_t)
            for j in range(FFN_CONV):
                dcw_ref[j:j + 1, cols] += _colsum(dy_t * sh[FFN_CONV - 1 - j][0:T])
            dx_ref[:, cols] = (w[2:3] * dy_t + w[1:2] * pltpu.roll(dy, TE - 1, 0)[0:T]
                               + w[0:1] * pltpu.roll(dy, TE - 2, 0)[0:T]).astype(BF16)

        for s in range(D_FF // LANE):
            gs = slice(s * LANE, (s + 1) * LANE)
            vs = slice(D_FF + s * LANE, D_FF + (s + 1) * LANE)
            gate, sh_g, w_g = conv(gs)
            val, sh_v, w_v = conv(vs)
            g, t = _gelu_parts(gate)
            dg_dx = 0.5 * (1.0 + t) + 0.5 * gate * (1.0 - t * t) * GELU_C * (1.0 + 3.0 * GELU_A * gate * gate)
            da = jnp.concatenate([da_ref[:, gs], jnp.where(last, 0.0, dan_ref[:, gs])], axis=0)
            back(gs, da * val * dg_dx, sh_g, w_g)
            back(vs, da * g, sh_v, w_v)

    t8 = T // 8
    nb8 = LP // 8
    prev8 = lambda w: pl.BlockSpec((8, w), lambda i: (jnp.maximum(i * t8 - 1, 0), 0))
    next8 = lambda w: pl.BlockSpec((8, w), lambda i: (jnp.minimum((i + 1) * t8, nb8 - 1), 0))
    row = lambda w: pl.BlockSpec((T, w), lambda i: (i, 0))
    small = lambda r: pl.BlockSpec((r, C), lambda i: (0, 0))
    return pl.pallas_call(
        body, name="convglu_bwd", grid=(nt,),
        in_specs=[row(C), prev8(C), next8(C), small(FFN_CONV), small(1), row(D_FF), next8(D_FF)],
        out_specs=[row(C), small(FFN_CONV), small(1)],
        out_shape=[jax.ShapeDtypeStruct((LP, C), BF16), jax.ShapeDtypeStruct((FFN_CONV, C), F32),
                   jax.ShapeDtypeStruct((1, C), F32)],
        compiler_params=_cp(("arbitrary",)),
    )(up, up, up, conv_w, conv_b, dact, dact)


def _final(h1, f, w_post, target, n_real):
    LP, D = h1.shape
    T = _tile(LP, 256)

    def body(h1_ref, f_ref, w_ref, t_ref, loss_ref, dout_ref, df_ref, dw_ref):
        @pl.when(pl.program_id(0) == 0)
        def _():
            loss_ref[...] = jnp.zeros_like(loss_ref)
            dw_ref[...] = jnp.zeros_like(dw_ref)
        rows = pl.program_id(0) * T + _iota2((T, 1), 0)
        real = (rows >= ROW0) & (rows < ROW0 + n_real)
        f = f_ref[...]
        out = h1_ref[...] + f * _rms(f) * w_ref[...]
        err = jnp.where(real, out - t_ref[...], 0.0)
        loss_ref[...] += 0.5 * jnp.sum(_colsum(jnp.mean(err * err, axis=-1, keepdims=True)), axis=-1, keepdims=True)
        dout = err * (1.0 / D)
        dout_ref[...] = dout
        dx, dwn = _rms_bwd(f, w_ref[...], dout)
        df_ref[...] = dx.astype(BF16)
        dw_ref[...] += _colsum(dwn)

    row = pl.BlockSpec((T, D), lambda i: (i, 0))
    vec = pl.BlockSpec((1, D), lambda i: (0, 0))
    return pl.pallas_call(
        body, name="final_loss", grid=(LP // T,),
        in_specs=[row, row, vec, row],
        out_specs=[pl.BlockSpec((1, 128), lambda i: (0, 0)), row, row, vec],
        out_shape=[jax.ShapeDtypeStruct((1, 128), F32), jax.ShapeDtypeStruct((LP, D), F32),
                   jax.ShapeDtypeStruct((LP, D), BF16), jax.ShapeDtypeStruct((1, D), F32)],
        compiler_params=_cp(("arbitrary",)),
    )(h1, f, w_post, target)


ANY_SPEC = pl.BlockSpec(memory_space=pl.ANY)
N_CHIP = 4


def _other_chips(x, y):
    return [(1 - x, y), (x, 1 - y), (1 - x, 1 - y)]


def _gather_direct(arrs, name):
    n = len(arrs)
    npeer = N_DEV - 1

    def body(*refs):
        ins, outs = refs[:n], refs[n:2 * n]
        send_sems, recv_sems, loc_sems = refs[2 * n:]
        x, y, c = lax.axis_index("x"), lax.axis_index("y"), lax.axis_index("c")
        me = 4 * x + 2 * y + c
        copies = []
        for a in range(n):
            for kk in range(1, N_DEV):
                px = 1 - x if kk & 4 else x
                py = 1 - y if kk & 2 else y
                pc = 1 - c if kk & 1 else c
                s = a * npeer + kk - 1
                cp = pltpu.make_async_remote_copy(src_ref=ins[a], dst_ref=outs[a].at[me], send_sem=send_sems.at[s],
                                                  recv_sem=recv_sems.at[s], device_id=(px, py, pc), device_id_type=MESH)
                cp.start()
                copies.append(cp)
            own = pltpu.make_async_copy(ins[a], outs[a].at[me], loc_sems.at[a])
            own.start()
            copies.append(own)
        for cp in copies:
            cp.wait()

    shapes = [jax.ShapeDtypeStruct((N_DEV,) + tuple(a.shape), a.dtype) for a in arrs]
    return pl.pallas_call(
        body, name=name, in_specs=[ANY_SPEC] * n, out_specs=[ANY_SPEC] * n, out_shape=shapes,
        scratch_shapes=[pltpu.SemaphoreType.DMA((n * npeer,)), pltpu.SemaphoreType.DMA((n * npeer,)),
                        pltpu.SemaphoreType.DMA((n,))],
        compiler_params=pltpu.CompilerParams(has_side_effects=True),
    )(*arrs)


class _Exchange:
    def __init__(self, arrs, out_shapes, scratch, start, finish, mid=None):
        self.arrs, self.out_shapes, self.scratch = list(arrs), list(out_shapes), list(scratch)
        self.start, self.finish, self.mid = start, finish, mid

    @property
    def n(self):
        return len(self.arrs)


def _run_exchange(ex, name):
    n = ex.n

    def body(*refs):
        ins, outs, sems = refs[:n], refs[n:2 * n], refs[2 * n:]
        ex.start(ins, outs, sems)
        if ex.mid is not None:
            ex.mid(ins, outs, sems)
        ex.finish(ins, outs, sems)

    return pl.pallas_call(
        body, name=name, in_specs=[ANY_SPEC] * n, out_specs=[ANY_SPEC] * n, out_shape=ex.out_shapes,
        scratch_shapes=ex.scratch, compiler_params=pltpu.CompilerParams(has_side_effects=True),
    )(*ex.arrs)


def _carry_begin(ex, refs, step, nsteps):
    if ex is None:
        return

    @pl.when(step == 0)
    def _():
        ex.start(*refs)

    if ex.mid is not None:
        @pl.when(step == min(nsteps - 1, (3 * nsteps) // 5))
        def _():
            ex.mid(*refs)


def _carry_end(ex, refs, step, nsteps):
    if ex is None:
        return

    @pl.when(step == nsteps - 1)
    def _():
        ex.finish(*refs)


def _gather_two_level(arrs):
    n = len(arrs)
    K = 7

    def env(ins, outs, sems):
        send_sems, recv_sems, loc_sems = sems
        x, y, c = lax.axis_index("x"), lax.axis_index("y"), lax.axis_index("c")

        def cp(a, k, src, slot, to):
            return pltpu.make_async_remote_copy(src_ref=src, dst_ref=outs[a].at[slot], send_sem=send_sems.at[a * K + k],
                                                recv_sem=recv_sems.at[a * K + k], device_id=to, device_id_type=MESH)

        me = 4 * x + 2 * y + c
        owns = [pltpu.make_async_copy(ins[a], outs[a].at[me], loc_sems.at[a]) for a in range(n)]
        first = []
        for a in range(n):
            first.append(cp(a, 0, ins[a], me, (x, y, 1 - c)))
            first += [cp(a, 1 + j, ins[a], me, (px, py, c)) for j, (px, py) in enumerate(_other_chips(x, y))]
        passed = []
        for j, (px, py) in enumerate(_other_chips(x, y)):
            slot = 4 * px + 2 * py + c
            passed += [(cp(a, 1 + j, ins[a], slot, (px, py, c)), cp(a, 4 + j, outs[a].at[slot], slot, (x, y, 1 - c)))
                       for a in range(n)]
        from_sib = []
        for a in range(n):
            from_sib.append(cp(a, 0, ins[a], 4 * x + 2 * y + (1 - c), (x, y, 1 - c)))
            from_sib += [cp(a, 4 + j, ins[a], 4 * px + 2 * py + (1 - c), (x, y, 1 - c))
                         for j, (px, py) in enumerate(_other_chips(x, y))]
        return owns, first, passed, from_sib

    def start(ins, outs, sems):
        owns, first, _, _ = env(ins, outs, sems)
        for cp in owns + first:
            cp.start()

    def mid(ins, outs, sems):
        _, _, passed, _ = env(ins, outs, sems)
        for arrival, fwd in passed:
            arrival.wait_recv()
            fwd.start()

    def finish(ins, outs, sems):
        owns, first, passed, from_sib = env(ins, outs, sems)
        for cp in from_sib:
            cp.wait_recv()
        for cp in first + [fwd for _, fwd in passed]:
            cp.wait_send()
        for cp in owns:
            cp.wait()

    shapes = [jax.ShapeDtypeStruct((N_DEV,) + tuple(a.shape), a.dtype) for a in arrs]
    scratch = [pltpu.SemaphoreType.DMA((n * K,)), pltpu.SemaphoreType.DMA((n * K,)), pltpu.SemaphoreType.DMA((n,))]
    return _Exchange(arrs, shapes, scratch, start, finish, mid)


def _swap_sibling(arrs):
    n = len(arrs)

    def copies(ins, outs, sems):
        send_sems, recv_sems = sems
        x, y, c = lax.axis_index("x"), lax.axis_index("y"), lax.axis_index("c")
        return [pltpu.make_async_remote_copy(src_ref=ins[a], dst_ref=outs[a], send_sem=send_sems.at[a],
                                             recv_sem=recv_sems.at[a], device_id=(x, y, 1 - c), device_id_type=MESH)
                for a in range(n)]

    def start(ins, outs, sems):
        for cp in copies(ins, outs, sems):
            cp.start()

    def finish(ins, outs, sems):
        for cp in copies(ins, outs, sems):
            cp.wait()

    shapes = [jax.ShapeDtypeStruct(tuple(a.shape), a.dtype) for a in arrs]
    return _Exchange(arrs, shapes, [pltpu.SemaphoreType.DMA((n,)), pltpu.SemaphoreType.DMA((n,))], start, finish)


def _exchange_chips(arrs):
    n = len(arrs)
    K = N_CHIP - 1

    def copies(ins, outs, sems):
        send_sems, recv_sems, loc_sems = sems
        x, y, c = lax.axis_index("x"), lax.axis_index("y"), lax.axis_index("c")
        mine = 2 * x + y
        out = []
        for a in range(n):
            out += [pltpu.make_async_remote_copy(src_ref=ins[a].at[2 * px + py], dst_ref=outs[a].at[mine],
                                                 send_sem=send_sems.at[a * K + j], recv_sem=recv_sems.at[a * K + j],
                                                 device_id=(px, py, c), device_id_type=MESH)
                    for j, (px, py) in enumerate(_other_chips(x, y))]
            out.append(pltpu.make_async_copy(ins[a].at[mine], outs[a].at[mine], loc_sems.at[a]))
        return out

    def start(ins, outs, sems):
        for cp in copies(ins, outs, sems):
            cp.start()

    def finish(ins, outs, sems):
        for cp in copies(ins, outs, sems):
            cp.wait()

    shapes = [jax.ShapeDtypeStruct(tuple(a.shape), a.dtype) for a in arrs]
    scratch = [pltpu.SemaphoreType.DMA((n * K,)), pltpu.SemaphoreType.DMA((n * K,)), pltpu.SemaphoreType.DMA((n,))]
    return _Exchange(arrs, shapes, scratch, start, finish)


def _add_halves(mine, theirs, name):
    _, R, C = mine.shape
    cap = max(16, ((2 * 1024 * 1024) // (4 * C * 10)) // 16 * 16)
    T = R if R <= cap else _tile(R, cap, 16)

    def body(a_ref, b_ref, o_ref):
        o_ref[...] = (a_ref[...] + b_ref[...].astype(F32)).astype(BF16)

    blk = pl.BlockSpec((N_CHIP, T, C), lambda i: (0, i, 0))
    return pl.pallas_call(
        body, name=name, grid=(R // T,), in_specs=[blk, blk], out_specs=blk,
        out_shape=jax.ShapeDtypeStruct(mine.shape, BF16), compiler_params=_cp(("parallel",)),
    )(mine, theirs)


def _adamw(parts, w, m, v, name):
    R, C = w.shape
    npart = parts.shape[0]
    cap = max(16, ((2 * 1024 * 1024) // (4 * C * 12)) // 16 * 16)
    T = R if R <= cap else _tile(R, cap, 16)

    def body(p_ref, w_ref, m_ref, v_ref, g_ref, d_ref, nm_ref, nv_ref):
        g = p_ref[0].astype(F32)
        for k in range(1, npart):
            g = g + p_ref[k].astype(F32)
        mm = ADAM_B1 * m_ref[...] + (1.0 - ADAM_B1) * g
        vv = ADAM_B2 * v_ref[...] + (1.0 - ADAM_B2) * (g * g)
        m_hat = mm / (1.0 - ADAM_B1 ** ADAM_STEP)
        v_hat = vv / (1.0 - ADAM_B2 ** ADAM_STEP)
        g_ref[...] = g
        d_ref[...] = -ADAM_LR * (m_hat / (jnp.sqrt(v_hat) + ADAM_EPS) + ADAM_WD * w_ref[...])
        nm_ref[...] = mm
        nv_ref[...] = vv

    row = pl.BlockSpec((T, C), lambda i: (i, 0))
    out = jax.ShapeDtypeStruct((R, C), F32)
    return pl.pallas_call(
        body, name=name, grid=(R // T,),
        in_specs=[pl.BlockSpec((npart, T, C), lambda i: (0, i, 0)), row, row, row],
        out_specs=[row] * 4, out_shape=[out] * 4,
        compiler_params=_cp(("parallel",)),
    )(parts, w, m, v)


SMALL = ("attn_pre_norm", "gdn_A_log", "gdn_dt_bias", "gdn_norm_w", "sb_norm_w", "attn_post_norm",
         "ffn_pre_norm", "ffn_conv_b", "ffn_post_norm")


def _pack_small(arrs):
    rows = []
    for a in arrs:
        flat = a.reshape(-1).astype(F32)
        n = -(-flat.shape[0] // 128) * 128
        rows.append(jnp.pad(flat, (0, n - flat.shape[0])).reshape(-1, 128))
    slab = jnp.concatenate(rows, axis=0)
    pad = (-slab.shape[0]) % 8
    return jnp.pad(slab, ((0, pad), (0, 0)))


def _unpack_small(slab, shapes):
    out, r = [], 0
    for shp in shapes:
        size = 1
        for s in shp:
            size *= s
        nr = -(-size // 128)
        out.append(slab[r:r + nr].reshape(-1)[:size].reshape(shp))
        r += nr
    return out


def _to_blocks_cols(a):
    R, C = a.shape
    return a.reshape(R, N_DEV, C // N_DEV).transpose(1, 0, 2)


def _from_blocks_cols(a):
    n, R, c = a.shape
    return a.transpose(1, 0, 2).reshape(R, n * c)


def kernel(x, meta_tokens, attn_pre_norm, w_in, gdn_conv_w, gdn_A_log, gdn_dt_bias, gdn_norm_w, sb_norm_w, w_out, attn_post_norm, ffn_pre_norm, w_ffn_up, ffn_conv_w, ffn_conv_b, w_ffn_down, ffn_post_norm, loss_target, m_meta_tokens, m_attn_pre_norm, m_w_in, m_gdn_conv_w, m_gdn_A_log, m_gdn_dt_bias, m_gdn_norm_w, m_sb_norm_w, m_w_out, m_attn_post_norm, m_ffn_pre_norm, m_w_ffn_up, m_ffn_conv_w, m_ffn_conv_b, m_w_ffn_down, m_ffn_post_norm, v_meta_tokens, v_attn_pre_norm, v_w_in, v_gdn_conv_w, v_gdn_A_log, v_gdn_dt_bias, v_gdn_norm_w, v_sb_norm_w, v_w_out, v_attn_post_norm, v_ffn_pre_norm, v_w_ffn_up, v_ffn_conv_w, v_ffn_conv_b, v_w_ffn_down, v_ffn_post_norm):
    args = dict(locals())
    seq = x.shape[1]
    LP = -(-(ROW0 + seq) // LP_ALIGN) * LP_ALIGN
    tail = LP - ROW0 - seq

    meta_f = _from_blocks_cols(_run_exchange(_gather_two_level([meta_tokens]), "gather_meta")[0])

    h0 = jnp.concatenate([jnp.zeros((PAD_ROWS, D_MODEL), F32), meta_f, x[0], jnp.zeros((tail, D_MODEL), F32)], axis=0)
    target = jnp.concatenate([jnp.zeros((ROW0, D_MODEL), F32), loss_target[0], jnp.zeros((tail, D_MODEL), F32)], axis=0)
    (u,), got = _prenorm_fwd(h0, attn_pre_norm, carry=_gather_two_level([w_in[0].astype(BF16), gdn_conv_w[0]]))
    win_o = _from_blocks_cols(got[0])
    o_ab = C_QKV
    o_z = o_ab + 2 * GDN_HEADS
    w_inp = jnp.concatenate([win_o[:, :C_QKV], win_o[:, o_z:o_z + C_Z], win_o[:, o_z + C_Z:],
                             win_o[:, o_ab:o_z], jnp.zeros((D_MODEL, C_AB - 2 * GDN_HEADS), BF16)], axis=1)
    gconv_f = _from_blocks_cols(got[1])
    proj = _mm(u, w_inp, F32, "mm_in")
    (qn, kn, vg, beta_b, g_b), got = _gdn_pre_fwd(
        proj, gconv_f, gdn_A_log, gdn_dt_bias,
        carry=_gather_two_level([w_out[0].astype(BF16), w_ffn_down[0].astype(BF16)]))
    w_out_f = got[0].reshape(D_MODEL, D_MODEL)
    w_down_f = got[1].reshape(D_FF, D_MODEL)
    (cu, cw, cqd, ckd, cqk, ct, cgl), got = _gdn_chunk_fwd(
        qn, kn, vg, beta_b, g_b, carry=_gather_two_level([w_ffn_up[0].astype(BF16), ffn_conv_w[0]]))
    w_up_f = _from_blocks_cols(got[0])
    fconv_f = _from_blocks_cols(got[1])
    og, ssave = _gdn_scan_fwd(cu, cw, cqd, ckd, cqk, cgl)
    osb, ctot, sb_nrun = _sb_fwd(proj)
    snw = sb_norm_w.reshape(1, SB_HEADS * SB_DH)
    y = _attn_norm_fwd(og, proj, osb, gdn_norm_w, snw)
    mix = _mm(y, w_out_f, F32, "mm_out")
    h1, n2 = _resid_fwd(h0, mix, attn_post_norm, ffn_pre_norm)
    up = _mm(n2, w_up_f, F32, "mm_up")
    act = _convglu_fwd(up, fconv_f, ffn_conv_b)
    f = _mm(act, w_down_f, F32, "mm_down")
    loss_part, dout, df, d_fpost = _final(h1, f, ffn_post_norm, target, seq)
    loss = lax.psum(loss_part[0, 0], ("x", "y", "c"))

    d_wdown = _mm_tn(act, df, "mm_dw_down")
    dact = _mm(df, w_down_f.T, F32, "mm_dact")
    dup, d_fconv, d_fconvb = _convglu_bwd(up, fconv_f, ffn_conv_b, dact)
    d_wup = _mm_tn(n2, dup, "mm_dw_up")
    dn2 = _mm(dup, w_up_f.T, F32, "mm_dn2")
    dh1, dmix, d_fpre, d_apost = _resid_bwd(h1, mix, attn_post_norm, ffn_pre_norm, dout, dn2)
    d_wout = _mm_tn(y, dmix, "mm_dw_out")
    dy = _mm(dmix, w_out_f.T, F32, "mm_dy")
    my_c = lax.axis_index("c")

    def core_halves(blocks):
        halves = [s.reshape((N_CHIP, 2) + s.shape[1:]) for s in blocks]
        return ([lax.dynamic_index_in_dim(h, my_c, axis=1, keepdims=False) for h in halves],
                [lax.dynamic_index_in_dim(h, 1 - my_c, axis=1, keepdims=False).astype(BF16) for h in halves])

    early_names = ("w_out", "w_ffn_up", "w_ffn_down", "ffn_conv_w")
    e_mine, e_send = core_halves([d_wout.reshape(N_DEV, D_MODEL // N_DEV, D_MODEL), _to_blocks_cols(d_wup),
                                  d_wdown.reshape(N_DEV, D_FF // N_DEV, D_MODEL), _to_blocks_cols(d_fconv)])
    (dog, dz, dos, d_gnw, d_snw), e_theirs = _attn_norm_bwd(og, proj, osb, gdn_norm_w, snw, dy,
                                                            carry=_swap_sibling(e_send))
    e_sums = [_add_halves(a, b, "grads_add_" + nm) for nm, a, b in zip(early_names, e_mine, e_theirs)]
    dqs, dks, dvs = _sb_bwd(proj, ctot, sb_nrun, dos)
    (du_, dw_, dqd_, dkd_, dqk_, dgl_), e_recv = _gdn_scan_bwd(cu, cw, cqd, ckd, cqk, cgl, ssave, dog,
                                                               carry=_exchange_chips(e_sums))
    dqn, dkn, dvg, dbeta, dg = _gdn_chunk_bwd(qn, kn, vg, beta_b, g_b, ct, du_, dw_, dqd_, dkd_, dqk_, dgl_)
    dqkv, dab, d_gconv, d_gsc = _gdn_pre_bwd(proj, gconv_f, gdn_A_log, gdn_dt_bias, dqn, dkn, dvg, dbeta, dg)
    dproj = jnp.concatenate([dqkv.astype(BF16), dz.astype(BF16), dqs.astype(BF16), dks.astype(BF16),
                             dvs.astype(BF16), dab.astype(BF16)], axis=1)
    d_winp = _mm_tn(u, dproj, "mm_dw_in")
    du0 = _mm(dproj, w_inp.T, F32, "mm_du")
    dh0, d_apre = _prenorm_bwd(h0, attn_pre_norm, du0, dh1)
    grad_x = dh0[ROW0:ROW0 + seq][None]
    d_meta = dh0[PAD_ROWS:ROW0]

    d_win = jnp.concatenate([d_winp[:, :C_QKV], d_winp[:, OFF_AB:OFF_AB + 2 * GDN_HEADS],
                             d_winp[:, OFF_Z:OFF_Z + C_Z], d_winp[:, OFF_SB:OFF_SB + C_SB]], axis=1)
    small_grads = [d_apre, d_gsc[0:1, :GDN_HEADS], d_gsc[1:2, :GDN_HEADS], d_gnw, d_snw.reshape(1, SB_HEADS, SB_DH),
                   d_apost, d_fpre, d_fconvb, d_fpost]
    late_names = ("w_in", "gdn_conv_w", "meta_tokens")
    l_mine, l_send = core_halves([_to_blocks_cols(d_win), _to_blocks_cols(d_gconv), _to_blocks_cols(d_meta)])
    l_theirs = _run_exchange(_swap_sibling(l_send), "grads_swap_sibling")
    l_sums = [_add_halves(a, b, "grads_add_" + nm) for nm, a, b in zip(late_names, l_mine, l_theirs)]
    l_recv = _run_exchange(_exchange_chips(l_sums), "grads_exchange_chips")
    slab_parts = _gather_direct([_pack_small(small_grads)], name="gather_small_grads")[0]

    res = {}
    for nm, parts in zip(early_names + late_names, list(e_recv) + list(l_recv)):
        wloc = args[nm]
        shp = wloc.shape
        w2 = wloc.reshape(shp[-2], shp[-1])
        outs = _adamw(parts, w2, args["m_" + nm].reshape(w2.shape), args["v_" + nm].reshape(w2.shape), "adamw_" + nm)
        res[nm] = [o.reshape(shp) for o in outs]
    small_shapes = [args[nm].shape for nm in SMALL]
    outs = _adamw(slab_parts, _pack_small([args[nm] for nm in SMALL]), _pack_small([args["m_" + nm] for nm in SMALL]),
                  _pack_small([args["v_" + nm] for nm in SMALL]), "adamw_small")
    for k in range(4):
        for nm, val in zip(SMALL, _unpack_small(outs[k], small_shapes)):
            res.setdefault(nm, [None] * 4)[k] = val

    order = ("meta_tokens", "attn_pre_norm", "w_in", "gdn_conv_w", "gdn_A_log", "gdn_dt_bias", "gdn_norm_w",
             "sb_norm_w", "w_out", "attn_post_norm", "ffn_pre_norm", "w_ffn_up", "ffn_conv_w", "ffn_conv_b",
             "w_ffn_down", "ffn_post_norm")
    return (loss, grad_x, *[res[nm][0] for nm in order], *[res[nm][1] for nm in order],
            *[res[nm][2] for nm in order], *[res[nm][3] for nm in order])
```

```python
import functools

import jax
import jax.numpy as jnp
from jax import lax
from jax.experimental import pallas as pl
from jax.experimental.pallas import tpu as pltpu

F32 = jnp.float32
BF16 = jnp.bfloat16

D_MODEL = 1024
N_META = 16
GDN_HEADS = 4
GDN_D = 128
GDN_CHUNK = 64
GDN_CONV = 4
GDN_ROWS = 256
SB_HEADS = 8
SB_DH = 64
SB_BLOCK = 128
D_FF = 2816
FFN_CONV = 3
NORM_EPS = 1e-6
L2_EPS = 1e-6
LANE = 128
N_DEV = 8

PAD_ROWS = SB_BLOCK - N_META
ROW0 = SB_BLOCK
SB_SPAN = 512
SB_DEAD = -104.0
SB_SUB = 256
SB_QTILE = 256
LP_ALIGN = 256

C_QKV = 3 * GDN_HEADS * GDN_D
C_Z = GDN_HEADS * GDN_D
C_SB = 3 * SB_HEADS * SB_DH
C_AB = 256
OFF_Z = C_QKV
OFF_SB = OFF_Z + C_Z
OFF_AB = OFF_SB + C_SB
D_INP = OFF_AB + C_AB
D_IN = C_QKV + 2 * GDN_HEADS + C_Z + C_SB

ADAM_LR = 0.001
ADAM_B1 = 0.9
ADAM_B2 = 0.999
ADAM_EPS = 1e-08
ADAM_WD = 0.01
ADAM_STEP = 10

VMEM_LIMIT = 56 * 1024 * 1024
MESH = pl.DeviceIdType.MESH


def _cp(sem=None):
    kw = dict(vmem_limit_bytes=VMEM_LIMIT)
    if sem is not None:
        kw["dimension_semantics"] = sem
    return pltpu.CompilerParams(**kw)


def _tile(n, cap, unit=128):
    best = None
    t = unit
    while t <= min(n, cap):
        if n % t == 0:
            best = t
        t += unit
    assert best is not None, (n, cap, unit)
    return best


def _dot(a, b):
    return jnp.dot(a, b, preferred_element_type=F32)


def _dot_nt(a, b):
    return lax.dot_general(a, b, (((1,), (1,)), ((), ())), preferred_element_type=F32)


def _dot_tn(a, b):
    return lax.dot_general(a, b, (((0,), (0,)), ((), ())), preferred_element_type=F32)


def _split(x):
    hi = x.astype(BF16)
    lo = (x - hi.astype(F32)).astype(BF16)
    return hi, lo


def _dot1(a, b, f=_dot):
    return f(a.astype(BF16), b.astype(BF16))


def _dot3(a, b, f=_dot):
    ah, al = _split(a)
    bh, bl = _split(b)
    return f(ah, bh) + (f(ah, bl) + f(al, bh))


def _dot_exact_l(m_bf16, x, f=_dot):
    xh, xl = _split(x)
    return f(m_bf16, xh) + f(m_bf16, xl)


def _dot_exact_r(x, m_bf16, f=_dot):
    xh, xl = _split(x)
    return f(xh, m_bf16) + f(xl, m_bf16)


def _iota2(shape, dim):
    return lax.broadcasted_iota(jnp.int32, shape, dim)


def _sigmoid(x):
    return 1.0 / (1.0 + jnp.exp(-x))


def _softplus(x):
    return jnp.maximum(x, 0.0) + jnp.log(1.0 + jnp.exp(-jnp.abs(x)))


def _colsum(x):
    return jnp.sum(x, axis=0, keepdims=True)


def _rowsum(x):
    return jnp.sum(x, axis=-1, keepdims=True)


def _mm(a, b, out_dtype, name):
    M, K = a.shape
    K2, N = b.shape
    assert K == K2
    tm = _tile(M, 768)
    tn = _tile(N, max(128, (6 * 1024 * 1024) // (2 * K)))

    def body(a_ref, b_ref, o_ref):
        o_ref[...] = _dot(a_ref[...].astype(BF16), b_ref[...].astype(BF16)).astype(o_ref.dtype)

    return pl.pallas_call(
        body, name=name, grid=(N // tn, M // tm),
        in_specs=[pl.BlockSpec((tm, K), lambda j, i: (i, 0)), pl.BlockSpec((K, tn), lambda j, i: (0, j))],
        out_specs=pl.BlockSpec((tm, tn), lambda j, i: (i, j)),
        out_shape=jax.ShapeDtypeStruct((M, N), out_dtype),
        compiler_params=_cp(("parallel", "parallel")),
    )(a, b)


def _mm_tn(a, b, name):
    M, K = a.shape
    M2, N = b.shape
    assert M == M2
    tm = _tile(M, 1408)
    tk = _tile(K, 1408)
    tn = _tile(N, 1408)

    def body(a_ref, b_ref, o_ref):
        @pl.when(pl.program_id(2) == 0)
        def _():
            o_ref[...] = jnp.zeros_like(o_ref)
        o_ref[...] += _dot_tn(a_ref[...].astype(BF16), b_ref[...].astype(BF16))

    return pl.pallas_call(
        body, name=name, grid=(K // tk, N // tn, M // tm),
        in_specs=[pl.BlockSpec((tm, tk), lambda i, j, m: (m, i)), pl.BlockSpec((tm, tn), lambda i, j, m: (m, j))],
        out_specs=pl.BlockSpec((tk, tn), lambda i, j, m: (i, j)),
        out_shape=jax.ShapeDtypeStruct((K, N), F32),
        compiler_params=_cp(("parallel", "parallel", "arbitrary")),
    )(a, b)


def _rms(x):
    return lax.rsqrt(jnp.mean(x * x, axis=-1, keepdims=True) + NORM_EPS)


def _rms_bwd(x, w, dy):
    r = _rms(x)
    n = x * r
    dyw = dy * w
    dx = r * (dyw - n * jnp.mean(dyw * n, axis=-1, keepdims=True))
    return dx, dy * n


def _prenorm_fwd(h0, w, carry=None):
    LP, D = h0.shape
    T = _tile(LP, 512)

    def body(h_ref, w_ref, u_ref):
        h = h_ref[...]
        u_ref[...] = (h * _rms(h) * w_ref[...]).astype(BF16)

    return _call_carrying(
        carry, body, LP // T, name="prenorm_fwd",
        in_specs=[pl.BlockSpec((T, D), lambda i: (i, 0)), pl.BlockSpec((1, D), lambda i: (0, 0))],
        out_specs=[pl.BlockSpec((T, D), lambda i: (i, 0))],
        out_shape=[jax.ShapeDtypeStruct((LP, D), BF16)],
        operands=(h0, w))


def _prenorm_bwd(h0, w, du, dh1, carry=None):
    LP, D = h0.shape
    T = _tile(LP, 512)

    def body(h_ref, w_ref, du_ref, dh1_ref, dh0_ref, dw_ref):
        @pl.when(pl.program_id(0) == 0)
        def _():
            dw_ref[...] = jnp.zeros_like(dw_ref)
        dx, dwn = _rms_bwd(h_ref[...], w_ref[...], du_ref[...])
        dh0_ref[...] = dh1_ref[...] + dx
        dw_ref[...] += _colsum(dwn)

    row = pl.BlockSpec((T, D), lambda i: (i, 0))
    vec = pl.BlockSpec((1, D), lambda i: (0, 0))
    return _call_carrying(
        carry, body, LP // T, name="prenorm_bwd",
        in_specs=[row, vec, row, row], out_specs=[row, vec],
        out_shape=[jax.ShapeDtypeStruct((LP, D), F32), jax.ShapeDtypeStruct((1, D), F32)],
        operands=(h0, w, du, dh1))


def _causal_taps(ext, w_ref, width, start, rows):
    y = w_ref[width - 1:width, :] * ext[start:start + rows]
    for j in range(width - 1):
        y = y + w_ref[j:j + 1, :] * pltpu.roll(ext, width - 1 - j, 0)[start:start + rows]
    return y


def _shifted_rows(ext, shift, start, rows):
    return ext[start:start + rows] if shift == 0 else pltpu.roll(ext, shift, 0)[start:start + rows]


def _anticausal_taps(dy_ext, w_ref, width, rows):
    n = dy_ext.shape[0]
    dx = w_ref[width - 1:width, :] * dy_ext[0:rows]
    for j in range(width - 1):
        dx = dx + w_ref[j:j + 1, :] * pltpu.roll(dy_ext, n - (width - 1 - j), 0)[0:rows]
    return dx


def _gdn_gate_consts(alog_ref, dtb_ref, h):
    a_coef = -jnp.exp(alog_ref[0:1, h:h + 1])
    return a_coef, dtb_ref[0:1, h:h + 1]


def _gdn_pre_fwd(proj, conv_w, a_log, dt_bias, carry=None):
    LP = proj.shape[0]
    T = _tile(LP, 256)
    C = C_QKV
    H = GDN_HEADS

    def body(x_ref, halo_ref, ab_ref, cw_ref, alog_ref, dtb_ref, q_ref, k_ref, v_ref, beta_ref, g_ref):
        i = pl.program_id(0)
        ext = jnp.concatenate([jnp.where(i > 0, halo_ref[...], 0.0), x_ref[...]], axis=0)
        y = _causal_taps(ext, cw_ref, GDN_CONV, 8, T)
        c = y * _sigmoid(y)
        for h in range(H):
            sl = slice(h * GDN_D, (h + 1) * GDN_D)
            cq = c[:, sl]
            q_ref[:, sl] = cq * lax.rsqrt(_rowsum(cq * cq) + L2_EPS) * (GDN_D ** -0.5)
            ck = c[:, 512 + h * GDN_D:512 + (h + 1) * GDN_D]
            k_ref[:, sl] = ck * lax.rsqrt(_rowsum(ck * ck) + L2_EPS)
        v_ref[...] = c[:, 1024:]
        ab = ab_ref[...]
        valid = (i * T + _iota2((T, 1), 0)) >= PAD_ROWS
        for h in range(H):
            sl = slice(h * GDN_D, (h + 1) * GDN_D)
            a_coef, dtb = _gdn_gate_consts(alog_ref, dtb_ref, h)
            g = jnp.where(valid, a_coef * _softplus(ab[:, h:h + 1] + dtb), 0.0)
            beta = jnp.where(valid, _sigmoid(ab[:, H + h:H + h + 1]), 0.0)
            g_ref[:, sl] = jnp.broadcast_to(g, (T, GDN_D))
            beta_ref[:, sl] = jnp.broadcast_to(beta, (T, GDN_D))

    t8 = T // 8
    row512 = pl.BlockSpec((T, 512), lambda i: (i, 0))
    small = lambda r, c: pl.BlockSpec((r, c), lambda i: (0, 0))
    out = jax.ShapeDtypeStruct((LP, 512), F32)
    return _call_carrying(
        carry, body, LP // T, name="gdn_pre_fwd",
        in_specs=[pl.BlockSpec((T, C), lambda i: (i, 0)),
                  pl.BlockSpec((8, C), lambda i: (jnp.maximum(i * t8 - 1, 0), 0)),
                  pl.BlockSpec((T, C_AB), lambda i: (i, OFF_AB // C_AB)),
                  small(GDN_CONV, C), small(1, H), small(1, H)],
        out_specs=[row512] * 5, out_shape=[out] * 5,
        operands=(proj, proj, proj, conv_w, a_log, dt_bias))


def _gdn_pre_bwd(proj, conv_w, a_log, dt_bias, dq, dk, dv, dbeta, dg):
    LP = proj.shape[0]
    T = _tile(LP, 256)
    C = C_QKV
    H = GDN_HEADS
    TE = T + 8
    nt = LP // T

    def body(x_ref, xp_ref, xn_ref, ab_ref, cw_ref, alog_ref, dtb_ref,
             dq_ref, dqn_ref, dk_ref, dkn_ref, dv_ref, dvn_ref, dbeta_ref, dg_ref,
             dx_ref, dab_ref, dcw_ref, dsc_ref, dys):
        i = pl.program_id(0)

        @pl.when(i == 0)
        def _():
            dcw_ref[...] = jnp.zeros_like(dcw_ref)
            dsc_ref[...] = jnp.zeros_like(dsc_ref)

        last = i == nt - 1
        ext = jnp.concatenate([jnp.where(i > 0, xp_ref[...], 0.0), x_ref[...], jnp.where(last, 0.0, xn_ref[...])],
                              axis=0)
        y = _causal_taps(ext, cw_ref, GDN_CONV, 8, TE)
        sg = _sigmoid(y)
        c = y * sg
        nxt = lambda a_ref, b_ref: jnp.concatenate([a_ref[...], jnp.where(last, 0.0, b_ref[...])], axis=0)
        dqn = nxt(dq_ref, dqn_ref)
        dkn = nxt(dk_ref, dkn_ref)
        dvv = nxt(dv_ref, dvn_ref)
        for h in range(H):
            sl = slice(h * GDN_D, (h + 1) * GDN_D)
            cq = c[:, sl]
            rq = lax.rsqrt(_rowsum(cq * cq) + L2_EPS)
            nq = cq * rq
            dqh = dqn[:, sl]
            dys[:, sl] = (GDN_D ** -0.5) * rq * (dqh - nq * _rowsum(dqh * nq))
            sk = slice(512 + h * GDN_D, 512 + (h + 1) * GDN_D)
            ck = c[:, sk]
            rk = lax.rsqrt(_rowsum(ck * ck) + L2_EPS)
            nk = ck * rk
            dkh = dkn[:, sl]
            dys[:, sk] = rk * (dkh - nk * _rowsum(dkh * nk))
        dys[:, 1024:] = dvv
        dy = dys[...] * (sg * (1.0 + y * (1.0 - sg)))
        for j in range(GDN_CONV):
            dcw_ref[j:j + 1, :] += _colsum(dy[0:T, :] * _shifted_rows(ext, GDN_CONV - 1 - j, 8, T))
        dx_ref[...] = _anticausal_taps(dy, cw_ref, GDN_CONV, T).astype(BF16)
        ab = ab_ref[...]
        valid = (i * T + _iota2((T, 1), 0)) >= PAD_ROWS
        lane = _iota2((T, C_AB), 1)
        lane1 = _iota2((1, 128), 1)
        dab = jnp.zeros((T, C_AB), F32)
        dsc_a = jnp.zeros((1, 128), F32)
        dsc_d = jnp.zeros((1, 128), F32)
        for h in range(H):
            a_coef, dtb = _gdn_gate_consts(alog_ref, dtb_ref, h)
            pre = ab[:, h:h + 1] + dtb
            dgh = jnp.where(valid, dg_ref[:, h * GDN_D:h * GDN_D + 1], 0.0)
            da = dgh * a_coef * _sigmoid(pre)
            beta = _sigmoid(ab[:, H + h:H + h + 1])
            db = jnp.where(valid, dbeta_ref[:, h * GDN_D:h * GDN_D + 1], 0.0) * beta * (1.0 - beta)
            dab = dab + jnp.where(lane == h, da, 0.0) + jnp.where(lane == H + h, db, 0.0)
            dsc_a = dsc_a + jnp.where(lane1 == h, _colsum(dgh * a_coef * _softplus(pre)), 0.0)
            dsc_d = dsc_d + jnp.where(lane1 == h, _colsum(da), 0.0)
        dab_ref[...] = dab.astype(BF16)
        dsc_ref[0:1, :] += dsc_a
        dsc_ref[1:2, :] += dsc_d

    t8 = T // 8
    nb8 = LP // 8
    prev8 = lambda w: pl.BlockSpec((8, w), lambda i: (jnp.maximum(i * t8 - 1, 0), 0))
    next8 = lambda w: pl.BlockSpec((8, w), lambda i: (jnp.minimum((i + 1) * t8, nb8 - 1), 0))
    row = lambda w: pl.BlockSpec((T, w), lambda i: (i, 0))
    small = lambda r, c: pl.BlockSpec((r, c), lambda i: (0, 0))
    return pl.pallas_call(
        body, name="gdn_pre_bwd", grid=(nt,),
        in_specs=[row(C), prev8(C), next8(C), pl.BlockSpec((T, C_AB), lambda i: (i, OFF_AB // C_AB)),
                  small(GDN_CONV, C), small(1, H), small(1, H),
                  row(512), next8(512), row(512), next8(512), row(512), next8(512), row(512), row(512)],
        out_specs=[row(C), row(C_AB), small(GDN_CONV, C), small(2, 128)],
        out_shape=[jax.ShapeDtypeStruct((LP, C), BF16), jax.ShapeDtypeStruct((LP, C_AB), BF16),
                   jax.ShapeDtypeStruct((GDN_CONV, C), F32), jax.ShapeDtypeStruct((2, 128), F32)],
        scratch_shapes=[pltpu.VMEM((TE, C), F32)],
        compiler_params=_cp(("arbitrary",)),
    )(proj, proj, proj, proj, conv_w, a_log, dt_bias, dq, dq, dk, dk, dv, dv, dbeta, dg)


def _tri_masks():
    r = _iota2((GDN_CHUNK, GDN_CHUNK), 0)
    c = _iota2((GDN_CHUNK, GDN_CHUNK), 1)
    return r >= c, r > c


def _gdn_chunk_common(q, k, v, beta, gb):
    incl, strict = _tri_masks()
    l_incl = incl.astype(BF16)
    gd = _dot_exact_l(l_incl, jnp.where(strict, gb[:, :GDN_CHUNK], 0.0))
    gc = _dot_exact_l(l_incl, gb)
    decay = jnp.where(incl, jnp.exp(jnp.where(incl, gd, 0.0)), 0.0)
    exp_g = jnp.exp(gc)
    g_last = gc[GDN_CHUNK - 1:GDN_CHUNK, :]
    kd_fac = jnp.exp(g_last - gc)
    gl = jnp.exp(g_last)
    kb = k * beta
    kk = _dot1(kb, k, _dot_nt)
    return dict(incl=incl, strict=strict, decay=decay, exp_g=exp_g, kd_fac=kd_fac, gl=gl, kb=kb, kk=kk,
                vb=v * beta, kbg=kb * exp_g)


def _interleave(gens):
    gens = list(gens)
    while gens:
        alive = []
        for g in gens:
            try:
                next(g)
                alive.append(g)
            except StopIteration:
                pass
        gens = alive


def _call_carrying(ex, body, nsteps, *, name, in_specs, out_specs, out_shape, operands, scratch_shapes=()):
    n_in, n_out, n_scr = len(in_specs), len(out_specs), len(scratch_shapes)
    n = ex.n if ex is not None else 0

    def full(*refs):
        o0 = n_in + n
        s0 = o0 + n_out + n
        ex_refs = (refs[n_in:o0], refs[o0 + n_out:s0], refs[s0 + n_scr:])
        step = pl.program_id(0)
        _carry_begin(ex, ex_refs, step, nsteps)
        body(*refs[:n_in], *refs[o0:o0 + n_out], *refs[s0:s0 + n_scr])
        _carry_end(ex, ex_refs, step, nsteps)

    res = pl.pallas_call(
        full, name=name, grid=(nsteps,),
        in_specs=list(in_specs) + [ANY_SPEC] * n, out_specs=list(out_specs) + [ANY_SPEC] * n,
        out_shape=list(out_shape) + (ex.out_shapes if ex is not None else []),
        scratch_shapes=list(scratch_shapes) + (ex.scratch if ex is not None else []),
        compiler_params=pltpu.CompilerParams(dimension_semantics=("arbitrary",), vmem_limit_bytes=VMEM_LIMIT,
                                             has_side_effects=ex is not None),
    )(*operands, *(ex.arrs if ex is not None else []))
    return list(res[:n_out]), list(res[n_out:])


def _gdn_chunk_fwd(qn, kn, v, beta_b, g_b, carry=None):
    LP = qn.shape[0]
    R = GDN_ROWS
    H = GDN_HEADS
    CH = GDN_CHUNK

    def body(q_ref, k_ref, v_ref, b_ref, g_ref, u_ref, w_ref, qd_ref, kd_ref, qk_ref, t_ref, gl_ref):
        def item(cc, h):
            rs = slice(cc * CH, (cc + 1) * CH)
            sl = slice(h * GDN_D, (h + 1) * GDN_D)
            s64 = slice(h * CH, (h + 1) * CH)
            q, k = q_ref[rs, sl], k_ref[rs, sl]
            m = _gdn_chunk_common(q, k, v_ref[rs, sl], b_ref[rs, sl], g_ref[rs, sl])
            qk_raw = _dot1(q, k, _dot_nt)
            yield
            a = jnp.where(m["strict"], m["kk"] * m["decay"], 0.0)
            eye = (_iota2((CH, CH), 0) == _iota2((CH, CH), 1)).astype(F32)
            t = eye - a
            p = _dot3(a, a)
            yield
            for _ in range(4):
                t = t + _dot3(t, p)
                p = _dot3(p, p)
                yield
            t = t + _dot3(t, p)
            yield
            u_ref[rs, sl] = _dot1(t, m["vb"])
            w_ref[rs, sl] = _dot1(t, m["kbg"])
            qk_ref[rs, s64] = qk_raw * m["decay"]
            t_ref[rs, s64] = t
            qd_ref[rs, sl] = q * m["exp_g"]
            kd_ref[rs, sl] = k * m["kd_fac"]
            gl_ref[cc * 8:(cc + 1) * 8, sl] = jnp.broadcast_to(m["gl"], (8, GDN_D))

        _interleave(item(cc, h) for cc in range(R // CH) for h in range(H))

    row = lambda w: pl.BlockSpec((R, w), lambda i: (i, 0))
    o512 = jax.ShapeDtypeStruct((LP, 512), F32)
    o256 = jax.ShapeDtypeStruct((LP, 256), F32)
    return _call_carrying(
        carry, body, LP // R, name="gdn_chunk_fwd",
        in_specs=[row(512)] * 5,
        out_specs=[row(512)] * 4 + [row(256)] * 2 + [pl.BlockSpec((R // 8, 512), lambda i: (i, 0))],
        out_shape=[o512] * 4 + [o256] * 2 + [jax.ShapeDtypeStruct((LP // 8, 512), F32)],
        operands=(qn, kn, v, beta_b, g_b))


def _gdn_chunk_bwd(qn, kn, v, beta_b, g_b, t_all, du, dw, dqd, dkd, dqk, dgl):
    LP = qn.shape[0]
    R = GDN_ROWS
    H = GDN_HEADS
    CH = GDN_CHUNK

    def body(q_ref, k_ref, v_ref, b_ref, g_ref, t_ref, du_ref, dw_ref, dqd_ref, dkd_ref, dqk_ref, dgl_ref,
             dq_ref, dk_ref, dv_ref, db_ref, dg_ref):
        ones = jnp.ones((CH, GDN_D), BF16)

        def item(cc, h):
            rs = slice(cc * CH, (cc + 1) * CH)
            sl = slice(h * GDN_D, (h + 1) * GDN_D)
            s64 = slice(h * CH, (h + 1) * CH)
            q, k, vv, beta = q_ref[rs, sl], k_ref[rs, sl], v_ref[rs, sl], b_ref[rs, sl]
            m = _gdn_chunk_common(q, k, vv, beta, g_ref[rs, sl])
            incl, strict, decay = m["incl"], m["strict"], m["decay"]
            t = t_ref[rs, s64]
            du_, dw_ = du_ref[rs, sl], dw_ref[rs, sl]
            dqd_, dkd_ = dqd_ref[rs, sl], dkd_ref[rs, sl]
            d_t = _dot1(du_, m["vb"], _dot_nt) + _dot1(dw_, m["kbg"], _dot_nt)
            dvb = _dot1(t, du_, _dot_tn)
            dkbg = _dot1(t, dw_, _dot_tn)
            qk_raw = _dot1(q, k, _dot_nt)
            yield
            x1 = _dot3(d_t, t, _dot_nt)
            dkb = dkbg * m["exp_g"]
            d_gi = _rowsum(dkbg * m["kbg"])
            yield
            d_a = jnp.where(strict, -_dot3(t, x1, _dot_tn), 0.0)
            yield
            d_kk = d_a * decay
            dqk_m = jnp.where(incl, dqk_ref[rs, s64], 0.0)
            dqk_raw = dqk_m * decay
            mm = (d_a * m["kk"] + dqk_m * qk_raw) * decay
            dkb = dkb + _dot1(d_kk, k)
            dk_ = _dot1(d_kk, m["kb"], _dot_tn) + _dot1(dqk_raw, q, _dot_tn)
            dq_ = _dot1(dqk_raw, k) + dqd_ * m["exp_g"]
            d_gi = d_gi + (_dot_exact_r(mm, ones) - _dot_exact_r(mm, ones, _dot_tn))
            yield
            d_gi = d_gi + _rowsum(dqd_ * q * m["exp_g"])
            e = _rowsum(dkd_ * k * m["kd_fac"])
            d_gi = d_gi - e
            d_glast = _colsum(jnp.broadcast_to(e, (CH, GDN_D))) + dgl_ref[cc * 8:cc * 8 + 1, sl] * m["gl"]
            dk_ = dk_ + dkd_ * m["kd_fac"] + dkb * beta
            d_gi = d_gi + jnp.where(_iota2((CH, GDN_D), 0) == CH - 1, d_glast, 0.0)
            u_incl = (_iota2((CH, CH), 1) >= _iota2((CH, CH), 0)).astype(BF16)
            dq_ref[rs, sl] = dq_
            dk_ref[rs, sl] = dk_
            dv_ref[rs, sl] = dvb * beta
            db_ref[rs, sl] = jnp.broadcast_to(_rowsum(dvb * vv) + _rowsum(dkb * k), (CH, GDN_D))
            dg_ref[rs, sl] = _dot_exact_l(u_incl, d_gi)

        _interleave(item(cc, h) for cc in range(R // CH) for h in range(H))

    row = lambda w: pl.BlockSpec((R, w), lambda i: (i, 0))
    o512 = jax.ShapeDtypeStruct((LP, 512), F32)
    gl_spec = pl.BlockSpec((R // 8, 512), lambda i: (i, 0))
    return pl.pallas_call(
        body, name="gdn_chunk_bwd", grid=(LP // R,),
        in_specs=[row(512)] * 5 + [row(256)] + [row(512)] * 4 + [row(256), gl_spec],
        out_specs=[row(512)] * 5, out_shape=[o512] * 5,
        compiler_params=_cp(("parallel",)),
    )(qn, kn, v, beta_b, g_b, t_all, du, dw, dqd, dkd, dqk, dgl)


def _gdn_scan_fwd(u, w, qd, kd, qk, gl):
    LP = u.shape[0]
    CH = GDN_CHUNK
    N = LP // CH
    H = GDN_HEADS

    def body(u_ref, w_ref, qd_ref, kd_ref, qk_ref, gl_ref, o_ref, ssave_ref, s_sc):
        @pl.when(pl.program_id(0) == 0)
        def _():
            s_sc[...] = jnp.zeros_like(s_sc)
        ssave_ref[...] = s_sc[...]

        def item(h):
            sl = slice(h * GDN_D, (h + 1) * GDN_D)
            s = s_sc[:, sl]
            v_new = u_ref[:, sl] - _dot1(w_ref[:, sl], s)
            o_s = _dot1(qd_ref[:, sl], s)
            yield
            o_ref[:, sl] = o_s + _dot1(qk_ref[:, h * CH:(h + 1) * CH], v_new)
            s_sc[:, sl] = s * gl_ref[0:1, sl] + _dot1(kd_ref[:, sl], v_new, _dot_tn)

        _interleave(item(h) for h in range(H))

    row = lambda w_: pl.BlockSpec((CH, w_), lambda n: (n, 0))
    return pl.pallas_call(
        body, name="gdn_scan_fwd", grid=(N,),
        in_specs=[row(512)] * 4 + [row(256), pl.BlockSpec((8, 512), lambda n: (n, 0))],
        out_specs=[row(512), pl.BlockSpec((GDN_D, 512), lambda n: (n, 0))],
        out_shape=[jax.ShapeDtypeStruct((LP, 512), F32), jax.ShapeDtypeStruct((N * GDN_D, 512), F32)],
        scratch_shapes=[pltpu.VMEM((GDN_D, 512), F32)],
        compiler_params=_cp(("arbitrary",)),
    )(u, w, qd, kd, qk, gl)


def _gdn_scan_bwd(u, w, qd, kd, qk, gl, ssave, do, carry=None):
    LP = u.shape[0]
    CH = GDN_CHUNK
    N = LP // CH
    H = GDN_HEADS

    def body(u_ref, w_ref, qd_ref, kd_ref, qk_ref, gl_ref, s_ref, do_ref,
             du_ref, dw_ref, dqd_ref, dkd_ref, dqk_ref, dgl_ref, ds_sc):
        @pl.when(pl.program_id(0) == 0)
        def _():
            ds_sc[...] = jnp.zeros_like(ds_sc)
        def item(h):
            sl = slice(h * GDN_D, (h + 1) * GDN_D)
            s64 = slice(h * CH, (h + 1) * CH)
            s = s_ref[:, sl]
            ds = ds_sc[:, sl]
            do_ = do_ref[:, sl]
            w_, qd_, kd_, qk_ = w_ref[:, sl], qd_ref[:, sl], kd_ref[:, sl], qk_ref[:, s64]
            v_new = u_ref[:, sl] - _dot1(w_, s)
            d_vnew = _dot1(qk_, do_, _dot_tn) + _dot1(kd_, ds)
            dqd_ref[:, sl] = _dot1(do_, s, _dot_nt)
            ds_new = ds * gl_ref[0:1, sl] + _dot1(qd_, do_, _dot_tn)
            dgl_ref[:, sl] = jnp.broadcast_to(jnp.sum(_colsum(ds * s), axis=-1, keepdims=True), (8, GDN_D))
            yield
            du_ref[:, sl] = d_vnew
            dw_ref[:, sl] = -_dot1(d_vnew, s, _dot_nt)
            dkd_ref[:, sl] = _dot1(v_new, ds, _dot_nt)
            dqk_ref[:, s64] = _dot1(do_, v_new, _dot_nt)
            ds_sc[:, sl] = ds_new - _dot1(w_, d_vnew, _dot_tn)

        _interleave(item(h) for h in range(H))

    rev = lambda w_: pl.BlockSpec((CH, w_), lambda n: (N - 1 - n, 0))
    rev8 = pl.BlockSpec((8, 512), lambda n: (N - 1 - n, 0))
    o512 = jax.ShapeDtypeStruct((LP, 512), F32)
    return _call_carrying(
        carry, body, N, name="gdn_scan_bwd",
        in_specs=[rev(512)] * 4 + [rev(256), rev8, pl.BlockSpec((GDN_D, 512), lambda n: (N - 1 - n, 0)), rev(512)],
        out_specs=[rev(512)] * 4 + [rev(256), rev8],
        out_shape=[o512] * 4 + [jax.ShapeDtypeStruct((LP, 256), F32), jax.ShapeDtypeStruct((LP // 8, 512), F32)],
        scratch_shapes=[pltpu.VMEM((GDN_D, 512), F32)],
        operands=(u, w, qd, kd, qk, gl, ssave, do))


def _sb_scores(qh, kblk, mask):
    z = _dot_nt(qh, kblk)
    e = jnp.exp(-jnp.abs(z))
    sp = jnp.maximum(z, 0.0) + jnp.log(1.0 + e)
    return z, e, jnp.where(mask, -sp, 0.0), z - sp


def _sb_fwd(proj):
    LP = proj.shape[0]
    B = SB_BLOCK
    W = min(SB_SPAN, LP)
    SUB = SB_SUB
    Q = min(SB_QTILE, LP)
    nq = LP // Q
    nsub = W // SUB
    scale = SB_DH ** -0.5
    qcol, kcol, vcol = OFF_SB // B, (OFF_SB + 512) // B, (OFF_SB + 1024) // B

    def body(q_ref, k_ref, v_ref, o_ref, c_ref, n_ref):
        i = pl.program_id(1)
        lane = _iota2((Q, B), 1)
        head_a = lane < SB_DH
        qs = q_ref[...] * scale
        qh = [jnp.where(head_a, qs, 0.0).astype(BF16), jnp.where(head_a, 0.0, qs).astype(BF16)]
        u_strict = (_iota2((SUB, SUB), 0) > _iota2((SUB, SUB), 1)).astype(BF16)
        qpos = i * Q + _iota2((Q, W), 0)
        hi0 = (i + 1) * Q
        nspan = (hi0 + W - 1) // W

        def live(st):
            return (st[0] < nspan) & (st[1] > 0)

        def span(st):
            r, carry = st[0], st[2:]
            hi = hi0 - r * W
            k0 = pl.multiple_of(jnp.maximum(hi - W, 0), B)
            kblk = k_ref[pl.ds(k0, W), :].astype(BF16)
            vblk = v_ref[pl.ds(k0, W), :].astype(BF16)
            kpos = k0 + _iota2((Q, W), 1)
            mask = (kpos < qpos) & (kpos >= PAD_ROWS) & (kpos < hi)
            new = [None] * 4

            def head(h):
                o_acc, c = carry[2 * h], carry[2 * h + 1]
                z, e, l1m, lsg = _sb_scores(qh[h], kblk, mask)
                yield
                subs = [slice(b * SUB, (b + 1) * SUB) for b in range(nsub)]
                suf = [_dot(l1m[:, bs].astype(BF16), u_strict) for bs in subs]
                yield
                parts = [None] * nsub
                for b in reversed(range(nsub)):
                    parts[b] = jnp.where(mask[:, subs[b]], jnp.exp(lsg[:, subs[b]] + suf[b] + c), 0.0)
                    c = c + _rowsum(l1m[:, subs[b]])
                att = jnp.concatenate(parts, axis=1).astype(BF16)
                new[2 * h], new[2 * h + 1] = o_acc + _dot(att, vblk), c

            _interleave(head(h) for h in range(2))
            more = (jnp.maximum(jnp.max(new[1]), jnp.max(new[3])) > SB_DEAD).astype(jnp.int32)
            return (r + 1, more, *new)

        zero_o = jnp.zeros((Q, B), F32)
        zero_c = jnp.zeros((Q, 1), F32)
        nrun, _, o_a, c_a, o_b, c_b = lax.while_loop(
            live, span, (jnp.int32(0), jnp.int32(1), zero_o, zero_c, zero_o, zero_c))
        o_ref[...] = jnp.where(head_a, o_a, o_b)
        c_ref[...] = jnp.where(head_a, c_a, c_b)
        n_ref[pl.program_id(0), i] = nrun

    blk = pl.BlockSpec((Q, B), lambda p, i: (i, p))
    out = jax.ShapeDtypeStruct((LP, 512), F32)
    return pl.pallas_call(
        body, name="sb_fwd", grid=(SB_HEADS // 2, nq),
        in_specs=[pl.BlockSpec((Q, B), lambda p, i: (i, qcol + p)),
                  pl.BlockSpec((LP, B), lambda p, i: (0, kcol + p)),
                  pl.BlockSpec((LP, B), lambda p, i: (0, vcol + p))],
        out_specs=[blk, blk, pl.BlockSpec(memory_space=pltpu.SMEM)],
        out_shape=[out, out, jax.ShapeDtypeStruct((SB_HEADS // 2, nq), jnp.int32)],
        compiler_params=_cp(("arbitrary", "arbitrary")),
    )(proj, proj, proj)


def _sb_bwd(proj, ctot, nrun_all, do):
    LP = proj.shape[0]
    B = SB_BLOCK
    W = min(SB_SPAN, LP)
    SUB = SB_SUB
    Q = min(SB_QTILE, LP)
    nq = LP // Q
    nsub = W // SUB
    scale = SB_DH ** -0.5
    qcol, kcol, vcol = OFF_SB // B, (OFF_SB + 512) // B, (OFF_SB + 1024) // B

    def body(n_ref, q_ref, k_ref, v_ref, c_ref, do_ref, dq_ref, dk_ref, dv_ref):
        i = pl.program_id(1)

        @pl.when(i == 0)
        def _():
            dk_ref[...] = jnp.zeros_like(dk_ref)
            dv_ref[...] = jnp.zeros_like(dv_ref)

        lane = _iota2((Q, B), 1)
        head_a = lane < SB_DH
        qs = q_ref[...] * scale
        qh = [jnp.where(head_a, qs, 0.0).astype(BF16), jnp.where(head_a, 0.0, qs).astype(BF16)]
        dof = do_ref[...]
        doh = [jnp.where(head_a, dof, 0.0).astype(BF16), jnp.where(head_a, 0.0, dof).astype(BF16)]
        cfull = c_ref[...]
        ctot_h = [cfull[:, 0:1], cfull[:, SB_DH:SB_DH + 1]]
        sub_r, sub_c = _iota2((SUB, SUB), 0), _iota2((SUB, SUB), 1)
        u_strict = (sub_r > sub_c).astype(BF16)
        l_strict = (sub_r < sub_c).astype(BF16)
        qpos = i * Q + _iota2((Q, W), 0)
        hi0 = (i + 1) * Q
        nrun = n_ref[pl.program_id(0), i]

        def span(t, carry):
            r = nrun - 1 - t
            hi = hi0 - r * W
            k0 = pl.multiple_of(jnp.maximum(hi - W, 0), B)
            kblk = k_ref[pl.ds(k0, W), :].astype(BF16)
            vblk = v_ref[pl.ds(k0, W), :].astype(BF16)
            kpos = k0 + _iota2((Q, W), 1)
            mask = (kpos < qpos) & (kpos >= PAD_ROWS) & (kpos < hi)
            new = [None] * 6
            dk_add, dv_add = [None, None], [None, None]
            subs = [slice(b * SUB, (b + 1) * SUB) for b in range(nsub)]

            def head(h):
                dq_acc, pre, ecar = carry[3 * h], carry[3 * h + 1], carry[3 * h + 2]
                z, e, l1m, lsg = _sb_scores(qh[h], kblk, mask)
                d_att = _dot_nt(doh[h], vblk)
                yield
                sig = jnp.where(z >= 0.0, 1.0, e) / (1.0 + e)
                suf = [_dot(l1m[:, bs].astype(BF16), u_strict) for bs in subs]
                yield
                att_parts, p_parts = [None] * nsub, [None] * nsub
                for b, bs in enumerate(subs):
                    pre = pre + _rowsum(l1m[:, bs])
                    att_parts[b] = jnp.where(mask[:, bs], jnp.exp(lsg[:, bs] + suf[b] + (ctot_h[h] - pre)), 0.0)
                    p_parts[b] = att_parts[b] * d_att[:, bs]
                pcum = [_dot(p.astype(BF16), l_strict) for p in p_parts]
                yield
                dz_parts = [None] * nsub
                for b, bs in enumerate(subs):
                    sg = sig[:, bs]
                    dz_parts[b] = jnp.where(mask[:, bs], p_parts[b] * (1.0 - sg) - sg * (ecar + pcum[b]), 0.0)
                    ecar = ecar + _rowsum(p_parts[b])
                att = jnp.concatenate(att_parts, axis=1).astype(BF16)
                dz = jnp.concatenate(dz_parts, axis=1).astype(BF16)
                new[3 * h:3 * h + 3] = [dq_acc + _dot(dz, kblk), pre, ecar]
                dk_add[h] = _dot_tn(dz, qh[h])
                dv_add[h] = _dot_tn(att, doh[h])

            _interleave(head(h) for h in range(2))
            dk_ref[pl.ds(k0, W), :] += dk_add[0] + dk_add[1]
            dv_ref[pl.ds(k0, W), :] += dv_add[0] + dv_add[1]
            return tuple(new)

        zero_o = jnp.zeros((Q, B), F32)
        zero_c = jnp.zeros((Q, 1), F32)
        res = lax.fori_loop(0, nrun, span, (zero_o, zero_c, zero_c, zero_o, zero_c, zero_c))
        dq_ref[...] = (jnp.where(head_a, res[0], res[3]) * scale).astype(BF16)

    blk = pl.BlockSpec((Q, B), lambda p, i: (i, p))
    col = pl.BlockSpec((LP, B), lambda p, i: (0, p))
    out = jax.ShapeDtypeStruct((LP, 512), F32)
    return pl.pallas_call(
        body, name="sb_bwd", grid=(SB_HEADS // 2, nq),
        in_specs=[pl.BlockSpec(memory_space=pltpu.SMEM),
                  pl.BlockSpec((Q, B), lambda p, i: (i, qcol + p)),
                  pl.BlockSpec((LP, B), lambda p, i: (0, kcol + p)),
                  pl.BlockSpec((LP, B), lambda p, i: (0, vcol + p)),
                  blk, blk],
        out_specs=[blk, col, col], out_shape=[jax.ShapeDtypeStruct((LP, 512), BF16), out, out],
        compiler_params=_cp(("arbitrary", "arbitrary")),
    )(nrun_all, proj, proj, proj, ctot, do)


def _sb_group_mean():
    r = jnp.right_shift(_iota2((512, 512), 0), 6)
    c = jnp.right_shift(_iota2((512, 512), 1), 6)
    return jnp.where(r == c, 1.0 / SB_DH, 0.0).astype(BF16)


def _attn_norm_fwd(og, proj, osb, gnw, snw):
    LP = og.shape[0]
    T = _tile(LP, 256)

    def body(og_ref, z_ref, os_ref, gnw_ref, snw_ref, y_ref):
        valid = (pl.program_id(0) * T + _iota2((T, 1), 0)) >= PAD_ROWS
        z = z_ref[...]
        zg = z * _sigmoid(z)
        for h in range(GDN_HEADS):
            sl = slice(h * GDN_D, (h + 1) * GDN_D)
            o = og_ref[:, sl]
            y = o * _rms(o) * gnw_ref[...] * zg[:, sl]
            y_ref[:, sl] = jnp.where(valid, y, 0.0).astype(BF16)
        o = os_ref[...]
        msq = _dot_exact_r(o * o, _sb_group_mean())
        y = o * lax.rsqrt(msq + NORM_EPS) * snw_ref[...]
        y_ref[:, 512:] = jnp.where(valid, y, 0.0).astype(BF16)

    row = pl.BlockSpec((T, 512), lambda i: (i, 0))
    return pl.pallas_call(
        body, name="attn_norm_fwd", grid=(LP // T,),
        in_specs=[row, pl.BlockSpec((T, 512), lambda i: (i, OFF_Z // 512)), row,
                  pl.BlockSpec((1, GDN_D), lambda i: (0, 0)), pl.BlockSpec((1, 512), lambda i: (0, 0))],
        out_specs=pl.BlockSpec((T, 1024), lambda i: (i, 0)),
        out_shape=jax.ShapeDtypeStruct((LP, 1024), BF16),
        compiler_params=_cp(("parallel",)),
    )(og, proj, osb, gnw, snw)


def _attn_norm_bwd(og, proj, osb, gnw, snw, dy, carry=None):
    LP = og.shape[0]
    T = _tile(LP, 256)

    def body(og_ref, z_ref, os_ref, gnw_ref, snw_ref, dy_ref, dog_ref, dz_ref, dos_ref, dgw_ref, dsw_ref):
        @pl.when(pl.program_id(0) == 0)
        def _():
            dgw_ref[...] = jnp.zeros_like(dgw_ref)
            dsw_ref[...] = jnp.zeros_like(dsw_ref)
        valid = (pl.program_id(0) * T + _iota2((T, 1), 0)) >= PAD_ROWS
        dy = jnp.where(valid, dy_ref[...], 0.0)
        z = z_ref[...]
        sg = _sigmoid(z)
        zg = z * sg
        dgw = jnp.zeros((1, GDN_D), F32)
        for h in range(GDN_HEADS):
            sl = slice(h * GDN_D, (h + 1) * GDN_D)
            o = og_ref[:, sl]
            dyh = dy[:, sl]
            dx, dwn = _rms_bwd(o, gnw_ref[...], dyh * zg[:, sl])
            dog_ref[:, sl] = dx
            dgw = dgw + _colsum(dwn)
            yn = o * _rms(o) * gnw_ref[...]
            dz_ref[:, sl] = (dyh * yn * (sg[:, sl] * (1.0 + z[:, sl] * (1.0 - sg[:, sl])))).astype(BF16)
        dgw_ref[...] += dgw
        o = os_ref[...]
        gm = _sb_group_mean()
        r = lax.rsqrt(_dot_exact_r(o * o, gm) + NORM_EPS)
        n = o * r
        dys = dy[:, 512:]
        dyw = dys * snw_ref[...]
        dos_ref[...] = r * (dyw - n * _dot_exact_r(dyw * n, gm))
        dsw_ref[...] += _colsum(dys * n)

    row = pl.BlockSpec((T, 512), lambda i: (i, 0))
    gw = pl.BlockSpec((1, GDN_D), lambda i: (0, 0))
    sw = pl.BlockSpec((1, 512), lambda i: (0, 0))
    o512 = jax.ShapeDtypeStruct((LP, 512), F32)
    return _call_carrying(
        carry, body, LP // T, name="attn_norm_bwd",
        in_specs=[row, pl.BlockSpec((T, 512), lambda i: (i, OFF_Z // 512)), row, gw, sw,
                  pl.BlockSpec((T, 1024), lambda i: (i, 0))],
        out_specs=[row, row, row, gw, sw],
        out_shape=[o512, jax.ShapeDtypeStruct((LP, 512), BF16), o512, jax.ShapeDtypeStruct((1, GDN_D), F32),
                   jax.ShapeDtypeStruct((1, 512), F32)],
        operands=(og, proj, osb, gnw, snw, dy))


def _resid_fwd(h0, mix, w_post, w_pre):
    LP, D = h0.shape
    T = _tile(LP, 512)

    def body(h0_ref, mix_ref, wp_ref, wf_ref, h1_ref, n2_ref):
        mix = mix_ref[...]
        h1 = h0_ref[...] + mix * _rms(mix) * wp_ref[...]
        h1_ref[...] = h1
        n2_ref[...] = (h1 * _rms(h1) * wf_ref[...]).astype(BF16)

    row = pl.BlockSpec((T, D), lambda i: (i, 0))
    vec = pl.BlockSpec((1, D), lambda i: (0, 0))
    return pl.pallas_call(
        body, name="resid_fwd", grid=(LP // T,),
        in_specs=[row, row, vec, vec], out_specs=[row, row],
        out_shape=[jax.ShapeDtypeStruct((LP, D), F32), jax.ShapeDtypeStruct((LP, D), BF16)],
        compiler_params=_cp(("parallel",)),
    )(h0, mix, w_post, w_pre)


def _resid_bwd(h1, mix, w_post, w_pre, dout, dn2):
    LP, D = h1.shape
    T = _tile(LP, 512)

    def body(h1_ref, mix_ref, wp_ref, wf_ref, dout_ref, dn2_ref, dh1_ref, dmix_ref, dwf_ref, dwp_ref):
        @pl.when(pl.program_id(0) == 0)
        def _():
            dwf_ref[...] = jnp.zeros_like(dwf_ref)
            dwp_ref[...] = jnp.zeros_like(dwp_ref)
        dx, dwn = _rms_bwd(h1_ref[...], wf_ref[...], dn2_ref[...])
        dh1 = dout_ref[...] + dx
        dh1_ref[...] = dh1
        dwf_ref[...] += _colsum(dwn)
        dmix, dwn2 = _rms_bwd(mix_ref[...], wp_ref[...], dh1)
        dmix_ref[...] = dmix.astype(BF16)
        dwp_ref[...] += _colsum(dwn2)

    row = pl.BlockSpec((T, D), lambda i: (i, 0))
    vec = pl.BlockSpec((1, D), lambda i: (0, 0))
    v = jax.ShapeDtypeStruct((1, D), F32)
    return pl.pallas_call(
        body, name="resid_bwd", grid=(LP // T,),
        in_specs=[row, row, vec, vec, row, row], out_specs=[row, row, vec, vec],
        out_shape=[jax.ShapeDtypeStruct((LP, D), F32), jax.ShapeDtypeStruct((LP, D), BF16), v, v],
        compiler_params=_cp(("arbitrary",)),
    )(h1, mix, w_post, w_pre, dout, dn2)


GELU_C = 0.7978845608028654
GELU_A = 0.044715


def _gelu_parts(x):
    t = jnp.tanh(GELU_C * (x + GELU_A * x * x * x))
    return 0.5 * x * (1.0 + t), t


def _convglu_fwd(up, conv_w, conv_b):
    LP, C = up.shape
    T = _tile(LP, 128)

    def body(x_ref, halo_ref, cw_ref, cb_ref, act_ref):
        i = pl.program_id(0)

        def conv(cols):
            ext = jnp.concatenate([jnp.where(i > 0, halo_ref[:, cols], 0.0), x_ref[:, cols]], axis=0)
            w = cw_ref[:, cols]
            return (w[2:3] * ext[8:] + w[1:2] * pltpu.roll(ext, 1, 0)[8:] + w[0:1] * pltpu.roll(ext, 2, 0)[8:]
                    + cb_ref[:, cols])

        for s in range(D_FF // LANE):
            gs = slice(s * LANE, (s + 1) * LANE)
            g, _ = _gelu_parts(conv(gs))
            act_ref[:, gs] = (g * conv(slice(D_FF + s * LANE, D_FF + (s + 1) * LANE))).astype(BF16)

    t8 = T // 8
    return pl.pallas_call(
        body, name="convglu_fwd", grid=(LP // T,),
        in_specs=[pl.BlockSpec((T, C), lambda i: (i, 0)),
                  pl.BlockSpec((8, C), lambda i: (jnp.maximum(i * t8 - 1, 0), 0)),
                  pl.BlockSpec((FFN_CONV, C), lambda i: (0, 0)), pl.BlockSpec((1, C), lambda i: (0, 0))],
        out_specs=pl.BlockSpec((T, D_FF), lambda i: (i, 0)),
        out_shape=jax.ShapeDtypeStruct((LP, D_FF), BF16),
        compiler_params=_cp(("parallel",)),
    )(up, up, conv_w, conv_b)


def _convglu_bwd(up, conv_w, conv_b, dact):
    LP, C = up.shape
    T = _tile(LP, 128)
    TE = T + 8
    nt = LP // T

    def body(x_ref, xp_ref, xn_ref, cw_ref, cb_ref, da_ref, dan_ref, dx_ref, dcw_ref, dcb_ref):
        i = pl.program_id(0)

        @pl.when(i == 0)
        def _():
            dcw_ref[...] = jnp.zeros_like(dcw_ref)
            dcb_ref[...] = jnp.zeros_like(dcb_ref)

        last = i == nt - 1

        def conv(cols):
            ext = jnp.concatenate([jnp.where(i > 0, xp_ref[:, cols], 0.0), x_ref[:, cols],
                                   jnp.where(last, 0.0, xn_ref[:, cols])], axis=0)
            sh = [ext[8:8 + TE], pltpu.roll(ext, 1, 0)[8:8 + TE], pltpu.roll(ext, 2, 0)[8:8 + TE]]
            w = cw_ref[:, cols]
            return w[2:3] * sh[0] + w[1:2] * sh[1] + w[0:1] * sh[2] + cb_ref[:, cols], sh, w

        def back(cols, dy, sh, w):
            dy_t = dy[0:T]
            dcb_ref[:, cols] += _colsum(dy_t)
            for j in range(FFN_CONV):
                dcw_ref[j:j + 1, cols] += _colsum(dy_t * sh[FFN_CONV - 1 - j][0:T])
            dx_ref[:, cols] = (w[2:3] * dy_t + w[1:2] * pltpu.roll(dy, TE - 1, 0)[0:T]
                               + w[0:1] * pltpu.roll(dy, TE - 2, 0)[0:T]).astype(BF16)

        for s in range(D_FF // LANE):
            gs = slice(s * LANE, (s + 1) * LANE)
            vs = slice(D_FF + s * LANE, D_FF + (s + 1) * LANE)
            gate, sh_g, w_g = conv(gs)
            val, sh_v, w_v = conv(vs)
            g, t = _gelu_parts(gate)
            dg_dx = 0.5 * (1.0 + t) + 0.5 * gate * (1.0 - t * t) * GELU_C * (1.0 + 3.0 * GELU_A * gate * gate)
            da = jnp.concatenate([da_ref[:, gs], jnp.where(last, 0.0, dan_ref[:, gs])], axis=0)
            back(gs, da * val * dg_dx, sh_g, w_g)
            back(vs, da * g, sh_v, w_v)

    t8 = T // 8
    nb8 = LP // 8
    prev8 = lambda w: pl.BlockSpec((8, w), lambda i: (jnp.maximum(i * t8 - 1, 0), 0))
    next8 = lambda w: pl.BlockSpec((8, w), lambda i: (jnp.minimum((i + 1) * t8, nb8 - 1), 0))
    row = lambda w: pl.BlockSpec((T, w), lambda i: (i, 0))
    small = lambda r: pl.BlockSpec((r, C), lambda i: (0, 0))
    return pl.pallas_call(
        body, name="convglu_bwd", grid=(nt,),
        in_specs=[row(C), prev8(C), next8(C), small(FFN_CONV), small(1), row(D_FF), next8(D_FF)],
        out_specs=[row(C), small(FFN_CONV), small(1)],
        out_shape=[jax.ShapeDtypeStruct((LP, C), BF16), jax.ShapeDtypeStruct((FFN_CONV, C), F32),
                   jax.ShapeDtypeStruct((1, C), F32)],
        compiler_params=_cp(("arbitrary",)),
    )(up, up, up, conv_w, conv_b, dact, dact)


def _final(h1, f, w_post, target, n_real):
    LP, D = h1.shape
    T = _tile(LP, 256)

    def body(h1_ref, f_ref, w_ref, t_ref, loss_ref, dout_ref, df_ref, dw_ref):
        @pl.when(pl.program_id(0) == 0)
        def _():
            loss_ref[...] = jnp.zeros_like(loss_ref)
            dw_ref[...] = jnp.zeros_like(dw_ref)
        rows = pl.program_id(0) * T + _iota2((T, 1), 0)
        real = (rows >= ROW0) & (rows < ROW0 + n_real)
        f = f_ref[...]
        out = h1_ref[...] + f * _rms(f) * w_ref[...]
        err = jnp.where(real, out - t_ref[...], 0.0)
        loss_ref[...] += 0.5 * jnp.sum(_colsum(jnp.mean(err * err, axis=-1, keepdims=True)), axis=-1, keepdims=True)
        dout = err * (1.0 / D)
        dout_ref[...] = dout
        dx, dwn = _rms_bwd(f, w_ref[...], dout)
        df_ref[...] = dx.astype(BF16)
        dw_ref[...] += _colsum(dwn)

    row = pl.BlockSpec((T, D), lambda i: (i, 0))
    vec = pl.BlockSpec((1, D), lambda i: (0, 0))
    return pl.pallas_call(
        body, name="final_loss", grid=(LP // T,),
        in_specs=[row, row, vec, row],
        out_specs=[pl.BlockSpec((1, 128), lambda i: (0, 0)), row, row, vec],
        out_shape=[jax.ShapeDtypeStruct((1, 128), F32), jax.ShapeDtypeStruct((LP, D), F32),
                   jax.ShapeDtypeStruct((LP, D), BF16), jax.ShapeDtypeStruct((1, D), F32)],
        compiler_params=_cp(("arbitrary",)),
    )(h1, f, w_post, target)


ANY_SPEC = pl.BlockSpec(memory_space=pl.ANY)
N_CHIP = 4


def _other_chips(x, y):
    return [(1 - x, y), (x, 1 - y), (1 - x, 1 - y)]


def _gather_direct(arrs, name):
    n = len(arrs)
    npeer = N_DEV - 1

    def body(*refs):
        ins, outs = refs[:n], refs[n:2 * n]
        send_sems, recv_sems, loc_sems = refs[2 * n:]
        x, y, c = lax.axis_index("x"), lax.axis_index("y"), lax.axis_index("c")
        me = 4 * x + 2 * y + c
        copies = []
        for a in range(n):
            for kk in range(1, N_DEV):
                px = 1 - x if kk & 4 else x
                py = 1 - y if kk & 2 else y
                pc = 1 - c if kk & 1 else c
                s = a * npeer + kk - 1
                cp = pltpu.make_async_remote_copy(src_ref=ins[a], dst_ref=outs[a].at[me], send_sem=send_sems.at[s],
                                                  recv_sem=recv_sems.at[s], device_id=(px, py, pc), device_id_type=MESH)
                cp.start()
                copies.append(cp)
            own = pltpu.make_async_copy(ins[a], outs[a].at[me], loc_sems.at[a])
            own.start()
            copies.append(own)
        for cp in copies:
            cp.wait()

    shapes = [jax.ShapeDtypeStruct((N_DEV,) + tuple(a.shape), a.dtype) for a in arrs]
    return pl.pallas_call(
        body, name=name, in_specs=[ANY_SPEC] * n, out_specs=[ANY_SPEC] * n, out_shape=shapes,
        scratch_shapes=[pltpu.SemaphoreType.DMA((n * npeer,)), pltpu.SemaphoreType.DMA((n * npeer,)),
                        pltpu.SemaphoreType.DMA((n,))],
        compiler_params=pltpu.CompilerParams(has_side_effects=True),
    )(*arrs)


class _Exchange:
    def __init__(self, arrs, out_shapes, scratch, start, finish, mid=None):
        self.arrs, self.out_shapes, self.scratch = list(arrs), list(out_shapes), list(scratch)
        self.start, self.finish, self.mid = start, finish, mid

    @property
    def n(self):
        return len(self.arrs)


def _run_exchange(ex, name):
    n = ex.n

    def body(*refs):
        ins, outs, sems = refs[:n], refs[n:2 * n], refs[2 * n:]
        ex.start(ins, outs, sems)
        if ex.mid is not None:
            ex.mid(ins, outs, sems)
        ex.finish(ins, outs, sems)

    return pl.pallas_call(
        body, name=name, in_specs=[ANY_SPEC] * n, out_specs=[ANY_SPEC] * n, out_shape=ex.out_shapes,
        scratch_shapes=ex.scratch, compiler_params=pltpu.CompilerParams(has_side_effects=True),
    )(*ex.arrs)


def _carry_begin(ex, refs, step, nsteps):
    if ex is None:
        return

    @pl.when(step == 0)
    def _():
        ex.start(*refs)

    if ex.mid is not None:
        @pl.when(step == min(nsteps - 1, (3 * nsteps) // 5))
        def _():
            ex.mid(*refs)


def _carry_end(ex, refs, step, nsteps):
    if ex is None:
        return

    @pl.when(step == nsteps - 1)
    def _():
        ex.finish(*refs)


def _gather_two_level(arrs):
    n = len(arrs)
    K = 7

    def env(ins, outs, sems):
        send_sems, recv_sems, loc_sems = sems
        x, y, c = lax.axis_index("x"), lax.axis_index("y"), lax.axis_index("c")

        def cp(a, k, src, slot, to):
            return pltpu.make_async_remote_copy(src_ref=src, dst_ref=outs[a].at[slot], send_sem=send_sems.at[a * K + k],
                                                recv_sem=recv_sems.at[a * K + k], device_id=to, device_id_type=MESH)

        me = 4 * x + 2 * y + c
        owns = [pltpu.make_async_copy(ins[a], outs[a].at[me], loc_sems.at[a]) for a in range(n)]
        first = []
        for a in range(n):
            first.append(cp(a, 0, ins[a], me, (x, y, 1 - c)))
            first += [cp(a, 1 + j, ins[a], me, (px, py, c)) for j, (px, py) in enumerate(_other_chips(x, y))]
        passed = []
        for j, (px, py) in enumerate(_other_chips(x, y)):
            slot = 4 * px + 2 * py + c
            passed += [(cp(a, 1 + j, ins[a], slot, (px, py, c)), cp(a, 4 + j, outs[a].at[slot], slot, (x, y, 1 - c)))
                       for a in range(n)]
        from_sib = []
        for a in range(n):
            from_sib.append(cp(a, 0, ins[a], 4 * x + 2 * y + (1 - c), (x, y, 1 - c)))
            from_sib += [cp(a, 4 + j, ins[a], 4 * px + 2 * py + (1 - c), (x, y, 1 - c))
                         for j, (px, py) in enumerate(_other_chips(x, y))]
        return owns, first, passed, from_sib

    def start(ins, outs, sems):
        owns, first, _, _ = env(ins, outs, sems)
        for cp in owns + first:
            cp.start()

    def mid(ins, outs, sems):
        _, _, passed, _ = env(ins, outs, sems)
        for arrival, fwd in passed:
            arrival.wait_recv()
            fwd.start()

    def finish(ins, outs, sems):
        owns, first, passed, from_sib = env(ins, outs, sems)
        for cp in from_sib:
            cp.wait_recv()
        for cp in first + [fwd for _, fwd in passed]:
            cp.wait_send()
        for cp in owns:
            cp.wait()

    shapes = [jax.ShapeDtypeStruct((N_DEV,) + tuple(a.shape), a.dtype) for a in arrs]
    scratch = [pltpu.SemaphoreType.DMA((n * K,)), pltpu.SemaphoreType.DMA((n * K,)), pltpu.SemaphoreType.DMA((n,))]
    return _Exchange(arrs, shapes, scratch, start, finish, mid)


def _swap_sibling(arrs):
    n = len(arrs)

    def copies(ins, outs, sems):
        send_sems, recv_sems = sems
        x, y, c = lax.axis_index("x"), lax.axis_index("y"), lax.axis_index("c")
        return [pltpu.make_async_remote_copy(src_ref=ins[a], dst_ref=outs[a], send_sem=send_sems.at[a],
                                             recv_sem=recv_sems.at[a], device_id=(x, y, 1 - c), device_id_type=MESH)
                for a in range(n)]

    def start(ins, outs, sems):
        for cp in copies(ins, outs, sems):
            cp.start()

    def finish(ins, outs, sems):
        for cp in copies(ins, outs, sems):
            cp.wait()

    shapes = [jax.ShapeDtypeStruct(tuple(a.shape), a.dtype) for a in arrs]
    return _Exchange(arrs, shapes, [pltpu.SemaphoreType.DMA((n,)), pltpu.SemaphoreType.DMA((n,))], start, finish)


def _exchange_chips(arrs):
    n = len(arrs)
    K = N_CHIP - 1

    def copies(ins, outs, sems):
        send_sems, recv_sems, loc_sems = sems
        x, y, c = lax.axis_index("x"), lax.axis_index("y"), lax.axis_index("c")
        mine = 2 * x + y
        out = []
        for a in range(n):
            out += [pltpu.make_async_remote_copy(src_ref=ins[a].at[2 * px + py], dst_ref=outs[a].at[mine],
                                                 send_sem=send_sems.at[a * K + j], recv_sem=recv_sems.at[a * K + j],
                                                 device_id=(px, py, c), device_id_type=MESH)
                    for j, (px, py) in enumerate(_other_chips(x, y))]
            out.append(pltpu.make_async_copy(ins[a].at[mine], outs[a].at[mine], loc_sems.at[a]))
        return out

    def start(ins, outs, sems):
        for cp in copies(ins, outs, sems):
            cp.start()

    def finish(ins, outs, sems):
        for cp in copies(ins, outs, sems):
            cp.wait()

    shapes = [jax.ShapeDtypeStruct(tuple(a.shape), a.dtype) for a in arrs]
    scratch = [pltpu.SemaphoreType.DMA((n * K,)), pltpu.SemaphoreType.DMA((n * K,)), pltpu.SemaphoreType.DMA((n,))]
    return _Exchange(arrs, shapes, scratch, start, finish)


def _add_halves(mine, theirs, name):
    _, R, C = mine.shape
    cap = max(16, ((2 * 1024 * 1024) // (4 * C * 10)) // 16 * 16)
    T = R if R <= cap else _tile(R, cap, 16)

    def body(a_ref, b_ref, o_ref):
        o_ref[...] = (a_ref[...] + b_ref[...].astype(F32)).astype(BF16)

    blk = pl.BlockSpec((N_CHIP, T, C), lambda i: (0, i, 0))
    return pl.pallas_call(
        body, name=name, grid=(R // T,), in_specs=[blk, blk], out_specs=blk,
        out_shape=jax.ShapeDtypeStruct(mine.shape, BF16), compiler_params=_cp(("parallel",)),
    )(mine, theirs)


def _adamw(parts, w, m, v, name):
    R, C = w.shape
    npart = parts.shape[0]
    cap = max(16, ((2 * 1024 * 1024) // (4 * C * 12)) // 16 * 16)
    T = R if R <= cap else _tile(R, cap, 16)

    def body(p_ref, w_ref, m_ref, v_ref, g_ref, d_ref, nm_ref, nv_ref):
        g = p_ref[0].astype(F32)
        for k in range(1, npart):
            g = g + p_ref[k].astype(F32)
        mm = ADAM_B1 * m_ref[...] + (1.0 - ADAM_B1) * g
        vv = ADAM_B2 * v_ref[...] + (1.0 - ADAM_B2) * (g * g)
        m_hat = mm / (1.0 - ADAM_B1 ** ADAM_STEP)
        v_hat = vv / (1.0 - ADAM_B2 ** ADAM_STEP)
        g_ref[...] = g
        d_ref[...] = -ADAM_LR * (m_hat / (jnp.sqrt(v_hat) + ADAM_EPS) + ADAM_WD * w_ref[...])
        nm_ref[...] = mm
        nv_ref[...] = vv

    row = pl.BlockSpec((T, C), lambda i: (i, 0))
    out = jax.ShapeDtypeStruct((R, C), F32)
    return pl.pallas_call(
        body, name=name, grid=(R // T,),
        in_specs=[pl.BlockSpec((npart, T, C), lambda i: (0, i, 0)), row, row, row],
        out_specs=[row] * 4, out_shape=[out] * 4,
        compiler_params=_cp(("parallel",)),
    )(parts, w, m, v)


SMALL = ("attn_pre_norm", "gdn_A_log", "gdn_dt_bias", "gdn_norm_w", "sb_norm_w", "attn_post_norm",
         "ffn_pre_norm", "ffn_conv_b", "ffn_post_norm")


def _pack_small(arrs):
    rows = []
    for a in arrs:
        flat = a.reshape(-1).astype(F32)
        n = -(-flat.shape[0] // 128) * 128
        rows.append(jnp.pad(flat, (0, n - flat.shape[0])).reshape(-1, 128))
    slab = jnp.concatenate(rows, axis=0)
    pad = (-slab.shape[0]) % 8
    return jnp.pad(slab, ((0, pad), (0, 0)))


def _unpack_small(slab, shapes):
    out, r = [], 0
    for shp in shapes:
        size = 1
        for s in shp:
            size *= s
        nr = -(-size // 128)
        out.append(slab[r:r + nr].reshape(-1)[:size].reshape(shp))
        r += nr
    return out


def _to_blocks_cols(a):
    R, C = a.shape
    return a.reshape(R, N_DEV, C // N_DEV).transpose(1, 0, 2)


def _from_blocks_cols(a):
    n, R, c = a.shape
    return a.transpose(1, 0, 2).reshape(R, n * c)


def kernel(x, meta_tokens, attn_pre_norm, w_in, gdn_conv_w, gdn_A_log, gdn_dt_bias, gdn_norm_w, sb_norm_w, w_out, attn_post_norm, ffn_pre_norm, w_ffn_up, ffn_conv_w, ffn_conv_b, w_ffn_down, ffn_post_norm, loss_target, m_meta_tokens, m_attn_pre_norm, m_w_in, m_gdn_conv_w, m_gdn_A_log, m_gdn_dt_bias, m_gdn_norm_w, m_sb_norm_w, m_w_out, m_attn_post_norm, m_ffn_pre_norm, m_w_ffn_up, m_ffn_conv_w, m_ffn_conv_b, m_w_ffn_down, m_ffn_post_norm, v_meta_tokens, v_attn_pre_norm, v_w_in, v_gdn_conv_w, v_gdn_A_log, v_gdn_dt_bias, v_gdn_norm_w, v_sb_norm_w, v_w_out, v_attn_post_norm, v_ffn_pre_norm, v_w_ffn_up, v_ffn_conv_w, v_ffn_conv_b, v_w_ffn_down, v_ffn_post_norm):
    args = dict(locals())
    seq = x.shape[1]
    LP = -(-(ROW0 + seq) // LP_ALIGN) * LP_ALIGN
    tail = LP - ROW0 - seq

    meta_f = _from_blocks_cols(_run_exchange(_gather_two_level([meta_tokens]), "gather_meta")[0])

    h0 = jnp.concatenate([jnp.zeros((PAD_ROWS, D_MODEL), F32), meta_f, x[0], jnp.zeros((tail, D_MODEL), F32)], axis=0)
    target = jnp.concatenate([jnp.zeros((ROW0, D_MODEL), F32), loss_target[0], jnp.zeros((tail, D_MODEL), F32)], axis=0)
    (u,), got = _prenorm_fwd(h0, attn_pre_norm, carry=_gather_two_level([w_in[0].astype(BF16), gdn_conv_w[0]]))
    win_o = _from_blocks_cols(got[0])
    o_ab = C_QKV
    o_z = o_ab + 2 * GDN_HEADS
    w_inp = jnp.concatenate([win_o[:, :C_QKV], win_o[:, o_z:o_z + C_Z], win_o[:, o_z + C_Z:],
                             win_o[:, o_ab:o_z], jnp.zeros((D_MODEL, C_AB - 2 * GDN_HEADS), BF16)], axis=1)
    gconv_f = _from_blocks_cols(got[1])
    proj = _mm(u, w_inp, F32, "mm_in")
    (qn, kn, vg, beta_b, g_b), got = _gdn_pre_fwd(
        proj, gconv_f, gdn_A_log, gdn_dt_bias,
        carry=_gather_two_level([w_out[0].astype(BF16), w_ffn_down[0].astype(BF16)]))
    w_out_f = got[0].reshape(D_MODEL, D_MODEL)
    w_down_f = got[1].reshape(D_FF, D_MODEL)
    (cu, cw, cqd, ckd, cqk, ct, cgl), got = _gdn_chunk_fwd(
        qn, kn, vg, beta_b, g_b, carry=_gather_two_level([w_ffn_up[0].astype(BF16), ffn_conv_w[0]]))
    w_up_f = _from_blocks_cols(got[0])
    fconv_f = _from_blocks_cols(got[1])
    og, ssave = _gdn_scan_fwd(cu, cw, cqd, ckd, cqk, cgl)
    osb, ctot, sb_nrun = _sb_fwd(proj)
    snw = sb_norm_w.reshape(1, SB_HEADS * SB_DH)
    y = _attn_norm_fwd(og, proj, osb, gdn_norm_w, snw)
    mix = _mm(y, w_out_f, F32, "mm_out")
    h1, n2 = _resid_fwd(h0, mix, attn_post_norm, ffn_pre_norm)
    up = _mm(n2, w_up_f, F32, "mm_up")
    act = _convglu_fwd(up, fconv_f, ffn_conv_b)
    f = _mm(act, w_down_f, F32, "mm_down")
    loss_part, dout, df, d_fpost = _final(h1, f, ffn_post_norm, target, seq)
    loss = lax.psum(loss_part[0, 0], ("x", "y", "c"))

    d_wdown = _mm_tn(act, df, "mm_dw_down")
    dact = _mm(df, w_down_f.T, F32, "mm_dact")
    dup, d_fconv, d_fconvb = _convglu_bwd(up, fconv_f, ffn_conv_b, dact)
    d_wup = _mm_tn(n2, dup, "mm_dw_up")
    dn2 = _mm(dup, w_up_f.T, F32, "mm_dn2")
    dh1, dmix, d_fpre, d_apost = _resid_bwd(h1, mix, attn_post_norm, ffn_pre_norm, dout, dn2)
    d_wout = _mm_tn(y, dmix, "mm_dw_out")
    dy = _mm(dmix, w_out_f.T, F32, "mm_dy")
    my_c = lax.axis_index("c")

    def core_halves(blocks):
        halves = [s.reshape((N_CHIP, 2) + s.shape[1:]) for s in blocks]
        return ([lax.dynamic_index_in_dim(h, my_c, axis=1, keepdims=False) for h in halves],
                [lax.dynamic_index_in_dim(h, 1 - my_c, axis=1, keepdims=False).astype(BF16) for h in halves])

    early_names = ("w_out", "w_ffn_up", "w_ffn_down", "ffn_conv_w")
    e_mine, e_send = core_halves([d_wout.reshape(N_DEV, D_MODEL // N_DEV, D_MODEL), _to_blocks_cols(d_wup),
                                  d_wdown.reshape(N_DEV, D_FF // N_DEV, D_MODEL), _to_blocks_cols(d_fconv)])
    (dog, dz, dos, d_gnw, d_snw), e_theirs = _attn_norm_bwd(og, proj, osb, gdn_norm_w, snw, dy,
                                                            carry=_swap_sibling(e_send))
    e_sums = [_add_halves(a, b, "grads_add_" + nm) for nm, a, b in zip(early_names, e_mine, e_theirs)]
    dqs, dks, dvs = _sb_bwd(proj, ctot, sb_nrun, dos)
    (du_, dw_, dqd_, dkd_, dqk_, dgl_), e_recv = _gdn_scan_bwd(cu, cw, cqd, ckd, cqk, cgl, ssave, dog,
                                                               carry=_exchange_chips(e_sums))
    dqn, dkn, dvg, dbeta, dg = _gdn_chunk_bwd(qn, kn, vg, beta_b, g_b, ct, du_, dw_, dqd_, dkd_, dqk_, dgl_)
    dqkv, dab, d_gconv, d_gsc = _gdn_pre_bwd(proj, gconv_f, gdn_A_log, gdn_dt_bias, dqn, dkn, dvg, dbeta, dg)
    dproj = jnp.concatenate([dqkv.astype(BF16), dz.astype(BF16), dqs.astype(BF16), dks.astype(BF16),
                             dvs.astype(BF16), dab.astype(BF16)], axis=1)
    d_winp = _mm_tn(u, dproj, "mm_dw_in")
    du0 = _mm(dproj, w_inp.T, F32, "mm_du")
    d_win = jnp.concatenate([d_winp[:, :C_QKV], d_winp[:, OFF_AB:OFF_AB + 2 * GDN_HEADS],
                             d_winp[:, OFF_Z:OFF_Z + C_Z], d_winp[:, OFF_SB:OFF_SB + C_SB]], axis=1)
    late_names = ("w_in", "gdn_conv_w")
    l_mine, l_send = core_halves([_to_blocks_cols(d_win), _to_blocks_cols(d_gconv)])
    l_theirs = _run_exchange(_swap_sibling(l_send), "grads_swap_sibling")
    l_sums = [_add_halves(a, b, "grads_add_" + nm) for nm, a, b in zip(late_names, l_mine, l_theirs)]
    (dh0, d_apre), l_recv = _prenorm_bwd(h0, attn_pre_norm, du0, dh1, carry=_exchange_chips(l_sums))
    grad_x = dh0[ROW0:ROW0 + seq][None]
    d_meta = dh0[PAD_ROWS:ROW0]

    small_grads = [d_apre, d_gsc[0:1, :GDN_HEADS], d_gsc[1:2, :GDN_HEADS], d_gnw, d_snw.reshape(1, SB_HEADS, SB_DH),
                   d_apost, d_fpre, d_fconvb, d_fpost]
    n_small_rows = _pack_small(small_grads).shape[0]
    slab_parts = _gather_direct([jnp.concatenate([_pack_small(small_grads), d_meta.reshape(-1, LANE)], axis=0)],
                                name="gather_small_grads")[0]
    me = 4 * lax.axis_index("x") + 2 * lax.axis_index("y") + my_c
    meta_parts = lax.dynamic_index_in_dim(
        slab_parts[:, n_small_rows:].reshape(N_DEV, N_META, N_DEV, LANE), me, axis=2, keepdims=False)
    slab_parts = slab_parts[:, :n_small_rows]

    res = {}
    for nm, parts in zip(early_names + late_names + ("meta_tokens",), list(e_recv) + list(l_recv) + [meta_parts]):
        wloc = args[nm]
        shp = wloc.shape
        w2 = wloc.reshape(shp[-2], shp[-1])
        outs = _adamw(parts, w2, args["m_" + nm].reshape(w2.shape), args["v_" + nm].reshape(w2.shape), "adamw_" + nm)
        res[nm] = [o.reshape(shp) for o in outs]
    small_shapes = [args[nm].shape for nm in SMALL]
    outs = _adamw(slab_parts, _pack_small([args[nm] for nm in SMALL]), _pack_small([args["m_" + nm] for nm in SMALL]),
                  _pack_small([args["v_" + nm] for nm in SMALL]), "adamw_small")
    for k in range(4):
        for nm, val in zip(SMALL, _unpack_small(outs[k], small_shapes)):
            res.setdefault(nm, [None] * 4)[k] = val

    order = ("meta_tokens", "attn_pre_norm", "w_in", "gdn_conv_w", "gdn_A_log", "gdn_dt_bias", "gdn_norm_w",
             "sb_norm_w", "w_out", "attn_post_norm", "ffn_pre_norm", "w_ffn_up", "ffn_conv_w", "ffn_conv_b",
             "w_ffn_down", "ffn_post_norm")
    return (loss, grad_x, *[res[nm][0] for nm in order], *[res[nm][1] for nm in order],
            *[res[nm][2] for nm in order], *[res[nm][3] for nm in order])
```

```python
import functools

import jax
import jax.numpy as jnp
from jax import lax
from jax.experimental import pallas as pl
from jax.experimental.pallas import tpu as pltpu

F32 = jnp.float32
BF16 = jnp.bfloat16

D_MODEL = 1024
N_META = 16
GDN_HEADS = 4
GDN_D = 128
GDN_CHUNK = 64
GDN_CONV = 4
GDN_ROWS = 256
SCAN_CHUNKS = 4
SB_HEADS = 8
SB_DH = 64
SB_BLOCK = 128
D_FF = 2816
FFN_CONV = 3
NORM_EPS = 1e-6
L2_EPS = 1e-6
LANE = 128
N_DEV = 8

PAD_ROWS = SB_BLOCK - N_META
ROW0 = SB_BLOCK
SB_SPAN = 512
SB_DEAD = -104.0
SB_SUB = 256
SB_QTILE = 256
LP_ALIGN = 256

C_QKV = 3 * GDN_HEADS * GDN_D
C_Z = GDN_HEADS * GDN_D
C_SB = 3 * SB_HEADS * SB_DH
C_AB = 256
OFF_Z = C_QKV
OFF_SB = OFF_Z + C_Z
OFF_AB = OFF_SB + C_SB
D_INP = OFF_AB + C_AB
D_IN = C_QKV + 2 * GDN_HEADS + C_Z + C_SB

ADAM_LR = 0.001
ADAM_B1 = 0.9
ADAM_B2 = 0.999
ADAM_EPS = 1e-08
ADAM_WD = 0.01
ADAM_STEP = 10

VMEM_LIMIT = 56 * 1024 * 1024
MESH = pl.DeviceIdType.MESH


def _cp(sem=None):
    kw = dict(vmem_limit_bytes=VMEM_LIMIT)
    if sem is not None:
        kw["dimension_semantics"] = sem
    return pltpu.CompilerParams(**kw)


def _tile(n, cap, unit=128):
    best = None
    t = unit
    while t <= min(n, cap):
        if n % t == 0:
            best = t
        t += unit
    assert best is not None, (n, cap, unit)
    return best


def _dot(a, b):
    return jnp.dot(a, b, preferred_element_type=F32)


def _dot_nt(a, b):
    return lax.dot_general(a, b, (((1,), (1,)), ((), ())), preferred_element_type=F32)


def _dot_tn(a, b):
    return lax.dot_general(a, b, (((0,), (0,)), ((), ())), preferred_element_type=F32)


def _split(x):
    hi = x.astype(BF16)
    lo = (x - hi.astype(F32)).astype(BF16)
    return hi, lo


def _dot1(a, b, f=_dot):
    return f(a.astype(BF16), b.astype(BF16))


def _dot3(a, b, f=_dot):
    ah, al = _split(a)
    bh, bl = _split(b)
    return f(ah, bh) + (f(ah, bl) + f(al, bh))


def _dot_exact_l(m_bf16, x, f=_dot):
    xh, xl = _split(x)
    return f(m_bf16, xh) + f(m_bf16, xl)


def _dot_exact_r(x, m_bf16, f=_dot):
    xh, xl = _split(x)
    return f(xh, m_bf16) + f(xl, m_bf16)


def _iota2(shape, dim):
    return lax.broadcasted_iota(jnp.int32, shape, dim)


def _sigmoid(x):
    return 1.0 / (1.0 + jnp.exp(-x))


def _softplus(x):
    return jnp.maximum(x, 0.0) + jnp.log(1.0 + jnp.exp(-jnp.abs(x)))


def _colsum(x):
    return jnp.sum(x, axis=0, keepdims=True)


def _rowsum(x):
    return jnp.sum(x, axis=-1, keepdims=True)


def _mm(a, b, out_dtype, name):
    M, K = a.shape
    K2, N = b.shape
    assert K == K2
    tm = _tile(M, 768)
    tn = _tile(N, max(128, (6 * 1024 * 1024) // (2 * K)))

    def body(a_ref, b_ref, o_ref):
        o_ref[...] = _dot(a_ref[...].astype(BF16), b_ref[...].astype(BF16)).astype(o_ref.dtype)

    return pl.pallas_call(
        body, name=name, grid=(N // tn, M // tm),
        in_specs=[pl.BlockSpec((tm, K), lambda j, i: (i, 0)), pl.BlockSpec((K, tn), lambda j, i: (0, j))],
        out_specs=pl.BlockSpec((tm, tn), lambda j, i: (i, j)),
        out_shape=jax.ShapeDtypeStruct((M, N), out_dtype),
        compiler_params=_cp(("parallel", "parallel")),
    )(a, b)


def _mm_tn(a, b, name):
    M, K = a.shape
    M2, N = b.shape
    assert M == M2
    tm = _tile(M, 1408)
    tk = _tile(K, 1408)
    tn = _tile(N, 1408)

    def body(a_ref, b_ref, o_ref):
        @pl.when(pl.program_id(2) == 0)
        def _():
            o_ref[...] = jnp.zeros_like(o_ref)
        o_ref[...] += _dot_tn(a_ref[...].astype(BF16), b_ref[...].astype(BF16))

    return pl.pallas_call(
        body, name=name, grid=(K // tk, N // tn, M // tm),
        in_specs=[pl.BlockSpec((tm, tk), lambda i, j, m: (m, i)), pl.BlockSpec((tm, tn), lambda i, j, m: (m, j))],
        out_specs=pl.BlockSpec((tk, tn), lambda i, j, m: (i, j)),
        out_shape=jax.ShapeDtypeStruct((K, N), F32),
        compiler_params=_cp(("parallel", "parallel", "arbitrary")),
    )(a, b)


def _rms(x):
    return lax.rsqrt(jnp.mean(x * x, axis=-1, keepdims=True) + NORM_EPS)


def _rms_bwd(x, w, dy):
    r = _rms(x)
    n = x * r
    dyw = dy * w
    dx = r * (dyw - n * jnp.mean(dyw * n, axis=-1, keepdims=True))
    return dx, dy * n


def _prenorm_fwd(h0, w, carry=None):
    LP, D = h0.shape
    T = _tile(LP, 512)

    def body(h_ref, w_ref, u_ref):
        h = h_ref[...]
        u_ref[...] = (h * _rms(h) * w_ref[...]).astype(BF16)

    return _call_carrying(
        carry, body, LP // T, name="prenorm_fwd",
        in_specs=[pl.BlockSpec((T, D), lambda i: (i, 0)), pl.BlockSpec((1, D), lambda i: (0, 0))],
        out_specs=[pl.BlockSpec((T, D), lambda i: (i, 0))],
        out_shape=[jax.ShapeDtypeStruct((LP, D), BF16)],
        operands=(h0, w))


def _prenorm_bwd(h0, w, du, dh1, carry=None):
    LP, D = h0.shape
    T = _tile(LP, 512)

    def body(h_ref, w_ref, du_ref, dh1_ref, dh0_ref, dw_ref):
        @pl.when(pl.program_id(0) == 0)
        def _():
            dw_ref[...] = jnp.zeros_like(dw_ref)
        dx, dwn = _rms_bwd(h_ref[...], w_ref[...], du_ref[...])
        dh0_ref[...] = dh1_ref[...] + dx
        dw_ref[...] += _colsum(dwn)

    row = pl.BlockSpec((T, D), lambda i: (i, 0))
    vec = pl.BlockSpec((1, D), lambda i: (0, 0))
    return _call_carrying(
        carry, body, LP // T, name="prenorm_bwd",
        in_specs=[row, vec, row, row], out_specs=[row, vec],
        out_shape=[jax.ShapeDtypeStruct((LP, D), F32), jax.ShapeDtypeStruct((1, D), F32)],
        operands=(h0, w, du, dh1))


def _causal_taps(ext, w_ref, width, start, rows):
    y = w_ref[width - 1:width, :] * ext[start:start + rows]
    for j in range(width - 1):
        y = y + w_ref[j:j + 1, :] * pltpu.roll(ext, width - 1 - j, 0)[start:start + rows]
    return y


def _shifted_rows(ext, shift, start, rows):
    return ext[start:start + rows] if shift == 0 else pltpu.roll(ext, shift, 0)[start:start + rows]


def _anticausal_taps(dy_ext, w_ref, width, rows):
    n = dy_ext.shape[0]
    dx = w_ref[width - 1:width, :] * dy_ext[0:rows]
    for j in range(width - 1):
        dx = dx + w_ref[j:j + 1, :] * pltpu.roll(dy_ext, n - (width - 1 - j), 0)[0:rows]
    return dx


def _gdn_gate_consts(alog_ref, dtb_ref, h):
    a_coef = -jnp.exp(alog_ref[0:1, h:h + 1])
    return a_coef, dtb_ref[0:1, h:h + 1]


def _gdn_pre_fwd(proj, conv_w, a_log, dt_bias, carry=None):
    LP = proj.shape[0]
    T = _tile(LP, 256)
    C = C_QKV
    H = GDN_HEADS

    def body(x_ref, halo_ref, ab_ref, cw_ref, alog_ref, dtb_ref, q_ref, k_ref, v_ref, beta_ref, g_ref):
        i = pl.program_id(0)
        ext = jnp.concatenate([jnp.where(i > 0, halo_ref[...], 0.0), x_ref[...]], axis=0)
        y = _causal_taps(ext, cw_ref, GDN_CONV, 8, T)
        c = y * _sigmoid(y)
        for h in range(H):
            sl = slice(h * GDN_D, (h + 1) * GDN_D)
            cq = c[:, sl]
            q_ref[:, sl] = cq * lax.rsqrt(_rowsum(cq * cq) + L2_EPS) * (GDN_D ** -0.5)
            ck = c[:, 512 + h * GDN_D:512 + (h + 1) * GDN_D]
            k_ref[:, sl] = ck * lax.rsqrt(_rowsum(ck * ck) + L2_EPS)
        v_ref[...] = c[:, 1024:]
        ab = ab_ref[...]
        valid = (i * T + _iota2((T, 1), 0)) >= PAD_ROWS
        for h in range(H):
            sl = slice(h * GDN_D, (h + 1) * GDN_D)
            a_coef, dtb = _gdn_gate_consts(alog_ref, dtb_ref, h)
            g = jnp.where(valid, a_coef * _softplus(ab[:, h:h + 1] + dtb), 0.0)
            beta = jnp.where(valid, _sigmoid(ab[:, H + h:H + h + 1]), 0.0)
            g_ref[:, sl] = jnp.broadcast_to(g, (T, GDN_D))
            beta_ref[:, sl] = jnp.broadcast_to(beta, (T, GDN_D))

    t8 = T // 8
    row512 = pl.BlockSpec((T, 512), lambda i: (i, 0))
    small = lambda r, c: pl.BlockSpec((r, c), lambda i: (0, 0))
    out = jax.ShapeDtypeStruct((LP, 512), F32)
    return _call_carrying(
        carry, body, LP // T, name="gdn_pre_fwd",
        in_specs=[pl.BlockSpec((T, C), lambda i: (i, 0)),
                  pl.BlockSpec((8, C), lambda i: (jnp.maximum(i * t8 - 1, 0), 0)),
                  pl.BlockSpec((T, C_AB), lambda i: (i, OFF_AB // C_AB)),
                  small(GDN_CONV, C), small(1, H), small(1, H)],
        out_specs=[row512] * 5, out_shape=[out] * 5,
        operands=(proj, proj, proj, conv_w, a_log, dt_bias))


def _gdn_pre_bwd(proj, conv_w, a_log, dt_bias, dq, dk, dv, dbeta, dg, carry=None):
    LP = proj.shape[0]
    T = _tile(LP, 256)
    C = C_QKV
    H = GDN_HEADS
    TE = T + 8
    nt = LP // T

    def body(x_ref, xp_ref, xn_ref, ab_ref, cw_ref, alog_ref, dtb_ref,
             dq_ref, dqn_ref, dk_ref, dkn_ref, dv_ref, dvn_ref, dbeta_ref, dg_ref,
             dx_ref, dab_ref, dcw_ref, dsc_ref, dys):
        i = pl.program_id(0)

        @pl.when(i == 0)
        def _():
            dcw_ref[...] = jnp.zeros_like(dcw_ref)
            dsc_ref[...] = jnp.zeros_like(dsc_ref)

        last = i == nt - 1
        ext = jnp.concatenate([jnp.where(i > 0, xp_ref[...], 0.0), x_ref[...], jnp.where(last, 0.0, xn_ref[...])],
                              axis=0)
        y = _causal_taps(ext, cw_ref, GDN_CONV, 8, TE)
        sg = _sigmoid(y)
        c = y * sg
        nxt = lambda a_ref, b_ref: jnp.concatenate([a_ref[...], jnp.where(last, 0.0, b_ref[...])], axis=0)
        dqn = nxt(dq_ref, dqn_ref)
        dkn = nxt(dk_ref, dkn_ref)
        dvv = nxt(dv_ref, dvn_ref)
        for h in range(H):
            sl = slice(h * GDN_D, (h + 1) * GDN_D)
            cq = c[:, sl]
            rq = lax.rsqrt(_rowsum(cq * cq) + L2_EPS)
            nq = cq * rq
            dqh = dqn[:, sl]
            dys[:, sl] = (GDN_D ** -0.5) * rq * (dqh - nq * _rowsum(dqh * nq))
            sk = slice(512 + h * GDN_D, 512 + (h + 1) * GDN_D)
            ck = c[:, sk]
            rk = lax.rsqrt(_rowsum(ck * ck) + L2_EPS)
            nk = ck * rk
            dkh = dkn[:, sl]
            dys[:, sk] = rk * (dkh - nk * _rowsum(dkh * nk))
        dys[:, 1024:] = dvv
        dy = dys[...] * (sg * (1.0 + y * (1.0 - sg)))
        for j in range(GDN_CONV):
            dcw_ref[j:j + 1, :] += _colsum(dy[0:T, :] * _shifted_rows(ext, GDN_CONV - 1 - j, 8, T))
        dx_ref[...] = _anticausal_taps(dy, cw_ref, GDN_CONV, T).astype(BF16)
        ab = ab_ref[...]
        valid = (i * T + _iota2((T, 1), 0)) >= PAD_ROWS
        lane = _iota2((T, C_AB), 1)
        lane1 = _iota2((1, 128), 1)
        dab = jnp.zeros((T, C_AB), F32)
        dsc_a = jnp.zeros((1, 128), F32)
        dsc_d = jnp.zeros((1, 128), F32)
        for h in range(H):
            a_coef, dtb = _gdn_gate_consts(alog_ref, dtb_ref, h)
            pre = ab[:, h:h + 1] + dtb
            dgh = jnp.where(valid, dg_ref[:, h * GDN_D:h * GDN_D + 1], 0.0)
            da = dgh * a_coef * _sigmoid(pre)
            beta = _sigmoid(ab[:, H + h:H + h + 1])
            db = jnp.where(valid, dbeta_ref[:, h * GDN_D:h * GDN_D + 1], 0.0) * beta * (1.0 - beta)
            dab = dab + jnp.where(lane == h, da, 0.0) + jnp.where(lane == H + h, db, 0.0)
            dsc_a = dsc_a + jnp.where(lane1 == h, _colsum(dgh * a_coef * _softplus(pre)), 0.0)
            dsc_d = dsc_d + jnp.where(lane1 == h, _colsum(da), 0.0)
        dab_ref[...] = dab.astype(BF16)
        dsc_ref[0:1, :] += dsc_a
        dsc_ref[1:2, :] += dsc_d

    t8 = T // 8
    nb8 = LP // 8
    prev8 = lambda w: pl.BlockSpec((8, w), lambda i: (jnp.maximum(i * t8 - 1, 0), 0))
    next8 = lambda w: pl.BlockSpec((8, w), lambda i: (jnp.minimum((i + 1) * t8, nb8 - 1), 0))
    row = lambda w: pl.BlockSpec((T, w), lambda i: (i, 0))
    small = lambda r, c: pl.BlockSpec((r, c), lambda i: (0, 0))
    return _call_carrying(
        carry, body, nt, name="gdn_pre_bwd",
        in_specs=[row(C), prev8(C), next8(C), pl.BlockSpec((T, C_AB), lambda i: (i, OFF_AB // C_AB)),
                  small(GDN_CONV, C), small(1, H), small(1, H),
                  row(512), next8(512), row(512), next8(512), row(512), next8(512), row(512), row(512)],
        out_specs=[row(C), row(C_AB), small(GDN_CONV, C), small(2, 128)],
        out_shape=[jax.ShapeDtypeStruct((LP, C), BF16), jax.ShapeDtypeStruct((LP, C_AB), BF16),
                   jax.ShapeDtypeStruct((GDN_CONV, C), F32), jax.ShapeDtypeStruct((2, 128), F32)],
        scratch_shapes=[pltpu.VMEM((TE, C), F32)],
        operands=(proj, proj, proj, proj, conv_w, a_log, dt_bias, dq, dq, dk, dk, dv, dv, dbeta, dg))


def _tri_masks():
    r = _iota2((GDN_CHUNK, GDN_CHUNK), 0)
    c = _iota2((GDN_CHUNK, GDN_CHUNK), 1)
    return r >= c, r > c


def _gdn_chunk_common(q, k, v, beta, gb):
    incl, strict = _tri_masks()
    l_incl = incl.astype(BF16)
    gd = _dot_exact_l(l_incl, jnp.where(strict, gb[:, :GDN_CHUNK], 0.0))
    gc = _dot_exact_l(l_incl, gb)
    decay = jnp.where(incl, jnp.exp(jnp.where(incl, gd, 0.0)), 0.0)
    exp_g = jnp.exp(gc)
    g_last = gc[GDN_CHUNK - 1:GDN_CHUNK, :]
    kd_fac = jnp.exp(g_last - gc)
    gl = jnp.exp(g_last)
    kb = k * beta
    kk = _dot1(kb, k, _dot_nt)
    return dict(incl=incl, strict=strict, decay=decay, exp_g=exp_g, kd_fac=kd_fac, gl=gl, kb=kb, kk=kk,
                vb=v * beta, kbg=kb * exp_g)


def _interleave(gens):
    gens = list(gens)
    while gens:
        alive = []
        for g in gens:
            try:
                next(g)
                alive.append(g)
            except StopIteration:
                pass
        gens = alive


def _call_carrying(ex, body, nsteps, *, name, in_specs, out_specs, out_shape, operands, scratch_shapes=()):
    n_in, n_out, n_scr = len(in_specs), len(out_specs), len(scratch_shapes)
    n = ex.n if ex is not None else 0

    def full(*refs):
        o0 = n_in + n
        s0 = o0 + n_out + n
        ex_refs = (refs[n_in:o0], refs[o0 + n_out:s0], refs[s0 + n_scr:])
        step = pl.program_id(0)
        _carry_begin(ex, ex_refs, step, nsteps)
        body(*refs[:n_in], *refs[o0:o0 + n_out], *refs[s0:s0 + n_scr])
        _carry_end(ex, ex_refs, step, nsteps)

    res = pl.pallas_call(
        full, name=name, grid=(nsteps,),
        in_specs=list(in_specs) + [ANY_SPEC] * n, out_specs=list(out_specs) + [ANY_SPEC] * n,
        out_shape=list(out_shape) + (ex.out_shapes if ex is not None else []),
        scratch_shapes=list(scratch_shapes) + (ex.scratch if ex is not None else []),
        compiler_params=pltpu.CompilerParams(dimension_semantics=("arbitrary",), vmem_limit_bytes=VMEM_LIMIT,
                                             has_side_effects=ex is not None),
    )(*operands, *(ex.arrs if ex is not None else []))
    return list(res[:n_out]), list(res[n_out:])


def _gdn_chunk_fwd(qn, kn, v, beta_b, g_b, carry=None):
    LP = qn.shape[0]
    R = GDN_ROWS
    H = GDN_HEADS
    CH = GDN_CHUNK

    def body(q_ref, k_ref, v_ref, b_ref, g_ref, u_ref, w_ref, qd_ref, kd_ref, qk_ref, t_ref, gl_ref):
        def item(cc, h):
            rs = slice(cc * CH, (cc + 1) * CH)
            sl = slice(h * GDN_D, (h + 1) * GDN_D)
            s64 = slice(h * CH, (h + 1) * CH)
            q, k = q_ref[rs, sl], k_ref[rs, sl]
            m = _gdn_chunk_common(q, k, v_ref[rs, sl], b_ref[rs, sl], g_ref[rs, sl])
            qk_raw = _dot1(q, k, _dot_nt)
            yield
            a = jnp.where(m["strict"], m["kk"] * m["decay"], 0.0)
            eye = (_iota2((CH, CH), 0) == _iota2((CH, CH), 1)).astype(F32)
            t = eye - a
            p = _dot3(a, a)
            yield
            for _ in range(4):
                t = t + _dot3(t, p)
                p = _dot3(p, p)
                yield
            t = t + _dot3(t, p)
            yield
            u_ref[rs, sl] = _dot1(t, m["vb"])
            w_ref[rs, sl] = _dot1(t, m["kbg"])
            qk_ref[rs, s64] = qk_raw * m["decay"]
            t_ref[rs, s64] = t
            qd_ref[rs, sl] = q * m["exp_g"]
            kd_ref[rs, sl] = k * m["kd_fac"]
            gl_ref[cc * 8:(cc + 1) * 8, sl] = jnp.broadcast_to(m["gl"], (8, GDN_D))

        _interleave(item(cc, h) for cc in range(R // CH) for h in range(H))

    row = lambda w: pl.BlockSpec((R, w), lambda i: (i, 0))
    o512 = jax.ShapeDtypeStruct((LP, 512), F32)
    o256 = jax.ShapeDtypeStruct((LP, 256), F32)
    return _call_carrying(
        carry, body, LP // R, name="gdn_chunk_fwd",
        in_specs=[row(512)] * 5,
        out_specs=[row(512)] * 4 + [row(256)] * 2 + [pl.BlockSpec((R // 8, 512), lambda i: (i, 0))],
        out_shape=[o512] * 4 + [o256] * 2 + [jax.ShapeDtypeStruct((LP // 8, 512), F32)],
        operands=(qn, kn, v, beta_b, g_b))


def _gdn_chunk_bwd(qn, kn, v, beta_b, g_b, t_all, du, dw, dqd, dkd, dqk, dgl):
    LP = qn.shape[0]
    R = GDN_ROWS
    H = GDN_HEADS
    CH = GDN_CHUNK

    def body(q_ref, k_ref, v_ref, b_ref, g_ref, t_ref, du_ref, dw_ref, dqd_ref, dkd_ref, dqk_ref, dgl_ref,
             dq_ref, dk_ref, dv_ref, db_ref, dg_ref):
        ones = jnp.ones((CH, GDN_D), BF16)

        def item(cc, h):
            rs = slice(cc * CH, (cc + 1) * CH)
            sl = slice(h * GDN_D, (h + 1) * GDN_D)
            s64 = slice(h * CH, (h + 1) * CH)
            q, k, vv, beta = q_ref[rs, sl], k_ref[rs, sl], v_ref[rs, sl], b_ref[rs, sl]
            m = _gdn_chunk_common(q, k, vv, beta, g_ref[rs, sl])
            incl, strict, decay = m["incl"], m["strict"], m["decay"]
            t = t_ref[rs, s64]
            du_, dw_ = du_ref[rs, sl], dw_ref[rs, sl]
            dqd_, dkd_ = dqd_ref[rs, sl], dkd_ref[rs, sl]
            d_t = _dot1(du_, m["vb"], _dot_nt) + _dot1(dw_, m["kbg"], _dot_nt)
            dvb = _dot1(t, du_, _dot_tn)
            dkbg = _dot1(t, dw_, _dot_tn)
            qk_raw = _dot1(q, k, _dot_nt)
            yield
            x1 = _dot3(d_t, t, _dot_nt)
            dkb = dkbg * m["exp_g"]
            d_gi = _rowsum(dkbg * m["kbg"])
            yield
            d_a = jnp.where(strict, -_dot3(t, x1, _dot_tn), 0.0)
            yield
            d_kk = d_a * decay
            dqk_m = jnp.where(incl, dqk_ref[rs, s64], 0.0)
            dqk_raw = dqk_m * decay
            mm = (d_a * m["kk"] + dqk_m * qk_raw) * decay
            dkb = dkb + _dot1(d_kk, k)
            dk_ = _dot1(d_kk, m["kb"], _dot_tn) + _dot1(dqk_raw, q, _dot_tn)
            dq_ = _dot1(dqk_raw, k) + dqd_ * m["exp_g"]
            d_gi = d_gi + (_dot_exact_r(mm, ones) - _dot_exact_r(mm, ones, _dot_tn))
            yield
            d_gi = d_gi + _rowsum(dqd_ * q * m["exp_g"])
            e = _rowsum(dkd_ * k * m["kd_fac"])
            d_gi = d_gi - e
            d_glast = _colsum(jnp.broadcast_to(e, (CH, GDN_D))) + dgl_ref[cc * 8:cc * 8 + 1, sl] * m["gl"]
            dk_ = dk_ + dkd_ * m["kd_fac"] + dkb * beta
            d_gi = d_gi + jnp.where(_iota2((CH, GDN_D), 0) == CH - 1, d_glast, 0.0)
            u_incl = (_iota2((CH, CH), 1) >= _iota2((CH, CH), 0)).astype(BF16)
            dq_ref[rs, sl] = dq_
            dk_ref[rs, sl] = dk_
            dv_ref[rs, sl] = dvb * beta
            db_ref[rs, sl] = jnp.broadcast_to(_rowsum(dvb * vv) + _rowsum(dkb * k), (CH, GDN_D))
            dg_ref[rs, sl] = _dot_exact_l(u_incl, d_gi)

        _interleave(item(cc, h) for cc in range(R // CH) for h in range(H))

    row = lambda w: pl.BlockSpec((R, w), lambda i: (i, 0))
    o512 = jax.ShapeDtypeStruct((LP, 512), F32)
    gl_spec = pl.BlockSpec((R // 8, 512), lambda i: (i, 0))
    return pl.pallas_call(
        body, name="gdn_chunk_bwd", grid=(LP // R,),
        in_specs=[row(512)] * 5 + [row(256)] + [row(512)] * 4 + [row(256), gl_spec],
        out_specs=[row(512)] * 5, out_shape=[o512] * 5,
        compiler_params=_cp(("parallel",)),
    )(qn, kn, v, beta_b, g_b, t_all, du, dw, dqd, dkd, dqk, dgl)


def _gdn_scan_fwd(u, w, qd, kd, qk, gl):
    LP = u.shape[0]
    CH = GDN_CHUNK
    CPS = SCAN_CHUNKS
    N = LP // CH
    NS = N // CPS
    H = GDN_HEADS

    def body(u_ref, w_ref, qd_ref, kd_ref, qk_ref, gl_ref, o_ref, ssave_ref, s_sc):
        @pl.when(pl.program_id(0) == 0)
        def _():
            s_sc[...] = jnp.zeros_like(s_sc)

        for cc in range(CPS):
            rs = slice(cc * CH, (cc + 1) * CH)
            ssave_ref[cc * GDN_D:(cc + 1) * GDN_D, :] = s_sc[...]

            def item(h):
                sl = slice(h * GDN_D, (h + 1) * GDN_D)
                s = s_sc[:, sl]
                v_new = u_ref[rs, sl] - _dot1(w_ref[rs, sl], s)
                o_s = _dot1(qd_ref[rs, sl], s)
                yield
                o_ref[rs, sl] = o_s + _dot1(qk_ref[rs, h * CH:(h + 1) * CH], v_new)
                s_sc[:, sl] = s * gl_ref[cc * 8:cc * 8 + 1, sl] + _dot1(kd_ref[rs, sl], v_new, _dot_tn)

            _interleave(item(h) for h in range(H))

    row = lambda w_: pl.BlockSpec((CPS * CH, w_), lambda n: (n, 0))
    return pl.pallas_call(
        body, name="gdn_scan_fwd", grid=(NS,),
        in_specs=[row(512)] * 4 + [row(256), pl.BlockSpec((CPS * 8, 512), lambda n: (n, 0))],
        out_specs=[row(512), pl.BlockSpec((CPS * GDN_D, 512), lambda n: (n, 0))],
        out_shape=[jax.ShapeDtypeStruct((LP, 512), F32), jax.ShapeDtypeStruct((N * GDN_D, 512), F32)],
        scratch_shapes=[pltpu.VMEM((GDN_D, 512), F32)],
        compiler_params=_cp(("arbitrary",)),
    )(u, w, qd, kd, qk, gl)


def _gdn_scan_bwd(u, w, qd, kd, qk, gl, ssave, do, carry=None):
    LP = u.shape[0]
    CH = GDN_CHUNK
    CPS = SCAN_CHUNKS
    N = LP // CH
    NS = N // CPS
    H = GDN_HEADS

    def body(u_ref, w_ref, qd_ref, kd_ref, qk_ref, gl_ref, s_ref, do_ref,
             du_ref, dw_ref, dqd_ref, dkd_ref, dqk_ref, dgl_ref, ds_sc):
        @pl.when(pl.program_id(0) == 0)
        def _():
            ds_sc[...] = jnp.zeros_like(ds_sc)

        for cc in reversed(range(CPS)):
            rs = slice(cc * CH, (cc + 1) * CH)
            r8 = slice(cc * 8, (cc + 1) * 8)

            def item(h):
                sl = slice(h * GDN_D, (h + 1) * GDN_D)
                s64 = slice(h * CH, (h + 1) * CH)
                s = s_ref[cc * GDN_D:(cc + 1) * GDN_D, sl]
                ds = ds_sc[:, sl]
                do_ = do_ref[rs, sl]
                w_, qd_, kd_, qk_ = w_ref[rs, sl], qd_ref[rs, sl], kd_ref[rs, sl], qk_ref[rs, s64]
                v_new = u_ref[rs, sl] - _dot1(w_, s)
                d_vnew = _dot1(qk_, do_, _dot_tn) + _dot1(kd_, ds)
                dqd_ref[rs, sl] = _dot1(do_, s, _dot_nt)
                ds_new = ds * gl_ref[cc * 8:cc * 8 + 1, sl] + _dot1(qd_, do_, _dot_tn)
                dgl_ref[r8, sl] = jnp.broadcast_to(jnp.sum(_colsum(ds * s), axis=-1, keepdims=True), (8, GDN_D))
                yield
                du_ref[rs, sl] = d_vnew
                dw_ref[rs, sl] = -_dot1(d_vnew, s, _dot_nt)
                dkd_ref[rs, sl] = _dot1(v_new, ds, _dot_nt)
                dqk_ref[rs, s64] = _dot1(do_, v_new, _dot_nt)
                ds_sc[:, sl] = ds_new - _dot1(w_, d_vnew, _dot_tn)

            _interleave(item(h) for h in range(H))

    rev = lambda w_: pl.BlockSpec((CPS * CH, w_), lambda n: (NS - 1 - n, 0))
    rev8 = pl.BlockSpec((CPS * 8, 512), lambda n: (NS - 1 - n, 0))
    o512 = jax.ShapeDtypeStruct((LP, 512), F32)
    return _call_carrying(
        carry, body, NS, name="gdn_scan_bwd",
        in_specs=[rev(512)] * 4 + [rev(256), rev8, pl.BlockSpec((CPS * GDN_D, 512), lambda n: (NS - 1 - n, 0)),
                  rev(512)],
        out_specs=[rev(512)] * 4 + [rev(256), rev8],
        out_shape=[o512] * 4 + [jax.ShapeDtypeStruct((LP, 256), F32), jax.ShapeDtypeStruct((LP // 8, 512), F32)],
        scratch_shapes=[pltpu.VMEM((GDN_D, 512), F32)],
        operands=(u, w, qd, kd, qk, gl, ssave, do))


def _sb_scores(qh, kblk, mask):
    z = _dot_nt(qh, kblk)
    e = jnp.exp(-jnp.abs(z))
    sp = jnp.maximum(z, 0.0) + jnp.log(1.0 + e)
    return z, e, jnp.where(mask, -sp, 0.0), z - sp


def _sb_fwd(proj):
    LP = proj.shape[0]
    B = SB_BLOCK
    W = min(SB_SPAN, LP)
    SUB = SB_SUB
    Q = min(SB_QTILE, LP)
    nq = LP // Q
    nsub = W // SUB
    scale = SB_DH ** -0.5
    qcol, kcol, vcol = OFF_SB // B, (OFF_SB + 512) // B, (OFF_SB + 1024) // B

    def body(q_ref, k_ref, v_ref, o_ref, c_ref, n_ref):
        i = pl.program_id(1)
        lane = _iota2((Q, B), 1)
        head_a = lane < SB_DH
        qs = q_ref[...] * scale
        qh = [jnp.where(head_a, qs, 0.0).astype(BF16), jnp.where(head_a, 0.0, qs).astype(BF16)]
        u_strict = (_iota2((SUB, SUB), 0) > _iota2((SUB, SUB), 1)).astype(BF16)
        qpos = i * Q + _iota2((Q, W), 0)
        hi0 = (i + 1) * Q
        nspan = (hi0 + W - 1) // W

        def live(st):
            return (st[0] < nspan) & (st[1] > 0)

        def span(st):
            r, carry = st[0], st[2:]
            hi = hi0 - r * W
            k0 = pl.multiple_of(jnp.maximum(hi - W, 0), B)
            kblk = k_ref[pl.ds(k0, W), :].astype(BF16)
            vblk = v_ref[pl.ds(k0, W), :].astype(BF16)
            kpos = k0 + _iota2((Q, W), 1)
            mask = (kpos < qpos) & (kpos >= PAD_ROWS) & (kpos < hi)
            new = [None] * 4

            def head(h):
                o_acc, c = carry[2 * h], carry[2 * h + 1]
                z, e, l1m, lsg = _sb_scores(qh[h], kblk, mask)
                yield
                subs = [slice(b * SUB, (b + 1) * SUB) for b in range(nsub)]
                suf = [_dot(l1m[:, bs].astype(BF16), u_strict) for bs in subs]
                yield
                parts = [None] * nsub
                for b in reversed(range(nsub)):
                    parts[b] = jnp.where(mask[:, subs[b]], jnp.exp(lsg[:, subs[b]] + suf[b] + c), 0.0)
                    c = c + _rowsum(l1m[:, subs[b]])
                att = jnp.concatenate(parts, axis=1).astype(BF16)
                new[2 * h], new[2 * h + 1] = o_acc + _dot(att, vblk), c

            _interleave(head(h) for h in range(2))
            more = (jnp.maximum(jnp.max(new[1]), jnp.max(new[3])) > SB_DEAD).astype(jnp.int32)
            return (r + 1, more, *new)

        zero_o = jnp.zeros((Q, B), F32)
        zero_c = jnp.zeros((Q, 1), F32)
        nrun, _, o_a, c_a, o_b, c_b = lax.while_loop(
            live, span, (jnp.int32(0), jnp.int32(1), zero_o, zero_c, zero_o, zero_c))
        o_ref[...] = jnp.where(head_a, o_a, o_b)
        c_ref[...] = jnp.where(head_a, c_a, c_b)
        n_ref[pl.program_id(0), i] = nrun

    blk = pl.BlockSpec((Q, B), lambda p, i: (i, p))
    out = jax.ShapeDtypeStruct((LP, 512), F32)
    return pl.pallas_call(
        body, name="sb_fwd", grid=(SB_HEADS // 2, nq),
        in_specs=[pl.BlockSpec((Q, B), lambda p, i: (i, qcol + p)),
                  pl.BlockSpec((LP, B), lambda p, i: (0, kcol + p)),
                  pl.BlockSpec((LP, B), lambda p, i: (0, vcol + p))],
        out_specs=[blk, blk, pl.BlockSpec(memory_space=pltpu.SMEM)],
        out_shape=[out, out, jax.ShapeDtypeStruct((SB_HEADS // 2, nq), jnp.int32)],
        compiler_params=_cp(("arbitrary", "arbitrary")),
    )(proj, proj, proj)


def _sb_bwd(proj, ctot, nrun_all, do):
    LP = proj.shape[0]
    B = SB_BLOCK
    W = min(SB_SPAN, LP)
    SUB = SB_SUB
    Q = min(SB_QTILE, LP)
    nq = LP // Q
    nsub = W // SUB
    scale = SB_DH ** -0.5
    qcol, kcol, vcol = OFF_SB // B, (OFF_SB + 512) // B, (OFF_SB + 1024) // B

    def body(n_ref, q_ref, k_ref, v_ref, c_ref, do_ref, dq_ref, dk_ref, dv_ref):
        i = pl.program_id(1)

        @pl.when(i == 0)
        def _():
            dk_ref[...] = jnp.zeros_like(dk_ref)
            dv_ref[...] = jnp.zeros_like(dv_ref)

        lane = _iota2((Q, B), 1)
        head_a = lane < SB_DH
        qs = q_ref[...] * scale
        qh = [jnp.where(head_a, qs, 0.0).astype(BF16), jnp.where(head_a, 0.0, qs).astype(BF16)]
        dof = do_ref[...]
        doh = [jnp.where(head_a, dof, 0.0).astype(BF16), jnp.where(head_a, 0.0, dof).astype(BF16)]
        cfull = c_ref[...]
        ctot_h = [cfull[:, 0:1], cfull[:, SB_DH:SB_DH + 1]]
        sub_r, sub_c = _iota2((SUB, SUB), 0), _iota2((SUB, SUB), 1)
        u_strict = (sub_r > sub_c).astype(BF16)
        l_strict = (sub_r < sub_c).astype(BF16)
        qpos = i * Q + _iota2((Q, W), 0)
        hi0 = (i + 1) * Q
        nrun = n_ref[pl.program_id(0), i]

        def span(t, carry):
            r = nrun - 1 - t
            hi = hi0 - r * W
            k0 = pl.multiple_of(jnp.maximum(hi - W, 0), B)
            kblk = k_ref[pl.ds(k0, W), :].astype(BF16)
            vblk = v_ref[pl.ds(k0, W), :].astype(BF16)
            kpos = k0 + _iota2((Q, W), 1)
            mask = (kpos < qpos) & (kpos >= PAD_ROWS) & (kpos < hi)
            new = [None] * 6
            dk_add, dv_add = [None, None], [None, None]
            subs = [slice(b * SUB, (b + 1) * SUB) for b in range(nsub)]

            def head(h):
                dq_acc, pre, ecar = carry[3 * h], carry[3 * h + 1], carry[3 * h + 2]
                z, e, l1m, lsg = _sb_scores(qh[h], kblk, mask)
                d_att = _dot_nt(doh[h], vblk)
                yield
                sig = jnp.where(z >= 0.0, 1.0, e) / (1.0 + e)
                suf = [_dot(l1m[:, bs].astype(BF16), u_strict) for bs in subs]
                yield
                att_parts, p_parts = [None] * nsub, [None] * nsub
                for b, bs in enumerate(subs):
                    pre = pre + _rowsum(l1m[:, bs])
                    att_parts[b] = jnp.where(mask[:, bs], jnp.exp(lsg[:, bs] + suf[b] + (ctot_h[h] - pre)), 0.0)
                    p_parts[b] = att_parts[b] * d_att[:, bs]
                pcum = [_dot(p.astype(BF16), l_strict) for p in p_parts]
                yield
                dz_parts = [None] * nsub
                for b, bs in enumerate(subs):
                    sg = sig[:, bs]
                    dz_parts[b] = jnp.where(mask[:, bs], p_parts[b] * (1.0 - sg) - sg * (ecar + pcum[b]), 0.0)
                    ecar = ecar + _rowsum(p_parts[b])
                att = jnp.concatenate(att_parts, axis=1).astype(BF16)
                dz = jnp.concatenate(dz_parts, axis=1).astype(BF16)
                new[3 * h:3 * h + 3] = [dq_acc + _dot(dz, kblk), pre, ecar]
                dk_add[h] = _dot_tn(dz, qh[h])
                dv_add[h] = _dot_tn(att, doh[h])

            _interleave(head(h) for h in range(2))
            dk_ref[pl.ds(k0, W), :] += dk_add[0] + dk_add[1]
            dv_ref[pl.ds(k0, W), :] += dv_add[0] + dv_add[1]
            return tuple(new)

        zero_o = jnp.zeros((Q, B), F32)
        zero_c = jnp.zeros((Q, 1), F32)
        res = lax.fori_loop(0, nrun, span, (zero_o, zero_c, zero_c, zero_o, zero_c, zero_c))
        dq_ref[...] = (jnp.where(head_a, res[0], res[3]) * scale).astype(BF16)

    blk = pl.BlockSpec((Q, B), lambda p, i: (i, p))
    col = pl.BlockSpec((LP, B), lambda p, i: (0, p))
    out = jax.ShapeDtypeStruct((LP, 512), F32)
    return pl.pallas_call(
        body, name="sb_bwd", grid=(SB_HEADS // 2, nq),
        in_specs=[pl.BlockSpec(memory_space=pltpu.SMEM),
                  pl.BlockSpec((Q, B), lambda p, i: (i, qcol + p)),
                  pl.BlockSpec((LP, B), lambda p, i: (0, kcol + p)),
                  pl.BlockSpec((LP, B), lambda p, i: (0, vcol + p)),
                  blk, blk],
        out_specs=[blk, col, col], out_shape=[jax.ShapeDtypeStruct((LP, 512), BF16), out, out],
        compiler_params=_cp(("arbitrary", "arbitrary")),
    )(nrun_all, proj, proj, proj, ctot, do)


def _sb_group_mean():
    r = jnp.right_shift(_iota2((512, 512), 0), 6)
    c = jnp.right_shift(_iota2((512, 512), 1), 6)
    return jnp.where(r == c, 1.0 / SB_DH, 0.0).astype(BF16)


def _attn_norm_fwd(og, proj, osb, gnw, snw):
    LP = og.shape[0]
    T = _tile(LP, 256)

    def body(og_ref, z_ref, os_ref, gnw_ref, snw_ref, y_ref):
        valid = (pl.program_id(0) * T + _iota2((T, 1), 0)) >= PAD_ROWS
        z = z_ref[...]
        zg = z * _sigmoid(z)
        for h in range(GDN_HEADS):
            sl = slice(h * GDN_D, (h + 1) * GDN_D)
            o = og_ref[:, sl]
            y = o * _rms(o) * gnw_ref[...] * zg[:, sl]
            y_ref[:, sl] = jnp.where(valid, y, 0.0).astype(BF16)
        o = os_ref[...]
        msq = _dot_exact_r(o * o, _sb_group_mean())
        y = o * lax.rsqrt(msq + NORM_EPS) * snw_ref[...]
        y_ref[:, 512:] = jnp.where(valid, y, 0.0).astype(BF16)

    row = pl.BlockSpec((T, 512), lambda i: (i, 0))
    return pl.pallas_call(
        body, name="attn_norm_fwd", grid=(LP // T,),
        in_specs=[row, pl.BlockSpec((T, 512), lambda i: (i, OFF_Z // 512)), row,
                  pl.BlockSpec((1, GDN_D), lambda i: (0, 0)), pl.BlockSpec((1, 512), lambda i: (0, 0))],
        out_specs=pl.BlockSpec((T, 1024), lambda i: (i, 0)),
        out_shape=jax.ShapeDtypeStruct((LP, 1024), BF16),
        compiler_params=_cp(("parallel",)),
    )(og, proj, osb, gnw, snw)


def _attn_norm_bwd(og, proj, osb, gnw, snw, dy, carry=None):
    LP = og.shape[0]
    T = _tile(LP, 256)

    def body(og_ref, z_ref, os_ref, gnw_ref, snw_ref, dy_ref, dog_ref, dz_ref, dos_ref, dgw_ref, dsw_ref):
        @pl.when(pl.program_id(0) == 0)
        def _():
            dgw_ref[...] = jnp.zeros_like(dgw_ref)
            dsw_ref[...] = jnp.zeros_like(dsw_ref)
        valid = (pl.program_id(0) * T + _iota2((T, 1), 0)) >= PAD_ROWS
        dy = jnp.where(valid, dy_ref[...], 0.0)
        z = z_ref[...]
        sg = _sigmoid(z)
        zg = z * sg
        dgw = jnp.zeros((1, GDN_D), F32)
        for h in range(GDN_HEADS):
            sl = slice(h * GDN_D, (h + 1) * GDN_D)
            o = og_ref[:, sl]
            dyh = dy[:, sl]
            dx, dwn = _rms_bwd(o, gnw_ref[...], dyh * zg[:, sl])
            dog_ref[:, sl] = dx
            dgw = dgw + _colsum(dwn)
            yn = o * _rms(o) * gnw_ref[...]
            dz_ref[:, sl] = (dyh * yn * (sg[:, sl] * (1.0 + z[:, sl] * (1.0 - sg[:, sl])))).astype(BF16)
        dgw_ref[...] += dgw
        o = os_ref[...]
        gm = _sb_group_mean()
        r = lax.rsqrt(_dot_exact_r(o * o, gm) + NORM_EPS)
        n = o * r
        dys = dy[:, 512:]
        dyw = dys * snw_ref[...]
        dos_ref[...] = r * (dyw - n * _dot_exact_r(dyw * n, gm))
        dsw_ref[...] += _colsum(dys * n)

    row = pl.BlockSpec((T, 512), lambda i: (i, 0))
    gw = pl.BlockSpec((1, GDN_D), lambda i: (0, 0))
    sw = pl.BlockSpec((1, 512), lambda i: (0, 0))
    o512 = jax.ShapeDtypeStruct((LP, 512), F32)
    return _call_carrying(
        carry, body, LP // T, name="attn_norm_bwd",
        in_specs=[row, pl.BlockSpec((T, 512), lambda i: (i, OFF_Z // 512)), row, gw, sw,
                  pl.BlockSpec((T, 1024), lambda i: (i, 0))],
        out_specs=[row, row, row, gw, sw],
        out_shape=[o512, jax.ShapeDtypeStruct((LP, 512), BF16), o512, jax.ShapeDtypeStruct((1, GDN_D), F32),
                   jax.ShapeDtypeStruct((1, 512), F32)],
        operands=(og, proj, osb, gnw, snw, dy))


def _resid_fwd(h0, mix, w_post, w_pre):
    LP, D = h0.shape
    T = _tile(LP, 512)

    def body(h0_ref, mix_ref, wp_ref, wf_ref, h1_ref, n2_ref):
        mix = mix_ref[...]
        h1 = h0_ref[...] + mix * _rms(mix) * wp_ref[...]
        h1_ref[...] = h1
        n2_ref[...] = (h1 * _rms(h1) * wf_ref[...]).astype(BF16)

    row = pl.BlockSpec((T, D), lambda i: (i, 0))
    vec = pl.BlockSpec((1, D), lambda i: (0, 0))
    return pl.pallas_call(
        body, name="resid_fwd", grid=(LP // T,),
        in_specs=[row, row, vec, vec], out_specs=[row, row],
        out_shape=[jax.ShapeDtypeStruct((LP, D), F32), jax.ShapeDtypeStruct((LP, D), BF16)],
        compiler_params=_cp(("parallel",)),
    )(h0, mix, w_post, w_pre)


def _resid_bwd(h1, mix, w_post, w_pre, dout, dn2):
    LP, D = h1.shape
    T = _tile(LP, 512)

    def body(h1_ref, mix_ref, wp_ref, wf_ref, dout_ref, dn2_ref, dh1_ref, dmix_ref, dwf_ref, dwp_ref):
        @pl.when(pl.program_id(0) == 0)
        def _():
            dwf_ref[...] = jnp.zeros_like(dwf_ref)
            dwp_ref[...] = jnp.zeros_like(dwp_ref)
        dx, dwn = _rms_bwd(h1_ref[...], wf_ref[...], dn2_ref[...])
        dh1 = dout_ref[...] + dx
        dh1_ref[...] = dh1
        dwf_ref[...] += _colsum(dwn)
        dmix, dwn2 = _rms_bwd(mix_ref[...], wp_ref[...], dh1)
        dmix_ref[...] = dmix.astype(BF16)
        dwp_ref[...] += _colsum(dwn2)

    row = pl.BlockSpec((T, D), lambda i: (i, 0))
    vec = pl.BlockSpec((1, D), lambda i: (0, 0))
    v = jax.ShapeDtypeStruct((1, D), F32)
    return pl.pallas_call(
        body, name="resid_bwd", grid=(LP // T,),
        in_specs=[row, row, vec, vec, row, row], out_specs=[row, row, vec, vec],
        out_shape=[jax.ShapeDtypeStruct((LP, D), F32), jax.ShapeDtypeStruct((LP, D), BF16), v, v],
        compiler_params=_cp(("arbitrary",)),
    )(h1, mix, w_post, w_pre, dout, dn2)


GELU_C = 0.7978845608028654
GELU_A = 0.044715


def _gelu_parts(x):
    t = jnp.tanh(GELU_C * (x + GELU_A * x * x * x))
    return 0.5 * x * (1.0 + t), t


def _convglu_fwd(up, conv_w, conv_b):
    LP, C = up.shape
    T = _tile(LP, 128)

    def body(x_ref, halo_ref, cw_ref, cb_ref, act_ref):
        i = pl.program_id(0)

        def conv(cols):
            ext = jnp.concatenate([jnp.where(i > 0, halo_ref[:, cols], 0.0), x_ref[:, cols]], axis=0)
            w = cw_ref[:, cols]
            return (w[2:3] * ext[8:] + w[1:2] * pltpu.roll(ext, 1, 0)[8:] + w[0:1] * pltpu.roll(ext, 2, 0)[8:]
                    + cb_ref[:, cols])

        for s in range(D_FF // LANE):
            gs = slice(s * LANE, (s + 1) * LANE)
            g, _ = _gelu_parts(conv(gs))
            act_ref[:, gs] = (g * conv(slice(D_FF + s * LANE, D_FF + (s + 1) * LANE))).astype(BF16)

    t8 = T // 8
    return pl.pallas_call(
        body, name="convglu_fwd", grid=(LP // T,),
        in_specs=[pl.BlockSpec((T, C), lambda i: (i, 0)),
                  pl.BlockSpec((8, C), lambda i: (jnp.maximum(i * t8 - 1, 0), 0)),
                  pl.BlockSpec((FFN_CONV, C), lambda i: (0, 0)), pl.BlockSpec((1, C), lambda i: (0, 0))],
        out_specs=pl.BlockSpec((T, D_FF), lambda i: (i, 0)),
        out_shape=jax.ShapeDtypeStruct((LP, D_FF), BF16),
        compiler_params=_cp(("parallel",)),
    )(up, up, conv_w, conv_b)


def _convglu_bwd(up, conv_w, conv_b, dact):
    LP, C = up.shape
    T = _tile(LP, 128)
    TE = T + 8
    nt = LP // T

    def body(x_ref, xp_ref, xn_ref, cw_ref, cb_ref, da_ref, dan_ref, dx_ref, dcw_ref, dcb_ref):
        i = pl.program_id(0)

        @pl.when(i == 0)
        def _():
            dcw_ref[...] = jnp.zeros_like(dcw_ref)
            dcb_ref[...] = jnp.zeros_like(dcb_ref)

        last = i == nt - 1

        def conv(cols):
            ext = jnp.concatenate([jnp.where(i > 0, xp_ref[:, cols], 0.0), x_ref[:, cols],
                                   jnp.where(last, 0.0, xn_ref[:, cols])], axis=0)
            sh = [ext[8:8 + TE], pltpu.roll(ext, 1, 0)[8:8 + TE], pltpu.roll(ext, 2, 0)[8:8 + TE]]
            w = cw_ref[:, cols]
            return w[2:3] * sh[0] + w[1:2] * sh[1] + w[0:1] * sh[2] + cb_ref[:, cols], sh, w

        def back(cols, dy, sh, w):
            dy_t = dy[0:T]
            dcb_ref[:, cols] += _colsum(dy_t)
            for j in range(FFN_CONV):
                dcw_ref[j:j + 1, cols] += _colsum(dy_t * sh[FFN_CONV - 1 - j][0:T])
            dx_ref[:, cols] = (w[2:3] * dy_t + w[1:2] * pltpu.roll(dy, TE - 1, 0)[0:T]
                               + w[0:1] * pltpu.roll(dy, TE - 2, 0)[0:T]).astype(BF16)

        for s in range(D_FF // LANE):
            gs = slice(s * LANE, (s + 1) * LANE)
            vs = slice(D_FF + s * LANE, D_FF + (s + 1) * LANE)
            gate, sh_g, w_g = conv(gs)
            val, sh_v, w_v = conv(vs)
            g, t = _gelu_parts(gate)
            dg_dx = 0.5 * (1.0 + t) + 0.5 * gate * (1.0 - t * t) * GELU_C * (1.0 + 3.0 * GELU_A * gate * gate)
            da = jnp.concatenate([da_ref[:, gs], jnp.where(last, 0.0, dan_ref[:, gs])], axis=0)
            back(gs, da * val * dg_dx, sh_g, w_g)
            back(vs, da * g, sh_v, w_v)

    t8 = T // 8
    nb8 = LP // 8
    prev8 = lambda w: pl.BlockSpec((8, w), lambda i: (jnp.maximum(i * t8 - 1, 0), 0))
    next8 = lambda w: pl.BlockSpec((8, w), lambda i: (jnp.minimum((i + 1) * t8, nb8 - 1), 0))
    row = lambda w: pl.BlockSpec((T, w), lambda i: (i, 0))
    small = lambda r: pl.BlockSpec((r, C), lambda i: (0, 0))
    return pl.pallas_call(
        body, name="convglu_bwd", grid=(nt,),
        in_specs=[row(C), prev8(C), next8(C), small(FFN_CONV), small(1), row(D_FF), next8(D_FF)],
        out_specs=[row(C), small(FFN_CONV), small(1)],
        out_shape=[jax.ShapeDtypeStruct((LP, C), BF16), jax.ShapeDtypeStruct((FFN_CONV, C), F32),
                   jax.ShapeDtypeStruct((1, C), F32)],
        compiler_params=_cp(("arbitrary",)),
    )(up, up, up, conv_w, conv_b, dact, dact)


def _final(h1, f, w_post, target, n_real):
    LP, D = h1.shape
    T = _tile(LP, 256)

    def body(h1_ref, f_ref, w_ref, t_ref, loss_ref, dout_ref, df_ref, dw_ref):
        @pl.when(pl.program_id(0) == 0)
        def _():
            loss_ref[...] = jnp.zeros_like(loss_ref)
            dw_ref[...] = jnp.zeros_like(dw_ref)
        rows = pl.program_id(0) * T + _iota2((T, 1), 0)
        real = (rows >= ROW0) & (rows < ROW0 + n_real)
        f = f_ref[...]
        out = h1_ref[...] + f * _rms(f) * w_ref[...]
        err = jnp.where(real, out - t_ref[...], 0.0)
        loss_ref[...] += 0.5 * jnp.sum(_colsum(jnp.mean(err * err, axis=-1, keepdims=True)), axis=-1, keepdims=True)
        dout = err * (1.0 / D)
        dout_ref[...] = dout
        dx, dwn = _rms_bwd(f, w_ref[...], dout)
        df_ref[...] = dx.astype(BF16)
        dw_ref[...] += _colsum(dwn)

    row = pl.BlockSpec((T, D), lambda i: (i, 0))
    vec = pl.BlockSpec((1, D), lambda i: (0, 0))
    return pl.pallas_call(
        body, name="final_loss", grid=(LP // T,),
        in_specs=[row, row, vec, row],
        out_specs=[pl.BlockSpec((1, 128), lambda i: (0, 0)), row, row, vec],
        out_shape=[jax.ShapeDtypeStruct((1, 128), F32), jax.ShapeDtypeStruct((LP, D), F32),
                   jax.ShapeDtypeStruct((LP, D), BF16), jax.ShapeDtypeStruct((1, D), F32)],
        compiler_params=_cp(("arbitrary",)),
    )(h1, f, w_post, target)


ANY_SPEC = pl.BlockSpec(memory_space=pl.ANY)
N_CHIP = 4


def _other_chips(x, y):
    return [(1 - x, y), (x, 1 - y), (1 - x, 1 - y)]


def _gather_direct(arrs, name):
    n = len(arrs)
    npeer = N_DEV - 1

    def body(*refs):
        ins, outs = refs[:n], refs[n:2 * n]
        send_sems, recv_sems, loc_sems = refs[2 * n:]
        x, y, c = lax.axis_index("x"), lax.axis_index("y"), lax.axis_index("c")
        me = 4 * x + 2 * y + c
        copies = []
        for a in range(n):
            for kk in range(1, N_DEV):
                px = 1 - x if kk & 4 else x
                py = 1 - y if kk & 2 else y
                pc = 1 - c if kk & 1 else c
                s = a * npeer + kk - 1
                cp = pltpu.make_async_remote_copy(src_ref=ins[a], dst_ref=outs[a].at[me], send_sem=send_sems.at[s],
                                                  recv_sem=recv_sems.at[s], device_id=(px, py, pc), device_id_type=MESH)
                cp.start()
                copies.append(cp)
            own = pltpu.make_async_copy(ins[a], outs[a].at[me], loc_sems.at[a])
            own.start()
            copies.append(own)
        for cp in copies:
            cp.wait()

    shapes = [jax.ShapeDtypeStruct((N_DEV,) + tuple(a.shape), a.dtype) for a in arrs]
    return pl.pallas_call(
        body, name=name, in_specs=[ANY_SPEC] * n, out_specs=[ANY_SPEC] * n, out_shape=shapes,
        scratch_shapes=[pltpu.SemaphoreType.DMA((n * npeer,)), pltpu.SemaphoreType.DMA((n * npeer,)),
                        pltpu.SemaphoreType.DMA((n,))],
        compiler_params=pltpu.CompilerParams(has_side_effects=True),
    )(*arrs)


class _Exchange:
    def __init__(self, arrs, out_shapes, scratch, start, finish, mid=None):
        self.arrs, self.out_shapes, self.scratch = list(arrs), list(out_shapes), list(scratch)
        self.start, self.finish, self.mid = start, finish, mid

    @property
    def n(self):
        return len(self.arrs)


def _run_exchange(ex, name):
    n = ex.n

    def body(*refs):
        ins, outs, sems = refs[:n], refs[n:2 * n], refs[2 * n:]
        ex.start(ins, outs, sems)
        if ex.mid is not None:
            ex.mid(ins, outs, sems)
        ex.finish(ins, outs, sems)

    return pl.pallas_call(
        body, name=name, in_specs=[ANY_SPEC] * n, out_specs=[ANY_SPEC] * n, out_shape=ex.out_shapes,
        scratch_shapes=ex.scratch, compiler_params=pltpu.CompilerParams(has_side_effects=True),
    )(*ex.arrs)


def _carry_begin(ex, refs, step, nsteps):
    if ex is None:
        return

    @pl.when(step == 0)
    def _():
        ex.start(*refs)

    if ex.mid is not None:
        @pl.when(step == min(nsteps - 1, (3 * nsteps) // 5))
        def _():
            ex.mid(*refs)


def _carry_end(ex, refs, step, nsteps):
    if ex is None:
        return

    @pl.when(step == nsteps - 1)
    def _():
        ex.finish(*refs)


def _gather_two_level(arrs):
    n = len(arrs)
    K = 7

    def env(ins, outs, sems):
        send_sems, recv_sems, loc_sems = sems
        x, y, c = lax.axis_index("x"), lax.axis_index("y"), lax.axis_index("c")

        def cp(a, k, src, slot, to):
            return pltpu.make_async_remote_copy(src_ref=src, dst_ref=outs[a].at[slot], send_sem=send_sems.at[a * K + k],
                                                recv_sem=recv_sems.at[a * K + k], device_id=to, device_id_type=MESH)

        me = 4 * x + 2 * y + c
        owns = [pltpu.make_async_copy(ins[a], outs[a].at[me], loc_sems.at[a]) for a in range(n)]
        first = []
        for a in range(n):
            first.append(cp(a, 0, ins[a], me, (x, y, 1 - c)))
            first += [cp(a, 1 + j, ins[a], me, (px, py, c)) for j, (px, py) in enumerate(_other_chips(x, y))]
        passed = []
        for j, (px, py) in enumerate(_other_chips(x, y)):
            slot = 4 * px + 2 * py + c
            passed += [(cp(a, 1 + j, ins[a], slot, (px, py, c)), cp(a, 4 + j, outs[a].at[slot], slot, (x, y, 1 - c)))
                       for a in range(n)]
        from_sib = []
        for a in range(n):
            from_sib.append(cp(a, 0, ins[a], 4 * x + 2 * y + (1 - c), (x, y, 1 - c)))
            from_sib += [cp(a, 4 + j, ins[a], 4 * px + 2 * py + (1 - c), (x, y, 1 - c))
                         for j, (px, py) in enumerate(_other_chips(x, y))]
        return owns, first, passed, from_sib

    def start(ins, outs, sems):
        owns, first, _, _ = env(ins, outs, sems)
        for cp in owns + first:
            cp.start()

    def mid(ins, outs, sems):
        _, _, passed, _ = env(ins, outs, sems)
        for arrival, fwd in passed:
            arrival.wait_recv()
            fwd.start()

    def finish(ins, outs, sems):
        owns, first, passed, from_sib = env(ins, outs, sems)
        for cp in from_sib:
            cp.wait_recv()
        for cp in first + [fwd for _, fwd in passed]:
            cp.wait_send()
        for cp in owns:
            cp.wait()

    shapes = [jax.ShapeDtypeStruct((N_DEV,) + tuple(a.shape), a.dtype) for a in arrs]
    scratch = [pltpu.SemaphoreType.DMA((n * K,)), pltpu.SemaphoreType.DMA((n * K,)), pltpu.SemaphoreType.DMA((n,))]
    return _Exchange(arrs, shapes, scratch, start, finish, mid)


def _swap_sibling(arrs):
    n = len(arrs)

    def copies(ins, outs, sems):
        send_sems, recv_sems = sems
        x, y, c = lax.axis_index("x"), lax.axis_index("y"), lax.axis_index("c")
        return [pltpu.make_async_remote_copy(src_ref=ins[a], dst_ref=outs[a], send_sem=send_sems.at[a],
                                             recv_sem=recv_sems.at[a], device_id=(x, y, 1 - c), device_id_type=MESH)
                for a in range(n)]

    def start(ins, outs, sems):
        for cp in copies(ins, outs, sems):
            cp.start()

    def finish(ins, outs, sems):
        for cp in copies(ins, outs, sems):
            cp.wait()

    shapes = [jax.ShapeDtypeStruct(tuple(a.shape), a.dtype) for a in arrs]
    return _Exchange(arrs, shapes, [pltpu.SemaphoreType.DMA((n,)), pltpu.SemaphoreType.DMA((n,))], start, finish)


def _exchange_chips(arrs):
    n = len(arrs)
    K = N_CHIP - 1

    def copies(ins, outs, sems):
        send_sems, recv_sems, loc_sems = sems
        x, y, c = lax.axis_index("x"), lax.axis_index("y"), lax.axis_index("c")
        mine = 2 * x + y
        out = []
        for a in range(n):
            out += [pltpu.make_async_remote_copy(src_ref=ins[a].at[2 * px + py], dst_ref=outs[a].at[mine],
                                                 send_sem=send_sems.at[a * K + j], recv_sem=recv_sems.at[a * K + j],
                                                 device_id=(px, py, c), device_id_type=MESH)
                    for j, (px, py) in enumerate(_other_chips(x, y))]
            out.append(pltpu.make_async_copy(ins[a].at[mine], outs[a].at[mine], loc_sems.at[a]))
        return out

    def start(ins, outs, sems):
        for cp in copies(ins, outs, sems):
            cp.start()

    def finish(ins, outs, sems):
        for cp in copies(ins, outs, sems):
            cp.wait()

    shapes = [jax.ShapeDtypeStruct(tuple(a.shape), a.dtype) for a in arrs]
    scratch = [pltpu.SemaphoreType.DMA((n * K,)), pltpu.SemaphoreType.DMA((n * K,)), pltpu.SemaphoreType.DMA((n,))]
    return _Exchange(arrs, shapes, scratch, start, finish)


def _add_halves(mine, theirs, name):
    _, R, C = mine.shape
    cap = max(16, ((2 * 1024 * 1024) // (4 * C * 10)) // 16 * 16)
    T = R if R <= cap else _tile(R, cap, 16)

    def body(a_ref, b_ref, o_ref):
        o_ref[...] = (a_ref[...] + b_ref[...].astype(F32)).astype(BF16)

    blk = pl.BlockSpec((N_CHIP, T, C), lambda i: (0, i, 0))
    return pl.pallas_call(
        body, name=name, grid=(R // T,), in_specs=[blk, blk], out_specs=blk,
        out_shape=jax.ShapeDtypeStruct(mine.shape, BF16), compiler_params=_cp(("parallel",)),
    )(mine, theirs)


def _adamw(parts, w, m, v, name):
    R, C = w.shape
    npart = parts.shape[0]
    cap = max(16, ((2 * 1024 * 1024) // (4 * C * 12)) // 16 * 16)
    T = R if R <= cap else _tile(R, cap, 16)

    def body(p_ref, w_ref, m_ref, v_ref, g_ref, d_ref, nm_ref, nv_ref):
        g = p_ref[0].astype(F32)
        for k in range(1, npart):
            g = g + p_ref[k].astype(F32)
        mm = ADAM_B1 * m_ref[...] + (1.0 - ADAM_B1) * g
        vv = ADAM_B2 * v_ref[...] + (1.0 - ADAM_B2) * (g * g)
        m_hat = mm / (1.0 - ADAM_B1 ** ADAM_STEP)
        v_hat = vv / (1.0 - ADAM_B2 ** ADAM_STEP)
        g_ref[...] = g
        d_ref[...] = -ADAM_LR * (m_hat / (jnp.sqrt(v_hat) + ADAM_EPS) + ADAM_WD * w_ref[...])
        nm_ref[...] = mm
        nv_ref[...] = vv

    row = pl.BlockSpec((T, C), lambda i: (i, 0))
    out = jax.ShapeDtypeStruct((R, C), F32)
    return pl.pallas_call(
        body, name=name, grid=(R // T,),
        in_specs=[pl.BlockSpec((npart, T, C), lambda i: (0, i, 0)), row, row, row],
        out_specs=[row] * 4, out_shape=[out] * 4,
        compiler_params=_cp(("parallel",)),
    )(parts, w, m, v)


SMALL = ("attn_pre_norm", "gdn_A_log", "gdn_dt_bias", "gdn_norm_w", "sb_norm_w", "attn_post_norm",
         "ffn_pre_norm", "ffn_conv_b", "ffn_post_norm")


def _pack_small(arrs):
    rows = []
    for a in arrs:
        flat = a.reshape(-1).astype(F32)
        n = -(-flat.shape[0] // 128) * 128
        rows.append(jnp.pad(flat, (0, n - flat.shape[0])).reshape(-1, 128))
    slab = jnp.concatenate(rows, axis=0)
    pad = (-slab.shape[0]) % 8
    return jnp.pad(slab, ((0, pad), (0, 0)))


def _unpack_small(slab, shapes):
    out, r = [], 0
    for shp in shapes:
        size = 1
        for s in shp:
            size *= s
        nr = -(-size // 128)
        out.append(slab[r:r + nr].reshape(-1)[:size].reshape(shp))
        r += nr
    return out


def _to_blocks_cols(a):
    R, C = a.shape
    return a.reshape(R, N_DEV, C // N_DEV).transpose(1, 0, 2)


def _from_blocks_cols(a):
    n, R, c = a.shape
    return a.transpose(1, 0, 2).reshape(R, n * c)


def kernel(x, meta_tokens, attn_pre_norm, w_in, gdn_conv_w, gdn_A_log, gdn_dt_bias, gdn_norm_w, sb_norm_w, w_out, attn_post_norm, ffn_pre_norm, w_ffn_up, ffn_conv_w, ffn_conv_b, w_ffn_down, ffn_post_norm, loss_target, m_meta_tokens, m_attn_pre_norm, m_w_in, m_gdn_conv_w, m_gdn_A_log, m_gdn_dt_bias, m_gdn_norm_w, m_sb_norm_w, m_w_out, m_attn_post_norm, m_ffn_pre_norm, m_w_ffn_up, m_ffn_conv_w, m_ffn_conv_b, m_w_ffn_down, m_ffn_post_norm, v_meta_tokens, v_attn_pre_norm, v_w_in, v_gdn_conv_w, v_gdn_A_log, v_gdn_dt_bias, v_gdn_norm_w, v_sb_norm_w, v_w_out, v_attn_post_norm, v_ffn_pre_norm, v_w_ffn_up, v_ffn_conv_w, v_ffn_conv_b, v_w_ffn_down, v_ffn_post_norm):
    args = dict(locals())
    seq = x.shape[1]
    LP = -(-(ROW0 + seq) // LP_ALIGN) * LP_ALIGN
    tail = LP - ROW0 - seq

    meta_f = _from_blocks_cols(_run_exchange(_gather_two_level([meta_tokens]), "gather_meta")[0])

    h0 = jnp.concatenate([jnp.zeros((PAD_ROWS, D_MODEL), F32), meta_f, x[0], jnp.zeros((tail, D_MODEL), F32)], axis=0)
    target = jnp.concatenate([jnp.zeros((ROW0, D_MODEL), F32), loss_target[0], jnp.zeros((tail, D_MODEL), F32)], axis=0)
    (u,), got = _prenorm_fwd(h0, attn_pre_norm, carry=_gather_two_level([w_in[0].astype(BF16), gdn_conv_w[0]]))
    win_o = _from_blocks_cols(got[0])
    o_ab = C_QKV
    o_z = o_ab + 2 * GDN_HEADS
    w_inp = jnp.concatenate([win_o[:, :C_QKV], win_o[:, o_z:o_z + C_Z], win_o[:, o_z + C_Z:],
                             win_o[:, o_ab:o_z], jnp.zeros((D_MODEL, C_AB - 2 * GDN_HEADS), BF16)], axis=1)
    gconv_f = _from_blocks_cols(got[1])
    proj = _mm(u, w_inp, F32, "mm_in")
    (qn, kn, vg, beta_b, g_b), got = _gdn_pre_fwd(
        proj, gconv_f, gdn_A_log, gdn_dt_bias,
        carry=_gather_two_level([w_out[0].astype(BF16), w_ffn_down[0].astype(BF16)]))
    w_out_f = got[0].reshape(D_MODEL, D_MODEL)
    w_down_f = got[1].reshape(D_FF, D_MODEL)
    (cu, cw, cqd, ckd, cqk, ct, cgl), got = _gdn_chunk_fwd(
        qn, kn, vg, beta_b, g_b, carry=_gather_two_level([w_ffn_up[0].astype(BF16), ffn_conv_w[0]]))
    w_up_f = _from_blocks_cols(got[0])
    fconv_f = _from_blocks_cols(got[1])
    og, ssave = _gdn_scan_fwd(cu, cw, cqd, ckd, cqk, cgl)
    osb, ctot, sb_nrun = _sb_fwd(proj)
    snw = sb_norm_w.reshape(1, SB_HEADS * SB_DH)
    y = _attn_norm_fwd(og, proj, osb, gdn_norm_w, snw)
    mix = _mm(y, w_out_f, F32, "mm_out")
    h1, n2 = _resid_fwd(h0, mix, attn_post_norm, ffn_pre_norm)
    up = _mm(n2, w_up_f, F32, "mm_up")
    act = _convglu_fwd(up, fconv_f, ffn_conv_b)
    f = _mm(act, w_down_f, F32, "mm_down")
    loss_part, dout, df, d_fpost = _final(h1, f, ffn_post_norm, target, seq)
    loss = lax.psum(loss_part[0, 0], ("x", "y", "c"))

    d_wdown = _mm_tn(act, df, "mm_dw_down")
    dact = _mm(df, w_down_f.T, F32, "mm_dact")
    dup, d_fconv, d_fconvb = _convglu_bwd(up, fconv_f, ffn_conv_b, dact)
    d_wup = _mm_tn(n2, dup, "mm_dw_up")
    dn2 = _mm(dup, w_up_f.T, F32, "mm_dn2")
    dh1, dmix, d_fpre, d_apost = _resid_bwd(h1, mix, attn_post_norm, ffn_pre_norm, dout, dn2)
    d_wout = _mm_tn(y, dmix, "mm_dw_out")
    dy = _mm(dmix, w_out_f.T, F32, "mm_dy")
    my_c = lax.axis_index("c")

    def core_halves(blocks):
        halves = [s.reshape((N_CHIP, 2) + s.shape[1:]) for s in blocks]
        return ([lax.dynamic_index_in_dim(h, my_c, axis=1, keepdims=False) for h in halves],
                [lax.dynamic_index_in_dim(h, 1 - my_c, axis=1, keepdims=False).astype(BF16) for h in halves])

    early_names = ("w_out", "w_ffn_up", "w_ffn_down", "ffn_conv_w")
    e_mine, e_send = core_halves([d_wout.reshape(N_DEV, D_MODEL // N_DEV, D_MODEL), _to_blocks_cols(d_wup),
                                  d_wdown.reshape(N_DEV, D_FF // N_DEV, D_MODEL), _to_blocks_cols(d_fconv)])
    (dog, dz, dos, d_gnw, d_snw), e_theirs = _attn_norm_bwd(og, proj, osb, gdn_norm_w, snw, dy,
                                                            carry=_swap_sibling(e_send))
    e_sums = [_add_halves(a, b, "grads_add_" + nm) for nm, a, b in zip(early_names, e_mine, e_theirs)]
    dqs, dks, dvs = _sb_bwd(proj, ctot, sb_nrun, dos)
    (du_, dw_, dqd_, dkd_, dqk_, dgl_), _ = _gdn_scan_bwd(cu, cw, cqd, ckd, cqk, cgl, ssave, dog)
    dqn, dkn, dvg, dbeta, dg = _gdn_chunk_bwd(qn, kn, vg, beta_b, g_b, ct, du_, dw_, dqd_, dkd_, dqk_, dgl_)
    (dqkv, dab, d_gconv, d_gsc), e_recv = _gdn_pre_bwd(proj, gconv_f, gdn_A_log, gdn_dt_bias, dqn, dkn, dvg, dbeta, dg,
                                                       carry=_exchange_chips(e_sums))
    dproj = jnp.concatenate([dqkv.astype(BF16), dz.astype(BF16), dqs.astype(BF16), dks.astype(BF16),
                             dvs.astype(BF16), dab.astype(BF16)], axis=1)
    d_winp = _mm_tn(u, dproj, "mm_dw_in")
    du0 = _mm(dproj, w_inp.T, F32, "mm_du")
    d_win = jnp.concatenate([d_winp[:, :C_QKV], d_winp[:, OFF_AB:OFF_AB + 2 * GDN_HEADS],
                             d_winp[:, OFF_Z:OFF_Z + C_Z], d_winp[:, OFF_SB:OFF_SB + C_SB]], axis=1)
    late_names = ("w_in", "gdn_conv_w")
    l_mine, l_send = core_halves([_to_blocks_cols(d_win), _to_blocks_cols(d_gconv)])
    l_theirs = _run_exchange(_swap_sibling(l_send), "grads_swap_sibling")
    l_sums = [_add_halves(a, b, "grads_add_" + nm) for nm, a, b in zip(late_names, l_mine, l_theirs)]
    (dh0, d_apre), l_recv = _prenorm_bwd(h0, attn_pre_norm, du0, dh1, carry=_exchange_chips(l_sums))
    grad_x = dh0[ROW0:ROW0 + seq][None]
    d_meta = dh0[PAD_ROWS:ROW0]

    small_grads = [d_apre, d_gsc[0:1, :GDN_HEADS], d_gsc[1:2, :GDN_HEADS], d_gnw, d_snw.reshape(1, SB_HEADS, SB_DH),
                   d_apost, d_fpre, d_fconvb, d_fpost]
    n_small_rows = _pack_small(small_grads).shape[0]
    slab_parts = _gather_direct([jnp.concatenate([_pack_small(small_grads), d_meta.reshape(-1, LANE)], axis=0)],
                                name="gather_small_grads")[0]
    me = 4 * lax.axis_index("x") + 2 * lax.axis_index("y") + my_c
    meta_parts = lax.dynamic_index_in_dim(
        slab_parts[:, n_small_rows:].reshape(N_DEV, N_META, N_DEV, LANE), me, axis=2, keepdims=False)
    slab_parts = slab_parts[:, :n_small_rows]

    res = {}
    for nm, parts in zip(early_names + late_names + ("meta_tokens",), list(e_recv) + list(l_recv) + [meta_parts]):
        wloc = args[nm]
        shp = wloc.shape
        w2 = wloc.reshape(shp[-2], shp[-1])
        outs = _adamw(parts, w2, args["m_" + nm].reshape(w2.shape), args["v_" + nm].reshape(w2.shape), "adamw_" + nm)
        res[nm] = [o.reshape(shp) for o in outs]
    small_shapes = [args[nm].shape for nm in SMALL]
    outs = _adamw(slab_parts, _pack_small([args[nm] for nm in SMALL]), _pack_small([args["m_" + nm] for nm in SMALL]),
                  _pack_small([args["v_" + nm] for nm in SMALL]), "adamw_small")
    for k in range(4):
        for nm, val in zip(SMALL, _unpack_small(outs[k], small_shapes)):
            res.setdefault(nm, [None] * 4)[k] = val

    order = ("meta_tokens", "attn_pre_norm", "w_in", "gdn_conv_w", "gdn_A_log", "gdn_dt_bias", "gdn_norm_w",
             "sb_norm_w", "w_out", "attn_post_norm", "ffn_pre_norm", "w_ffn_up", "ffn_conv_w", "ffn_conv_b",
             "w_ffn_down", "ffn_post_norm")
    return (loss, grad_x, *[res[nm][0] for nm in order], *[res[nm][1] for nm in order],
            *[res[nm][2] for nm in order], *[res[nm][3] for nm in order])
```

```python
import functools

import jax
import jax.numpy as jnp
from jax import lax
from jax.experimental import pallas as pl
from jax.experimental.pallas import tpu as pltpu

F32 = jnp.float32
BF16 = jnp.bfloat16

D_MODEL = 1024
N_META = 16
GDN_HEADS = 4
GDN_D = 128
GDN_CHUNK = 64
GDN_CONV = 4
GDN_ROWS = 256
SCAN_CHUNKS = 4
SB_HEADS = 8
SB_DH = 64
SB_BLOCK = 128
D_FF = 2816
FFN_CONV = 3
NORM_EPS = 1e-6
L2_EPS = 1e-6
LANE = 128
N_DEV = 8

PAD_ROWS = SB_BLOCK - N_META
ROW0 = SB_BLOCK
SB_SPAN = 512
SB_DEAD = -104.0
SB_SUB = 256
SB_QTILE = 256
LP_ALIGN = 256

C_QKV = 3 * GDN_HEADS * GDN_D
C_Z = GDN_HEADS * GDN_D
C_SB = 3 * SB_HEADS * SB_DH
C_AB = 256
OFF_Z = C_QKV
OFF_SB = OFF_Z + C_Z
OFF_AB = OFF_SB + C_SB
D_INP = OFF_AB + C_AB
D_IN = C_QKV + 2 * GDN_HEADS + C_Z + C_SB

ADAM_LR = 0.001
ADAM_B1 = 0.9
ADAM_B2 = 0.999
ADAM_EPS = 1e-08
ADAM_WD = 0.01
ADAM_STEP = 10

VMEM_LIMIT = 56 * 1024 * 1024
MESH = pl.DeviceIdType.MESH


def _cp(sem=None):
    kw = dict(vmem_limit_bytes=VMEM_LIMIT)
    if sem is not None:
        kw["dimension_semantics"] = sem
    return pltpu.CompilerParams(**kw)


def _tile(n, cap, unit=128):
    best = None
    t = unit
    while t <= min(n, cap):
        if n % t == 0:
            best = t
        t += unit
    assert best is not None, (n, cap, unit)
    return best


def _dot(a, b):
    return jnp.dot(a, b, preferred_element_type=F32)


def _dot_nt(a, b):
    return lax.dot_general(a, b, (((1,), (1,)), ((), ())), preferred_element_type=F32)


def _dot_tn(a, b):
    return lax.dot_general(a, b, (((0,), (0,)), ((), ())), preferred_element_type=F32)


def _split(x):
    hi = x.astype(BF16)
    lo = (x - hi.astype(F32)).astype(BF16)
    return hi, lo


def _dot1(a, b, f=_dot):
    return f(a.astype(BF16), b.astype(BF16))


def _dot3(a, b, f=_dot):
    ah, al = _split(a)
    bh, bl = _split(b)
    return f(ah, bh) + (f(ah, bl) + f(al, bh))


def _dot_exact_l(m_bf16, x, f=_dot):
    xh, xl = _split(x)
    return f(m_bf16, xh) + f(m_bf16, xl)


def _dot_exact_r(x, m_bf16, f=_dot):
    xh, xl = _split(x)
    return f(xh, m_bf16) + f(xl, m_bf16)


def _iota2(shape, dim):
    return lax.broadcasted_iota(jnp.int32, shape, dim)


def _sigmoid(x):
    return 1.0 / (1.0 + jnp.exp(-x))


def _softplus(x):
    return jnp.maximum(x, 0.0) + jnp.log(1.0 + jnp.exp(-jnp.abs(x)))


def _colsum(x):
    return jnp.sum(x, axis=0, keepdims=True)


def _rowsum(x):
    return jnp.sum(x, axis=-1, keepdims=True)


def _mm(a, b, out_dtype, name):
    M, K = a.shape
    K2, N = b.shape
    assert K == K2
    tm = _tile(M, 768)
    tn = _tile(N, max(128, (6 * 1024 * 1024) // (2 * K)))

    def body(a_ref, b_ref, o_ref):
        o_ref[...] = _dot(a_ref[...].astype(BF16), b_ref[...].astype(BF16)).astype(o_ref.dtype)

    return pl.pallas_call(
        body, name=name, grid=(N // tn, M // tm),
        in_specs=[pl.BlockSpec((tm, K), lambda j, i: (i, 0)), pl.BlockSpec((K, tn), lambda j, i: (0, j))],
        out_specs=pl.BlockSpec((tm, tn), lambda j, i: (i, j)),
        out_shape=jax.ShapeDtypeStruct((M, N), out_dtype),
        compiler_params=_cp(("parallel", "parallel")),
    )(a, b)


def _mm_nt(a, b, out_dtype, name):
    M, K = a.shape
    N, K2 = b.shape
    assert K == K2
    tm = _tile(M, 768)
    tn = _tile(N, max(128, (6 * 1024 * 1024) // (2 * K)))

    def body(a_ref, b_ref, o_ref):
        o_ref[...] = _dot_nt(a_ref[...].astype(BF16), b_ref[...].astype(BF16)).astype(o_ref.dtype)

    return pl.pallas_call(
        body, name=name, grid=(N // tn, M // tm),
        in_specs=[pl.BlockSpec((tm, K), lambda j, i: (i, 0)), pl.BlockSpec((tn, K), lambda j, i: (j, 0))],
        out_specs=pl.BlockSpec((tm, tn), lambda j, i: (i, j)),
        out_shape=jax.ShapeDtypeStruct((M, N), out_dtype),
        compiler_params=_cp(("parallel", "parallel")),
    )(a, b)


def _mm_nt_pieces(pieces, offsets, b, out_dtype, name):
    M = pieces[0].shape[0]
    N = b.shape[0]
    n = len(pieces)
    widths = [p.shape[1] for p in pieces]
    assert all(off % k == 0 for off, k in zip(offsets, widths))
    tm = _tile(M, 768)
    tn = _tile(N, 512)

    def body(*refs):
        acc = _dot_nt(refs[0][...].astype(BF16), refs[n][...].astype(BF16))
        for p in range(1, n):
            acc = acc + _dot_nt(refs[p][...].astype(BF16), refs[n + p][...].astype(BF16))
        refs[2 * n][...] = acc.astype(out_dtype)

    return pl.pallas_call(
        body, name=name, grid=(N // tn, M // tm),
        in_specs=[pl.BlockSpec((tm, k), lambda j, i: (i, 0)) for k in widths]
        + [pl.BlockSpec((tn, k), functools.partial(lambda j, i, blk: (j, blk), blk=off // k))
           for off, k in zip(offsets, widths)],
        out_specs=pl.BlockSpec((tm, tn), lambda j, i: (i, j)),
        out_shape=jax.ShapeDtypeStruct((M, N), out_dtype),
        compiler_params=_cp(("parallel", "parallel")),
    )(*pieces, *([b] * n))


def _mm_tn_pieces(a, pieces, name):
    M, K = a.shape
    n = len(pieces)
    tm = _tile(M, 768)

    def body(*refs):
        @pl.when(pl.program_id(0) == 0)
        def _():
            for p in range(n):
                refs[1 + n + p][...] = jnp.zeros_like(refs[1 + n + p])
        at = refs[0][...].astype(BF16)
        for p in range(n):
            refs[1 + n + p][...] += _dot_tn(at, refs[1 + p][...].astype(BF16))

    return pl.pallas_call(
        body, name=name, grid=(M // tm,),
        in_specs=[pl.BlockSpec((tm, K), lambda m: (m, 0))] + [pl.BlockSpec((tm, p.shape[1]), lambda m: (m, 0)) for p in pieces],
        out_specs=[pl.BlockSpec((K, p.shape[1]), lambda m: (0, 0)) for p in pieces],
        out_shape=[jax.ShapeDtypeStruct((K, p.shape[1]), F32) for p in pieces],
        compiler_params=_cp(("arbitrary",)),
    )(a, *pieces)


def _mm_tn(a, b, name):
    M, K = a.shape
    M2, N = b.shape
    assert M == M2
    tm = _tile(M, 1408)
    tk = _tile(K, 1408)
    tn = _tile(N, 1408)

    def body(a_ref, b_ref, o_ref):
        @pl.when(pl.program_id(2) == 0)
        def _():
            o_ref[...] = jnp.zeros_like(o_ref)
        o_ref[...] += _dot_tn(a_ref[...].astype(BF16), b_ref[...].astype(BF16))

    return pl.pallas_call(
        body, name=name, grid=(K // tk, N // tn, M // tm),
        in_specs=[pl.BlockSpec((tm, tk), lambda i, j, m: (m, i)), pl.BlockSpec((tm, tn), lambda i, j, m: (m, j))],
        out_specs=pl.BlockSpec((tk, tn), lambda i, j, m: (i, j)),
        out_shape=jax.ShapeDtypeStruct((K, N), F32),
        compiler_params=_cp(("parallel", "parallel", "arbitrary")),
    )(a, b)


def _rms(x):
    return lax.rsqrt(jnp.mean(x * x, axis=-1, keepdims=True) + NORM_EPS)


def _rms_bwd(x, w, dy):
    r = _rms(x)
    n = x * r
    dyw = dy * w
    dx = r * (dyw - n * jnp.mean(dyw * n, axis=-1, keepdims=True))
    return dx, dy * n


def _prenorm_fwd(h0, w, carry=None):
    LP, D = h0.shape
    T = _tile(LP, 512)

    def body(h_ref, w_ref, u_ref):
        h = h_ref[...]
        u_ref[...] = (h * _rms(h) * w_ref[...]).astype(BF16)

    return _call_carrying(
        carry, body, LP // T, name="prenorm_fwd",
        in_specs=[pl.BlockSpec((T, D), lambda i: (i, 0)), pl.BlockSpec((1, D), lambda i: (0, 0))],
        out_specs=[pl.BlockSpec((T, D), lambda i: (i, 0))],
        out_shape=[jax.ShapeDtypeStruct((LP, D), BF16)],
        operands=(h0, w))


def _prenorm_bwd(h0, w, du, dh1, carry=None):
    LP, D = h0.shape
    T = _tile(LP, 512)

    def body(h_ref, w_ref, du_ref, dh1_ref, dh0_ref, dw_ref):
        @pl.when(pl.program_id(0) == 0)
        def _():
            dw_ref[...] = jnp.zeros_like(dw_ref)
        dx, dwn = _rms_bwd(h_ref[...], w_ref[...], du_ref[...])
        dh0_ref[...] = dh1_ref[...] + dx
        dw_ref[...] += _colsum(dwn)

    row = pl.BlockSpec((T, D), lambda i: (i, 0))
    vec = pl.BlockSpec((1, D), lambda i: (0, 0))
    return _call_carrying(
        carry, body, LP // T, name="prenorm_bwd",
        in_specs=[row, vec, row, row], out_specs=[row, vec],
        out_shape=[jax.ShapeDtypeStruct((LP, D), F32), jax.ShapeDtypeStruct((1, D), F32)],
        operands=(h0, w, du, dh1))


def _causal_taps(ext, w_ref, width, start, rows):
    y = w_ref[width - 1:width, :] * ext[start:start + rows]
    for j in range(width - 1):
        y = y + w_ref[j:j + 1, :] * pltpu.roll(ext, width - 1 - j, 0)[start:start + rows]
    return y


def _shifted_rows(ext, shift, start, rows):
    return ext[start:start + rows] if shift == 0 else pltpu.roll(ext, shift, 0)[start:start + rows]


def _anticausal_taps(dy_ext, w_ref, width, rows):
    n = dy_ext.shape[0]
    dx = w_ref[width - 1:width, :] * dy_ext[0:rows]
    for j in range(width - 1):
        dx = dx + w_ref[j:j + 1, :] * pltpu.roll(dy_ext, n - (width - 1 - j), 0)[0:rows]
    return dx


def _gdn_gate_consts(alog_ref, dtb_ref, h):
    a_coef = -jnp.exp(alog_ref[0:1, h:h + 1])
    return a_coef, dtb_ref[0:1, h:h + 1]


def _gdn_pre_fwd(proj, conv_w, a_log, dt_bias, carry=None):
    LP = proj.shape[0]
    T = _tile(LP, 256)
    C = C_QKV
    H = GDN_HEADS

    def body(x_ref, halo_ref, ab_ref, cw_ref, alog_ref, dtb_ref, q_ref, k_ref, v_ref, beta_ref, g_ref):
        i = pl.program_id(0)
        ext = jnp.concatenate([jnp.where(i > 0, halo_ref[...], 0.0), x_ref[...]], axis=0)
        y = _causal_taps(ext, cw_ref, GDN_CONV, 8, T)
        c = y * _sigmoid(y)
        for h in range(H):
            sl = slice(h * GDN_D, (h + 1) * GDN_D)
            cq = c[:, sl]
            q_ref[:, sl] = cq * lax.rsqrt(_rowsum(cq * cq) + L2_EPS) * (GDN_D ** -0.5)
            ck = c[:, 512 + h * GDN_D:512 + (h + 1) * GDN_D]
            k_ref[:, sl] = ck * lax.rsqrt(_rowsum(ck * ck) + L2_EPS)
        v_ref[...] = c[:, 1024:]
        ab = ab_ref[...]
        valid = (i * T + _iota2((T, 1), 0)) >= PAD_ROWS
        for h in range(H):
            sl = slice(h * GDN_D, (h + 1) * GDN_D)
            a_coef, dtb = _gdn_gate_consts(alog_ref, dtb_ref, h)
            g = jnp.where(valid, a_coef * _softplus(ab[:, h:h + 1] + dtb), 0.0)
            beta = jnp.where(valid, _sigmoid(ab[:, H + h:H + h + 1]), 0.0)
            g_ref[:, sl] = jnp.broadcast_to(g, (T, GDN_D))
            beta_ref[:, sl] = jnp.broadcast_to(beta, (T, GDN_D))

    t8 = T // 8
    row512 = pl.BlockSpec((T, 512), lambda i: (i, 0))
    small = lambda r, c: pl.BlockSpec((r, c), lambda i: (0, 0))
    out = jax.ShapeDtypeStruct((LP, 512), F32)
    return _call_carrying(
        carry, body, LP // T, name="gdn_pre_fwd",
        in_specs=[pl.BlockSpec((T, C), lambda i: (i, 0)),
                  pl.BlockSpec((8, C), lambda i: (jnp.maximum(i * t8 - 1, 0), 0)),
                  pl.BlockSpec((T, C_AB), lambda i: (i, OFF_AB // C_AB)),
                  small(GDN_CONV, C), small(1, H), small(1, H)],
        out_specs=[row512] * 5, out_shape=[out] * 5,
        operands=(proj, proj, proj, conv_w, a_log, dt_bias))


def _gdn_pre_bwd(proj, conv_w, a_log, dt_bias, dq, dk, dv, dbeta, dg, carry=None):
    LP = proj.shape[0]
    T = _tile(LP, 256)
    C = C_QKV
    H = GDN_HEADS
    TE = T + 8
    nt = LP // T

    def body(x_ref, xp_ref, xn_ref, ab_ref, cw_ref, alog_ref, dtb_ref,
             dq_ref, dqn_ref, dk_ref, dkn_ref, dv_ref, dvn_ref, dbeta_ref, dg_ref,
             dx_ref, dab_ref, dcw_ref, dsc_ref, dys):
        i = pl.program_id(0)

        @pl.when(i == 0)
        def _():
            dcw_ref[...] = jnp.zeros_like(dcw_ref)
            dsc_ref[...] = jnp.zeros_like(dsc_ref)

        last = i == nt - 1
        ext = jnp.concatenate([jnp.where(i > 0, xp_ref[...], 0.0), x_ref[...], jnp.where(last, 0.0, xn_ref[...])],
                              axis=0)
        y = _causal_taps(ext, cw_ref, GDN_CONV, 8, TE)
        sg = _sigmoid(y)
        c = y * sg
        nxt = lambda a_ref, b_ref: jnp.concatenate([a_ref[...], jnp.where(last, 0.0, b_ref[...])], axis=0)
        dqn = nxt(dq_ref, dqn_ref)
        dkn = nxt(dk_ref, dkn_ref)
        dvv = nxt(dv_ref, dvn_ref)
        for h in range(H):
            sl = slice(h * GDN_D, (h + 1) * GDN_D)
            cq = c[:, sl]
            rq = lax.rsqrt(_rowsum(cq * cq) + L2_EPS)
            nq = cq * rq
            dqh = dqn[:, sl]
            dys[:, sl] = (GDN_D ** -0.5) * rq * (dqh - nq * _rowsum(dqh * nq))
            sk = slice(512 + h * GDN_D, 512 + (h + 1) * GDN_D)
            ck = c[:, sk]
            rk = lax.rsqrt(_rowsum(ck * ck) + L2_EPS)
            nk = ck * rk
            dkh = dkn[:, sl]
            dys[:, sk] = rk * (dkh - nk * _rowsum(dkh * nk))
        dys[:, 1024:] = dvv
        dy = dys[...] * (sg * (1.0 + y * (1.0 - sg)))
        for j in range(GDN_CONV):
            dcw_ref[j:j + 1, :] += _colsum(dy[0:T, :] * _shifted_rows(ext, GDN_CONV - 1 - j, 8, T))
        dx_ref[...] = _anticausal_taps(dy, cw_ref, GDN_CONV, T).astype(BF16)
        ab = ab_ref[...]
        valid = (i * T + _iota2((T, 1), 0)) >= PAD_ROWS
        lane = _iota2((T, C_AB), 1)
        lane1 = _iota2((1, 128), 1)
        dab = jnp.zeros((T, C_AB), F32)
        dsc_a = jnp.zeros((1, 128), F32)
        dsc_d = jnp.zeros((1, 128), F32)
        for h in range(H):
            a_coef, dtb = _gdn_gate_consts(alog_ref, dtb_ref, h)
            pre = ab[:, h:h + 1] + dtb
            dgh = jnp.where(valid, dg_ref[:, h * GDN_D:h * GDN_D + 1], 0.0)
            da = dgh * a_coef * _sigmoid(pre)
            beta = _sigmoid(ab[:, H + h:H + h + 1])
            db = jnp.where(valid, dbeta_ref[:, h * GDN_D:h * GDN_D + 1], 0.0) * beta * (1.0 - beta)
            dab = dab + jnp.where(lane == h, da, 0.0) + jnp.where(lane == H + h, db, 0.0)
            dsc_a = dsc_a + jnp.where(lane1 == h, _colsum(dgh * a_coef * _softplus(pre)), 0.0)
            dsc_d = dsc_d + jnp.where(lane1 == h, _colsum(da), 0.0)
        dab_ref[...] = dab.astype(BF16)
        dsc_ref[0:1, :] += dsc_a
        dsc_ref[1:2, :] += dsc_d

    t8 = T // 8
    nb8 = LP // 8
    prev8 = lambda w: pl.BlockSpec((8, w), lambda i: (jnp.maximum(i * t8 - 1, 0), 0))
    next8 = lambda w: pl.BlockSpec((8, w), lambda i: (jnp.minimum((i + 1) * t8, nb8 - 1), 0))
    row = lambda w: pl.BlockSpec((T, w), lambda i: (i, 0))
    small = lambda r, c: pl.BlockSpec((r, c), lambda i: (0, 0))
    return _call_carrying(
        carry, body, nt, name="gdn_pre_bwd",
        in_specs=[row(C), prev8(C), next8(C), pl.BlockSpec((T, C_AB), lambda i: (i, OFF_AB // C_AB)),
                  small(GDN_CONV, C), small(1, H), small(1, H),
                  row(512), next8(512), row(512), next8(512), row(512), next8(512), row(512), row(512)],
        out_specs=[row(C), row(C_AB), small(GDN_CONV, C), small(2, 128)],
        out_shape=[jax.ShapeDtypeStruct((LP, C), BF16), jax.ShapeDtypeStruct((LP, C_AB), BF16),
                   jax.ShapeDtypeStruct((GDN_CONV, C), F32), jax.ShapeDtypeStruct((2, 128), F32)],
        scratch_shapes=[pltpu.VMEM((TE, C), F32)],
        operands=(proj, proj, proj, proj, conv_w, a_log, dt_bias, dq, dq, dk, dk, dv, dv, dbeta, dg))


def _tri_masks():
    r = _iota2((GDN_CHUNK, GDN_CHUNK), 0)
    c = _iota2((GDN_CHUNK, GDN_CHUNK), 1)
    return r >= c, r > c


def _gdn_chunk_common(q, k, v, beta, gb):
    incl, strict = _tri_masks()
    l_incl = incl.astype(BF16)
    gd = _dot_exact_l(l_incl, jnp.where(strict, gb[:, :GDN_CHUNK], 0.0))
    gc = _dot_exact_l(l_incl, gb)
    decay = jnp.where(incl, jnp.exp(jnp.where(incl, gd, 0.0)), 0.0)
    exp_g = jnp.exp(gc)
    g_last = gc[GDN_CHUNK - 1:GDN_CHUNK, :]
    kd_fac = jnp.exp(g_last - gc)
    gl = jnp.exp(g_last)
    kb = k * beta
    kk = _dot1(kb, k, _dot_nt)
    return dict(incl=incl, strict=strict, decay=decay, exp_g=exp_g, kd_fac=kd_fac, gl=gl, kb=kb, kk=kk,
                vb=v * beta, kbg=kb * exp_g)


def _interleave(gens):
    gens = list(gens)
    while gens:
        alive = []
        for g in gens:
            try:
                next(g)
                alive.append(g)
            except StopIteration:
                pass
        gens = alive


def _call_carrying(ex, body, nsteps, *, name, in_specs, out_specs, out_shape, operands, scratch_shapes=()):
    n_in, n_out, n_scr = len(in_specs), len(out_specs), len(scratch_shapes)
    n = ex.n if ex is not None else 0

    def full(*refs):
        o0 = n_in + n
        s0 = o0 + n_out + n
        ex_refs = (refs[n_in:o0], refs[o0 + n_out:s0], refs[s0 + n_scr:])
        step = pl.program_id(0)
        _carry_begin(ex, ex_refs, step, nsteps)
        body(*refs[:n_in], *refs[o0:o0 + n_out], *refs[s0:s0 + n_scr])
        _carry_end(ex, ex_refs, step, nsteps)

    res = pl.pallas_call(
        full, name=name, grid=(nsteps,),
        in_specs=list(in_specs) + [ANY_SPEC] * n, out_specs=list(out_specs) + [ANY_SPEC] * n,
        out_shape=list(out_shape) + (ex.out_shapes if ex is not None else []),
        scratch_shapes=list(scratch_shapes) + (ex.scratch if ex is not None else []),
        compiler_params=pltpu.CompilerParams(dimension_semantics=("arbitrary",), vmem_limit_bytes=VMEM_LIMIT,
                                             has_side_effects=ex is not None),
    )(*operands, *(ex.arrs if ex is not None else []))
    return list(res[:n_out]), list(res[n_out:])


def _gdn_chunk_fwd(qn, kn, v, beta_b, g_b, carry=None):
    LP = qn.shape[0]
    R = GDN_ROWS
    H = GDN_HEADS
    CH = GDN_CHUNK

    def body(q_ref, k_ref, v_ref, b_ref, g_ref, u_ref, w_ref, qd_ref, kd_ref, qk_ref, t_ref, gl_ref):
        def item(cc, h):
            rs = slice(cc * CH, (cc + 1) * CH)
            sl = slice(h * GDN_D, (h + 1) * GDN_D)
            s64 = slice(h * CH, (h + 1) * CH)
            q, k = q_ref[rs, sl], k_ref[rs, sl]
            m = _gdn_chunk_common(q, k, v_ref[rs, sl], b_ref[rs, sl], g_ref[rs, sl])
            qk_raw = _dot1(q, k, _dot_nt)
            yield
            a = jnp.where(m["strict"], m["kk"] * m["decay"], 0.0)
            eye = (_iota2((CH, CH), 0) == _iota2((CH, CH), 1)).astype(F32)
            t = eye - a
            p = _dot3(a, a)
            yield
            for _ in range(4):
                t = t + _dot3(t, p)
                p = _dot3(p, p)
                yield
            t = t + _dot3(t, p)
            yield
            u_ref[rs, sl] = _dot1(t, m["vb"])
            w_ref[rs, sl] = _dot1(t, m["kbg"])
            qk_ref[rs, s64] = qk_raw * m["decay"]
            t_ref[rs, s64] = t
            qd_ref[rs, sl] = q * m["exp_g"]
            kd_ref[rs, sl] = k * m["kd_fac"]
            gl_ref[cc * 8:(cc + 1) * 8, sl] = jnp.broadcast_to(m["gl"], (8, GDN_D))

        _interleave(item(cc, h) for cc in range(R // CH) for h in range(H))

    row = lambda w: pl.BlockSpec((R, w), lambda i: (i, 0))
    o512 = jax.ShapeDtypeStruct((LP, 512), F32)
    o256 = jax.ShapeDtypeStruct((LP, 256), F32)
    return _call_carrying(
        carry, body, LP // R, name="gdn_chunk_fwd",
        in_specs=[row(512)] * 5,
        out_specs=[row(512)] * 4 + [row(256)] * 2 + [pl.BlockSpec((R // 8, 512), lambda i: (i, 0))],
        out_shape=[o512] * 4 + [o256] * 2 + [jax.ShapeDtypeStruct((LP // 8, 512), F32)],
        operands=(qn, kn, v, beta_b, g_b))


def _gdn_chunk_bwd(qn, kn, v, beta_b, g_b, t_all, du, dw, dqd, dkd, dqk, dgl):
    LP = qn.shape[0]
    R = GDN_ROWS
    H = GDN_HEADS
    CH = GDN_CHUNK

    def body(q_ref, k_ref, v_ref, b_ref, g_ref, t_ref, du_ref, dw_ref, dqd_ref, dkd_ref, dqk_ref, dgl_ref,
             dq_ref, dk_ref, dv_ref, db_ref, dg_ref):
        ones = jnp.ones((CH, GDN_D), BF16)

        def item(cc, h):
            rs = slice(cc * CH, (cc + 1) * CH)
            sl = slice(h * GDN_D, (h + 1) * GDN_D)
            s64 = slice(h * CH, (h + 1) * CH)
            q, k, vv, beta = q_ref[rs, sl], k_ref[rs, sl], v_ref[rs, sl], b_ref[rs, sl]
            m = _gdn_chunk_common(q, k, vv, beta, g_ref[rs, sl])
            incl, strict, decay = m["incl"], m["strict"], m["decay"]
            t = t_ref[rs, s64]
            du_, dw_ = du_ref[rs, sl], dw_ref[rs, sl]
            dqd_, dkd_ = dqd_ref[rs, sl], dkd_ref[rs, sl]
            d_t = _dot1(du_, m["vb"], _dot_nt) + _dot1(dw_, m["kbg"], _dot_nt)
            dvb = _dot1(t, du_, _dot_tn)
            dkbg = _dot1(t, dw_, _dot_tn)
            qk_raw = _dot1(q, k, _dot_nt)
            yield
            x1 = _dot3(d_t, t, _dot_nt)
            dkb = dkbg * m["exp_g"]
            d_gi = _rowsum(dkbg * m["kbg"])
            yield
            d_a = jnp.where(strict, -_dot3(t, x1, _dot_tn), 0.0)
            yield
            d_kk = d_a * decay
            dqk_m = jnp.where(incl, dqk_ref[rs, s64], 0.0)
            dqk_raw = dqk_m * decay
            mm = (d_a * m["kk"] + dqk_m * qk_raw) * decay
            dkb = dkb + _dot1(d_kk, k)
            dk_ = _dot1(d_kk, m["kb"], _dot_tn) + _dot1(dqk_raw, q, _dot_tn)
            dq_ = _dot1(dqk_raw, k) + dqd_ * m["exp_g"]
            d_gi = d_gi + (_dot_exact_r(mm, ones) - _dot_exact_r(mm, ones, _dot_tn))
            yield
            d_gi = d_gi + _rowsum(dqd_ * q * m["exp_g"])
            e = _rowsum(dkd_ * k * m["kd_fac"])
            d_gi = d_gi - e
            d_glast = _colsum(jnp.broadcast_to(e, (CH, GDN_D))) + dgl_ref[cc * 8:cc * 8 + 1, sl] * m["gl"]
            dk_ = dk_ + dkd_ * m["kd_fac"] + dkb * beta
            d_gi = d_gi + jnp.where(_iota2((CH, GDN_D), 0) == CH - 1, d_glast, 0.0)
            u_incl = (_iota2((CH, CH), 1) >= _iota2((CH, CH), 0)).astype(BF16)
            dq_ref[rs, sl] = dq_
            dk_ref[rs, sl] = dk_
            dv_ref[rs, sl] = dvb * beta
            db_ref[rs, sl] = jnp.broadcast_to(_rowsum(dvb * vv) + _rowsum(dkb * k), (CH, GDN_D))
            dg_ref[rs, sl] = _dot_exact_l(u_incl, d_gi)

        _interleave(item(cc, h) for cc in range(R // CH) for h in range(H))

    row = lambda w: pl.BlockSpec((R, w), lambda i: (i, 0))
    o512 = jax.ShapeDtypeStruct((LP, 512), F32)
    gl_spec = pl.BlockSpec((R // 8, 512), lambda i: (i, 0))
    return pl.pallas_call(
        body, name="gdn_chunk_bwd", grid=(LP // R,),
        in_specs=[row(512)] * 5 + [row(256)] + [row(512)] * 4 + [row(256), gl_spec],
        out_specs=[row(512)] * 5, out_shape=[o512] * 5,
        compiler_params=_cp(("parallel",)),
    )(qn, kn, v, beta_b, g_b, t_all, du, dw, dqd, dkd, dqk, dgl)


def _gdn_scan_fwd(u, w, qd, kd, qk, gl):
    LP = u.shape[0]
    CH = GDN_CHUNK
    CPS = SCAN_CHUNKS
    N = LP // CH
    NS = N // CPS
    H = GDN_HEADS

    def body(u_ref, w_ref, qd_ref, kd_ref, qk_ref, gl_ref, o_ref, ssave_ref, s_sc):
        @pl.when(pl.program_id(0) == 0)
        def _():
            s_sc[...] = jnp.zeros_like(s_sc)

        for cc in range(CPS):
            rs = slice(cc * CH, (cc + 1) * CH)
            ssave_ref[cc * GDN_D:(cc + 1) * GDN_D, :] = s_sc[...]

            def item(h):
                sl = slice(h * GDN_D, (h + 1) * GDN_D)
                s = s_sc[:, sl]
                v_new = u_ref[rs, sl] - _dot1(w_ref[rs, sl], s)
                o_s = _dot1(qd_ref[rs, sl], s)
                yield
                o_ref[rs, sl] = o_s + _dot1(qk_ref[rs, h * CH:(h + 1) * CH], v_new)
                s_sc[:, sl] = s * gl_ref[cc * 8:cc * 8 + 1, sl] + _dot1(kd_ref[rs, sl], v_new, _dot_tn)

            _interleave(item(h) for h in range(H))

    row = lambda w_: pl.BlockSpec((CPS * CH, w_), lambda n: (n, 0))
    return pl.pallas_call(
        body, name="gdn_scan_fwd", grid=(NS,),
        in_specs=[row(512)] * 4 + [row(256), pl.BlockSpec((CPS * 8, 512), lambda n: (n, 0))],
        out_specs=[row(512), pl.BlockSpec((CPS * GDN_D, 512), lambda n: (n, 0))],
        out_shape=[jax.ShapeDtypeStruct((LP, 512), F32), jax.ShapeDtypeStruct((N * GDN_D, 512), F32)],
        scratch_shapes=[pltpu.VMEM((GDN_D, 512), F32)],
        compiler_params=_cp(("arbitrary",)),
    )(u, w, qd, kd, qk, gl)


def _gdn_scan_bwd(u, w, qd, kd, qk, gl, ssave, do, carry=None):
    LP = u.shape[0]
    CH = GDN_CHUNK
    CPS = SCAN_CHUNKS
    N = LP // CH
    NS = N // CPS
    H = GDN_HEADS

    def body(u_ref, w_ref, qd_ref, kd_ref, qk_ref, gl_ref, s_ref, do_ref,
             du_ref, dw_ref, dqd_ref, dkd_ref, dqk_ref, dgl_ref, ds_sc):
        @pl.when(pl.program_id(0) == 0)
        def _():
            ds_sc[...] = jnp.zeros_like(ds_sc)

        for cc in reversed(range(CPS)):
            rs = slice(cc * CH, (cc + 1) * CH)
            r8 = slice(cc * 8, (cc + 1) * 8)

            def item(h):
                sl = slice(h * GDN_D, (h + 1) * GDN_D)
                s64 = slice(h * CH, (h + 1) * CH)
                s = s_ref[cc * GDN_D:(cc + 1) * GDN_D, sl]
                ds = ds_sc[:, sl]
                do_ = do_ref[rs, sl]
                w_, qd_, kd_, qk_ = w_ref[rs, sl], qd_ref[rs, sl], kd_ref[rs, sl], qk_ref[rs, s64]
                v_new = u_ref[rs, sl] - _dot1(w_, s)
                d_vnew = _dot1(qk_, do_, _dot_tn) + _dot1(kd_, ds)
                dqd_ref[rs, sl] = _dot1(do_, s, _dot_nt)
                ds_new = ds * gl_ref[cc * 8:cc * 8 + 1, sl] + _dot1(qd_, do_, _dot_tn)
                dgl_ref[r8, sl] = jnp.broadcast_to(jnp.sum(_colsum(ds * s), axis=-1, keepdims=True), (8, GDN_D))
                yield
                du_ref[rs, sl] = d_vnew
                dw_ref[rs, sl] = -_dot1(d_vnew, s, _dot_nt)
                dkd_ref[rs, sl] = _dot1(v_new, ds, _dot_nt)
                dqk_ref[rs, s64] = _dot1(do_, v_new, _dot_nt)
                ds_sc[:, sl] = ds_new - _dot1(w_, d_vnew, _dot_tn)

            _interleave(item(h) for h in range(H))

    rev = lambda w_: pl.BlockSpec((CPS * CH, w_), lambda n: (NS - 1 - n, 0))
    rev8 = pl.BlockSpec((CPS * 8, 512), lambda n: (NS - 1 - n, 0))
    o512 = jax.ShapeDtypeStruct((LP, 512), F32)
    return _call_carrying(
        carry, body, NS, name="gdn_scan_bwd",
        in_specs=[rev(512)] * 4 + [rev(256), rev8, pl.BlockSpec((CPS * GDN_D, 512), lambda n: (NS - 1 - n, 0)),
                  rev(512)],
        out_specs=[rev(512)] * 4 + [rev(256), rev8],
        out_shape=[o512] * 4 + [jax.ShapeDtypeStruct((LP, 256), F32), jax.ShapeDtypeStruct((LP // 8, 512), F32)],
        scratch_shapes=[pltpu.VMEM((GDN_D, 512), F32)],
        operands=(u, w, qd, kd, qk, gl, ssave, do))


def _sb_scores(qh, kblk, mask):
    z = _dot_nt(qh, kblk)
    e = jnp.exp(-jnp.abs(z))
    sp = jnp.maximum(z, 0.0) + jnp.log(1.0 + e)
    return z, e, jnp.where(mask, -sp, 0.0), z - sp


def _sb_fwd(proj):
    LP = proj.shape[0]
    B = SB_BLOCK
    W = min(SB_SPAN, LP)
    SUB = SB_SUB
    Q = min(SB_QTILE, LP)
    nq = LP // Q
    nsub = W // SUB
    scale = SB_DH ** -0.5
    qcol, kcol, vcol = OFF_SB // B, (OFF_SB + 512) // B, (OFF_SB + 1024) // B

    def body(q_ref, k_ref, v_ref, o_ref, c_ref, n_ref):
        i = pl.program_id(1)
        lane = _iota2((Q, B), 1)
        head_a = lane < SB_DH
        qs = q_ref[...] * scale
        qh = [jnp.where(head_a, qs, 0.0).astype(BF16), jnp.where(head_a, 0.0, qs).astype(BF16)]
        u_strict = (_iota2((SUB, SUB), 0) > _iota2((SUB, SUB), 1)).astype(BF16)
        qpos = i * Q + _iota2((Q, W), 0)
        hi0 = (i + 1) * Q
        nspan = (hi0 + W - 1) // W

        def live(st):
            return (st[0] < nspan) & (st[1] > 0)

        def span(st):
            r, carry = st[0], st[2:]
            hi = hi0 - r * W
            k0 = pl.multiple_of(jnp.maximum(hi - W, 0), B)
            kblk = k_ref[pl.ds(k0, W), :].astype(BF16)
            vblk = v_ref[pl.ds(k0, W), :].astype(BF16)
            kpos = k0 + _iota2((Q, W), 1)
            mask = (kpos < qpos) & (kpos >= PAD_ROWS) & (kpos < hi)
            new = [None] * 4

            def head(h):
                o_acc, c = carry[2 * h], carry[2 * h + 1]
                z, e, l1m, lsg = _sb_scores(qh[h], kblk, mask)
                yield
                subs = [slice(b * SUB, (b + 1) * SUB) for b in range(nsub)]
                suf = [_dot(l1m[:, bs].astype(BF16), u_strict) for bs in subs]
                yield
                parts = [None] * nsub
                for b in reversed(range(nsub)):
                    parts[b] = jnp.where(mask[:, subs[b]], jnp.exp(lsg[:, subs[b]] + suf[b] + c), 0.0)
                    c = c + _rowsum(l1m[:, subs[b]])
                att = jnp.concatenate(parts, axis=1).astype(BF16)
                new[2 * h], new[2 * h + 1] = o_acc + _dot(att, vblk), c

            _interleave(head(h) for h in range(2))
            more = (jnp.maximum(jnp.max(new[1]), jnp.max(new[3])) > SB_DEAD).astype(jnp.int32)
            return (r + 1, more, *new)

        zero_o = jnp.zeros((Q, B), F32)
        zero_c = jnp.zeros((Q, 1), F32)
        nrun, _, o_a, c_a, o_b, c_b = lax.while_loop(
            live, span, (jnp.int32(0), jnp.int32(1), zero_o, zero_c, zero_o, zero_c))
        o_ref[...] = jnp.where(head_a, o_a, o_b)
        c_ref[...] = jnp.where(head_a, c_a, c_b)
        n_ref[pl.program_id(0), i] = nrun

    blk = pl.BlockSpec((Q, B), lambda p, i: (i, p))
    out = jax.ShapeDtypeStruct((LP, 512), F32)
    return pl.pallas_call(
        body, name="sb_fwd", grid=(SB_HEADS // 2, nq),
        in_specs=[pl.BlockSpec((Q, B), lambda p, i: (i, qcol + p)),
                  pl.BlockSpec((LP, B), lambda p, i: (0, kcol + p)),
                  pl.BlockSpec((LP, B), lambda p, i: (0, vcol + p))],
        out_specs=[blk, blk, pl.BlockSpec(memory_space=pltpu.SMEM)],
        out_shape=[out, out, jax.ShapeDtypeStruct((SB_HEADS // 2, nq), jnp.int32)],
        compiler_params=_cp(("arbitrary", "arbitrary")),
    )(proj, proj, proj)


def _sb_bwd(proj, ctot, nrun_all, do):
    LP = proj.shape[0]
    B = SB_BLOCK
    W = min(SB_SPAN, LP)
    SUB = SB_SUB
    Q = min(SB_QTILE, LP)
    nq = LP // Q
    nsub = W // SUB
    scale = SB_DH ** -0.5
    qcol, kcol, vcol = OFF_SB // B, (OFF_SB + 512) // B, (OFF_SB + 1024) // B

    def body(n_ref, q_ref, k_ref, v_ref, c_ref, do_ref, dq_ref, dk_ref, dv_ref):
        i = pl.program_id(1)

        @pl.when(i == 0)
        def _():
            dk_ref[...] = jnp.zeros_like(dk_ref)
            dv_ref[...] = jnp.zeros_like(dv_ref)

        lane = _iota2((Q, B), 1)
        head_a = lane < SB_DH
        qs = q_ref[...] * scale
        qh = [jnp.where(head_a, qs, 0.0).astype(BF16), jnp.where(head_a, 0.0, qs).astype(BF16)]
        dof = do_ref[...]
        doh = [jnp.where(head_a, dof, 0.0).astype(BF16), jnp.where(head_a, 0.0, dof).astype(BF16)]
        cfull = c_ref[...]
        ctot_h = [cfull[:, 0:1], cfull[:, SB_DH:SB_DH + 1]]
        sub_r, sub_c = _iota2((SUB, SUB), 0), _iota2((SUB, SUB), 1)
        u_strict = (sub_r > sub_c).astype(BF16)
        l_strict = (sub_r < sub_c).astype(BF16)
        qpos = i * Q + _iota2((Q, W), 0)
        hi0 = (i + 1) * Q
        nrun = n_ref[pl.program_id(0), i]

        def span(t, carry):
            r = nrun - 1 - t
            hi = hi0 - r * W
            k0 = pl.multiple_of(jnp.maximum(hi - W, 0), B)
            kblk = k_ref[pl.ds(k0, W), :].astype(BF16)
            vblk = v_ref[pl.ds(k0, W), :].astype(BF16)
            kpos = k0 + _iota2((Q, W), 1)
            mask = (kpos < qpos) & (kpos >= PAD_ROWS) & (kpos < hi)
            new = [None] * 6
            dk_add, dv_add = [None, None], [None, None]
            subs = [slice(b * SUB, (b + 1) * SUB) for b in range(nsub)]

            def head(h):
                dq_acc, pre, ecar = carry[3 * h], carry[3 * h + 1], carry[3 * h + 2]
                z, e, l1m, lsg = _sb_scores(qh[h], kblk, mask)
                d_att = _dot_nt(doh[h], vblk)
                yield
                sig = jnp.where(z >= 0.0, 1.0, e) / (1.0 + e)
                suf = [_dot(l1m[:, bs].astype(BF16), u_strict) for bs in subs]
                yield
                att_parts, p_parts = [None] * nsub, [None] * nsub
                for b, bs in enumerate(subs):
                    pre = pre + _rowsum(l1m[:, bs])
                    att_parts[b] = jnp.where(mask[:, bs], jnp.exp(lsg[:, bs] + suf[b] + (ctot_h[h] - pre)), 0.0)
                    p_parts[b] = att_parts[b] * d_att[:, bs]
                pcum = [_dot(p.astype(BF16), l_strict) for p in p_parts]
                yield
                dz_parts = [None] * nsub
                for b, bs in enumerate(subs):
                    sg = sig[:, bs]
                    dz_parts[b] = jnp.where(mask[:, bs], p_parts[b] * (1.0 - sg) - sg * (ecar + pcum[b]), 0.0)
                    ecar = ecar + _rowsum(p_parts[b])
                att = jnp.concatenate(att_parts, axis=1).astype(BF16)
                dz = jnp.concatenate(dz_parts, axis=1).astype(BF16)
                new[3 * h:3 * h + 3] = [dq_acc + _dot(dz, kblk), pre, ecar]
                dk_add[h] = _dot_tn(dz, qh[h])
                dv_add[h] = _dot_tn(att, doh[h])

            _interleave(head(h) for h in range(2))
            dk_ref[pl.ds(k0, W), :] += dk_add[0] + dk_add[1]
            dv_ref[pl.ds(k0, W), :] += dv_add[0] + dv_add[1]
            return tuple(new)

        zero_o = jnp.zeros((Q, B), F32)
        zero_c = jnp.zeros((Q, 1), F32)
        res = lax.fori_loop(0, nrun, span, (zero_o, zero_c, zero_c, zero_o, zero_c, zero_c))
        dq_ref[...] = (jnp.where(head_a, res[0], res[3]) * scale).astype(BF16)

    blk = pl.BlockSpec((Q, B), lambda p, i: (i, p))
    col = pl.BlockSpec((LP, B), lambda p, i: (0, p))
    out = jax.ShapeDtypeStruct((LP, 512), F32)
    return pl.pallas_call(
        body, name="sb_bwd", grid=(SB_HEADS // 2, nq),
        in_specs=[pl.BlockSpec(memory_space=pltpu.SMEM),
                  pl.BlockSpec((Q, B), lambda p, i: (i, qcol + p)),
                  pl.BlockSpec((LP, B), lambda p, i: (0, kcol + p)),
                  pl.BlockSpec((LP, B), lambda p, i: (0, vcol + p)),
                  blk, blk],
        out_specs=[blk, col, col], out_shape=[jax.ShapeDtypeStruct((LP, 512), BF16), out, out],
        compiler_params=_cp(("arbitrary", "arbitrary")),
    )(nrun_all, proj, proj, proj, ctot, do)


def _sb_group_mean():
    r = jnp.right_shift(_iota2((512, 512), 0), 6)
    c = jnp.right_shift(_iota2((512, 512), 1), 6)
    return jnp.where(r == c, 1.0 / SB_DH, 0.0).astype(BF16)


def _attn_norm_fwd(og, proj, osb, gnw, snw):
    LP = og.shape[0]
    T = _tile(LP, 256)

    def body(og_ref, z_ref, os_ref, gnw_ref, snw_ref, y_ref):
        valid = (pl.program_id(0) * T + _iota2((T, 1), 0)) >= PAD_ROWS
        z = z_ref[...]
        zg = z * _sigmoid(z)
        for h in range(GDN_HEADS):
            sl = slice(h * GDN_D, (h + 1) * GDN_D)
            o = og_ref[:, sl]
            y = o * _rms(o) * gnw_ref[...] * zg[:, sl]
            y_ref[:, sl] = jnp.where(valid, y, 0.0).astype(BF16)
        o = os_ref[...]
        msq = _dot_exact_r(o * o, _sb_group_mean())
        y = o * lax.rsqrt(msq + NORM_EPS) * snw_ref[...]
        y_ref[:, 512:] = jnp.where(valid, y, 0.0).astype(BF16)

    row = pl.BlockSpec((T, 512), lambda i: (i, 0))
    return pl.pallas_call(
        body, name="attn_norm_fwd", grid=(LP // T,),
        in_specs=[row, pl.BlockSpec((T, 512), lambda i: (i, OFF_Z // 512)), row,
                  pl.BlockSpec((1, GDN_D), lambda i: (0, 0)), pl.BlockSpec((1, 512), lambda i: (0, 0))],
        out_specs=pl.BlockSpec((T, 1024), lambda i: (i, 0)),
        out_shape=jax.ShapeDtypeStruct((LP, 1024), BF16),
        compiler_params=_cp(("parallel",)),
    )(og, proj, osb, gnw, snw)


def _attn_norm_bwd(og, proj, osb, gnw, snw, dy, carry=None):
    LP = og.shape[0]
    T = _tile(LP, 256)

    def body(og_ref, z_ref, os_ref, gnw_ref, snw_ref, dy_ref, dog_ref, dz_ref, dos_ref, dgw_ref, dsw_ref):
        @pl.when(pl.program_id(0) == 0)
        def _():
            dgw_ref[...] = jnp.zeros_like(dgw_ref)
            dsw_ref[...] = jnp.zeros_like(dsw_ref)
        valid = (pl.program_id(0) * T + _iota2((T, 1), 0)) >= PAD_ROWS
        dy = jnp.where(valid, dy_ref[...], 0.0)
        z = z_ref[...]
        sg = _sigmoid(z)
        zg = z * sg
        dgw = jnp.zeros((1, GDN_D), F32)
        for h in range(GDN_HEADS):
            sl = slice(h * GDN_D, (h + 1) * GDN_D)
            o = og_ref[:, sl]
            dyh = dy[:, sl]
            dx, dwn = _rms_bwd(o, gnw_ref[...], dyh * zg[:, sl])
            dog_ref[:, sl] = dx
            dgw = dgw + _colsum(dwn)
            yn = o * _rms(o) * gnw_ref[...]
            dz_ref[:, sl] = (dyh * yn * (sg[:, sl] * (1.0 + z[:, sl] * (1.0 - sg[:, sl])))).astype(BF16)
        dgw_ref[...] += dgw
        o = os_ref[...]
        gm = _sb_group_mean()
        r = lax.rsqrt(_dot_exact_r(o * o, gm) + NORM_EPS)
        n = o * r
        dys = dy[:, 512:]
        dyw = dys * snw_ref[...]
        dos_ref[...] = r * (dyw - n * _dot_exact_r(dyw * n, gm))
        dsw_ref[...] += _colsum(dys * n)

    row = pl.BlockSpec((T, 512), lambda i: (i, 0))
    gw = pl.BlockSpec((1, GDN_D), lambda i: (0, 0))
    sw = pl.BlockSpec((1, 512), lambda i: (0, 0))
    o512 = jax.ShapeDtypeStruct((LP, 512), F32)
    return _call_carrying(
        carry, body, LP // T, name="attn_norm_bwd",
        in_specs=[row, pl.BlockSpec((T, 512), lambda i: (i, OFF_Z // 512)), row, gw, sw,
                  pl.BlockSpec((T, 1024), lambda i: (i, 0))],
        out_specs=[row, row, row, gw, sw],
        out_shape=[o512, jax.ShapeDtypeStruct((LP, 512), BF16), o512, jax.ShapeDtypeStruct((1, GDN_D), F32),
                   jax.ShapeDtypeStruct((1, 512), F32)],
        operands=(og, proj, osb, gnw, snw, dy))


def _resid_fwd(h0, mix, w_post, w_pre):
    LP, D = h0.shape
    T = _tile(LP, 512)

    def body(h0_ref, mix_ref, wp_ref, wf_ref, h1_ref, n2_ref):
        mix = mix_ref[...]
        h1 = h0_ref[...] + mix * _rms(mix) * wp_ref[...]
        h1_ref[...] = h1
        n2_ref[...] = (h1 * _rms(h1) * wf_ref[...]).astype(BF16)

    row = pl.BlockSpec((T, D), lambda i: (i, 0))
    vec = pl.BlockSpec((1, D), lambda i: (0, 0))
    return pl.pallas_call(
        body, name="resid_fwd", grid=(LP // T,),
        in_specs=[row, row, vec, vec], out_specs=[row, row],
        out_shape=[jax.ShapeDtypeStruct((LP, D), F32), jax.ShapeDtypeStruct((LP, D), BF16)],
        compiler_params=_cp(("parallel",)),
    )(h0, mix, w_post, w_pre)


def _resid_bwd(h1, mix, w_post, w_pre, dout, dn2):
    LP, D = h1.shape
    T = _tile(LP, 512)

    def body(h1_ref, mix_ref, wp_ref, wf_ref, dout_ref, dn2_ref, dh1_ref, dmix_ref, dwf_ref, dwp_ref):
        @pl.when(pl.program_id(0) == 0)
        def _():
            dwf_ref[...] = jnp.zeros_like(dwf_ref)
            dwp_ref[...] = jnp.zeros_like(dwp_ref)
        dx, dwn = _rms_bwd(h1_ref[...], wf_ref[...], dn2_ref[...])
        dh1 = dout_ref[...] + dx
        dh1_ref[...] = dh1
        dwf_ref[...] += _colsum(dwn)
        dmix, dwn2 = _rms_bwd(mix_ref[...], wp_ref[...], dh1)
        dmix_ref[...] = dmix.astype(BF16)
        dwp_ref[...] += _colsum(dwn2)

    row = pl.BlockSpec((T, D), lambda i: (i, 0))
    vec = pl.BlockSpec((1, D), lambda i: (0, 0))
    v = jax.ShapeDtypeStruct((1, D), F32)
    return pl.pallas_call(
        body, name="resid_bwd", grid=(LP // T,),
        in_specs=[row, row, vec, vec, row, row], out_specs=[row, row, vec, vec],
        out_shape=[jax.ShapeDtypeStruct((LP, D), F32), jax.ShapeDtypeStruct((LP, D), BF16), v, v],
        compiler_params=_cp(("arbitrary",)),
    )(h1, mix, w_post, w_pre, dout, dn2)


GELU_C = 0.7978845608028654
GELU_A = 0.044715


def _gelu_parts(x):
    t = jnp.tanh(GELU_C * (x + GELU_A * x * x * x))
    return 0.5 * x * (1.0 + t), t


def _convglu_fwd(up, conv_w, conv_b):
    LP, C = up.shape
    T = _tile(LP, 128)

    def body(x_ref, halo_ref, cw_ref, cb_ref, act_ref):
        i = pl.program_id(0)

        def conv(cols):
            ext = jnp.concatenate([jnp.where(i > 0, halo_ref[:, cols], 0.0), x_ref[:, cols]], axis=0)
            w = cw_ref[:, cols]
            return (w[2:3] * ext[8:] + w[1:2] * pltpu.roll(ext, 1, 0)[8:] + w[0:1] * pltpu.roll(ext, 2, 0)[8:]
                    + cb_ref[:, cols])

        for s in range(D_FF // LANE):
            gs = slice(s * LANE, (s + 1) * LANE)
            g, _ = _gelu_parts(conv(gs))
            act_ref[:, gs] = (g * conv(slice(D_FF + s * LANE, D_FF + (s + 1) * LANE))).astype(BF16)

    t8 = T // 8
    return pl.pallas_call(
        body, name="convglu_fwd", grid=(LP // T,),
        in_specs=[pl.BlockSpec((T, C), lambda i: (i, 0)),
                  pl.BlockSpec((8, C), lambda i: (jnp.maximum(i * t8 - 1, 0), 0)),
                  pl.BlockSpec((FFN_CONV, C), lambda i: (0, 0)), pl.BlockSpec((1, C), lambda i: (0, 0))],
        out_specs=pl.BlockSpec((T, D_FF), lambda i: (i, 0)),
        out_shape=jax.ShapeDtypeStruct((LP, D_FF), BF16),
        compiler_params=_cp(("parallel",)),
    )(up, up, conv_w, conv_b)


def _convglu_bwd(up, conv_w, conv_b, dact):
    LP, C = up.shape
    T = _tile(LP, 128)
    TE = T + 8
    nt = LP // T

    def body(x_ref, xp_ref, xn_ref, cw_ref, cb_ref, da_ref, dan_ref, dx_ref, dcw_ref, dcb_ref):
        i = pl.program_id(0)

        @pl.when(i == 0)
        def _():
            dcw_ref[...] = jnp.zeros_like(dcw_ref)
            dcb_ref[...] = jnp.zeros_like(dcb_ref)

        last = i == nt - 1

        def conv(cols):
            ext = jnp.concatenate([jnp.where(i > 0, xp_ref[:, cols], 0.0), x_ref[:, cols],
                                   jnp.where(last, 0.0, xn_ref[:, cols])], axis=0)
            sh = [ext[8:8 + TE], pltpu.roll(ext, 1, 0)[8:8 + TE], pltpu.roll(ext, 2, 0)[8:8 + TE]]
            w = cw_ref[:, cols]
            return w[2:3] * sh[0] + w[1:2] * sh[1] + w[0:1] * sh[2] + cb_ref[:, cols], sh, w

        def back(cols, dy, sh, w):
            dy_t = dy[0:T]
            dcb_ref[:, cols] += _colsum(dy_t)
            for j in range(FFN_CONV):
                dcw_ref[j:j + 1, cols] += _colsum(dy_t * sh[FFN_CONV - 1 - j][0:T])
            dx_ref[:, cols] = (w[2:3] * dy_t + w[1:2] * pltpu.roll(dy, TE - 1, 0)[0:T]
                               + w[0:1] * pltpu.roll(dy, TE - 2, 0)[0:T]).astype(BF16)

        for s in range(D_FF // LANE):
            gs = slice(s * LANE, (s + 1) * LANE)
            vs = slice(D_FF + s * LANE, D_FF + (s + 1) * LANE)
            gate, sh_g, w_g = conv(gs)
            val, sh_v, w_v = conv(vs)
            g, t = _gelu_parts(gate)
            dg_dx = 0.5 * (1.0 + t) + 0.5 * gate * (1.0 - t * t) * GELU_C * (1.0 + 3.0 * GELU_A * gate * gate)
            da = jnp.concatenate([da_ref[:, gs], jnp.where(last, 0.0, dan_ref[:, gs])], axis=0)
            back(gs, da * val * dg_dx, sh_g, w_g)
            back(vs, da * g, sh_v, w_v)

    t8 = T // 8
    nb8 = LP // 8
    prev8 = lambda w: pl.BlockSpec((8, w), lambda i: (jnp.maximum(i * t8 - 1, 0), 0))
    next8 = lambda w: pl.BlockSpec((8, w), lambda i: (jnp.minimum((i + 1) * t8, nb8 - 1), 0))
    row = lambda w: pl.BlockSpec((T, w), lambda i: (i, 0))
    small = lambda r: pl.BlockSpec((r, C), lambda i: (0, 0))
    return pl.pallas_call(
        body, name="convglu_bwd", grid=(nt,),
        in_specs=[row(C), prev8(C), next8(C), small(FFN_CONV), small(1), row(D_FF), next8(D_FF)],
        out_specs=[row(C), small(FFN_CONV), small(1)],
        out_shape=[jax.ShapeDtypeStruct((LP, C), BF16), jax.ShapeDtypeStruct((FFN_CONV, C), F32),
                   jax.ShapeDtypeStruct((1, C), F32)],
        compiler_params=_cp(("arbitrary",)),
    )(up, up, up, conv_w, conv_b, dact, dact)


def _final(h1, f, w_post, target, n_real):
    LP, D = h1.shape
    T = _tile(LP, 256)

    def body(h1_ref, f_ref, w_ref, t_ref, loss_ref, dout_ref, df_ref, dw_ref):
        @pl.when(pl.program_id(0) == 0)
        def _():
            loss_ref[...] = jnp.zeros_like(loss_ref)
            dw_ref[...] = jnp.zeros_like(dw_ref)
        rows = pl.program_id(0) * T + _iota2((T, 1), 0)
        real = (rows >= ROW0) & (rows < ROW0 + n_real)
        f = f_ref[...]
        out = h1_ref[...] + f * _rms(f) * w_ref[...]
        err = jnp.where(real, out - t_ref[...], 0.0)
        loss_ref[...] += 0.5 * jnp.sum(_colsum(jnp.mean(err * err, axis=-1, keepdims=True)), axis=-1, keepdims=True)
        dout = err * (1.0 / D)
        dout_ref[...] = dout
        dx, dwn = _rms_bwd(f, w_ref[...], dout)
        df_ref[...] = dx.astype(BF16)
        dw_ref[...] += _colsum(dwn)

    row = pl.BlockSpec((T, D), lambda i: (i, 0))
    vec = pl.BlockSpec((1, D), lambda i: (0, 0))
    return pl.pallas_call(
        body, name="final_loss", grid=(LP // T,),
        in_specs=[row, row, vec, row],
        out_specs=[pl.BlockSpec((1, 128), lambda i: (0, 0)), row, row, vec],
        out_shape=[jax.ShapeDtypeStruct((1, 128), F32), jax.ShapeDtypeStruct((LP, D), F32),
                   jax.ShapeDtypeStruct((LP, D), BF16), jax.ShapeDtypeStruct((1, D), F32)],
        compiler_params=_cp(("arbitrary",)),
    )(h1, f, w_post, target)


ANY_SPEC = pl.BlockSpec(memory_space=pl.ANY)
N_CHIP = 4


def _other_chips(x, y):
    return [(1 - x, y), (x, 1 - y), (1 - x, 1 - y)]


def _gather_direct(arrs, name):
    n = len(arrs)
    npeer = N_DEV - 1

    def body(*refs):
        ins, outs = refs[:n], refs[n:2 * n]
        send_sems, recv_sems, loc_sems = refs[2 * n:]
        x, y, c = lax.axis_index("x"), lax.axis_index("y"), lax.axis_index("c")
        me = 4 * x + 2 * y + c
        copies = []
        for a in range(n):
            for kk in range(1, N_DEV):
                px = 1 - x if kk & 4 else x
                py = 1 - y if kk & 2 else y
                pc = 1 - c if kk & 1 else c
                s = a * npeer + kk - 1
                cp = pltpu.make_async_remote_copy(src_ref=ins[a], dst_ref=outs[a].at[me], send_sem=send_sems.at[s],
                                                  recv_sem=recv_sems.at[s], device_id=(px, py, pc), device_id_type=MESH)
                cp.start()
                copies.append(cp)
            own = pltpu.make_async_copy(ins[a], outs[a].at[me], loc_sems.at[a])
            own.start()
            copies.append(own)
        for cp in copies:
            cp.wait()

    shapes = [jax.ShapeDtypeStruct((N_DEV,) + tuple(a.shape), a.dtype) for a in arrs]
    return pl.pallas_call(
        body, name=name, in_specs=[ANY_SPEC] * n, out_specs=[ANY_SPEC] * n, out_shape=shapes,
        scratch_shapes=[pltpu.SemaphoreType.DMA((n * npeer,)), pltpu.SemaphoreType.DMA((n * npeer,)),
                        pltpu.SemaphoreType.DMA((n,))],
        compiler_params=pltpu.CompilerParams(has_side_effects=True),
    )(*arrs)


class _Exchange:
    def __init__(self, arrs, out_shapes, scratch, start, finish, mid=None):
        self.arrs, self.out_shapes, self.scratch = list(arrs), list(out_shapes), list(scratch)
        self.start, self.finish, self.mid = start, finish, mid

    @property
    def n(self):
        return len(self.arrs)


def _run_exchange(ex, name):
    n = ex.n

    def body(*refs):
        ins, outs, sems = refs[:n], refs[n:2 * n], refs[2 * n:]
        ex.start(ins, outs, sems)
        if ex.mid is not None:
            ex.mid(ins, outs, sems)
        ex.finish(ins, outs, sems)

    return pl.pallas_call(
        body, name=name, in_specs=[ANY_SPEC] * n, out_specs=[ANY_SPEC] * n, out_shape=ex.out_shapes,
        scratch_shapes=ex.scratch, compiler_params=pltpu.CompilerParams(has_side_effects=True),
    )(*ex.arrs)


def _carry_begin(ex, refs, step, nsteps):
    if ex is None:
        return

    @pl.when(step == 0)
    def _():
        ex.start(*refs)

    if ex.mid is not None:
        @pl.when(step == min(nsteps - 1, (3 * nsteps) // 5))
        def _():
            ex.mid(*refs)


def _carry_end(ex, refs, step, nsteps):
    if ex is None:
        return

    @pl.when(step == nsteps - 1)
    def _():
        ex.finish(*refs)


def _gather_two_level(arrs):
    n = len(arrs)
    K = 7

    def env(ins, outs, sems):
        send_sems, recv_sems, loc_sems = sems
        x, y, c = lax.axis_index("x"), lax.axis_index("y"), lax.axis_index("c")

        def cp(a, k, src, slot, to):
            return pltpu.make_async_remote_copy(src_ref=src, dst_ref=outs[a].at[slot], send_sem=send_sems.at[a * K + k],
                                                recv_sem=recv_sems.at[a * K + k], device_id=to, device_id_type=MESH)

        me = 4 * x + 2 * y + c
        owns = [pltpu.make_async_copy(ins[a], outs[a].at[me], loc_sems.at[a]) for a in range(n)]
        first = []
        for a in range(n):
            first.append(cp(a, 0, ins[a], me, (x, y, 1 - c)))
            first += [cp(a, 1 + j, ins[a], me, (px, py, c)) for j, (px, py) in enumerate(_other_chips(x, y))]
        passed = []
        for j, (px, py) in enumerate(_other_chips(x, y)):
            slot = 4 * px + 2 * py + c
            passed += [(cp(a, 1 + j, ins[a], slot, (px, py, c)), cp(a, 4 + j, outs[a].at[slot], slot, (x, y, 1 - c)))
                       for a in range(n)]
        from_sib = []
        for a in range(n):
            from_sib.append(cp(a, 0, ins[a], 4 * x + 2 * y + (1 - c), (x, y, 1 - c)))
            from_sib += [cp(a, 4 + j, ins[a], 4 * px + 2 * py + (1 - c), (x, y, 1 - c))
                         for j, (px, py) in enumerate(_other_chips(x, y))]
        return owns, first, passed, from_sib

    def start(ins, outs, sems):
        owns, first, _, _ = env(ins, outs, sems)
        for cp in owns + first:
            cp.start()

    def mid(ins, outs, sems):
        _, _, passed, _ = env(ins, outs, sems)
        for arrival, fwd in passed:
            arrival.wait_recv()
            fwd.start()

    def finish(ins, outs, sems):
        owns, first, passed, from_sib = env(ins, outs, sems)
        for cp in from_sib:
            cp.wait_recv()
        for cp in first + [fwd for _, fwd in passed]:
            cp.wait_send()
        for cp in owns:
            cp.wait()

    shapes = [jax.ShapeDtypeStruct((N_DEV,) + tuple(a.shape), a.dtype) for a in arrs]
    scratch = [pltpu.SemaphoreType.DMA((n * K,)), pltpu.SemaphoreType.DMA((n * K,)), pltpu.SemaphoreType.DMA((n,))]
    return _Exchange(arrs, shapes, scratch, start, finish, mid)


def _swap_sibling(arrs):
    n = len(arrs)

    def copies(ins, outs, sems):
        send_sems, recv_sems = sems
        x, y, c = lax.axis_index("x"), lax.axis_index("y"), lax.axis_index("c")
        return [pltpu.make_async_remote_copy(src_ref=ins[a], dst_ref=outs[a], send_sem=send_sems.at[a],
                                             recv_sem=recv_sems.at[a], device_id=(x, y, 1 - c), device_id_type=MESH)
                for a in range(n)]

    def start(ins, outs, sems):
        for cp in copies(ins, outs, sems):
            cp.start()

    def finish(ins, outs, sems):
        for cp in copies(ins, outs, sems):
            cp.wait()

    shapes = [jax.ShapeDtypeStruct(tuple(a.shape), a.dtype) for a in arrs]
    return _Exchange(arrs, shapes, [pltpu.SemaphoreType.DMA((n,)), pltpu.SemaphoreType.DMA((n,))], start, finish)


def _exchange_chips(arrs):
    n = len(arrs)
    K = N_CHIP - 1

    def copies(ins, outs, sems):
        send_sems, recv_sems, loc_sems = sems
        x, y, c = lax.axis_index("x"), lax.axis_index("y"), lax.axis_index("c")
        mine = 2 * x + y
        out = []
        for a in range(n):
            out += [pltpu.make_async_remote_copy(src_ref=ins[a].at[2 * px + py], dst_ref=outs[a].at[mine],
                                                 send_sem=send_sems.at[a * K + j], recv_sem=recv_sems.at[a * K + j],
                                                 device_id=(px, py, c), device_id_type=MESH)
                    for j, (px, py) in enumerate(_other_chips(x, y))]
            out.append(pltpu.make_async_copy(ins[a].at[mine], outs[a].at[mine], loc_sems.at[a]))
        return out

    def start(ins, outs, sems):
        for cp in copies(ins, outs, sems):
            cp.start()

    def finish(ins, outs, sems):
        for cp in copies(ins, outs, sems):
            cp.wait()

    shapes = [jax.ShapeDtypeStruct(tuple(a.shape), a.dtype) for a in arrs]
    scratch = [pltpu.SemaphoreType.DMA((n * K,)), pltpu.SemaphoreType.DMA((n * K,)), pltpu.SemaphoreType.DMA((n,))]
    return _Exchange(arrs, shapes, scratch, start, finish)


def _add_halves(mine, theirs, name):
    _, R, C = mine.shape
    cap = max(16, ((2 * 1024 * 1024) // (4 * C * 10)) // 16 * 16)
    T = R if R <= cap else _tile(R, cap, 16)

    def body(a_ref, b_ref, o_ref):
        o_ref[...] = (a_ref[...] + b_ref[...].astype(F32)).astype(BF16)

    blk = pl.BlockSpec((N_CHIP, T, C), lambda i: (0, i, 0))
    return pl.pallas_call(
        body, name=name, grid=(R // T,), in_specs=[blk, blk], out_specs=blk,
        out_shape=jax.ShapeDtypeStruct(mine.shape, BF16), compiler_params=_cp(("parallel",)),
    )(mine, theirs)


def _adamw(parts, w, m, v, name):
    R, C = w.shape
    npart = parts.shape[0]
    cap = max(16, ((2 * 1024 * 1024) // (4 * C * 12)) // 16 * 16)
    T = R if R <= cap else _tile(R, cap, 16)

    def body(p_ref, w_ref, m_ref, v_ref, g_ref, d_ref, nm_ref, nv_ref):
        g = p_ref[0].astype(F32)
        for k in range(1, npart):
            g = g + p_ref[k].astype(F32)
        mm = ADAM_B1 * m_ref[...] + (1.0 - ADAM_B1) * g
        vv = ADAM_B2 * v_ref[...] + (1.0 - ADAM_B2) * (g * g)
        m_hat = mm / (1.0 - ADAM_B1 ** ADAM_STEP)
        v_hat = vv / (1.0 - ADAM_B2 ** ADAM_STEP)
        g_ref[...] = g
        d_ref[...] = -ADAM_LR * (m_hat / (jnp.sqrt(v_hat) + ADAM_EPS) + ADAM_WD * w_ref[...])
        nm_ref[...] = mm
        nv_ref[...] = vv

    row = pl.BlockSpec((T, C), lambda i: (i, 0))
    out = jax.ShapeDtypeStruct((R, C), F32)
    return pl.pallas_call(
        body, name=name, grid=(R // T,),
        in_specs=[pl.BlockSpec((npart, T, C), lambda i: (0, i, 0)), row, row, row],
        out_specs=[row] * 4, out_shape=[out] * 4,
        compiler_params=_cp(("parallel",)),
    )(parts, w, m, v)


SMALL = ("attn_pre_norm", "gdn_A_log", "gdn_dt_bias", "gdn_norm_w", "sb_norm_w", "attn_post_norm",
         "ffn_pre_norm", "ffn_conv_b", "ffn_post_norm")


def _pack_small(arrs):
    rows = []
    for a in arrs:
        flat = a.reshape(-1).astype(F32)
        n = -(-flat.shape[0] // 128) * 128
        rows.append(jnp.pad(flat, (0, n - flat.shape[0])).reshape(-1, 128))
    slab = jnp.concatenate(rows, axis=0)
    pad = (-slab.shape[0]) % 8
    return jnp.pad(slab, ((0, pad), (0, 0)))


def _unpack_small(slab, shapes):
    out, r = [], 0
    for shp in shapes:
        size = 1
        for s in shp:
            size *= s
        nr = -(-size // 128)
        out.append(slab[r:r + nr].reshape(-1)[:size].reshape(shp))
        r += nr
    return out


def _to_blocks_cols(a):
    R, C = a.shape
    return a.reshape(R, N_DEV, C // N_DEV).transpose(1, 0, 2)


def _from_blocks_cols(a):
    n, R, c = a.shape
    return a.transpose(1, 0, 2).reshape(R, n * c)


def kernel(x, meta_tokens, attn_pre_norm, w_in, gdn_conv_w, gdn_A_log, gdn_dt_bias, gdn_norm_w, sb_norm_w, w_out, attn_post_norm, ffn_pre_norm, w_ffn_up, ffn_conv_w, ffn_conv_b, w_ffn_down, ffn_post_norm, loss_target, m_meta_tokens, m_attn_pre_norm, m_w_in, m_gdn_conv_w, m_gdn_A_log, m_gdn_dt_bias, m_gdn_norm_w, m_sb_norm_w, m_w_out, m_attn_post_norm, m_ffn_pre_norm, m_w_ffn_up, m_ffn_conv_w, m_ffn_conv_b, m_w_ffn_down, m_ffn_post_norm, v_meta_tokens, v_attn_pre_norm, v_w_in, v_gdn_conv_w, v_gdn_A_log, v_gdn_dt_bias, v_gdn_norm_w, v_sb_norm_w, v_w_out, v_attn_post_norm, v_ffn_pre_norm, v_w_ffn_up, v_ffn_conv_w, v_ffn_conv_b, v_w_ffn_down, v_ffn_post_norm):
    args = dict(locals())
    seq = x.shape[1]
    LP = -(-(ROW0 + seq) // LP_ALIGN) * LP_ALIGN
    tail = LP - ROW0 - seq

    meta_f = _from_blocks_cols(_run_exchange(_gather_two_level([meta_tokens]), "gather_meta")[0])

    h0 = jnp.concatenate([jnp.zeros((PAD_ROWS, D_MODEL), F32), meta_f, x[0], jnp.zeros((tail, D_MODEL), F32)], axis=0)
    target = jnp.concatenate([jnp.zeros((ROW0, D_MODEL), F32), loss_target[0], jnp.zeros((tail, D_MODEL), F32)], axis=0)
    (u,), got = _prenorm_fwd(h0, attn_pre_norm, carry=_gather_two_level([w_in[0].astype(BF16), gdn_conv_w[0]]))
    win_o = _from_blocks_cols(got[0])
    o_ab = C_QKV
    o_z = o_ab + 2 * GDN_HEADS
    w_inp = jnp.concatenate([win_o[:, :C_QKV], win_o[:, o_z:o_z + C_Z], win_o[:, o_z + C_Z:],
                             win_o[:, o_ab:o_z], jnp.zeros((D_MODEL, C_AB - 2 * GDN_HEADS), BF16)], axis=1)
    gconv_f = _from_blocks_cols(got[1])
    proj = _mm(u, w_inp, F32, "mm_in")
    (qn, kn, vg, beta_b, g_b), got = _gdn_pre_fwd(
        proj, gconv_f, gdn_A_log, gdn_dt_bias,
        carry=_gather_two_level([w_out[0].astype(BF16), w_ffn_down[0].astype(BF16)]))
    w_out_f = got[0].reshape(D_MODEL, D_MODEL)
    w_down_f = got[1].reshape(D_FF, D_MODEL)
    (cu, cw, cqd, ckd, cqk, ct, cgl), got = _gdn_chunk_fwd(
        qn, kn, vg, beta_b, g_b, carry=_gather_two_level([w_ffn_up[0].astype(BF16), ffn_conv_w[0]]))
    w_up_f = _from_blocks_cols(got[0])
    fconv_f = _from_blocks_cols(got[1])
    og, ssave = _gdn_scan_fwd(cu, cw, cqd, ckd, cqk, cgl)
    osb, ctot, sb_nrun = _sb_fwd(proj)
    snw = sb_norm_w.reshape(1, SB_HEADS * SB_DH)
    y = _attn_norm_fwd(og, proj, osb, gdn_norm_w, snw)
    mix = _mm(y, w_out_f, F32, "mm_out")
    h1, n2 = _resid_fwd(h0, mix, attn_post_norm, ffn_pre_norm)
    up = _mm(n2, w_up_f, F32, "mm_up")
    act = _convglu_fwd(up, fconv_f, ffn_conv_b)
    f = _mm(act, w_down_f, F32, "mm_down")
    loss_part, dout, df, d_fpost = _final(h1, f, ffn_post_norm, target, seq)
    loss = lax.psum(loss_part[0, 0], ("x", "y", "c"))

    d_wdown = _mm_tn(act, df, "mm_dw_down")
    dact = _mm_nt(df, w_down_f, F32, "mm_dact")
    dup, d_fconv, d_fconvb = _convglu_bwd(up, fconv_f, ffn_conv_b, dact)
    d_wup = _mm_tn(n2, dup, "mm_dw_up")
    dn2 = _mm_nt(dup, w_up_f, F32, "mm_dn2")
    dh1, dmix, d_fpre, d_apost = _resid_bwd(h1, mix, attn_post_norm, ffn_pre_norm, dout, dn2)
    d_wout = _mm_tn(y, dmix, "mm_dw_out")
    dy = _mm_nt(dmix, w_out_f, F32, "mm_dy")
    my_c = lax.axis_index("c")

    def core_halves(blocks):
        halves = [s.reshape((N_CHIP, 2) + s.shape[1:]) for s in blocks]
        return ([lax.dynamic_index_in_dim(h, my_c, axis=1, keepdims=False) for h in halves],
                [lax.dynamic_index_in_dim(h, 1 - my_c, axis=1, keepdims=False).astype(BF16) for h in halves])

    early_names = ("w_out", "w_ffn_up", "w_ffn_down", "ffn_conv_w")
    e_mine, e_send = core_halves([d_wout.reshape(N_DEV, D_MODEL // N_DEV, D_MODEL), _to_blocks_cols(d_wup),
                                  d_wdown.reshape(N_DEV, D_FF // N_DEV, D_MODEL), _to_blocks_cols(d_fconv)])
    (dog, dz, dos, d_gnw, d_snw), e_theirs = _attn_norm_bwd(og, proj, osb, gdn_norm_w, snw, dy,
                                                            carry=_swap_sibling(e_send))
    e_sums = [_add_halves(a, b, "grads_add_" + nm) for nm, a, b in zip(early_names, e_mine, e_theirs)]
    dqs, dks, dvs = _sb_bwd(proj, ctot, sb_nrun, dos)
    (du_, dw_, dqd_, dkd_, dqk_, dgl_), _ = _gdn_scan_bwd(cu, cw, cqd, ckd, cqk, cgl, ssave, dog)
    dqn, dkn, dvg, dbeta, dg = _gdn_chunk_bwd(qn, kn, vg, beta_b, g_b, ct, du_, dw_, dqd_, dkd_, dqk_, dgl_)
    (dqkv, dab, d_gconv, d_gsc), e_recv = _gdn_pre_bwd(proj, gconv_f, gdn_A_log, gdn_dt_bias, dqn, dkn, dvg, dbeta, dg,
                                                       carry=_exchange_chips(e_sums))
    dpieces = [dqkv, dz, dqs, dks, dvs, dab]
    doffs = [0, OFF_Z, OFF_SB, OFF_SB + 512, OFF_SB + 1024, OFF_AB]
    dw_qkv, dw_ab = _mm_tn_pieces(u, [dqkv, dab], "mm_dw_in_gdn")
    dw_z, dw_qs, dw_ks, dw_vs = _mm_tn_pieces(u, [dz, dqs, dks, dvs], "mm_dw_in_rest")
    du0 = _mm_nt_pieces(dpieces, doffs, w_inp, F32, "mm_du")
    d_win = jnp.concatenate([dw_qkv, dw_ab[:, :2 * GDN_HEADS], dw_z, dw_qs, dw_ks, dw_vs], axis=1)
    late_names = ("w_in", "gdn_conv_w")
    l_mine, l_send = core_halves([_to_blocks_cols(d_win), _to_blocks_cols(d_gconv)])
    l_theirs = _run_exchange(_swap_sibling(l_send), "grads_swap_sibling")
    l_sums = [_add_halves(a, b, "grads_add_" + nm) for nm, a, b in zip(late_names, l_mine, l_theirs)]
    (dh0, d_apre), l_recv = _prenorm_bwd(h0, attn_pre_norm, du0, dh1, carry=_exchange_chips(l_sums))
    grad_x = dh0[ROW0:ROW0 + seq][None]
    d_meta = dh0[PAD_ROWS:ROW0]

    small_grads = [d_apre, d_gsc[0:1, :GDN_HEADS], d_gsc[1:2, :GDN_HEADS], d_gnw, d_snw.reshape(1, SB_HEADS, SB_DH),
                   d_apost, d_fpre, d_fconvb, d_fpost]
    n_small_rows = _pack_small(small_grads).shape[0]
    slab_parts = _gather_direct([jnp.concatenate([_pack_small(small_grads), d_meta.reshape(-1, LANE)], axis=0)],
                                name="gather_small_grads")[0]
    me = 4 * lax.axis_index("x") + 2 * lax.axis_index("y") + my_c
    meta_parts = lax.dynamic_index_in_dim(
        slab_parts[:, n_small_rows:].reshape(N_DEV, N_META, N_DEV, LANE), me, axis=2, keepdims=False)
    slab_parts = slab_parts[:, :n_small_rows]

    res = {}
    for nm, parts in zip(early_names + late_names + ("meta_tokens",), list(e_recv) + list(l_recv) + [meta_parts]):
        wloc = args[nm]
        shp = wloc.shape
        w2 = wloc.reshape(shp[-2], shp[-1])
        outs = _adamw(parts, w2, args["m_" + nm].reshape(w2.shape), args["v_" + nm].reshape(w2.shape), "adamw_" + nm)
        res[nm] = [o.reshape(shp) for o in outs]
    small_shapes = [args[nm].shape for nm in SMALL]
    outs = _adamw(slab_parts, _pack_small([args[nm] for nm in SMALL]), _pack_small([args["m_" + nm] for nm in SMALL]),
                  _pack_small([args["v_" + nm] for nm in SMALL]), "adamw_small")
    for k in range(4):
        for nm, val in zip(SMALL, _unpack_small(outs[k], small_shapes)):
            res.setdefault(nm, [None] * 4)[k] = val

    order = ("meta_tokens", "attn_pre_norm", "w_in", "gdn_conv_w", "gdn_A_log", "gdn_dt_bias", "gdn_norm_w",
             "sb_norm_w", "w_out", "attn_post_norm", "ffn_pre_norm", "w_ffn_up", "ffn_conv_w", "ffn_conv_b",
             "w_ffn_down", "ffn_post_norm")
    return (loss, grad_x, *[res[nm][0] for nm in order], *[res[nm][1] for nm in order],
            *[res[nm][2] for nm in order], *[res[nm][3] for nm in order])
```

```python
import functools

import jax
import jax.numpy as jnp
from jax import lax
from jax.experimental import pallas as pl
from jax.experimental.pallas import tpu as pltpu

F32 = jnp.float32
BF16 = jnp.bfloat16

D_MODEL = 1024
N_META = 16
GDN_HEADS = 4
GDN_D = 128
GDN_CHUNK = 64
GDN_CONV = 4
GDN_ROWS = 256
SCAN_CHUNKS = 4
SB_HEADS = 8
SB_DH = 64
SB_BLOCK = 128
D_FF = 2816
FFN_CONV = 3
NORM_EPS = 1e-6
L2_EPS = 1e-6
LANE = 128
N_DEV = 8

PAD_ROWS = SB_BLOCK - N_META
ROW0 = SB_BLOCK
SB_SPAN = 512
SB_DEAD = -104.0
SB_SUB = 256
SB_QTILE = 256
LP_ALIGN = 256

C_QKV = 3 * GDN_HEADS * GDN_D
C_Z = GDN_HEADS * GDN_D
C_SB = 3 * SB_HEADS * SB_DH
C_AB = 256
OFF_Z = C_QKV
OFF_SB = OFF_Z + C_Z
OFF_AB = OFF_SB + C_SB
D_INP = OFF_AB + C_AB
D_IN = C_QKV + 2 * GDN_HEADS + C_Z + C_SB

ADAM_LR = 0.001
ADAM_B1 = 0.9
ADAM_B2 = 0.999
ADAM_EPS = 1e-08
ADAM_WD = 0.01
ADAM_STEP = 10

VMEM_LIMIT = 56 * 1024 * 1024
ELEMWISE_VMEM = 8 * 1024 * 1024
MESH = pl.DeviceIdType.MESH


def _cp(sem=None):
    kw = dict(vmem_limit_bytes=VMEM_LIMIT)
    if sem is not None:
        kw["dimension_semantics"] = sem
    return pltpu.CompilerParams(**kw)


def _tile(n, cap, unit=128):
    best = None
    t = unit
    while t <= min(n, cap):
        if n % t == 0:
            best = t
        t += unit
    assert best is not None, (n, cap, unit)
    return best


def _dot(a, b):
    return jnp.dot(a, b, preferred_element_type=F32)


def _dot_nt(a, b):
    return lax.dot_general(a, b, (((1,), (1,)), ((), ())), preferred_element_type=F32)


def _dot_tn(a, b):
    return lax.dot_general(a, b, (((0,), (0,)), ((), ())), preferred_element_type=F32)


def _split(x):
    hi = x.astype(BF16)
    lo = (x - hi.astype(F32)).astype(BF16)
    return hi, lo


def _dot1(a, b, f=_dot):
    return f(a.astype(BF16), b.astype(BF16))


def _dot3(a, b, f=_dot):
    ah, al = _split(a)
    bh, bl = _split(b)
    return f(ah, bh) + (f(ah, bl) + f(al, bh))


def _dot_exact_l(m_bf16, x, f=_dot):
    xh, xl = _split(x)
    return f(m_bf16, xh) + f(m_bf16, xl)


def _dot_exact_r(x, m_bf16, f=_dot):
    xh, xl = _split(x)
    return f(xh, m_bf16) + f(xl, m_bf16)


def _iota2(shape, dim):
    return lax.broadcasted_iota(jnp.int32, shape, dim)


def _sigmoid(x):
    return 1.0 / (1.0 + jnp.exp(-x))


def _softplus(x):
    return jnp.maximum(x, 0.0) + jnp.log(1.0 + jnp.exp(-jnp.abs(x)))


def _colsum(x):
    return jnp.sum(x, axis=0, keepdims=True)


def _rowsum(x):
    return jnp.sum(x, axis=-1, keepdims=True)


def _mm(a, b, out_dtype, name):
    M, K = a.shape
    K2, N = b.shape
    assert K == K2
    tm = _tile(M, 768)
    tn = _tile(N, max(128, (6 * 1024 * 1024) // (2 * K)))

    def body(a_ref, b_ref, o_ref):
        o_ref[...] = _dot(a_ref[...].astype(BF16), b_ref[...].astype(BF16)).astype(o_ref.dtype)

    return pl.pallas_call(
        body, name=name, grid=(N // tn, M // tm),
        in_specs=[pl.BlockSpec((tm, K), lambda j, i: (i, 0)), pl.BlockSpec((K, tn), lambda j, i: (0, j))],
        out_specs=pl.BlockSpec((tm, tn), lambda j, i: (i, j)),
        out_shape=jax.ShapeDtypeStruct((M, N), out_dtype),
        compiler_params=_cp(("parallel", "parallel")),
    )(a, b)


def _mm_nt(a, b, out_dtype, name):
    M, K = a.shape
    N, K2 = b.shape
    assert K == K2
    tm = _tile(M, 768)
    tn = _tile(N, max(128, (6 * 1024 * 1024) // (2 * K)))

    def body(a_ref, b_ref, o_ref):
        o_ref[...] = _dot_nt(a_ref[...].astype(BF16), b_ref[...].astype(BF16)).astype(o_ref.dtype)

    return pl.pallas_call(
        body, name=name, grid=(N // tn, M // tm),
        in_specs=[pl.BlockSpec((tm, K), lambda j, i: (i, 0)), pl.BlockSpec((tn, K), lambda j, i: (j, 0))],
        out_specs=pl.BlockSpec((tm, tn), lambda j, i: (i, j)),
        out_shape=jax.ShapeDtypeStruct((M, N), out_dtype),
        compiler_params=_cp(("parallel", "parallel")),
    )(a, b)


def _mm_nt_pieces(pieces, offsets, b, out_dtype, name):
    M = pieces[0].shape[0]
    N = b.shape[0]
    n = len(pieces)
    widths = [p.shape[1] for p in pieces]
    assert all(off % k == 0 for off, k in zip(offsets, widths))
    tm = _tile(M, 768)
    tn = _tile(N, 512)

    def body(*refs):
        acc = _dot_nt(refs[0][...].astype(BF16), refs[n][...].astype(BF16))
        for p in range(1, n):
            acc = acc + _dot_nt(refs[p][...].astype(BF16), refs[n + p][...].astype(BF16))
        refs[2 * n][...] = acc.astype(out_dtype)

    return pl.pallas_call(
        body, name=name, grid=(N // tn, M // tm),
        in_specs=[pl.BlockSpec((tm, k), lambda j, i: (i, 0)) for k in widths]
        + [pl.BlockSpec((tn, k), functools.partial(lambda j, i, blk: (j, blk), blk=off // k))
           for off, k in zip(offsets, widths)],
        out_specs=pl.BlockSpec((tm, tn), lambda j, i: (i, j)),
        out_shape=jax.ShapeDtypeStruct((M, N), out_dtype),
        compiler_params=_cp(("parallel", "parallel")),
    )(*pieces, *([b] * n))


def _mm_tn_pieces(a, pieces, name):
    M, K = a.shape
    n = len(pieces)
    tm = _tile(M, 768)

    def body(*refs):
        @pl.when(pl.program_id(0) == 0)
        def _():
            for p in range(n):
                refs[1 + n + p][...] = jnp.zeros_like(refs[1 + n + p])
        at = refs[0][...].astype(BF16)
        for p in range(n):
            refs[1 + n + p][...] += _dot_tn(at, refs[1 + p][...].astype(BF16))

    return pl.pallas_call(
        body, name=name, grid=(M // tm,),
        in_specs=[pl.BlockSpec((tm, K), lambda m: (m, 0))] + [pl.BlockSpec((tm, p.shape[1]), lambda m: (m, 0)) for p in pieces],
        out_specs=[pl.BlockSpec((K, p.shape[1]), lambda m: (0, 0)) for p in pieces],
        out_shape=[jax.ShapeDtypeStruct((K, p.shape[1]), F32) for p in pieces],
        compiler_params=_cp(("arbitrary",)),
    )(a, *pieces)


def _mm_tn(a, b, name):
    M, K = a.shape
    M2, N = b.shape
    assert M == M2
    tm = _tile(M, 1408)
    tk = _tile(K, 1408)
    tn = _tile(N, 1408)

    def body(a_ref, b_ref, o_ref):
        @pl.when(pl.program_id(2) == 0)
        def _():
            o_ref[...] = jnp.zeros_like(o_ref)
        o_ref[...] += _dot_tn(a_ref[...].astype(BF16), b_ref[...].astype(BF16))

    return pl.pallas_call(
        body, name=name, grid=(K // tk, N // tn, M // tm),
        in_specs=[pl.BlockSpec((tm, tk), lambda i, j, m: (m, i)), pl.BlockSpec((tm, tn), lambda i, j, m: (m, j))],
        out_specs=pl.BlockSpec((tk, tn), lambda i, j, m: (i, j)),
        out_shape=jax.ShapeDtypeStruct((K, N), F32),
        compiler_params=_cp(("parallel", "parallel", "arbitrary")),
    )(a, b)


def _rms(x):
    return lax.rsqrt(jnp.mean(x * x, axis=-1, keepdims=True) + NORM_EPS)


def _rms_bwd(x, w, dy):
    r = _rms(x)
    n = x * r
    dyw = dy * w
    dx = r * (dyw - n * jnp.mean(dyw * n, axis=-1, keepdims=True))
    return dx, dy * n


def _build_rows(x, meta, target, LP, carry=None):
    seq, D = x.shape
    T = SB_BLOCK
    nx = seq // T
    assert seq % T == 0 and meta.shape[0] == N_META

    def body(x_ref, m_ref, t_ref, h_ref, tp_ref):
        i = pl.program_id(0)
        inside = (i >= 1) & (i <= nx)
        head = jnp.concatenate([jnp.zeros((PAD_ROWS, D), F32), m_ref[...]], axis=0)
        h_ref[...] = jnp.where(i == 0, head, jnp.where(inside, x_ref[...], 0.0))
        tp_ref[...] = jnp.where(inside, t_ref[...], 0.0)

    tok = pl.BlockSpec((T, D), lambda i: (jnp.clip(i - 1, 0, nx - 1), 0))
    row = pl.BlockSpec((T, D), lambda i: (i, 0))
    out = jax.ShapeDtypeStruct((LP, D), F32)
    return _call_carrying(
        carry, body, LP // T, name="build_rows",
        in_specs=[tok, pl.BlockSpec((N_META, D), lambda i: (0, 0)), tok], out_specs=[row, row], out_shape=[out, out],
        operands=(x, meta, target))


def _prenorm_fwd(h0, w, carry=None):
    LP, D = h0.shape
    T = _tile(LP, 512)

    def body(h_ref, w_ref, u_ref):
        h = h_ref[...]
        u_ref[...] = (h * _rms(h) * w_ref[...]).astype(BF16)

    return _call_carrying(
        carry, body, LP // T, name="prenorm_fwd",
        in_specs=[pl.BlockSpec((T, D), lambda i: (i, 0)), pl.BlockSpec((1, D), lambda i: (0, 0))],
        out_specs=[pl.BlockSpec((T, D), lambda i: (i, 0))],
        out_shape=[jax.ShapeDtypeStruct((LP, D), BF16)],
        operands=(h0, w))


def _prenorm_bwd(h0, w, du, dh1, carry=None):
    LP, D = h0.shape
    T = _tile(LP, 512)

    def body(h_ref, w_ref, du_ref, dh1_ref, dh0_ref, dw_ref):
        @pl.when(pl.program_id(0) == 0)
        def _():
            dw_ref[...] = jnp.zeros_like(dw_ref)
        dx, dwn = _rms_bwd(h_ref[...], w_ref[...], du_ref[...])
        dh0_ref[...] = dh1_ref[...] + dx
        dw_ref[...] += _colsum(dwn)

    row = pl.BlockSpec((T, D), lambda i: (i, 0))
    vec = pl.BlockSpec((1, D), lambda i: (0, 0))
    return _call_carrying(
        carry, body, LP // T, name="prenorm_bwd",
        in_specs=[row, vec, row, row], out_specs=[row, vec],
        out_shape=[jax.ShapeDtypeStruct((LP, D), F32), jax.ShapeDtypeStruct((1, D), F32)],
        operands=(h0, w, du, dh1))


def _causal_taps(ext, w_ref, width, start, rows):
    y = w_ref[width - 1:width, :] * ext[start:start + rows]
    for j in range(width - 1):
        y = y + w_ref[j:j + 1, :] * pltpu.roll(ext, width - 1 - j, 0)[start:start + rows]
    return y


def _shifted_rows(ext, shift, start, rows):
    return ext[start:start + rows] if shift == 0 else pltpu.roll(ext, shift, 0)[start:start + rows]


def _anticausal_taps(dy_ext, w_ref, width, rows):
    n = dy_ext.shape[0]
    dx = w_ref[width - 1:width, :] * dy_ext[0:rows]
    for j in range(width - 1):
        dx = dx + w_ref[j:j + 1, :] * pltpu.roll(dy_ext, n - (width - 1 - j), 0)[0:rows]
    return dx


def _gdn_gate_consts(alog_ref, dtb_ref, h):
    a_coef = -jnp.exp(alog_ref[0:1, h:h + 1])
    return a_coef, dtb_ref[0:1, h:h + 1]


def _gdn_pre_fwd(proj, conv_w, a_log, dt_bias, carry=None):
    LP = proj.shape[0]
    T = _tile(LP, 256)
    C = C_QKV
    H = GDN_HEADS

    def body(x_ref, halo_ref, ab_ref, cw_ref, alog_ref, dtb_ref, q_ref, k_ref, v_ref, beta_ref, g_ref):
        i = pl.program_id(0)
        ext = jnp.concatenate([jnp.where(i > 0, halo_ref[...], 0.0), x_ref[...]], axis=0)
        y = _causal_taps(ext, cw_ref, GDN_CONV, 8, T)
        c = y * _sigmoid(y)
        for h in range(H):
            sl = slice(h * GDN_D, (h + 1) * GDN_D)
            cq = c[:, sl]
            q_ref[:, sl] = cq * lax.rsqrt(_rowsum(cq * cq) + L2_EPS) * (GDN_D ** -0.5)
            ck = c[:, 512 + h * GDN_D:512 + (h + 1) * GDN_D]
            k_ref[:, sl] = ck * lax.rsqrt(_rowsum(ck * ck) + L2_EPS)
        v_ref[...] = c[:, 1024:]
        ab = ab_ref[...]
        valid = (i * T + _iota2((T, 1), 0)) >= PAD_ROWS
        for h in range(H):
            sl = slice(h * GDN_D, (h + 1) * GDN_D)
            a_coef, dtb = _gdn_gate_consts(alog_ref, dtb_ref, h)
            g = jnp.where(valid, a_coef * _softplus(ab[:, h:h + 1] + dtb), 0.0)
            beta = jnp.where(valid, _sigmoid(ab[:, H + h:H + h + 1]), 0.0)
            g_ref[:, sl] = jnp.broadcast_to(g, (T, GDN_D))
            beta_ref[:, sl] = jnp.broadcast_to(beta, (T, GDN_D))

    t8 = T // 8
    row512 = pl.BlockSpec((T, 512), lambda i: (i, 0))
    small = lambda r, c: pl.BlockSpec((r, c), lambda i: (0, 0))
    out = jax.ShapeDtypeStruct((LP, 512), F32)
    return _call_carrying(
        carry, body, LP // T, name="gdn_pre_fwd",
        in_specs=[pl.BlockSpec((T, C), lambda i: (i, 0)),
                  pl.BlockSpec((8, C), lambda i: (jnp.maximum(i * t8 - 1, 0), 0)),
                  pl.BlockSpec((T, C_AB), lambda i: (i, OFF_AB // C_AB)),
                  small(GDN_CONV, C), small(1, H), small(1, H)],
        out_specs=[row512] * 5, out_shape=[out] * 5,
        operands=(proj, proj, proj, conv_w, a_log, dt_bias))


def _gdn_pre_bwd(proj, conv_w, a_log, dt_bias, dq, dk, dv, dbeta, dg, carry=None):
    LP = proj.shape[0]
    T = _tile(LP, 256)
    C = C_QKV
    H = GDN_HEADS
    TE = T + 8
    nt = LP // T

    def body(x_ref, xp_ref, xn_ref, ab_ref, cw_ref, alog_ref, dtb_ref,
             dq_ref, dqn_ref, dk_ref, dkn_ref, dv_ref, dvn_ref, dbeta_ref, dg_ref,
             dx_ref, dab_ref, dcw_ref, dsc_ref, dys):
        i = pl.program_id(0)

        @pl.when(i == 0)
        def _():
            dcw_ref[...] = jnp.zeros_like(dcw_ref)
            dsc_ref[...] = jnp.zeros_like(dsc_ref)

        last = i == nt - 1
        ext = jnp.concatenate([jnp.where(i > 0, xp_ref[...], 0.0), x_ref[...], jnp.where(last, 0.0, xn_ref[...])],
                              axis=0)
        y = _causal_taps(ext, cw_ref, GDN_CONV, 8, TE)
        sg = _sigmoid(y)
        c = y * sg
        nxt = lambda a_ref, b_ref: jnp.concatenate([a_ref[...], jnp.where(last, 0.0, b_ref[...])], axis=0)
        dqn = nxt(dq_ref, dqn_ref)
        dkn = nxt(dk_ref, dkn_ref)
        dvv = nxt(dv_ref, dvn_ref)
        for h in range(H):
            sl = slice(h * GDN_D, (h + 1) * GDN_D)
            cq = c[:, sl]
            rq = lax.rsqrt(_rowsum(cq * cq) + L2_EPS)
            nq = cq * rq
            dqh = dqn[:, sl]
            dys[:, sl] = (GDN_D ** -0.5) * rq * (dqh - nq * _rowsum(dqh * nq))
            sk = slice(512 + h * GDN_D, 512 + (h + 1) * GDN_D)
            ck = c[:, sk]
            rk = lax.rsqrt(_rowsum(ck * ck) + L2_EPS)
            nk = ck * rk
            dkh = dkn[:, sl]
            dys[:, sk] = rk * (dkh - nk * _rowsum(dkh * nk))
        dys[:, 1024:] = dvv
        dy = dys[...] * (sg * (1.0 + y * (1.0 - sg)))
        for j in range(GDN_CONV):
            dcw_ref[j:j + 1, :] += _colsum(dy[0:T, :] * _shifted_rows(ext, GDN_CONV - 1 - j, 8, T))
        dx_ref[...] = _anticausal_taps(dy, cw_ref, GDN_CONV, T).astype(BF16)
        ab = ab_ref[...]
        valid = (i * T + _iota2((T, 1), 0)) >= PAD_ROWS
        lane = _iota2((T, C_AB), 1)
        lane1 = _iota2((1, 128), 1)
        dab = jnp.zeros((T, C_AB), F32)
        dsc_a = jnp.zeros((1, 128), F32)
        dsc_d = jnp.zeros((1, 128), F32)
        for h in range(H):
            a_coef, dtb = _gdn_gate_consts(alog_ref, dtb_ref, h)
            pre = ab[:, h:h + 1] + dtb
            dgh = jnp.where(valid, dg_ref[:, h * GDN_D:h * GDN_D + 1], 0.0)
            da = dgh * a_coef * _sigmoid(pre)
            beta = _sigmoid(ab[:, H + h:H + h + 1])
            db = jnp.where(valid, dbeta_ref[:, h * GDN_D:h * GDN_D + 1], 0.0) * beta * (1.0 - beta)
            dab = dab + jnp.where(lane == h, da, 0.0) + jnp.where(lane == H + h, db, 0.0)
            dsc_a = dsc_a + jnp.where(lane1 == h, _colsum(dgh * a_coef * _softplus(pre)), 0.0)
            dsc_d = dsc_d + jnp.where(lane1 == h, _colsum(da), 0.0)
        dab_ref[...] = dab.astype(BF16)
        dsc_ref[0:1, :] += dsc_a
        dsc_ref[1:2, :] += dsc_d

    t8 = T // 8
    nb8 = LP // 8
    prev8 = lambda w: pl.BlockSpec((8, w), lambda i: (jnp.maximum(i * t8 - 1, 0), 0))
    next8 = lambda w: pl.BlockSpec((8, w), lambda i: (jnp.minimum((i + 1) * t8, nb8 - 1), 0))
    row = lambda w: pl.BlockSpec((T, w), lambda i: (i, 0))
    small = lambda r, c: pl.BlockSpec((r, c), lambda i: (0, 0))
    return _call_carrying(
        carry, body, nt, name="gdn_pre_bwd",
        in_specs=[row(C), prev8(C), next8(C), pl.BlockSpec((T, C_AB), lambda i: (i, OFF_AB // C_AB)),
                  small(GDN_CONV, C), small(1, H), small(1, H),
                  row(512), next8(512), row(512), next8(512), row(512), next8(512), row(512), row(512)],
        out_specs=[row(C), row(C_AB), small(GDN_CONV, C), small(2, 128)],
        out_shape=[jax.ShapeDtypeStruct((LP, C), BF16), jax.ShapeDtypeStruct((LP, C_AB), BF16),
                   jax.ShapeDtypeStruct((GDN_CONV, C), F32), jax.ShapeDtypeStruct((2, 128), F32)],
        scratch_shapes=[pltpu.VMEM((TE, C), F32)],
        operands=(proj, proj, proj, proj, conv_w, a_log, dt_bias, dq, dq, dk, dk, dv, dv, dbeta, dg))


def _tri_masks():
    r = _iota2((GDN_CHUNK, GDN_CHUNK), 0)
    c = _iota2((GDN_CHUNK, GDN_CHUNK), 1)
    return r >= c, r > c


def _gdn_chunk_common(q, k, v, beta, gb):
    incl, strict = _tri_masks()
    l_incl = incl.astype(BF16)
    gd = _dot_exact_l(l_incl, jnp.where(strict, gb[:, :GDN_CHUNK], 0.0))
    gc = _dot_exact_l(l_incl, gb)
    decay = jnp.where(incl, jnp.exp(jnp.where(incl, gd, 0.0)), 0.0)
    exp_g = jnp.exp(gc)
    g_last = gc[GDN_CHUNK - 1:GDN_CHUNK, :]
    kd_fac = jnp.exp(g_last - gc)
    gl = jnp.exp(g_last)
    kb = k * beta
    kk = _dot1(kb, k, _dot_nt)
    return dict(incl=incl, strict=strict, decay=decay, exp_g=exp_g, kd_fac=kd_fac, gl=gl, kb=kb, kk=kk,
                vb=v * beta, kbg=kb * exp_g)


def _interleave(gens):
    gens = list(gens)
    while gens:
        alive = []
        for g in gens:
            try:
                next(g)
                alive.append(g)
            except StopIteration:
                pass
        gens = alive


def _call_carrying(ex, body, nsteps, *, name, in_specs, out_specs, out_shape, operands, scratch_shapes=()):
    n_in, n_out, n_scr = len(in_specs), len(out_specs), len(scratch_shapes)
    n = ex.n if ex is not None else 0

    def full(*refs):
        o0 = n_in + n
        s0 = o0 + n_out + n
        ex_refs = (refs[n_in:o0], refs[o0 + n_out:s0], refs[s0 + n_scr:])
        step = pl.program_id(0)
        _carry_begin(ex, ex_refs, step, nsteps)
        body(*refs[:n_in], *refs[o0:o0 + n_out], *refs[s0:s0 + n_scr])
        _carry_end(ex, ex_refs, step, nsteps)

    res = pl.pallas_call(
        full, name=name, grid=(nsteps,),
        in_specs=list(in_specs) + [ANY_SPEC] * n, out_specs=list(out_specs) + [ANY_SPEC] * n,
        out_shape=list(out_shape) + (ex.out_shapes if ex is not None else []),
        scratch_shapes=list(scratch_shapes) + (ex.scratch if ex is not None else []),
        compiler_params=pltpu.CompilerParams(dimension_semantics=("arbitrary",), vmem_limit_bytes=VMEM_LIMIT,
                                             has_side_effects=ex is not None),
    )(*operands, *(ex.arrs if ex is not None else []))
    return list(res[:n_out]), list(res[n_out:])


def _gdn_chunk_fwd(qn, kn, v, beta_b, g_b, carry=None):
    LP = qn.shape[0]
    R = GDN_ROWS
    H = GDN_HEADS
    CH = GDN_CHUNK

    def body(q_ref, k_ref, v_ref, b_ref, g_ref, u_ref, w_ref, qd_ref, kd_ref, qk_ref, t_ref, gl_ref):
        def item(cc, h):
            rs = slice(cc * CH, (cc + 1) * CH)
            sl = slice(h * GDN_D, (h + 1) * GDN_D)
            s64 = slice(h * CH, (h + 1) * CH)
            q, k = q_ref[rs, sl], k_ref[rs, sl]
            m = _gdn_chunk_common(q, k, v_ref[rs, sl], b_ref[rs, sl], g_ref[rs, sl])
            qk_raw = _dot1(q, k, _dot_nt)
            yield
            a = jnp.where(m["strict"], m["kk"] * m["decay"], 0.0)
            eye = (_iota2((CH, CH), 0) == _iota2((CH, CH), 1)).astype(F32)
            t = eye - a
            p = _dot3(a, a)
            yield
            for _ in range(4):
                t = t + _dot3(t, p)
                p = _dot3(p, p)
                yield
            t = t + _dot3(t, p)
            yield
            u_ref[rs, sl] = _dot1(t, m["vb"])
            w_ref[rs, sl] = _dot1(t, m["kbg"])
            qk_ref[rs, s64] = qk_raw * m["decay"]
            t_ref[rs, s64] = t
            qd_ref[rs, sl] = q * m["exp_g"]
            kd_ref[rs, sl] = k * m["kd_fac"]
            gl_ref[cc * 8:(cc + 1) * 8, sl] = jnp.broadcast_to(m["gl"], (8, GDN_D))

        _interleave(item(cc, h) for cc in range(R // CH) for h in range(H))

    row = lambda w: pl.BlockSpec((R, w), lambda i: (i, 0))
    o512 = jax.ShapeDtypeStruct((LP, 512), F32)
    o256 = jax.ShapeDtypeStruct((LP, 256), F32)
    return _call_carrying(
        carry, body, LP // R, name="gdn_chunk_fwd",
        in_specs=[row(512)] * 5,
        out_specs=[row(512)] * 4 + [row(256)] * 2 + [pl.BlockSpec((R // 8, 512), lambda i: (i, 0))],
        out_shape=[o512] * 4 + [o256] * 2 + [jax.ShapeDtypeStruct((LP // 8, 512), F32)],
        operands=(qn, kn, v, beta_b, g_b))


def _gdn_chunk_bwd(qn, kn, v, beta_b, g_b, t_all, du, dw, dqd, dkd, dqk, dgl):
    LP = qn.shape[0]
    R = GDN_ROWS
    H = GDN_HEADS
    CH = GDN_CHUNK

    def body(q_ref, k_ref, v_ref, b_ref, g_ref, t_ref, du_ref, dw_ref, dqd_ref, dkd_ref, dqk_ref, dgl_ref,
             dq_ref, dk_ref, dv_ref, db_ref, dg_ref):
        ones = jnp.ones((CH, GDN_D), BF16)

        def item(cc, h):
            rs = slice(cc * CH, (cc + 1) * CH)
            sl = slice(h * GDN_D, (h + 1) * GDN_D)
            s64 = slice(h * CH, (h + 1) * CH)
            q, k, vv, beta = q_ref[rs, sl], k_ref[rs, sl], v_ref[rs, sl], b_ref[rs, sl]
            m = _gdn_chunk_common(q, k, vv, beta, g_ref[rs, sl])
            incl, strict, decay = m["incl"], m["strict"], m["decay"]
            t = t_ref[rs, s64]
            du_, dw_ = du_ref[rs, sl], dw_ref[rs, sl]
            dqd_, dkd_ = dqd_ref[rs, sl], dkd_ref[rs, sl]
            d_t = _dot1(du_, m["vb"], _dot_nt) + _dot1(dw_, m["kbg"], _dot_nt)
            dvb = _dot1(t, du_, _dot_tn)
            dkbg = _dot1(t, dw_, _dot_tn)
            qk_raw = _dot1(q, k, _dot_nt)
            yield
            x1 = _dot3(d_t, t, _dot_nt)
            dkb = dkbg * m["exp_g"]
            d_gi = _rowsum(dkbg * m["kbg"])
            yield
            d_a = jnp.where(strict, -_dot3(t, x1, _dot_tn), 0.0)
            yield
            d_kk = d_a * decay
            dqk_m = jnp.where(incl, dqk_ref[rs, s64], 0.0)
            dqk_raw = dqk_m * decay
            mm = (d_a * m["kk"] + dqk_m * qk_raw) * decay
            dkb = dkb + _dot1(d_kk, k)
            dk_ = _dot1(d_kk, m["kb"], _dot_tn) + _dot1(dqk_raw, q, _dot_tn)
            dq_ = _dot1(dqk_raw, k) + dqd_ * m["exp_g"]
            d_gi = d_gi + (_dot_exact_r(mm, ones) - _dot_exact_r(mm, ones, _dot_tn))
            yield
            d_gi = d_gi + _rowsum(dqd_ * q * m["exp_g"])
            e = _rowsum(dkd_ * k * m["kd_fac"])
            d_gi = d_gi - e
            d_glast = _colsum(jnp.broadcast_to(e, (CH, GDN_D))) + dgl_ref[cc * 8:cc * 8 + 1, sl] * m["gl"]
            dk_ = dk_ + dkd_ * m["kd_fac"] + dkb * beta
            d_gi = d_gi + jnp.where(_iota2((CH, GDN_D), 0) == CH - 1, d_glast, 0.0)
            u_incl = (_iota2((CH, CH), 1) >= _iota2((CH, CH), 0)).astype(BF16)
            dq_ref[rs, sl] = dq_
            dk_ref[rs, sl] = dk_
            dv_ref[rs, sl] = dvb * beta
            db_ref[rs, sl] = jnp.broadcast_to(_rowsum(dvb * vv) + _rowsum(dkb * k), (CH, GDN_D))
            dg_ref[rs, sl] = _dot_exact_l(u_incl, d_gi)

        _interleave(item(cc, h) for cc in range(R // CH) for h in range(H))

    row = lambda w: pl.BlockSpec((R, w), lambda i: (i, 0))
    o512 = jax.ShapeDtypeStruct((LP, 512), F32)
    gl_spec = pl.BlockSpec((R // 8, 512), lambda i: (i, 0))
    return pl.pallas_call(
        body, name="gdn_chunk_bwd", grid=(LP // R,),
        in_specs=[row(512)] * 5 + [row(256)] + [row(512)] * 4 + [row(256), gl_spec],
        out_specs=[row(512)] * 5, out_shape=[o512] * 5,
        compiler_params=_cp(("parallel",)),
    )(qn, kn, v, beta_b, g_b, t_all, du, dw, dqd, dkd, dqk, dgl)


def _gdn_scan_fwd(u, w, qd, kd, qk, gl):
    LP = u.shape[0]
    CH = GDN_CHUNK
    CPS = SCAN_CHUNKS
    N = LP // CH
    NS = N // CPS
    H = GDN_HEADS

    def body(u_ref, w_ref, qd_ref, kd_ref, qk_ref, gl_ref, o_ref, ssave_ref, s_sc):
        @pl.when(pl.program_id(0) == 0)
        def _():
            s_sc[...] = jnp.zeros_like(s_sc)

        for cc in range(CPS):
            rs = slice(cc * CH, (cc + 1) * CH)
            ssave_ref[cc * GDN_D:(cc + 1) * GDN_D, :] = s_sc[...]

            def item(h):
                sl = slice(h * GDN_D, (h + 1) * GDN_D)
                s = s_sc[:, sl]
                v_new = u_ref[rs, sl] - _dot1(w_ref[rs, sl], s)
                o_s = _dot1(qd_ref[rs, sl], s)
                yield
                o_ref[rs, sl] = o_s + _dot1(qk_ref[rs, h * CH:(h + 1) * CH], v_new)
                s_sc[:, sl] = s * gl_ref[cc * 8:cc * 8 + 1, sl] + _dot1(kd_ref[rs, sl], v_new, _dot_tn)

            _interleave(item(h) for h in range(H))

    row = lambda w_: pl.BlockSpec((CPS * CH, w_), lambda n: (n, 0))
    return pl.pallas_call(
        body, name="gdn_scan_fwd", grid=(NS,),
        in_specs=[row(512)] * 4 + [row(256), pl.BlockSpec((CPS * 8, 512), lambda n: (n, 0))],
        out_specs=[row(512), pl.BlockSpec((CPS * GDN_D, 512), lambda n: (n, 0))],
        out_shape=[jax.ShapeDtypeStruct((LP, 512), F32), jax.ShapeDtypeStruct((N * GDN_D, 512), F32)],
        scratch_shapes=[pltpu.VMEM((GDN_D, 512), F32)],
        compiler_params=_cp(("arbitrary",)),
    )(u, w, qd, kd, qk, gl)


def _gdn_scan_bwd(u, w, qd, kd, qk, gl, ssave, do, carry=None):
    LP = u.shape[0]
    CH = GDN_CHUNK
    CPS = SCAN_CHUNKS
    N = LP // CH
    NS = N // CPS
    H = GDN_HEADS

    def body(u_ref, w_ref, qd_ref, kd_ref, qk_ref, gl_ref, s_ref, do_ref,
             du_ref, dw_ref, dqd_ref, dkd_ref, dqk_ref, dgl_ref, ds_sc):
        @pl.when(pl.program_id(0) == 0)
        def _():
            ds_sc[...] = jnp.zeros_like(ds_sc)

        for cc in reversed(range(CPS)):
            rs = slice(cc * CH, (cc + 1) * CH)
            r8 = slice(cc * 8, (cc + 1) * 8)

            def item(h):
                sl = slice(h * GDN_D, (h + 1) * GDN_D)
                s64 = slice(h * CH, (h + 1) * CH)
                s = s_ref[cc * GDN_D:(cc + 1) * GDN_D, sl]
                ds = ds_sc[:, sl]
                do_ = do_ref[rs, sl]
                w_, qd_, kd_, qk_ = w_ref[rs, sl], qd_ref[rs, sl], kd_ref[rs, sl], qk_ref[rs, s64]
                v_new = u_ref[rs, sl] - _dot1(w_, s)
                d_vnew = _dot1(qk_, do_, _dot_tn) + _dot1(kd_, ds)
                dqd_ref[rs, sl] = _dot1(do_, s, _dot_nt)
                ds_new = ds * gl_ref[cc * 8:cc * 8 + 1, sl] + _dot1(qd_, do_, _dot_tn)
                dgl_ref[r8, sl] = jnp.broadcast_to(jnp.sum(_colsum(ds * s), axis=-1, keepdims=True), (8, GDN_D))
                yield
                du_ref[rs, sl] = d_vnew
                dw_ref[rs, sl] = -_dot1(d_vnew, s, _dot_nt)
                dkd_ref[rs, sl] = _dot1(v_new, ds, _dot_nt)
                dqk_ref[rs, s64] = _dot1(do_, v_new, _dot_nt)
                ds_sc[:, sl] = ds_new - _dot1(w_, d_vnew, _dot_tn)

            _interleave(item(h) for h in range(H))

    rev = lambda w_: pl.BlockSpec((CPS * CH, w_), lambda n: (NS - 1 - n, 0))
    rev8 = pl.BlockSpec((CPS * 8, 512), lambda n: (NS - 1 - n, 0))
    o512 = jax.ShapeDtypeStruct((LP, 512), F32)
    return _call_carrying(
        carry, body, NS, name="gdn_scan_bwd",
        in_specs=[rev(512)] * 4 + [rev(256), rev8, pl.BlockSpec((CPS * GDN_D, 512), lambda n: (NS - 1 - n, 0)),
                  rev(512)],
        out_specs=[rev(512)] * 4 + [rev(256), rev8],
        out_shape=[o512] * 4 + [jax.ShapeDtypeStruct((LP, 256), F32), jax.ShapeDtypeStruct((LP // 8, 512), F32)],
        scratch_shapes=[pltpu.VMEM((GDN_D, 512), F32)],
        operands=(u, w, qd, kd, qk, gl, ssave, do))


def _sb_scores(qh, kblk, mask):
    z = _dot_nt(qh, kblk)
    e = jnp.exp(-jnp.abs(z))
    sp = jnp.maximum(z, 0.0) + jnp.log(1.0 + e)
    return z, e, jnp.where(mask, -sp, 0.0), z - sp


def _sb_fwd(proj):
    LP = proj.shape[0]
    B = SB_BLOCK
    W = min(SB_SPAN, LP)
    SUB = SB_SUB
    Q = min(SB_QTILE, LP)
    nq = LP // Q
    nsub = W // SUB
    scale = SB_DH ** -0.5
    qcol, kcol, vcol = OFF_SB // B, (OFF_SB + 512) // B, (OFF_SB + 1024) // B

    def body(q_ref, k_ref, v_ref, o_ref, c_ref, n_ref):
        i = pl.program_id(1)
        lane = _iota2((Q, B), 1)
        head_a = lane < SB_DH
        qs = q_ref[...] * scale
        qh = [jnp.where(head_a, qs, 0.0).astype(BF16), jnp.where(head_a, 0.0, qs).astype(BF16)]
        u_strict = (_iota2((SUB, SUB), 0) > _iota2((SUB, SUB), 1)).astype(BF16)
        qpos = i * Q + _iota2((Q, W), 0)
        hi0 = (i + 1) * Q
        nspan = (hi0 + W - 1) // W

        def live(st):
            return (st[0] < nspan) & (st[1] > 0)

        def span(st):
            r, carry = st[0], st[2:]
            hi = hi0 - r * W
            k0 = pl.multiple_of(jnp.maximum(hi - W, 0), B)
            kblk = k_ref[pl.ds(k0, W), :].astype(BF16)
            vblk = v_ref[pl.ds(k0, W), :].astype(BF16)
            kpos = k0 + _iota2((Q, W), 1)
            mask = (kpos < qpos) & (kpos >= PAD_ROWS) & (kpos < hi)
            new = [None] * 4

            def head(h):
                o_acc, c = carry[2 * h], carry[2 * h + 1]
                z, e, l1m, lsg = _sb_scores(qh[h], kblk, mask)
                yield
                subs = [slice(b * SUB, (b + 1) * SUB) for b in range(nsub)]
                suf = [_dot(l1m[:, bs].astype(BF16), u_strict) for bs in subs]
                yield
                parts = [None] * nsub
                for b in reversed(range(nsub)):
                    parts[b] = jnp.where(mask[:, subs[b]], jnp.exp(lsg[:, subs[b]] + suf[b] + c), 0.0)
                    c = c + _rowsum(l1m[:, subs[b]])
                att = jnp.concatenate(parts, axis=1).astype(BF16)
                new[2 * h], new[2 * h + 1] = o_acc + _dot(att, vblk), c

            _interleave(head(h) for h in range(2))
            more = (jnp.maximum(jnp.max(new[1]), jnp.max(new[3])) > SB_DEAD).astype(jnp.int32)
            return (r + 1, more, *new)

        zero_o = jnp.zeros((Q, B), F32)
        zero_c = jnp.zeros((Q, 1), F32)
        nrun, _, o_a, c_a, o_b, c_b = lax.while_loop(
            live, span, (jnp.int32(0), jnp.int32(1), zero_o, zero_c, zero_o, zero_c))
        o_ref[...] = jnp.where(head_a, o_a, o_b)
        c_ref[...] = jnp.where(head_a, c_a, c_b)
        n_ref[pl.program_id(0), i] = nrun

    blk = pl.BlockSpec((Q, B), lambda p, i: (i, p))
    out = jax.ShapeDtypeStruct((LP, 512), F32)
    return pl.pallas_call(
        body, name="sb_fwd", grid=(SB_HEADS // 2, nq),
        in_specs=[pl.BlockSpec((Q, B), lambda p, i: (i, qcol + p)),
                  pl.BlockSpec((LP, B), lambda p, i: (0, kcol + p)),
                  pl.BlockSpec((LP, B), lambda p, i: (0, vcol + p))],
        out_specs=[blk, blk, pl.BlockSpec(memory_space=pltpu.SMEM)],
        out_shape=[out, out, jax.ShapeDtypeStruct((SB_HEADS // 2, nq), jnp.int32)],
        compiler_params=_cp(("arbitrary", "arbitrary")),
    )(proj, proj, proj)


def _sb_bwd(proj, ctot, nrun_all, do):
    LP = proj.shape[0]
    B = SB_BLOCK
    W = min(SB_SPAN, LP)
    SUB = SB_SUB
    Q = min(SB_QTILE, LP)
    nq = LP // Q
    nsub = W // SUB
    scale = SB_DH ** -0.5
    qcol, kcol, vcol = OFF_SB // B, (OFF_SB + 512) // B, (OFF_SB + 1024) // B

    def body(n_ref, q_ref, k_ref, v_ref, c_ref, do_ref, dq_ref, dk_ref, dv_ref):
        i = pl.program_id(1)

        @pl.when(i == 0)
        def _():
            dk_ref[...] = jnp.zeros_like(dk_ref)
            dv_ref[...] = jnp.zeros_like(dv_ref)

        lane = _iota2((Q, B), 1)
        head_a = lane < SB_DH
        qs = q_ref[...] * scale
        qh = [jnp.where(head_a, qs, 0.0).astype(BF16), jnp.where(head_a, 0.0, qs).astype(BF16)]
        dof = do_ref[...]
        doh = [jnp.where(head_a, dof, 0.0).astype(BF16), jnp.where(head_a, 0.0, dof).astype(BF16)]
        cfull = c_ref[...]
        ctot_h = [cfull[:, 0:1], cfull[:, SB_DH:SB_DH + 1]]
        sub_r, sub_c = _iota2((SUB, SUB), 0), _iota2((SUB, SUB), 1)
        u_strict = (sub_r > sub_c).astype(BF16)
        l_strict = (sub_r < sub_c).astype(BF16)
        qpos = i * Q + _iota2((Q, W), 0)
        hi0 = (i + 1) * Q
        nrun = n_ref[pl.program_id(0), i]

        def span(t, carry):
            r = nrun - 1 - t
            hi = hi0 - r * W
            k0 = pl.multiple_of(jnp.maximum(hi - W, 0), B)
            kblk = k_ref[pl.ds(k0, W), :].astype(BF16)
            vblk = v_ref[pl.ds(k0, W), :].astype(BF16)
            kpos = k0 + _iota2((Q, W), 1)
            mask = (kpos < qpos) & (kpos >= PAD_ROWS) & (kpos < hi)
            new = [None] * 6
            dk_add, dv_add = [None, None], [None, None]
            subs = [slice(b * SUB, (b + 1) * SUB) for b in range(nsub)]

            def head(h):
                dq_acc, pre, ecar = carry[3 * h], carry[3 * h + 1], carry[3 * h + 2]
                z, e, l1m, lsg = _sb_scores(qh[h], kblk, mask)
                d_att = _dot_nt(doh[h], vblk)
                yield
                sig = jnp.where(z >= 0.0, 1.0, e) / (1.0 + e)
                suf = [_dot(l1m[:, bs].astype(BF16), u_strict) for bs in subs]
                yield
                att_parts, p_parts = [None] * nsub, [None] * nsub
                for b, bs in enumerate(subs):
                    pre = pre + _rowsum(l1m[:, bs])
                    att_parts[b] = jnp.where(mask[:, bs], jnp.exp(lsg[:, bs] + suf[b] + (ctot_h[h] - pre)), 0.0)
                    p_parts[b] = att_parts[b] * d_att[:, bs]
                pcum = [_dot(p.astype(BF16), l_strict) for p in p_parts]
                yield
                dz_parts = [None] * nsub
                for b, bs in enumerate(subs):
                    sg = sig[:, bs]
                    dz_parts[b] = jnp.where(mask[:, bs], p_parts[b] * (1.0 - sg) - sg * (ecar + pcum[b]), 0.0)
                    ecar = ecar + _rowsum(p_parts[b])
                att = jnp.concatenate(att_parts, axis=1).astype(BF16)
                dz = jnp.concatenate(dz_parts, axis=1).astype(BF16)
                new[3 * h:3 * h + 3] = [dq_acc + _dot(dz, kblk), pre, ecar]
                dk_add[h] = _dot_tn(dz, qh[h])
                dv_add[h] = _dot_tn(att, doh[h])

            _interleave(head(h) for h in range(2))
            dk_ref[pl.ds(k0, W), :] += dk_add[0] + dk_add[1]
            dv_ref[pl.ds(k0, W), :] += dv_add[0] + dv_add[1]
            return tuple(new)

        zero_o = jnp.zeros((Q, B), F32)
        zero_c = jnp.zeros((Q, 1), F32)
        res = lax.fori_loop(0, nrun, span, (zero_o, zero_c, zero_c, zero_o, zero_c, zero_c))
        dq_ref[...] = (jnp.where(head_a, res[0], res[3]) * scale).astype(BF16)

    blk = pl.BlockSpec((Q, B), lambda p, i: (i, p))
    col = pl.BlockSpec((LP, B), lambda p, i: (0, p))
    out = jax.ShapeDtypeStruct((LP, 512), F32)
    return pl.pallas_call(
        body, name="sb_bwd", grid=(SB_HEADS // 2, nq),
        in_specs=[pl.BlockSpec(memory_space=pltpu.SMEM),
                  pl.BlockSpec((Q, B), lambda p, i: (i, qcol + p)),
                  pl.BlockSpec((LP, B), lambda p, i: (0, kcol + p)),
                  pl.BlockSpec((LP, B), lambda p, i: (0, vcol + p)),
                  blk, blk],
        out_specs=[blk, col, col], out_shape=[jax.ShapeDtypeStruct((LP, 512), BF16), out, out],
        compiler_params=_cp(("arbitrary", "arbitrary")),
    )(nrun_all, proj, proj, proj, ctot, do)


def _sb_group_mean():
    r = jnp.right_shift(_iota2((512, 512), 0), 6)
    c = jnp.right_shift(_iota2((512, 512), 1), 6)
    return jnp.where(r == c, 1.0 / SB_DH, 0.0).astype(BF16)


def _attn_norm_fwd(og, proj, osb, gnw, snw):
    LP = og.shape[0]
    T = _tile(LP, 256)

    def body(og_ref, z_ref, os_ref, gnw_ref, snw_ref, y_ref):
        valid = (pl.program_id(0) * T + _iota2((T, 1), 0)) >= PAD_ROWS
        z = z_ref[...]
        zg = z * _sigmoid(z)
        for h in range(GDN_HEADS):
            sl = slice(h * GDN_D, (h + 1) * GDN_D)
            o = og_ref[:, sl]
            y = o * _rms(o) * gnw_ref[...] * zg[:, sl]
            y_ref[:, sl] = jnp.where(valid, y, 0.0).astype(BF16)
        o = os_ref[...]
        msq = _dot_exact_r(o * o, _sb_group_mean())
        y = o * lax.rsqrt(msq + NORM_EPS) * snw_ref[...]
        y_ref[:, 512:] = jnp.where(valid, y, 0.0).astype(BF16)

    row = pl.BlockSpec((T, 512), lambda i: (i, 0))
    return pl.pallas_call(
        body, name="attn_norm_fwd", grid=(LP // T,),
        in_specs=[row, pl.BlockSpec((T, 512), lambda i: (i, OFF_Z // 512)), row,
                  pl.BlockSpec((1, GDN_D), lambda i: (0, 0)), pl.BlockSpec((1, 512), lambda i: (0, 0))],
        out_specs=pl.BlockSpec((T, 1024), lambda i: (i, 0)),
        out_shape=jax.ShapeDtypeStruct((LP, 1024), BF16),
        compiler_params=_cp(("parallel",)),
    )(og, proj, osb, gnw, snw)


def _attn_norm_bwd(og, proj, osb, gnw, snw, dy, carry=None):
    LP = og.shape[0]
    T = _tile(LP, 256)

    def body(og_ref, z_ref, os_ref, gnw_ref, snw_ref, dy_ref, dog_ref, dz_ref, dos_ref, dgw_ref, dsw_ref):
        @pl.when(pl.program_id(0) == 0)
        def _():
            dgw_ref[...] = jnp.zeros_like(dgw_ref)
            dsw_ref[...] = jnp.zeros_like(dsw_ref)
        valid = (pl.program_id(0) * T + _iota2((T, 1), 0)) >= PAD_ROWS
        dy = jnp.where(valid, dy_ref[...], 0.0)
        z = z_ref[...]
        sg = _sigmoid(z)
        zg = z * sg
        dgw = jnp.zeros((1, GDN_D), F32)
        for h in range(GDN_HEADS):
            sl = slice(h * GDN_D, (h + 1) * GDN_D)
            o = og_ref[:, sl]
            dyh = dy[:, sl]
            dx, dwn = _rms_bwd(o, gnw_ref[...], dyh * zg[:, sl])
            dog_ref[:, sl] = dx
            dgw = dgw + _colsum(dwn)
            yn = o * _rms(o) * gnw_ref[...]
            dz_ref[:, sl] = (dyh * yn * (sg[:, sl] * (1.0 + z[:, sl] * (1.0 - sg[:, sl])))).astype(BF16)
        dgw_ref[...] += dgw
        o = os_ref[...]
        gm = _sb_group_mean()
        r = lax.rsqrt(_dot_exact_r(o * o, gm) + NORM_EPS)
        n = o * r
        dys = dy[:, 512:]
        dyw = dys * snw_ref[...]
        dos_ref[...] = r * (dyw - n * _dot_exact_r(dyw * n, gm))
        dsw_ref[...] += _colsum(dys * n)

    row = pl.BlockSpec((T, 512), lambda i: (i, 0))
    gw = pl.BlockSpec((1, GDN_D), lambda i: (0, 0))
    sw = pl.BlockSpec((1, 512), lambda i: (0, 0))
    o512 = jax.ShapeDtypeStruct((LP, 512), F32)
    return _call_carrying(
        carry, body, LP // T, name="attn_norm_bwd",
        in_specs=[row, pl.BlockSpec((T, 512), lambda i: (i, OFF_Z // 512)), row, gw, sw,
                  pl.BlockSpec((T, 1024), lambda i: (i, 0))],
        out_specs=[row, row, row, gw, sw],
        out_shape=[o512, jax.ShapeDtypeStruct((LP, 512), BF16), o512, jax.ShapeDtypeStruct((1, GDN_D), F32),
                   jax.ShapeDtypeStruct((1, 512), F32)],
        operands=(og, proj, osb, gnw, snw, dy))


def _resid_fwd(h0, mix, w_post, w_pre):
    LP, D = h0.shape
    T = _tile(LP, 512)

    def body(h0_ref, mix_ref, wp_ref, wf_ref, h1_ref, n2_ref):
        mix = mix_ref[...]
        h1 = h0_ref[...] + mix * _rms(mix) * wp_ref[...]
        h1_ref[...] = h1
        n2_ref[...] = (h1 * _rms(h1) * wf_ref[...]).astype(BF16)

    row = pl.BlockSpec((T, D), lambda i: (i, 0))
    vec = pl.BlockSpec((1, D), lambda i: (0, 0))
    return pl.pallas_call(
        body, name="resid_fwd", grid=(LP // T,),
        in_specs=[row, row, vec, vec], out_specs=[row, row],
        out_shape=[jax.ShapeDtypeStruct((LP, D), F32), jax.ShapeDtypeStruct((LP, D), BF16)],
        compiler_params=_cp(("parallel",)),
    )(h0, mix, w_post, w_pre)


def _resid_bwd(h1, mix, w_post, w_pre, dout, dn2):
    LP, D = h1.shape
    T = _tile(LP, 512)

    def body(h1_ref, mix_ref, wp_ref, wf_ref, dout_ref, dn2_ref, dh1_ref, dmix_ref, dwf_ref, dwp_ref):
        @pl.when(pl.program_id(0) == 0)
        def _():
            dwf_ref[...] = jnp.zeros_like(dwf_ref)
            dwp_ref[...] = jnp.zeros_like(dwp_ref)
        dx, dwn = _rms_bwd(h1_ref[...], wf_ref[...], dn2_ref[...])
        dh1 = dout_ref[...] + dx
        dh1_ref[...] = dh1
        dwf_ref[...] += _colsum(dwn)
        dmix, dwn2 = _rms_bwd(mix_ref[...], wp_ref[...], dh1)
        dmix_ref[...] = dmix.astype(BF16)
        dwp_ref[...] += _colsum(dwn2)

    row = pl.BlockSpec((T, D), lambda i: (i, 0))
    vec = pl.BlockSpec((1, D), lambda i: (0, 0))
    v = jax.ShapeDtypeStruct((1, D), F32)
    return pl.pallas_call(
        body, name="resid_bwd", grid=(LP // T,),
        in_specs=[row, row, vec, vec, row, row], out_specs=[row, row, vec, vec],
        out_shape=[jax.ShapeDtypeStruct((LP, D), F32), jax.ShapeDtypeStruct((LP, D), BF16), v, v],
        compiler_params=_cp(("arbitrary",)),
    )(h1, mix, w_post, w_pre, dout, dn2)


GELU_C = 0.7978845608028654
GELU_A = 0.044715


def _gelu_parts(x):
    t = jnp.tanh(GELU_C * (x + GELU_A * x * x * x))
    return 0.5 * x * (1.0 + t), t


def _convglu_fwd(up, conv_w, conv_b):
    LP, C = up.shape
    T = _tile(LP, 128)

    def body(x_ref, halo_ref, cw_ref, cb_ref, act_ref):
        i = pl.program_id(0)

        def conv(cols):
            ext = jnp.concatenate([jnp.where(i > 0, halo_ref[:, cols], 0.0), x_ref[:, cols]], axis=0)
            w = cw_ref[:, cols]
            return (w[2:3] * ext[8:] + w[1:2] * pltpu.roll(ext, 1, 0)[8:] + w[0:1] * pltpu.roll(ext, 2, 0)[8:]
                    + cb_ref[:, cols])

        for s in range(D_FF // LANE):
            gs = slice(s * LANE, (s + 1) * LANE)
            g, _ = _gelu_parts(conv(gs))
            act_ref[:, gs] = (g * conv(slice(D_FF + s * LANE, D_FF + (s + 1) * LANE))).astype(BF16)

    t8 = T // 8
    return pl.pallas_call(
        body, name="convglu_fwd", grid=(LP // T,),
        in_specs=[pl.BlockSpec((T, C), lambda i: (i, 0)),
                  pl.BlockSpec((8, C), lambda i: (jnp.maximum(i * t8 - 1, 0), 0)),
                  pl.BlockSpec((FFN_CONV, C), lambda i: (0, 0)), pl.BlockSpec((1, C), lambda i: (0, 0))],
        out_specs=pl.BlockSpec((T, D_FF), lambda i: (i, 0)),
        out_shape=jax.ShapeDtypeStruct((LP, D_FF), BF16),
        compiler_params=_cp(("parallel",)),
    )(up, up, conv_w, conv_b)


def _convglu_bwd(up, conv_w, conv_b, dact):
    LP, C = up.shape
    T = _tile(LP, 128)
    TE = T + 8
    nt = LP // T

    def body(x_ref, xp_ref, xn_ref, cw_ref, cb_ref, da_ref, dan_ref, dx_ref, dcw_ref, dcb_ref):
        i = pl.program_id(0)

        @pl.when(i == 0)
        def _():
            dcw_ref[...] = jnp.zeros_like(dcw_ref)
            dcb_ref[...] = jnp.zeros_like(dcb_ref)

        last = i == nt - 1

        def conv(cols):
            ext = jnp.concatenate([jnp.where(i > 0, xp_ref[:, cols], 0.0), x_ref[:, cols],
                                   jnp.where(last, 0.0, xn_ref[:, cols])], axis=0)
            sh = [ext[8:8 + TE], pltpu.roll(ext, 1, 0)[8:8 + TE], pltpu.roll(ext, 2, 0)[8:8 + TE]]
            w = cw_ref[:, cols]
            return w[2:3] * sh[0] + w[1:2] * sh[1] + w[0:1] * sh[2] + cb_ref[:, cols], sh, w

        def back(cols, dy, sh, w):
            dy_t = dy[0:T]
            dcb_ref[:, cols] += _colsum(dy_t)
            for j in range(FFN_CONV):
                dcw_ref[j:j + 1, cols] += _colsum(dy_t * sh[FFN_CONV - 1 - j][0:T])
            dx_ref[:, cols] = (w[2:3] * dy_t + w[1:2] * pltpu.roll(dy, TE - 1, 0)[0:T]
                               + w[0:1] * pltpu.roll(dy, TE - 2, 0)[0:T]).astype(BF16)

        for s in range(D_FF // LANE):
            gs = slice(s * LANE, (s + 1) * LANE)
            vs = slice(D_FF + s * LANE, D_FF + (s + 1) * LANE)
            gate, sh_g, w_g = conv(gs)
            val, sh_v, w_v = conv(vs)
            g, t = _gelu_parts(gate)
            dg_dx = 0.5 * (1.0 + t) + 0.5 * gate * (1.0 - t * t) * GELU_C * (1.0 + 3.0 * GELU_A * gate * gate)
            da = jnp.concatenate([da_ref[:, gs], jnp.where(last, 0.0, dan_ref[:, gs])], axis=0)
            back(gs, da * val * dg_dx, sh_g, w_g)
            back(vs, da * g, sh_v, w_v)

    t8 = T // 8
    nb8 = LP // 8
    prev8 = lambda w: pl.BlockSpec((8, w), lambda i: (jnp.maximum(i * t8 - 1, 0), 0))
    next8 = lambda w: pl.BlockSpec((8, w), lambda i: (jnp.minimum((i + 1) * t8, nb8 - 1), 0))
    row = lambda w: pl.BlockSpec((T, w), lambda i: (i, 0))
    small = lambda r: pl.BlockSpec((r, C), lambda i: (0, 0))
    return pl.pallas_call(
        body, name="convglu_bwd", grid=(nt,),
        in_specs=[row(C), prev8(C), next8(C), small(FFN_CONV), small(1), row(D_FF), next8(D_FF)],
        out_specs=[row(C), small(FFN_CONV), small(1)],
        out_shape=[jax.ShapeDtypeStruct((LP, C), BF16), jax.ShapeDtypeStruct((FFN_CONV, C), F32),
                   jax.ShapeDtypeStruct((1, C), F32)],
        compiler_params=_cp(("arbitrary",)),
    )(up, up, up, conv_w, conv_b, dact, dact)


def _final(h1, f, w_post, target, n_real):
    LP, D = h1.shape
    T = _tile(LP, 256)

    def body(h1_ref, f_ref, w_ref, t_ref, loss_ref, dout_ref, df_ref, dw_ref):
        @pl.when(pl.program_id(0) == 0)
        def _():
            loss_ref[...] = jnp.zeros_like(loss_ref)
            dw_ref[...] = jnp.zeros_like(dw_ref)
        rows = pl.program_id(0) * T + _iota2((T, 1), 0)
        real = (rows >= ROW0) & (rows < ROW0 + n_real)
        f = f_ref[...]
        out = h1_ref[...] + f * _rms(f) * w_ref[...]
        err = jnp.where(real, out - t_ref[...], 0.0)
        loss_ref[...] += 0.5 * jnp.sum(_colsum(jnp.mean(err * err, axis=-1, keepdims=True)), axis=-1, keepdims=True)
        dout = err * (1.0 / D)
        dout_ref[...] = dout
        dx, dwn = _rms_bwd(f, w_ref[...], dout)
        df_ref[...] = dx.astype(BF16)
        dw_ref[...] += _colsum(dwn)

    row = pl.BlockSpec((T, D), lambda i: (i, 0))
    vec = pl.BlockSpec((1, D), lambda i: (0, 0))
    return pl.pallas_call(
        body, name="final_loss", grid=(LP // T,),
        in_specs=[row, row, vec, row],
        out_specs=[pl.BlockSpec((1, 128), lambda i: (0, 0)), row, row, vec],
        out_shape=[jax.ShapeDtypeStruct((1, 128), F32), jax.ShapeDtypeStruct((LP, D), F32),
                   jax.ShapeDtypeStruct((LP, D), BF16), jax.ShapeDtypeStruct((1, D), F32)],
        compiler_params=_cp(("arbitrary",)),
    )(h1, f, w_post, target)


ANY_SPEC = pl.BlockSpec(memory_space=pl.ANY)
N_CHIP = 4


def _other_chips(x, y):
    return [(1 - x, y), (x, 1 - y), (1 - x, 1 - y)]


def _gather_direct(arrs, name):
    n = len(arrs)
    npeer = N_DEV - 1

    def body(*refs):
        ins, outs = refs[:n], refs[n:2 * n]
        send_sems, recv_sems, loc_sems = refs[2 * n:]
        x, y, c = lax.axis_index("x"), lax.axis_index("y"), lax.axis_index("c")
        me = 4 * x + 2 * y + c
        copies = []
        for a in range(n):
            for kk in range(1, N_DEV):
                px = 1 - x if kk & 4 else x
                py = 1 - y if kk & 2 else y
                pc = 1 - c if kk & 1 else c
                s = a * npeer + kk - 1
                cp = pltpu.make_async_remote_copy(src_ref=ins[a], dst_ref=outs[a].at[me], send_sem=send_sems.at[s],
                                                  recv_sem=recv_sems.at[s], device_id=(px, py, pc), device_id_type=MESH)
                cp.start()
                copies.append(cp)
            own = pltpu.make_async_copy(ins[a], outs[a].at[me], loc_sems.at[a])
            own.start()
            copies.append(own)
        for cp in copies:
            cp.wait()

    shapes = [jax.ShapeDtypeStruct((N_DEV,) + tuple(a.shape), a.dtype) for a in arrs]
    return pl.pallas_call(
        body, name=name, in_specs=[ANY_SPEC] * n, out_specs=[ANY_SPEC] * n, out_shape=shapes,
        scratch_shapes=[pltpu.SemaphoreType.DMA((n * npeer,)), pltpu.SemaphoreType.DMA((n * npeer,)),
                        pltpu.SemaphoreType.DMA((n,))],
        compiler_params=pltpu.CompilerParams(has_side_effects=True),
    )(*arrs)


class _Exchange:
    def __init__(self, arrs, out_shapes, scratch, start, finish, mid=None):
        self.arrs, self.out_shapes, self.scratch = list(arrs), list(out_shapes), list(scratch)
        self.start, self.finish, self.mid = start, finish, mid

    @property
    def n(self):
        return len(self.arrs)


def _run_exchange(ex, name):
    n = ex.n

    def body(*refs):
        ins, outs, sems = refs[:n], refs[n:2 * n], refs[2 * n:]
        ex.start(ins, outs, sems)
        if ex.mid is not None:
            ex.mid(ins, outs, sems)
        ex.finish(ins, outs, sems)

    return pl.pallas_call(
        body, name=name, in_specs=[ANY_SPEC] * n, out_specs=[ANY_SPEC] * n, out_shape=ex.out_shapes,
        scratch_shapes=ex.scratch, compiler_params=pltpu.CompilerParams(has_side_effects=True),
    )(*ex.arrs)


def _carry_begin(ex, refs, step, nsteps):
    if ex is None:
        return

    @pl.when(step == 0)
    def _():
        ex.start(*refs)

    if ex.mid is not None:
        @pl.when(step == min(nsteps - 1, (3 * nsteps) // 5))
        def _():
            ex.mid(*refs)


def _carry_end(ex, refs, step, nsteps):
    if ex is None:
        return

    @pl.when(step == nsteps - 1)
    def _():
        ex.finish(*refs)


def _gather_two_level(arrs):
    n = len(arrs)
    K = 7

    def env(ins, outs, sems):
        send_sems, recv_sems, loc_sems = sems
        x, y, c = lax.axis_index("x"), lax.axis_index("y"), lax.axis_index("c")

        def cp(a, k, src, slot, to):
            return pltpu.make_async_remote_copy(src_ref=src, dst_ref=outs[a].at[slot], send_sem=send_sems.at[a * K + k],
                                                recv_sem=recv_sems.at[a * K + k], device_id=to, device_id_type=MESH)

        me = 4 * x + 2 * y + c
        owns = [pltpu.make_async_copy(ins[a], outs[a].at[me], loc_sems.at[a]) for a in range(n)]
        first = []
        for a in range(n):
            first.append(cp(a, 0, ins[a], me, (x, y, 1 - c)))
            first += [cp(a, 1 + j, ins[a], me, (px, py, c)) for j, (px, py) in enumerate(_other_chips(x, y))]
        passed = []
        for j, (px, py) in enumerate(_other_chips(x, y)):
            slot = 4 * px + 2 * py + c
            passed += [(cp(a, 1 + j, ins[a], slot, (px, py, c)), cp(a, 4 + j, outs[a].at[slot], slot, (x, y, 1 - c)))
                       for a in range(n)]
        from_sib = []
        for a in range(n):
            from_sib.append(cp(a, 0, ins[a], 4 * x + 2 * y + (1 - c), (x, y, 1 - c)))
            from_sib += [cp(a, 4 + j, ins[a], 4 * px + 2 * py + (1 - c), (x, y, 1 - c))
                         for j, (px, py) in enumerate(_other_chips(x, y))]
        return owns, first, passed, from_sib

    def start(ins, outs, sems):
        owns, first, _, _ = env(ins, outs, sems)
        for cp in owns + first:
            cp.start()

    def mid(ins, outs, sems):
        _, _, passed, _ = env(ins, outs, sems)
        for arrival, fwd in passed:
            arrival.wait_recv()
            fwd.start()

    def finish(ins, outs, sems):
        owns, first, passed, from_sib = env(ins, outs, sems)
        for cp in from_sib:
            cp.wait_recv()
        for cp in first + [fwd for _, fwd in passed]:
            cp.wait_send()
        for cp in owns:
            cp.wait()

    shapes = [jax.ShapeDtypeStruct((N_DEV,) + tuple(a.shape), a.dtype) for a in arrs]
    scratch = [pltpu.SemaphoreType.DMA((n * K,)), pltpu.SemaphoreType.DMA((n * K,)), pltpu.SemaphoreType.DMA((n,))]
    return _Exchange(arrs, shapes, scratch, start, finish, mid)


def _swap_sibling(arrs):
    n = len(arrs)

    def copies(ins, outs, sems):
        send_sems, recv_sems = sems
        x, y, c = lax.axis_index("x"), lax.axis_index("y"), lax.axis_index("c")
        return [pltpu.make_async_remote_copy(src_ref=ins[a], dst_ref=outs[a], send_sem=send_sems.at[a],
                                             recv_sem=recv_sems.at[a], device_id=(x, y, 1 - c), device_id_type=MESH)
                for a in range(n)]

    def start(ins, outs, sems):
        for cp in copies(ins, outs, sems):
            cp.start()

    def finish(ins, outs, sems):
        for cp in copies(ins, outs, sems):
            cp.wait()

    shapes = [jax.ShapeDtypeStruct(tuple(a.shape), a.dtype) for a in arrs]
    return _Exchange(arrs, shapes, [pltpu.SemaphoreType.DMA((n,)), pltpu.SemaphoreType.DMA((n,))], start, finish)


def _exchange_chips(arrs):
    n = len(arrs)
    K = N_CHIP - 1

    def copies(ins, outs, sems):
        send_sems, recv_sems, loc_sems = sems
        x, y, c = lax.axis_index("x"), lax.axis_index("y"), lax.axis_index("c")
        mine = 2 * x + y
        out = []
        for a in range(n):
            out += [pltpu.make_async_remote_copy(src_ref=ins[a].at[2 * px + py], dst_ref=outs[a].at[mine],
                                                 send_sem=send_sems.at[a * K + j], recv_sem=recv_sems.at[a * K + j],
                                                 device_id=(px, py, c), device_id_type=MESH)
                    for j, (px, py) in enumerate(_other_chips(x, y))]
            out.append(pltpu.make_async_copy(ins[a].at[mine], outs[a].at[mine], loc_sems.at[a]))
        return out

    def start(ins, outs, sems):
        for cp in copies(ins, outs, sems):
            cp.start()

    def finish(ins, outs, sems):
        for cp in copies(ins, outs, sems):
            cp.wait()

    shapes = [jax.ShapeDtypeStruct(tuple(a.shape), a.dtype) for a in arrs]
    scratch = [pltpu.SemaphoreType.DMA((n * K,)), pltpu.SemaphoreType.DMA((n * K,)), pltpu.SemaphoreType.DMA((n,))]
    return _Exchange(arrs, shapes, scratch, start, finish)


def _add_halves(mine, theirs, name):
    _, R, C = mine.shape
    cap = max(16, (ELEMWISE_VMEM // (4 * C * 10)) // 16 * 16)
    T = R if R <= cap else _tile(R, cap, 16)

    def body(a_ref, b_ref, o_ref):
        o_ref[...] = (a_ref[...] + b_ref[...].astype(F32)).astype(BF16)

    blk = pl.BlockSpec((N_CHIP, T, C), lambda i: (0, i, 0))
    return pl.pallas_call(
        body, name=name, grid=(R // T,), in_specs=[blk, blk], out_specs=blk,
        out_shape=jax.ShapeDtypeStruct(mine.shape, BF16), compiler_params=_cp(("parallel",)),
    )(mine, theirs)


def _adamw(parts, w, m, v, name):
    R, C = w.shape
    npart = parts.shape[0]
    cap = max(16, (ELEMWISE_VMEM // (4 * C * 12)) // 16 * 16)
    T = R if R <= cap else _tile(R, cap, 16)

    def body(p_ref, w_ref, m_ref, v_ref, g_ref, d_ref, nm_ref, nv_ref):
        g = p_ref[0].astype(F32)
        for k in range(1, npart):
            g = g + p_ref[k].astype(F32)
        mm = ADAM_B1 * m_ref[...] + (1.0 - ADAM_B1) * g
        vv = ADAM_B2 * v_ref[...] + (1.0 - ADAM_B2) * (g * g)
        m_hat = mm / (1.0 - ADAM_B1 ** ADAM_STEP)
        v_hat = vv / (1.0 - ADAM_B2 ** ADAM_STEP)
        g_ref[...] = g
        d_ref[...] = -ADAM_LR * (m_hat / (jnp.sqrt(v_hat) + ADAM_EPS) + ADAM_WD * w_ref[...])
        nm_ref[...] = mm
        nv_ref[...] = vv

    row = pl.BlockSpec((T, C), lambda i: (i, 0))
    out = jax.ShapeDtypeStruct((R, C), F32)
    return pl.pallas_call(
        body, name=name, grid=(R // T,),
        in_specs=[pl.BlockSpec((npart, T, C), lambda i: (0, i, 0)), row, row, row],
        out_specs=[row] * 4, out_shape=[out] * 4,
        compiler_params=_cp(("parallel",)),
    )(parts, w, m, v)


SMALL = ("attn_pre_norm", "gdn_A_log", "gdn_dt_bias", "gdn_norm_w", "sb_norm_w", "attn_post_norm",
         "ffn_pre_norm", "ffn_conv_b", "ffn_post_norm")


def _pack_small(arrs):
    rows = []
    for a in arrs:
        flat = a.reshape(-1).astype(F32)
        n = -(-flat.shape[0] // 128) * 128
        rows.append(jnp.pad(flat, (0, n - flat.shape[0])).reshape(-1, 128))
    slab = jnp.concatenate(rows, axis=0)
    pad = (-slab.shape[0]) % 8
    return jnp.pad(slab, ((0, pad), (0, 0)))


def _unpack_small(slab, shapes):
    out, r = [], 0
    for shp in shapes:
        size = 1
        for s in shp:
            size *= s
        nr = -(-size // 128)
        out.append(slab[r:r + nr].reshape(-1)[:size].reshape(shp))
        r += nr
    return out


def _to_blocks_cols(a):
    R, C = a.shape
    return a.reshape(R, N_DEV, C // N_DEV).transpose(1, 0, 2)


def _from_blocks_cols(a):
    n, R, c = a.shape
    return a.transpose(1, 0, 2).reshape(R, n * c)


def kernel(x, meta_tokens, attn_pre_norm, w_in, gdn_conv_w, gdn_A_log, gdn_dt_bias, gdn_norm_w, sb_norm_w, w_out, attn_post_norm, ffn_pre_norm, w_ffn_up, ffn_conv_w, ffn_conv_b, w_ffn_down, ffn_post_norm, loss_target, m_meta_tokens, m_attn_pre_norm, m_w_in, m_gdn_conv_w, m_gdn_A_log, m_gdn_dt_bias, m_gdn_norm_w, m_sb_norm_w, m_w_out, m_attn_post_norm, m_ffn_pre_norm, m_w_ffn_up, m_ffn_conv_w, m_ffn_conv_b, m_w_ffn_down, m_ffn_post_norm, v_meta_tokens, v_attn_pre_norm, v_w_in, v_gdn_conv_w, v_gdn_A_log, v_gdn_dt_bias, v_gdn_norm_w, v_sb_norm_w, v_w_out, v_attn_post_norm, v_ffn_pre_norm, v_w_ffn_up, v_ffn_conv_w, v_ffn_conv_b, v_w_ffn_down, v_ffn_post_norm):
    args = dict(locals())
    seq = x.shape[1]
    LP = -(-(ROW0 + seq) // LP_ALIGN) * LP_ALIGN
    tail = LP - ROW0 - seq

    meta_f = _from_blocks_cols(_run_exchange(_gather_two_level([meta_tokens]), "gather_meta")[0])

    (h0, target), got = _build_rows(x[0], meta_f, loss_target[0], LP,
                                    carry=_gather_two_level([w_in[0].astype(BF16), gdn_conv_w[0]]))
    (u,), _ = _prenorm_fwd(h0, attn_pre_norm)
    win_o = _from_blocks_cols(got[0])
    o_ab = C_QKV
    o_z = o_ab + 2 * GDN_HEADS
    w_inp = jnp.concatenate([win_o[:, :C_QKV], win_o[:, o_z:o_z + C_Z], win_o[:, o_z + C_Z:],
                             win_o[:, o_ab:o_z], jnp.zeros((D_MODEL, C_AB - 2 * GDN_HEADS), BF16)], axis=1)
    gconv_f = _from_blocks_cols(got[1])
    proj = _mm(u, w_inp, F32, "mm_in")
    (qn, kn, vg, beta_b, g_b), got = _gdn_pre_fwd(
        proj, gconv_f, gdn_A_log, gdn_dt_bias,
        carry=_gather_two_level([w_out[0].astype(BF16), w_ffn_down[0].astype(BF16)]))
    w_out_f = got[0].reshape(D_MODEL, D_MODEL)
    w_down_f = got[1].reshape(D_FF, D_MODEL)
    (cu, cw, cqd, ckd, cqk, ct, cgl), got = _gdn_chunk_fwd(
        qn, kn, vg, beta_b, g_b, carry=_gather_two_level([w_ffn_up[0].astype(BF16), ffn_conv_w[0]]))
    w_up_f = _from_blocks_cols(got[0])
    fconv_f = _from_blocks_cols(got[1])
    og, ssave = _gdn_scan_fwd(cu, cw, cqd, ckd, cqk, cgl)
    osb, ctot, sb_nrun = _sb_fwd(proj)
    snw = sb_norm_w.reshape(1, SB_HEADS * SB_DH)
    y = _attn_norm_fwd(og, proj, osb, gdn_norm_w, snw)
    mix = _mm(y, w_out_f, F32, "mm_out")
    h1, n2 = _resid_fwd(h0, mix, attn_post_norm, ffn_pre_norm)
    up = _mm(n2, w_up_f, F32, "mm_up")
    act = _convglu_fwd(up, fconv_f, ffn_conv_b)
    f = _mm(act, w_down_f, F32, "mm_down")
    loss_part, dout, df, d_fpost = _final(h1, f, ffn_post_norm, target, seq)
    loss = lax.psum(loss_part[0, 0], ("x", "y", "c"))

    d_wdown = _mm_tn(act, df, "mm_dw_down")
    dact = _mm_nt(df, w_down_f, F32, "mm_dact")
    dup, d_fconv, d_fconvb = _convglu_bwd(up, fconv_f, ffn_conv_b, dact)
    d_wup = _mm_tn(n2, dup, "mm_dw_up")
    dn2 = _mm_nt(dup, w_up_f, F32, "mm_dn2")
    dh1, dmix, d_fpre, d_apost = _resid_bwd(h1, mix, attn_post_norm, ffn_pre_norm, dout, dn2)
    d_wout = _mm_tn(y, dmix, "mm_dw_out")
    dy = _mm_nt(dmix, w_out_f, F32, "mm_dy")
    my_c = lax.axis_index("c")

    def core_halves(blocks):
        halves = [s.reshape((N_CHIP, 2) + s.shape[1:]) for s in blocks]
        return ([lax.dynamic_index_in_dim(h, my_c, axis=1, keepdims=False) for h in halves],
                [lax.dynamic_index_in_dim(h, 1 - my_c, axis=1, keepdims=False).astype(BF16) for h in halves])

    early_names = ("w_out", "w_ffn_up", "w_ffn_down", "ffn_conv_w")
    e_mine, e_send = core_halves([d_wout.reshape(N_DEV, D_MODEL // N_DEV, D_MODEL), _to_blocks_cols(d_wup),
                                  d_wdown.reshape(N_DEV, D_FF // N_DEV, D_MODEL), _to_blocks_cols(d_fconv)])
    (dog, dz, dos, d_gnw, d_snw), e_theirs = _attn_norm_bwd(og, proj, osb, gdn_norm_w, snw, dy,
                                                            carry=_swap_sibling(e_send))
    e_sums = [_add_halves(a, b, "grads_add_" + nm) for nm, a, b in zip(early_names, e_mine, e_theirs)]
    dqs, dks, dvs = _sb_bwd(proj, ctot, sb_nrun, dos)
    (du_, dw_, dqd_, dkd_, dqk_, dgl_), _ = _gdn_scan_bwd(cu, cw, cqd, ckd, cqk, cgl, ssave, dog)
    dqn, dkn, dvg, dbeta, dg = _gdn_chunk_bwd(qn, kn, vg, beta_b, g_b, ct, du_, dw_, dqd_, dkd_, dqk_, dgl_)
    (dqkv, dab, d_gconv, d_gsc), e_recv = _gdn_pre_bwd(proj, gconv_f, gdn_A_log, gdn_dt_bias, dqn, dkn, dvg, dbeta, dg,
                                                       carry=_exchange_chips(e_sums))
    dpieces = [dqkv, dz, dqs, dks, dvs, dab]
    doffs = [0, OFF_Z, OFF_SB, OFF_SB + 512, OFF_SB + 1024, OFF_AB]
    dw_qkv, dw_ab = _mm_tn_pieces(u, [dqkv, dab], "mm_dw_in_gdn")
    dw_z, dw_qs, dw_ks, dw_vs = _mm_tn_pieces(u, [dz, dqs, dks, dvs], "mm_dw_in_rest")
    du0 = _mm_nt_pieces(dpieces, doffs, w_inp, F32, "mm_du")
    d_win = jnp.concatenate([dw_qkv, dw_ab[:, :2 * GDN_HEADS], dw_z, dw_qs, dw_ks, dw_vs], axis=1)
    late_names = ("w_in", "gdn_conv_w")
    l_mine, l_send = core_halves([_to_blocks_cols(d_win), _to_blocks_cols(d_gconv)])
    l_theirs = _run_exchange(_swap_sibling(l_send), "grads_swap_sibling")
    l_sums = [_add_halves(a, b, "grads_add_" + nm) for nm, a, b in zip(late_names, l_mine, l_theirs)]
    (dh0, d_apre), l_recv = _prenorm_bwd(h0, attn_pre_norm, du0, dh1, carry=_exchange_chips(l_sums))
    grad_x = dh0[ROW0:ROW0 + seq][None]
    d_meta = dh0[PAD_ROWS:ROW0]

    small_grads = [d_apre, d_gsc[0:1, :GDN_HEADS], d_gsc[1:2, :GDN_HEADS], d_gnw, d_snw.reshape(1, SB_HEADS, SB_DH),
                   d_apost, d_fpre, d_fconvb, d_fpost]
    n_small_rows = _pack_small(small_grads).shape[0]
    slab_parts = _gather_direct([jnp.concatenate([_pack_small(small_grads), d_meta.reshape(-1, LANE)], axis=0)],
                                name="gather_small_grads")[0]
    me = 4 * lax.axis_index("x") + 2 * lax.axis_index("y") + my_c
    meta_parts = lax.dynamic_index_in_dim(
        slab_parts[:, n_small_rows:].reshape(N_DEV, N_META, N_DEV, LANE), me, axis=2, keepdims=False)
    slab_parts = slab_parts[:, :n_small_rows]

    res = {}
    for nm, parts in zip(early_names + late_names + ("meta_tokens",), list(e_recv) + list(l_recv) + [meta_parts]):
        wloc = args[nm]
        shp = wloc.shape
        w2 = wloc.reshape(shp[-2], shp[-1])
        outs = _adamw(parts, w2, args["m_" + nm].reshape(w2.shape), args["v_" + nm].reshape(w2.shape), "adamw_" + nm)
        res[nm] = [o.reshape(shp) for o in outs]
    small_shapes = [args[nm].shape for nm in SMALL]
    outs = _adamw(slab_parts, _pack_small([args[nm] for nm in SMALL]), _pack_small([args["m_" + nm] for nm in SMALL]),
                  _pack_small([args["v_" + nm] for nm in SMALL]), "adamw_small")
    for k in range(4):
        for nm, val in zip(SMALL, _unpack_small(outs[k], small_shapes)):
            res.setdefault(nm, [None] * 4)[k] = val

    order = ("meta_tokens", "attn_pre_norm", "w_in", "gdn_conv_w", "gdn_A_log", "gdn_dt_bias", "gdn_norm_w",
             "sb_norm_w", "w_out", "attn_post_norm", "ffn_pre_norm", "w_ffn_up", "ffn_conv_w", "ffn_conv_b",
             "w_ffn_down", "ffn_post_norm")
    return (loss, grad_x, *[res[nm][0] for nm in order], *[res[nm][1] for nm in order],
            *[res[nm][2] for nm in order], *[res[nm][3] for nm in order])
```

```python
import functools

import jax
import jax.numpy as jnp
from jax import lax
from jax.experimental import pallas as pl
from jax.experimental.pallas import tpu as pltpu

F32 = jnp.float32
BF16 = jnp.bfloat16

D_MODEL = 1024
N_META = 16
GDN_HEADS = 4
GDN_D = 128
GDN_CHUNK = 64
GDN_CONV = 4
GDN_ROWS = 256
SCAN_CHUNKS = 4
SB_HEADS = 8
SB_DH = 64
SB_BLOCK = 128
D_FF = 2816
FFN_CONV = 3
NORM_EPS = 1e-6
L2_EPS = 1e-6
LANE = 128
N_DEV = 8

PAD_ROWS = SB_BLOCK - N_META
ROW0 = SB_BLOCK
SB_SPAN = 512
SB_DEAD = -104.0
SB_SUB = 256
SB_QTILE = 256
LP_ALIGN = 256

C_QKV = 3 * GDN_HEADS * GDN_D
C_Z = GDN_HEADS * GDN_D
C_SB = 3 * SB_HEADS * SB_DH
C_AB = 256
OFF_Z = C_QKV
OFF_SB = OFF_Z + C_Z
OFF_AB = OFF_SB + C_SB
D_INP = OFF_AB + C_AB
D_IN = C_QKV + 2 * GDN_HEADS + C_Z + C_SB

ADAM_LR = 0.001
ADAM_B1 = 0.9
ADAM_B2 = 0.999
ADAM_EPS = 1e-08
ADAM_WD = 0.01
ADAM_STEP = 10

VMEM_LIMIT = 56 * 1024 * 1024
ELEMWISE_VMEM = 8 * 1024 * 1024
MESH = pl.DeviceIdType.MESH


def _cp(sem=None):
    kw = dict(vmem_limit_bytes=VMEM_LIMIT)
    if sem is not None:
        kw["dimension_semantics"] = sem
    return pltpu.CompilerParams(**kw)


def _tile(n, cap, unit=128):
    best = None
    t = unit
    while t <= min(n, cap):
        if n % t == 0:
            best = t
        t += unit
    assert best is not None, (n, cap, unit)
    return best


def _dot(a, b):
    return jnp.dot(a, b, preferred_element_type=F32)


def _dot_nt(a, b):
    return lax.dot_general(a, b, (((1,), (1,)), ((), ())), preferred_element_type=F32)


def _dot_tn(a, b):
    return lax.dot_general(a, b, (((0,), (0,)), ((), ())), preferred_element_type=F32)


def _split(x):
    hi = x.astype(BF16)
    lo = (x - hi.astype(F32)).astype(BF16)
    return hi, lo


def _dot1(a, b, f=_dot):
    return f(a.astype(BF16), b.astype(BF16))


def _dot3(a, b, f=_dot):
    ah, al = _split(a)
    bh, bl = _split(b)
    return f(ah, bh) + (f(ah, bl) + f(al, bh))


def _dot_exact_l(m_bf16, x, f=_dot):
    xh, xl = _split(x)
    return f(m_bf16, xh) + f(m_bf16, xl)


def _dot_exact_r(x, m_bf16, f=_dot):
    xh, xl = _split(x)
    return f(xh, m_bf16) + f(xl, m_bf16)


def _iota2(shape, dim):
    return lax.broadcasted_iota(jnp.int32, shape, dim)


def _sigmoid(x):
    return 1.0 / (1.0 + jnp.exp(-x))


def _softplus(x):
    return jnp.maximum(x, 0.0) + jnp.log(1.0 + jnp.exp(-jnp.abs(x)))


def _colsum(x):
    return jnp.sum(x, axis=0, keepdims=True)


def _rowsum(x):
    return jnp.sum(x, axis=-1, keepdims=True)


def _mm(a, b, out_dtype, name):
    M, K = a.shape
    K2, N = b.shape
    assert K == K2
    tm = _tile(M, 768)
    tn = _tile(N, max(128, (6 * 1024 * 1024) // (2 * K)))

    def body(a_ref, b_ref, o_ref):
        o_ref[...] = _dot(a_ref[...].astype(BF16), b_ref[...].astype(BF16)).astype(o_ref.dtype)

    return pl.pallas_call(
        body, name=name, grid=(N // tn, M // tm),
        in_specs=[pl.BlockSpec((tm, K), lambda j, i: (i, 0)), pl.BlockSpec((K, tn), lambda j, i: (0, j))],
        out_specs=pl.BlockSpec((tm, tn), lambda j, i: (i, j)),
        out_shape=jax.ShapeDtypeStruct((M, N), out_dtype),
        compiler_params=_cp(("parallel", "parallel")),
    )(a, b)


def _mm_nt(a, b, out_dtype, name):
    M, K = a.shape
    N, K2 = b.shape
    assert K == K2
    tm = _tile(M, 768)
    tn = _tile(N, max(128, (6 * 1024 * 1024) // (2 * K)))

    def body(a_ref, b_ref, o_ref):
        o_ref[...] = _dot_nt(a_ref[...].astype(BF16), b_ref[...].astype(BF16)).astype(o_ref.dtype)

    return pl.pallas_call(
        body, name=name, grid=(N // tn, M // tm),
        in_specs=[pl.BlockSpec((tm, K), lambda j, i: (i, 0)), pl.BlockSpec((tn, K), lambda j, i: (j, 0))],
        out_specs=pl.BlockSpec((tm, tn), lambda j, i: (i, j)),
        out_shape=jax.ShapeDtypeStruct((M, N), out_dtype),
        compiler_params=_cp(("parallel", "parallel")),
    )(a, b)


def _mm_nt_pieces(pieces, offsets, b, out_dtype, name):
    M = pieces[0].shape[0]
    N = b.shape[0]
    n = len(pieces)
    widths = [p.shape[1] for p in pieces]
    assert all(off % k == 0 for off, k in zip(offsets, widths))
    tm = _tile(M, 768)
    tn = _tile(N, 512)

    def body(*refs):
        acc = _dot_nt(refs[0][...].astype(BF16), refs[n][...].astype(BF16))
        for p in range(1, n):
            acc = acc + _dot_nt(refs[p][...].astype(BF16), refs[n + p][...].astype(BF16))
        refs[2 * n][...] = acc.astype(out_dtype)

    return pl.pallas_call(
        body, name=name, grid=(N // tn, M // tm),
        in_specs=[pl.BlockSpec((tm, k), lambda j, i: (i, 0)) for k in widths]
        + [pl.BlockSpec((tn, k), functools.partial(lambda j, i, blk: (j, blk), blk=off // k))
           for off, k in zip(offsets, widths)],
        out_specs=pl.BlockSpec((tm, tn), lambda j, i: (i, j)),
        out_shape=jax.ShapeDtypeStruct((M, N), out_dtype),
        compiler_params=_cp(("parallel", "parallel")),
    )(*pieces, *([b] * n))


def _mm_tn_pieces(a, pieces, name):
    M, K = a.shape
    n = len(pieces)
    tm = _tile(M, 768)

    def body(*refs):
        @pl.when(pl.program_id(0) == 0)
        def _():
            for p in range(n):
                refs[1 + n + p][...] = jnp.zeros_like(refs[1 + n + p])
        at = refs[0][...].astype(BF16)
        for p in range(n):
            refs[1 + n + p][...] += _dot_tn(at, refs[1 + p][...].astype(BF16))

    return pl.pallas_call(
        body, name=name, grid=(M // tm,),
        in_specs=[pl.BlockSpec((tm, K), lambda m: (m, 0))] + [pl.BlockSpec((tm, p.shape[1]), lambda m: (m, 0)) for p in pieces],
        out_specs=[pl.BlockSpec((K, p.shape[1]), lambda m: (0, 0)) for p in pieces],
        out_shape=[jax.ShapeDtypeStruct((K, p.shape[1]), F32) for p in pieces],
        compiler_params=_cp(("arbitrary",)),
    )(a, *pieces)


def _mm_tn(a, b, name):
    M, K = a.shape
    M2, N = b.shape
    assert M == M2
    tm = _tile(M, 1408)
    tk = _tile(K, 1408)
    tn = _tile(N, 1408)

    def body(a_ref, b_ref, o_ref):
        @pl.when(pl.program_id(2) == 0)
        def _():
            o_ref[...] = jnp.zeros_like(o_ref)
        o_ref[...] += _dot_tn(a_ref[...].astype(BF16), b_ref[...].astype(BF16))

    return pl.pallas_call(
        body, name=name, grid=(K // tk, N // tn, M // tm),
        in_specs=[pl.BlockSpec((tm, tk), lambda i, j, m: (m, i)), pl.BlockSpec((tm, tn), lambda i, j, m: (m, j))],
        out_specs=pl.BlockSpec((tk, tn), lambda i, j, m: (i, j)),
        out_shape=jax.ShapeDtypeStruct((K, N), F32),
        compiler_params=_cp(("parallel", "parallel", "arbitrary")),
    )(a, b)


def _rms(x):
    return lax.rsqrt(jnp.mean(x * x, axis=-1, keepdims=True) + NORM_EPS)


def _rms_bwd(x, w, dy):
    r = _rms(x)
    n = x * r
    dyw = dy * w
    dx = r * (dyw - n * jnp.mean(dyw * n, axis=-1, keepdims=True))
    return dx, dy * n


def _build_rows(x, meta, target, LP, carry=None):
    seq, D = x.shape
    T = SB_BLOCK
    nx = seq // T
    assert seq % T == 0 and meta.shape[0] == N_META

    def body(x_ref, m_ref, t_ref, h_ref, tp_ref):
        i = pl.program_id(0)
        inside = (i >= 1) & (i <= nx)
        head = jnp.concatenate([jnp.zeros((PAD_ROWS, D), F32), m_ref[...]], axis=0)
        h_ref[...] = jnp.where(i == 0, head, jnp.where(inside, x_ref[...], 0.0))
        tp_ref[...] = jnp.where(inside, t_ref[...], 0.0)

    tok = pl.BlockSpec((T, D), lambda i: (jnp.clip(i - 1, 0, nx - 1), 0))
    row = pl.BlockSpec((T, D), lambda i: (i, 0))
    out = jax.ShapeDtypeStruct((LP, D), F32)
    return _call_carrying(
        carry, body, LP // T, name="build_rows",
        in_specs=[tok, pl.BlockSpec((N_META, D), lambda i: (0, 0)), tok], out_specs=[row, row], out_shape=[out, out],
        operands=(x, meta, target))


def _prenorm_fwd(h0, w, carry=None):
    LP, D = h0.shape
    T = _tile(LP, 512)

    def body(h_ref, w_ref, u_ref):
        h = h_ref[...]
        u_ref[...] = (h * _rms(h) * w_ref[...]).astype(BF16)

    return _call_carrying(
        carry, body, LP // T, name="prenorm_fwd",
        in_specs=[pl.BlockSpec((T, D), lambda i: (i, 0)), pl.BlockSpec((1, D), lambda i: (0, 0))],
        out_specs=[pl.BlockSpec((T, D), lambda i: (i, 0))],
        out_shape=[jax.ShapeDtypeStruct((LP, D), BF16)],
        operands=(h0, w))


def _prenorm_bwd(h0, w, du, dh1, carry=None):
    LP, D = h0.shape
    T = _tile(LP, 512)

    def body(h_ref, w_ref, du_ref, dh1_ref, dh0_ref, dw_ref):
        @pl.when(pl.program_id(0) == 0)
        def _():
            dw_ref[...] = jnp.zeros_like(dw_ref)
        dx, dwn = _rms_bwd(h_ref[...], w_ref[...], du_ref[...])
        dh0_ref[...] = dh1_ref[...] + dx
        dw_ref[...] += _colsum(dwn)

    row = pl.BlockSpec((T, D), lambda i: (i, 0))
    vec = pl.BlockSpec((1, D), lambda i: (0, 0))
    return _call_carrying(
        carry, body, LP // T, name="prenorm_bwd",
        in_specs=[row, vec, row, row], out_specs=[row, vec],
        out_shape=[jax.ShapeDtypeStruct((LP, D), F32), jax.ShapeDtypeStruct((1, D), F32)],
        operands=(h0, w, du, dh1))


def _gdn_gate_consts(alog_ref, dtb_ref, h):
    a_coef = -jnp.exp(alog_ref[0:1, h:h + 1])
    return a_coef, dtb_ref[0:1, h:h + 1]


def _gdn_pre_fwd(proj, conv_w, a_log, dt_bias, carry=None):
    LP = proj.shape[0]
    T = _tile(LP, 256)
    C = C_QKV
    H = GDN_HEADS

    def body(x_ref, halo_ref, ab_ref, cw_ref, alog_ref, dtb_ref, q_ref, k_ref, v_ref, beta_ref, g_ref):
        i = pl.program_id(0)

        def conv_silu(cols):
            ext = jnp.concatenate([jnp.where(i > 0, halo_ref[:, cols], 0.0), x_ref[:, cols]], axis=0)
            w = cw_ref[:, cols]
            y = w[GDN_CONV - 1:GDN_CONV] * ext[8:]
            for j in range(GDN_CONV - 1):
                y = y + w[j:j + 1] * pltpu.roll(ext, GDN_CONV - 1 - j, 0)[8:]
            return y * _sigmoid(y)

        for h in range(H):
            sl = slice(h * GDN_D, (h + 1) * GDN_D)
            cq = conv_silu(sl)
            q_ref[:, sl] = cq * lax.rsqrt(_rowsum(cq * cq) + L2_EPS) * (GDN_D ** -0.5)
            ck = conv_silu(slice(512 + h * GDN_D, 512 + (h + 1) * GDN_D))
            k_ref[:, sl] = ck * lax.rsqrt(_rowsum(ck * ck) + L2_EPS)
            v_ref[:, sl] = conv_silu(slice(1024 + h * GDN_D, 1024 + (h + 1) * GDN_D))
        ab = ab_ref[...]
        valid = (i * T + _iota2((T, 1), 0)) >= PAD_ROWS
        for h in range(H):
            sl = slice(h * GDN_D, (h + 1) * GDN_D)
            a_coef, dtb = _gdn_gate_consts(alog_ref, dtb_ref, h)
            g = jnp.where(valid, a_coef * _softplus(ab[:, h:h + 1] + dtb), 0.0)
            beta = jnp.where(valid, _sigmoid(ab[:, H + h:H + h + 1]), 0.0)
            g_ref[:, sl] = jnp.broadcast_to(g, (T, GDN_D))
            beta_ref[:, sl] = jnp.broadcast_to(beta, (T, GDN_D))

    t8 = T // 8
    row512 = pl.BlockSpec((T, 512), lambda i: (i, 0))
    small = lambda r, c: pl.BlockSpec((r, c), lambda i: (0, 0))
    out = jax.ShapeDtypeStruct((LP, 512), F32)
    return _call_carrying(
        carry, body, LP // T, name="gdn_pre_fwd",
        in_specs=[pl.BlockSpec((T, C), lambda i: (i, 0)),
                  pl.BlockSpec((8, C), lambda i: (jnp.maximum(i * t8 - 1, 0), 0)),
                  pl.BlockSpec((T, C_AB), lambda i: (i, OFF_AB // C_AB)),
                  small(GDN_CONV, C), small(1, H), small(1, H)],
        out_specs=[row512] * 5, out_shape=[out] * 5,
        operands=(proj, proj, proj, conv_w, a_log, dt_bias))


def _gdn_pre_bwd(proj, conv_w, a_log, dt_bias, dq, dk, dv, dbeta, dg, carry=None):
    LP = proj.shape[0]
    T = _tile(LP, 256)
    C = C_QKV
    H = GDN_HEADS
    TE = T + 8
    nt = LP // T

    def body(x_ref, xp_ref, xn_ref, ab_ref, cw_ref, alog_ref, dtb_ref,
             dq_ref, dqn_ref, dk_ref, dkn_ref, dv_ref, dvn_ref, dbeta_ref, dg_ref,
             dx_ref, dab_ref, dcw_ref, dsc_ref):
        i = pl.program_id(0)

        @pl.when(i == 0)
        def _():
            dcw_ref[...] = jnp.zeros_like(dcw_ref)
            dsc_ref[...] = jnp.zeros_like(dsc_ref)

        last = i == nt - 1

        def strip(cols, d_ref, dn_ref, dcols, scale):
            ext = jnp.concatenate([jnp.where(i > 0, xp_ref[:, cols], 0.0), x_ref[:, cols],
                                   jnp.where(last, 0.0, xn_ref[:, cols])], axis=0)
            sh = [ext[8:8 + TE]] + [pltpu.roll(ext, s, 0)[8:8 + TE] for s in range(1, GDN_CONV)]
            w = cw_ref[:, cols]
            y = w[GDN_CONV - 1:GDN_CONV] * sh[0]
            for j in range(GDN_CONV - 1):
                y = y + w[j:j + 1] * sh[GDN_CONV - 1 - j]
            sg = _sigmoid(y)
            d = jnp.concatenate([d_ref[:, dcols], jnp.where(last, 0.0, dn_ref[:, dcols])], axis=0)
            if scale is not None:
                c = y * sg
                r = lax.rsqrt(_rowsum(c * c) + L2_EPS)
                n = c * r
                d = scale * r * (d - n * _rowsum(d * n))
            dy = d * (sg * (1.0 + y * (1.0 - sg)))
            dy_t = dy[0:T]
            for j in range(GDN_CONV):
                dcw_ref[j:j + 1, cols] += _colsum(dy_t * sh[GDN_CONV - 1 - j][0:T])
            dx = w[GDN_CONV - 1:GDN_CONV] * dy_t
            for j in range(GDN_CONV - 1):
                dx = dx + w[j:j + 1] * pltpu.roll(dy, TE - (GDN_CONV - 1 - j), 0)[0:T]
            dx_ref[:, cols] = dx.astype(BF16)

        for h in range(H):
            sl = slice(h * GDN_D, (h + 1) * GDN_D)
            strip(sl, dq_ref, dqn_ref, sl, GDN_D ** -0.5)
            strip(slice(512 + h * GDN_D, 512 + (h + 1) * GDN_D), dk_ref, dkn_ref, sl, 1.0)
            strip(slice(1024 + h * GDN_D, 1024 + (h + 1) * GDN_D), dv_ref, dvn_ref, sl, None)
        ab = ab_ref[...]
        valid = (i * T + _iota2((T, 1), 0)) >= PAD_ROWS
        lane = _iota2((T, C_AB), 1)
        lane1 = _iota2((1, 128), 1)
        dab = jnp.zeros((T, C_AB), F32)
        dsc_a = jnp.zeros((1, 128), F32)
        dsc_d = jnp.zeros((1, 128), F32)
        for h in range(H):
            a_coef, dtb = _gdn_gate_consts(alog_ref, dtb_ref, h)
            pre = ab[:, h:h + 1] + dtb
            dgh = jnp.where(valid, dg_ref[:, h * GDN_D:h * GDN_D + 1], 0.0)
            da = dgh * a_coef * _sigmoid(pre)
            beta = _sigmoid(ab[:, H + h:H + h + 1])
            db = jnp.where(valid, dbeta_ref[:, h * GDN_D:h * GDN_D + 1], 0.0) * beta * (1.0 - beta)
            dab = dab + jnp.where(lane == h, da, 0.0) + jnp.where(lane == H + h, db, 0.0)
            dsc_a = dsc_a + jnp.where(lane1 == h, _colsum(dgh * a_coef * _softplus(pre)), 0.0)
            dsc_d = dsc_d + jnp.where(lane1 == h, _colsum(da), 0.0)
        dab_ref[...] = dab.astype(BF16)
        dsc_ref[0:1, :] += dsc_a
        dsc_ref[1:2, :] += dsc_d

    t8 = T // 8
    nb8 = LP // 8
    prev8 = lambda w: pl.BlockSpec((8, w), lambda i: (jnp.maximum(i * t8 - 1, 0), 0))
    next8 = lambda w: pl.BlockSpec((8, w), lambda i: (jnp.minimum((i + 1) * t8, nb8 - 1), 0))
    row = lambda w: pl.BlockSpec((T, w), lambda i: (i, 0))
    small = lambda r, c: pl.BlockSpec((r, c), lambda i: (0, 0))
    return _call_carrying(
        carry, body, nt, name="gdn_pre_bwd",
        in_specs=[row(C), prev8(C), next8(C), pl.BlockSpec((T, C_AB), lambda i: (i, OFF_AB // C_AB)),
                  small(GDN_CONV, C), small(1, H), small(1, H),
                  row(512), next8(512), row(512), next8(512), row(512), next8(512), row(512), row(512)],
        out_specs=[row(C), row(C_AB), small(GDN_CONV, C), small(2, 128)],
        out_shape=[jax.ShapeDtypeStruct((LP, C), BF16), jax.ShapeDtypeStruct((LP, C_AB), BF16),
                   jax.ShapeDtypeStruct((GDN_CONV, C), F32), jax.ShapeDtypeStruct((2, 128), F32)],
        operands=(proj, proj, proj, proj, conv_w, a_log, dt_bias, dq, dq, dk, dk, dv, dv, dbeta, dg))


def _tri_masks():
    r = _iota2((GDN_CHUNK, GDN_CHUNK), 0)
    c = _iota2((GDN_CHUNK, GDN_CHUNK), 1)
    return r >= c, r > c


def _gdn_chunk_common(q, k, v, beta, gb):
    incl, strict = _tri_masks()
    l_incl = incl.astype(BF16)
    gd = _dot_exact_l(l_incl, jnp.where(strict, gb[:, :GDN_CHUNK], 0.0))
    gc = _dot_exact_l(l_incl, gb)
    decay = jnp.where(incl, jnp.exp(jnp.where(incl, gd, 0.0)), 0.0)
    exp_g = jnp.exp(gc)
    g_last = gc[GDN_CHUNK - 1:GDN_CHUNK, :]
    kd_fac = jnp.exp(g_last - gc)
    gl = jnp.exp(g_last)
    kb = k * beta
    kk = _dot1(kb, k, _dot_nt)
    return dict(incl=incl, strict=strict, decay=decay, exp_g=exp_g, kd_fac=kd_fac, gl=gl, kb=kb, kk=kk,
                vb=v * beta, kbg=kb * exp_g)


def _interleave(gens):
    gens = list(gens)
    while gens:
        alive = []
        for g in gens:
            try:
                next(g)
                alive.append(g)
            except StopIteration:
                pass
        gens = alive


def _call_carrying(ex, body, nsteps, *, name, in_specs, out_specs, out_shape, operands, scratch_shapes=()):
    n_in, n_out, n_scr = len(in_specs), len(out_specs), len(scratch_shapes)
    n = ex.n if ex is not None else 0

    def full(*refs):
        o0 = n_in + n
        s0 = o0 + n_out + n
        ex_refs = (refs[n_in:o0], refs[o0 + n_out:s0], refs[s0 + n_scr:])
        step = pl.program_id(0)
        _carry_begin(ex, ex_refs, step, nsteps)
        body(*refs[:n_in], *refs[o0:o0 + n_out], *refs[s0:s0 + n_scr])
        _carry_end(ex, ex_refs, step, nsteps)

    res = pl.pallas_call(
        full, name=name, grid=(nsteps,),
        in_specs=list(in_specs) + [ANY_SPEC] * n, out_specs=list(out_specs) + [ANY_SPEC] * n,
        out_shape=list(out_shape) + (ex.out_shapes if ex is not None else []),
        scratch_shapes=list(scratch_shapes) + (ex.scratch if ex is not None else []),
        compiler_params=pltpu.CompilerParams(dimension_semantics=("arbitrary",), vmem_limit_bytes=VMEM_LIMIT,
                                             has_side_effects=ex is not None),
    )(*operands, *(ex.arrs if ex is not None else []))
    return list(res[:n_out]), list(res[n_out:])


def _gdn_chunk_fwd(qn, kn, v, beta_b, g_b, carry=None):
    LP = qn.shape[0]
    R = GDN_ROWS
    H = GDN_HEADS
    CH = GDN_CHUNK

    def body(q_ref, k_ref, v_ref, b_ref, g_ref, u_ref, w_ref, qd_ref, kd_ref, qk_ref, t_ref, gl_ref):
        def item(cc, h):
            rs = slice(cc * CH, (cc + 1) * CH)
            sl = slice(h * GDN_D, (h + 1) * GDN_D)
            s64 = slice(h * CH, (h + 1) * CH)
            q, k = q_ref[rs, sl], k_ref[rs, sl]
            m = _gdn_chunk_common(q, k, v_ref[rs, sl], b_ref[rs, sl], g_ref[rs, sl])
            qk_raw = _dot1(q, k, _dot_nt)
            yield
            a = jnp.where(m["strict"], m["kk"] * m["decay"], 0.0)
            eye = (_iota2((CH, CH), 0) == _iota2((CH, CH), 1)).astype(F32)
            t = eye - a
            p = _dot3(a, a)
            yield
            for _ in range(4):
                t = t + _dot3(t, p)
                p = _dot3(p, p)
                yield
            t = t + _dot3(t, p)
            yield
            u_ref[rs, sl] = _dot1(t, m["vb"])
            w_ref[rs, sl] = _dot1(t, m["kbg"])
            qk_ref[rs, s64] = qk_raw * m["decay"]
            t_ref[rs, s64] = t
            qd_ref[rs, sl] = q * m["exp_g"]
            kd_ref[rs, sl] = k * m["kd_fac"]
            gl_ref[cc * 8:(cc + 1) * 8, sl] = jnp.broadcast_to(m["gl"], (8, GDN_D))

        _interleave(item(cc, h) for cc in range(R // CH) for h in range(H))

    row = lambda w: pl.BlockSpec((R, w), lambda i: (i, 0))
    o512 = jax.ShapeDtypeStruct((LP, 512), F32)
    o256 = jax.ShapeDtypeStruct((LP, 256), F32)
    return _call_carrying(
        carry, body, LP // R, name="gdn_chunk_fwd",
        in_specs=[row(512)] * 5,
        out_specs=[row(512)] * 4 + [row(256)] * 2 + [pl.BlockSpec((R // 8, 512), lambda i: (i, 0))],
        out_shape=[o512] * 4 + [o256] * 2 + [jax.ShapeDtypeStruct((LP // 8, 512), F32)],
        operands=(qn, kn, v, beta_b, g_b))


def _gdn_chunk_bwd(qn, kn, v, beta_b, g_b, t_all, du, dw, dqd, dkd, dqk, dgl):
    LP = qn.shape[0]
    R = GDN_ROWS
    H = GDN_HEADS
    CH = GDN_CHUNK

    def body(q_ref, k_ref, v_ref, b_ref, g_ref, t_ref, du_ref, dw_ref, dqd_ref, dkd_ref, dqk_ref, dgl_ref,
             dq_ref, dk_ref, dv_ref, db_ref, dg_ref):
        ones = jnp.ones((CH, GDN_D), BF16)

        def item(cc, h):
            rs = slice(cc * CH, (cc + 1) * CH)
            sl = slice(h * GDN_D, (h + 1) * GDN_D)
            s64 = slice(h * CH, (h + 1) * CH)
            q, k, vv, beta = q_ref[rs, sl], k_ref[rs, sl], v_ref[rs, sl], b_ref[rs, sl]
            m = _gdn_chunk_common(q, k, vv, beta, g_ref[rs, sl])
            incl, strict, decay = m["incl"], m["strict"], m["decay"]
            t = t_ref[rs, s64]
            du_, dw_ = du_ref[rs, sl], dw_ref[rs, sl]
            dqd_, dkd_ = dqd_ref[rs, sl], dkd_ref[rs, sl]
            d_t = _dot1(du_, m["vb"], _dot_nt) + _dot1(dw_, m["kbg"], _dot_nt)
            dvb = _dot1(t, du_, _dot_tn)
            dkbg = _dot1(t, dw_, _dot_tn)
            qk_raw = _dot1(q, k, _dot_nt)
            yield
            x1 = _dot3(d_t, t, _dot_nt)
            dkb = dkbg * m["exp_g"]
            d_gi = _rowsum(dkbg * m["kbg"])
            yield
            d_a = jnp.where(strict, -_dot3(t, x1, _dot_tn), 0.0)
            yield
            d_kk = d_a * decay
            dqk_m = jnp.where(incl, dqk_ref[rs, s64], 0.0)
            dqk_raw = dqk_m * decay
            mm = (d_a * m["kk"] + dqk_m * qk_raw) * decay
            dkb = dkb + _dot1(d_kk, k)
            dk_ = _dot1(d_kk, m["kb"], _dot_tn) + _dot1(dqk_raw, q, _dot_tn)
            dq_ = _dot1(dqk_raw, k) + dqd_ * m["exp_g"]
            d_gi = d_gi + (_dot_exact_r(mm, ones) - _dot_exact_r(mm, ones, _dot_tn))
            yield
            d_gi = d_gi + _rowsum(dqd_ * q * m["exp_g"])
            e = _rowsum(dkd_ * k * m["kd_fac"])
            d_gi = d_gi - e
            d_glast = _colsum(jnp.broadcast_to(e, (CH, GDN_D))) + dgl_ref[cc * 8:cc * 8 + 1, sl] * m["gl"]
            dk_ = dk_ + dkd_ * m["kd_fac"] + dkb * beta
            d_gi = d_gi + jnp.where(_iota2((CH, GDN_D), 0) == CH - 1, d_glast, 0.0)
            u_incl = (_iota2((CH, CH), 1) >= _iota2((CH, CH), 0)).astype(BF16)
            dq_ref[rs, sl] = dq_
            dk_ref[rs, sl] = dk_
            dv_ref[rs, sl] = dvb * beta
            db_ref[rs, sl] = jnp.broadcast_to(_rowsum(dvb * vv) + _rowsum(dkb * k), (CH, GDN_D))
            dg_ref[rs, sl] = _dot_exact_l(u_incl, d_gi)

        _interleave(item(cc, h) for cc in range(R // CH) for h in range(H))

    row = lambda w: pl.BlockSpec((R, w), lambda i: (i, 0))
    o512 = jax.ShapeDtypeStruct((LP, 512), F32)
    gl_spec = pl.BlockSpec((R // 8, 512), lambda i: (i, 0))
    return pl.pallas_call(
        body, name="gdn_chunk_bwd", grid=(LP // R,),
        in_specs=[row(512)] * 5 + [row(256)] + [row(512)] * 4 + [row(256), gl_spec],
        out_specs=[row(512)] * 5, out_shape=[o512] * 5,
        compiler_params=_cp(("parallel",)),
    )(qn, kn, v, beta_b, g_b, t_all, du, dw, dqd, dkd, dqk, dgl)


def _gdn_scan_fwd(u, w, qd, kd, qk, gl):
    LP = u.shape[0]
    CH = GDN_CHUNK
    CPS = SCAN_CHUNKS
    N = LP // CH
    NS = N // CPS
    H = GDN_HEADS

    def body(u_ref, w_ref, qd_ref, kd_ref, qk_ref, gl_ref, o_ref, ssave_ref, s_sc):
        @pl.when(pl.program_id(0) == 0)
        def _():
            s_sc[...] = jnp.zeros_like(s_sc)

        for cc in range(CPS):
            rs = slice(cc * CH, (cc + 1) * CH)
            ssave_ref[cc * GDN_D:(cc + 1) * GDN_D, :] = s_sc[...]

            def item(h):
                sl = slice(h * GDN_D, (h + 1) * GDN_D)
                s = s_sc[:, sl]
                v_new = u_ref[rs, sl] - _dot1(w_ref[rs, sl], s)
                o_s = _dot1(qd_ref[rs, sl], s)
                yield
                o_ref[rs, sl] = o_s + _dot1(qk_ref[rs, h * CH:(h + 1) * CH], v_new)
                s_sc[:, sl] = s * gl_ref[cc * 8:cc * 8 + 1, sl] + _dot1(kd_ref[rs, sl], v_new, _dot_tn)

            _interleave(item(h) for h in range(H))

    row = lambda w_: pl.BlockSpec((CPS * CH, w_), lambda n: (n, 0))
    return pl.pallas_call(
        body, name="gdn_scan_fwd", grid=(NS,),
        in_specs=[row(512)] * 4 + [row(256), pl.BlockSpec((CPS * 8, 512), lambda n: (n, 0))],
        out_specs=[row(512), pl.BlockSpec((CPS * GDN_D, 512), lambda n: (n, 0))],
        out_shape=[jax.ShapeDtypeStruct((LP, 512), F32), jax.ShapeDtypeStruct((N * GDN_D, 512), F32)],
        scratch_shapes=[pltpu.VMEM((GDN_D, 512), F32)],
        compiler_params=_cp(("arbitrary",)),
    )(u, w, qd, kd, qk, gl)


def _gdn_scan_bwd(u, w, qd, kd, qk, gl, ssave, do, carry=None):
    LP = u.shape[0]
    CH = GDN_CHUNK
    CPS = SCAN_CHUNKS
    N = LP // CH
    NS = N // CPS
    H = GDN_HEADS

    def body(u_ref, w_ref, qd_ref, kd_ref, qk_ref, gl_ref, s_ref, do_ref,
             du_ref, dw_ref, dqd_ref, dkd_ref, dqk_ref, dgl_ref, ds_sc):
        @pl.when(pl.program_id(0) == 0)
        def _():
            ds_sc[...] = jnp.zeros_like(ds_sc)

        for cc in reversed(range(CPS)):
            rs = slice(cc * CH, (cc + 1) * CH)
            r8 = slice(cc * 8, (cc + 1) * 8)

            def item(h):
                sl = slice(h * GDN_D, (h + 1) * GDN_D)
                s64 = slice(h * CH, (h + 1) * CH)
                s = s_ref[cc * GDN_D:(cc + 1) * GDN_D, sl]
                ds = ds_sc[:, sl]
                do_ = do_ref[rs, sl]
                w_, qd_, kd_, qk_ = w_ref[rs, sl], qd_ref[rs, sl], kd_ref[rs, sl], qk_ref[rs, s64]
                v_new = u_ref[rs, sl] - _dot1(w_, s)
                d_vnew = _dot1(qk_, do_, _dot_tn) + _dot1(kd_, ds)
                dqd_ref[rs, sl] = _dot1(do_, s, _dot_nt)
                ds_new = ds * gl_ref[cc * 8:cc * 8 + 1, sl] + _dot1(qd_, do_, _dot_tn)
                dgl_ref[r8, sl] = jnp.broadcast_to(jnp.sum(_colsum(ds * s), axis=-1, keepdims=True), (8, GDN_D))
                yield
                du_ref[rs, sl] = d_vnew
                dw_ref[rs, sl] = -_dot1(d_vnew, s, _dot_nt)
                dkd_ref[rs, sl] = _dot1(v_new, ds, _dot_nt)
                dqk_ref[rs, s64] = _dot1(do_, v_new, _dot_nt)
                ds_sc[:, sl] = ds_new - _dot1(w_, d_vnew, _dot_tn)

            _interleave(item(h) for h in range(H))

    rev = lambda w_: pl.BlockSpec((CPS * CH, w_), lambda n: (NS - 1 - n, 0))
    rev8 = pl.BlockSpec((CPS * 8, 512), lambda n: (NS - 1 - n, 0))
    o512 = jax.ShapeDtypeStruct((LP, 512), F32)
    return _call_carrying(
        carry, body, NS, name="gdn_scan_bwd",
        in_specs=[rev(512)] * 4 + [rev(256), rev8, pl.BlockSpec((CPS * GDN_D, 512), lambda n: (NS - 1 - n, 0)),
                  rev(512)],
        out_specs=[rev(512)] * 4 + [rev(256), rev8],
        out_shape=[o512] * 4 + [jax.ShapeDtypeStruct((LP, 256), F32), jax.ShapeDtypeStruct((LP // 8, 512), F32)],
        scratch_shapes=[pltpu.VMEM((GDN_D, 512), F32)],
        operands=(u, w, qd, kd, qk, gl, ssave, do))


def _sb_scores(qh, kblk, mask):
    z = _dot_nt(qh, kblk)
    e = jnp.exp(-jnp.abs(z))
    sp = jnp.maximum(z, 0.0) + jnp.log(1.0 + e)
    return z, e, jnp.where(mask, -sp, 0.0), z - sp


def _sb_fwd(proj):
    LP = proj.shape[0]
    B = SB_BLOCK
    W = min(SB_SPAN, LP)
    SUB = SB_SUB
    Q = min(SB_QTILE, LP)
    nq = LP // Q
    nsub = W // SUB
    scale = SB_DH ** -0.5
    qcol, kcol, vcol = OFF_SB // B, (OFF_SB + 512) // B, (OFF_SB + 1024) // B

    def body(q_ref, k_ref, v_ref, tri_ref, o_ref, c_ref, n_ref):
        i = pl.program_id(1)
        lane = _iota2((Q, B), 1)
        head_a = lane < SB_DH
        qs = q_ref[...] * scale
        qh = [jnp.where(head_a, qs, 0.0).astype(BF16), jnp.where(head_a, 0.0, qs).astype(BF16)]
        u_strict = tri_ref[...]
        qpos = i * Q + _iota2((Q, W), 0)
        hi0 = (i + 1) * Q
        nspan = (hi0 + W - 1) // W

        def live(st):
            return (st[0] < nspan) & (st[1] > 0)

        def span(st):
            r, carry = st[0], st[2:]
            hi = hi0 - r * W
            k0 = pl.multiple_of(jnp.maximum(hi - W, 0), B)
            kblk = k_ref[pl.ds(k0, W), :].astype(BF16)
            vblk = v_ref[pl.ds(k0, W), :].astype(BF16)
            kpos = k0 + _iota2((Q, W), 1)
            mask = (kpos < qpos) & (kpos >= PAD_ROWS) & (kpos < hi)
            new = [None] * 4

            def head(h):
                o_acc, c = carry[2 * h], carry[2 * h + 1]
                z, e, l1m, lsg = _sb_scores(qh[h], kblk, mask)
                yield
                subs = [slice(b * SUB, (b + 1) * SUB) for b in range(nsub)]
                suf = [_dot(l1m[:, bs].astype(BF16), u_strict) for bs in subs]
                yield
                parts = [None] * nsub
                for b in reversed(range(nsub)):
                    parts[b] = jnp.where(mask[:, subs[b]], jnp.exp(lsg[:, subs[b]] + suf[b] + c), 0.0)
                    c = c + _rowsum(l1m[:, subs[b]])
                att = jnp.concatenate(parts, axis=1).astype(BF16)
                new[2 * h], new[2 * h + 1] = o_acc + _dot(att, vblk), c

            _interleave(head(h) for h in range(2))
            more = (jnp.maximum(jnp.max(new[1]), jnp.max(new[3])) > SB_DEAD).astype(jnp.int32)
            return (r + 1, more, *new)

        zero_o = jnp.zeros((Q, B), F32)
        zero_c = jnp.zeros((Q, 1), F32)
        nrun, _, o_a, c_a, o_b, c_b = lax.while_loop(
            live, span, (jnp.int32(0), jnp.int32(1), zero_o, zero_c, zero_o, zero_c))
        o_ref[...] = jnp.where(head_a, o_a, o_b)
        c_ref[...] = jnp.where(head_a, c_a, c_b)
        n_ref[pl.program_id(0), i] = nrun

    blk = pl.BlockSpec((Q, B), lambda p, i: (i, p))
    out = jax.ShapeDtypeStruct((LP, 512), F32)
    return pl.pallas_call(
        body, name="sb_fwd", grid=(SB_HEADS // 2, nq),
        in_specs=[pl.BlockSpec((Q, B), lambda p, i: (i, qcol + p)),
                  pl.BlockSpec((LP, B), lambda p, i: (0, kcol + p)),
                  pl.BlockSpec((LP, B), lambda p, i: (0, vcol + p)),
                  pl.BlockSpec((SUB, SUB), lambda p, i: (0, 0))],
        out_specs=[blk, blk, pl.BlockSpec(memory_space=pltpu.SMEM)],
        out_shape=[out, out, jax.ShapeDtypeStruct((SB_HEADS // 2, nq), jnp.int32)],
        compiler_params=_cp(("arbitrary", "arbitrary")),
    )(proj, proj, proj, jnp.tril(jnp.ones((SUB, SUB), BF16), -1))


def _sb_bwd(proj, ctot, nrun_all, do):
    LP = proj.shape[0]
    B = SB_BLOCK
    W = min(SB_SPAN, LP)
    SUB = SB_SUB
    Q = min(SB_QTILE, LP)
    nq = LP // Q
    nsub = W // SUB
    scale = SB_DH ** -0.5
    qcol, kcol, vcol = OFF_SB // B, (OFF_SB + 512) // B, (OFF_SB + 1024) // B

    def body(n_ref, q_ref, k_ref, v_ref, c_ref, do_ref, tril_ref, triu_ref, dq_ref, dk_ref, dv_ref):
        i = pl.program_id(1)

        @pl.when(i == 0)
        def _():
            dk_ref[...] = jnp.zeros_like(dk_ref)
            dv_ref[...] = jnp.zeros_like(dv_ref)

        lane = _iota2((Q, B), 1)
        head_a = lane < SB_DH
        qs = q_ref[...] * scale
        qh = [jnp.where(head_a, qs, 0.0).astype(BF16), jnp.where(head_a, 0.0, qs).astype(BF16)]
        dof = do_ref[...]
        doh = [jnp.where(head_a, dof, 0.0).astype(BF16), jnp.where(head_a, 0.0, dof).astype(BF16)]
        cfull = c_ref[...]
        ctot_h = [cfull[:, 0:1], cfull[:, SB_DH:SB_DH + 1]]
        u_strict = tril_ref[...]
        l_strict = triu_ref[...]
        qpos = i * Q + _iota2((Q, W), 0)
        hi0 = (i + 1) * Q
        nrun = n_ref[pl.program_id(0), i]

        def span(t, carry):
            r = nrun - 1 - t
            hi = hi0 - r * W
            k0 = pl.multiple_of(jnp.maximum(hi - W, 0), B)
            kblk = k_ref[pl.ds(k0, W), :].astype(BF16)
            vblk = v_ref[pl.ds(k0, W), :].astype(BF16)
            kpos = k0 + _iota2((Q, W), 1)
            mask = (kpos < qpos) & (kpos >= PAD_ROWS) & (kpos < hi)
            new = [None] * 6
            dk_add, dv_add = [None, None], [None, None]
            subs = [slice(b * SUB, (b + 1) * SUB) for b in range(nsub)]

            def head(h):
                dq_acc, pre, ecar = carry[3 * h], carry[3 * h + 1], carry[3 * h + 2]
                z, e, l1m, lsg = _sb_scores(qh[h], kblk, mask)
                d_att = _dot_nt(doh[h], vblk)
                yield
                sig = jnp.where(z >= 0.0, 1.0, e) / (1.0 + e)
                suf = [_dot(l1m[:, bs].astype(BF16), u_strict) for bs in subs]
                yield
                att_parts, p_parts = [None] * nsub, [None] * nsub
                for b, bs in enumerate(subs):
                    pre = pre + _rowsum(l1m[:, bs])
                    att_parts[b] = jnp.where(mask[:, bs], jnp.exp(lsg[:, bs] + suf[b] + (ctot_h[h] - pre)), 0.0)
                    p_parts[b] = att_parts[b] * d_att[:, bs]
                pcum = [_dot(p.astype(BF16), l_strict) for p in p_parts]
                yield
                dz_parts = [None] * nsub
                for b, bs in enumerate(subs):
                    sg = sig[:, bs]
                    dz_parts[b] = jnp.where(mask[:, bs], p_parts[b] * (1.0 - sg) - sg * (ecar + pcum[b]), 0.0)
                    ecar = ecar + _rowsum(p_parts[b])
                att = jnp.concatenate(att_parts, axis=1).astype(BF16)
                dz = jnp.concatenate(dz_parts, axis=1).astype(BF16)
                new[3 * h:3 * h + 3] = [dq_acc + _dot(dz, kblk), pre, ecar]
                dk_add[h] = _dot_tn(dz, qh[h])
                dv_add[h] = _dot_tn(att, doh[h])

            _interleave(head(h) for h in range(2))
            dk_ref[pl.ds(k0, W), :] += dk_add[0] + dk_add[1]
            dv_ref[pl.ds(k0, W), :] += dv_add[0] + dv_add[1]
            return tuple(new)

        zero_o = jnp.zeros((Q, B), F32)
        zero_c = jnp.zeros((Q, 1), F32)
        res = lax.fori_loop(0, nrun, span, (zero_o, zero_c, zero_c, zero_o, zero_c, zero_c))
        dq_ref[...] = (jnp.where(head_a, res[0], res[3]) * scale).astype(BF16)

    blk = pl.BlockSpec((Q, B), lambda p, i: (i, p))
    col = pl.BlockSpec((LP, B), lambda p, i: (0, p))
    tri = pl.BlockSpec((SUB, SUB), lambda p, i: (0, 0))
    out = jax.ShapeDtypeStruct((LP, 512), F32)
    return pl.pallas_call(
        body, name="sb_bwd", grid=(SB_HEADS // 2, nq),
        in_specs=[pl.BlockSpec(memory_space=pltpu.SMEM),
                  pl.BlockSpec((Q, B), lambda p, i: (i, qcol + p)),
                  pl.BlockSpec((LP, B), lambda p, i: (0, kcol + p)),
                  pl.BlockSpec((LP, B), lambda p, i: (0, vcol + p)),
                  blk, blk, tri, tri],
        out_specs=[blk, col, col], out_shape=[jax.ShapeDtypeStruct((LP, 512), BF16), out, out],
        compiler_params=_cp(("arbitrary", "arbitrary")),
    )(nrun_all, proj, proj, proj, ctot, do, jnp.tril(jnp.ones((SUB, SUB), BF16), -1),
      jnp.triu(jnp.ones((SUB, SUB), BF16), 1))


def _sb_group_mean():
    r = jnp.right_shift(_iota2((512, 512), 0), 6)
    c = jnp.right_shift(_iota2((512, 512), 1), 6)
    return jnp.where(r == c, 1.0 / SB_DH, 0.0).astype(BF16)


def _attn_norm_fwd(og, proj, osb, gnw, snw):
    LP = og.shape[0]
    T = _tile(LP, 256)

    def body(og_ref, z_ref, os_ref, gnw_ref, snw_ref, y_ref):
        valid = (pl.program_id(0) * T + _iota2((T, 1), 0)) >= PAD_ROWS
        z = z_ref[...]
        zg = z * _sigmoid(z)
        for h in range(GDN_HEADS):
            sl = slice(h * GDN_D, (h + 1) * GDN_D)
            o = og_ref[:, sl]
            y = o * _rms(o) * gnw_ref[...] * zg[:, sl]
            y_ref[:, sl] = jnp.where(valid, y, 0.0).astype(BF16)
        o = os_ref[...]
        msq = _dot_exact_r(o * o, _sb_group_mean())
        y = o * lax.rsqrt(msq + NORM_EPS) * snw_ref[...]
        y_ref[:, 512:] = jnp.where(valid, y, 0.0).astype(BF16)

    row = pl.BlockSpec((T, 512), lambda i: (i, 0))
    return pl.pallas_call(
        body, name="attn_norm_fwd", grid=(LP // T,),
        in_specs=[row, pl.BlockSpec((T, 512), lambda i: (i, OFF_Z // 512)), row,
                  pl.BlockSpec((1, GDN_D), lambda i: (0, 0)), pl.BlockSpec((1, 512), lambda i: (0, 0))],
        out_specs=pl.BlockSpec((T, 1024), lambda i: (i, 0)),
        out_shape=jax.ShapeDtypeStruct((LP, 1024), BF16),
        compiler_params=_cp(("parallel",)),
    )(og, proj, osb, gnw, snw)


def _attn_norm_bwd(og, proj, osb, gnw, snw, dy, carry=None):
    LP = og.shape[0]
    T = _tile(LP, 256)

    def body(og_ref, z_ref, os_ref, gnw_ref, snw_ref, dy_ref, dog_ref, dz_ref, dos_ref, dgw_ref, dsw_ref):
        @pl.when(pl.program_id(0) == 0)
        def _():
            dgw_ref[...] = jnp.zeros_like(dgw_ref)
            dsw_ref[...] = jnp.zeros_like(dsw_ref)
        valid = (pl.program_id(0) * T + _iota2((T, 1), 0)) >= PAD_ROWS
        dy = jnp.where(valid, dy_ref[...], 0.0)
        z = z_ref[...]
        sg = _sigmoid(z)
        zg = z * sg
        dgw = jnp.zeros((1, GDN_D), F32)
        for h in range(GDN_HEADS):
            sl = slice(h * GDN_D, (h + 1) * GDN_D)
            o = og_ref[:, sl]
            dyh = dy[:, sl]
            dx, dwn = _rms_bwd(o, gnw_ref[...], dyh * zg[:, sl])
            dog_ref[:, sl] = dx
            dgw = dgw + _colsum(dwn)
            yn = o * _rms(o) * gnw_ref[...]
            dz_ref[:, sl] = (dyh * yn * (sg[:, sl] * (1.0 + z[:, sl] * (1.0 - sg[:, sl])))).astype(BF16)
        dgw_ref[...] += dgw
        o = os_ref[...]
        gm = _sb_group_mean()
        r = lax.rsqrt(_dot_exact_r(o * o, gm) + NORM_EPS)
        n = o * r
        dys = dy[:, 512:]
        dyw = dys * snw_ref[...]
        dos_ref[...] = r * (dyw - n * _dot_exact_r(dyw * n, gm))
        dsw_ref[...] += _colsum(dys * n)

    row = pl.BlockSpec((T, 512), lambda i: (i, 0))
    gw = pl.BlockSpec((1, GDN_D), lambda i: (0, 0))
    sw = pl.BlockSpec((1, 512), lambda i: (0, 0))
    o512 = jax.ShapeDtypeStruct((LP, 512), F32)
    return _call_carrying(
        carry, body, LP // T, name="attn_norm_bwd",
        in_specs=[row, pl.BlockSpec((T, 512), lambda i: (i, OFF_Z // 512)), row, gw, sw,
                  pl.BlockSpec((T, 1024), lambda i: (i, 0))],
        out_specs=[row, row, row, gw, sw],
        out_shape=[o512, jax.ShapeDtypeStruct((LP, 512), BF16), o512, jax.ShapeDtypeStruct((1, GDN_D), F32),
                   jax.ShapeDtypeStruct((1, 512), F32)],
        operands=(og, proj, osb, gnw, snw, dy))


def _resid_fwd(h0, mix, w_post, w_pre):
    LP, D = h0.shape
    T = _tile(LP, 512)

    def body(h0_ref, mix_ref, wp_ref, wf_ref, h1_ref, n2_ref):
        mix = mix_ref[...]
        h1 = h0_ref[...] + mix * _rms(mix) * wp_ref[...]
        h1_ref[...] = h1
        n2_ref[...] = (h1 * _rms(h1) * wf_ref[...]).astype(BF16)

    row = pl.BlockSpec((T, D), lambda i: (i, 0))
    vec = pl.BlockSpec((1, D), lambda i: (0, 0))
    return pl.pallas_call(
        body, name="resid_fwd", grid=(LP // T,),
        in_specs=[row, row, vec, vec], out_specs=[row, row],
        out_shape=[jax.ShapeDtypeStruct((LP, D), F32), jax.ShapeDtypeStruct((LP, D), BF16)],
        compiler_params=_cp(("parallel",)),
    )(h0, mix, w_post, w_pre)


def _resid_bwd(h1, mix, w_post, w_pre, dout, dn2):
    LP, D = h1.shape
    T = _tile(LP, 512)

    def body(h1_ref, mix_ref, wp_ref, wf_ref, dout_ref, dn2_ref, dh1_ref, dmix_ref, dwf_ref, dwp_ref):
        @pl.when(pl.program_id(0) == 0)
        def _():
            dwf_ref[...] = jnp.zeros_like(dwf_ref)
            dwp_ref[...] = jnp.zeros_like(dwp_ref)
        dx, dwn = _rms_bwd(h1_ref[...], wf_ref[...], dn2_ref[...])
        dh1 = dout_ref[...] + dx
        dh1_ref[...] = dh1
        dwf_ref[...] += _colsum(dwn)
        dmix, dwn2 = _rms_bwd(mix_ref[...], wp_ref[...], dh1)
        dmix_ref[...] = dmix.astype(BF16)
        dwp_ref[...] += _colsum(dwn2)

    row = pl.BlockSpec((T, D), lambda i: (i, 0))
    vec = pl.BlockSpec((1, D), lambda i: (0, 0))
    v = jax.ShapeDtypeStruct((1, D), F32)
    return pl.pallas_call(
        body, name="resid_bwd", grid=(LP // T,),
        in_specs=[row, row, vec, vec, row, row], out_specs=[row, row, vec, vec],
        out_shape=[jax.ShapeDtypeStruct((LP, D), F32), jax.ShapeDtypeStruct((LP, D), BF16), v, v],
        compiler_params=_cp(("arbitrary",)),
    )(h1, mix, w_post, w_pre, dout, dn2)


GELU_C = 0.7978845608028654
GELU_A = 0.044715


def _gelu_parts(x):
    t = jnp.tanh(GELU_C * (x + GELU_A * x * x * x))
    return 0.5 * x * (1.0 + t), t


def _convglu_fwd(up, conv_w, conv_b):
    LP, C = up.shape
    T = _tile(LP, 128)

    def body(x_ref, halo_ref, cw_ref, cb_ref, act_ref, y_ref):
        i = pl.program_id(0)

        def conv(cols):
            ext = jnp.concatenate([jnp.where(i > 0, halo_ref[:, cols], 0.0), x_ref[:, cols]], axis=0)
            w = cw_ref[:, cols]
            y = (w[2:3] * ext[8:] + w[1:2] * pltpu.roll(ext, 1, 0)[8:] + w[0:1] * pltpu.roll(ext, 2, 0)[8:]
                 + cb_ref[:, cols])
            y_ref[:, cols] = y
            return y

        for s in range(D_FF // LANE):
            gs = slice(s * LANE, (s + 1) * LANE)
            g, _ = _gelu_parts(conv(gs))
            act_ref[:, gs] = (g * conv(slice(D_FF + s * LANE, D_FF + (s + 1) * LANE))).astype(BF16)

    t8 = T // 8
    return pl.pallas_call(
        body, name="convglu_fwd", grid=(LP // T,),
        in_specs=[pl.BlockSpec((T, C), lambda i: (i, 0)),
                  pl.BlockSpec((8, C), lambda i: (jnp.maximum(i * t8 - 1, 0), 0)),
                  pl.BlockSpec((FFN_CONV, C), lambda i: (0, 0)), pl.BlockSpec((1, C), lambda i: (0, 0))],
        out_specs=[pl.BlockSpec((T, D_FF), lambda i: (i, 0)), pl.BlockSpec((T, C), lambda i: (i, 0))],
        out_shape=[jax.ShapeDtypeStruct((LP, D_FF), BF16), jax.ShapeDtypeStruct((LP, C), F32)],
        compiler_params=_cp(("parallel",)),
    )(up, up, conv_w, conv_b)


def _convglu_bwd(up, y, conv_w, dact):
    LP, C = up.shape
    T = _tile(LP, 128)
    TE = T + 8
    nt = LP // T

    def body(x_ref, y_ref, yn_ref, cw_ref, da_ref, dan_ref, dx_ref, dcw_ref, dcb_ref):
        i = pl.program_id(0)

        @pl.when(i == 0)
        def _():
            dcw_ref[...] = jnp.zeros_like(dcw_ref)
            dcb_ref[...] = jnp.zeros_like(dcb_ref)

        last = i == nt - 1

        def back(cols, dy):
            w = cw_ref[:, cols]
            later = [dy[0:T], pltpu.roll(dy, TE - 1, 0)[0:T], pltpu.roll(dy, TE - 2, 0)[0:T]]
            x_t = x_ref[:, cols]
            dcb_ref[:, cols] += _colsum(later[0])
            for j in range(FFN_CONV):
                dcw_ref[j:j + 1, cols] += _colsum(later[FFN_CONV - 1 - j] * x_t)
            dx_ref[:, cols] = (w[2:3] * later[0] + w[1:2] * later[1] + w[0:1] * later[2]).astype(BF16)

        for s in range(D_FF // LANE):
            gs = slice(s * LANE, (s + 1) * LANE)
            vs = slice(D_FF + s * LANE, D_FF + (s + 1) * LANE)
            gate = jnp.concatenate([y_ref[:, gs], yn_ref[:, gs]], axis=0)
            val = jnp.concatenate([y_ref[:, vs], yn_ref[:, vs]], axis=0)
            g, t = _gelu_parts(gate)
            dg_dx = 0.5 * (1.0 + t) + 0.5 * gate * (1.0 - t * t) * GELU_C * (1.0 + 3.0 * GELU_A * gate * gate)
            da = jnp.concatenate([da_ref[:, gs], jnp.where(last, 0.0, dan_ref[:, gs])], axis=0)
            back(gs, da * val * dg_dx)
            back(vs, da * g)

    t8 = T // 8
    nb8 = LP // 8
    next8 = lambda w: pl.BlockSpec((8, w), lambda i: (jnp.minimum((i + 1) * t8, nb8 - 1), 0))
    row = lambda w: pl.BlockSpec((T, w), lambda i: (i, 0))
    small = lambda r: pl.BlockSpec((r, C), lambda i: (0, 0))
    return pl.pallas_call(
        body, name="convglu_bwd", grid=(nt,),
        in_specs=[row(C), row(C), next8(C), small(FFN_CONV), row(D_FF), next8(D_FF)],
        out_specs=[row(C), small(FFN_CONV), small(1)],
        out_shape=[jax.ShapeDtypeStruct((LP, C), BF16), jax.ShapeDtypeStruct((FFN_CONV, C), F32),
                   jax.ShapeDtypeStruct((1, C), F32)],
        compiler_params=_cp(("arbitrary",)),
    )(up, y, y, conv_w, dact, dact)


def _final(h1, f, w_post, target, n_real):
    LP, D = h1.shape
    T = _tile(LP, 256)

    def body(h1_ref, f_ref, w_ref, t_ref, loss_ref, dout_ref, df_ref, dw_ref):
        @pl.when(pl.program_id(0) == 0)
        def _():
            loss_ref[...] = jnp.zeros_like(loss_ref)
            dw_ref[...] = jnp.zeros_like(dw_ref)
        rows = pl.program_id(0) * T + _iota2((T, 1), 0)
        real = (rows >= ROW0) & (rows < ROW0 + n_real)
        f = f_ref[...]
        out = h1_ref[...] + f * _rms(f) * w_ref[...]
        err = jnp.where(real, out - t_ref[...], 0.0)
        loss_ref[...] += 0.5 * jnp.sum(_colsum(jnp.mean(err * err, axis=-1, keepdims=True)), axis=-1, keepdims=True)
        dout = err * (1.0 / D)
        dout_ref[...] = dout
        dx, dwn = _rms_bwd(f, w_ref[...], dout)
        df_ref[...] = dx.astype(BF16)
        dw_ref[...] += _colsum(dwn)

    row = pl.BlockSpec((T, D), lambda i: (i, 0))
    vec = pl.BlockSpec((1, D), lambda i: (0, 0))
    return pl.pallas_call(
        body, name="final_loss", grid=(LP // T,),
        in_specs=[row, row, vec, row],
        out_specs=[pl.BlockSpec((1, 128), lambda i: (0, 0)), row, row, vec],
        out_shape=[jax.ShapeDtypeStruct((1, 128), F32), jax.ShapeDtypeStruct((LP, D), F32),
                   jax.ShapeDtypeStruct((LP, D), BF16), jax.ShapeDtypeStruct((1, D), F32)],
        compiler_params=_cp(("arbitrary",)),
    )(h1, f, w_post, target)


ANY_SPEC = pl.BlockSpec(memory_space=pl.ANY)
N_CHIP = 4


def _other_chips(x, y):
    return [(1 - x, y), (x, 1 - y), (1 - x, 1 - y)]


def _gather_direct(arrs, name):
    n = len(arrs)
    npeer = N_DEV - 1

    def body(*refs):
        ins, outs = refs[:n], refs[n:2 * n]
        send_sems, recv_sems, loc_sems = refs[2 * n:]
        x, y, c = lax.axis_index("x"), lax.axis_index("y"), lax.axis_index("c")
        me = 4 * x + 2 * y + c
        copies = []
        for a in range(n):
            for kk in range(1, N_DEV):
                px = 1 - x if kk & 4 else x
                py = 1 - y if kk & 2 else y
                pc = 1 - c if kk & 1 else c
                s = a * npeer + kk - 1
                cp = pltpu.make_async_remote_copy(src_ref=ins[a], dst_ref=outs[a].at[me], send_sem=send_sems.at[s],
                                                  recv_sem=recv_sems.at[s], device_id=(px, py, pc), device_id_type=MESH)
                cp.start()
                copies.append(cp)
            own = pltpu.make_async_copy(ins[a], outs[a].at[me], loc_sems.at[a])
            own.start()
            copies.append(own)
        for cp in copies:
            cp.wait()

    shapes = [jax.ShapeDtypeStruct((N_DEV,) + tuple(a.shape), a.dtype) for a in arrs]
    return pl.pallas_call(
        body, name=name, in_specs=[ANY_SPEC] * n, out_specs=[ANY_SPEC] * n, out_shape=shapes,
        scratch_shapes=[pltpu.SemaphoreType.DMA((n * npeer,)), pltpu.SemaphoreType.DMA((n * npeer,)),
                        pltpu.SemaphoreType.DMA((n,))],
        compiler_params=pltpu.CompilerParams(has_side_effects=True),
    )(*arrs)


class _Exchange:
    def __init__(self, arrs, out_shapes, scratch, start, finish, mid=None):
        self.arrs, self.out_shapes, self.scratch = list(arrs), list(out_shapes), list(scratch)
        self.start, self.finish, self.mid = start, finish, mid

    @property
    def n(self):
        return len(self.arrs)


def _run_exchange(ex, name):
    n = ex.n

    def body(*refs):
        ins, outs, sems = refs[:n], refs[n:2 * n], refs[2 * n:]
        ex.start(ins, outs, sems)
        if ex.mid is not None:
            ex.mid(ins, outs, sems)
        ex.finish(ins, outs, sems)

    return pl.pallas_call(
        body, name=name, in_specs=[ANY_SPEC] * n, out_specs=[ANY_SPEC] * n, out_shape=ex.out_shapes,
        scratch_shapes=ex.scratch, compiler_params=pltpu.CompilerParams(has_side_effects=True),
    )(*ex.arrs)


def _carry_begin(ex, refs, step, nsteps):
    if ex is None:
        return

    @pl.when(step == 0)
    def _():
        ex.start(*refs)

    if ex.mid is not None:
        @pl.when(step == min(nsteps - 1, (3 * nsteps) // 5))
        def _():
            ex.mid(*refs)


def _carry_end(ex, refs, step, nsteps):
    if ex is None:
        return

    @pl.when(step == nsteps - 1)
    def _():
        ex.finish(*refs)


def _gather_two_level(arrs):
    n = len(arrs)
    K = 7

    def env(ins, outs, sems):
        send_sems, recv_sems, loc_sems = sems
        x, y, c = lax.axis_index("x"), lax.axis_index("y"), lax.axis_index("c")

        def cp(a, k, src, slot, to):
            return pltpu.make_async_remote_copy(src_ref=src, dst_ref=outs[a].at[slot], send_sem=send_sems.at[a * K + k],
                                                recv_sem=recv_sems.at[a * K + k], device_id=to, device_id_type=MESH)

        me = 4 * x + 2 * y + c
        owns = [pltpu.make_async_copy(ins[a], outs[a].at[me], loc_sems.at[a]) for a in range(n)]
        first = []
        for a in range(n):
            first.append(cp(a, 0, ins[a], me, (x, y, 1 - c)))
            first += [cp(a, 1 + j, ins[a], me, (px, py, c)) for j, (px, py) in enumerate(_other_chips(x, y))]
        passed = []
        for j, (px, py) in enumerate(_other_chips(x, y)):
            slot = 4 * px + 2 * py + c
            passed += [(cp(a, 1 + j, ins[a], slot, (px, py, c)), cp(a, 4 + j, outs[a].at[slot], slot, (x, y, 1 - c)))
                       for a in range(n)]
        from_sib = []
        for a in range(n):
            from_sib.append(cp(a, 0, ins[a], 4 * x + 2 * y + (1 - c), (x, y, 1 - c)))
            from_sib += [cp(a, 4 + j, ins[a], 4 * px + 2 * py + (1 - c), (x, y, 1 - c))
                         for j, (px, py) in enumerate(_other_chips(x, y))]
        return owns, first, passed, from_sib

    def start(ins, outs, sems):
        owns, first, _, _ = env(ins, outs, sems)
        for cp in owns + first:
            cp.start()

    def mid(ins, outs, sems):
        _, _, passed, _ = env(ins, outs, sems)
        for arrival, fwd in passed:
            arrival.wait_recv()
            fwd.start()

    def finish(ins, outs, sems):
        owns, first, passed, from_sib = env(ins, outs, sems)
        for cp in from_sib:
            cp.wait_recv()
        for cp in first + [fwd for _, fwd in passed]:
            cp.wait_send()
        for cp in owns:
            cp.wait()

    shapes = [jax.ShapeDtypeStruct((N_DEV,) + tuple(a.shape), a.dtype) for a in arrs]
    scratch = [pltpu.SemaphoreType.DMA((n * K,)), pltpu.SemaphoreType.DMA((n * K,)), pltpu.SemaphoreType.DMA((n,))]
    return _Exchange(arrs, shapes, scratch, start, finish, mid)


def _swap_sibling(arrs):
    n = len(arrs)

    def copies(ins, outs, sems):
        send_sems, recv_sems = sems
        x, y, c = lax.axis_index("x"), lax.axis_index("y"), lax.axis_index("c")
        return [pltpu.make_async_remote_copy(src_ref=ins[a], dst_ref=outs[a], send_sem=send_sems.at[a],
                                             recv_sem=recv_sems.at[a], device_id=(x, y, 1 - c), device_id_type=MESH)
                for a in range(n)]

    def start(ins, outs, sems):
        for cp in copies(ins, outs, sems):
            cp.start()

    def finish(ins, outs, sems):
        for cp in copies(ins, outs, sems):
            cp.wait()

    shapes = [jax.ShapeDtypeStruct(tuple(a.shape), a.dtype) for a in arrs]
    return _Exchange(arrs, shapes, [pltpu.SemaphoreType.DMA((n,)), pltpu.SemaphoreType.DMA((n,))], start, finish)


def _exchange_chips(arrs):
    n = len(arrs)
    K = N_CHIP - 1

    def copies(ins, outs, sems):
        send_sems, recv_sems, loc_sems = sems
        x, y, c = lax.axis_index("x"), lax.axis_index("y"), lax.axis_index("c")
        mine = 2 * x + y
        out = []
        for a in range(n):
            out += [pltpu.make_async_remote_copy(src_ref=ins[a].at[2 * px + py], dst_ref=outs[a].at[mine],
                                                 send_sem=send_sems.at[a * K + j], recv_sem=recv_sems.at[a * K + j],
                                                 device_id=(px, py, c), device_id_type=MESH)
                    for j, (px, py) in enumerate(_other_chips(x, y))]
            out.append(pltpu.make_async_copy(ins[a].at[mine], outs[a].at[mine], loc_sems.at[a]))
        return out

    def start(ins, outs, sems):
        for cp in copies(ins, outs, sems):
            cp.start()

    def finish(ins, outs, sems):
        for cp in copies(ins, outs, sems):
            cp.wait()

    shapes = [jax.ShapeDtypeStruct(tuple(a.shape), a.dtype) for a in arrs]
    scratch = [pltpu.SemaphoreType.DMA((n * K,)), pltpu.SemaphoreType.DMA((n * K,)), pltpu.SemaphoreType.DMA((n,))]
    return _Exchange(arrs, shapes, scratch, start, finish)


def _add_halves(mine, theirs, name):
    _, R, C = mine.shape
    cap = max(16, (ELEMWISE_VMEM // (4 * C * 10)) // 16 * 16)
    T = R if R <= cap else _tile(R, cap, 16)

    def body(a_ref, b_ref, o_ref):
        o_ref[...] = (a_ref[...] + b_ref[...].astype(F32)).astype(BF16)

    blk = pl.BlockSpec((N_CHIP, T, C), lambda i: (0, i, 0))
    return pl.pallas_call(
        body, name=name, grid=(R // T,), in_specs=[blk, blk], out_specs=blk,
        out_shape=jax.ShapeDtypeStruct(mine.shape, BF16), compiler_params=_cp(("parallel",)),
    )(mine, theirs)


def _adamw(parts, w, m, v, name):
    R, C = w.shape
    npart = parts.shape[0]
    cap = max(16, (ELEMWISE_VMEM // (4 * C * 12)) // 16 * 16)
    T = R if R <= cap else _tile(R, cap, 16)

    def body(p_ref, w_ref, m_ref, v_ref, g_ref, d_ref, nm_ref, nv_ref):
        g = p_ref[0].astype(F32)
        for k in range(1, npart):
            g = g + p_ref[k].astype(F32)
        mm = ADAM_B1 * m_ref[...] + (1.0 - ADAM_B1) * g
        vv = ADAM_B2 * v_ref[...] + (1.0 - ADAM_B2) * (g * g)
        m_hat = mm / (1.0 - ADAM_B1 ** ADAM_STEP)
        v_hat = vv / (1.0 - ADAM_B2 ** ADAM_STEP)
        g_ref[...] = g
        d_ref[...] = -ADAM_LR * (m_hat / (jnp.sqrt(v_hat) + ADAM_EPS) + ADAM_WD * w_ref[...])
        nm_ref[...] = mm
        nv_ref[...] = vv

    row = pl.BlockSpec((T, C), lambda i: (i, 0))
    out = jax.ShapeDtypeStruct((R, C), F32)
    return pl.pallas_call(
        body, name=name, grid=(R // T,),
        in_specs=[pl.BlockSpec((npart, T, C), lambda i: (0, i, 0)), row, row, row],
        out_specs=[row] * 4, out_shape=[out] * 4,
        compiler_params=_cp(("parallel",)),
    )(parts, w, m, v)


SMALL = ("attn_pre_norm", "gdn_A_log", "gdn_dt_bias", "gdn_norm_w", "sb_norm_w", "attn_post_norm",
         "ffn_pre_norm", "ffn_conv_b", "ffn_post_norm")


def _pack_small(arrs):
    rows = []
    for a in arrs:
        flat = a.reshape(-1).astype(F32)
        n = -(-flat.shape[0] // 128) * 128
        rows.append(jnp.pad(flat, (0, n - flat.shape[0])).reshape(-1, 128))
    slab = jnp.concatenate(rows, axis=0)
    pad = (-slab.shape[0]) % 8
    return jnp.pad(slab, ((0, pad), (0, 0)))


def _unpack_small(slab, shapes):
    out, r = [], 0
    for shp in shapes:
        size = 1
        for s in shp:
            size *= s
        nr = -(-size // 128)
        out.append(slab[r:r + nr].reshape(-1)[:size].reshape(shp))
        r += nr
    return out


def _to_blocks_cols(a):
    R, C = a.shape
    return a.reshape(R, N_DEV, C // N_DEV).transpose(1, 0, 2)


def _from_blocks_cols(a):
    n, R, c = a.shape
    return a.transpose(1, 0, 2).reshape(R, n * c)


def kernel(x, meta_tokens, attn_pre_norm, w_in, gdn_conv_w, gdn_A_log, gdn_dt_bias, gdn_norm_w, sb_norm_w, w_out, attn_post_norm, ffn_pre_norm, w_ffn_up, ffn_conv_w, ffn_conv_b, w_ffn_down, ffn_post_norm, loss_target, m_meta_tokens, m_attn_pre_norm, m_w_in, m_gdn_conv_w, m_gdn_A_log, m_gdn_dt_bias, m_gdn_norm_w, m_sb_norm_w, m_w_out, m_attn_post_norm, m_ffn_pre_norm, m_w_ffn_up, m_ffn_conv_w, m_ffn_conv_b, m_w_ffn_down, m_ffn_post_norm, v_meta_tokens, v_attn_pre_norm, v_w_in, v_gdn_conv_w, v_gdn_A_log, v_gdn_dt_bias, v_gdn_norm_w, v_sb_norm_w, v_w_out, v_attn_post_norm, v_ffn_pre_norm, v_w_ffn_up, v_ffn_conv_w, v_ffn_conv_b, v_w_ffn_down, v_ffn_post_norm):
    args = dict(locals())
    seq = x.shape[1]
    LP = -(-(ROW0 + seq) // LP_ALIGN) * LP_ALIGN
    tail = LP - ROW0 - seq

    meta_f = _from_blocks_cols(_run_exchange(_gather_two_level([meta_tokens]), "gather_meta")[0])

    (h0, target), got = _build_rows(x[0], meta_f, loss_target[0], LP,
                                    carry=_gather_two_level([w_in[0].astype(BF16), gdn_conv_w[0]]))
    (u,), _ = _prenorm_fwd(h0, attn_pre_norm)
    win_o = _from_blocks_cols(got[0])
    o_ab = C_QKV
    o_z = o_ab + 2 * GDN_HEADS
    w_inp = jnp.concatenate([win_o[:, :C_QKV], win_o[:, o_z:o_z + C_Z], win_o[:, o_z + C_Z:],
                             win_o[:, o_ab:o_z], jnp.zeros((D_MODEL, C_AB - 2 * GDN_HEADS), BF16)], axis=1)
    gconv_f = _from_blocks_cols(got[1])
    proj = _mm(u, w_inp, F32, "mm_in")
    (qn, kn, vg, beta_b, g_b), got = _gdn_pre_fwd(
        proj, gconv_f, gdn_A_log, gdn_dt_bias,
        carry=_gather_two_level([w_out[0].astype(BF16), w_ffn_down[0].astype(BF16)]))
    w_out_f = got[0].reshape(D_MODEL, D_MODEL)
    w_down_f = got[1].reshape(D_FF, D_MODEL)
    (cu, cw, cqd, ckd, cqk, ct, cgl), got = _gdn_chunk_fwd(
        qn, kn, vg, beta_b, g_b, carry=_gather_two_level([w_ffn_up[0].astype(BF16), ffn_conv_w[0]]))
    w_up_f = _from_blocks_cols(got[0])
    fconv_f = _from_blocks_cols(got[1])
    og, ssave = _gdn_scan_fwd(cu, cw, cqd, ckd, cqk, cgl)
    osb, ctot, sb_nrun = _sb_fwd(proj)
    snw = sb_norm_w.reshape(1, SB_HEADS * SB_DH)
    y = _attn_norm_fwd(og, proj, osb, gdn_norm_w, snw)
    mix = _mm(y, w_out_f, F32, "mm_out")
    h1, n2 = _resid_fwd(h0, mix, attn_post_norm, ffn_pre_norm)
    up = _mm(n2, w_up_f, F32, "mm_up")
    act, conv_y = _convglu_fwd(up, fconv_f, ffn_conv_b)
    f = _mm(act, w_down_f, F32, "mm_down")
    loss_part, dout, df, d_fpost = _final(h1, f, ffn_post_norm, target, seq)
    loss = lax.psum(loss_part[0, 0], ("x", "y", "c"))

    d_wdown = _mm_tn(act, df, "mm_dw_down")
    dact = _mm_nt(df, w_down_f, F32, "mm_dact")
    dup, d_fconv, d_fconvb = _convglu_bwd(up, conv_y, fconv_f, dact)
    d_wup = _mm_tn(n2, dup, "mm_dw_up")
    dn2 = _mm_nt(dup, w_up_f, F32, "mm_dn2")
    dh1, dmix, d_fpre, d_apost = _resid_bwd(h1, mix, attn_post_norm, ffn_pre_norm, dout, dn2)
    d_wout = _mm_tn(y, dmix, "mm_dw_out")
    dy = _mm_nt(dmix, w_out_f, F32, "mm_dy")
    my_c = lax.axis_index("c")

    def core_halves(blocks):
        halves = [s.reshape((N_CHIP, 2) + s.shape[1:]) for s in blocks]
        return ([lax.dynamic_index_in_dim(h, my_c, axis=1, keepdims=False) for h in halves],
                [lax.dynamic_index_in_dim(h, 1 - my_c, axis=1, keepdims=False).astype(BF16) for h in halves])

    early_names = ("w_out", "w_ffn_up", "w_ffn_down", "ffn_conv_w")
    e_mine, e_send = core_halves([d_wout.reshape(N_DEV, D_MODEL // N_DEV, D_MODEL), _to_blocks_cols(d_wup),
                                  d_wdown.reshape(N_DEV, D_FF // N_DEV, D_MODEL), _to_blocks_cols(d_fconv)])
    (dog, dz, dos, d_gnw, d_snw), e_theirs = _attn_norm_bwd(og, proj, osb, gdn_norm_w, snw, dy,
                                                            carry=_swap_sibling(e_send))
    e_sums = [_add_halves(a, b, "grads_add_" + nm) for nm, a, b in zip(early_names, e_mine, e_theirs)]
    dqs, dks, dvs = _sb_bwd(proj, ctot, sb_nrun, dos)
    (du_, dw_, dqd_, dkd_, dqk_, dgl_), _ = _gdn_scan_bwd(cu, cw, cqd, ckd, cqk, cgl, ssave, dog)
    dqn, dkn, dvg, dbeta, dg = _gdn_chunk_bwd(qn, kn, vg, beta_b, g_b, ct, du_, dw_, dqd_, dkd_, dqk_, dgl_)
    (dqkv, dab, d_gconv, d_gsc), e_recv = _gdn_pre_bwd(proj, gconv_f, gdn_A_log, gdn_dt_bias, dqn, dkn, dvg, dbeta, dg,
                                                       carry=_exchange_chips(e_sums))
    dpieces = [dqkv, dz, dqs, dks, dvs, dab]
    doffs = [0, OFF_Z, OFF_SB, OFF_SB + 512, OFF_SB + 1024, OFF_AB]
    dw_qkv, dw_ab = _mm_tn_pieces(u, [dqkv, dab], "mm_dw_in_gdn")
    dw_z, dw_qs, dw_ks, dw_vs = _mm_tn_pieces(u, [dz, dqs, dks, dvs], "mm_dw_in_rest")
    du0 = _mm_nt_pieces(dpieces, doffs, w_inp, F32, "mm_du")
    d_win = jnp.concatenate([dw_qkv, dw_ab[:, :2 * GDN_HEADS], dw_z, dw_qs, dw_ks, dw_vs], axis=1)
    late_names = ("w_in", "gdn_conv_w")
    l_mine, l_send = core_halves([_to_blocks_cols(d_win), _to_blocks_cols(d_gconv)])
    l_theirs = _run_exchange(_swap_sibling(l_send), "grads_swap_sibling")
    l_sums = [_add_halves(a, b, "grads_add_" + nm) for nm, a, b in zip(late_names, l_mine, l_theirs)]
    (dh0, d_apre), l_recv = _prenorm_bwd(h0, attn_pre_norm, du0, dh1, carry=_exchange_chips(l_sums))
    grad_x = dh0[ROW0:ROW0 + seq][None]
    d_meta = dh0[PAD_ROWS:ROW0]

    small_grads = [d_apre, d_gsc[0:1, :GDN_HEADS], d_gsc[1:2, :GDN_HEADS], d_gnw, d_snw.reshape(1, SB_HEADS, SB_DH),
                   d_apost, d_fpre, d_fconvb, d_fpost]
    n_small_rows = _pack_small(small_grads).shape[0]
    slab_parts = _gather_direct([jnp.concatenate([_pack_small(small_grads), d_meta.reshape(-1, LANE)], axis=0)],
                                name="gather_small_grads")[0]
    me = 4 * lax.axis_index("x") + 2 * lax.axis_index("y") + my_c
    meta_parts = lax.dynamic_index_in_dim(
        slab_parts[:, n_small_rows:].reshape(N_DEV, N_META, N_DEV, LANE), me, axis=2, keepdims=False)
    slab_parts = slab_parts[:, :n_small_rows]

    res = {}
    for nm, parts in zip(early_names + late_names + ("meta_tokens",), list(e_recv) + list(l_recv) + [meta_parts]):
        wloc = args[nm]
        shp = wloc.shape
        w2 = wloc.reshape(shp[-2], shp[-1])
        outs = _adamw(parts, w2, args["m_" + nm].reshape(w2.shape), args["v_" + nm].reshape(w2.shape), "adamw_" + nm)
        res[nm] = [o.reshape(shp) for o in outs]
    small_shapes = [args[nm].shape for nm in SMALL]
    outs = _adamw(slab_parts, _pack_small([args[nm] for nm in SMALL]), _pack_small([args["m_" + nm] for nm in SMALL]),
                  _pack_small([args["v_" + nm] for nm in SMALL]), "adamw_small")
    for k in range(4):
        for nm, val in zip(SMALL, _unpack_small(outs[k], small_shapes)):
            res.setdefault(nm, [None] * 4)[k] = val

    order = ("meta_tokens", "attn_pre_norm", "w_in", "gdn_conv_w", "gdn_A_log", "gdn_dt_bias", "gdn_norm_w",
             "sb_norm_w", "w_out", "attn_post_norm", "ffn_pre_norm", "w_ffn_up", "ffn_conv_w", "ffn_conv_b",
             "w_ffn_down", "ffn_post_norm")
    return (loss, grad_x, *[res[nm][0] for nm in order], *[res[nm][1] for nm in order],
            *[res[nm][2] for nm in order], *[res[nm][3] for nm in order])
```

```python
import functools

import jax
import jax.numpy as jnp
from jax import lax
from jax.experimental import pallas as pl
from jax.experimental.pallas import tpu as pltpu

F32 = jnp.float32
BF16 = jnp.bfloat16

D_MODEL = 1024
N_META = 16
GDN_HEADS = 4
GDN_D = 128
GDN_CHUNK = 64
GDN_CONV = 4
GDN_ROWS = 256
SCAN_CHUNKS = 4
SB_HEADS = 8
SB_DH = 64
SB_BLOCK = 128
D_FF = 2816
FFN_CONV = 3
NORM_EPS = 1e-6
L2_EPS = 1e-6
LANE = 128
N_DEV = 8

PAD_ROWS = SB_BLOCK - N_META
ROW0 = SB_BLOCK
SB_SPAN = 512
SB_DEAD = -104.0
SB_SUB = 256
SB_QTILE = 256
LP_ALIGN = 256

C_QKV = 3 * GDN_HEADS * GDN_D
C_Z = GDN_HEADS * GDN_D
C_SB = 3 * SB_HEADS * SB_DH
C_AB = 256
OFF_Z = C_QKV
OFF_SB = OFF_Z + C_Z
OFF_AB = OFF_SB + C_SB
D_INP = OFF_AB + C_AB
D_IN = C_QKV + 2 * GDN_HEADS + C_Z + C_SB

ADAM_LR = 0.001
ADAM_B1 = 0.9
ADAM_B2 = 0.999
ADAM_EPS = 1e-08
ADAM_WD = 0.01
ADAM_STEP = 10

VMEM_LIMIT = 56 * 1024 * 1024
ELEMWISE_VMEM = 8 * 1024 * 1024
MESH = pl.DeviceIdType.MESH


def _cp(sem=None):
    kw = dict(vmem_limit_bytes=VMEM_LIMIT)
    if sem is not None:
        kw["dimension_semantics"] = sem
    return pltpu.CompilerParams(**kw)


def _tile(n, cap, unit=128):
    best = None
    t = unit
    while t <= min(n, cap):
        if n % t == 0:
            best = t
        t += unit
    assert best is not None, (n, cap, unit)
    return best


def _dot(a, b):
    return jnp.dot(a, b, preferred_element_type=F32)


def _dot_nt(a, b):
    return lax.dot_general(a, b, (((1,), (1,)), ((), ())), preferred_element_type=F32)


def _dot_tn(a, b):
    return lax.dot_general(a, b, (((0,), (0,)), ((), ())), preferred_element_type=F32)


def _split(x):
    hi = x.astype(BF16)
    lo = (x - hi.astype(F32)).astype(BF16)
    return hi, lo


def _dot1(a, b, f=_dot):
    return f(a.astype(BF16), b.astype(BF16))


def _dot3(a, b, f=_dot):
    ah, al = _split(a)
    bh, bl = _split(b)
    return f(ah, bh) + (f(ah, bl) + f(al, bh))


def _dot_exact_l(m_bf16, x, f=_dot):
    xh, xl = _split(x)
    return f(m_bf16, xh) + f(m_bf16, xl)


def _dot_exact_r(x, m_bf16, f=_dot):
    xh, xl = _split(x)
    return f(xh, m_bf16) + f(xl, m_bf16)


def _iota2(shape, dim):
    return lax.broadcasted_iota(jnp.int32, shape, dim)


def _sigmoid(x):
    return 1.0 / (1.0 + jnp.exp(-x))


def _softplus(x):
    return jnp.maximum(x, 0.0) + jnp.log(1.0 + jnp.exp(-jnp.abs(x)))


def _colsum(x):
    return jnp.sum(x, axis=0, keepdims=True)


def _rowsum(x):
    return jnp.sum(x, axis=-1, keepdims=True)


def _mm(a, b, out_dtype, name):
    M, K = a.shape
    K2, N = b.shape
    assert K == K2
    tm = _tile(M, 768)
    tn = _tile(N, max(128, (6 * 1024 * 1024) // (2 * K)))

    def body(a_ref, b_ref, o_ref):
        o_ref[...] = _dot(a_ref[...].astype(BF16), b_ref[...].astype(BF16)).astype(o_ref.dtype)

    return pl.pallas_call(
        body, name=name, grid=(N // tn, M // tm),
        in_specs=[pl.BlockSpec((tm, K), lambda j, i: (i, 0)), pl.BlockSpec((K, tn), lambda j, i: (0, j))],
        out_specs=pl.BlockSpec((tm, tn), lambda j, i: (i, j)),
        out_shape=jax.ShapeDtypeStruct((M, N), out_dtype),
        compiler_params=_cp(("parallel", "parallel")),
    )(a, b)


def _mm_nt(a, b, out_dtype, name):
    M, K = a.shape
    N, K2 = b.shape
    assert K == K2
    tm = _tile(M, 768)
    tn = _tile(N, max(128, (6 * 1024 * 1024) // (2 * K)))

    def body(a_ref, b_ref, o_ref):
        o_ref[...] = _dot_nt(a_ref[...].astype(BF16), b_ref[...].astype(BF16)).astype(o_ref.dtype)

    return pl.pallas_call(
        body, name=name, grid=(N // tn, M // tm),
        in_specs=[pl.BlockSpec((tm, K), lambda j, i: (i, 0)), pl.BlockSpec((tn, K), lambda j, i: (j, 0))],
        out_specs=pl.BlockSpec((tm, tn), lambda j, i: (i, j)),
        out_shape=jax.ShapeDtypeStruct((M, N), out_dtype),
        compiler_params=_cp(("parallel", "parallel")),
    )(a, b)


def _mm_nt_pieces(pieces, offsets, b, out_dtype, name):
    M = pieces[0].shape[0]
    N = b.shape[0]
    n = len(pieces)
    widths = [p.shape[1] for p in pieces]
    assert all(off % k == 0 for off, k in zip(offsets, widths))
    tm = _tile(M, 768)
    tn = _tile(N, 512)

    def body(*refs):
        acc = _dot_nt(refs[0][...].astype(BF16), refs[n][...].astype(BF16))
        for p in range(1, n):
            acc = acc + _dot_nt(refs[p][...].astype(BF16), refs[n + p][...].astype(BF16))
        refs[2 * n][...] = acc.astype(out_dtype)

    return pl.pallas_call(
        body, name=name, grid=(N // tn, M // tm),
        in_specs=[pl.BlockSpec((tm, k), lambda j, i: (i, 0)) for k in widths]
        + [pl.BlockSpec((tn, k), functools.partial(lambda j, i, blk: (j, blk), blk=off // k))
           for off, k in zip(offsets, widths)],
        out_specs=pl.BlockSpec((tm, tn), lambda j, i: (i, j)),
        out_shape=jax.ShapeDtypeStruct((M, N), out_dtype),
        compiler_params=_cp(("parallel", "parallel")),
    )(*pieces, *([b] * n))


def _mm_tn_pieces(a, pieces, name):
    M, K = a.shape
    n = len(pieces)
    tm = _tile(M, 768)

    def body(*refs):
        @pl.when(pl.program_id(0) == 0)
        def _():
            for p in range(n):
                refs[1 + n + p][...] = jnp.zeros_like(refs[1 + n + p])
        at = refs[0][...].astype(BF16)
        for p in range(n):
            refs[1 + n + p][...] += _dot_tn(at, refs[1 + p][...].astype(BF16))

    return pl.pallas_call(
        body, name=name, grid=(M // tm,),
        in_specs=[pl.BlockSpec((tm, K), lambda m: (m, 0))] + [pl.BlockSpec((tm, p.shape[1]), lambda m: (m, 0)) for p in pieces],
        out_specs=[pl.BlockSpec((K, p.shape[1]), lambda m: (0, 0)) for p in pieces],
        out_shape=[jax.ShapeDtypeStruct((K, p.shape[1]), F32) for p in pieces],
        compiler_params=_cp(("arbitrary",)),
    )(a, *pieces)


def _mm_tn(a, b, name):
    M, K = a.shape
    M2, N = b.shape
    assert M == M2
    tm = _tile(M, 1408)
    tk = _tile(K, 1408)
    tn = _tile(N, 1408)

    def body(a_ref, b_ref, o_ref):
        @pl.when(pl.program_id(2) == 0)
        def _():
            o_ref[...] = jnp.zeros_like(o_ref)
        o_ref[...] += _dot_tn(a_ref[...].astype(BF16), b_ref[...].astype(BF16))

    return pl.pallas_call(
        body, name=name, grid=(K // tk, N // tn, M // tm),
        in_specs=[pl.BlockSpec((tm, tk), lambda i, j, m: (m, i)), pl.BlockSpec((tm, tn), lambda i, j, m: (m, j))],
        out_specs=pl.BlockSpec((tk, tn), lambda i, j, m: (i, j)),
        out_shape=jax.ShapeDtypeStruct((K, N), F32),
        compiler_params=_cp(("parallel", "parallel", "arbitrary")),
    )(a, b)


def _rms(x):
    return lax.rsqrt(jnp.mean(x * x, axis=-1, keepdims=True) + NORM_EPS)


def _rms_bwd(x, w, dy):
    r = _rms(x)
    n = x * r
    dyw = dy * w
    dx = r * (dyw - n * jnp.mean(dyw * n, axis=-1, keepdims=True))
    return dx, dy * n


def _build_rows(x, meta, target, LP, carry=None):
    seq, D = x.shape
    T = SB_BLOCK
    nx = seq // T
    assert seq % T == 0 and meta.shape[0] == N_META

    def body(x_ref, m_ref, t_ref, h_ref, tp_ref):
        i = pl.program_id(0)
        inside = (i >= 1) & (i <= nx)
        head = jnp.concatenate([jnp.zeros((PAD_ROWS, D), F32), m_ref[...]], axis=0)
        h_ref[...] = jnp.where(i == 0, head, jnp.where(inside, x_ref[...], 0.0))
        tp_ref[...] = jnp.where(inside, t_ref[...], 0.0)

    tok = pl.BlockSpec((T, D), lambda i: (jnp.clip(i - 1, 0, nx - 1), 0))
    row = pl.BlockSpec((T, D), lambda i: (i, 0))
    out = jax.ShapeDtypeStruct((LP, D), F32)
    return _call_carrying(
        carry, body, LP // T, name="build_rows",
        in_specs=[tok, pl.BlockSpec((N_META, D), lambda i: (0, 0)), tok], out_specs=[row, row], out_shape=[out, out],
        operands=(x, meta, target))


def _prenorm_fwd(h0, w, carry=None):
    LP, D = h0.shape
    T = _tile(LP, 512)

    def body(h_ref, w_ref, u_ref):
        h = h_ref[...]
        u_ref[...] = (h * _rms(h) * w_ref[...]).astype(BF16)

    return _call_carrying(
        carry, body, LP // T, name="prenorm_fwd",
        in_specs=[pl.BlockSpec((T, D), lambda i: (i, 0)), pl.BlockSpec((1, D), lambda i: (0, 0))],
        out_specs=[pl.BlockSpec((T, D), lambda i: (i, 0))],
        out_shape=[jax.ShapeDtypeStruct((LP, D), BF16)],
        operands=(h0, w))


def _prenorm_bwd(h0, w, du, dh1, carry=None):
    LP, D = h0.shape
    T = _tile(LP, 512)

    def body(h_ref, w_ref, du_ref, dh1_ref, dh0_ref, dw_ref):
        @pl.when(pl.program_id(0) == 0)
        def _():
            dw_ref[...] = jnp.zeros_like(dw_ref)
        dx, dwn = _rms_bwd(h_ref[...], w_ref[...], du_ref[...])
        dh0_ref[...] = dh1_ref[...] + dx
        dw_ref[...] += _colsum(dwn)

    row = pl.BlockSpec((T, D), lambda i: (i, 0))
    vec = pl.BlockSpec((1, D), lambda i: (0, 0))
    return _call_carrying(
        carry, body, LP // T, name="prenorm_bwd",
        in_specs=[row, vec, row, row], out_specs=[row, vec],
        out_shape=[jax.ShapeDtypeStruct((LP, D), F32), jax.ShapeDtypeStruct((1, D), F32)],
        operands=(h0, w, du, dh1))


def _gdn_gate_consts(alog_ref, dtb_ref, h):
    a_coef = -jnp.exp(alog_ref[0:1, h:h + 1])
    return a_coef, dtb_ref[0:1, h:h + 1]


def _gdn_pre_fwd(proj, conv_w, a_log, dt_bias, carry=None):
    LP = proj.shape[0]
    T = _tile(LP, 256)
    C = C_QKV
    H = GDN_HEADS

    def body(x_ref, halo_ref, ab_ref, cw_ref, alog_ref, dtb_ref, q_ref, k_ref, v_ref, beta_ref, g_ref):
        i = pl.program_id(0)

        def conv_silu(cols):
            ext = jnp.concatenate([jnp.where(i > 0, halo_ref[:, cols], 0.0), x_ref[:, cols]], axis=0)
            w = cw_ref[:, cols]
            y = w[GDN_CONV - 1:GDN_CONV] * ext[8:]
            for j in range(GDN_CONV - 1):
                y = y + w[j:j + 1] * pltpu.roll(ext, GDN_CONV - 1 - j, 0)[8:]
            return y * _sigmoid(y)

        for h in range(H):
            sl = slice(h * GDN_D, (h + 1) * GDN_D)
            cq = conv_silu(sl)
            q_ref[:, sl] = cq * lax.rsqrt(_rowsum(cq * cq) + L2_EPS) * (GDN_D ** -0.5)
            ck = conv_silu(slice(512 + h * GDN_D, 512 + (h + 1) * GDN_D))
            k_ref[:, sl] = ck * lax.rsqrt(_rowsum(ck * ck) + L2_EPS)
            v_ref[:, sl] = conv_silu(slice(1024 + h * GDN_D, 1024 + (h + 1) * GDN_D))
        ab = ab_ref[...]
        valid = (i * T + _iota2((T, 1), 0)) >= PAD_ROWS
        for h in range(H):
            sl = slice(h * GDN_D, (h + 1) * GDN_D)
            a_coef, dtb = _gdn_gate_consts(alog_ref, dtb_ref, h)
            g = jnp.where(valid, a_coef * _softplus(ab[:, h:h + 1] + dtb), 0.0)
            beta = jnp.where(valid, _sigmoid(ab[:, H + h:H + h + 1]), 0.0)
            g_ref[:, sl] = jnp.broadcast_to(g, (T, GDN_D))
            beta_ref[:, sl] = jnp.broadcast_to(beta, (T, GDN_D))

    t8 = T // 8
    row512 = pl.BlockSpec((T, 512), lambda i: (i, 0))
    small = lambda r, c: pl.BlockSpec((r, c), lambda i: (0, 0))
    out = jax.ShapeDtypeStruct((LP, 512), F32)
    return _call_carrying(
        carry, body, LP // T, name="gdn_pre_fwd",
        in_specs=[pl.BlockSpec((T, C), lambda i: (i, 0)),
                  pl.BlockSpec((8, C), lambda i: (jnp.maximum(i * t8 - 1, 0), 0)),
                  pl.BlockSpec((T, C_AB), lambda i: (i, OFF_AB // C_AB)),
                  small(GDN_CONV, C), small(1, H), small(1, H)],
        out_specs=[row512] * 5, out_shape=[out] * 5,
        operands=(proj, proj, proj, conv_w, a_log, dt_bias))


def _gdn_pre_bwd(proj, conv_w, a_log, dt_bias, dq, dk, dv, dbeta, dg, carry=None):
    LP = proj.shape[0]
    T = _tile(LP, 256)
    C = C_QKV
    H = GDN_HEADS
    TE = T + 8
    nt = LP // T

    def body(x_ref, xp_ref, xn_ref, ab_ref, cw_ref, alog_ref, dtb_ref,
             dq_ref, dqn_ref, dk_ref, dkn_ref, dv_ref, dvn_ref, dbeta_ref, dg_ref,
             dx_ref, dab_ref, dcw_ref, dsc_ref):
        i = pl.program_id(0)

        @pl.when(i == 0)
        def _():
            dcw_ref[...] = jnp.zeros_like(dcw_ref)
            dsc_ref[...] = jnp.zeros_like(dsc_ref)

        last = i == nt - 1

        def strip(cols, d_ref, dn_ref, dcols, scale):
            ext = jnp.concatenate([jnp.where(i > 0, xp_ref[:, cols], 0.0), x_ref[:, cols],
                                   jnp.where(last, 0.0, xn_ref[:, cols])], axis=0)
            sh = [ext[8:8 + TE]] + [pltpu.roll(ext, s, 0)[8:8 + TE] for s in range(1, GDN_CONV)]
            w = cw_ref[:, cols]
            y = w[GDN_CONV - 1:GDN_CONV] * sh[0]
            for j in range(GDN_CONV - 1):
                y = y + w[j:j + 1] * sh[GDN_CONV - 1 - j]
            sg = _sigmoid(y)
            d = jnp.concatenate([d_ref[:, dcols], jnp.where(last, 0.0, dn_ref[:, dcols])], axis=0)
            if scale is not None:
                c = y * sg
                r = lax.rsqrt(_rowsum(c * c) + L2_EPS)
                n = c * r
                d = scale * r * (d - n * _rowsum(d * n))
            dy = d * (sg * (1.0 + y * (1.0 - sg)))
            dy_t = dy[0:T]
            for j in range(GDN_CONV):
                dcw_ref[j:j + 1, cols] += _colsum(dy_t * sh[GDN_CONV - 1 - j][0:T])
            dx = w[GDN_CONV - 1:GDN_CONV] * dy_t
            for j in range(GDN_CONV - 1):
                dx = dx + w[j:j + 1] * pltpu.roll(dy, TE - (GDN_CONV - 1 - j), 0)[0:T]
            dx_ref[:, cols] = dx.astype(BF16)

        for h in range(H):
            sl = slice(h * GDN_D, (h + 1) * GDN_D)
            strip(sl, dq_ref, dqn_ref, sl, GDN_D ** -0.5)
            strip(slice(512 + h * GDN_D, 512 + (h + 1) * GDN_D), dk_ref, dkn_ref, sl, 1.0)
            strip(slice(1024 + h * GDN_D, 1024 + (h + 1) * GDN_D), dv_ref, dvn_ref, sl, None)
        ab = ab_ref[...]
        valid = (i * T + _iota2((T, 1), 0)) >= PAD_ROWS
        lane = _iota2((T, C_AB), 1)
        lane1 = _iota2((1, 128), 1)
        dab = jnp.zeros((T, C_AB), F32)
        dsc_a = jnp.zeros((1, 128), F32)
        dsc_d = jnp.zeros((1, 128), F32)
        for h in range(H):
            a_coef, dtb = _gdn_gate_consts(alog_ref, dtb_ref, h)
            pre = ab[:, h:h + 1] + dtb
            dgh = jnp.where(valid, dg_ref[:, h * GDN_D:h * GDN_D + 1], 0.0)
            da = dgh * a_coef * _sigmoid(pre)
            beta = _sigmoid(ab[:, H + h:H + h + 1])
            db = jnp.where(valid, dbeta_ref[:, h * GDN_D:h * GDN_D + 1], 0.0) * beta * (1.0 - beta)
            dab = dab + jnp.where(lane == h, da, 0.0) + jnp.where(lane == H + h, db, 0.0)
            dsc_a = dsc_a + jnp.where(lane1 == h, _colsum(dgh * a_coef * _softplus(pre)), 0.0)
            dsc_d = dsc_d + jnp.where(lane1 == h, _colsum(da), 0.0)
        dab_ref[...] = dab.astype(BF16)
        dsc_ref[0:1, :] += dsc_a
        dsc_ref[1:2, :] += dsc_d

    t8 = T // 8
    nb8 = LP // 8
    prev8 = lambda w: pl.BlockSpec((8, w), lambda i: (jnp.maximum(i * t8 - 1, 0), 0))
    next8 = lambda w: pl.BlockSpec((8, w), lambda i: (jnp.minimum((i + 1) * t8, nb8 - 1), 0))
    row = lambda w: pl.BlockSpec((T, w), lambda i: (i, 0))
    small = lambda r, c: pl.BlockSpec((r, c), lambda i: (0, 0))
    return _call_carrying(
        carry, body, nt, name="gdn_pre_bwd",
        in_specs=[row(C), prev8(C), next8(C), pl.BlockSpec((T, C_AB), lambda i: (i, OFF_AB // C_AB)),
                  small(GDN_CONV, C), small(1, H), small(1, H),
                  row(512), next8(512), row(512), next8(512), row(512), next8(512), row(512), row(512)],
        out_specs=[row(C), row(C_AB), small(GDN_CONV, C), small(2, 128)],
        out_shape=[jax.ShapeDtypeStruct((LP, C), BF16), jax.ShapeDtypeStruct((LP, C_AB), BF16),
                   jax.ShapeDtypeStruct((GDN_CONV, C), F32), jax.ShapeDtypeStruct((2, 128), F32)],
        operands=(proj, proj, proj, proj, conv_w, a_log, dt_bias, dq, dq, dk, dk, dv, dv, dbeta, dg))


def _tri_masks():
    r = _iota2((GDN_CHUNK, GDN_CHUNK), 0)
    c = _iota2((GDN_CHUNK, GDN_CHUNK), 1)
    return r >= c, r > c


def _gdn_chunk_common(q, k, v, beta, gb):
    incl, strict = _tri_masks()
    l_incl = incl.astype(BF16)
    gd = _dot_exact_l(l_incl, jnp.where(strict, gb[:, :GDN_CHUNK], 0.0))
    gc = _dot_exact_l(l_incl, gb)
    decay = jnp.where(incl, jnp.exp(jnp.where(incl, gd, 0.0)), 0.0)
    exp_g = jnp.exp(gc)
    g_last = gc[GDN_CHUNK - 1:GDN_CHUNK, :]
    kd_fac = jnp.exp(g_last - gc)
    gl = jnp.exp(g_last)
    kb = k * beta
    kk = _dot1(kb, k, _dot_nt)
    return dict(incl=incl, strict=strict, decay=decay, exp_g=exp_g, kd_fac=kd_fac, gl=gl, kb=kb, kk=kk,
                vb=v * beta, kbg=kb * exp_g)


def _interleave(gens):
    gens = list(gens)
    while gens:
        alive = []
        for g in gens:
            try:
                next(g)
                alive.append(g)
            except StopIteration:
                pass
        gens = alive


def _call_carrying(ex, body, nsteps, *, name, in_specs, out_specs, out_shape, operands, scratch_shapes=()):
    n_in, n_out, n_scr = len(in_specs), len(out_specs), len(scratch_shapes)
    n = ex.n if ex is not None else 0

    def full(*refs):
        o0 = n_in + n
        s0 = o0 + n_out + n
        ex_refs = (refs[n_in:o0], refs[o0 + n_out:s0], refs[s0 + n_scr:])
        step = pl.program_id(0)
        _carry_begin(ex, ex_refs, step, nsteps)
        body(*refs[:n_in], *refs[o0:o0 + n_out], *refs[s0:s0 + n_scr])
        _carry_end(ex, ex_refs, step, nsteps)

    res = pl.pallas_call(
        full, name=name, grid=(nsteps,),
        in_specs=list(in_specs) + [ANY_SPEC] * n, out_specs=list(out_specs) + [ANY_SPEC] * n,
        out_shape=list(out_shape) + (ex.out_shapes if ex is not None else []),
        scratch_shapes=list(scratch_shapes) + (ex.scratch if ex is not None else []),
        compiler_params=pltpu.CompilerParams(dimension_semantics=("arbitrary",), vmem_limit_bytes=VMEM_LIMIT,
                                             has_side_effects=ex is not None),
    )(*operands, *(ex.arrs if ex is not None else []))
    return list(res[:n_out]), list(res[n_out:])


def _gdn_chunk_fwd(qn, kn, v, beta_b, g_b, carry=None):
    LP = qn.shape[0]
    R = GDN_ROWS
    H = GDN_HEADS
    CH = GDN_CHUNK

    def body(q_ref, k_ref, v_ref, b_ref, g_ref, u_ref, w_ref, qd_ref, kd_ref, qk_ref, t_ref, gl_ref):
        def item(cc, h):
            rs = slice(cc * CH, (cc + 1) * CH)
            sl = slice(h * GDN_D, (h + 1) * GDN_D)
            s64 = slice(h * CH, (h + 1) * CH)
            q, k = q_ref[rs, sl], k_ref[rs, sl]
            m = _gdn_chunk_common(q, k, v_ref[rs, sl], b_ref[rs, sl], g_ref[rs, sl])
            qk_raw = _dot1(q, k, _dot_nt)
            yield
            a = jnp.where(m["strict"], m["kk"] * m["decay"], 0.0)
            eye = (_iota2((CH, CH), 0) == _iota2((CH, CH), 1)).astype(F32)
            t = eye - a
            p = _dot3(a, a)
            yield
            for _ in range(4):
                t = t + _dot3(t, p)
                p = _dot3(p, p)
                yield
            t = t + _dot3(t, p)
            yield
            u_ref[rs, sl] = _dot1(t, m["vb"])
            w_ref[rs, sl] = _dot1(t, m["kbg"])
            qk_ref[rs, s64] = qk_raw * m["decay"]
            t_ref[rs, s64] = t
            qd_ref[rs, sl] = q * m["exp_g"]
            kd_ref[rs, sl] = k * m["kd_fac"]
            gl_ref[cc * 8:(cc + 1) * 8, sl] = jnp.broadcast_to(m["gl"], (8, GDN_D))

        _interleave(item(cc, h) for cc in range(R // CH) for h in range(H))

    row = lambda w: pl.BlockSpec((R, w), lambda i: (i, 0))
    o512 = jax.ShapeDtypeStruct((LP, 512), F32)
    o256 = jax.ShapeDtypeStruct((LP, 256), F32)
    return _call_carrying(
        carry, body, LP // R, name="gdn_chunk_fwd",
        in_specs=[row(512)] * 5,
        out_specs=[row(512)] * 4 + [row(256)] * 2 + [pl.BlockSpec((R // 8, 512), lambda i: (i, 0))],
        out_shape=[o512] * 4 + [o256] * 2 + [jax.ShapeDtypeStruct((LP // 8, 512), F32)],
        operands=(qn, kn, v, beta_b, g_b))


def _gdn_chunk_bwd(qn, kn, v, beta_b, g_b, t_all, du, dw, dqd, dkd, dqk, dgl):
    LP = qn.shape[0]
    R = GDN_ROWS
    H = GDN_HEADS
    CH = GDN_CHUNK

    def body(q_ref, k_ref, v_ref, b_ref, g_ref, t_ref, du_ref, dw_ref, dqd_ref, dkd_ref, dqk_ref, dgl_ref,
             dq_ref, dk_ref, dv_ref, db_ref, dg_ref):
        ones = jnp.ones((CH, GDN_D), BF16)

        def item(cc, h):
            rs = slice(cc * CH, (cc + 1) * CH)
            sl = slice(h * GDN_D, (h + 1) * GDN_D)
            s64 = slice(h * CH, (h + 1) * CH)
            q, k, vv, beta = q_ref[rs, sl], k_ref[rs, sl], v_ref[rs, sl], b_ref[rs, sl]
            m = _gdn_chunk_common(q, k, vv, beta, g_ref[rs, sl])
            incl, strict, decay = m["incl"], m["strict"], m["decay"]
            t = t_ref[rs, s64]
            du_, dw_ = du_ref[rs, sl], dw_ref[rs, sl]
            dqd_, dkd_ = dqd_ref[rs, sl], dkd_ref[rs, sl]
            d_t = _dot1(du_, m["vb"], _dot_nt) + _dot1(dw_, m["kbg"], _dot_nt)
            dvb = _dot1(t, du_, _dot_tn)
            dkbg = _dot1(t, dw_, _dot_tn)
            qk_raw = _dot1(q, k, _dot_nt)
            yield
            x1 = _dot3(d_t, t, _dot_nt)
            dkb = dkbg * m["exp_g"]
            d_gi = _rowsum(dkbg * m["kbg"])
            yield
            d_a = jnp.where(strict, -_dot3(t, x1, _dot_tn), 0.0)
            yield
            d_kk = d_a * decay
            dqk_m = jnp.where(incl, dqk_ref[rs, s64], 0.0)
            dqk_raw = dqk_m * decay
            mm = (d_a * m["kk"] + dqk_m * qk_raw) * decay
            dkb = dkb + _dot1(d_kk, k)
            dk_ = _dot1(d_kk, m["kb"], _dot_tn) + _dot1(dqk_raw, q, _dot_tn)
            dq_ = _dot1(dqk_raw, k) + dqd_ * m["exp_g"]
            d_gi = d_gi + (_dot_exact_r(mm, ones) - _dot_exact_r(mm, ones, _dot_tn))
            yield
            d_gi = d_gi + _rowsum(dqd_ * q * m["exp_g"])
            e = _rowsum(dkd_ * k * m["kd_fac"])
            d_gi = d_gi - e
            d_glast = _colsum(jnp.broadcast_to(e, (CH, GDN_D))) + dgl_ref[cc * 8:cc * 8 + 1, sl] * m["gl"]
            dk_ = dk_ + dkd_ * m["kd_fac"] + dkb * beta
            d_gi = d_gi + jnp.where(_iota2((CH, GDN_D), 0) == CH - 1, d_glast, 0.0)
            u_incl = (_iota2((CH, CH), 1) >= _iota2((CH, CH), 0)).astype(BF16)
            dq_ref[rs, sl] = dq_
            dk_ref[rs, sl] = dk_
            dv_ref[rs, sl] = dvb * beta
            db_ref[rs, sl] = jnp.broadcast_to(_rowsum(dvb * vv) + _rowsum(dkb * k), (CH, GDN_D))
            dg_ref[rs, sl] = _dot_exact_l(u_incl, d_gi)

        _interleave(item(cc, h) for cc in range(R // CH) for h in range(H))

    row = lambda w: pl.BlockSpec((R, w), lambda i: (i, 0))
    o512 = jax.ShapeDtypeStruct((LP, 512), F32)
    gl_spec = pl.BlockSpec((R // 8, 512), lambda i: (i, 0))
    return pl.pallas_call(
        body, name="gdn_chunk_bwd", grid=(LP // R,),
        in_specs=[row(512)] * 5 + [row(256)] + [row(512)] * 4 + [row(256), gl_spec],
        out_specs=[row(512)] * 5, out_shape=[o512] * 5,
        compiler_params=_cp(("parallel",)),
    )(qn, kn, v, beta_b, g_b, t_all, du, dw, dqd, dkd, dqk, dgl)


def _gdn_scan_fwd(u, w, qd, kd, qk, gl):
    LP = u.shape[0]
    CH = GDN_CHUNK
    CPS = SCAN_CHUNKS
    N = LP // CH
    NS = N // CPS
    H = GDN_HEADS

    def body(u_ref, w_ref, qd_ref, kd_ref, qk_ref, gl_ref, o_ref, ssave_ref, s_sc):
        @pl.when(pl.program_id(0) == 0)
        def _():
            s_sc[...] = jnp.zeros_like(s_sc)

        for cc in range(CPS):
            rs = slice(cc * CH, (cc + 1) * CH)
            ssave_ref[cc * GDN_D:(cc + 1) * GDN_D, :] = s_sc[...]

            def item(h):
                sl = slice(h * GDN_D, (h + 1) * GDN_D)
                s = s_sc[:, sl]
                v_new = u_ref[rs, sl] - _dot1(w_ref[rs, sl], s)
                o_s = _dot1(qd_ref[rs, sl], s)
                yield
                o_ref[rs, sl] = o_s + _dot1(qk_ref[rs, h * CH:(h + 1) * CH], v_new)
                s_sc[:, sl] = s * gl_ref[cc * 8:cc * 8 + 1, sl] + _dot1(kd_ref[rs, sl], v_new, _dot_tn)

            _interleave(item(h) for h in range(H))

    row = lambda w_: pl.BlockSpec((CPS * CH, w_), lambda n: (n, 0))
    return pl.pallas_call(
        body, name="gdn_scan_fwd", grid=(NS,),
        in_specs=[row(512)] * 4 + [row(256), pl.BlockSpec((CPS * 8, 512), lambda n: (n, 0))],
        out_specs=[row(512), pl.BlockSpec((CPS * GDN_D, 512), lambda n: (n, 0))],
        out_shape=[jax.ShapeDtypeStruct((LP, 512), F32), jax.ShapeDtypeStruct((N * GDN_D, 512), F32)],
        scratch_shapes=[pltpu.VMEM((GDN_D, 512), F32)],
        compiler_params=_cp(("arbitrary",)),
    )(u, w, qd, kd, qk, gl)


def _gdn_scan_bwd(u, w, qd, kd, qk, gl, ssave, do, carry=None):
    LP = u.shape[0]
    CH = GDN_CHUNK
    CPS = SCAN_CHUNKS
    N = LP // CH
    NS = N // CPS
    H = GDN_HEADS

    def body(u_ref, w_ref, qd_ref, kd_ref, qk_ref, gl_ref, s_ref, do_ref,
             du_ref, dw_ref, dqd_ref, dkd_ref, dqk_ref, dgl_ref, ds_sc):
        @pl.when(pl.program_id(0) == 0)
        def _():
            ds_sc[...] = jnp.zeros_like(ds_sc)

        for cc in reversed(range(CPS)):
            rs = slice(cc * CH, (cc + 1) * CH)
            r8 = slice(cc * 8, (cc + 1) * 8)

            def item(h):
                sl = slice(h * GDN_D, (h + 1) * GDN_D)
                s64 = slice(h * CH, (h + 1) * CH)
                s = s_ref[cc * GDN_D:(cc + 1) * GDN_D, sl]
                ds = ds_sc[:, sl]
                do_ = do_ref[rs, sl]
                w_, qd_, kd_, qk_ = w_ref[rs, sl], qd_ref[rs, sl], kd_ref[rs, sl], qk_ref[rs, s64]
                v_new = u_ref[rs, sl] - _dot1(w_, s)
                d_vnew = _dot1(qk_, do_, _dot_tn) + _dot1(kd_, ds)
                dqd_ref[rs, sl] = _dot1(do_, s, _dot_nt)
                ds_new = ds * gl_ref[cc * 8:cc * 8 + 1, sl] + _dot1(qd_, do_, _dot_tn)
                dgl_ref[r8, sl] = jnp.broadcast_to(jnp.sum(_colsum(ds * s), axis=-1, keepdims=True), (8, GDN_D))
                yield
                du_ref[rs, sl] = d_vnew
                dw_ref[rs, sl] = -_dot1(d_vnew, s, _dot_nt)
                dkd_ref[rs, sl] = _dot1(v_new, ds, _dot_nt)
                dqk_ref[rs, s64] = _dot1(do_, v_new, _dot_nt)
                ds_sc[:, sl] = ds_new - _dot1(w_, d_vnew, _dot_tn)

            _interleave(item(h) for h in range(H))

    rev = lambda w_: pl.BlockSpec((CPS * CH, w_), lambda n: (NS - 1 - n, 0))
    rev8 = pl.BlockSpec((CPS * 8, 512), lambda n: (NS - 1 - n, 0))
    o512 = jax.ShapeDtypeStruct((LP, 512), F32)
    return _call_carrying(
        carry, body, NS, name="gdn_scan_bwd",
        in_specs=[rev(512)] * 4 + [rev(256), rev8, pl.BlockSpec((CPS * GDN_D, 512), lambda n: (NS - 1 - n, 0)),
                  rev(512)],
        out_specs=[rev(512)] * 4 + [rev(256), rev8],
        out_shape=[o512] * 4 + [jax.ShapeDtypeStruct((LP, 256), F32), jax.ShapeDtypeStruct((LP // 8, 512), F32)],
        scratch_shapes=[pltpu.VMEM((GDN_D, 512), F32)],
        operands=(u, w, qd, kd, qk, gl, ssave, do))


def _sb_scores(qh, kblk, mask):
    z = _dot_nt(qh, kblk)
    e = jnp.exp(-jnp.abs(z))
    sp = jnp.maximum(z, 0.0) + jnp.log(1.0 + e)
    return z, e, jnp.where(mask, -sp, 0.0), z - sp


def _sb_fwd(proj):
    LP = proj.shape[0]
    B = SB_BLOCK
    W = min(SB_SPAN, LP)
    SUB = SB_SUB
    Q = min(SB_QTILE, LP)
    nq = LP // Q
    nsub = W // SUB
    scale = SB_DH ** -0.5
    qcol, kcol, vcol = OFF_SB // B, (OFF_SB + 512) // B, (OFF_SB + 1024) // B

    def body(q_ref, k_ref, v_ref, tri_ref, o_ref, c_ref, n_ref):
        i = pl.program_id(1)
        lane = _iota2((Q, B), 1)
        head_a = lane < SB_DH
        qs = q_ref[...] * scale
        qh = [jnp.where(head_a, qs, 0.0).astype(BF16), jnp.where(head_a, 0.0, qs).astype(BF16)]
        u_strict = tri_ref[...]
        qpos = i * Q + _iota2((Q, W), 0)
        hi0 = (i + 1) * Q
        nspan = (hi0 + W - 1) // W

        def live(st):
            return (st[0] < nspan) & (st[1] > 0)

        def span(st):
            r, carry = st[0], st[2:]
            hi = hi0 - r * W
            k0 = pl.multiple_of(jnp.maximum(hi - W, 0), B)
            kblk = k_ref[pl.ds(k0, W), :].astype(BF16)
            vblk = v_ref[pl.ds(k0, W), :].astype(BF16)
            kpos = k0 + _iota2((Q, W), 1)
            mask = (kpos < qpos) & (kpos >= PAD_ROWS) & (kpos < hi)
            new = [None] * 4

            def head(h):
                o_acc, c = carry[2 * h], carry[2 * h + 1]
                z, e, l1m, lsg = _sb_scores(qh[h], kblk, mask)
                yield
                subs = [slice(b * SUB, (b + 1) * SUB) for b in range(nsub)]
                suf = [_dot(l1m[:, bs].astype(BF16), u_strict) for bs in subs]
                yield
                parts = [None] * nsub
                for b in reversed(range(nsub)):
                    parts[b] = jnp.where(mask[:, subs[b]], jnp.exp(lsg[:, subs[b]] + suf[b] + c), 0.0)
                    c = c + _rowsum(l1m[:, subs[b]])
                att = jnp.concatenate(parts, axis=1).astype(BF16)
                new[2 * h], new[2 * h + 1] = o_acc + _dot(att, vblk), c

            _interleave(head(h) for h in range(2))
            more = (jnp.maximum(jnp.max(new[1]), jnp.max(new[3])) > SB_DEAD).astype(jnp.int32)
            return (r + 1, more, *new)

        zero_o = jnp.zeros((Q, B), F32)
        zero_c = jnp.zeros((Q, 1), F32)
        nrun, _, o_a, c_a, o_b, c_b = lax.while_loop(
            live, span, (jnp.int32(0), jnp.int32(1), zero_o, zero_c, zero_o, zero_c))
        o_ref[...] = jnp.where(head_a, o_a, o_b)
        c_ref[...] = jnp.where(head_a, c_a, c_b)
        n_ref[pl.program_id(0), i] = nrun

    blk = pl.BlockSpec((Q, B), lambda p, i: (i, p))
    out = jax.ShapeDtypeStruct((LP, 512), F32)
    return pl.pallas_call(
        body, name="sb_fwd", grid=(SB_HEADS // 2, nq),
        in_specs=[pl.BlockSpec((Q, B), lambda p, i: (i, qcol + p)),
                  pl.BlockSpec((LP, B), lambda p, i: (0, kcol + p)),
                  pl.BlockSpec((LP, B), lambda p, i: (0, vcol + p)),
                  pl.BlockSpec((SUB, SUB), lambda p, i: (0, 0))],
        out_specs=[blk, blk, pl.BlockSpec(memory_space=pltpu.SMEM)],
        out_shape=[out, out, jax.ShapeDtypeStruct((SB_HEADS // 2, nq), jnp.int32)],
        compiler_params=_cp(("arbitrary", "arbitrary")),
    )(proj, proj, proj, jnp.tril(jnp.ones((SUB, SUB), BF16), -1))


def _sb_bwd(proj, ctot, nrun_all, do):
    LP = proj.shape[0]
    B = SB_BLOCK
    W = min(SB_SPAN, LP)
    SUB = SB_SUB
    Q = min(SB_QTILE, LP)
    nq = LP // Q
    nsub = W // SUB
    scale = SB_DH ** -0.5
    qcol, kcol, vcol = OFF_SB // B, (OFF_SB + 512) // B, (OFF_SB + 1024) // B

    def body(n_ref, q_ref, k_ref, v_ref, c_ref, do_ref, tril_ref, triu_ref, dq_ref, dk_ref, dv_ref):
        i = pl.program_id(1)

        @pl.when(i == 0)
        def _():
            dk_ref[...] = jnp.zeros_like(dk_ref)
            dv_ref[...] = jnp.zeros_like(dv_ref)

        lane = _iota2((Q, B), 1)
        head_a = lane < SB_DH
        qs = q_ref[...] * scale
        qh = [jnp.where(head_a, qs, 0.0).astype(BF16), jnp.where(head_a, 0.0, qs).astype(BF16)]
        dof = do_ref[...]
        doh = [jnp.where(head_a, dof, 0.0).astype(BF16), jnp.where(head_a, 0.0, dof).astype(BF16)]
        cfull = c_ref[...]
        ctot_h = [cfull[:, 0:1], cfull[:, SB_DH:SB_DH + 1]]
        u_strict = tril_ref[...]
        l_strict = triu_ref[...]
        qpos = i * Q + _iota2((Q, W), 0)
        hi0 = (i + 1) * Q
        nrun = n_ref[pl.program_id(0), i]

        def span(t, carry):
            r = nrun - 1 - t
            hi = hi0 - r * W
            k0 = pl.multiple_of(jnp.maximum(hi - W, 0), B)
            kblk = k_ref[pl.ds(k0, W), :].astype(BF16)
            vblk = v_ref[pl.ds(k0, W), :].astype(BF16)
            kpos = k0 + _iota2((Q, W), 1)
            mask = (kpos < qpos) & (kpos >= PAD_ROWS) & (kpos < hi)
            new = [None] * 6
            dk_add, dv_add = [None, None], [None, None]
            subs = [slice(b * SUB, (b + 1) * SUB) for b in range(nsub)]

            def head(h):
                dq_acc, pre, ecar = carry[3 * h], carry[3 * h + 1], carry[3 * h + 2]
                z, e, l1m, lsg = _sb_scores(qh[h], kblk, mask)
                d_att = _dot_nt(doh[h], vblk)
                yield
                sig = jnp.where(z >= 0.0, 1.0, e) / (1.0 + e)
                suf = [_dot(l1m[:, bs].astype(BF16), u_strict) for bs in subs]
                yield
                att_parts, p_parts = [None] * nsub, [None] * nsub
                for b, bs in enumerate(subs):
                    pre = pre + _rowsum(l1m[:, bs])
                    att_parts[b] = jnp.where(mask[:, bs], jnp.exp(lsg[:, bs] + suf[b] + (ctot_h[h] - pre)), 0.0)
                    p_parts[b] = att_parts[b] * d_att[:, bs]
                pcum = [_dot(p.astype(BF16), l_strict) for p in p_parts]
                yield
                dz_parts = [None] * nsub
                for b, bs in enumerate(subs):
                    sg = sig[:, bs]
                    dz_parts[b] = jnp.where(mask[:, bs], p_parts[b] * (1.0 - sg) - sg * (ecar + pcum[b]), 0.0)
                    ecar = ecar + _rowsum(p_parts[b])
                att = jnp.concatenate(att_parts, axis=1).astype(BF16)
                dz = jnp.concatenate(dz_parts, axis=1).astype(BF16)
                new[3 * h:3 * h + 3] = [dq_acc + _dot(dz, kblk), pre, ecar]
                dk_add[h] = _dot_tn(dz, qh[h])
                dv_add[h] = _dot_tn(att, doh[h])

            _interleave(head(h) for h in range(2))
            dk_ref[pl.ds(k0, W), :] += dk_add[0] + dk_add[1]
            dv_ref[pl.ds(k0, W), :] += dv_add[0] + dv_add[1]
            return tuple(new)

        zero_o = jnp.zeros((Q, B), F32)
        zero_c = jnp.zeros((Q, 1), F32)
        res = lax.fori_loop(0, nrun, span, (zero_o, zero_c, zero_c, zero_o, zero_c, zero_c))
        dq_ref[...] = (jnp.where(head_a, res[0], res[3]) * scale).astype(BF16)

    blk = pl.BlockSpec((Q, B), lambda p, i: (i, p))
    col = pl.BlockSpec((LP, B), lambda p, i: (0, p))
    tri = pl.BlockSpec((SUB, SUB), lambda p, i: (0, 0))
    out = jax.ShapeDtypeStruct((LP, 512), F32)
    return pl.pallas_call(
        body, name="sb_bwd", grid=(SB_HEADS // 2, nq),
        in_specs=[pl.BlockSpec(memory_space=pltpu.SMEM),
                  pl.BlockSpec((Q, B), lambda p, i: (i, qcol + p)),
                  pl.BlockSpec((LP, B), lambda p, i: (0, kcol + p)),
                  pl.BlockSpec((LP, B), lambda p, i: (0, vcol + p)),
                  blk, blk, tri, tri],
        out_specs=[blk, col, col], out_shape=[jax.ShapeDtypeStruct((LP, 512), BF16), out, out],
        compiler_params=_cp(("arbitrary", "arbitrary")),
    )(nrun_all, proj, proj, proj, ctot, do, jnp.tril(jnp.ones((SUB, SUB), BF16), -1),
      jnp.triu(jnp.ones((SUB, SUB), BF16), 1))


def _sb_group_mean():
    r = jnp.right_shift(_iota2((512, 512), 0), 6)
    c = jnp.right_shift(_iota2((512, 512), 1), 6)
    return jnp.where(r == c, 1.0 / SB_DH, 0.0).astype(BF16)


def _attn_norm_fwd(og, proj, osb, gnw, snw):
    LP = og.shape[0]
    T = _tile(LP, 256)

    def body(og_ref, z_ref, os_ref, gnw_ref, snw_ref, y_ref):
        valid = (pl.program_id(0) * T + _iota2((T, 1), 0)) >= PAD_ROWS
        z = z_ref[...]
        zg = z * _sigmoid(z)
        for h in range(GDN_HEADS):
            sl = slice(h * GDN_D, (h + 1) * GDN_D)
            o = og_ref[:, sl]
            y = o * _rms(o) * gnw_ref[...] * zg[:, sl]
            y_ref[:, sl] = jnp.where(valid, y, 0.0).astype(BF16)
        o = os_ref[...]
        msq = _dot_exact_r(o * o, _sb_group_mean())
        y = o * lax.rsqrt(msq + NORM_EPS) * snw_ref[...]
        y_ref[:, 512:] = jnp.where(valid, y, 0.0).astype(BF16)

    row = pl.BlockSpec((T, 512), lambda i: (i, 0))
    return pl.pallas_call(
        body, name="attn_norm_fwd", grid=(LP // T,),
        in_specs=[row, pl.BlockSpec((T, 512), lambda i: (i, OFF_Z // 512)), row,
                  pl.BlockSpec((1, GDN_D), lambda i: (0, 0)), pl.BlockSpec((1, 512), lambda i: (0, 0))],
        out_specs=pl.BlockSpec((T, 1024), lambda i: (i, 0)),
        out_shape=jax.ShapeDtypeStruct((LP, 1024), BF16),
        compiler_params=_cp(("parallel",)),
    )(og, proj, osb, gnw, snw)


def _attn_norm_bwd(og, proj, osb, gnw, snw, dy, carry=None):
    LP = og.shape[0]
    T = _tile(LP, 256)

    def body(og_ref, z_ref, os_ref, gnw_ref, snw_ref, dy_ref, dog_ref, dz_ref, dos_ref, dgw_ref, dsw_ref):
        @pl.when(pl.program_id(0) == 0)
        def _():
            dgw_ref[...] = jnp.zeros_like(dgw_ref)
            dsw_ref[...] = jnp.zeros_like(dsw_ref)
        valid = (pl.program_id(0) * T + _iota2((T, 1), 0)) >= PAD_ROWS
        dy = jnp.where(valid, dy_ref[...], 0.0)
        z = z_ref[...]
        sg = _sigmoid(z)
        zg = z * sg
        dgw = jnp.zeros((1, GDN_D), F32)
        for h in range(GDN_HEADS):
            sl = slice(h * GDN_D, (h + 1) * GDN_D)
            o = og_ref[:, sl]
            dyh = dy[:, sl]
            dx, dwn = _rms_bwd(o, gnw_ref[...], dyh * zg[:, sl])
            dog_ref[:, sl] = dx
            dgw = dgw + _colsum(dwn)
            yn = o * _rms(o) * gnw_ref[...]
            dz_ref[:, sl] = (dyh * yn * (sg[:, sl] * (1.0 + z[:, sl] * (1.0 - sg[:, sl])))).astype(BF16)
        dgw_ref[...] += dgw
        o = os_ref[...]
        gm = _sb_group_mean()
        r = lax.rsqrt(_dot_exact_r(o * o, gm) + NORM_EPS)
        n = o * r
        dys = dy[:, 512:]
        dyw = dys * snw_ref[...]
        dos_ref[...] = r * (dyw - n * _dot_exact_r(dyw * n, gm))
        dsw_ref[...] += _colsum(dys * n)

    row = pl.BlockSpec((T, 512), lambda i: (i, 0))
    gw = pl.BlockSpec((1, GDN_D), lambda i: (0, 0))
    sw = pl.BlockSpec((1, 512), lambda i: (0, 0))
    o512 = jax.ShapeDtypeStruct((LP, 512), F32)
    return _call_carrying(
        carry, body, LP // T, name="attn_norm_bwd",
        in_specs=[row, pl.BlockSpec((T, 512), lambda i: (i, OFF_Z // 512)), row, gw, sw,
                  pl.BlockSpec((T, 1024), lambda i: (i, 0))],
        out_specs=[row, row, row, gw, sw],
        out_shape=[o512, jax.ShapeDtypeStruct((LP, 512), BF16), o512, jax.ShapeDtypeStruct((1, GDN_D), F32),
                   jax.ShapeDtypeStruct((1, 512), F32)],
        operands=(og, proj, osb, gnw, snw, dy))


def _resid_fwd(h0, mix, w_post, w_pre):
    LP, D = h0.shape
    T = _tile(LP, 512)

    def body(h0_ref, mix_ref, wp_ref, wf_ref, h1_ref, n2_ref):
        mix = mix_ref[...]
        h1 = h0_ref[...] + mix * _rms(mix) * wp_ref[...]
        h1_ref[...] = h1
        n2_ref[...] = (h1 * _rms(h1) * wf_ref[...]).astype(BF16)

    row = pl.BlockSpec((T, D), lambda i: (i, 0))
    vec = pl.BlockSpec((1, D), lambda i: (0, 0))
    return pl.pallas_call(
        body, name="resid_fwd", grid=(LP // T,),
        in_specs=[row, row, vec, vec], out_specs=[row, row],
        out_shape=[jax.ShapeDtypeStruct((LP, D), F32), jax.ShapeDtypeStruct((LP, D), BF16)],
        compiler_params=_cp(("parallel",)),
    )(h0, mix, w_post, w_pre)


def _resid_bwd(h1, mix, w_post, w_pre, dout, dn2):
    LP, D = h1.shape
    T = _tile(LP, 512)

    def body(h1_ref, mix_ref, wp_ref, wf_ref, dout_ref, dn2_ref, dh1_ref, dmix_ref, dwf_ref, dwp_ref):
        @pl.when(pl.program_id(0) == 0)
        def _():
            dwf_ref[...] = jnp.zeros_like(dwf_ref)
            dwp_ref[...] = jnp.zeros_like(dwp_ref)
        dx, dwn = _rms_bwd(h1_ref[...], wf_ref[...], dn2_ref[...])
        dh1 = dout_ref[...] + dx
        dh1_ref[...] = dh1
        dwf_ref[...] += _colsum(dwn)
        dmix, dwn2 = _rms_bwd(mix_ref[...], wp_ref[...], dh1)
        dmix_ref[...] = dmix.astype(BF16)
        dwp_ref[...] += _colsum(dwn2)

    row = pl.BlockSpec((T, D), lambda i: (i, 0))
    vec = pl.BlockSpec((1, D), lambda i: (0, 0))
    v = jax.ShapeDtypeStruct((1, D), F32)
    return pl.pallas_call(
        body, name="resid_bwd", grid=(LP // T,),
        in_specs=[row, row, vec, vec, row, row], out_specs=[row, row, vec, vec],
        out_shape=[jax.ShapeDtypeStruct((LP, D), F32), jax.ShapeDtypeStruct((LP, D), BF16), v, v],
        compiler_params=_cp(("arbitrary",)),
    )(h1, mix, w_post, w_pre, dout, dn2)


GELU_C = 0.7978845608028654
GELU_A = 0.044715


def _gelu_parts(x):
    t = jnp.tanh(GELU_C * (x + GELU_A * x * x * x))
    return 0.5 * x * (1.0 + t), t


def _convglu_fwd(up, conv_w, conv_b):
    LP, C = up.shape
    T = _tile(LP, 128)

    def body(x_ref, halo_ref, cw_ref, cb_ref, act_ref, y_ref):
        i = pl.program_id(0)

        def conv(cols):
            ext = jnp.concatenate([jnp.where(i > 0, halo_ref[:, cols], 0.0), x_ref[:, cols]], axis=0)
            w = cw_ref[:, cols]
            y = (w[2:3] * ext[8:] + w[1:2] * pltpu.roll(ext, 1, 0)[8:] + w[0:1] * pltpu.roll(ext, 2, 0)[8:]
                 + cb_ref[:, cols])
            y_ref[:, cols] = y
            return y

        for s in range(D_FF // LANE):
            gs = slice(s * LANE, (s + 1) * LANE)
            g, _ = _gelu_parts(conv(gs))
            act_ref[:, gs] = (g * conv(slice(D_FF + s * LANE, D_FF + (s + 1) * LANE))).astype(BF16)

    t8 = T // 8
    return pl.pallas_call(
        body, name="convglu_fwd", grid=(LP // T,),
        in_specs=[pl.BlockSpec((T, C), lambda i: (i, 0)),
                  pl.BlockSpec((8, C), lambda i: (jnp.maximum(i * t8 - 1, 0), 0)),
                  pl.BlockSpec((FFN_CONV, C), lambda i: (0, 0)), pl.BlockSpec((1, C), lambda i: (0, 0))],
        out_specs=[pl.BlockSpec((T, D_FF), lambda i: (i, 0)), pl.BlockSpec((T, C), lambda i: (i, 0))],
        out_shape=[jax.ShapeDtypeStruct((LP, D_FF), BF16), jax.ShapeDtypeStruct((LP, C), F32)],
        compiler_params=_cp(("parallel",)),
    )(up, up, conv_w, conv_b)


def _convglu_bwd(up, y, conv_w, dact):
    LP, C = up.shape
    T = _tile(LP, 128)
    TE = T + 8
    nt = LP // T

    def body(x_ref, y_ref, yn_ref, cw_ref, da_ref, dan_ref, dx_ref, dcw_ref, dcb_ref):
        i = pl.program_id(0)

        @pl.when(i == 0)
        def _():
            dcw_ref[...] = jnp.zeros_like(dcw_ref)
            dcb_ref[...] = jnp.zeros_like(dcb_ref)

        last = i == nt - 1

        def back(cols, dy):
            w = cw_ref[:, cols]
            later = [dy[0:T], pltpu.roll(dy, TE - 1, 0)[0:T], pltpu.roll(dy, TE - 2, 0)[0:T]]
            x_t = x_ref[:, cols]
            dcb_ref[:, cols] += _colsum(later[0])
            for j in range(FFN_CONV):
                dcw_ref[j:j + 1, cols] += _colsum(later[FFN_CONV - 1 - j] * x_t)
            dx_ref[:, cols] = (w[2:3] * later[0] + w[1:2] * later[1] + w[0:1] * later[2]).astype(BF16)

        for s in range(D_FF // LANE):
            gs = slice(s * LANE, (s + 1) * LANE)
            vs = slice(D_FF + s * LANE, D_FF + (s + 1) * LANE)
            gate = jnp.concatenate([y_ref[:, gs], yn_ref[:, gs]], axis=0)
            val = jnp.concatenate([y_ref[:, vs], yn_ref[:, vs]], axis=0)
            g, t = _gelu_parts(gate)
            dg_dx = 0.5 * (1.0 + t) + 0.5 * gate * (1.0 - t * t) * GELU_C * (1.0 + 3.0 * GELU_A * gate * gate)
            da = jnp.concatenate([da_ref[:, gs], jnp.where(last, 0.0, dan_ref[:, gs])], axis=0)
            back(gs, da * val * dg_dx)
            back(vs, da * g)

    t8 = T // 8
    nb8 = LP // 8
    next8 = lambda w: pl.BlockSpec((8, w), lambda i: (jnp.minimum((i + 1) * t8, nb8 - 1), 0))
    row = lambda w: pl.BlockSpec((T, w), lambda i: (i, 0))
    small = lambda r: pl.BlockSpec((r, C), lambda i: (0, 0))
    return pl.pallas_call(
        body, name="convglu_bwd", grid=(nt,),
        in_specs=[row(C), row(C), next8(C), small(FFN_CONV), row(D_FF), next8(D_FF)],
        out_specs=[row(C), small(FFN_CONV), small(1)],
        out_shape=[jax.ShapeDtypeStruct((LP, C), BF16), jax.ShapeDtypeStruct((FFN_CONV, C), F32),
                   jax.ShapeDtypeStruct((1, C), F32)],
        compiler_params=_cp(("arbitrary",)),
    )(up, y, y, conv_w, dact, dact)


def _final(h1, f, w_post, target, n_real):
    LP, D = h1.shape
    T = _tile(LP, 256)

    def body(h1_ref, f_ref, w_ref, t_ref, loss_ref, dout_ref, df_ref, dw_ref):
        @pl.when(pl.program_id(0) == 0)
        def _():
            loss_ref[...] = jnp.zeros_like(loss_ref)
            dw_ref[...] = jnp.zeros_like(dw_ref)
        rows = pl.program_id(0) * T + _iota2((T, 1), 0)
        real = (rows >= ROW0) & (rows < ROW0 + n_real)
        f = f_ref[...]
        out = h1_ref[...] + f * _rms(f) * w_ref[...]
        err = jnp.where(real, out - t_ref[...], 0.0)
        loss_ref[...] += 0.5 * jnp.sum(_colsum(jnp.mean(err * err, axis=-1, keepdims=True)), axis=-1, keepdims=True)
        dout = err * (1.0 / D)
        dout_ref[...] = dout
        dx, dwn = _rms_bwd(f, w_ref[...], dout)
        df_ref[...] = dx.astype(BF16)
        dw_ref[...] += _colsum(dwn)

    row = pl.BlockSpec((T, D), lambda i: (i, 0))
    vec = pl.BlockSpec((1, D), lambda i: (0, 0))
    return pl.pallas_call(
        body, name="final_loss", grid=(LP // T,),
        in_specs=[row, row, vec, row],
        out_specs=[pl.BlockSpec((1, 128), lambda i: (0, 0)), row, row, vec],
        out_shape=[jax.ShapeDtypeStruct((1, 128), F32), jax.ShapeDtypeStruct((LP, D), F32),
                   jax.ShapeDtypeStruct((LP, D), BF16), jax.ShapeDtypeStruct((1, D), F32)],
        compiler_params=_cp(("arbitrary",)),
    )(h1, f, w_post, target)


ANY_SPEC = pl.BlockSpec(memory_space=pl.ANY)
N_CHIP = 4


def _other_chips(x, y):
    return [(1 - x, y), (x, 1 - y), (1 - x, 1 - y)]


def _gather_direct(arrs, name):
    n = len(arrs)
    npeer = N_DEV - 1

    def body(*refs):
        ins, outs = refs[:n], refs[n:2 * n]
        send_sems, recv_sems, loc_sems = refs[2 * n:]
        x, y, c = lax.axis_index("x"), lax.axis_index("y"), lax.axis_index("c")
        me = 4 * x + 2 * y + c
        copies = []
        for a in range(n):
            for kk in range(1, N_DEV):
                px = 1 - x if kk & 4 else x
                py = 1 - y if kk & 2 else y
                pc = 1 - c if kk & 1 else c
                s = a * npeer + kk - 1
                cp = pltpu.make_async_remote_copy(src_ref=ins[a], dst_ref=outs[a].at[me], send_sem=send_sems.at[s],
                                                  recv_sem=recv_sems.at[s], device_id=(px, py, pc), device_id_type=MESH)
                cp.start()
                copies.append(cp)
            own = pltpu.make_async_copy(ins[a], outs[a].at[me], loc_sems.at[a])
            own.start()
            copies.append(own)
        for cp in copies:
            cp.wait()

    shapes = [jax.ShapeDtypeStruct((N_DEV,) + tuple(a.shape), a.dtype) for a in arrs]
    return pl.pallas_call(
        body, name=name, in_specs=[ANY_SPEC] * n, out_specs=[ANY_SPEC] * n, out_shape=shapes,
        scratch_shapes=[pltpu.SemaphoreType.DMA((n * npeer,)), pltpu.SemaphoreType.DMA((n * npeer,)),
                        pltpu.SemaphoreType.DMA((n,))],
        compiler_params=pltpu.CompilerParams(has_side_effects=True),
    )(*arrs)


class _Exchange:
    def __init__(self, arrs, out_shapes, scratch, start, finish, mid=None):
        self.arrs, self.out_shapes, self.scratch = list(arrs), list(out_shapes), list(scratch)
        self.start, self.finish, self.mid = start, finish, mid

    @property
    def n(self):
        return len(self.arrs)


def _run_exchange(ex, name):
    n = ex.n

    def body(*refs):
        ins, outs, sems = refs[:n], refs[n:2 * n], refs[2 * n:]
        ex.start(ins, outs, sems)
        if ex.mid is not None:
            ex.mid(ins, outs, sems)
        ex.finish(ins, outs, sems)

    return pl.pallas_call(
        body, name=name, in_specs=[ANY_SPEC] * n, out_specs=[ANY_SPEC] * n, out_shape=ex.out_shapes,
        scratch_shapes=ex.scratch, compiler_params=pltpu.CompilerParams(has_side_effects=True),
    )(*ex.arrs)


def _carry_begin(ex, refs, step, nsteps):
    if ex is None:
        return

    @pl.when(step == 0)
    def _():
        ex.start(*refs)

    if ex.mid is not None:
        @pl.when(step == min(nsteps - 1, (3 * nsteps) // 5))
        def _():
            ex.mid(*refs)


def _carry_end(ex, refs, step, nsteps):
    if ex is None:
        return

    @pl.when(step == nsteps - 1)
    def _():
        ex.finish(*refs)


def _gather_two_level(arrs):
    n = len(arrs)
    K = 7

    def env(ins, outs, sems):
        send_sems, recv_sems, loc_sems = sems
        x, y, c = lax.axis_index("x"), lax.axis_index("y"), lax.axis_index("c")

        def cp(a, k, src, slot, to):
            return pltpu.make_async_remote_copy(src_ref=src, dst_ref=outs[a].at[slot], send_sem=send_sems.at[a * K + k],
                                                recv_sem=recv_sems.at[a * K + k], device_id=to, device_id_type=MESH)

        me = 4 * x + 2 * y + c
        owns = [pltpu.make_async_copy(ins[a], outs[a].at[me], loc_sems.at[a]) for a in range(n)]
        first = []
        for a in range(n):
            first.append(cp(a, 0, ins[a], me, (x, y, 1 - c)))
            first += [cp(a, 1 + j, ins[a], me, (px, py, c)) for j, (px, py) in enumerate(_other_chips(x, y))]
        passed = []
        for j, (px, py) in enumerate(_other_chips(x, y)):
            slot = 4 * px + 2 * py + c
            passed += [(cp(a, 1 + j, ins[a], slot, (px, py, c)), cp(a, 4 + j, outs[a].at[slot], slot, (x, y, 1 - c)))
                       for a in range(n)]
        from_sib = []
        for a in range(n):
            from_sib.append(cp(a, 0, ins[a], 4 * x + 2 * y + (1 - c), (x, y, 1 - c)))
            from_sib += [cp(a, 4 + j, ins[a], 4 * px + 2 * py + (1 - c), (x, y, 1 - c))
                         for j, (px, py) in enumerate(_other_chips(x, y))]
        return owns, first, passed, from_sib

    def start(ins, outs, sems):
        owns, first, _, _ = env(ins, outs, sems)
        for cp in owns + first:
            cp.start()

    def mid(ins, outs, sems):
        _, _, passed, _ = env(ins, outs, sems)
        for arrival, fwd in passed:
            arrival.wait_recv()
            fwd.start()

    def finish(ins, outs, sems):
        owns, first, passed, from_sib = env(ins, outs, sems)
        for cp in from_sib:
            cp.wait_recv()
        for cp in first + [fwd for _, fwd in passed]:
            cp.wait_send()
        for cp in owns:
            cp.wait()

    shapes = [jax.ShapeDtypeStruct((N_DEV,) + tuple(a.shape), a.dtype) for a in arrs]
    scratch = [pltpu.SemaphoreType.DMA((n * K,)), pltpu.SemaphoreType.DMA((n * K,)), pltpu.SemaphoreType.DMA((n,))]
    return _Exchange(arrs, shapes, scratch, start, finish, mid)


def _swap_sibling(arrs):
    n = len(arrs)

    def copies(ins, outs, sems):
        send_sems, recv_sems = sems
        x, y, c = lax.axis_index("x"), lax.axis_index("y"), lax.axis_index("c")
        return [pltpu.make_async_remote_copy(src_ref=ins[a], dst_ref=outs[a], send_sem=send_sems.at[a],
                                             recv_sem=recv_sems.at[a], device_id=(x, y, 1 - c), device_id_type=MESH)
                for a in range(n)]

    def start(ins, outs, sems):
        for cp in copies(ins, outs, sems):
            cp.start()

    def finish(ins, outs, sems):
        for cp in copies(ins, outs, sems):
            cp.wait()

    shapes = [jax.ShapeDtypeStruct(tuple(a.shape), a.dtype) for a in arrs]
    return _Exchange(arrs, shapes, [pltpu.SemaphoreType.DMA((n,)), pltpu.SemaphoreType.DMA((n,))], start, finish)


def _exchange_chips(arrs):
    n = len(arrs)
    K = N_CHIP - 1

    def copies(ins, outs, sems):
        send_sems, recv_sems, loc_sems = sems
        x, y, c = lax.axis_index("x"), lax.axis_index("y"), lax.axis_index("c")
        mine = 2 * x + y
        out = []
        for a in range(n):
            out += [pltpu.make_async_remote_copy(src_ref=ins[a].at[2 * px + py], dst_ref=outs[a].at[mine],
                                                 send_sem=send_sems.at[a * K + j], recv_sem=recv_sems.at[a * K + j],
                                                 device_id=(px, py, c), device_id_type=MESH)
                    for j, (px, py) in enumerate(_other_chips(x, y))]
            out.append(pltpu.make_async_copy(ins[a].at[mine], outs[a].at[mine], loc_sems.at[a]))
        return out

    def start(ins, outs, sems):
        for cp in copies(ins, outs, sems):
            cp.start()

    def finish(ins, outs, sems):
        for cp in copies(ins, outs, sems):
            cp.wait()

    shapes = [jax.ShapeDtypeStruct(tuple(a.shape), a.dtype) for a in arrs]
    scratch = [pltpu.SemaphoreType.DMA((n * K,)), pltpu.SemaphoreType.DMA((n * K,)), pltpu.SemaphoreType.DMA((n,))]
    return _Exchange(arrs, shapes, scratch, start, finish)


def _add_halves(mine, theirs, name):
    _, R, C = mine.shape
    cap = max(16, (ELEMWISE_VMEM // (4 * C * 10)) // 16 * 16)
    T = R if R <= cap else _tile(R, cap, 16)

    def body(a_ref, b_ref, o_ref):
        o_ref[...] = (a_ref[...] + b_ref[...].astype(F32)).astype(BF16)

    blk = pl.BlockSpec((N_CHIP, T, C), lambda i: (0, i, 0))
    return pl.pallas_call(
        body, name=name, grid=(R // T,), in_specs=[blk, blk], out_specs=blk,
        out_shape=jax.ShapeDtypeStruct(mine.shape, BF16), compiler_params=_cp(("parallel",)),
    )(mine, theirs)


def _adamw(parts, w, m, v, name):
    R, C = w.shape
    npart = parts.shape[0]
    cap = max(16, (ELEMWISE_VMEM // (4 * C * 12)) // 16 * 16)
    T = R if R <= cap else _tile(R, cap, 16)

    def body(p_ref, w_ref, m_ref, v_ref, g_ref, d_ref, nm_ref, nv_ref):
        g = p_ref[0].astype(F32)
        for k in range(1, npart):
            g = g + p_ref[k].astype(F32)
        mm = ADAM_B1 * m_ref[...] + (1.0 - ADAM_B1) * g
        vv = ADAM_B2 * v_ref[...] + (1.0 - ADAM_B2) * (g * g)
        m_hat = mm / (1.0 - ADAM_B1 ** ADAM_STEP)
        v_hat = vv / (1.0 - ADAM_B2 ** ADAM_STEP)
        g_ref[...] = g
        d_ref[...] = -ADAM_LR * (m_hat / (jnp.sqrt(v_hat) + ADAM_EPS) + ADAM_WD * w_ref[...])
        nm_ref[...] = mm
        nv_ref[...] = vv

    row = pl.BlockSpec((T, C), lambda i: (i, 0))
    out = jax.ShapeDtypeStruct((R, C), F32)
    return pl.pallas_call(
        body, name=name, grid=(R // T,),
        in_specs=[pl.BlockSpec((npart, T, C), lambda i: (0, i, 0)), row, row, row],
        out_specs=[row] * 4, out_shape=[out] * 4,
        compiler_params=_cp(("parallel",)),
    )(parts, w, m, v)


SMALL = ("attn_pre_norm", "gdn_A_log", "gdn_dt_bias", "gdn_norm_w", "sb_norm_w", "attn_post_norm",
         "ffn_pre_norm", "ffn_conv_b", "ffn_post_norm")


def _pack_small(arrs):
    rows = []
    for a in arrs:
        flat = a.reshape(-1).astype(F32)
        n = -(-flat.shape[0] // 128) * 128
        rows.append(jnp.pad(flat, (0, n - flat.shape[0])).reshape(-1, 128))
    slab = jnp.concatenate(rows, axis=0)
    pad = (-slab.shape[0]) % 8
    return jnp.pad(slab, ((0, pad), (0, 0)))


def _unpack_small(slab, shapes):
    out, r = [], 0
    for shp in shapes:
        size = 1
        for s in shp:
            size *= s
        nr = -(-size // 128)
        out.append(slab[r:r + nr].reshape(-1)[:size].reshape(shp))
        r += nr
    return out


def _to_blocks_cols(a):
    R, C = a.shape
    return a.reshape(R, N_DEV, C // N_DEV).transpose(1, 0, 2)


def _from_blocks_cols(a):
    n, R, c = a.shape
    return a.transpose(1, 0, 2).reshape(R, n * c)


def kernel(x, meta_tokens, attn_pre_norm, w_in, gdn_conv_w, gdn_A_log, gdn_dt_bias, gdn_norm_w, sb_norm_w, w_out, attn_post_norm, ffn_pre_norm, w_ffn_up, ffn_conv_w, ffn_conv_b, w_ffn_down, ffn_post_norm, loss_target, m_meta_tokens, m_attn_pre_norm, m_w_in, m_gdn_conv_w, m_gdn_A_log, m_gdn_dt_bias, m_gdn_norm_w, m_sb_norm_w, m_w_out, m_attn_post_norm, m_ffn_pre_norm, m_w_ffn_up, m_ffn_conv_w, m_ffn_conv_b, m_w_ffn_down, m_ffn_post_norm, v_meta_tokens, v_attn_pre_norm, v_w_in, v_gdn_conv_w, v_gdn_A_log, v_gdn_dt_bias, v_gdn_norm_w, v_sb_norm_w, v_w_out, v_attn_post_norm, v_ffn_pre_norm, v_w_ffn_up, v_ffn_conv_w, v_ffn_conv_b, v_w_ffn_down, v_ffn_post_norm):
    args = dict(locals())
    seq = x.shape[1]
    LP = -(-(ROW0 + seq) // LP_ALIGN) * LP_ALIGN
    tail = LP - ROW0 - seq

    meta_f = _from_blocks_cols(_run_exchange(_gather_two_level([meta_tokens]), "gather_meta")[0])

    (h0, target), got = _build_rows(x[0], meta_f, loss_target[0], LP,
                                    carry=_gather_two_level([w_in[0].astype(BF16), gdn_conv_w[0]]))
    (u,), _ = _prenorm_fwd(h0, attn_pre_norm)
    win_o = _from_blocks_cols(got[0])
    o_ab = C_QKV
    o_z = o_ab + 2 * GDN_HEADS
    w_inp = jnp.concatenate([win_o[:, :C_QKV], win_o[:, o_z:o_z + C_Z], win_o[:, o_z + C_Z:],
                             win_o[:, o_ab:o_z], jnp.zeros((D_MODEL, C_AB - 2 * GDN_HEADS), BF16)], axis=1)
    gconv_f = _from_blocks_cols(got[1])
    proj = _mm(u, w_inp, F32, "mm_in")
    (qn, kn, vg, beta_b, g_b), got = _gdn_pre_fwd(
        proj, gconv_f, gdn_A_log, gdn_dt_bias,
        carry=_gather_two_level([w_out[0].astype(BF16), w_ffn_down[0].astype(BF16)]))
    w_out_f = got[0].reshape(D_MODEL, D_MODEL)
    w_down_f = got[1].reshape(D_FF, D_MODEL)
    (cu, cw, cqd, ckd, cqk, ct, cgl), got = _gdn_chunk_fwd(
        qn, kn, vg, beta_b, g_b, carry=_gather_two_level([w_ffn_up[0].astype(BF16), ffn_conv_w[0]]))
    w_up_f = _from_blocks_cols(got[0])
    fconv_f = _from_blocks_cols(got[1])
    og, ssave = _gdn_scan_fwd(cu, cw, cqd, ckd, cqk, cgl)
    osb, ctot, sb_nrun = _sb_fwd(proj)
    snw = sb_norm_w.reshape(1, SB_HEADS * SB_DH)
    y = _attn_norm_fwd(og, proj, osb, gdn_norm_w, snw)
    mix = _mm(y, w_out_f, F32, "mm_out")
    h1, n2 = _resid_fwd(h0, mix, attn_post_norm, ffn_pre_norm)
    up = _mm(n2, w_up_f, F32, "mm_up")
    act, conv_y = _convglu_fwd(up, fconv_f, ffn_conv_b)
    f = _mm(act, w_down_f, F32, "mm_down")
    loss_part, dout, df, d_fpost = _final(h1, f, ffn_post_norm, target, seq)

    d_wdown = _mm_tn(act, df, "mm_dw_down")
    dact = _mm_nt(df, w_down_f, F32, "mm_dact")
    dup, d_fconv, d_fconvb = _convglu_bwd(up, conv_y, fconv_f, dact)
    d_wup = _mm_tn(n2, dup, "mm_dw_up")
    dn2 = _mm_nt(dup, w_up_f, F32, "mm_dn2")
    dh1, dmix, d_fpre, d_apost = _resid_bwd(h1, mix, attn_post_norm, ffn_pre_norm, dout, dn2)
    d_wout = _mm_tn(y, dmix, "mm_dw_out")
    dy = _mm_nt(dmix, w_out_f, F32, "mm_dy")
    my_c = lax.axis_index("c")

    def core_halves(blocks):
        halves = [s.reshape((N_CHIP, 2) + s.shape[1:]) for s in blocks]
        return ([lax.dynamic_index_in_dim(h, my_c, axis=1, keepdims=False) for h in halves],
                [lax.dynamic_index_in_dim(h, 1 - my_c, axis=1, keepdims=False).astype(BF16) for h in halves])

    early_names = ("w_out", "w_ffn_up", "w_ffn_down", "ffn_conv_w")
    e_mine, e_send = core_halves([d_wout.reshape(N_DEV, D_MODEL // N_DEV, D_MODEL), _to_blocks_cols(d_wup),
                                  d_wdown.reshape(N_DEV, D_FF // N_DEV, D_MODEL), _to_blocks_cols(d_fconv)])
    (dog, dz, dos, d_gnw, d_snw), e_theirs = _attn_norm_bwd(og, proj, osb, gdn_norm_w, snw, dy,
                                                            carry=_swap_sibling(e_send))
    e_sums = [_add_halves(a, b, "grads_add_" + nm) for nm, a, b in zip(early_names, e_mine, e_theirs)]
    dqs, dks, dvs = _sb_bwd(proj, ctot, sb_nrun, dos)
    (du_, dw_, dqd_, dkd_, dqk_, dgl_), _ = _gdn_scan_bwd(cu, cw, cqd, ckd, cqk, cgl, ssave, dog)
    dqn, dkn, dvg, dbeta, dg = _gdn_chunk_bwd(qn, kn, vg, beta_b, g_b, ct, du_, dw_, dqd_, dkd_, dqk_, dgl_)
    (dqkv, dab, d_gconv, d_gsc), e_recv = _gdn_pre_bwd(proj, gconv_f, gdn_A_log, gdn_dt_bias, dqn, dkn, dvg, dbeta, dg,
                                                       carry=_exchange_chips(e_sums))
    dpieces = [dqkv, dz, dqs, dks, dvs, dab]
    doffs = [0, OFF_Z, OFF_SB, OFF_SB + 512, OFF_SB + 1024, OFF_AB]
    dw_qkv, dw_ab = _mm_tn_pieces(u, [dqkv, dab], "mm_dw_in_gdn")
    dw_z, dw_qs, dw_ks, dw_vs = _mm_tn_pieces(u, [dz, dqs, dks, dvs], "mm_dw_in_rest")
    du0 = _mm_nt_pieces(dpieces, doffs, w_inp, F32, "mm_du")
    d_win = jnp.concatenate([dw_qkv, dw_ab[:, :2 * GDN_HEADS], dw_z, dw_qs, dw_ks, dw_vs], axis=1)
    late_names = ("w_in", "gdn_conv_w")
    l_mine, l_send = core_halves([_to_blocks_cols(d_win), _to_blocks_cols(d_gconv)])
    l_theirs = _run_exchange(_swap_sibling(l_send), "grads_swap_sibling")
    l_sums = [_add_halves(a, b, "grads_add_" + nm) for nm, a, b in zip(late_names, l_mine, l_theirs)]
    (dh0, d_apre), l_recv = _prenorm_bwd(h0, attn_pre_norm, du0, dh1, carry=_exchange_chips(l_sums))
    grad_x = dh0[ROW0:ROW0 + seq][None]
    d_meta = dh0[PAD_ROWS:ROW0]

    small_grads = [d_apre, d_gsc[0:1, :GDN_HEADS], d_gsc[1:2, :GDN_HEADS], d_gnw, d_snw.reshape(1, SB_HEADS, SB_DH),
                   d_apost, d_fpre, d_fconvb, d_fpost]
    loss_rows = jnp.pad(loss_part, ((0, 7), (0, 0)))
    n_param_rows = _pack_small(small_grads).shape[0]
    n_small_rows = n_param_rows + loss_rows.shape[0]
    slab_parts = _gather_direct(
        [jnp.concatenate([_pack_small(small_grads), loss_rows, d_meta.reshape(-1, LANE)], axis=0)],
        name="gather_small_grads")[0]
    me = 4 * lax.axis_index("x") + 2 * lax.axis_index("y") + my_c
    meta_parts = lax.dynamic_index_in_dim(
        slab_parts[:, n_small_rows:].reshape(N_DEV, N_META, N_DEV, LANE), me, axis=2, keepdims=False)
    slab_parts = slab_parts[:, :n_small_rows]

    res = {}
    for nm, parts in zip(early_names + late_names + ("meta_tokens",), list(e_recv) + list(l_recv) + [meta_parts]):
        wloc = args[nm]
        shp = wloc.shape
        w2 = wloc.reshape(shp[-2], shp[-1])
        outs = _adamw(parts, w2, args["m_" + nm].reshape(w2.shape), args["v_" + nm].reshape(w2.shape), "adamw_" + nm)
        res[nm] = [o.reshape(shp) for o in outs]
    small_shapes = [args[nm].shape for nm in SMALL]
    with_loss_rows = lambda slab: jnp.pad(slab, ((0, n_small_rows - n_param_rows), (0, 0)))
    outs = _adamw(slab_parts, with_loss_rows(_pack_small([args[nm] for nm in SMALL])),
                  with_loss_rows(_pack_small([args["m_" + nm] for nm in SMALL])),
                  with_loss_rows(_pack_small([args["v_" + nm] for nm in SMALL])), "adamw_small")
    loss = outs[0][n_param_rows, 0]
    for k in range(4):
        for nm, val in zip(SMALL, _unpack_small(outs[k], small_shapes)):
            res.setdefault(nm, [None] * 4)[k] = val

    order = ("meta_tokens", "attn_pre_norm", "w_in", "gdn_conv_w", "gdn_A_log", "gdn_dt_bias", "gdn_norm_w",
             "sb_norm_w", "w_out", "attn_post_norm", "ffn_pre_norm", "w_ffn_up", "ffn_conv_w", "ffn_conv_b",
             "w_ffn_down", "ffn_post_norm")
    return (loss, grad_x, *[res[nm][0] for nm in order], *[res[nm][1] for nm in order],
            *[res[nm][2] for nm in order], *[res[nm][3] for nm in order])
```

```python
import functools

import jax
import jax.numpy as jnp
from jax import lax
from jax.experimental import pallas as pl
from jax.experimental.pallas import tpu as pltpu

F32 = jnp.float32
BF16 = jnp.bfloat16

D_MODEL = 1024
N_META = 16
GDN_HEADS = 4
GDN_D = 128
GDN_CHUNK = 64
GDN_CONV = 4
GDN_ROWS = 256
SCAN_CHUNKS = 4
SB_HEADS = 8
SB_DH = 64
SB_BLOCK = 128
D_FF = 2816
FFN_CONV = 3
NORM_EPS = 1e-6
L2_EPS = 1e-6
LANE = 128
N_DEV = 8

PAD_ROWS = SB_BLOCK - N_META
ROW0 = SB_BLOCK
SB_SPAN = 512
SB_DEAD = -104.0
SB_SUB = 256
SB_QTILE = 256
LP_ALIGN = 256

C_QKV = 3 * GDN_HEADS * GDN_D
C_Z = GDN_HEADS * GDN_D
C_SB = 3 * SB_HEADS * SB_DH
C_AB = 256
OFF_Z = C_QKV
OFF_SB = OFF_Z + C_Z
OFF_AB = OFF_SB + C_SB
D_INP = OFF_AB + C_AB
D_IN = C_QKV + 2 * GDN_HEADS + C_Z + C_SB

ADAM_LR = 0.001
ADAM_B1 = 0.9
ADAM_B2 = 0.999
ADAM_EPS = 1e-08
ADAM_WD = 0.01
ADAM_STEP = 10

VMEM_LIMIT = 56 * 1024 * 1024
ELEMWISE_VMEM = 8 * 1024 * 1024
MESH = pl.DeviceIdType.MESH


def _cp(sem=None):
    kw = dict(vmem_limit_bytes=VMEM_LIMIT)
    if sem is not None:
        kw["dimension_semantics"] = sem
    return pltpu.CompilerParams(**kw)


def _tile(n, cap, unit=128):
    best = None
    t = unit
    while t <= min(n, cap):
        if n % t == 0:
            best = t
        t += unit
    assert best is not None, (n, cap, unit)
    return best


def _dot(a, b):
    return jnp.dot(a, b, preferred_element_type=F32)


def _dot_nt(a, b):
    return lax.dot_general(a, b, (((1,), (1,)), ((), ())), preferred_element_type=F32)


def _dot_tn(a, b):
    return lax.dot_general(a, b, (((0,), (0,)), ((), ())), preferred_element_type=F32)


def _split(x):
    hi = x.astype(BF16)
    lo = (x - hi.astype(F32)).astype(BF16)
    return hi, lo


def _dot1(a, b, f=_dot):
    return f(a.astype(BF16), b.astype(BF16))


def _dot3(a, b, f=_dot):
    ah, al = _split(a)
    bh, bl = _split(b)
    return f(ah, bh) + (f(ah, bl) + f(al, bh))


def _dot_exact_l(m_bf16, x, f=_dot):
    xh, xl = _split(x)
    return f(m_bf16, xh) + f(m_bf16, xl)


def _dot_exact_r(x, m_bf16, f=_dot):
    xh, xl = _split(x)
    return f(xh, m_bf16) + f(xl, m_bf16)


def _iota2(shape, dim):
    return lax.broadcasted_iota(jnp.int32, shape, dim)


def _sigmoid(x):
    return 1.0 / (1.0 + jnp.exp(-x))


def _softplus(x):
    return jnp.maximum(x, 0.0) + jnp.log(1.0 + jnp.exp(-jnp.abs(x)))


def _colsum(x):
    return jnp.sum(x, axis=0, keepdims=True)


def _rowsum(x):
    return jnp.sum(x, axis=-1, keepdims=True)


def _mm(a, b, out_dtype, name):
    M, K = a.shape
    K2, N = b.shape
    assert K == K2
    tm = _tile(M, 768)
    tn = _tile(N, max(128, (6 * 1024 * 1024) // (2 * K)))

    def body(a_ref, b_ref, o_ref):
        o_ref[...] = _dot(a_ref[...].astype(BF16), b_ref[...].astype(BF16)).astype(o_ref.dtype)

    return pl.pallas_call(
        body, name=name, grid=(N // tn, M // tm),
        in_specs=[pl.BlockSpec((tm, K), lambda j, i: (i, 0)), pl.BlockSpec((K, tn), lambda j, i: (0, j))],
        out_specs=pl.BlockSpec((tm, tn), lambda j, i: (i, j)),
        out_shape=jax.ShapeDtypeStruct((M, N), out_dtype),
        compiler_params=_cp(("parallel", "parallel")),
    )(a, b)


def _mm_nt(a, b, out_dtype, name):
    M, K = a.shape
    N, K2 = b.shape
    assert K == K2
    tm = _tile(M, 768)
    tn = _tile(N, max(128, (6 * 1024 * 1024) // (2 * K)))

    def body(a_ref, b_ref, o_ref):
        o_ref[...] = _dot_nt(a_ref[...].astype(BF16), b_ref[...].astype(BF16)).astype(o_ref.dtype)

    return pl.pallas_call(
        body, name=name, grid=(N // tn, M // tm),
        in_specs=[pl.BlockSpec((tm, K), lambda j, i: (i, 0)), pl.BlockSpec((tn, K), lambda j, i: (j, 0))],
        out_specs=pl.BlockSpec((tm, tn), lambda j, i: (i, j)),
        out_shape=jax.ShapeDtypeStruct((M, N), out_dtype),
        compiler_params=_cp(("parallel", "parallel")),
    )(a, b)


def _mm_nt_pieces(pieces, offsets, b, out_dtype, name):
    M = pieces[0].shape[0]
    N = b.shape[0]
    n = len(pieces)
    widths = [p.shape[1] for p in pieces]
    assert all(off % k == 0 for off, k in zip(offsets, widths))
    tm = _tile(M, 768)
    tn = _tile(N, 512)

    def body(*refs):
        acc = _dot_nt(refs[0][...].astype(BF16), refs[n][...].astype(BF16))
        for p in range(1, n):
            acc = acc + _dot_nt(refs[p][...].astype(BF16), refs[n + p][...].astype(BF16))
        refs[2 * n][...] = acc.astype(out_dtype)

    return pl.pallas_call(
        body, name=name, grid=(N // tn, M // tm),
        in_specs=[pl.BlockSpec((tm, k), lambda j, i: (i, 0)) for k in widths]
        + [pl.BlockSpec((tn, k), functools.partial(lambda j, i, blk: (j, blk), blk=off // k))
           for off, k in zip(offsets, widths)],
        out_specs=pl.BlockSpec((tm, tn), lambda j, i: (i, j)),
        out_shape=jax.ShapeDtypeStruct((M, N), out_dtype),
        compiler_params=_cp(("parallel", "parallel")),
    )(*pieces, *([b] * n))


def _mm_tn_pieces(a, pieces, name):
    M, K = a.shape
    n = len(pieces)
    tm = _tile(M, 768)

    def body(*refs):
        @pl.when(pl.program_id(0) == 0)
        def _():
            for p in range(n):
                refs[1 + n + p][...] = jnp.zeros_like(refs[1 + n + p])
        at = refs[0][...].astype(BF16)
        for p in range(n):
            refs[1 + n + p][...] += _dot_tn(at, refs[1 + p][...].astype(BF16))

    return pl.pallas_call(
        body, name=name, grid=(M // tm,),
        in_specs=[pl.BlockSpec((tm, K), lambda m: (m, 0))] + [pl.BlockSpec((tm, p.shape[1]), lambda m: (m, 0)) for p in pieces],
        out_specs=[pl.BlockSpec((K, p.shape[1]), lambda m: (0, 0)) for p in pieces],
        out_shape=[jax.ShapeDtypeStruct((K, p.shape[1]), F32) for p in pieces],
        compiler_params=_cp(("arbitrary",)),
    )(a, *pieces)


def _mm_tn(a, b, name):
    M, K = a.shape
    M2, N = b.shape
    assert M == M2
    tm = _tile(M, 1408)
    tk = _tile(K, 1408)
    tn = _tile(N, 1408)

    def body(a_ref, b_ref, o_ref):
        @pl.when(pl.program_id(2) == 0)
        def _():
            o_ref[...] = jnp.zeros_like(o_ref)
        o_ref[...] += _dot_tn(a_ref[...].astype(BF16), b_ref[...].astype(BF16))

    return pl.pallas_call(
        body, name=name, grid=(K // tk, N // tn, M // tm),
        in_specs=[pl.BlockSpec((tm, tk), lambda i, j, m: (m, i)), pl.BlockSpec((tm, tn), lambda i, j, m: (m, j))],
        out_specs=pl.BlockSpec((tk, tn), lambda i, j, m: (i, j)),
        out_shape=jax.ShapeDtypeStruct((K, N), F32),
        compiler_params=_cp(("parallel", "parallel", "arbitrary")),
    )(a, b)


def _rms(x):
    return lax.rsqrt(jnp.mean(x * x, axis=-1, keepdims=True) + NORM_EPS)


def _rms_bwd(x, w, dy):
    r = _rms(x)
    n = x * r
    dyw = dy * w
    dx = r * (dyw - n * jnp.mean(dyw * n, axis=-1, keepdims=True))
    return dx, dy * n


def _build_rows(x, meta, target, LP, carry=None):
    seq, D = x.shape
    T = SB_BLOCK
    nx = seq // T
    assert seq % T == 0 and meta.shape[0] == N_META

    def body(x_ref, m_ref, t_ref, h_ref, tp_ref):
        i = pl.program_id(0)
        inside = (i >= 1) & (i <= nx)
        head = jnp.concatenate([jnp.zeros((PAD_ROWS, D), F32), m_ref[...]], axis=0)
        h_ref[...] = jnp.where(i == 0, head, jnp.where(inside, x_ref[...], 0.0))
        tp_ref[...] = jnp.where(inside, t_ref[...], 0.0)

    tok = pl.BlockSpec((T, D), lambda i: (jnp.clip(i - 1, 0, nx - 1), 0))
    row = pl.BlockSpec((T, D), lambda i: (i, 0))
    out = jax.ShapeDtypeStruct((LP, D), F32)
    return _call_carrying(
        carry, body, LP // T, name="build_rows",
        in_specs=[tok, pl.BlockSpec((N_META, D), lambda i: (0, 0)), tok], out_specs=[row, row], out_shape=[out, out],
        operands=(x, meta, target))


def _prenorm_fwd(h0, w, carry=None):
    LP, D = h0.shape
    T = _tile(LP, 512)

    def body(h_ref, w_ref, u_ref):
        h = h_ref[...]
        u_ref[...] = (h * _rms(h) * w_ref[...]).astype(BF16)

    return _call_carrying(
        carry, body, LP // T, name="prenorm_fwd",
        in_specs=[pl.BlockSpec((T, D), lambda i: (i, 0)), pl.BlockSpec((1, D), lambda i: (0, 0))],
        out_specs=[pl.BlockSpec((T, D), lambda i: (i, 0))],
        out_shape=[jax.ShapeDtypeStruct((LP, D), BF16)],
        operands=(h0, w))


def _prenorm_bwd(h0, w, du, dh1, carry=None):
    LP, D = h0.shape
    T = _tile(LP, 512)

    def body(h_ref, w_ref, du_ref, dh1_ref, dh0_ref, dw_ref):
        @pl.when(pl.program_id(0) == 0)
        def _():
            dw_ref[...] = jnp.zeros_like(dw_ref)
        dx, dwn = _rms_bwd(h_ref[...], w_ref[...], du_ref[...])
        dh0_ref[...] = dh1_ref[...] + dx
        dw_ref[...] += _colsum(dwn)

    row = pl.BlockSpec((T, D), lambda i: (i, 0))
    vec = pl.BlockSpec((1, D), lambda i: (0, 0))
    return _call_carrying(
        carry, body, LP // T, name="prenorm_bwd",
        in_specs=[row, vec, row, row], out_specs=[row, vec],
        out_shape=[jax.ShapeDtypeStruct((LP, D), F32), jax.ShapeDtypeStruct((1, D), F32)],
        operands=(h0, w, du, dh1))


def _gdn_gate_consts(alog_ref, dtb_ref, h):
    a_coef = -jnp.exp(alog_ref[0:1, h:h + 1])
    return a_coef, dtb_ref[0:1, h:h + 1]


def _gdn_pre_fwd(proj, conv_w, a_log, dt_bias, carry=None):
    LP = proj.shape[0]
    T = _tile(LP, 256)
    C = C_QKV
    H = GDN_HEADS

    def body(x_ref, halo_ref, ab_ref, cw_ref, alog_ref, dtb_ref, q_ref, k_ref, v_ref, beta_ref, g_ref):
        i = pl.program_id(0)

        def conv_silu(cols):
            ext = jnp.concatenate([jnp.where(i > 0, halo_ref[:, cols], 0.0), x_ref[:, cols]], axis=0)
            w = cw_ref[:, cols]
            y = w[GDN_CONV - 1:GDN_CONV] * ext[8:]
            for j in range(GDN_CONV - 1):
                y = y + w[j:j + 1] * pltpu.roll(ext, GDN_CONV - 1 - j, 0)[8:]
            return y * _sigmoid(y)

        for h in range(H):
            sl = slice(h * GDN_D, (h + 1) * GDN_D)
            cq = conv_silu(sl)
            q_ref[:, sl] = cq * lax.rsqrt(_rowsum(cq * cq) + L2_EPS) * (GDN_D ** -0.5)
            ck = conv_silu(slice(512 + h * GDN_D, 512 + (h + 1) * GDN_D))
            k_ref[:, sl] = ck * lax.rsqrt(_rowsum(ck * ck) + L2_EPS)
            v_ref[:, sl] = conv_silu(slice(1024 + h * GDN_D, 1024 + (h + 1) * GDN_D))
        ab = ab_ref[...]
        valid = (i * T + _iota2((T, 1), 0)) >= PAD_ROWS
        for h in range(H):
            sl = slice(h * GDN_D, (h + 1) * GDN_D)
            a_coef, dtb = _gdn_gate_consts(alog_ref, dtb_ref, h)
            g = jnp.where(valid, a_coef * _softplus(ab[:, h:h + 1] + dtb), 0.0)
            beta = jnp.where(valid, _sigmoid(ab[:, H + h:H + h + 1]), 0.0)
            g_ref[:, sl] = jnp.broadcast_to(g, (T, GDN_D))
            beta_ref[:, sl] = jnp.broadcast_to(beta, (T, GDN_D))

    t8 = T // 8
    row512 = pl.BlockSpec((T, 512), lambda i: (i, 0))
    small = lambda r, c: pl.BlockSpec((r, c), lambda i: (0, 0))
    out = jax.ShapeDtypeStruct((LP, 512), F32)
    return _call_carrying(
        carry, body, LP // T, name="gdn_pre_fwd",
        in_specs=[pl.BlockSpec((T, C), lambda i: (i, 0)),
                  pl.BlockSpec((8, C), lambda i: (jnp.maximum(i * t8 - 1, 0), 0)),
                  pl.BlockSpec((T, C_AB), lambda i: (i, OFF_AB // C_AB)),
                  small(GDN_CONV, C), small(1, H), small(1, H)],
        out_specs=[row512] * 5, out_shape=[out] * 5,
        operands=(proj, proj, proj, conv_w, a_log, dt_bias))


def _gdn_pre_bwd(proj, conv_w, a_log, dt_bias, dq, dk, dv, dbeta, dg, carry=None):
    LP = proj.shape[0]
    T = _tile(LP, 256)
    C = C_QKV
    H = GDN_HEADS
    TE = T + 8
    nt = LP // T

    def body(x_ref, xp_ref, xn_ref, ab_ref, cw_ref, alog_ref, dtb_ref,
             dq_ref, dqn_ref, dk_ref, dkn_ref, dv_ref, dvn_ref, dbeta_ref, dg_ref,
             dx_ref, dab_ref, dcw_ref, dsc_ref):
        i = pl.program_id(0)

        @pl.when(i == 0)
        def _():
            dcw_ref[...] = jnp.zeros_like(dcw_ref)
            dsc_ref[...] = jnp.zeros_like(dsc_ref)

        last = i == nt - 1

        def strip(cols, d_ref, dn_ref, dcols, scale):
            ext = jnp.concatenate([jnp.where(i > 0, xp_ref[:, cols], 0.0), x_ref[:, cols],
                                   jnp.where(last, 0.0, xn_ref[:, cols])], axis=0)
            sh = [ext[8:8 + TE]] + [pltpu.roll(ext, s, 0)[8:8 + TE] for s in range(1, GDN_CONV)]
            w = cw_ref[:, cols]
            y = w[GDN_CONV - 1:GDN_CONV] * sh[0]
            for j in range(GDN_CONV - 1):
                y = y + w[j:j + 1] * sh[GDN_CONV - 1 - j]
            sg = _sigmoid(y)
            d = jnp.concatenate([d_ref[:, dcols], jnp.where(last, 0.0, dn_ref[:, dcols])], axis=0)
            if scale is not None:
                c = y * sg
                r = lax.rsqrt(_rowsum(c * c) + L2_EPS)
                n = c * r
                d = scale * r * (d - n * _rowsum(d * n))
            dy = d * (sg * (1.0 + y * (1.0 - sg)))
            dy_t = dy[0:T]
            for j in range(GDN_CONV):
                dcw_ref[j:j + 1, cols] += _colsum(dy_t * sh[GDN_CONV - 1 - j][0:T])
            dx = w[GDN_CONV - 1:GDN_CONV] * dy_t
            for j in range(GDN_CONV - 1):
                dx = dx + w[j:j + 1] * pltpu.roll(dy, TE - (GDN_CONV - 1 - j), 0)[0:T]
            dx_ref[:, cols] = dx.astype(BF16)

        for h in range(H):
            sl = slice(h * GDN_D, (h + 1) * GDN_D)
            strip(sl, dq_ref, dqn_ref, sl, GDN_D ** -0.5)
            strip(slice(512 + h * GDN_D, 512 + (h + 1) * GDN_D), dk_ref, dkn_ref, sl, 1.0)
            strip(slice(1024 + h * GDN_D, 1024 + (h + 1) * GDN_D), dv_ref, dvn_ref, sl, None)
        ab = ab_ref[...]
        valid = (i * T + _iota2((T, 1), 0)) >= PAD_ROWS
        lane = _iota2((T, C_AB), 1)
        lane1 = _iota2((1, 128), 1)
        dab = jnp.zeros((T, C_AB), F32)
        dsc_a = jnp.zeros((1, 128), F32)
        dsc_d = jnp.zeros((1, 128), F32)
        for h in range(H):
            a_coef, dtb = _gdn_gate_consts(alog_ref, dtb_ref, h)
            pre = ab[:, h:h + 1] + dtb
            dgh = jnp.where(valid, dg_ref[:, h * GDN_D:h * GDN_D + 1], 0.0)
            da = dgh * a_coef * _sigmoid(pre)
            beta = _sigmoid(ab[:, H + h:H + h + 1])
            db = jnp.where(valid, dbeta_ref[:, h * GDN_D:h * GDN_D + 1], 0.0) * beta * (1.0 - beta)
            dab = dab + jnp.where(lane == h, da, 0.0) + jnp.where(lane == H + h, db, 0.0)
            dsc_a = dsc_a + jnp.where(lane1 == h, _colsum(dgh * a_coef * _softplus(pre)), 0.0)
            dsc_d = dsc_d + jnp.where(lane1 == h, _colsum(da), 0.0)
        dab_ref[...] = dab.astype(BF16)
        dsc_ref[0:1, :] += dsc_a
        dsc_ref[1:2, :] += dsc_d

    t8 = T // 8
    nb8 = LP // 8
    prev8 = lambda w: pl.BlockSpec((8, w), lambda i: (jnp.maximum(i * t8 - 1, 0), 0))
    next8 = lambda w: pl.BlockSpec((8, w), lambda i: (jnp.minimum((i + 1) * t8, nb8 - 1), 0))
    row = lambda w: pl.BlockSpec((T, w), lambda i: (i, 0))
    small = lambda r, c: pl.BlockSpec((r, c), lambda i: (0, 0))
    return _call_carrying(
        carry, body, nt, name="gdn_pre_bwd",
        in_specs=[row(C), prev8(C), next8(C), pl.BlockSpec((T, C_AB), lambda i: (i, OFF_AB // C_AB)),
                  small(GDN_CONV, C), small(1, H), small(1, H),
                  row(512), next8(512), row(512), next8(512), row(512), next8(512), row(512), row(512)],
        out_specs=[row(C), row(C_AB), small(GDN_CONV, C), small(2, 128)],
        out_shape=[jax.ShapeDtypeStruct((LP, C), BF16), jax.ShapeDtypeStruct((LP, C_AB), BF16),
                   jax.ShapeDtypeStruct((GDN_CONV, C), F32), jax.ShapeDtypeStruct((2, 128), F32)],
        operands=(proj, proj, proj, proj, conv_w, a_log, dt_bias, dq, dq, dk, dk, dv, dv, dbeta, dg))


def _tri_masks():
    r = _iota2((GDN_CHUNK, GDN_CHUNK), 0)
    c = _iota2((GDN_CHUNK, GDN_CHUNK), 1)
    return r >= c, r > c


def _gdn_chunk_common(q, k, v, beta, gb):
    incl, strict = _tri_masks()
    l_incl = incl.astype(BF16)
    gd = _dot_exact_l(l_incl, jnp.where(strict, gb[:, :GDN_CHUNK], 0.0))
    gc = _dot_exact_l(l_incl, gb)
    decay = jnp.where(incl, jnp.exp(jnp.where(incl, gd, 0.0)), 0.0)
    exp_g = jnp.exp(gc)
    g_last = gc[GDN_CHUNK - 1:GDN_CHUNK, :]
    kd_fac = jnp.exp(g_last - gc)
    gl = jnp.exp(g_last)
    kb = k * beta
    kk = _dot1(kb, k, _dot_nt)
    return dict(incl=incl, strict=strict, decay=decay, exp_g=exp_g, kd_fac=kd_fac, gl=gl, kb=kb, kk=kk,
                vb=v * beta, kbg=kb * exp_g)


def _interleave(gens):
    gens = list(gens)
    while gens:
        alive = []
        for g in gens:
            try:
                next(g)
                alive.append(g)
            except StopIteration:
                pass
        gens = alive


def _call_carrying(ex, body, nsteps, *, name, in_specs, out_specs, out_shape, operands, scratch_shapes=()):
    n_in, n_out, n_scr = len(in_specs), len(out_specs), len(scratch_shapes)
    n = ex.n if ex is not None else 0

    def full(*refs):
        o0 = n_in + n
        s0 = o0 + n_out + n
        ex_refs = (refs[n_in:o0], refs[o0 + n_out:s0], refs[s0 + n_scr:])
        step = pl.program_id(0)
        _carry_begin(ex, ex_refs, step, nsteps)
        body(*refs[:n_in], *refs[o0:o0 + n_out], *refs[s0:s0 + n_scr])
        _carry_end(ex, ex_refs, step, nsteps)

    res = pl.pallas_call(
        full, name=name, grid=(nsteps,),
        in_specs=list(in_specs) + [ANY_SPEC] * n, out_specs=list(out_specs) + [ANY_SPEC] * n,
        out_shape=list(out_shape) + (ex.out_shapes if ex is not None else []),
        scratch_shapes=list(scratch_shapes) + (ex.scratch if ex is not None else []),
        compiler_params=pltpu.CompilerParams(dimension_semantics=("arbitrary",), vmem_limit_bytes=VMEM_LIMIT,
                                             has_side_effects=ex is not None),
    )(*operands, *(ex.arrs if ex is not None else []))
    return list(res[:n_out]), list(res[n_out:])


def _gdn_chunk_fwd(qn, kn, v, beta_b, g_b, carry=None):
    LP = qn.shape[0]
    R = GDN_ROWS
    H = GDN_HEADS
    CH = GDN_CHUNK

    def body(q_ref, k_ref, v_ref, b_ref, g_ref, u_ref, w_ref, qd_ref, kd_ref, qk_ref, t_ref, gl_ref):
        def item(cc, h):
            rs = slice(cc * CH, (cc + 1) * CH)
            sl = slice(h * GDN_D, (h + 1) * GDN_D)
            s64 = slice(h * CH, (h + 1) * CH)
            q, k = q_ref[rs, sl], k_ref[rs, sl]
            m = _gdn_chunk_common(q, k, v_ref[rs, sl], b_ref[rs, sl], g_ref[rs, sl])
            qk_raw = _dot1(q, k, _dot_nt)
            yield
            a = jnp.where(m["strict"], m["kk"] * m["decay"], 0.0)
            eye = (_iota2((CH, CH), 0) == _iota2((CH, CH), 1)).astype(F32)
            t = eye - a
            p = _dot3(a, a)
            yield
            for _ in range(4):
                t = t + _dot3(t, p)
                p = _dot3(p, p)
                yield
            t = t + _dot3(t, p)
            yield
            u_ref[rs, sl] = _dot1(t, m["vb"])
            w_ref[rs, sl] = _dot1(t, m["kbg"])
            qk_ref[rs, s64] = qk_raw * m["decay"]
            t_ref[rs, s64] = t
            qd_ref[rs, sl] = q * m["exp_g"]
            kd_ref[rs, sl] = k * m["kd_fac"]
            gl_ref[cc * 8:(cc + 1) * 8, sl] = jnp.broadcast_to(m["gl"], (8, GDN_D))

        _interleave(item(cc, h) for cc in range(R // CH) for h in range(H))

    row = lambda w: pl.BlockSpec((R, w), lambda i: (i, 0))
    o512 = jax.ShapeDtypeStruct((LP, 512), F32)
    o256 = jax.ShapeDtypeStruct((LP, 256), F32)
    return _call_carrying(
        carry, body, LP // R, name="gdn_chunk_fwd",
        in_specs=[row(512)] * 5,
        out_specs=[row(512)] * 4 + [row(256)] * 2 + [pl.BlockSpec((R // 8, 512), lambda i: (i, 0))],
        out_shape=[o512] * 4 + [o256] * 2 + [jax.ShapeDtypeStruct((LP // 8, 512), F32)],
        operands=(qn, kn, v, beta_b, g_b))


def _gdn_chunk_bwd(qn, kn, v, beta_b, g_b, t_all, du, dw, dqd, dkd, dqk, dgl):
    LP = qn.shape[0]
    R = GDN_ROWS
    H = GDN_HEADS
    CH = GDN_CHUNK

    def body(q_ref, k_ref, v_ref, b_ref, g_ref, t_ref, du_ref, dw_ref, dqd_ref, dkd_ref, dqk_ref, dgl_ref,
             dq_ref, dk_ref, dv_ref, db_ref, dg_ref):
        ones = jnp.ones((CH, GDN_D), BF16)

        def item(cc, h):
            rs = slice(cc * CH, (cc + 1) * CH)
            sl = slice(h * GDN_D, (h + 1) * GDN_D)
            s64 = slice(h * CH, (h + 1) * CH)
            q, k, vv, beta = q_ref[rs, sl], k_ref[rs, sl], v_ref[rs, sl], b_ref[rs, sl]
            m = _gdn_chunk_common(q, k, vv, beta, g_ref[rs, sl])
            incl, strict, decay = m["incl"], m["strict"], m["decay"]
            t = t_ref[rs, s64]
            du_, dw_ = du_ref[rs, sl], dw_ref[rs, sl]
            dqd_, dkd_ = dqd_ref[rs, sl], dkd_ref[rs, sl]
            d_t = _dot1(du_, m["vb"], _dot_nt) + _dot1(dw_, m["kbg"], _dot_nt)
            dvb = _dot1(t, du_, _dot_tn)
            dkbg = _dot1(t, dw_, _dot_tn)
            qk_raw = _dot1(q, k, _dot_nt)
            yield
            x1 = _dot3(d_t, t, _dot_nt)
            dkb = dkbg * m["exp_g"]
            d_gi = _rowsum(dkbg * m["kbg"])
            yield
            d_a = jnp.where(strict, -_dot3(t, x1, _dot_tn), 0.0)
            yield
            d_kk = d_a * decay
            dqk_m = jnp.where(incl, dqk_ref[rs, s64], 0.0)
            dqk_raw = dqk_m * decay
            mm = (d_a * m["kk"] + dqk_m * qk_raw) * decay
            dkb = dkb + _dot1(d_kk, k)
            dk_ = _dot1(d_kk, m["kb"], _dot_tn) + _dot1(dqk_raw, q, _dot_tn)
            dq_ = _dot1(dqk_raw, k) + dqd_ * m["exp_g"]
            d_gi = d_gi + (_dot_exact_r(mm, ones) - _dot_exact_r(mm, ones, _dot_tn))
            yield
            d_gi = d_gi + _rowsum(dqd_ * q * m["exp_g"])
            e = _rowsum(dkd_ * k * m["kd_fac"])
            d_gi = d_gi - e
            d_glast = _colsum(jnp.broadcast_to(e, (CH, GDN_D))) + dgl_ref[cc * 8:cc * 8 + 1, sl] * m["gl"]
            dk_ = dk_ + dkd_ * m["kd_fac"] + dkb * beta
            d_gi = d_gi + jnp.where(_iota2((CH, GDN_D), 0) == CH - 1, d_glast, 0.0)
            u_incl = (_iota2((CH, CH), 1) >= _iota2((CH, CH), 0)).astype(BF16)
            dq_ref[rs, sl] = dq_
            dk_ref[rs, sl] = dk_
            dv_ref[rs, sl] = dvb * beta
            db_ref[rs, sl] = jnp.broadcast_to(_rowsum(dvb * vv) + _rowsum(dkb * k), (CH, GDN_D))
            dg_ref[rs, sl] = _dot_exact_l(u_incl, d_gi)

        _interleave(item(cc, h) for cc in range(R // CH) for h in range(H))

    row = lambda w: pl.BlockSpec((R, w), lambda i: (i, 0))
    o512 = jax.ShapeDtypeStruct((LP, 512), F32)
    gl_spec = pl.BlockSpec((R // 8, 512), lambda i: (i, 0))
    return pl.pallas_call(
        body, name="gdn_chunk_bwd", grid=(LP // R,),
        in_specs=[row(512)] * 5 + [row(256)] + [row(512)] * 4 + [row(256), gl_spec],
        out_specs=[row(512)] * 5, out_shape=[o512] * 5,
        compiler_params=_cp(("parallel",)),
    )(qn, kn, v, beta_b, g_b, t_all, du, dw, dqd, dkd, dqk, dgl)


def _gdn_scan_fwd(u, w, qd, kd, qk, gl):
    LP = u.shape[0]
    CH = GDN_CHUNK
    CPS = SCAN_CHUNKS
    N = LP // CH
    NS = N // CPS
    H = GDN_HEADS

    def body(u_ref, w_ref, qd_ref, kd_ref, qk_ref, gl_ref, o_ref, ssave_ref, s_sc):
        @pl.when(pl.program_id(0) == 0)
        def _():
            s_sc[...] = jnp.zeros_like(s_sc)

        for cc in range(CPS):
            rs = slice(cc * CH, (cc + 1) * CH)
            ssave_ref[cc * GDN_D:(cc + 1) * GDN_D, :] = s_sc[...]

            def item(h):
                sl = slice(h * GDN_D, (h + 1) * GDN_D)
                s = s_sc[:, sl]
                v_new = u_ref[rs, sl] - _dot1(w_ref[rs, sl], s)
                o_s = _dot1(qd_ref[rs, sl], s)
                yield
                o_ref[rs, sl] = o_s + _dot1(qk_ref[rs, h * CH:(h + 1) * CH], v_new)
                s_sc[:, sl] = s * gl_ref[cc * 8:cc * 8 + 1, sl] + _dot1(kd_ref[rs, sl], v_new, _dot_tn)

            _interleave(item(h) for h in range(H))

    row = lambda w_: pl.BlockSpec((CPS * CH, w_), lambda n: (n, 0))
    return pl.pallas_call(
        body, name="gdn_scan_fwd", grid=(NS,),
        in_specs=[row(512)] * 4 + [row(256), pl.BlockSpec((CPS * 8, 512), lambda n: (n, 0))],
        out_specs=[row(512), pl.BlockSpec((CPS * GDN_D, 512), lambda n: (n, 0))],
        out_shape=[jax.ShapeDtypeStruct((LP, 512), F32), jax.ShapeDtypeStruct((N * GDN_D, 512), F32)],
        scratch_shapes=[pltpu.VMEM((GDN_D, 512), F32)],
        compiler_params=_cp(("arbitrary",)),
    )(u, w, qd, kd, qk, gl)


def _gdn_scan_bwd(u, w, qd, kd, qk, gl, ssave, do, carry=None):
    LP = u.shape[0]
    CH = GDN_CHUNK
    CPS = SCAN_CHUNKS
    N = LP // CH
    NS = N // CPS
    H = GDN_HEADS

    def body(u_ref, w_ref, qd_ref, kd_ref, qk_ref, gl_ref, s_ref, do_ref,
             du_ref, dw_ref, dqd_ref, dkd_ref, dqk_ref, dgl_ref, ds_sc):
        @pl.when(pl.program_id(0) == 0)
        def _():
            ds_sc[...] = jnp.zeros_like(ds_sc)

        for cc in reversed(range(CPS)):
            rs = slice(cc * CH, (cc + 1) * CH)
            r8 = slice(cc * 8, (cc + 1) * 8)

            def item(h):
                sl = slice(h * GDN_D, (h + 1) * GDN_D)
                s64 = slice(h * CH, (h + 1) * CH)
                s = s_ref[cc * GDN_D:(cc + 1) * GDN_D, sl]
                ds = ds_sc[:, sl]
                do_ = do_ref[rs, sl]
                w_, qd_, kd_, qk_ = w_ref[rs, sl], qd_ref[rs, sl], kd_ref[rs, sl], qk_ref[rs, s64]
                v_new = u_ref[rs, sl] - _dot1(w_, s)
                d_vnew = _dot1(qk_, do_, _dot_tn) + _dot1(kd_, ds)
                dqd_ref[rs, sl] = _dot1(do_, s, _dot_nt)
                ds_new = ds * gl_ref[cc * 8:cc * 8 + 1, sl] + _dot1(qd_, do_, _dot_tn)
                dgl_ref[r8, sl] = jnp.broadcast_to(jnp.sum(_colsum(ds * s), axis=-1, keepdims=True), (8, GDN_D))
                yield
                du_ref[rs, sl] = d_vnew
                dw_ref[rs, sl] = -_dot1(d_vnew, s, _dot_nt)
                dkd_ref[rs, sl] = _dot1(v_new, ds, _dot_nt)
                dqk_ref[rs, s64] = _dot1(do_, v_new, _dot_nt)
                ds_sc[:, sl] = ds_new - _dot1(w_, d_vnew, _dot_tn)

            _interleave(item(h) for h in range(H))

    rev = lambda w_: pl.BlockSpec((CPS * CH, w_), lambda n: (NS - 1 - n, 0))
    rev8 = pl.BlockSpec((CPS * 8, 512), lambda n: (NS - 1 - n, 0))
    o512 = jax.ShapeDtypeStruct((LP, 512), F32)
    return _call_carrying(
        carry, body, NS, name="gdn_scan_bwd",
        in_specs=[rev(512)] * 4 + [rev(256), rev8, pl.BlockSpec((CPS * GDN_D, 512), lambda n: (NS - 1 - n, 0)),
                  rev(512)],
        out_specs=[rev(512)] * 4 + [rev(256), rev8],
        out_shape=[o512] * 4 + [jax.ShapeDtypeStruct((LP, 256), F32), jax.ShapeDtypeStruct((LP // 8, 512), F32)],
        scratch_shapes=[pltpu.VMEM((GDN_D, 512), F32)],
        operands=(u, w, qd, kd, qk, gl, ssave, do))


def _sb_scores(qh, kblk, mask):
    z = _dot_nt(qh, kblk)
    e = jnp.exp(-jnp.abs(z))
    sp = jnp.maximum(z, 0.0) + jnp.log(1.0 + e)
    return z, e, jnp.where(mask, -sp, 0.0), z - sp


def _sb_fwd(proj):
    LP = proj.shape[0]
    B = SB_BLOCK
    W = min(SB_SPAN, LP)
    SUB = SB_SUB
    Q = min(SB_QTILE, LP)
    nq = LP // Q
    nsub = W // SUB
    scale = SB_DH ** -0.5
    qcol, kcol, vcol = OFF_SB // B, (OFF_SB + 512) // B, (OFF_SB + 1024) // B

    def body(q_ref, k_ref, v_ref, tri_ref, o_ref, c_ref, n_ref):
        i = pl.program_id(1)
        lane = _iota2((Q, B), 1)
        head_a = lane < SB_DH
        qs = q_ref[...] * scale
        qh = [jnp.where(head_a, qs, 0.0).astype(BF16), jnp.where(head_a, 0.0, qs).astype(BF16)]
        u_strict = tri_ref[...]
        qpos = i * Q + _iota2((Q, W), 0)
        hi0 = (i + 1) * Q
        nspan = (hi0 + W - 1) // W

        def live(st):
            return (st[0] < nspan) & (st[1] > 0)

        def span(st):
            r, carry = st[0], st[2:]
            hi = hi0 - r * W
            k0 = pl.multiple_of(jnp.maximum(hi - W, 0), B)
            kblk = k_ref[pl.ds(k0, W), :].astype(BF16)
            vblk = v_ref[pl.ds(k0, W), :].astype(BF16)
            kpos = k0 + _iota2((Q, W), 1)
            mask = (kpos < qpos) & (kpos >= PAD_ROWS) & (kpos < hi)
            new = [None] * 4

            def head(h):
                o_acc, c = carry[2 * h], carry[2 * h + 1]
                z, e, l1m, lsg = _sb_scores(qh[h], kblk, mask)
                yield
                subs = [slice(b * SUB, (b + 1) * SUB) for b in range(nsub)]
                suf = [_dot(l1m[:, bs].astype(BF16), u_strict) for bs in subs]
                yield
                parts = [None] * nsub
                for b in reversed(range(nsub)):
                    parts[b] = jnp.where(mask[:, subs[b]], jnp.exp(lsg[:, subs[b]] + suf[b] + c), 0.0)
                    c = c + _rowsum(l1m[:, subs[b]])
                att = jnp.concatenate(parts, axis=1).astype(BF16)
                new[2 * h], new[2 * h + 1] = o_acc + _dot(att, vblk), c

            _interleave(head(h) for h in range(2))
            more = (jnp.maximum(jnp.max(new[1]), jnp.max(new[3])) > SB_DEAD).astype(jnp.int32)
            return (r + 1, more, *new)

        zero_o = jnp.zeros((Q, B), F32)
        zero_c = jnp.zeros((Q, 1), F32)
        nrun, _, o_a, c_a, o_b, c_b = lax.while_loop(
            live, span, (jnp.int32(0), jnp.int32(1), zero_o, zero_c, zero_o, zero_c))
        o_ref[...] = jnp.where(head_a, o_a, o_b)
        c_ref[...] = jnp.where(head_a, c_a, c_b)
        n_ref[pl.program_id(0), i] = nrun

    blk = pl.BlockSpec((Q, B), lambda p, i: (i, p))
    out = jax.ShapeDtypeStruct((LP, 512), F32)
    return pl.pallas_call(
        body, name="sb_fwd", grid=(SB_HEADS // 2, nq),
        in_specs=[pl.BlockSpec((Q, B), lambda p, i: (i, qcol + p)),
                  pl.BlockSpec((LP, B), lambda p, i: (0, kcol + p)),
                  pl.BlockSpec((LP, B), lambda p, i: (0, vcol + p)),
                  pl.BlockSpec((SUB, SUB), lambda p, i: (0, 0))],
        out_specs=[blk, blk, pl.BlockSpec(memory_space=pltpu.SMEM)],
        out_shape=[out, out, jax.ShapeDtypeStruct((SB_HEADS // 2, nq), jnp.int32)],
        compiler_params=_cp(("arbitrary", "arbitrary")),
    )(proj, proj, proj, jnp.tril(jnp.ones((SUB, SUB), BF16), -1))


def _sb_bwd(proj, ctot, nrun_all, do):
    LP = proj.shape[0]
    B = SB_BLOCK
    W = min(SB_SPAN, LP)
    SUB = SB_SUB
    Q = min(SB_QTILE, LP)
    nq = LP // Q
    nsub = W // SUB
    scale = SB_DH ** -0.5
    qcol, kcol, vcol = OFF_SB // B, (OFF_SB + 512) // B, (OFF_SB + 1024) // B

    def body(n_ref, q_ref, k_ref, v_ref, c_ref, do_ref, tril_ref, triu_ref, dq_ref, dk_ref, dv_ref):
        i = pl.program_id(1)

        @pl.when(i == 0)
        def _():
            dk_ref[...] = jnp.zeros_like(dk_ref)
            dv_ref[...] = jnp.zeros_like(dv_ref)

        lane = _iota2((Q, B), 1)
        head_a = lane < SB_DH
        qs = q_ref[...] * scale
        qh = [jnp.where(head_a, qs, 0.0).astype(BF16), jnp.where(head_a, 0.0, qs).astype(BF16)]
        dof = do_ref[...]
        doh = [jnp.where(head_a, dof, 0.0).astype(BF16), jnp.where(head_a, 0.0, dof).astype(BF16)]
        cfull = c_ref[...]
        ctot_h = [cfull[:, 0:1], cfull[:, SB_DH:SB_DH + 1]]
        u_strict = tril_ref[...]
        l_strict = triu_ref[...]
        qpos = i * Q + _iota2((Q, W), 0)
        hi0 = (i + 1) * Q
        nrun = n_ref[pl.program_id(0), i]

        def span(t, carry):
            r = nrun - 1 - t
            hi = hi0 - r * W
            k0 = pl.multiple_of(jnp.maximum(hi - W, 0), B)
            kblk = k_ref[pl.ds(k0, W), :].astype(BF16)
            vblk = v_ref[pl.ds(k0, W), :].astype(BF16)
            kpos = k0 + _iota2((Q, W), 1)
            mask = (kpos < qpos) & (kpos >= PAD_ROWS) & (kpos < hi)
            new = [None] * 6
            dk_add, dv_add = [None, None], [None, None]
            subs = [slice(b * SUB, (b + 1) * SUB) for b in range(nsub)]

            def head(h):
                dq_acc, pre, ecar = carry[3 * h], carry[3 * h + 1], carry[3 * h + 2]
                z, e, l1m, lsg = _sb_scores(qh[h], kblk, mask)
                d_att = _dot_nt(doh[h], vblk)
                yield
                sig = jnp.where(z >= 0.0, 1.0, e) / (1.0 + e)
                suf = [_dot(l1m[:, bs].astype(BF16), u_strict) for bs in subs]
                yield
                att_parts, p_parts = [None] * nsub, [None] * nsub
                for b, bs in enumerate(subs):
                    pre = pre + _rowsum(l1m[:, bs])
                    att_parts[b] = jnp.where(mask[:, bs], jnp.exp(lsg[:, bs] + suf[b] + (ctot_h[h] - pre)), 0.0)
                    p_parts[b] = att_parts[b] * d_att[:, bs]
                pcum = [_dot(p.astype(BF16), l_strict) for p in p_parts]
                yield
                dz_parts = [None] * nsub
                for b, bs in enumerate(subs):
                    sg = sig[:, bs]
                    dz_parts[b] = jnp.where(mask[:, bs], p_parts[b] * (1.0 - sg) - sg * (ecar + pcum[b]), 0.0)
                    ecar = ecar + _rowsum(p_parts[b])
                att = jnp.concatenate(att_parts, axis=1).astype(BF16)
                dz = jnp.concatenate(dz_parts, axis=1).astype(BF16)
                new[3 * h:3 * h + 3] = [dq_acc + _dot(dz, kblk), pre, ecar]
                dk_add[h] = _dot_tn(dz, qh[h])
                dv_add[h] = _dot_tn(att, doh[h])

            _interleave(head(h) for h in range(2))
            dk_ref[pl.ds(k0, W), :] += dk_add[0] + dk_add[1]
            dv_ref[pl.ds(k0, W), :] += dv_add[0] + dv_add[1]
            return tuple(new)

        zero_o = jnp.zeros((Q, B), F32)
        zero_c = jnp.zeros((Q, 1), F32)
        res = lax.fori_loop(0, nrun, span, (zero_o, zero_c, zero_c, zero_o, zero_c, zero_c))
        dq_ref[...] = (jnp.where(head_a, res[0], res[3]) * scale).astype(BF16)

    blk = pl.BlockSpec((Q, B), lambda p, i: (i, p))
    col = pl.BlockSpec((LP, B), lambda p, i: (0, p))
    tri = pl.BlockSpec((SUB, SUB), lambda p, i: (0, 0))
    out = jax.ShapeDtypeStruct((LP, 512), F32)
    return pl.pallas_call(
        body, name="sb_bwd", grid=(SB_HEADS // 2, nq),
        in_specs=[pl.BlockSpec(memory_space=pltpu.SMEM),
                  pl.BlockSpec((Q, B), lambda p, i: (i, qcol + p)),
                  pl.BlockSpec((LP, B), lambda p, i: (0, kcol + p)),
                  pl.BlockSpec((LP, B), lambda p, i: (0, vcol + p)),
                  blk, blk, tri, tri],
        out_specs=[blk, col, col], out_shape=[jax.ShapeDtypeStruct((LP, 512), BF16), out, out],
        compiler_params=_cp(("arbitrary", "arbitrary")),
    )(nrun_all, proj, proj, proj, ctot, do, jnp.tril(jnp.ones((SUB, SUB), BF16), -1),
      jnp.triu(jnp.ones((SUB, SUB), BF16), 1))


def _sb_group_mean():
    r = jnp.right_shift(_iota2((512, 512), 0), 6)
    c = jnp.right_shift(_iota2((512, 512), 1), 6)
    return jnp.where(r == c, 1.0 / SB_DH, 0.0).astype(BF16)


def _attn_norm_fwd(og, proj, osb, gnw, snw):
    LP = og.shape[0]
    T = _tile(LP, 256)

    def body(og_ref, z_ref, os_ref, gnw_ref, snw_ref, y_ref):
        valid = (pl.program_id(0) * T + _iota2((T, 1), 0)) >= PAD_ROWS
        z = z_ref[...]
        zg = z * _sigmoid(z)
        for h in range(GDN_HEADS):
            sl = slice(h * GDN_D, (h + 1) * GDN_D)
            o = og_ref[:, sl]
            y = o * _rms(o) * gnw_ref[...] * zg[:, sl]
            y_ref[:, sl] = jnp.where(valid, y, 0.0).astype(BF16)
        o = os_ref[...]
        msq = _dot_exact_r(o * o, _sb_group_mean())
        y = o * lax.rsqrt(msq + NORM_EPS) * snw_ref[...]
        y_ref[:, 512:] = jnp.where(valid, y, 0.0).astype(BF16)

    row = pl.BlockSpec((T, 512), lambda i: (i, 0))
    return pl.pallas_call(
        body, name="attn_norm_fwd", grid=(LP // T,),
        in_specs=[row, pl.BlockSpec((T, 512), lambda i: (i, OFF_Z // 512)), row,
                  pl.BlockSpec((1, GDN_D), lambda i: (0, 0)), pl.BlockSpec((1, 512), lambda i: (0, 0))],
        out_specs=pl.BlockSpec((T, 1024), lambda i: (i, 0)),
        out_shape=jax.ShapeDtypeStruct((LP, 1024), BF16),
        compiler_params=_cp(("parallel",)),
    )(og, proj, osb, gnw, snw)


def _attn_norm_bwd(og, proj, osb, gnw, snw, dy, carry=None):
    LP = og.shape[0]
    T = _tile(LP, 256)

    def body(og_ref, z_ref, os_ref, gnw_ref, snw_ref, dy_ref, dog_ref, dz_ref, dos_ref, dgw_ref, dsw_ref):
        @pl.when(pl.program_id(0) == 0)
        def _():
            dgw_ref[...] = jnp.zeros_like(dgw_ref)
            dsw_ref[...] = jnp.zeros_like(dsw_ref)
        valid = (pl.program_id(0) * T + _iota2((T, 1), 0)) >= PAD_ROWS
        dy = jnp.where(valid, dy_ref[...], 0.0)
        z = z_ref[...]
        sg = _sigmoid(z)
        zg = z * sg
        dgw = jnp.zeros((1, GDN_D), F32)
        for h in range(GDN_HEADS):
            sl = slice(h * GDN_D, (h + 1) * GDN_D)
            o = og_ref[:, sl]
            dyh = dy[:, sl]
            dx, dwn = _rms_bwd(o, gnw_ref[...], dyh * zg[:, sl])
            dog_ref[:, sl] = dx
            dgw = dgw + _colsum(dwn)
            yn = o * _rms(o) * gnw_ref[...]
            dz_ref[:, sl] = (dyh * yn * (sg[:, sl] * (1.0 + z[:, sl] * (1.0 - sg[:, sl])))).astype(BF16)
        dgw_ref[...] += dgw
        o = os_ref[...]
        gm = _sb_group_mean()
        r = lax.rsqrt(_dot_exact_r(o * o, gm) + NORM_EPS)
        n = o * r
        dys = dy[:, 512:]
        dyw = dys * snw_ref[...]
        dos_ref[...] = r * (dyw - n * _dot_exact_r(dyw * n, gm))
        dsw_ref[...] += _colsum(dys * n)

    row = pl.BlockSpec((T, 512), lambda i: (i, 0))
    gw = pl.BlockSpec((1, GDN_D), lambda i: (0, 0))
    sw = pl.BlockSpec((1, 512), lambda i: (0, 0))
    o512 = jax.ShapeDtypeStruct((LP, 512), F32)
    return _call_carrying(
        carry, body, LP // T, name="attn_norm_bwd",
        in_specs=[row, pl.BlockSpec((T, 512), lambda i: (i, OFF_Z // 512)), row, gw, sw,
                  pl.BlockSpec((T, 1024), lambda i: (i, 0))],
        out_specs=[row, row, row, gw, sw],
        out_shape=[o512, jax.ShapeDtypeStruct((LP, 512), BF16), o512, jax.ShapeDtypeStruct((1, GDN_D), F32),
                   jax.ShapeDtypeStruct((1, 512), F32)],
        operands=(og, proj, osb, gnw, snw, dy))


def _resid_fwd(h0, mix, w_post, w_pre):
    LP, D = h0.shape
    T = _tile(LP, 512)

    def body(h0_ref, mix_ref, wp_ref, wf_ref, h1_ref, n2_ref):
        mix = mix_ref[...]
        h1 = h0_ref[...] + mix * _rms(mix) * wp_ref[...]
        h1_ref[...] = h1
        n2_ref[...] = (h1 * _rms(h1) * wf_ref[...]).astype(BF16)

    row = pl.BlockSpec((T, D), lambda i: (i, 0))
    vec = pl.BlockSpec((1, D), lambda i: (0, 0))
    return pl.pallas_call(
        body, name="resid_fwd", grid=(LP // T,),
        in_specs=[row, row, vec, vec], out_specs=[row, row],
        out_shape=[jax.ShapeDtypeStruct((LP, D), F32), jax.ShapeDtypeStruct((LP, D), BF16)],
        compiler_params=_cp(("parallel",)),
    )(h0, mix, w_post, w_pre)


def _resid_bwd(h1, mix, w_post, w_pre, dout, dn2):
    LP, D = h1.shape
    T = _tile(LP, 512)

    def body(h1_ref, mix_ref, wp_ref, wf_ref, dout_ref, dn2_ref, dh1_ref, dmix_ref, dwf_ref, dwp_ref):
        @pl.when(pl.program_id(0) == 0)
        def _():
            dwf_ref[...] = jnp.zeros_like(dwf_ref)
            dwp_ref[...] = jnp.zeros_like(dwp_ref)
        dx, dwn = _rms_bwd(h1_ref[...], wf_ref[...], dn2_ref[...])
        dh1 = dout_ref[...] + dx
        dh1_ref[...] = dh1
        dwf_ref[...] += _colsum(dwn)
        dmix, dwn2 = _rms_bwd(mix_ref[...], wp_ref[...], dh1)
        dmix_ref[...] = dmix.astype(BF16)
        dwp_ref[...] += _colsum(dwn2)

    row = pl.BlockSpec((T, D), lambda i: (i, 0))
    vec = pl.BlockSpec((1, D), lambda i: (0, 0))
    v = jax.ShapeDtypeStruct((1, D), F32)
    return pl.pallas_call(
        body, name="resid_bwd", grid=(LP // T,),
        in_specs=[row, row, vec, vec, row, row], out_specs=[row, row, vec, vec],
        out_shape=[jax.ShapeDtypeStruct((LP, D), F32), jax.ShapeDtypeStruct((LP, D), BF16), v, v],
        compiler_params=_cp(("arbitrary",)),
    )(h1, mix, w_post, w_pre, dout, dn2)


GELU_C = 0.7978845608028654
GELU_A = 0.044715


def _gelu_parts(x):
    t = jnp.tanh(GELU_C * (x + GELU_A * x * x * x))
    return 0.5 * x * (1.0 + t), t


def _convglu_fwd(up, conv_w, conv_b):
    LP, C = up.shape
    T = _tile(LP, 128)

    def body(x_ref, halo_ref, cw_ref, cb_ref, act_ref, y_ref):
        i = pl.program_id(0)

        def conv(cols):
            ext = jnp.concatenate([jnp.where(i > 0, halo_ref[:, cols], 0.0), x_ref[:, cols]], axis=0)
            w = cw_ref[:, cols]
            y = (w[2:3] * ext[8:] + w[1:2] * pltpu.roll(ext, 1, 0)[8:] + w[0:1] * pltpu.roll(ext, 2, 0)[8:]
                 + cb_ref[:, cols])
            y_ref[:, cols] = y.astype(BF16)
            return y

        for s in range(D_FF // LANE):
            gs = slice(s * LANE, (s + 1) * LANE)
            g, _ = _gelu_parts(conv(gs))
            act_ref[:, gs] = (g * conv(slice(D_FF + s * LANE, D_FF + (s + 1) * LANE))).astype(BF16)

    t8 = T // 8
    return pl.pallas_call(
        body, name="convglu_fwd", grid=(LP // T,),
        in_specs=[pl.BlockSpec((T, C), lambda i: (i, 0)),
                  pl.BlockSpec((8, C), lambda i: (jnp.maximum(i * t8 - 1, 0), 0)),
                  pl.BlockSpec((FFN_CONV, C), lambda i: (0, 0)), pl.BlockSpec((1, C), lambda i: (0, 0))],
        out_specs=[pl.BlockSpec((T, D_FF), lambda i: (i, 0)), pl.BlockSpec((T, C), lambda i: (i, 0))],
        out_shape=[jax.ShapeDtypeStruct((LP, D_FF), BF16), jax.ShapeDtypeStruct((LP, C), BF16)],
        compiler_params=_cp(("parallel",)),
    )(up, up, conv_w, conv_b)


def _convglu_bwd(up, y, conv_w, dact):
    LP, C = up.shape
    T = _tile(LP, 128)
    TE = T + 8
    nt = LP // T

    def body(x_ref, y_ref, yn_ref, cw_ref, da_ref, dan_ref, dx_ref, dcw_ref, dcb_ref):
        i = pl.program_id(0)

        @pl.when(i == 0)
        def _():
            dcw_ref[...] = jnp.zeros_like(dcw_ref)
            dcb_ref[...] = jnp.zeros_like(dcb_ref)

        last = i == nt - 1

        def back(cols, dy):
            w = cw_ref[:, cols]
            later = [dy[0:T], pltpu.roll(dy, TE - 1, 0)[0:T], pltpu.roll(dy, TE - 2, 0)[0:T]]
            x_t = x_ref[:, cols]
            dcb_ref[:, cols] += _colsum(later[0])
            for j in range(FFN_CONV):
                dcw_ref[j:j + 1, cols] += _colsum(later[FFN_CONV - 1 - j] * x_t)
            dx_ref[:, cols] = (w[2:3] * later[0] + w[1:2] * later[1] + w[0:1] * later[2]).astype(BF16)

        for s in range(D_FF // LANE):
            gs = slice(s * LANE, (s + 1) * LANE)
            vs = slice(D_FF + s * LANE, D_FF + (s + 1) * LANE)
            gate = jnp.concatenate([y_ref[:, gs].astype(F32), yn_ref[0:8, gs].astype(F32)], axis=0)
            val = jnp.concatenate([y_ref[:, vs].astype(F32), yn_ref[0:8, vs].astype(F32)], axis=0)
            g, t = _gelu_parts(gate)
            dg_dx = 0.5 * (1.0 + t) + 0.5 * gate * (1.0 - t * t) * GELU_C * (1.0 + 3.0 * GELU_A * gate * gate)
            da = jnp.concatenate([da_ref[:, gs], jnp.where(last, 0.0, dan_ref[:, gs])], axis=0)
            back(gs, da * val * dg_dx)
            back(vs, da * g)

    t8 = T // 8
    nb8 = LP // 8
    next8 = lambda w: pl.BlockSpec((8, w), lambda i: (jnp.minimum((i + 1) * t8, nb8 - 1), 0))
    row = lambda w: pl.BlockSpec((T, w), lambda i: (i, 0))
    small = lambda r: pl.BlockSpec((r, C), lambda i: (0, 0))
    return pl.pallas_call(
        body, name="convglu_bwd", grid=(nt,),
        in_specs=[row(C), row(C), pl.BlockSpec((16, C), lambda i: (jnp.minimum((i + 1) * (T // 16), LP // 16 - 1), 0)),
                  small(FFN_CONV), row(D_FF), next8(D_FF)],
        out_specs=[row(C), small(FFN_CONV), small(1)],
        out_shape=[jax.ShapeDtypeStruct((LP, C), BF16), jax.ShapeDtypeStruct((FFN_CONV, C), F32),
                   jax.ShapeDtypeStruct((1, C), F32)],
        compiler_params=_cp(("arbitrary",)),
    )(up, y, y, conv_w, dact, dact)


def _final(h1, f, w_post, target, n_real):
    LP, D = h1.shape
    T = _tile(LP, 256)

    def body(h1_ref, f_ref, w_ref, t_ref, loss_ref, dout_ref, df_ref, dw_ref):
        @pl.when(pl.program_id(0) == 0)
        def _():
            loss_ref[...] = jnp.zeros_like(loss_ref)
            dw_ref[...] = jnp.zeros_like(dw_ref)
        rows = pl.program_id(0) * T + _iota2((T, 1), 0)
        real = (rows >= ROW0) & (rows < ROW0 + n_real)
        f = f_ref[...]
        out = h1_ref[...] + f * _rms(f) * w_ref[...]
        err = jnp.where(real, out - t_ref[...], 0.0)
        loss_ref[...] += 0.5 * jnp.sum(_colsum(jnp.mean(err * err, axis=-1, keepdims=True)), axis=-1, keepdims=True)
        dout = err * (1.0 / D)
        dout_ref[...] = dout
        dx, dwn = _rms_bwd(f, w_ref[...], dout)
        df_ref[...] = dx.astype(BF16)
        dw_ref[...] += _colsum(dwn)

    row = pl.BlockSpec((T, D), lambda i: (i, 0))
    vec = pl.BlockSpec((1, D), lambda i: (0, 0))
    return pl.pallas_call(
        body, name="final_loss", grid=(LP // T,),
        in_specs=[row, row, vec, row],
        out_specs=[pl.BlockSpec((1, 128), lambda i: (0, 0)), row, row, vec],
        out_shape=[jax.ShapeDtypeStruct((1, 128), F32), jax.ShapeDtypeStruct((LP, D), F32),
                   jax.ShapeDtypeStruct((LP, D), BF16), jax.ShapeDtypeStruct((1, D), F32)],
        compiler_params=_cp(("arbitrary",)),
    )(h1, f, w_post, target)


ANY_SPEC = pl.BlockSpec(memory_space=pl.ANY)
N_CHIP = 4


def _other_chips(x, y):
    return [(1 - x, y), (x, 1 - y), (1 - x, 1 - y)]


def _gather_direct(arrs, name):
    n = len(arrs)
    npeer = N_DEV - 1

    def body(*refs):
        ins, outs = refs[:n], refs[n:2 * n]
        send_sems, recv_sems, loc_sems = refs[2 * n:]
        x, y, c = lax.axis_index("x"), lax.axis_index("y"), lax.axis_index("c")
        me = 4 * x + 2 * y + c
        copies = []
        for a in range(n):
            for kk in range(1, N_DEV):
                px = 1 - x if kk & 4 else x
                py = 1 - y if kk & 2 else y
                pc = 1 - c if kk & 1 else c
                s = a * npeer + kk - 1
                cp = pltpu.make_async_remote_copy(src_ref=ins[a], dst_ref=outs[a].at[me], send_sem=send_sems.at[s],
                                                  recv_sem=recv_sems.at[s], device_id=(px, py, pc), device_id_type=MESH)
                cp.start()
                copies.append(cp)
            own = pltpu.make_async_copy(ins[a], outs[a].at[me], loc_sems.at[a])
            own.start()
            copies.append(own)
        for cp in copies:
            cp.wait()

    shapes = [jax.ShapeDtypeStruct((N_DEV,) + tuple(a.shape), a.dtype) for a in arrs]
    return pl.pallas_call(
        body, name=name, in_specs=[ANY_SPEC] * n, out_specs=[ANY_SPEC] * n, out_shape=shapes,
        scratch_shapes=[pltpu.SemaphoreType.DMA((n * npeer,)), pltpu.SemaphoreType.DMA((n * npeer,)),
                        pltpu.SemaphoreType.DMA((n,))],
        compiler_params=pltpu.CompilerParams(has_side_effects=True),
    )(*arrs)


class _Exchange:
    def __init__(self, arrs, out_shapes, scratch, start, finish, mid=None):
        self.arrs, self.out_shapes, self.scratch = list(arrs), list(out_shapes), list(scratch)
        self.start, self.finish, self.mid = start, finish, mid

    @property
    def n(self):
        return len(self.arrs)


def _run_exchange(ex, name):
    n = ex.n

    def body(*refs):
        ins, outs, sems = refs[:n], refs[n:2 * n], refs[2 * n:]
        ex.start(ins, outs, sems)
        if ex.mid is not None:
            ex.mid(ins, outs, sems)
        ex.finish(ins, outs, sems)

    return pl.pallas_call(
        body, name=name, in_specs=[ANY_SPEC] * n, out_specs=[ANY_SPEC] * n, out_shape=ex.out_shapes,
        scratch_shapes=ex.scratch, compiler_params=pltpu.CompilerParams(has_side_effects=True),
    )(*ex.arrs)


def _carry_begin(ex, refs, step, nsteps):
    if ex is None:
        return

    @pl.when(step == 0)
    def _():
        ex.start(*refs)

    if ex.mid is not None:
        @pl.when(step == min(nsteps - 1, (3 * nsteps) // 5))
        def _():
            ex.mid(*refs)


def _carry_end(ex, refs, step, nsteps):
    if ex is None:
        return

    @pl.when(step == nsteps - 1)
    def _():
        ex.finish(*refs)


def _gather_two_level(arrs):
    n = len(arrs)
    K = 7

    def env(ins, outs, sems):
        send_sems, recv_sems, loc_sems = sems
        x, y, c = lax.axis_index("x"), lax.axis_index("y"), lax.axis_index("c")

        def cp(a, k, src, slot, to):
            return pltpu.make_async_remote_copy(src_ref=src, dst_ref=outs[a].at[slot], send_sem=send_sems.at[a * K + k],
                                                recv_sem=recv_sems.at[a * K + k], device_id=to, device_id_type=MESH)

        me = 4 * x + 2 * y + c
        owns = [pltpu.make_async_copy(ins[a], outs[a].at[me], loc_sems.at[a]) for a in range(n)]
        first = []
        for a in range(n):
            first.append(cp(a, 0, ins[a], me, (x, y, 1 - c)))
            first += [cp(a, 1 + j, ins[a], me, (px, py, c)) for j, (px, py) in enumerate(_other_chips(x, y))]
        passed = []
        for j, (px, py) in enumerate(_other_chips(x, y)):
            slot = 4 * px + 2 * py + c
            passed += [(cp(a, 1 + j, ins[a], slot, (px, py, c)), cp(a, 4 + j, outs[a].at[slot], slot, (x, y, 1 - c)))
                       for a in range(n)]
        from_sib = []
        for a in range(n):
            from_sib.append(cp(a, 0, ins[a], 4 * x + 2 * y + (1 - c), (x, y, 1 - c)))
            from_sib += [cp(a, 4 + j, ins[a], 4 * px + 2 * py + (1 - c), (x, y, 1 - c))
                         for j, (px, py) in enumerate(_other_chips(x, y))]
        return owns, first, passed, from_sib

    def start(ins, outs, sems):
        owns, first, _, _ = env(ins, outs, sems)
        for cp in owns + first:
            cp.start()

    def mid(ins, outs, sems):
        _, _, passed, _ = env(ins, outs, sems)
        for arrival, fwd in passed:
            arrival.wait_recv()
            fwd.start()

    def finish(ins, outs, sems):
        owns, first, passed, from_sib = env(ins, outs, sems)
        for cp in from_sib:
            cp.wait_recv()
        for cp in first + [fwd for _, fwd in passed]:
            cp.wait_send()
        for cp in owns:
            cp.wait()

    shapes = [jax.ShapeDtypeStruct((N_DEV,) + tuple(a.shape), a.dtype) for a in arrs]
    scratch = [pltpu.SemaphoreType.DMA((n * K,)), pltpu.SemaphoreType.DMA((n * K,)), pltpu.SemaphoreType.DMA((n,))]
    return _Exchange(arrs, shapes, scratch, start, finish, mid)


def _swap_sibling(arrs):
    n = len(arrs)

    def copies(ins, outs, sems):
        send_sems, recv_sems = sems
        x, y, c = lax.axis_index("x"), lax.axis_index("y"), lax.axis_index("c")
        return [pltpu.make_async_remote_copy(src_ref=ins[a], dst_ref=outs[a], send_sem=send_sems.at[a],
                                             recv_sem=recv_sems.at[a], device_id=(x, y, 1 - c), device_id_type=MESH)
                for a in range(n)]

    def start(ins, outs, sems):
        for cp in copies(ins, outs, sems):
            cp.start()

    def finish(ins, outs, sems):
        for cp in copies(ins, outs, sems):
            cp.wait()

    shapes = [jax.ShapeDtypeStruct(tuple(a.shape), a.dtype) for a in arrs]
    return _Exchange(arrs, shapes, [pltpu.SemaphoreType.DMA((n,)), pltpu.SemaphoreType.DMA((n,))], start, finish)


def _exchange_chips(arrs):
    n = len(arrs)
    K = N_CHIP - 1

    def copies(ins, outs, sems):
        send_sems, recv_sems, loc_sems = sems
        x, y, c = lax.axis_index("x"), lax.axis_index("y"), lax.axis_index("c")
        mine = 2 * x + y
        out = []
        for a in range(n):
            out += [pltpu.make_async_remote_copy(src_ref=ins[a].at[2 * px + py], dst_ref=outs[a].at[mine],
                                                 send_sem=send_sems.at[a * K + j], recv_sem=recv_sems.at[a * K + j],
                                                 device_id=(px, py, c), device_id_type=MESH)
                    for j, (px, py) in enumerate(_other_chips(x, y))]
            out.append(pltpu.make_async_copy(ins[a].at[mine], outs[a].at[mine], loc_sems.at[a]))
        return out

    def start(ins, outs, sems):
        for cp in copies(ins, outs, sems):
            cp.start()

    def finish(ins, outs, sems):
        for cp in copies(ins, outs, sems):
            cp.wait()

    shapes = [jax.ShapeDtypeStruct(tuple(a.shape), a.dtype) for a in arrs]
    scratch = [pltpu.SemaphoreType.DMA((n * K,)), pltpu.SemaphoreType.DMA((n * K,)), pltpu.SemaphoreType.DMA((n,))]
    return _Exchange(arrs, shapes, scratch, start, finish)


def _add_halves(mine, theirs, name):
    _, R, C = mine.shape
    cap = max(16, (ELEMWISE_VMEM // (4 * C * 10)) // 16 * 16)
    T = R if R <= cap else _tile(R, cap, 16)

    def body(a_ref, b_ref, o_ref):
        o_ref[...] = (a_ref[...] + b_ref[...].astype(F32)).astype(BF16)

    blk = pl.BlockSpec((N_CHIP, T, C), lambda i: (0, i, 0))
    return pl.pallas_call(
        body, name=name, grid=(R // T,), in_specs=[blk, blk], out_specs=blk,
        out_shape=jax.ShapeDtypeStruct(mine.shape, BF16), compiler_params=_cp(("parallel",)),
    )(mine, theirs)


def _adamw(parts, w, m, v, name):
    R, C = w.shape
    npart = parts.shape[0]
    cap = max(16, (ELEMWISE_VMEM // (4 * C * 12)) // 16 * 16)
    T = R if R <= cap else _tile(R, cap, 16)

    def body(p_ref, w_ref, m_ref, v_ref, g_ref, d_ref, nm_ref, nv_ref):
        g = p_ref[0].astype(F32)
        for k in range(1, npart):
            g = g + p_ref[k].astype(F32)
        mm = ADAM_B1 * m_ref[...] + (1.0 - ADAM_B1) * g
        vv = ADAM_B2 * v_ref[...] + (1.0 - ADAM_B2) * (g * g)
        m_hat = mm / (1.0 - ADAM_B1 ** ADAM_STEP)
        v_hat = vv / (1.0 - ADAM_B2 ** ADAM_STEP)
        g_ref[...] = g
        d_ref[...] = -ADAM_LR * (m_hat / (jnp.sqrt(v_hat) + ADAM_EPS) + ADAM_WD * w_ref[...])
        nm_ref[...] = mm
        nv_ref[...] = vv

    row = pl.BlockSpec((T, C), lambda i: (i, 0))
    out = jax.ShapeDtypeStruct((R, C), F32)
    return pl.pallas_call(
        body, name=name, grid=(R // T,),
        in_specs=[pl.BlockSpec((npart, T, C), lambda i: (0, i, 0)), row, row, row],
        out_specs=[row] * 4, out_shape=[out] * 4,
        compiler_params=_cp(("parallel",)),
    )(parts, w, m, v)


SMALL = ("attn_pre_norm", "gdn_A_log", "gdn_dt_bias", "gdn_norm_w", "sb_norm_w", "attn_post_norm",
         "ffn_pre_norm", "ffn_conv_b", "ffn_post_norm")


def _pack_small(arrs):
    rows = []
    for a in arrs:
        flat = a.reshape(-1).astype(F32)
        n = -(-flat.shape[0] // 128) * 128
        rows.append(jnp.pad(flat, (0, n - flat.shape[0])).reshape(-1, 128))
    slab = jnp.concatenate(rows, axis=0)
    pad = (-slab.shape[0]) % 8
    return jnp.pad(slab, ((0, pad), (0, 0)))


def _unpack_small(slab, shapes):
    out, r = [], 0
    for shp in shapes:
        size = 1
        for s in shp:
            size *= s
        nr = -(-size // 128)
        out.append(slab[r:r + nr].reshape(-1)[:size].reshape(shp))
        r += nr
    return out


def _to_blocks_cols(a):
    R, C = a.shape
    return a.reshape(R, N_DEV, C // N_DEV).transpose(1, 0, 2)


def _from_blocks_cols(a):
    n, R, c = a.shape
    return a.transpose(1, 0, 2).reshape(R, n * c)


def kernel(x, meta_tokens, attn_pre_norm, w_in, gdn_conv_w, gdn_A_log, gdn_dt_bias, gdn_norm_w, sb_norm_w, w_out, attn_post_norm, ffn_pre_norm, w_ffn_up, ffn_conv_w, ffn_conv_b, w_ffn_down, ffn_post_norm, loss_target, m_meta_tokens, m_attn_pre_norm, m_w_in, m_gdn_conv_w, m_gdn_A_log, m_gdn_dt_bias, m_gdn_norm_w, m_sb_norm_w, m_w_out, m_attn_post_norm, m_ffn_pre_norm, m_w_ffn_up, m_ffn_conv_w, m_ffn_conv_b, m_w_ffn_down, m_ffn_post_norm, v_meta_tokens, v_attn_pre_norm, v_w_in, v_gdn_conv_w, v_gdn_A_log, v_gdn_dt_bias, v_gdn_norm_w, v_sb_norm_w, v_w_out, v_attn_post_norm, v_ffn_pre_norm, v_w_ffn_up, v_ffn_conv_w, v_ffn_conv_b, v_w_ffn_down, v_ffn_post_norm):
    args = dict(locals())
    seq = x.shape[1]
    LP = -(-(ROW0 + seq) // LP_ALIGN) * LP_ALIGN
    tail = LP - ROW0 - seq

    meta_f = _from_blocks_cols(_run_exchange(_gather_two_level([meta_tokens]), "gather_meta")[0])

    (h0, target), got = _build_rows(x[0], meta_f, loss_target[0], LP,
                                    carry=_gather_two_level([w_in[0].astype(BF16), gdn_conv_w[0]]))
    (u,), _ = _prenorm_fwd(h0, attn_pre_norm)
    win_o = _from_blocks_cols(got[0])
    o_ab = C_QKV
    o_z = o_ab + 2 * GDN_HEADS
    w_inp = jnp.concatenate([win_o[:, :C_QKV], win_o[:, o_z:o_z + C_Z], win_o[:, o_z + C_Z:],
                             win_o[:, o_ab:o_z], jnp.zeros((D_MODEL, C_AB - 2 * GDN_HEADS), BF16)], axis=1)
    gconv_f = _from_blocks_cols(got[1])
    proj = _mm(u, w_inp, F32, "mm_in")
    (qn, kn, vg, beta_b, g_b), got = _gdn_pre_fwd(
        proj, gconv_f, gdn_A_log, gdn_dt_bias,
        carry=_gather_two_level([w_out[0].astype(BF16), w_ffn_down[0].astype(BF16)]))
    w_out_f = got[0].reshape(D_MODEL, D_MODEL)
    w_down_f = got[1].reshape(D_FF, D_MODEL)
    (cu, cw, cqd, ckd, cqk, ct, cgl), got = _gdn_chunk_fwd(
        qn, kn, vg, beta_b, g_b, carry=_gather_two_level([w_ffn_up[0].astype(BF16), ffn_conv_w[0]]))
    w_up_f = _from_blocks_cols(got[0])
    fconv_f = _from_blocks_cols(got[1])
    og, ssave = _gdn_scan_fwd(cu, cw, cqd, ckd, cqk, cgl)
    osb, ctot, sb_nrun = _sb_fwd(proj)
    snw = sb_norm_w.reshape(1, SB_HEADS * SB_DH)
    y = _attn_norm_fwd(og, proj, osb, gdn_norm_w, snw)
    mix = _mm(y, w_out_f, F32, "mm_out")
    h1, n2 = _resid_fwd(h0, mix, attn_post_norm, ffn_pre_norm)
    up = _mm(n2, w_up_f, F32, "mm_up")
    act, conv_y = _convglu_fwd(up, fconv_f, ffn_conv_b)
    f = _mm(act, w_down_f, F32, "mm_down")
    loss_part, dout, df, d_fpost = _final(h1, f, ffn_post_norm, target, seq)

    d_wdown = _mm_tn(act, df, "mm_dw_down")
    dact = _mm_nt(df, w_down_f, F32, "mm_dact")
    dup, d_fconv, d_fconvb = _convglu_bwd(up, conv_y, fconv_f, dact)
    d_wup = _mm_tn(n2, dup, "mm_dw_up")
    dn2 = _mm_nt(dup, w_up_f, F32, "mm_dn2")
    dh1, dmix, d_fpre, d_apost = _resid_bwd(h1, mix, attn_post_norm, ffn_pre_norm, dout, dn2)
    d_wout = _mm_tn(y, dmix, "mm_dw_out")
    dy = _mm_nt(dmix, w_out_f, F32, "mm_dy")
    my_c = lax.axis_index("c")

    def core_halves(blocks):
        halves = [s.reshape((N_CHIP, 2) + s.shape[1:]) for s in blocks]
        return ([lax.dynamic_index_in_dim(h, my_c, axis=1, keepdims=False) for h in halves],
                [lax.dynamic_index_in_dim(h, 1 - my_c, axis=1, keepdims=False).astype(BF16) for h in halves])

    early_names = ("w_out", "w_ffn_up", "w_ffn_down", "ffn_conv_w")
    e_mine, e_send = core_halves([d_wout.reshape(N_DEV, D_MODEL // N_DEV, D_MODEL), _to_blocks_cols(d_wup),
                                  d_wdown.reshape(N_DEV, D_FF // N_DEV, D_MODEL), _to_blocks_cols(d_fconv)])
    (dog, dz, dos, d_gnw, d_snw), e_theirs = _attn_norm_bwd(og, proj, osb, gdn_norm_w, snw, dy,
                                                            carry=_swap_sibling(e_send))
    e_sums = [_add_halves(a, b, "grads_add_" + nm) for nm, a, b in zip(early_names, e_mine, e_theirs)]
    dqs, dks, dvs = _sb_bwd(proj, ctot, sb_nrun, dos)
    (du_, dw_, dqd_, dkd_, dqk_, dgl_), _ = _gdn_scan_bwd(cu, cw, cqd, ckd, cqk, cgl, ssave, dog)
    dqn, dkn, dvg, dbeta, dg = _gdn_chunk_bwd(qn, kn, vg, beta_b, g_b, ct, du_, dw_, dqd_, dkd_, dqk_, dgl_)
    (dqkv, dab, d_gconv, d_gsc), e_recv = _gdn_pre_bwd(proj, gconv_f, gdn_A_log, gdn_dt_bias, dqn, dkn, dvg, dbeta, dg,
                                                       carry=_exchange_chips(e_sums))
    dpieces = [dqkv, dz, dqs, dks, dvs, dab]
    doffs = [0, OFF_Z, OFF_SB, OFF_SB + 512, OFF_SB + 1024, OFF_AB]
    dw_qkv, dw_ab = _mm_tn_pieces(u, [dqkv, dab], "mm_dw_in_gdn")
    dw_z, dw_qs, dw_ks, dw_vs = _mm_tn_pieces(u, [dz, dqs, dks, dvs], "mm_dw_in_rest")
    du0 = _mm_nt_pieces(dpieces, doffs, w_inp, F32, "mm_du")
    d_win = jnp.concatenate([dw_qkv, dw_ab[:, :2 * GDN_HEADS], dw_z, dw_qs, dw_ks, dw_vs], axis=1)
    late_names = ("w_in", "gdn_conv_w")
    l_mine, l_send = core_halves([_to_blocks_cols(d_win), _to_blocks_cols(d_gconv)])
    l_theirs = _run_exchange(_swap_sibling(l_send), "grads_swap_sibling")
    l_sums = [_add_halves(a, b, "grads_add_" + nm) for nm, a, b in zip(late_names, l_mine, l_theirs)]
    (dh0, d_apre), l_recv = _prenorm_bwd(h0, attn_pre_norm, du0, dh1, carry=_exchange_chips(l_sums))
    grad_x = dh0[ROW0:ROW0 + seq][None]
    d_meta = dh0[PAD_ROWS:ROW0]

    small_grads = [d_apre, d_gsc[0:1, :GDN_HEADS], d_gsc[1:2, :GDN_HEADS], d_gnw, d_snw.reshape(1, SB_HEADS, SB_DH),
                   d_apost, d_fpre, d_fconvb, d_fpost]
    loss_rows = jnp.pad(loss_part, ((0, 7), (0, 0)))
    n_param_rows = _pack_small(small_grads).shape[0]
    n_small_rows = n_param_rows + loss_rows.shape[0]
    slab_parts = _gather_direct(
        [jnp.concatenate([_pack_small(small_grads), loss_rows, d_meta.reshape(-1, LANE)], axis=0)],
        name="gather_small_grads")[0]
    me = 4 * lax.axis_index("x") + 2 * lax.axis_index("y") + my_c
    meta_parts = lax.dynamic_index_in_dim(
        slab_parts[:, n_small_rows:].reshape(N_DEV, N_META, N_DEV, LANE), me, axis=2, keepdims=False)
    slab_parts = slab_parts[:, :n_small_rows]

    res = {}
    for nm, parts in zip(early_names + late_names + ("meta_tokens",), list(e_recv) + list(l_recv) + [meta_parts]):
        wloc = args[nm]
        shp = wloc.shape
        w2 = wloc.reshape(shp[-2], shp[-1])
        outs = _adamw(parts, w2, args["m_" + nm].reshape(w2.shape), args["v_" + nm].reshape(w2.shape), "adamw_" + nm)
        res[nm] = [o.reshape(shp) for o in outs]
    small_shapes = [args[nm].shape for nm in SMALL]
    with_loss_rows = lambda slab: jnp.pad(slab, ((0, n_small_rows - n_param_rows), (0, 0)))
    outs = _adamw(slab_parts, with_loss_rows(_pack_small([args[nm] for nm in SMALL])),
                  with_loss_rows(_pack_small([args["m_" + nm] for nm in SMALL])),
                  with_loss_rows(_pack_small([args["v_" + nm] for nm in SMALL])), "adamw_small")
    loss = outs[0][n_param_rows, 0]
    for k in range(4):
        for nm, val in zip(SMALL, _unpack_small(outs[k], small_shapes)):
            res.setdefault(nm, [None] * 4)[k] = val

    order = ("meta_tokens", "attn_pre_norm", "w_in", "gdn_conv_w", "gdn_A_log", "gdn_dt_bias", "gdn_norm_w",
             "sb_norm_w", "w_out", "attn_post_norm", "ffn_pre_norm", "w_ffn_up", "ffn_conv_w", "ffn_conv_b",
             "w_ffn_down", "ffn_post_norm")
    return (loss, grad_x, *[res[nm][0] for nm in order], *[res[nm][1] for nm in order],
            *[res[nm][2] for nm in order], *[res[nm][3] for nm in order])
```

```python
import functools

import jax
import jax.numpy as jnp
from jax import lax
from jax.experimental import pallas as pl
from jax.experimental.pallas import tpu as pltpu

F32 = jnp.float32
BF16 = jnp.bfloat16

D_MODEL = 1024
N_META = 16
GDN_HEADS = 4
GDN_D = 128
GDN_CHUNK = 64
GDN_CONV = 4
GDN_ROWS = 256
SCAN_CHUNKS = 4
SB_HEADS = 8
SB_DH = 64
SB_BLOCK = 128
D_FF = 2816
FFN_CONV = 3
NORM_EPS = 1e-6
L2_EPS = 1e-6
LANE = 128
N_DEV = 8

PAD_ROWS = SB_BLOCK - N_META
ROW0 = SB_BLOCK
SB_SPAN = 512
SB_DEAD = -104.0
SB_SUB = 256
SB_QTILE = 256
LP_ALIGN = 256

C_QKV = 3 * GDN_HEADS * GDN_D
C_Z = GDN_HEADS * GDN_D
C_SB = 3 * SB_HEADS * SB_DH
C_AB = 256
OFF_Z = C_QKV
OFF_SB = OFF_Z + C_Z
OFF_AB = OFF_SB + C_SB
D_INP = OFF_AB + C_AB
D_IN = C_QKV + 2 * GDN_HEADS + C_Z + C_SB

ADAM_LR = 0.001
ADAM_B1 = 0.9
ADAM_B2 = 0.999
ADAM_EPS = 1e-08
ADAM_WD = 0.01
ADAM_STEP = 10

VMEM_LIMIT = 56 * 1024 * 1024
ELEMWISE_VMEM = 8 * 1024 * 1024
MESH = pl.DeviceIdType.MESH


def _cp(sem=None):
    kw = dict(vmem_limit_bytes=VMEM_LIMIT)
    if sem is not None:
        kw["dimension_semantics"] = sem
    return pltpu.CompilerParams(**kw)


def _tile(n, cap, unit=128):
    best = None
    t = unit
    while t <= min(n, cap):
        if n % t == 0:
            best = t
        t += unit
    assert best is not None, (n, cap, unit)
    return best


def _dot(a, b):
    return jnp.dot(a, b, preferred_element_type=F32)


def _dot_nt(a, b):
    return lax.dot_general(a, b, (((1,), (1,)), ((), ())), preferred_element_type=F32)


def _dot_tn(a, b):
    return lax.dot_general(a, b, (((0,), (0,)), ((), ())), preferred_element_type=F32)


def _split(x):
    hi = x.astype(BF16)
    lo = (x - hi.astype(F32)).astype(BF16)
    return hi, lo


def _dot1(a, b, f=_dot):
    return f(a.astype(BF16), b.astype(BF16))


def _dot3(a, b, f=_dot):
    ah, al = _split(a)
    bh, bl = _split(b)
    return f(ah, bh) + (f(ah, bl) + f(al, bh))


def _dot_exact_l(m_bf16, x, f=_dot):
    xh, xl = _split(x)
    return f(m_bf16, xh) + f(m_bf16, xl)


def _dot_exact_r(x, m_bf16, f=_dot):
    xh, xl = _split(x)
    return f(xh, m_bf16) + f(xl, m_bf16)


def _iota2(shape, dim):
    return lax.broadcasted_iota(jnp.int32, shape, dim)


def _sigmoid(x):
    return 1.0 / (1.0 + jnp.exp(-x))


def _softplus(x):
    return jnp.maximum(x, 0.0) + jnp.log(1.0 + jnp.exp(-jnp.abs(x)))


def _colsum(x):
    return jnp.sum(x, axis=0, keepdims=True)


def _rowsum(x):
    return jnp.sum(x, axis=-1, keepdims=True)


def _mm(a, b, out_dtype, name):
    M, K = a.shape
    K2, N = b.shape
    assert K == K2
    tm = _tile(M, 768)
    tn = _tile(N, max(128, (6 * 1024 * 1024) // (2 * K)))

    def body(a_ref, b_ref, o_ref):
        o_ref[...] = _dot(a_ref[...].astype(BF16), b_ref[...].astype(BF16)).astype(o_ref.dtype)

    return pl.pallas_call(
        body, name=name, grid=(N // tn, M // tm),
        in_specs=[pl.BlockSpec((tm, K), lambda j, i: (i, 0)), pl.BlockSpec((K, tn), lambda j, i: (0, j))],
        out_specs=pl.BlockSpec((tm, tn), lambda j, i: (i, j)),
        out_shape=jax.ShapeDtypeStruct((M, N), out_dtype),
        compiler_params=_cp(("parallel", "parallel")),
    )(a, b)


def _mm_nt(a, b, out_dtype, name):
    M, K = a.shape
    N, K2 = b.shape
    assert K == K2
    tm = _tile(M, 768)
    tn = _tile(N, max(128, (6 * 1024 * 1024) // (2 * K)))

    def body(a_ref, b_ref, o_ref):
        o_ref[...] = _dot_nt(a_ref[...].astype(BF16), b_ref[...].astype(BF16)).astype(o_ref.dtype)

    return pl.pallas_call(
        body, name=name, grid=(N // tn, M // tm),
        in_specs=[pl.BlockSpec((tm, K), lambda j, i: (i, 0)), pl.BlockSpec((tn, K), lambda j, i: (j, 0))],
        out_specs=pl.BlockSpec((tm, tn), lambda j, i: (i, j)),
        out_shape=jax.ShapeDtypeStruct((M, N), out_dtype),
        compiler_params=_cp(("parallel", "parallel")),
    )(a, b)


def _mm_nt_pieces(pieces, offsets, b, out_dtype, name):
    M = pieces[0].shape[0]
    N = b.shape[0]
    n = len(pieces)
    widths = [p.shape[1] for p in pieces]
    assert all(off % k == 0 for off, k in zip(offsets, widths))
    tm = _tile(M, 768)
    tn = _tile(N, 512)

    def body(*refs):
        acc = _dot_nt(refs[0][...].astype(BF16), refs[n][...].astype(BF16))
        for p in range(1, n):
            acc = acc + _dot_nt(refs[p][...].astype(BF16), refs[n + p][...].astype(BF16))
        refs[2 * n][...] = acc.astype(out_dtype)

    return pl.pallas_call(
        body, name=name, grid=(N // tn, M // tm),
        in_specs=[pl.BlockSpec((tm, k), lambda j, i: (i, 0)) for k in widths]
        + [pl.BlockSpec((tn, k), functools.partial(lambda j, i, blk: (j, blk), blk=off // k))
           for off, k in zip(offsets, widths)],
        out_specs=pl.BlockSpec((tm, tn), lambda j, i: (i, j)),
        out_shape=jax.ShapeDtypeStruct((M, N), out_dtype),
        compiler_params=_cp(("parallel", "parallel")),
    )(*pieces, *([b] * n))


def _mm_tn_pieces(a, pieces, name):
    M, K = a.shape
    n = len(pieces)
    tm = _tile(M, 768)

    def body(*refs):
        @pl.when(pl.program_id(0) == 0)
        def _():
            for p in range(n):
                refs[1 + n + p][...] = jnp.zeros_like(refs[1 + n + p])
        at = refs[0][...].astype(BF16)
        for p in range(n):
            refs[1 + n + p][...] += _dot_tn(at, refs[1 + p][...].astype(BF16))

    return pl.pallas_call(
        body, name=name, grid=(M // tm,),
        in_specs=[pl.BlockSpec((tm, K), lambda m: (m, 0))] + [pl.BlockSpec((tm, p.shape[1]), lambda m: (m, 0)) for p in pieces],
        out_specs=[pl.BlockSpec((K, p.shape[1]), lambda m: (0, 0)) for p in pieces],
        out_shape=[jax.ShapeDtypeStruct((K, p.shape[1]), F32) for p in pieces],
        compiler_params=_cp(("arbitrary",)),
    )(a, *pieces)


def _mm_tn(a, b, name):
    M, K = a.shape
    M2, N = b.shape
    assert M == M2
    tm = _tile(M, 768)
    tk = _tile(K, 1408)
    tn = _tile(N, 2816)

    def body(a_ref, b_ref, o_ref):
        @pl.when(pl.program_id(2) == 0)
        def _():
            o_ref[...] = jnp.zeros_like(o_ref)
        o_ref[...] += _dot_tn(a_ref[...].astype(BF16), b_ref[...].astype(BF16))

    return pl.pallas_call(
        body, name=name, grid=(K // tk, N // tn, M // tm),
        in_specs=[pl.BlockSpec((tm, tk), lambda i, j, m: (m, i)), pl.BlockSpec((tm, tn), lambda i, j, m: (m, j))],
        out_specs=pl.BlockSpec((tk, tn), lambda i, j, m: (i, j)),
        out_shape=jax.ShapeDtypeStruct((K, N), F32),
        compiler_params=_cp(("parallel", "parallel", "arbitrary")),
    )(a, b)


def _rms(x):
    return lax.rsqrt(jnp.mean(x * x, axis=-1, keepdims=True) + NORM_EPS)


def _rms_bwd(x, w, dy):
    r = _rms(x)
    n = x * r
    dyw = dy * w
    dx = r * (dyw - n * jnp.mean(dyw * n, axis=-1, keepdims=True))
    return dx, dy * n


def _build_rows(x, meta, target, LP, carry=None):
    seq, D = x.shape
    T = SB_BLOCK
    nx = seq // T
    assert seq % T == 0 and meta.shape[0] == N_META

    def body(x_ref, m_ref, t_ref, h_ref, tp_ref):
        i = pl.program_id(0)
        inside = (i >= 1) & (i <= nx)
        head = jnp.concatenate([jnp.zeros((PAD_ROWS, D), F32), m_ref[...]], axis=0)
        h_ref[...] = jnp.where(i == 0, head, jnp.where(inside, x_ref[...], 0.0))
        tp_ref[...] = jnp.where(inside, t_ref[...], 0.0)

    tok = pl.BlockSpec((T, D), lambda i: (jnp.clip(i - 1, 0, nx - 1), 0))
    row = pl.BlockSpec((T, D), lambda i: (i, 0))
    out = jax.ShapeDtypeStruct((LP, D), F32)
    return _call_carrying(
        carry, body, LP // T, name="build_rows",
        in_specs=[tok, pl.BlockSpec((N_META, D), lambda i: (0, 0)), tok], out_specs=[row, row], out_shape=[out, out],
        operands=(x, meta, target))


def _prenorm_fwd(h0, w, carry=None):
    LP, D = h0.shape
    T = _tile(LP, 512)

    def body(h_ref, w_ref, u_ref):
        h = h_ref[...]
        u_ref[...] = (h * _rms(h) * w_ref[...]).astype(BF16)

    return _call_carrying(
        carry, body, LP // T, name="prenorm_fwd",
        in_specs=[pl.BlockSpec((T, D), lambda i: (i, 0)), pl.BlockSpec((1, D), lambda i: (0, 0))],
        out_specs=[pl.BlockSpec((T, D), lambda i: (i, 0))],
        out_shape=[jax.ShapeDtypeStruct((LP, D), BF16)],
        operands=(h0, w))


def _prenorm_bwd(h0, w, du, dh1, carry=None):
    LP, D = h0.shape
    T = _tile(LP, 512)

    def body(h_ref, w_ref, du_ref, dh1_ref, dh0_ref, dw_ref):
        @pl.when(pl.program_id(0) == 0)
        def _():
            dw_ref[...] = jnp.zeros_like(dw_ref)
        dx, dwn = _rms_bwd(h_ref[...], w_ref[...], du_ref[...])
        dh0_ref[...] = dh1_ref[...] + dx
        dw_ref[...] += _colsum(dwn)

    row = pl.BlockSpec((T, D), lambda i: (i, 0))
    vec = pl.BlockSpec((1, D), lambda i: (0, 0))
    return _call_carrying(
        carry, body, LP // T, name="prenorm_bwd",
        in_specs=[row, vec, row, row], out_specs=[row, vec],
        out_shape=[jax.ShapeDtypeStruct((LP, D), F32), jax.ShapeDtypeStruct((1, D), F32)],
        operands=(h0, w, du, dh1))


def _gdn_gate_consts(alog_ref, dtb_ref, h):
    a_coef = -jnp.exp(alog_ref[0:1, h:h + 1])
    return a_coef, dtb_ref[0:1, h:h + 1]


def _gdn_pre_fwd(proj, conv_w, a_log, dt_bias, carry=None):
    LP = proj.shape[0]
    T = _tile(LP, 256)
    C = C_QKV
    H = GDN_HEADS

    def body(x_ref, halo_ref, ab_ref, cw_ref, alog_ref, dtb_ref, q_ref, k_ref, v_ref, beta_ref, g_ref):
        i = pl.program_id(0)

        def conv_silu(cols):
            ext = jnp.concatenate([jnp.where(i > 0, halo_ref[:, cols], 0.0), x_ref[:, cols]], axis=0)
            w = cw_ref[:, cols]
            y = w[GDN_CONV - 1:GDN_CONV] * ext[8:]
            for j in range(GDN_CONV - 1):
                y = y + w[j:j + 1] * pltpu.roll(ext, GDN_CONV - 1 - j, 0)[8:]
            return y * _sigmoid(y)

        for h in range(H):
            sl = slice(h * GDN_D, (h + 1) * GDN_D)
            cq = conv_silu(sl)
            q_ref[:, sl] = cq * lax.rsqrt(_rowsum(cq * cq) + L2_EPS) * (GDN_D ** -0.5)
            ck = conv_silu(slice(512 + h * GDN_D, 512 + (h + 1) * GDN_D))
            k_ref[:, sl] = ck * lax.rsqrt(_rowsum(ck * ck) + L2_EPS)
            v_ref[:, sl] = conv_silu(slice(1024 + h * GDN_D, 1024 + (h + 1) * GDN_D))
        ab = ab_ref[...]
        valid = (i * T + _iota2((T, 1), 0)) >= PAD_ROWS
        for h in range(H):
            sl = slice(h * GDN_D, (h + 1) * GDN_D)
            a_coef, dtb = _gdn_gate_consts(alog_ref, dtb_ref, h)
            g = jnp.where(valid, a_coef * _softplus(ab[:, h:h + 1] + dtb), 0.0)
            beta = jnp.where(valid, _sigmoid(ab[:, H + h:H + h + 1]), 0.0)
            g_ref[:, sl] = jnp.broadcast_to(g, (T, GDN_D))
            beta_ref[:, sl] = jnp.broadcast_to(beta, (T, GDN_D))

    t8 = T // 8
    row512 = pl.BlockSpec((T, 512), lambda i: (i, 0))
    small = lambda r, c: pl.BlockSpec((r, c), lambda i: (0, 0))
    out = jax.ShapeDtypeStruct((LP, 512), F32)
    return _call_carrying(
        carry, body, LP // T, name="gdn_pre_fwd",
        in_specs=[pl.BlockSpec((T, C), lambda i: (i, 0)),
                  pl.BlockSpec((8, C), lambda i: (jnp.maximum(i * t8 - 1, 0), 0)),
                  pl.BlockSpec((T, C_AB), lambda i: (i, OFF_AB // C_AB)),
                  small(GDN_CONV, C), small(1, H), small(1, H)],
        out_specs=[row512] * 5, out_shape=[out] * 5,
        operands=(proj, proj, proj, conv_w, a_log, dt_bias))


def _gdn_pre_bwd(proj, conv_w, a_log, dt_bias, dq, dk, dv, dbeta, dg, carry=None):
    LP = proj.shape[0]
    T = _tile(LP, 256)
    C = C_QKV
    H = GDN_HEADS
    TE = T + 8
    nt = LP // T

    def body(x_ref, xp_ref, xn_ref, ab_ref, cw_ref, alog_ref, dtb_ref,
             dq_ref, dqn_ref, dk_ref, dkn_ref, dv_ref, dvn_ref, dbeta_ref, dg_ref,
             dx_ref, dab_ref, dcw_ref, dsc_ref):
        i = pl.program_id(0)

        @pl.when(i == 0)
        def _():
            dcw_ref[...] = jnp.zeros_like(dcw_ref)
            dsc_ref[...] = jnp.zeros_like(dsc_ref)

        last = i == nt - 1

        def strip(cols, d_ref, dn_ref, dcols, scale):
            ext = jnp.concatenate([jnp.where(i > 0, xp_ref[:, cols], 0.0), x_ref[:, cols],
                                   jnp.where(last, 0.0, xn_ref[:, cols])], axis=0)
            sh = [ext[8:8 + TE]] + [pltpu.roll(ext, s, 0)[8:8 + TE] for s in range(1, GDN_CONV)]
            w = cw_ref[:, cols]
            y = w[GDN_CONV - 1:GDN_CONV] * sh[0]
            for j in range(GDN_CONV - 1):
                y = y + w[j:j + 1] * sh[GDN_CONV - 1 - j]
            sg = _sigmoid(y)
            d = jnp.concatenate([d_ref[:, dcols], jnp.where(last, 0.0, dn_ref[:, dcols])], axis=0)
            if scale is not None:
                c = y * sg
                r = lax.rsqrt(_rowsum(c * c) + L2_EPS)
                n = c * r
                d = scale * r * (d - n * _rowsum(d * n))
            dy = d * (sg * (1.0 + y * (1.0 - sg)))
            dy_t = dy[0:T]
            for j in range(GDN_CONV):
                dcw_ref[j:j + 1, cols] += _colsum(dy_t * sh[GDN_CONV - 1 - j][0:T])
            dx = w[GDN_CONV - 1:GDN_CONV] * dy_t
            for j in range(GDN_CONV - 1):
                dx = dx + w[j:j + 1] * pltpu.roll(dy, TE - (GDN_CONV - 1 - j), 0)[0:T]
            dx_ref[:, cols] = dx.astype(BF16)

        for h in range(H):
            sl = slice(h * GDN_D, (h + 1) * GDN_D)
            strip(sl, dq_ref, dqn_ref, sl, GDN_D ** -0.5)
            strip(slice(512 + h * GDN_D, 512 + (h + 1) * GDN_D), dk_ref, dkn_ref, sl, 1.0)
            strip(slice(1024 + h * GDN_D, 1024 + (h + 1) * GDN_D), dv_ref, dvn_ref, sl, None)
        ab = ab_ref[...]
        valid = (i * T + _iota2((T, 1), 0)) >= PAD_ROWS
        lane = _iota2((T, C_AB), 1)
        lane1 = _iota2((1, 128), 1)
        dab = jnp.zeros((T, C_AB), F32)
        dsc_a = jnp.zeros((1, 128), F32)
        dsc_d = jnp.zeros((1, 128), F32)
        for h in range(H):
            a_coef, dtb = _gdn_gate_consts(alog_ref, dtb_ref, h)
            pre = ab[:, h:h + 1] + dtb
            dgh = jnp.where(valid, dg_ref[:, h * GDN_D:h * GDN_D + 1], 0.0)
            da = dgh * a_coef * _sigmoid(pre)
            beta = _sigmoid(ab[:, H + h:H + h + 1])
            db = jnp.where(valid, dbeta_ref[:, h * GDN_D:h * GDN_D + 1], 0.0) * beta * (1.0 - beta)
            dab = dab + jnp.where(lane == h, da, 0.0) + jnp.where(lane == H + h, db, 0.0)
            dsc_a = dsc_a + jnp.where(lane1 == h, _colsum(dgh * a_coef * _softplus(pre)), 0.0)
            dsc_d = dsc_d + jnp.where(lane1 == h, _colsum(da), 0.0)
        dab_ref[...] = dab.astype(BF16)
        dsc_ref[0:1, :] += dsc_a
        dsc_ref[1:2, :] += dsc_d

    t8 = T // 8
    nb8 = LP // 8
    prev8 = lambda w: pl.BlockSpec((8, w), lambda i: (jnp.maximum(i * t8 - 1, 0), 0))
    next8 = lambda w: pl.BlockSpec((8, w), lambda i: (jnp.minimum((i + 1) * t8, nb8 - 1), 0))
    row = lambda w: pl.BlockSpec((T, w), lambda i: (i, 0))
    small = lambda r, c: pl.BlockSpec((r, c), lambda i: (0, 0))
    return _call_carrying(
        carry, body, nt, name="gdn_pre_bwd",
        in_specs=[row(C), prev8(C), next8(C), pl.BlockSpec((T, C_AB), lambda i: (i, OFF_AB // C_AB)),
                  small(GDN_CONV, C), small(1, H), small(1, H),
                  row(512), next8(512), row(512), next8(512), row(512), next8(512), row(512), row(512)],
        out_specs=[row(C), row(C_AB), small(GDN_CONV, C), small(2, 128)],
        out_shape=[jax.ShapeDtypeStruct((LP, C), BF16), jax.ShapeDtypeStruct((LP, C_AB), BF16),
                   jax.ShapeDtypeStruct((GDN_CONV, C), F32), jax.ShapeDtypeStruct((2, 128), F32)],
        operands=(proj, proj, proj, proj, conv_w, a_log, dt_bias, dq, dq, dk, dk, dv, dv, dbeta, dg))


def _tri_masks():
    r = _iota2((GDN_CHUNK, GDN_CHUNK), 0)
    c = _iota2((GDN_CHUNK, GDN_CHUNK), 1)
    return r >= c, r > c


def _gdn_chunk_common(q, k, v, beta, gb):
    incl, strict = _tri_masks()
    l_incl = incl.astype(BF16)
    gd = _dot_exact_l(l_incl, jnp.where(strict, gb[:, :GDN_CHUNK], 0.0))
    gc = _dot_exact_l(l_incl, gb)
    decay = jnp.where(incl, jnp.exp(jnp.where(incl, gd, 0.0)), 0.0)
    exp_g = jnp.exp(gc)
    g_last = gc[GDN_CHUNK - 1:GDN_CHUNK, :]
    kd_fac = jnp.exp(g_last - gc)
    gl = jnp.exp(g_last)
    kb = k * beta
    kk = _dot1(kb, k, _dot_nt)
    return dict(incl=incl, strict=strict, decay=decay, exp_g=exp_g, kd_fac=kd_fac, gl=gl, kb=kb, kk=kk,
                vb=v * beta, kbg=kb * exp_g)


def _interleave(gens):
    gens = list(gens)
    while gens:
        alive = []
        for g in gens:
            try:
                next(g)
                alive.append(g)
            except StopIteration:
                pass
        gens = alive


def _call_carrying(ex, body, nsteps, *, name, in_specs, out_specs, out_shape, operands, scratch_shapes=()):
    n_in, n_out, n_scr = len(in_specs), len(out_specs), len(scratch_shapes)
    n = ex.n if ex is not None else 0

    def full(*refs):
        o0 = n_in + n
        s0 = o0 + n_out + n
        ex_refs = (refs[n_in:o0], refs[o0 + n_out:s0], refs[s0 + n_scr:])
        step = pl.program_id(0)
        _carry_begin(ex, ex_refs, step, nsteps)
        body(*refs[:n_in], *refs[o0:o0 + n_out], *refs[s0:s0 + n_scr])
        _carry_end(ex, ex_refs, step, nsteps)

    res = pl.pallas_call(
        full, name=name, grid=(nsteps,),
        in_specs=list(in_specs) + [ANY_SPEC] * n, out_specs=list(out_specs) + [ANY_SPEC] * n,
        out_shape=list(out_shape) + (ex.out_shapes if ex is not None else []),
        scratch_shapes=list(scratch_shapes) + (ex.scratch if ex is not None else []),
        compiler_params=pltpu.CompilerParams(dimension_semantics=("arbitrary",), vmem_limit_bytes=VMEM_LIMIT,
                                             has_side_effects=ex is not None),
    )(*operands, *(ex.arrs if ex is not None else []))
    return list(res[:n_out]), list(res[n_out:])


def _gdn_chunk_fwd(qn, kn, v, beta_b, g_b, carry=None):
    LP = qn.shape[0]
    R = GDN_ROWS
    H = GDN_HEADS
    CH = GDN_CHUNK

    def body(q_ref, k_ref, v_ref, b_ref, g_ref, u_ref, w_ref, qd_ref, kd_ref, qk_ref, t_ref, gl_ref):
        def item(cc, h):
            rs = slice(cc * CH, (cc + 1) * CH)
            sl = slice(h * GDN_D, (h + 1) * GDN_D)
            s64 = slice(h * CH, (h + 1) * CH)
            q, k = q_ref[rs, sl], k_ref[rs, sl]
            m = _gdn_chunk_common(q, k, v_ref[rs, sl], b_ref[rs, sl], g_ref[rs, sl])
            qk_raw = _dot1(q, k, _dot_nt)
            yield
            a = jnp.where(m["strict"], m["kk"] * m["decay"], 0.0)
            eye = (_iota2((CH, CH), 0) == _iota2((CH, CH), 1)).astype(F32)
            t = eye - a
            p = _dot3(a, a)
            yield
            for _ in range(4):
                t = t + _dot3(t, p)
                p = _dot3(p, p)
                yield
            t = t + _dot3(t, p)
            yield
            u_ref[rs, sl] = _dot1(t, m["vb"])
            w_ref[rs, sl] = _dot1(t, m["kbg"])
            qk_ref[rs, s64] = qk_raw * m["decay"]
            t_ref[rs, s64] = t
            qd_ref[rs, sl] = q * m["exp_g"]
            kd_ref[rs, sl] = k * m["kd_fac"]
            gl_ref[cc * 8:(cc + 1) * 8, sl] = jnp.broadcast_to(m["gl"], (8, GDN_D))

        _interleave(item(cc, h) for cc in range(R // CH) for h in range(H))

    row = lambda w: pl.BlockSpec((R, w), lambda i: (i, 0))
    o512 = jax.ShapeDtypeStruct((LP, 512), F32)
    o256 = jax.ShapeDtypeStruct((LP, 256), F32)
    return _call_carrying(
        carry, body, LP // R, name="gdn_chunk_fwd",
        in_specs=[row(512)] * 5,
        out_specs=[row(512)] * 4 + [row(256)] * 2 + [pl.BlockSpec((R // 8, 512), lambda i: (i, 0))],
        out_shape=[o512] * 4 + [o256] * 2 + [jax.ShapeDtypeStruct((LP // 8, 512), F32)],
        operands=(qn, kn, v, beta_b, g_b))


def _gdn_chunk_bwd(qn, kn, v, beta_b, g_b, t_all, du, dw, dqd, dkd, dqk, dgl):
    LP = qn.shape[0]
    R = GDN_ROWS
    H = GDN_HEADS
    CH = GDN_CHUNK

    def body(q_ref, k_ref, v_ref, b_ref, g_ref, t_ref, du_ref, dw_ref, dqd_ref, dkd_ref, dqk_ref, dgl_ref,
             dq_ref, dk_ref, dv_ref, db_ref, dg_ref):
        ones = jnp.ones((CH, GDN_D), BF16)

        def item(cc, h):
            rs = slice(cc * CH, (cc + 1) * CH)
            sl = slice(h * GDN_D, (h + 1) * GDN_D)
            s64 = slice(h * CH, (h + 1) * CH)
            q, k, vv, beta = q_ref[rs, sl], k_ref[rs, sl], v_ref[rs, sl], b_ref[rs, sl]
            m = _gdn_chunk_common(q, k, vv, beta, g_ref[rs, sl])
            incl, strict, decay = m["incl"], m["strict"], m["decay"]
            t = t_ref[rs, s64]
            du_, dw_ = du_ref[rs, sl], dw_ref[rs, sl]
            dqd_, dkd_ = dqd_ref[rs, sl], dkd_ref[rs, sl]
            d_t = _dot1(du_, m["vb"], _dot_nt) + _dot1(dw_, m["kbg"], _dot_nt)
            dvb = _dot1(t, du_, _dot_tn)
            dkbg = _dot1(t, dw_, _dot_tn)
            qk_raw = _dot1(q, k, _dot_nt)
            yield
            x1 = _dot3(d_t, t, _dot_nt)
            dkb = dkbg * m["exp_g"]
            d_gi = _rowsum(dkbg * m["kbg"])
            yield
            d_a = jnp.where(strict, -_dot3(t, x1, _dot_tn), 0.0)
            yield
            d_kk = d_a * decay
            dqk_m = jnp.where(incl, dqk_ref[rs, s64], 0.0)
            dqk_raw = dqk_m * decay
            mm = (d_a * m["kk"] + dqk_m * qk_raw) * decay
            dkb = dkb + _dot1(d_kk, k)
            dk_ = _dot1(d_kk, m["kb"], _dot_tn) + _dot1(dqk_raw, q, _dot_tn)
            dq_ = _dot1(dqk_raw, k) + dqd_ * m["exp_g"]
            d_gi = d_gi + (_dot_exact_r(mm, ones) - _dot_exact_r(mm, ones, _dot_tn))
            yield
            d_gi = d_gi + _rowsum(dqd_ * q * m["exp_g"])
            e = _rowsum(dkd_ * k * m["kd_fac"])
            d_gi = d_gi - e
            d_glast = _colsum(jnp.broadcast_to(e, (CH, GDN_D))) + dgl_ref[cc * 8:cc * 8 + 1, sl] * m["gl"]
            dk_ = dk_ + dkd_ * m["kd_fac"] + dkb * beta
            d_gi = d_gi + jnp.where(_iota2((CH, GDN_D), 0) == CH - 1, d_glast, 0.0)
            u_incl = (_iota2((CH, CH), 1) >= _iota2((CH, CH), 0)).astype(BF16)
            dq_ref[rs, sl] = dq_
            dk_ref[rs, sl] = dk_
            dv_ref[rs, sl] = dvb * beta
            db_ref[rs, sl] = jnp.broadcast_to(_rowsum(dvb * vv) + _rowsum(dkb * k), (CH, GDN_D))
            dg_ref[rs, sl] = _dot_exact_l(u_incl, d_gi)

        _interleave(item(cc, h) for cc in range(R // CH) for h in range(H))

    row = lambda w: pl.BlockSpec((R, w), lambda i: (i, 0))
    o512 = jax.ShapeDtypeStruct((LP, 512), F32)
    gl_spec = pl.BlockSpec((R // 8, 512), lambda i: (i, 0))
    return pl.pallas_call(
        body, name="gdn_chunk_bwd", grid=(LP // R,),
        in_specs=[row(512)] * 5 + [row(256)] + [row(512)] * 4 + [row(256), gl_spec],
        out_specs=[row(512)] * 5, out_shape=[o512] * 5,
        compiler_params=_cp(("parallel",)),
    )(qn, kn, v, beta_b, g_b, t_all, du, dw, dqd, dkd, dqk, dgl)


def _gdn_scan_fwd(u, w, qd, kd, qk, gl):
    LP = u.shape[0]
    CH = GDN_CHUNK
    CPS = SCAN_CHUNKS
    N = LP // CH
    NS = N // CPS
    H = GDN_HEADS

    def body(u_ref, w_ref, qd_ref, kd_ref, qk_ref, gl_ref, o_ref, ssave_ref, s_sc):
        @pl.when(pl.program_id(0) == 0)
        def _():
            s_sc[...] = jnp.zeros_like(s_sc)

        for cc in range(CPS):
            rs = slice(cc * CH, (cc + 1) * CH)
            ssave_ref[cc * GDN_D:(cc + 1) * GDN_D, :] = s_sc[...]

            def item(h):
                sl = slice(h * GDN_D, (h + 1) * GDN_D)
                s = s_sc[:, sl]
                v_new = u_ref[rs, sl] - _dot1(w_ref[rs, sl], s)
                o_s = _dot1(qd_ref[rs, sl], s)
                yield
                o_ref[rs, sl] = o_s + _dot1(qk_ref[rs, h * CH:(h + 1) * CH], v_new)
                s_sc[:, sl] = s * gl_ref[cc * 8:cc * 8 + 1, sl] + _dot1(kd_ref[rs, sl], v_new, _dot_tn)

            _interleave(item(h) for h in range(H))

    row = lambda w_: pl.BlockSpec((CPS * CH, w_), lambda n: (n, 0))
    return pl.pallas_call(
        body, name="gdn_scan_fwd", grid=(NS,),
        in_specs=[row(512)] * 4 + [row(256), pl.BlockSpec((CPS * 8, 512), lambda n: (n, 0))],
        out_specs=[row(512), pl.BlockSpec((CPS * GDN_D, 512), lambda n: (n, 0))],
        out_shape=[jax.ShapeDtypeStruct((LP, 512), F32), jax.ShapeDtypeStruct((N * GDN_D, 512), F32)],
        scratch_shapes=[pltpu.VMEM((GDN_D, 512), F32)],
        compiler_params=_cp(("arbitrary",)),
    )(u, w, qd, kd, qk, gl)


def _gdn_scan_bwd(u, w, qd, kd, qk, gl, ssave, do, carry=None):
    LP = u.shape[0]
    CH = GDN_CHUNK
    CPS = SCAN_CHUNKS
    N = LP // CH
    NS = N // CPS
    H = GDN_HEADS

    def body(u_ref, w_ref, qd_ref, kd_ref, qk_ref, gl_ref, s_ref, do_ref,
             du_ref, dw_ref, dqd_ref, dkd_ref, dqk_ref, dgl_ref, ds_sc):
        @pl.when(pl.program_id(0) == 0)
        def _():
            ds_sc[...] = jnp.zeros_like(ds_sc)

        for cc in reversed(range(CPS)):
            rs = slice(cc * CH, (cc + 1) * CH)
            r8 = slice(cc * 8, (cc + 1) * 8)

            def item(h):
                sl = slice(h * GDN_D, (h + 1) * GDN_D)
                s64 = slice(h * CH, (h + 1) * CH)
                s = s_ref[cc * GDN_D:(cc + 1) * GDN_D, sl]
                ds = ds_sc[:, sl]
                do_ = do_ref[rs, sl]
                w_, qd_, kd_, qk_ = w_ref[rs, sl], qd_ref[rs, sl], kd_ref[rs, sl], qk_ref[rs, s64]
                v_new = u_ref[rs, sl] - _dot1(w_, s)
                d_vnew = _dot1(qk_, do_, _dot_tn) + _dot1(kd_, ds)
                dqd_ref[rs, sl] = _dot1(do_, s, _dot_nt)
                ds_new = ds * gl_ref[cc * 8:cc * 8 + 1, sl] + _dot1(qd_, do_, _dot_tn)
                dgl_ref[r8, sl] = jnp.broadcast_to(jnp.sum(_colsum(ds * s), axis=-1, keepdims=True), (8, GDN_D))
                yield
                du_ref[rs, sl] = d_vnew
                dw_ref[rs, sl] = -_dot1(d_vnew, s, _dot_nt)
                dkd_ref[rs, sl] = _dot1(v_new, ds, _dot_nt)
                dqk_ref[rs, s64] = _dot1(do_, v_new, _dot_nt)
                ds_sc[:, sl] = ds_new - _dot1(w_, d_vnew, _dot_tn)

            _interleave(item(h) for h in range(H))

    rev = lambda w_: pl.BlockSpec((CPS * CH, w_), lambda n: (NS - 1 - n, 0))
    rev8 = pl.BlockSpec((CPS * 8, 512), lambda n: (NS - 1 - n, 0))
    o512 = jax.ShapeDtypeStruct((LP, 512), F32)
    return _call_carrying(
        carry, body, NS, name="gdn_scan_bwd",
        in_specs=[rev(512)] * 4 + [rev(256), rev8, pl.BlockSpec((CPS * GDN_D, 512), lambda n: (NS - 1 - n, 0)),
                  rev(512)],
        out_specs=[rev(512)] * 4 + [rev(256), rev8],
        out_shape=[o512] * 4 + [jax.ShapeDtypeStruct((LP, 256), F32), jax.ShapeDtypeStruct((LP // 8, 512), F32)],
        scratch_shapes=[pltpu.VMEM((GDN_D, 512), F32)],
        operands=(u, w, qd, kd, qk, gl, ssave, do))


def _sb_scores(qh, kblk, mask):
    z = _dot_nt(qh, kblk)
    e = jnp.exp(-jnp.abs(z))
    sp = jnp.maximum(z, 0.0) + jnp.log(1.0 + e)
    return z, e, jnp.where(mask, -sp, 0.0), z - sp


def _sb_fwd(proj):
    LP = proj.shape[0]
    B = SB_BLOCK
    W = min(SB_SPAN, LP)
    SUB = SB_SUB
    Q = min(SB_QTILE, LP)
    nq = LP // Q
    nsub = W // SUB
    scale = SB_DH ** -0.5
    qcol, kcol, vcol = OFF_SB // B, (OFF_SB + 512) // B, (OFF_SB + 1024) // B

    def body(q_ref, k_ref, v_ref, tri_ref, o_ref, c_ref, n_ref):
        i = pl.program_id(1)
        lane = _iota2((Q, B), 1)
        head_a = lane < SB_DH
        qs = q_ref[...] * scale
        qh = [jnp.where(head_a, qs, 0.0).astype(BF16), jnp.where(head_a, 0.0, qs).astype(BF16)]
        u_strict = tri_ref[...]
        qpos = i * Q + _iota2((Q, W), 0)
        hi0 = (i + 1) * Q
        nspan = (hi0 + W - 1) // W

        def live(st):
            return (st[0] < nspan) & (st[1] > 0)

        def span(st):
            r, carry = st[0], st[2:]
            hi = hi0 - r * W
            k0 = pl.multiple_of(jnp.maximum(hi - W, 0), B)
            kblk = k_ref[pl.ds(k0, W), :].astype(BF16)
            vblk = v_ref[pl.ds(k0, W), :].astype(BF16)
            kpos = k0 + _iota2((Q, W), 1)
            mask = (kpos < qpos) & (kpos >= PAD_ROWS) & (kpos < hi)
            new = [None] * 4

            def head(h):
                o_acc, c = carry[2 * h], carry[2 * h + 1]
                z, e, l1m, lsg = _sb_scores(qh[h], kblk, mask)
                yield
                subs = [slice(b * SUB, (b + 1) * SUB) for b in range(nsub)]
                suf = [_dot(l1m[:, bs].astype(BF16), u_strict) for bs in subs]
                yield
                parts = [None] * nsub
                for b in reversed(range(nsub)):
                    parts[b] = jnp.where(mask[:, subs[b]], jnp.exp(lsg[:, subs[b]] + suf[b] + c), 0.0)
                    c = c + _rowsum(l1m[:, subs[b]])
                att = jnp.concatenate(parts, axis=1).astype(BF16)
                new[2 * h], new[2 * h + 1] = o_acc + _dot(att, vblk), c

            _interleave(head(h) for h in range(2))
            more = (jnp.maximum(jnp.max(new[1]), jnp.max(new[3])) > SB_DEAD).astype(jnp.int32)
            return (r + 1, more, *new)

        zero_o = jnp.zeros((Q, B), F32)
        zero_c = jnp.zeros((Q, 1), F32)
        nrun, _, o_a, c_a, o_b, c_b = lax.while_loop(
            live, span, (jnp.int32(0), jnp.int32(1), zero_o, zero_c, zero_o, zero_c))
        o_ref[...] = jnp.where(head_a, o_a, o_b)
        c_ref[...] = jnp.where(head_a, c_a, c_b)
        n_ref[pl.program_id(0), i] = nrun

    blk = pl.BlockSpec((Q, B), lambda p, i: (i, p))
    out = jax.ShapeDtypeStruct((LP, 512), F32)
    return pl.pallas_call(
        body, name="sb_fwd", grid=(SB_HEADS // 2, nq),
        in_specs=[pl.BlockSpec((Q, B), lambda p, i: (i, qcol + p)),
                  pl.BlockSpec((LP, B), lambda p, i: (0, kcol + p)),
                  pl.BlockSpec((LP, B), lambda p, i: (0, vcol + p)),
                  pl.BlockSpec((SUB, SUB), lambda p, i: (0, 0))],
        out_specs=[blk, blk, pl.BlockSpec(memory_space=pltpu.SMEM)],
        out_shape=[out, out, jax.ShapeDtypeStruct((SB_HEADS // 2, nq), jnp.int32)],
        compiler_params=_cp(("arbitrary", "arbitrary")),
    )(proj, proj, proj, jnp.tril(jnp.ones((SUB, SUB), BF16), -1))


def _sb_bwd(proj, ctot, nrun_all, do):
    LP = proj.shape[0]
    B = SB_BLOCK
    W = min(SB_SPAN, LP)
    SUB = SB_SUB
    Q = min(SB_QTILE, LP)
    nq = LP // Q
    nsub = W // SUB
    scale = SB_DH ** -0.5
    qcol, kcol, vcol = OFF_SB // B, (OFF_SB + 512) // B, (OFF_SB + 1024) // B

    def body(n_ref, q_ref, k_ref, v_ref, c_ref, do_ref, tril_ref, triu_ref, dq_ref, dk_ref, dv_ref):
        i = pl.program_id(1)

        @pl.when(i == 0)
        def _():
            dk_ref[...] = jnp.zeros_like(dk_ref)
            dv_ref[...] = jnp.zeros_like(dv_ref)

        lane = _iota2((Q, B), 1)
        head_a = lane < SB_DH
        qs = q_ref[...] * scale
        qh = [jnp.where(head_a, qs, 0.0).astype(BF16), jnp.where(head_a, 0.0, qs).astype(BF16)]
        dof = do_ref[...]
        doh = [jnp.where(head_a, dof, 0.0).astype(BF16), jnp.where(head_a, 0.0, dof).astype(BF16)]
        cfull = c_ref[...]
        ctot_h = [cfull[:, 0:1], cfull[:, SB_DH:SB_DH + 1]]
        u_strict = tril_ref[...]
        l_strict = triu_ref[...]
        qpos = i * Q + _iota2((Q, W), 0)
        hi0 = (i + 1) * Q
        nrun = n_ref[pl.program_id(0), i]

        def span(t, carry):
            r = nrun - 1 - t
            hi = hi0 - r * W
            k0 = pl.multiple_of(jnp.maximum(hi - W, 0), B)
            kblk = k_ref[pl.ds(k0, W), :].astype(BF16)
            vblk = v_ref[pl.ds(k0, W), :].astype(BF16)
            kpos = k0 + _iota2((Q, W), 1)
            mask = (kpos < qpos) & (kpos >= PAD_ROWS) & (kpos < hi)
            new = [None] * 6
            dk_add, dv_add = [None, None], [None, None]
            subs = [slice(b * SUB, (b + 1) * SUB) for b in range(nsub)]

            def head(h):
                dq_acc, pre, ecar = carry[3 * h], carry[3 * h + 1], carry[3 * h + 2]
                z, e, l1m, lsg = _sb_scores(qh[h], kblk, mask)
                d_att = _dot_nt(doh[h], vblk)
                yield
                sig = jnp.where(z >= 0.0, 1.0, e) / (1.0 + e)
                suf = [_dot(l1m[:, bs].astype(BF16), u_strict) for bs in subs]
                yield
                att_parts, p_parts = [None] * nsub, [None] * nsub
                for b, bs in enumerate(subs):
                    pre = pre + _rowsum(l1m[:, bs])
                    att_parts[b] = jnp.where(mask[:, bs], jnp.exp(lsg[:, bs] + suf[b] + (ctot_h[h] - pre)), 0.0)
                    p_parts[b] = att_parts[b] * d_att[:, bs]
                pcum = [_dot(p.astype(BF16), l_strict) for p in p_parts]
                yield
                dz_parts = [None] * nsub
                for b, bs in enumerate(subs):
                    sg = sig[:, bs]
                    dz_parts[b] = jnp.where(mask[:, bs], p_parts[b] * (1.0 - sg) - sg * (ecar + pcum[b]), 0.0)
                    ecar = ecar + _rowsum(p_parts[b])
                att = jnp.concatenate(att_parts, axis=1).astype(BF16)
                dz = jnp.concatenate(dz_parts, axis=1).astype(BF16)
                new[3 * h:3 * h + 3] = [dq_acc + _dot(dz, kblk), pre, ecar]
                dk_add[h] = _dot_tn(dz, qh[h])
                dv_add[h] = _dot_tn(att, doh[h])

            _interleave(head(h) for h in range(2))
            dk_ref[pl.ds(k0, W), :] += dk_add[0] + dk_add[1]
            dv_ref[pl.ds(k0, W), :] += dv_add[0] + dv_add[1]
            return tuple(new)

        zero_o = jnp.zeros((Q, B), F32)
        zero_c = jnp.zeros((Q, 1), F32)
        res = lax.fori_loop(0, nrun, span, (zero_o, zero_c, zero_c, zero_o, zero_c, zero_c))
        dq_ref[...] = (jnp.where(head_a, res[0], res[3]) * scale).astype(BF16)

    blk = pl.BlockSpec((Q, B), lambda p, i: (i, p))
    col = pl.BlockSpec((LP, B), lambda p, i: (0, p))
    tri = pl.BlockSpec((SUB, SUB), lambda p, i: (0, 0))
    out = jax.ShapeDtypeStruct((LP, 512), F32)
    return pl.pallas_call(
        body, name="sb_bwd", grid=(SB_HEADS // 2, nq),
        in_specs=[pl.BlockSpec(memory_space=pltpu.SMEM),
                  pl.BlockSpec((Q, B), lambda p, i: (i, qcol + p)),
                  pl.BlockSpec((LP, B), lambda p, i: (0, kcol + p)),
                  pl.BlockSpec((LP, B), lambda p, i: (0, vcol + p)),
                  blk, blk, tri, tri],
        out_specs=[blk, col, col], out_shape=[jax.ShapeDtypeStruct((LP, 512), BF16), out, out],
        compiler_params=_cp(("arbitrary", "arbitrary")),
    )(nrun_all, proj, proj, proj, ctot, do, jnp.tril(jnp.ones((SUB, SUB), BF16), -1),
      jnp.triu(jnp.ones((SUB, SUB), BF16), 1))


def _sb_group_mean():
    r = jnp.right_shift(_iota2((512, 512), 0), 6)
    c = jnp.right_shift(_iota2((512, 512), 1), 6)
    return jnp.where(r == c, 1.0 / SB_DH, 0.0).astype(BF16)


def _attn_norm_fwd(og, proj, osb, gnw, snw):
    LP = og.shape[0]
    T = _tile(LP, 256)

    def body(og_ref, z_ref, os_ref, gnw_ref, snw_ref, y_ref):
        valid = (pl.program_id(0) * T + _iota2((T, 1), 0)) >= PAD_ROWS
        z = z_ref[...]
        zg = z * _sigmoid(z)
        for h in range(GDN_HEADS):
            sl = slice(h * GDN_D, (h + 1) * GDN_D)
            o = og_ref[:, sl]
            y = o * _rms(o) * gnw_ref[...] * zg[:, sl]
            y_ref[:, sl] = jnp.where(valid, y, 0.0).astype(BF16)
        o = os_ref[...]
        msq = _dot_exact_r(o * o, _sb_group_mean())
        y = o * lax.rsqrt(msq + NORM_EPS) * snw_ref[...]
        y_ref[:, 512:] = jnp.where(valid, y, 0.0).astype(BF16)

    row = pl.BlockSpec((T, 512), lambda i: (i, 0))
    return pl.pallas_call(
        body, name="attn_norm_fwd", grid=(LP // T,),
        in_specs=[row, pl.BlockSpec((T, 512), lambda i: (i, OFF_Z // 512)), row,
                  pl.BlockSpec((1, GDN_D), lambda i: (0, 0)), pl.BlockSpec((1, 512), lambda i: (0, 0))],
        out_specs=pl.BlockSpec((T, 1024), lambda i: (i, 0)),
        out_shape=jax.ShapeDtypeStruct((LP, 1024), BF16),
        compiler_params=_cp(("parallel",)),
    )(og, proj, osb, gnw, snw)


def _attn_norm_bwd(og, proj, osb, gnw, snw, dy, carry=None):
    LP = og.shape[0]
    T = _tile(LP, 256)

    def body(og_ref, z_ref, os_ref, gnw_ref, snw_ref, dy_ref, dog_ref, dz_ref, dos_ref, dgw_ref, dsw_ref):
        @pl.when(pl.program_id(0) == 0)
        def _():
            dgw_ref[...] = jnp.zeros_like(dgw_ref)
            dsw_ref[...] = jnp.zeros_like(dsw_ref)
        valid = (pl.program_id(0) * T + _iota2((T, 1), 0)) >= PAD_ROWS
        dy = jnp.where(valid, dy_ref[...], 0.0)
        z = z_ref[...]
        sg = _sigmoid(z)
        zg = z * sg
        dgw = jnp.zeros((1, GDN_D), F32)
        for h in range(GDN_HEADS):
            sl = slice(h * GDN_D, (h + 1) * GDN_D)
            o = og_ref[:, sl]
            dyh = dy[:, sl]
            dx, dwn = _rms_bwd(o, gnw_ref[...], dyh * zg[:, sl])
            dog_ref[:, sl] = dx
            dgw = dgw + _colsum(dwn)
            yn = o * _rms(o) * gnw_ref[...]
            dz_ref[:, sl] = (dyh * yn * (sg[:, sl] * (1.0 + z[:, sl] * (1.0 - sg[:, sl])))).astype(BF16)
        dgw_ref[...] += dgw
        o = os_ref[...]
        gm = _sb_group_mean()
        r = lax.rsqrt(_dot_exact_r(o * o, gm) + NORM_EPS)
        n = o * r
        dys = dy[:, 512:]
        dyw = dys * snw_ref[...]
        dos_ref[...] = r * (dyw - n * _dot_exact_r(dyw * n, gm))
        dsw_ref[...] += _colsum(dys * n)

    row = pl.BlockSpec((T, 512), lambda i: (i, 0))
    gw = pl.BlockSpec((1, GDN_D), lambda i: (0, 0))
    sw = pl.BlockSpec((1, 512), lambda i: (0, 0))
    o512 = jax.ShapeDtypeStruct((LP, 512), F32)
    return _call_carrying(
        carry, body, LP // T, name="attn_norm_bwd",
        in_specs=[row, pl.BlockSpec((T, 512), lambda i: (i, OFF_Z // 512)), row, gw, sw,
                  pl.BlockSpec((T, 1024), lambda i: (i, 0))],
        out_specs=[row, row, row, gw, sw],
        out_shape=[o512, jax.ShapeDtypeStruct((LP, 512), BF16), o512, jax.ShapeDtypeStruct((1, GDN_D), F32),
                   jax.ShapeDtypeStruct((1, 512), F32)],
        operands=(og, proj, osb, gnw, snw, dy))


def _resid_fwd(h0, mix, w_post, w_pre):
    LP, D = h0.shape
    T = _tile(LP, 512)

    def body(h0_ref, mix_ref, wp_ref, wf_ref, h1_ref, n2_ref):
        mix = mix_ref[...]
        h1 = h0_ref[...] + mix * _rms(mix) * wp_ref[...]
        h1_ref[...] = h1
        n2_ref[...] = (h1 * _rms(h1) * wf_ref[...]).astype(BF16)

    row = pl.BlockSpec((T, D), lambda i: (i, 0))
    vec = pl.BlockSpec((1, D), lambda i: (0, 0))
    return pl.pallas_call(
        body, name="resid_fwd", grid=(LP // T,),
        in_specs=[row, row, vec, vec], out_specs=[row, row],
        out_shape=[jax.ShapeDtypeStruct((LP, D), F32), jax.ShapeDtypeStruct((LP, D), BF16)],
        compiler_params=_cp(("parallel",)),
    )(h0, mix, w_post, w_pre)


def _resid_bwd(h1, mix, w_post, w_pre, dout, dn2):
    LP, D = h1.shape
    T = _tile(LP, 512)

    def body(h1_ref, mix_ref, wp_ref, wf_ref, dout_ref, dn2_ref, dh1_ref, dmix_ref, dwf_ref, dwp_ref):
        @pl.when(pl.program_id(0) == 0)
        def _():
            dwf_ref[...] = jnp.zeros_like(dwf_ref)
            dwp_ref[...] = jnp.zeros_like(dwp_ref)
        dx, dwn = _rms_bwd(h1_ref[...], wf_ref[...], dn2_ref[...])
        dh1 = dout_ref[...] + dx
        dh1_ref[...] = dh1
        dwf_ref[...] += _colsum(dwn)
        dmix, dwn2 = _rms_bwd(mix_ref[...], wp_ref[...], dh1)
        dmix_ref[...] = dmix.astype(BF16)
        dwp_ref[...] += _colsum(dwn2)

    row = pl.BlockSpec((T, D), lambda i: (i, 0))
    vec = pl.BlockSpec((1, D), lambda i: (0, 0))
    v = jax.ShapeDtypeStruct((1, D), F32)
    return pl.pallas_call(
        body, name="resid_bwd", grid=(LP // T,),
        in_specs=[row, row, vec, vec, row, row], out_specs=[row, row, vec, vec],
        out_shape=[jax.ShapeDtypeStruct((LP, D), F32), jax.ShapeDtypeStruct((LP, D), BF16), v, v],
        compiler_params=_cp(("arbitrary",)),
    )(h1, mix, w_post, w_pre, dout, dn2)


GELU_C = 0.7978845608028654
GELU_A = 0.044715


def _gelu_parts(x):
    t = jnp.tanh(GELU_C * (x + GELU_A * x * x * x))
    return 0.5 * x * (1.0 + t), t


def _convglu_fwd(up, conv_w, conv_b):
    LP, C = up.shape
    T = _tile(LP, 128)

    def body(x_ref, halo_ref, cw_ref, cb_ref, act_ref, y_ref):
        i = pl.program_id(0)

        def conv(cols):
            ext = jnp.concatenate([jnp.where(i > 0, halo_ref[:, cols], 0.0), x_ref[:, cols]], axis=0)
            w = cw_ref[:, cols]
            y = (w[2:3] * ext[8:] + w[1:2] * pltpu.roll(ext, 1, 0)[8:] + w[0:1] * pltpu.roll(ext, 2, 0)[8:]
                 + cb_ref[:, cols])
            y_ref[:, cols] = y.astype(BF16)
            return y

        for s in range(D_FF // LANE):
            gs = slice(s * LANE, (s + 1) * LANE)
            g, _ = _gelu_parts(conv(gs))
            act_ref[:, gs] = (g * conv(slice(D_FF + s * LANE, D_FF + (s + 1) * LANE))).astype(BF16)

    t8 = T // 8
    return pl.pallas_call(
        body, name="convglu_fwd", grid=(LP // T,),
        in_specs=[pl.BlockSpec((T, C), lambda i: (i, 0)),
                  pl.BlockSpec((8, C), lambda i: (jnp.maximum(i * t8 - 1, 0), 0)),
                  pl.BlockSpec((FFN_CONV, C), lambda i: (0, 0)), pl.BlockSpec((1, C), lambda i: (0, 0))],
        out_specs=[pl.BlockSpec((T, D_FF), lambda i: (i, 0)), pl.BlockSpec((T, C), lambda i: (i, 0))],
        out_shape=[jax.ShapeDtypeStruct((LP, D_FF), BF16), jax.ShapeDtypeStruct((LP, C), BF16)],
        compiler_params=_cp(("parallel",)),
    )(up, up, conv_w, conv_b)


def _convglu_bwd(up, y, conv_w, dact):
    LP, C = up.shape
    T = _tile(LP, 128)
    TE = T + 8
    nt = LP // T

    def body(x_ref, y_ref, yn_ref, cw_ref, da_ref, dan_ref, dx_ref, dcw_ref, dcb_ref):
        i = pl.program_id(0)

        @pl.when(i == 0)
        def _():
            dcw_ref[...] = jnp.zeros_like(dcw_ref)
            dcb_ref[...] = jnp.zeros_like(dcb_ref)

        last = i == nt - 1

        def back(cols, dy):
            w = cw_ref[:, cols]
            later = [dy[0:T], pltpu.roll(dy, TE - 1, 0)[0:T], pltpu.roll(dy, TE - 2, 0)[0:T]]
            x_t = x_ref[:, cols]
            dcb_ref[:, cols] += _colsum(later[0])
            for j in range(FFN_CONV):
                dcw_ref[j:j + 1, cols] += _colsum(later[FFN_CONV - 1 - j] * x_t)
            dx_ref[:, cols] = (w[2:3] * later[0] + w[1:2] * later[1] + w[0:1] * later[2]).astype(BF16)

        for s in range(D_FF // LANE):
            gs = slice(s * LANE, (s + 1) * LANE)
            vs = slice(D_FF + s * LANE, D_FF + (s + 1) * LANE)
            gate = jnp.concatenate([y_ref[:, gs].astype(F32), yn_ref[0:8, gs].astype(F32)], axis=0)
            val = jnp.concatenate([y_ref[:, vs].astype(F32), yn_ref[0:8, vs].astype(F32)], axis=0)
            g, t = _gelu_parts(gate)
            dg_dx = 0.5 * (1.0 + t) + 0.5 * gate * (1.0 - t * t) * GELU_C * (1.0 + 3.0 * GELU_A * gate * gate)
            da = jnp.concatenate([da_ref[:, gs], jnp.where(last, 0.0, dan_ref[:, gs])], axis=0)
            back(gs, da * val * dg_dx)
            back(vs, da * g)

    t8 = T // 8
    nb8 = LP // 8
    next8 = lambda w: pl.BlockSpec((8, w), lambda i: (jnp.minimum((i + 1) * t8, nb8 - 1), 0))
    row = lambda w: pl.BlockSpec((T, w), lambda i: (i, 0))
    small = lambda r: pl.BlockSpec((r, C), lambda i: (0, 0))
    return pl.pallas_call(
        body, name="convglu_bwd", grid=(nt,),
        in_specs=[row(C), row(C), pl.BlockSpec((16, C), lambda i: (jnp.minimum((i + 1) * (T // 16), LP // 16 - 1), 0)),
                  small(FFN_CONV), row(D_FF), next8(D_FF)],
        out_specs=[row(C), small(FFN_CONV), small(1)],
        out_shape=[jax.ShapeDtypeStruct((LP, C), BF16), jax.ShapeDtypeStruct((FFN_CONV, C), F32),
                   jax.ShapeDtypeStruct((1, C), F32)],
        compiler_params=_cp(("arbitrary",)),
    )(up, y, y, conv_w, dact, dact)


def _final(h1, f, w_post, target, n_real):
    LP, D = h1.shape
    T = _tile(LP, 256)

    def body(h1_ref, f_ref, w_ref, t_ref, loss_ref, dout_ref, df_ref, dw_ref):
        @pl.when(pl.program_id(0) == 0)
        def _():
            loss_ref[...] = jnp.zeros_like(loss_ref)
            dw_ref[...] = jnp.zeros_like(dw_ref)
        rows = pl.program_id(0) * T + _iota2((T, 1), 0)
        real = (rows >= ROW0) & (rows < ROW0 + n_real)
        f = f_ref[...]
        out = h1_ref[...] + f * _rms(f) * w_ref[...]
        err = jnp.where(real, out - t_ref[...], 0.0)
        loss_ref[...] += 0.5 * jnp.sum(_colsum(jnp.mean(err * err, axis=-1, keepdims=True)), axis=-1, keepdims=True)
        dout = err * (1.0 / D)
        dout_ref[...] = dout
        dx, dwn = _rms_bwd(f, w_ref[...], dout)
        df_ref[...] = dx.astype(BF16)
        dw_ref[...] += _colsum(dwn)

    row = pl.BlockSpec((T, D), lambda i: (i, 0))
    vec = pl.BlockSpec((1, D), lambda i: (0, 0))
    return pl.pallas_call(
        body, name="final_loss", grid=(LP // T,),
        in_specs=[row, row, vec, row],
        out_specs=[pl.BlockSpec((1, 128), lambda i: (0, 0)), row, row, vec],
        out_shape=[jax.ShapeDtypeStruct((1, 128), F32), jax.ShapeDtypeStruct((LP, D), F32),
                   jax.ShapeDtypeStruct((LP, D), BF16), jax.ShapeDtypeStruct((1, D), F32)],
        compiler_params=_cp(("arbitrary",)),
    )(h1, f, w_post, target)


ANY_SPEC = pl.BlockSpec(memory_space=pl.ANY)
N_CHIP = 4


def _other_chips(x, y):
    return [(1 - x, y), (x, 1 - y), (1 - x, 1 - y)]


def _gather_direct(arrs, name):
    n = len(arrs)
    npeer = N_DEV - 1

    def body(*refs):
        ins, outs = refs[:n], refs[n:2 * n]
        send_sems, recv_sems, loc_sems = refs[2 * n:]
        x, y, c = lax.axis_index("x"), lax.axis_index("y"), lax.axis_index("c")
        me = 4 * x + 2 * y + c
        copies = []
        for a in range(n):
            for kk in range(1, N_DEV):
                px = 1 - x if kk & 4 else x
                py = 1 - y if kk & 2 else y
                pc = 1 - c if kk & 1 else c
                s = a * npeer + kk - 1
                cp = pltpu.make_async_remote_copy(src_ref=ins[a], dst_ref=outs[a].at[me], send_sem=send_sems.at[s],
                                                  recv_sem=recv_sems.at[s], device_id=(px, py, pc), device_id_type=MESH)
                cp.start()
                copies.append(cp)
            own = pltpu.make_async_copy(ins[a], outs[a].at[me], loc_sems.at[a])
            own.start()
            copies.append(own)
        for cp in copies:
            cp.wait()

    shapes = [jax.ShapeDtypeStruct((N_DEV,) + tuple(a.shape), a.dtype) for a in arrs]
    return pl.pallas_call(
        body, name=name, in_specs=[ANY_SPEC] * n, out_specs=[ANY_SPEC] * n, out_shape=shapes,
        scratch_shapes=[pltpu.SemaphoreType.DMA((n * npeer,)), pltpu.SemaphoreType.DMA((n * npeer,)),
                        pltpu.SemaphoreType.DMA((n,))],
        compiler_params=pltpu.CompilerParams(has_side_effects=True),
    )(*arrs)


class _Exchange:
    def __init__(self, arrs, out_shapes, scratch, start, finish, mid=None):
        self.arrs, self.out_shapes, self.scratch = list(arrs), list(out_shapes), list(scratch)
        self.start, self.finish, self.mid = start, finish, mid

    @property
    def n(self):
        return len(self.arrs)


def _run_exchange(ex, name):
    n = ex.n

    def body(*refs):
        ins, outs, sems = refs[:n], refs[n:2 * n], refs[2 * n:]
        ex.start(ins, outs, sems)
        if ex.mid is not None:
            ex.mid(ins, outs, sems)
        ex.finish(ins, outs, sems)

    return pl.pallas_call(
        body, name=name, in_specs=[ANY_SPEC] * n, out_specs=[ANY_SPEC] * n, out_shape=ex.out_shapes,
        scratch_shapes=ex.scratch, compiler_params=pltpu.CompilerParams(has_side_effects=True),
    )(*ex.arrs)


def _carry_begin(ex, refs, step, nsteps):
    if ex is None:
        return

    @pl.when(step == 0)
    def _():
        ex.start(*refs)

    if ex.mid is not None:
        @pl.when(step == min(nsteps - 1, (3 * nsteps) // 5))
        def _():
            ex.mid(*refs)


def _carry_end(ex, refs, step, nsteps):
    if ex is None:
        return

    @pl.when(step == nsteps - 1)
    def _():
        ex.finish(*refs)


def _gather_two_level(arrs):
    n = len(arrs)
    K = 7

    def env(ins, outs, sems):
        send_sems, recv_sems, loc_sems = sems
        x, y, c = lax.axis_index("x"), lax.axis_index("y"), lax.axis_index("c")

        def cp(a, k, src, slot, to):
            return pltpu.make_async_remote_copy(src_ref=src, dst_ref=outs[a].at[slot], send_sem=send_sems.at[a * K + k],
                                                recv_sem=recv_sems.at[a * K + k], device_id=to, device_id_type=MESH)

        me = 4 * x + 2 * y + c
        owns = [pltpu.make_async_copy(ins[a], outs[a].at[me], loc_sems.at[a]) for a in range(n)]
        first = []
        for a in range(n):
            first.append(cp(a, 0, ins[a], me, (x, y, 1 - c)))
            first += [cp(a, 1 + j, ins[a], me, (px, py, c)) for j, (px, py) in enumerate(_other_chips(x, y))]
        passed = []
        for j, (px, py) in enumerate(_other_chips(x, y)):
            slot = 4 * px + 2 * py + c
            passed += [(cp(a, 1 + j, ins[a], slot, (px, py, c)), cp(a, 4 + j, outs[a].at[slot], slot, (x, y, 1 - c)))
                       for a in range(n)]
        from_sib = []
        for a in range(n):
            from_sib.append(cp(a, 0, ins[a], 4 * x + 2 * y + (1 - c), (x, y, 1 - c)))
            from_sib += [cp(a, 4 + j, ins[a], 4 * px + 2 * py + (1 - c), (x, y, 1 - c))
                         for j, (px, py) in enumerate(_other_chips(x, y))]
        return owns, first, passed, from_sib

    def start(ins, outs, sems):
        owns, first, _, _ = env(ins, outs, sems)
        for cp in owns + first:
            cp.start()

    def mid(ins, outs, sems):
        _, _, passed, _ = env(ins, outs, sems)
        for arrival, fwd in passed:
            arrival.wait_recv()
            fwd.start()

    def finish(ins, outs, sems):
        owns, first, passed, from_sib = env(ins, outs, sems)
        for cp in from_sib:
            cp.wait_recv()
        for cp in first + [fwd for _, fwd in passed]:
            cp.wait_send()
        for cp in owns:
            cp.wait()

    shapes = [jax.ShapeDtypeStruct((N_DEV,) + tuple(a.shape), a.dtype) for a in arrs]
    scratch = [pltpu.SemaphoreType.DMA((n * K,)), pltpu.SemaphoreType.DMA((n * K,)), pltpu.SemaphoreType.DMA((n,))]
    return _Exchange(arrs, shapes, scratch, start, finish, mid)


def _swap_sibling(arrs):
    n = len(arrs)

    def copies(ins, outs, sems):
        send_sems, recv_sems = sems
        x, y, c = lax.axis_index("x"), lax.axis_index("y"), lax.axis_index("c")
        return [pltpu.make_async_remote_copy(src_ref=ins[a], dst_ref=outs[a], send_sem=send_sems.at[a],
                                             recv_sem=recv_sems.at[a], device_id=(x, y, 1 - c), device_id_type=MESH)
                for a in range(n)]

    def start(ins, outs, sems):
        for cp in copies(ins, outs, sems):
            cp.start()

    def finish(ins, outs, sems):
        for cp in copies(ins, outs, sems):
            cp.wait()

    shapes = [jax.ShapeDtypeStruct(tuple(a.shape), a.dtype) for a in arrs]
    return _Exchange(arrs, shapes, [pltpu.SemaphoreType.DMA((n,)), pltpu.SemaphoreType.DMA((n,))], start, finish)


def _exchange_chips(arrs):
    n = len(arrs)
    K = N_CHIP - 1

    def copies(ins, outs, sems):
        send_sems, recv_sems, loc_sems = sems
        x, y, c = lax.axis_index("x"), lax.axis_index("y"), lax.axis_index("c")
        mine = 2 * x + y
        out = []
        for a in range(n):
            out += [pltpu.make_async_remote_copy(src_ref=ins[a].at[2 * px + py], dst_ref=outs[a].at[mine],
                                                 send_sem=send_sems.at[a * K + j], recv_sem=recv_sems.at[a * K + j],
                                                 device_id=(px, py, c), device_id_type=MESH)
                    for j, (px, py) in enumerate(_other_chips(x, y))]
            out.append(pltpu.make_async_copy(ins[a].at[mine], outs[a].at[mine], loc_sems.at[a]))
        return out

    def start(ins, outs, sems):
        for cp in copies(ins, outs, sems):
            cp.start()

    def finish(ins, outs, sems):
        for cp in copies(ins, outs, sems):
            cp.wait()

    shapes = [jax.ShapeDtypeStruct(tuple(a.shape), a.dtype) for a in arrs]
    scratch = [pltpu.SemaphoreType.DMA((n * K,)), pltpu.SemaphoreType.DMA((n * K,)), pltpu.SemaphoreType.DMA((n,))]
    return _Exchange(arrs, shapes, scratch, start, finish)


def _add_halves(mine, theirs, name):
    _, R, C = mine.shape
    cap = max(16, (ELEMWISE_VMEM // (4 * C * 10)) // 16 * 16)
    T = R if R <= cap else _tile(R, cap, 16)

    def body(a_ref, b_ref, o_ref):
        o_ref[...] = (a_ref[...] + b_ref[...].astype(F32)).astype(BF16)

    blk = pl.BlockSpec((N_CHIP, T, C), lambda i: (0, i, 0))
    return pl.pallas_call(
        body, name=name, grid=(R // T,), in_specs=[blk, blk], out_specs=blk,
        out_shape=jax.ShapeDtypeStruct(mine.shape, BF16), compiler_params=_cp(("parallel",)),
    )(mine, theirs)


def _adamw(parts, w, m, v, name):
    R, C = w.shape
    npart = parts.shape[0]
    cap = max(16, (ELEMWISE_VMEM // (4 * C * 12)) // 16 * 16)
    T = R if R <= cap else _tile(R, cap, 16)

    def body(p_ref, w_ref, m_ref, v_ref, g_ref, d_ref, nm_ref, nv_ref):
        g = p_ref[0].astype(F32)
        for k in range(1, npart):
            g = g + p_ref[k].astype(F32)
        mm = ADAM_B1 * m_ref[...] + (1.0 - ADAM_B1) * g
        vv = ADAM_B2 * v_ref[...] + (1.0 - ADAM_B2) * (g * g)
        m_hat = mm / (1.0 - ADAM_B1 ** ADAM_STEP)
        v_hat = vv / (1.0 - ADAM_B2 ** ADAM_STEP)
        g_ref[...] = g
        d_ref[...] = -ADAM_LR * (m_hat / (jnp.sqrt(v_hat) + ADAM_EPS) + ADAM_WD * w_ref[...])
        nm_ref[...] = mm
        nv_ref[...] = vv

    row = pl.BlockSpec((T, C), lambda i: (i, 0))
    out = jax.ShapeDtypeStruct((R, C), F32)
    return pl.pallas_call(
        body, name=name, grid=(R // T,),
        in_specs=[pl.BlockSpec((npart, T, C), lambda i: (0, i, 0)), row, row, row],
        out_specs=[row] * 4, out_shape=[out] * 4,
        compiler_params=_cp(("parallel",)),
    )(parts, w, m, v)


SMALL = ("attn_pre_norm", "gdn_A_log", "gdn_dt_bias", "gdn_norm_w", "sb_norm_w", "attn_post_norm",
         "ffn_pre_norm", "ffn_conv_b", "ffn_post_norm")


def _pack_small(arrs):
    rows = []
    for a in arrs:
        flat = a.reshape(-1).astype(F32)
        n = -(-flat.shape[0] // 128) * 128
        rows.append(jnp.pad(flat, (0, n - flat.shape[0])).reshape(-1, 128))
    slab = jnp.concatenate(rows, axis=0)
    pad = (-slab.shape[0]) % 8
    return jnp.pad(slab, ((0, pad), (0, 0)))


def _unpack_small(slab, shapes):
    out, r = [], 0
    for shp in shapes:
        size = 1
        for s in shp:
            size *= s
        nr = -(-size // 128)
        out.append(slab[r:r + nr].reshape(-1)[:size].reshape(shp))
        r += nr
    return out


def _to_blocks_cols(a):
    R, C = a.shape
    return a.reshape(R, N_DEV, C // N_DEV).transpose(1, 0, 2)


def _from_blocks_cols(a):
    n, R, c = a.shape
    return a.transpose(1, 0, 2).reshape(R, n * c)


def kernel(x, meta_tokens, attn_pre_norm, w_in, gdn_conv_w, gdn_A_log, gdn_dt_bias, gdn_norm_w, sb_norm_w, w_out, attn_post_norm, ffn_pre_norm, w_ffn_up, ffn_conv_w, ffn_conv_b, w_ffn_down, ffn_post_norm, loss_target, m_meta_tokens, m_attn_pre_norm, m_w_in, m_gdn_conv_w, m_gdn_A_log, m_gdn_dt_bias, m_gdn_norm_w, m_sb_norm_w, m_w_out, m_attn_post_norm, m_ffn_pre_norm, m_w_ffn_up, m_ffn_conv_w, m_ffn_conv_b, m_w_ffn_down, m_ffn_post_norm, v_meta_tokens, v_attn_pre_norm, v_w_in, v_gdn_conv_w, v_gdn_A_log, v_gdn_dt_bias, v_gdn_norm_w, v_sb_norm_w, v_w_out, v_attn_post_norm, v_ffn_pre_norm, v_w_ffn_up, v_ffn_conv_w, v_ffn_conv_b, v_w_ffn_down, v_ffn_post_norm):
    args = dict(locals())
    seq = x.shape[1]
    LP = -(-(ROW0 + seq) // LP_ALIGN) * LP_ALIGN
    tail = LP - ROW0 - seq

    meta_f = _from_blocks_cols(_run_exchange(_gather_two_level([meta_tokens]), "gather_meta")[0])

    (h0, target), got = _build_rows(x[0], meta_f, loss_target[0], LP,
                                    carry=_gather_two_level([w_in[0].astype(BF16), gdn_conv_w[0]]))
    (u,), _ = _prenorm_fwd(h0, attn_pre_norm)
    win_o = _from_blocks_cols(got[0])
    o_ab = C_QKV
    o_z = o_ab + 2 * GDN_HEADS
    w_inp = jnp.concatenate([win_o[:, :C_QKV], win_o[:, o_z:o_z + C_Z], win_o[:, o_z + C_Z:],
                             win_o[:, o_ab:o_z], jnp.zeros((D_MODEL, C_AB - 2 * GDN_HEADS), BF16)], axis=1)
    gconv_f = _from_blocks_cols(got[1])
    proj = _mm(u, w_inp, F32, "mm_in")
    (qn, kn, vg, beta_b, g_b), got = _gdn_pre_fwd(
        proj, gconv_f, gdn_A_log, gdn_dt_bias,
        carry=_gather_two_level([w_out[0].astype(BF16), w_ffn_down[0].astype(BF16)]))
    w_out_f = got[0].reshape(D_MODEL, D_MODEL)
    w_down_f = got[1].reshape(D_FF, D_MODEL)
    (cu, cw, cqd, ckd, cqk, ct, cgl), got = _gdn_chunk_fwd(
        qn, kn, vg, beta_b, g_b, carry=_gather_two_level([w_ffn_up[0].astype(BF16), ffn_conv_w[0]]))
    w_up_f = _from_blocks_cols(got[0])
    fconv_f = _from_blocks_cols(got[1])
    og, ssave = _gdn_scan_fwd(cu, cw, cqd, ckd, cqk, cgl)
    osb, ctot, sb_nrun = _sb_fwd(proj)
    snw = sb_norm_w.reshape(1, SB_HEADS * SB_DH)
    y = _attn_norm_fwd(og, proj, osb, gdn_norm_w, snw)
    mix = _mm(y, w_out_f, F32, "mm_out")
    h1, n2 = _resid_fwd(h0, mix, attn_post_norm, ffn_pre_norm)
    up = _mm(n2, w_up_f, F32, "mm_up")
    act, conv_y = _convglu_fwd(up, fconv_f, ffn_conv_b)
    f = _mm(act, w_down_f, F32, "mm_down")
    loss_part, dout, df, d_fpost = _final(h1, f, ffn_post_norm, target, seq)

    d_wdown = _mm_tn(act, df, "mm_dw_down")
    dact = _mm_nt(df, w_down_f, F32, "mm_dact")
    dup, d_fconv, d_fconvb = _convglu_bwd(up, conv_y, fconv_f, dact)
    d_wup = _mm_tn(n2, dup, "mm_dw_up")
    dn2 = _mm_nt(dup, w_up_f, F32, "mm_dn2")
    dh1, dmix, d_fpre, d_apost = _resid_bwd(h1, mix, attn_post_norm, ffn_pre_norm, dout, dn2)
    d_wout = _mm_tn(y, dmix, "mm_dw_out")
    dy = _mm_nt(dmix, w_out_f, F32, "mm_dy")
    my_c = lax.axis_index("c")

    def core_halves(blocks):
        halves = [s.reshape((N_CHIP, 2) + s.shape[1:]) for s in blocks]
        return ([lax.dynamic_index_in_dim(h, my_c, axis=1, keepdims=False) for h in halves],
                [lax.dynamic_index_in_dim(h, 1 - my_c, axis=1, keepdims=False).astype(BF16) for h in halves])

    early_names = ("w_out", "w_ffn_up", "w_ffn_down", "ffn_conv_w")
    e_mine, e_send = core_halves([d_wout.reshape(N_DEV, D_MODEL // N_DEV, D_MODEL), _to_blocks_cols(d_wup),
                                  d_wdown.reshape(N_DEV, D_FF // N_DEV, D_MODEL), _to_blocks_cols(d_fconv)])
    (dog, dz, dos, d_gnw, d_snw), e_theirs = _attn_norm_bwd(og, proj, osb, gdn_norm_w, snw, dy,
                                                            carry=_swap_sibling(e_send))
    e_sums = [_add_halves(a, b, "grads_add_" + nm) for nm, a, b in zip(early_names, e_mine, e_theirs)]
    dqs, dks, dvs = _sb_bwd(proj, ctot, sb_nrun, dos)
    (du_, dw_, dqd_, dkd_, dqk_, dgl_), _ = _gdn_scan_bwd(cu, cw, cqd, ckd, cqk, cgl, ssave, dog)
    dqn, dkn, dvg, dbeta, dg = _gdn_chunk_bwd(qn, kn, vg, beta_b, g_b, ct, du_, dw_, dqd_, dkd_, dqk_, dgl_)
    (dqkv, dab, d_gconv, d_gsc), e_recv = _gdn_pre_bwd(proj, gconv_f, gdn_A_log, gdn_dt_bias, dqn, dkn, dvg, dbeta, dg,
                                                       carry=_exchange_chips(e_sums))
    dpieces = [dqkv, dz, dqs, dks, dvs, dab]
    doffs = [0, OFF_Z, OFF_SB, OFF_SB + 512, OFF_SB + 1024, OFF_AB]
    dw_qkv, dw_ab = _mm_tn_pieces(u, [dqkv, dab], "mm_dw_in_gdn")
    dw_z, dw_qs, dw_ks, dw_vs = _mm_tn_pieces(u, [dz, dqs, dks, dvs], "mm_dw_in_rest")
    du0 = _mm_nt_pieces(dpieces, doffs, w_inp, F32, "mm_du")
    d_win = jnp.concatenate([dw_qkv, dw_ab[:, :2 * GDN_HEADS], dw_z, dw_qs, dw_ks, dw_vs], axis=1)
    late_names = ("w_in", "gdn_conv_w")
    l_mine, l_send = core_halves([_to_blocks_cols(d_win), _to_blocks_cols(d_gconv)])
    l_theirs = _run_exchange(_swap_sibling(l_send), "grads_swap_sibling")
    l_sums = [_add_halves(a, b, "grads_add_" + nm) for nm, a, b in zip(late_names, l_mine, l_theirs)]
    (dh0, d_apre), l_recv = _prenorm_bwd(h0, attn_pre_norm, du0, dh1, carry=_exchange_chips(l_sums))
    grad_x = dh0[ROW0:ROW0 + seq][None]
    d_meta = dh0[PAD_ROWS:ROW0]

    small_grads = [d_apre, d_gsc[0:1, :GDN_HEADS], d_gsc[1:2, :GDN_HEADS], d_gnw, d_snw.reshape(1, SB_HEADS, SB_DH),
                   d_apost, d_fpre, d_fconvb, d_fpost]
    loss_rows = jnp.pad(loss_part, ((0, 7), (0, 0)))
    n_param_rows = _pack_small(small_grads).shape[0]
    n_small_rows = n_param_rows + loss_rows.shape[0]
    slab_parts = _gather_direct(
        [jnp.concatenate([_pack_small(small_grads), loss_rows, d_meta.reshape(-1, LANE)], axis=0)],
        name="gather_small_grads")[0]
    me = 4 * lax.axis_index("x") + 2 * lax.axis_index("y") + my_c
    meta_parts = lax.dynamic_index_in_dim(
        slab_parts[:, n_small_rows:].reshape(N_DEV, N_META, N_DEV, LANE), me, axis=2, keepdims=False)
    slab_parts = slab_parts[:, :n_small_rows]

    res = {}
    for nm, parts in zip(early_names + late_names + ("meta_tokens",), list(e_recv) + list(l_recv) + [meta_parts]):
        wloc = args[nm]
        shp = wloc.shape
        w2 = wloc.reshape(shp[-2], shp[-1])
        outs = _adamw(parts, w2, args["m_" + nm].reshape(w2.shape), args["v_" + nm].reshape(w2.shape), "adamw_" + nm)
        res[nm] = [o.reshape(shp) for o in outs]
    small_shapes = [args[nm].shape for nm in SMALL]
    with_loss_rows = lambda slab: jnp.pad(slab, ((0, n_small_rows - n_param_rows), (0, 0)))
    outs = _adamw(slab_parts, with_loss_rows(_pack_small([args[nm] for nm in SMALL])),
                  with_loss_rows(_pack_small([args["m_" + nm] for nm in SMALL])),
                  with_loss_rows(_pack_small([args["v_" + nm] for nm in SMALL])), "adamw_small")
    loss = outs[0][n_param_rows, 0]
    for k in range(4):
        for nm, val in zip(SMALL, _unpack_small(outs[k], small_shapes)):
            res.setdefault(nm, [None] * 4)[k] = val

    order = ("meta_tokens", "attn_pre_norm", "w_in", "gdn_conv_w", "gdn_A_log", "gdn_dt_bias", "gdn_norm_w",
             "sb_norm_w", "w_out", "attn_post_norm", "ffn_pre_norm", "w_ffn_up", "ffn_conv_w", "ffn_conv_b",
             "w_ffn_down", "ffn_post_norm")
    return (loss, grad_x, *[res[nm][0] for nm in order], *[res[nm][1] for nm in order],
            *[res[nm][2] for nm in order], *[res[nm][3] for nm in order])
```

```python
import functools

import jax
import jax.numpy as jnp
from jax import lax
from jax.experimental import pallas as pl
from jax.experimental.pallas import tpu as pltpu

F32 = jnp.float32
BF16 = jnp.bfloat16

D_MODEL = 1024
N_META = 16
GDN_HEADS = 4
GDN_D = 128
GDN_CHUNK = 64
GDN_CONV = 4
GDN_ROWS = 256
SCAN_CHUNKS = 4
SB_HEADS = 8
SB_DH = 64
SB_BLOCK = 128
D_FF = 2816
FFN_CONV = 3
NORM_EPS = 1e-6
L2_EPS = 1e-6
LANE = 128
N_DEV = 8

PAD_ROWS = SB_BLOCK - N_META
ROW0 = SB_BLOCK
SB_SPAN = 512
SB_DEAD = -104.0
SB_SUB = 256
SB_QTILE = 256
LP_ALIGN = 256

C_QKV = 3 * GDN_HEADS * GDN_D
C_Z = GDN_HEADS * GDN_D
C_SB = 3 * SB_HEADS * SB_DH
C_AB = 256
OFF_Z = C_QKV
OFF_SB = OFF_Z + C_Z
OFF_AB = OFF_SB + C_SB
D_INP = OFF_AB + C_AB
D_IN = C_QKV + 2 * GDN_HEADS + C_Z + C_SB

ADAM_LR = 0.001
ADAM_B1 = 0.9
ADAM_B2 = 0.999
ADAM_EPS = 1e-08
ADAM_WD = 0.01
ADAM_STEP = 10

VMEM_LIMIT = 56 * 1024 * 1024
ELEMWISE_VMEM = 8 * 1024 * 1024
MESH = pl.DeviceIdType.MESH


def _cp(sem=None):
    kw = dict(vmem_limit_bytes=VMEM_LIMIT)
    if sem is not None:
        kw["dimension_semantics"] = sem
    return pltpu.CompilerParams(**kw)


def _tile(n, cap, unit=128):
    best = None
    t = unit
    while t <= min(n, cap):
        if n % t == 0:
            best = t
        t += unit
    assert best is not None, (n, cap, unit)
    return best


def _dot(a, b):
    return jnp.dot(a, b, preferred_element_type=F32)


def _dot_nt(a, b):
    return lax.dot_general(a, b, (((1,), (1,)), ((), ())), preferred_element_type=F32)


def _dot_tn(a, b):
    return lax.dot_general(a, b, (((0,), (0,)), ((), ())), preferred_element_type=F32)


def _split(x):
    hi = x.astype(BF16)
    lo = (x - hi.astype(F32)).astype(BF16)
    return hi, lo


def _dot1(a, b, f=_dot):
    return f(a.astype(BF16), b.astype(BF16))


def _dot3(a, b, f=_dot):
    ah, al = _split(a)
    bh, bl = _split(b)
    return f(ah, bh) + (f(ah, bl) + f(al, bh))


def _dot_exact_l(m_bf16, x, f=_dot):
    xh, xl = _split(x)
    return f(m_bf16, xh) + f(m_bf16, xl)


def _dot_exact_r(x, m_bf16, f=_dot):
    xh, xl = _split(x)
    return f(xh, m_bf16) + f(xl, m_bf16)


def _iota2(shape, dim):
    return lax.broadcasted_iota(jnp.int32, shape, dim)


def _sigmoid(x):
    return 1.0 / (1.0 + jnp.exp(-x))


def _softplus(x):
    return jnp.maximum(x, 0.0) + jnp.log(1.0 + jnp.exp(-jnp.abs(x)))


def _colsum(x):
    return jnp.sum(x, axis=0, keepdims=True)


def _rowsum(x):
    return jnp.sum(x, axis=-1, keepdims=True)


def _mm(a, b, out_dtype, name):
    M, K = a.shape
    K2, N = b.shape
    assert K == K2
    tm = _tile(M, 768)
    tn = _tile(N, max(128, (6 * 1024 * 1024) // (2 * K)))

    def body(a_ref, b_ref, o_ref):
        o_ref[...] = _dot(a_ref[...].astype(BF16), b_ref[...].astype(BF16)).astype(o_ref.dtype)

    return pl.pallas_call(
        body, name=name, grid=(N // tn, M // tm),
        in_specs=[pl.BlockSpec((tm, K), lambda j, i: (i, 0)), pl.BlockSpec((K, tn), lambda j, i: (0, j))],
        out_specs=pl.BlockSpec((tm, tn), lambda j, i: (i, j)),
        out_shape=jax.ShapeDtypeStruct((M, N), out_dtype),
        compiler_params=_cp(("parallel", "parallel")),
    )(a, b)


def _mm_nt(a, b, out_dtype, name):
    M, K = a.shape
    N, K2 = b.shape
    assert K == K2
    tm = _tile(M, 768)
    tn = _tile(N, max(128, (6 * 1024 * 1024) // (2 * K)))

    def body(a_ref, b_ref, o_ref):
        o_ref[...] = _dot_nt(a_ref[...].astype(BF16), b_ref[...].astype(BF16)).astype(o_ref.dtype)

    return pl.pallas_call(
        body, name=name, grid=(N // tn, M // tm),
        in_specs=[pl.BlockSpec((tm, K), lambda j, i: (i, 0)), pl.BlockSpec((tn, K), lambda j, i: (j, 0))],
        out_specs=pl.BlockSpec((tm, tn), lambda j, i: (i, j)),
        out_shape=jax.ShapeDtypeStruct((M, N), out_dtype),
        compiler_params=_cp(("parallel", "parallel")),
    )(a, b)


def _mm_nt_pieces(pieces, offsets, b, out_dtype, name):
    M = pieces[0].shape[0]
    N = b.shape[0]
    n = len(pieces)
    widths = [p.shape[1] for p in pieces]
    assert all(off % k == 0 for off, k in zip(offsets, widths))
    tm = _tile(M, 768)
    tn = _tile(N, 1024)

    def body(*refs):
        acc = _dot_nt(refs[0][...].astype(BF16), refs[n][...].astype(BF16))
        for p in range(1, n):
            acc = acc + _dot_nt(refs[p][...].astype(BF16), refs[n + p][...].astype(BF16))
        refs[2 * n][...] = acc.astype(out_dtype)

    return pl.pallas_call(
        body, name=name, grid=(N // tn, M // tm),
        in_specs=[pl.BlockSpec((tm, k), lambda j, i: (i, 0)) for k in widths]
        + [pl.BlockSpec((tn, k), functools.partial(lambda j, i, blk: (j, blk), blk=off // k))
           for off, k in zip(offsets, widths)],
        out_specs=pl.BlockSpec((tm, tn), lambda j, i: (i, j)),
        out_shape=jax.ShapeDtypeStruct((M, N), out_dtype),
        compiler_params=_cp(("parallel", "parallel")),
    )(*pieces, *([b] * n))


def _mm_tn_pieces(a, pieces, name):
    M, K = a.shape
    n = len(pieces)
    tm = _tile(M, 768)

    def body(*refs):
        @pl.when(pl.program_id(0) == 0)
        def _():
            for p in range(n):
                refs[1 + n + p][...] = jnp.zeros_like(refs[1 + n + p])
        at = refs[0][...].astype(BF16)
        for p in range(n):
            refs[1 + n + p][...] += _dot_tn(at, refs[1 + p][...].astype(BF16))

    return pl.pallas_call(
        body, name=name, grid=(M // tm,),
        in_specs=[pl.BlockSpec((tm, K), lambda m: (m, 0))] + [pl.BlockSpec((tm, p.shape[1]), lambda m: (m, 0)) for p in pieces],
        out_specs=[pl.BlockSpec((K, p.shape[1]), lambda m: (0, 0)) for p in pieces],
        out_shape=[jax.ShapeDtypeStruct((K, p.shape[1]), F32) for p in pieces],
        compiler_params=_cp(("arbitrary",)),
    )(a, *pieces)


def _mm_tn(a, b, name):
    M, K = a.shape
    M2, N = b.shape
    assert M == M2
    tm = _tile(M, 768)
    tk = _tile(K, 2816)
    tn = _tile(N, 2816)

    def body(a_ref, b_ref, o_ref):
        @pl.when(pl.program_id(2) == 0)
        def _():
            o_ref[...] = jnp.zeros_like(o_ref)
        o_ref[...] += _dot_tn(a_ref[...].astype(BF16), b_ref[...].astype(BF16))

    return pl.pallas_call(
        body, name=name, grid=(K // tk, N // tn, M // tm),
        in_specs=[pl.BlockSpec((tm, tk), lambda i, j, m: (m, i)), pl.BlockSpec((tm, tn), lambda i, j, m: (m, j))],
        out_specs=pl.BlockSpec((tk, tn), lambda i, j, m: (i, j)),
        out_shape=jax.ShapeDtypeStruct((K, N), F32),
        compiler_params=_cp(("parallel", "parallel", "arbitrary")),
    )(a, b)


def _rms(x):
    return lax.rsqrt(jnp.mean(x * x, axis=-1, keepdims=True) + NORM_EPS)


def _rms_bwd(x, w, dy):
    r = _rms(x)
    n = x * r
    dyw = dy * w
    dx = r * (dyw - n * jnp.mean(dyw * n, axis=-1, keepdims=True))
    return dx, dy * n


def _build_rows(x, meta, target, LP, carry=None):
    seq, D = x.shape
    T = SB_BLOCK
    nx = seq // T
    assert seq % T == 0 and meta.shape[0] == N_META

    def body(x_ref, m_ref, t_ref, h_ref, tp_ref):
        i = pl.program_id(0)
        inside = (i >= 1) & (i <= nx)
        head = jnp.concatenate([jnp.zeros((PAD_ROWS, D), F32), m_ref[...]], axis=0)
        h_ref[...] = jnp.where(i == 0, head, jnp.where(inside, x_ref[...], 0.0))
        tp_ref[...] = jnp.where(inside, t_ref[...], 0.0)

    tok = pl.BlockSpec((T, D), lambda i: (jnp.clip(i - 1, 0, nx - 1), 0))
    row = pl.BlockSpec((T, D), lambda i: (i, 0))
    out = jax.ShapeDtypeStruct((LP, D), F32)
    return _call_carrying(
        carry, body, LP // T, name="build_rows",
        in_specs=[tok, pl.BlockSpec((N_META, D), lambda i: (0, 0)), tok], out_specs=[row, row], out_shape=[out, out],
        operands=(x, meta, target))


def _prenorm_fwd(h0, w, carry=None):
    LP, D = h0.shape
    T = _tile(LP, 512)

    def body(h_ref, w_ref, u_ref):
        h = h_ref[...]
        u_ref[...] = (h * _rms(h) * w_ref[...]).astype(BF16)

    return _call_carrying(
        carry, body, LP // T, name="prenorm_fwd",
        in_specs=[pl.BlockSpec((T, D), lambda i: (i, 0)), pl.BlockSpec((1, D), lambda i: (0, 0))],
        out_specs=[pl.BlockSpec((T, D), lambda i: (i, 0))],
        out_shape=[jax.ShapeDtypeStruct((LP, D), BF16)],
        operands=(h0, w))


def _prenorm_bwd(h0, w, du, dh1, carry=None):
    LP, D = h0.shape
    T = _tile(LP, 512)

    def body(h_ref, w_ref, du_ref, dh1_ref, dh0_ref, dw_ref):
        @pl.when(pl.program_id(0) == 0)
        def _():
            dw_ref[...] = jnp.zeros_like(dw_ref)
        dx, dwn = _rms_bwd(h_ref[...], w_ref[...], du_ref[...])
        dh0_ref[...] = dh1_ref[...] + dx
        dw_ref[...] += _colsum(dwn)

    row = pl.BlockSpec((T, D), lambda i: (i, 0))
    vec = pl.BlockSpec((1, D), lambda i: (0, 0))
    return _call_carrying(
        carry, body, LP // T, name="prenorm_bwd",
        in_specs=[row, vec, row, row], out_specs=[row, vec],
        out_shape=[jax.ShapeDtypeStruct((LP, D), F32), jax.ShapeDtypeStruct((1, D), F32)],
        operands=(h0, w, du, dh1))


def _gdn_gate_consts(alog_ref, dtb_ref, h):
    a_coef = -jnp.exp(alog_ref[0:1, h:h + 1])
    return a_coef, dtb_ref[0:1, h:h + 1]


def _gdn_pre_fwd(proj, conv_w, a_log, dt_bias, carry=None):
    LP = proj.shape[0]
    T = _tile(LP, 256)
    C = C_QKV
    H = GDN_HEADS

    def body(x_ref, halo_ref, ab_ref, cw_ref, alog_ref, dtb_ref, q_ref, k_ref, v_ref, beta_ref, g_ref):
        i = pl.program_id(0)

        def conv_silu(cols):
            ext = jnp.concatenate([jnp.where(i > 0, halo_ref[:, cols], 0.0), x_ref[:, cols]], axis=0)
            w = cw_ref[:, cols]
            y = w[GDN_CONV - 1:GDN_CONV] * ext[8:]
            for j in range(GDN_CONV - 1):
                y = y + w[j:j + 1] * pltpu.roll(ext, GDN_CONV - 1 - j, 0)[8:]
            return y * _sigmoid(y)

        for h in range(H):
            sl = slice(h * GDN_D, (h + 1) * GDN_D)
            cq = conv_silu(sl)
            q_ref[:, sl] = cq * lax.rsqrt(_rowsum(cq * cq) + L2_EPS) * (GDN_D ** -0.5)
            ck = conv_silu(slice(512 + h * GDN_D, 512 + (h + 1) * GDN_D))
            k_ref[:, sl] = ck * lax.rsqrt(_rowsum(ck * ck) + L2_EPS)
            v_ref[:, sl] = conv_silu(slice(1024 + h * GDN_D, 1024 + (h + 1) * GDN_D))
        ab = ab_ref[...]
        valid = (i * T + _iota2((T, 1), 0)) >= PAD_ROWS
        for h in range(H):
            sl = slice(h * GDN_D, (h + 1) * GDN_D)
            a_coef, dtb = _gdn_gate_consts(alog_ref, dtb_ref, h)
            g = jnp.where(valid, a_coef * _softplus(ab[:, h:h + 1] + dtb), 0.0)
            beta = jnp.where(valid, _sigmoid(ab[:, H + h:H + h + 1]), 0.0)
            g_ref[:, sl] = jnp.broadcast_to(g, (T, GDN_D))
            beta_ref[:, sl] = jnp.broadcast_to(beta, (T, GDN_D))

    t8 = T // 8
    row512 = pl.BlockSpec((T, 512), lambda i: (i, 0))
    small = lambda r, c: pl.BlockSpec((r, c), lambda i: (0, 0))
    out = jax.ShapeDtypeStruct((LP, 512), F32)
    return _call_carrying(
        carry, body, LP // T, name="gdn_pre_fwd",
        in_specs=[pl.BlockSpec((T, C), lambda i: (i, 0)),
                  pl.BlockSpec((8, C), lambda i: (jnp.maximum(i * t8 - 1, 0), 0)),
                  pl.BlockSpec((T, C_AB), lambda i: (i, OFF_AB // C_AB)),
                  small(GDN_CONV, C), small(1, H), small(1, H)],
        out_specs=[row512] * 5, out_shape=[out] * 5,
        operands=(proj, proj, proj, conv_w, a_log, dt_bias))


def _gdn_pre_bwd(proj, conv_w, a_log, dt_bias, dq, dk, dv, dbeta, dg, carry=None):
    LP = proj.shape[0]
    T = _tile(LP, 256)
    C = C_QKV
    H = GDN_HEADS
    TE = T + 8
    nt = LP // T

    def body(x_ref, xp_ref, xn_ref, ab_ref, cw_ref, alog_ref, dtb_ref,
             dq_ref, dqn_ref, dk_ref, dkn_ref, dv_ref, dvn_ref, dbeta_ref, dg_ref,
             dx_ref, dab_ref, dcw_ref, dsc_ref):
        i = pl.program_id(0)

        @pl.when(i == 0)
        def _():
            dcw_ref[...] = jnp.zeros_like(dcw_ref)
            dsc_ref[...] = jnp.zeros_like(dsc_ref)

        last = i == nt - 1

        def strip(cols, d_ref, dn_ref, dcols, scale):
            ext = jnp.concatenate([jnp.where(i > 0, xp_ref[:, cols], 0.0), x_ref[:, cols],
                                   jnp.where(last, 0.0, xn_ref[:, cols])], axis=0)
            sh = [ext[8:8 + TE]] + [pltpu.roll(ext, s, 0)[8:8 + TE] for s in range(1, GDN_CONV)]
            w = cw_ref[:, cols]
            y = w[GDN_CONV - 1:GDN_CONV] * sh[0]
            for j in range(GDN_CONV - 1):
                y = y + w[j:j + 1] * sh[GDN_CONV - 1 - j]
            sg = _sigmoid(y)
            d = jnp.concatenate([d_ref[:, dcols], jnp.where(last, 0.0, dn_ref[:, dcols])], axis=0)
            if scale is not None:
                c = y * sg
                r = lax.rsqrt(_rowsum(c * c) + L2_EPS)
                n = c * r
                d = scale * r * (d - n * _rowsum(d * n))
            dy = d * (sg * (1.0 + y * (1.0 - sg)))
            dy_t = dy[0:T]
            for j in range(GDN_CONV):
                dcw_ref[j:j + 1, cols] += _colsum(dy_t * sh[GDN_CONV - 1 - j][0:T])
            dx = w[GDN_CONV - 1:GDN_CONV] * dy_t
            for j in range(GDN_CONV - 1):
                dx = dx + w[j:j + 1] * pltpu.roll(dy, TE - (GDN_CONV - 1 - j), 0)[0:T]
            dx_ref[:, cols] = dx.astype(BF16)

        for h in range(H):
            sl = slice(h * GDN_D, (h + 1) * GDN_D)
            strip(sl, dq_ref, dqn_ref, sl, GDN_D ** -0.5)
            strip(slice(512 + h * GDN_D, 512 + (h + 1) * GDN_D), dk_ref, dkn_ref, sl, 1.0)
            strip(slice(1024 + h * GDN_D, 1024 + (h + 1) * GDN_D), dv_ref, dvn_ref, sl, None)
        ab = ab_ref[...]
        valid = (i * T + _iota2((T, 1), 0)) >= PAD_ROWS
        lane = _iota2((T, C_AB), 1)
        lane1 = _iota2((1, 128), 1)
        dab = jnp.zeros((T, C_AB), F32)
        dsc_a = jnp.zeros((1, 128), F32)
        dsc_d = jnp.zeros((1, 128), F32)
        for h in range(H):
            a_coef, dtb = _gdn_gate_consts(alog_ref, dtb_ref, h)
            pre = ab[:, h:h + 1] + dtb
            dgh = jnp.where(valid, dg_ref[:, h * GDN_D:h * GDN_D + 1], 0.0)
            da = dgh * a_coef * _sigmoid(pre)
            beta = _sigmoid(ab[:, H + h:H + h + 1])
            db = jnp.where(valid, dbeta_ref[:, h * GDN_D:h * GDN_D + 1], 0.0) * beta * (1.0 - beta)
            dab = dab + jnp.where(lane == h, da, 0.0) + jnp.where(lane == H + h, db, 0.0)
            dsc_a = dsc_a + jnp.where(lane1 == h, _colsum(dgh * a_coef * _softplus(pre)), 0.0)
            dsc_d = dsc_d + jnp.where(lane1 == h, _colsum(da), 0.0)
        dab_ref[...] = dab.astype(BF16)
        dsc_ref[0:1, :] += dsc_a
        dsc_ref[1:2, :] += dsc_d

    t8 = T // 8
    nb8 = LP // 8
    prev8 = lambda w: pl.BlockSpec((8, w), lambda i: (jnp.maximum(i * t8 - 1, 0), 0))
    next8 = lambda w: pl.BlockSpec((8, w), lambda i: (jnp.minimum((i + 1) * t8, nb8 - 1), 0))
    row = lambda w: pl.BlockSpec((T, w), lambda i: (i, 0))
    small = lambda r, c: pl.BlockSpec((r, c), lambda i: (0, 0))
    return _call_carrying(
        carry, body, nt, name="gdn_pre_bwd",
        in_specs=[row(C), prev8(C), next8(C), pl.BlockSpec((T, C_AB), lambda i: (i, OFF_AB // C_AB)),
                  small(GDN_CONV, C), small(1, H), small(1, H),
                  row(512), next8(512), row(512), next8(512), row(512), next8(512), row(512), row(512)],
        out_specs=[row(C), row(C_AB), small(GDN_CONV, C), small(2, 128)],
        out_shape=[jax.ShapeDtypeStruct((LP, C), BF16), jax.ShapeDtypeStruct((LP, C_AB), BF16),
                   jax.ShapeDtypeStruct((GDN_CONV, C), F32), jax.ShapeDtypeStruct((2, 128), F32)],
        operands=(proj, proj, proj, proj, conv_w, a_log, dt_bias, dq, dq, dk, dk, dv, dv, dbeta, dg))


def _tri_masks():
    r = _iota2((GDN_CHUNK, GDN_CHUNK), 0)
    c = _iota2((GDN_CHUNK, GDN_CHUNK), 1)
    return r >= c, r > c


def _gdn_chunk_common(q, k, v, beta, gb):
    incl, strict = _tri_masks()
    l_incl = incl.astype(BF16)
    gd = _dot_exact_l(l_incl, jnp.where(strict, gb[:, :GDN_CHUNK], 0.0))
    gc = _dot_exact_l(l_incl, gb)
    decay = jnp.where(incl, jnp.exp(jnp.where(incl, gd, 0.0)), 0.0)
    exp_g = jnp.exp(gc)
    g_last = gc[GDN_CHUNK - 1:GDN_CHUNK, :]
    kd_fac = jnp.exp(g_last - gc)
    gl = jnp.exp(g_last)
    kb = k * beta
    kk = _dot1(kb, k, _dot_nt)
    return dict(incl=incl, strict=strict, decay=decay, exp_g=exp_g, kd_fac=kd_fac, gl=gl, kb=kb, kk=kk,
                vb=v * beta, kbg=kb * exp_g)


def _interleave(gens):
    gens = list(gens)
    while gens:
        alive = []
        for g in gens:
            try:
                next(g)
                alive.append(g)
            except StopIteration:
                pass
        gens = alive


def _call_carrying(ex, body, nsteps, *, name, in_specs, out_specs, out_shape, operands, scratch_shapes=()):
    n_in, n_out, n_scr = len(in_specs), len(out_specs), len(scratch_shapes)
    n = ex.n if ex is not None else 0

    def full(*refs):
        o0 = n_in + n
        s0 = o0 + n_out + n
        ex_refs = (refs[n_in:o0], refs[o0 + n_out:s0], refs[s0 + n_scr:])
        step = pl.program_id(0)
        _carry_begin(ex, ex_refs, step, nsteps)
        body(*refs[:n_in], *refs[o0:o0 + n_out], *refs[s0:s0 + n_scr])
        _carry_end(ex, ex_refs, step, nsteps)

    res = pl.pallas_call(
        full, name=name, grid=(nsteps,),
        in_specs=list(in_specs) + [ANY_SPEC] * n, out_specs=list(out_specs) + [ANY_SPEC] * n,
        out_shape=list(out_shape) + (ex.out_shapes if ex is not None else []),
        scratch_shapes=list(scratch_shapes) + (ex.scratch if ex is not None else []),
        compiler_params=pltpu.CompilerParams(dimension_semantics=("arbitrary",), vmem_limit_bytes=VMEM_LIMIT,
                                             has_side_effects=ex is not None),
    )(*operands, *(ex.arrs if ex is not None else []))
    return list(res[:n_out]), list(res[n_out:])


def _gdn_chunk_fwd(qn, kn, v, beta_b, g_b, carry=None):
    LP = qn.shape[0]
    R = GDN_ROWS
    H = GDN_HEADS
    CH = GDN_CHUNK

    def body(q_ref, k_ref, v_ref, b_ref, g_ref, u_ref, w_ref, qd_ref, kd_ref, qk_ref, t_ref, gl_ref):
        def item(cc, h):
            rs = slice(cc * CH, (cc + 1) * CH)
            sl = slice(h * GDN_D, (h + 1) * GDN_D)
            s64 = slice(h * CH, (h + 1) * CH)
            q, k = q_ref[rs, sl], k_ref[rs, sl]
            m = _gdn_chunk_common(q, k, v_ref[rs, sl], b_ref[rs, sl], g_ref[rs, sl])
            qk_raw = _dot1(q, k, _dot_nt)
            yield
            a = jnp.where(m["strict"], m["kk"] * m["decay"], 0.0)
            eye = (_iota2((CH, CH), 0) == _iota2((CH, CH), 1)).astype(F32)
            t = eye - a
            p = _dot3(a, a)
            yield
            for _ in range(4):
                t = t + _dot3(t, p)
                p = _dot3(p, p)
                yield
            t = t + _dot3(t, p)
            yield
            u_ref[rs, sl] = _dot1(t, m["vb"])
            w_ref[rs, sl] = _dot1(t, m["kbg"])
            qk_ref[rs, s64] = qk_raw * m["decay"]
            t_ref[rs, s64] = t
            qd_ref[rs, sl] = q * m["exp_g"]
            kd_ref[rs, sl] = k * m["kd_fac"]
            gl_ref[cc * 8:(cc + 1) * 8, sl] = jnp.broadcast_to(m["gl"], (8, GDN_D))

        _interleave(item(cc, h) for cc in range(R // CH) for h in range(H))

    row = lambda w: pl.BlockSpec((R, w), lambda i: (i, 0))
    o512 = jax.ShapeDtypeStruct((LP, 512), F32)
    o256 = jax.ShapeDtypeStruct((LP, 256), F32)
    return _call_carrying(
        carry, body, LP // R, name="gdn_chunk_fwd",
        in_specs=[row(512)] * 5,
        out_specs=[row(512)] * 4 + [row(256)] * 2 + [pl.BlockSpec((R // 8, 512), lambda i: (i, 0))],
        out_shape=[o512] * 4 + [o256] * 2 + [jax.ShapeDtypeStruct((LP // 8, 512), F32)],
        operands=(qn, kn, v, beta_b, g_b))


def _gdn_chunk_bwd(qn, kn, v, beta_b, g_b, t_all, du, dw, dqd, dkd, dqk, dgl):
    LP = qn.shape[0]
    R = GDN_ROWS
    H = GDN_HEADS
    CH = GDN_CHUNK

    def body(q_ref, k_ref, v_ref, b_ref, g_ref, t_ref, du_ref, dw_ref, dqd_ref, dkd_ref, dqk_ref, dgl_ref,
             dq_ref, dk_ref, dv_ref, db_ref, dg_ref):
        ones = jnp.ones((CH, GDN_D), BF16)

        def item(cc, h):
            rs = slice(cc * CH, (cc + 1) * CH)
            sl = slice(h * GDN_D, (h + 1) * GDN_D)
            s64 = slice(h * CH, (h + 1) * CH)
            q, k, vv, beta = q_ref[rs, sl], k_ref[rs, sl], v_ref[rs, sl], b_ref[rs, sl]
            m = _gdn_chunk_common(q, k, vv, beta, g_ref[rs, sl])
            incl, strict, decay = m["incl"], m["strict"], m["decay"]
            t = t_ref[rs, s64]
            du_, dw_ = du_ref[rs, sl], dw_ref[rs, sl]
            dqd_, dkd_ = dqd_ref[rs, sl], dkd_ref[rs, sl]
            d_t = _dot1(du_, m["vb"], _dot_nt) + _dot1(dw_, m["kbg"], _dot_nt)
            dvb = _dot1(t, du_, _dot_tn)
            dkbg = _dot1(t, dw_, _dot_tn)
            qk_raw = _dot1(q, k, _dot_nt)
            yield
            x1 = _dot3(d_t, t, _dot_nt)
            dkb = dkbg * m["exp_g"]
            d_gi = _rowsum(dkbg * m["kbg"])
            yield
            d_a = jnp.where(strict, -_dot3(t, x1, _dot_tn), 0.0)
            yield
            d_kk = d_a * decay
            dqk_m = jnp.where(incl, dqk_ref[rs, s64], 0.0)
            dqk_raw = dqk_m * decay
            mm = (d_a * m["kk"] + dqk_m * qk_raw) * decay
            dkb = dkb + _dot1(d_kk, k)
            dk_ = _dot1(d_kk, m["kb"], _dot_tn) + _dot1(dqk_raw, q, _dot_tn)
            dq_ = _dot1(dqk_raw, k) + dqd_ * m["exp_g"]
            d_gi = d_gi + (_dot_exact_r(mm, ones) - _dot_exact_r(mm, ones, _dot_tn))
            yield
            d_gi = d_gi + _rowsum(dqd_ * q * m["exp_g"])
            e = _rowsum(dkd_ * k * m["kd_fac"])
            d_gi = d_gi - e
            d_glast = _colsum(jnp.broadcast_to(e, (CH, GDN_D))) + dgl_ref[cc * 8:cc * 8 + 1, sl] * m["gl"]
            dk_ = dk_ + dkd_ * m["kd_fac"] + dkb * beta
            d_gi = d_gi + jnp.where(_iota2((CH, GDN_D), 0) == CH - 1, d_glast, 0.0)
            u_incl = (_iota2((CH, CH), 1) >= _iota2((CH, CH), 0)).astype(BF16)
            dq_ref[rs, sl] = dq_
            dk_ref[rs, sl] = dk_
            dv_ref[rs, sl] = dvb * beta
            db_ref[rs, sl] = jnp.broadcast_to(_rowsum(dvb * vv) + _rowsum(dkb * k), (CH, GDN_D))
            dg_ref[rs, sl] = _dot_exact_l(u_incl, d_gi)

        _interleave(item(cc, h) for cc in range(R // CH) for h in range(H))

    row = lambda w: pl.BlockSpec((R, w), lambda i: (i, 0))
    o512 = jax.ShapeDtypeStruct((LP, 512), F32)
    gl_spec = pl.BlockSpec((R // 8, 512), lambda i: (i, 0))
    return pl.pallas_call(
        body, name="gdn_chunk_bwd", grid=(LP // R,),
        in_specs=[row(512)] * 5 + [row(256)] + [row(512)] * 4 + [row(256), gl_spec],
        out_specs=[row(512)] * 5, out_shape=[o512] * 5,
        compiler_params=_cp(("parallel",)),
    )(qn, kn, v, beta_b, g_b, t_all, du, dw, dqd, dkd, dqk, dgl)


def _gdn_scan_fwd(u, w, qd, kd, qk, gl):
    LP = u.shape[0]
    CH = GDN_CHUNK
    CPS = SCAN_CHUNKS
    N = LP // CH
    NS = N // CPS
    H = GDN_HEADS

    def body(u_ref, w_ref, qd_ref, kd_ref, qk_ref, gl_ref, o_ref, ssave_ref, s_sc):
        @pl.when(pl.program_id(0) == 0)
        def _():
            s_sc[...] = jnp.zeros_like(s_sc)

        for cc in range(CPS):
            rs = slice(cc * CH, (cc + 1) * CH)
            ssave_ref[cc * GDN_D:(cc + 1) * GDN_D, :] = s_sc[...]

            def item(h):
                sl = slice(h * GDN_D, (h + 1) * GDN_D)
                s = s_sc[:, sl]
                v_new = u_ref[rs, sl] - _dot1(w_ref[rs, sl], s)
                o_s = _dot1(qd_ref[rs, sl], s)
                yield
                o_ref[rs, sl] = o_s + _dot1(qk_ref[rs, h * CH:(h + 1) * CH], v_new)
                s_sc[:, sl] = s * gl_ref[cc * 8:cc * 8 + 1, sl] + _dot1(kd_ref[rs, sl], v_new, _dot_tn)

            _interleave(item(h) for h in range(H))

    row = lambda w_: pl.BlockSpec((CPS * CH, w_), lambda n: (n, 0))
    return pl.pallas_call(
        body, name="gdn_scan_fwd", grid=(NS,),
        in_specs=[row(512)] * 4 + [row(256), pl.BlockSpec((CPS * 8, 512), lambda n: (n, 0))],
        out_specs=[row(512), pl.BlockSpec((CPS * GDN_D, 512), lambda n: (n, 0))],
        out_shape=[jax.ShapeDtypeStruct((LP, 512), F32), jax.ShapeDtypeStruct((N * GDN_D, 512), F32)],
        scratch_shapes=[pltpu.VMEM((GDN_D, 512), F32)],
        compiler_params=_cp(("arbitrary",)),
    )(u, w, qd, kd, qk, gl)


def _gdn_scan_bwd(u, w, qd, kd, qk, gl, ssave, do, carry=None):
    LP = u.shape[0]
    CH = GDN_CHUNK
    CPS = SCAN_CHUNKS
    N = LP // CH
    NS = N // CPS
    H = GDN_HEADS

    def body(u_ref, w_ref, qd_ref, kd_ref, qk_ref, gl_ref, s_ref, do_ref,
             du_ref, dw_ref, dqd_ref, dkd_ref, dqk_ref, dgl_ref, ds_sc):
        @pl.when(pl.program_id(0) == 0)
        def _():
            ds_sc[...] = jnp.zeros_like(ds_sc)

        for cc in reversed(range(CPS)):
            rs = slice(cc * CH, (cc + 1) * CH)
            r8 = slice(cc * 8, (cc + 1) * 8)

            def item(h):
                sl = slice(h * GDN_D, (h + 1) * GDN_D)
                s64 = slice(h * CH, (h + 1) * CH)
                s = s_ref[cc * GDN_D:(cc + 1) * GDN_D, sl]
                ds = ds_sc[:, sl]
                do_ = do_ref[rs, sl]
                w_, qd_, kd_, qk_ = w_ref[rs, sl], qd_ref[rs, sl], kd_ref[rs, sl], qk_ref[rs, s64]
                v_new = u_ref[rs, sl] - _dot1(w_, s)
                d_vnew = _dot1(qk_, do_, _dot_tn) + _dot1(kd_, ds)
                dqd_ref[rs, sl] = _dot1(do_, s, _dot_nt)
                ds_new = ds * gl_ref[cc * 8:cc * 8 + 1, sl] + _dot1(qd_, do_, _dot_tn)
                dgl_ref[r8, sl] = jnp.broadcast_to(jnp.sum(_colsum(ds * s), axis=-1, keepdims=True), (8, GDN_D))
                yield
                du_ref[rs, sl] = d_vnew
                dw_ref[rs, sl] = -_dot1(d_vnew, s, _dot_nt)
                dkd_ref[rs, sl] = _dot1(v_new, ds, _dot_nt)
                dqk_ref[rs, s64] = _dot1(do_, v_new, _dot_nt)
                ds_sc[:, sl] = ds_new - _dot1(w_, d_vnew, _dot_tn)

            _interleave(item(h) for h in range(H))

    rev = lambda w_: pl.BlockSpec((CPS * CH, w_), lambda n: (NS - 1 - n, 0))
    rev8 = pl.BlockSpec((CPS * 8, 512), lambda n: (NS - 1 - n, 0))
    o512 = jax.ShapeDtypeStruct((LP, 512), F32)
    return _call_carrying(
        carry, body, NS, name="gdn_scan_bwd",
        in_specs=[rev(512)] * 4 + [rev(256), rev8, pl.BlockSpec((CPS * GDN_D, 512), lambda n: (NS - 1 - n, 0)),
                  rev(512)],
        out_specs=[rev(512)] * 4 + [rev(256), rev8],
        out_shape=[o512] * 4 + [jax.ShapeDtypeStruct((LP, 256), F32), jax.ShapeDtypeStruct((LP // 8, 512), F32)],
        scratch_shapes=[pltpu.VMEM((GDN_D, 512), F32)],
        operands=(u, w, qd, kd, qk, gl, ssave, do))


def _sb_scores(qh, kblk, mask):
    z = _dot_nt(qh, kblk)
    e = jnp.exp(-jnp.abs(z))
    sp = jnp.maximum(z, 0.0) + jnp.log(1.0 + e)
    return z, e, jnp.where(mask, -sp, 0.0), z - sp


def _sb_fwd(proj):
    LP = proj.shape[0]
    B = SB_BLOCK
    W = min(SB_SPAN, LP)
    SUB = SB_SUB
    Q = min(SB_QTILE, LP)
    nq = LP // Q
    nsub = W // SUB
    scale = SB_DH ** -0.5
    qcol, kcol, vcol = OFF_SB // B, (OFF_SB + 512) // B, (OFF_SB + 1024) // B

    def body(q_ref, k_ref, v_ref, tri_ref, o_ref, c_ref, n_ref):
        i = pl.program_id(1)
        lane = _iota2((Q, B), 1)
        head_a = lane < SB_DH
        qs = q_ref[...] * scale
        qh = [jnp.where(head_a, qs, 0.0).astype(BF16), jnp.where(head_a, 0.0, qs).astype(BF16)]
        u_strict = tri_ref[...]
        qpos = i * Q + _iota2((Q, W), 0)
        hi0 = (i + 1) * Q
        nspan = (hi0 + W - 1) // W

        def live(st):
            return (st[0] < nspan) & (st[1] > 0)

        def span(st):
            r, carry = st[0], st[2:]
            hi = hi0 - r * W
            k0 = pl.multiple_of(jnp.maximum(hi - W, 0), B)
            kblk = k_ref[pl.ds(k0, W), :].astype(BF16)
            vblk = v_ref[pl.ds(k0, W), :].astype(BF16)
            kpos = k0 + _iota2((Q, W), 1)
            mask = (kpos < qpos) & (kpos >= PAD_ROWS) & (kpos < hi)
            new = [None] * 4

            def head(h):
                o_acc, c = carry[2 * h], carry[2 * h + 1]
                z, e, l1m, lsg = _sb_scores(qh[h], kblk, mask)
                yield
                subs = [slice(b * SUB, (b + 1) * SUB) for b in range(nsub)]
                suf = [_dot(l1m[:, bs].astype(BF16), u_strict) for bs in subs]
                yield
                parts = [None] * nsub
                for b in reversed(range(nsub)):
                    parts[b] = jnp.where(mask[:, subs[b]], jnp.exp(lsg[:, subs[b]] + suf[b] + c), 0.0)
                    c = c + _rowsum(l1m[:, subs[b]])
                att = jnp.concatenate(parts, axis=1).astype(BF16)
                new[2 * h], new[2 * h + 1] = o_acc + _dot(att, vblk), c

            _interleave(head(h) for h in range(2))
            more = (jnp.maximum(jnp.max(new[1]), jnp.max(new[3])) > SB_DEAD).astype(jnp.int32)
            return (r + 1, more, *new)

        zero_o = jnp.zeros((Q, B), F32)
        zero_c = jnp.zeros((Q, 1), F32)
        nrun, _, o_a, c_a, o_b, c_b = lax.while_loop(
            live, span, (jnp.int32(0), jnp.int32(1), zero_o, zero_c, zero_o, zero_c))
        o_ref[...] = jnp.where(head_a, o_a, o_b)
        c_ref[...] = jnp.where(head_a, c_a, c_b)
        n_ref[pl.program_id(0), i] = nrun

    blk = pl.BlockSpec((Q, B), lambda p, i: (i, p))
    out = jax.ShapeDtypeStruct((LP, 512), F32)
    return pl.pallas_call(
        body, name="sb_fwd", grid=(SB_HEADS // 2, nq),
        in_specs=[pl.BlockSpec((Q, B), lambda p, i: (i, qcol + p)),
                  pl.BlockSpec((LP, B), lambda p, i: (0, kcol + p)),
                  pl.BlockSpec((LP, B), lambda p, i: (0, vcol + p)),
                  pl.BlockSpec((SUB, SUB), lambda p, i: (0, 0))],
        out_specs=[blk, blk, pl.BlockSpec(memory_space=pltpu.SMEM)],
        out_shape=[out, out, jax.ShapeDtypeStruct((SB_HEADS // 2, nq), jnp.int32)],
        compiler_params=_cp(("arbitrary", "arbitrary")),
    )(proj, proj, proj, jnp.tril(jnp.ones((SUB, SUB), BF16), -1))


def _sb_bwd(proj, ctot, nrun_all, do):
    LP = proj.shape[0]
    B = SB_BLOCK
    W = min(SB_SPAN, LP)
    SUB = SB_SUB
    Q = min(SB_QTILE, LP)
    nq = LP // Q
    nsub = W // SUB
    scale = SB_DH ** -0.5
    qcol, kcol, vcol = OFF_SB // B, (OFF_SB + 512) // B, (OFF_SB + 1024) // B

    def body(n_ref, q_ref, k_ref, v_ref, c_ref, do_ref, tril_ref, triu_ref, dq_ref, dk_ref, dv_ref):
        i = pl.program_id(1)

        @pl.when(i == 0)
        def _():
            dk_ref[...] = jnp.zeros_like(dk_ref)
            dv_ref[...] = jnp.zeros_like(dv_ref)

        lane = _iota2((Q, B), 1)
        head_a = lane < SB_DH
        qs = q_ref[...] * scale
        qh = [jnp.where(head_a, qs, 0.0).astype(BF16), jnp.where(head_a, 0.0, qs).astype(BF16)]
        dof = do_ref[...]
        doh = [jnp.where(head_a, dof, 0.0).astype(BF16), jnp.where(head_a, 0.0, dof).astype(BF16)]
        cfull = c_ref[...]
        ctot_h = [cfull[:, 0:1], cfull[:, SB_DH:SB_DH + 1]]
        u_strict = tril_ref[...]
        l_strict = triu_ref[...]
        qpos = i * Q + _iota2((Q, W), 0)
        hi0 = (i + 1) * Q
        nrun = n_ref[pl.program_id(0), i]

        def span(t, carry):
            r = nrun - 1 - t
            hi = hi0 - r * W
            k0 = pl.multiple_of(jnp.maximum(hi - W, 0), B)
            kblk = k_ref[pl.ds(k0, W), :].astype(BF16)
            vblk = v_ref[pl.ds(k0, W), :].astype(BF16)
            kpos = k0 + _iota2((Q, W), 1)
            mask = (kpos < qpos) & (kpos >= PAD_ROWS) & (kpos < hi)
            new = [None] * 6
            dk_add, dv_add = [None, None], [None, None]
            subs = [slice(b * SUB, (b + 1) * SUB) for b in range(nsub)]

            def head(h):
                dq_acc, pre, ecar = carry[3 * h], carry[3 * h + 1], carry[3 * h + 2]
                z, e, l1m, lsg = _sb_scores(qh[h], kblk, mask)
                d_att = _dot_nt(doh[h], vblk)
                yield
                sig = jnp.where(z >= 0.0, 1.0, e) / (1.0 + e)
                suf = [_dot(l1m[:, bs].astype(BF16), u_strict) for bs in subs]
                yield
                att_parts, p_parts = [None] * nsub, [None] * nsub
                for b, bs in enumerate(subs):
                    pre = pre + _rowsum(l1m[:, bs])
                    att_parts[b] = jnp.where(mask[:, bs], jnp.exp(lsg[:, bs] + suf[b] + (ctot_h[h] - pre)), 0.0)
                    p_parts[b] = att_parts[b] * d_att[:, bs]
                pcum = [_dot(p.astype(BF16), l_strict) for p in p_parts]
                yield
                dz_parts = [None] * nsub
                for b, bs in enumerate(subs):
                    sg = sig[:, bs]
                    dz_parts[b] = jnp.where(mask[:, bs], p_parts[b] * (1.0 - sg) - sg * (ecar + pcum[b]), 0.0)
                    ecar = ecar + _rowsum(p_parts[b])
                att = jnp.concatenate(att_parts, axis=1).astype(BF16)
                dz = jnp.concatenate(dz_parts, axis=1).astype(BF16)
                new[3 * h:3 * h + 3] = [dq_acc + _dot(dz, kblk), pre, ecar]
                dk_add[h] = _dot_tn(dz, qh[h])
                dv_add[h] = _dot_tn(att, doh[h])

            _interleave(head(h) for h in range(2))
            dk_ref[pl.ds(k0, W), :] += dk_add[0] + dk_add[1]
            dv_ref[pl.ds(k0, W), :] += dv_add[0] + dv_add[1]
            return tuple(new)

        zero_o = jnp.zeros((Q, B), F32)
        zero_c = jnp.zeros((Q, 1), F32)
        res = lax.fori_loop(0, nrun, span, (zero_o, zero_c, zero_c, zero_o, zero_c, zero_c))
        dq_ref[...] = (jnp.where(head_a, res[0], res[3]) * scale).astype(BF16)

    blk = pl.BlockSpec((Q, B), lambda p, i: (i, p))
    col = pl.BlockSpec((LP, B), lambda p, i: (0, p))
    tri = pl.BlockSpec((SUB, SUB), lambda p, i: (0, 0))
    out = jax.ShapeDtypeStruct((LP, 512), F32)
    return pl.pallas_call(
        body, name="sb_bwd", grid=(SB_HEADS // 2, nq),
        in_specs=[pl.BlockSpec(memory_space=pltpu.SMEM),
                  pl.BlockSpec((Q, B), lambda p, i: (i, qcol + p)),
                  pl.BlockSpec((LP, B), lambda p, i: (0, kcol + p)),
                  pl.BlockSpec((LP, B), lambda p, i: (0, vcol + p)),
                  blk, blk, tri, tri],
        out_specs=[blk, col, col], out_shape=[jax.ShapeDtypeStruct((LP, 512), BF16), out, out],
        compiler_params=_cp(("arbitrary", "arbitrary")),
    )(nrun_all, proj, proj, proj, ctot, do, jnp.tril(jnp.ones((SUB, SUB), BF16), -1),
      jnp.triu(jnp.ones((SUB, SUB), BF16), 1))


def _sb_group_mean():
    r = jnp.right_shift(_iota2((512, 512), 0), 6)
    c = jnp.right_shift(_iota2((512, 512), 1), 6)
    return jnp.where(r == c, 1.0 / SB_DH, 0.0).astype(BF16)


def _attn_norm_fwd(og, proj, osb, gnw, snw):
    LP = og.shape[0]
    T = _tile(LP, 256)

    def body(og_ref, z_ref, os_ref, gnw_ref, snw_ref, y_ref):
        valid = (pl.program_id(0) * T + _iota2((T, 1), 0)) >= PAD_ROWS
        z = z_ref[...]
        zg = z * _sigmoid(z)
        for h in range(GDN_HEADS):
            sl = slice(h * GDN_D, (h + 1) * GDN_D)
            o = og_ref[:, sl]
            y = o * _rms(o) * gnw_ref[...] * zg[:, sl]
            y_ref[:, sl] = jnp.where(valid, y, 0.0).astype(BF16)
        o = os_ref[...]
        msq = _dot_exact_r(o * o, _sb_group_mean())
        y = o * lax.rsqrt(msq + NORM_EPS) * snw_ref[...]
        y_ref[:, 512:] = jnp.where(valid, y, 0.0).astype(BF16)

    row = pl.BlockSpec((T, 512), lambda i: (i, 0))
    return pl.pallas_call(
        body, name="attn_norm_fwd", grid=(LP // T,),
        in_specs=[row, pl.BlockSpec((T, 512), lambda i: (i, OFF_Z // 512)), row,
                  pl.BlockSpec((1, GDN_D), lambda i: (0, 0)), pl.BlockSpec((1, 512), lambda i: (0, 0))],
        out_specs=pl.BlockSpec((T, 1024), lambda i: (i, 0)),
        out_shape=jax.ShapeDtypeStruct((LP, 1024), BF16),
        compiler_params=_cp(("parallel",)),
    )(og, proj, osb, gnw, snw)


def _attn_norm_bwd(og, proj, osb, gnw, snw, dy, carry=None):
    LP = og.shape[0]
    T = _tile(LP, 256)

    def body(og_ref, z_ref, os_ref, gnw_ref, snw_ref, dy_ref, dog_ref, dz_ref, dos_ref, dgw_ref, dsw_ref):
        @pl.when(pl.program_id(0) == 0)
        def _():
            dgw_ref[...] = jnp.zeros_like(dgw_ref)
            dsw_ref[...] = jnp.zeros_like(dsw_ref)
        valid = (pl.program_id(0) * T + _iota2((T, 1), 0)) >= PAD_ROWS
        dy = jnp.where(valid, dy_ref[...], 0.0)
        z = z_ref[...]
        sg = _sigmoid(z)
        zg = z * sg
        dgw = jnp.zeros((1, GDN_D), F32)
        for h in range(GDN_HEADS):
            sl = slice(h * GDN_D, (h + 1) * GDN_D)
            o = og_ref[:, sl]
            dyh = dy[:, sl]
            dx, dwn = _rms_bwd(o, gnw_ref[...], dyh * zg[:, sl])
            dog_ref[:, sl] = dx
            dgw = dgw + _colsum(dwn)
            yn = o * _rms(o) * gnw_ref[...]
            dz_ref[:, sl] = (dyh * yn * (sg[:, sl] * (1.0 + z[:, sl] * (1.0 - sg[:, sl])))).astype(BF16)
        dgw_ref[...] += dgw
        o = os_ref[...]
        gm = _sb_group_mean()
        r = lax.rsqrt(_dot_exact_r(o * o, gm) + NORM_EPS)
        n = o * r
        dys = dy[:, 512:]
        dyw = dys * snw_ref[...]
        dos_ref[...] = r * (dyw - n * _dot_exact_r(dyw * n, gm))
        dsw_ref[...] += _colsum(dys * n)

    row = pl.BlockSpec((T, 512), lambda i: (i, 0))
    gw = pl.BlockSpec((1, GDN_D), lambda i: (0, 0))
    sw = pl.BlockSpec((1, 512), lambda i: (0, 0))
    o512 = jax.ShapeDtypeStruct((LP, 512), F32)
    return _call_carrying(
        carry, body, LP // T, name="attn_norm_bwd",
        in_specs=[row, pl.BlockSpec((T, 512), lambda i: (i, OFF_Z // 512)), row, gw, sw,
                  pl.BlockSpec((T, 1024), lambda i: (i, 0))],
        out_specs=[row, row, row, gw, sw],
        out_shape=[o512, jax.ShapeDtypeStruct((LP, 512), BF16), o512, jax.ShapeDtypeStruct((1, GDN_D), F32),
                   jax.ShapeDtypeStruct((1, 512), F32)],
        operands=(og, proj, osb, gnw, snw, dy))


def _resid_fwd(h0, mix, w_post, w_pre):
    LP, D = h0.shape
    T = _tile(LP, 512)

    def body(h0_ref, mix_ref, wp_ref, wf_ref, h1_ref, n2_ref):
        mix = mix_ref[...]
        h1 = h0_ref[...] + mix * _rms(mix) * wp_ref[...]
        h1_ref[...] = h1
        n2_ref[...] = (h1 * _rms(h1) * wf_ref[...]).astype(BF16)

    row = pl.BlockSpec((T, D), lambda i: (i, 0))
    vec = pl.BlockSpec((1, D), lambda i: (0, 0))
    return pl.pallas_call(
        body, name="resid_fwd", grid=(LP // T,),
        in_specs=[row, row, vec, vec], out_specs=[row, row],
        out_shape=[jax.ShapeDtypeStruct((LP, D), F32), jax.ShapeDtypeStruct((LP, D), BF16)],
        compiler_params=_cp(("parallel",)),
    )(h0, mix, w_post, w_pre)


def _resid_bwd(h1, mix, w_post, w_pre, dout, dn2):
    LP, D = h1.shape
    T = _tile(LP, 512)

    def body(h1_ref, mix_ref, wp_ref, wf_ref, dout_ref, dn2_ref, dh1_ref, dmix_ref, dwf_ref, dwp_ref):
        @pl.when(pl.program_id(0) == 0)
        def _():
            dwf_ref[...] = jnp.zeros_like(dwf_ref)
            dwp_ref[...] = jnp.zeros_like(dwp_ref)
        dx, dwn = _rms_bwd(h1_ref[...], wf_ref[...], dn2_ref[...])
        dh1 = dout_ref[...] + dx
        dh1_ref[...] = dh1
        dwf_ref[...] += _colsum(dwn)
        dmix, dwn2 = _rms_bwd(mix_ref[...], wp_ref[...], dh1)
        dmix_ref[...] = dmix.astype(BF16)
        dwp_ref[...] += _colsum(dwn2)

    row = pl.BlockSpec((T, D), lambda i: (i, 0))
    vec = pl.BlockSpec((1, D), lambda i: (0, 0))
    v = jax.ShapeDtypeStruct((1, D), F32)
    return pl.pallas_call(
        body, name="resid_bwd", grid=(LP // T,),
        in_specs=[row, row, vec, vec, row, row], out_specs=[row, row, vec, vec],
        out_shape=[jax.ShapeDtypeStruct((LP, D), F32), jax.ShapeDtypeStruct((LP, D), BF16), v, v],
        compiler_params=_cp(("arbitrary",)),
    )(h1, mix, w_post, w_pre, dout, dn2)


GELU_C = 0.7978845608028654
GELU_A = 0.044715


def _gelu_parts(x):
    t = jnp.tanh(GELU_C * (x + GELU_A * x * x * x))
    return 0.5 * x * (1.0 + t), t


def _convglu_fwd(up, conv_w, conv_b):
    LP, C = up.shape
    T = _tile(LP, 128)

    def body(x_ref, halo_ref, cw_ref, cb_ref, act_ref, y_ref):
        i = pl.program_id(0)

        def conv(cols):
            ext = jnp.concatenate([jnp.where(i > 0, halo_ref[:, cols], 0.0), x_ref[:, cols]], axis=0)
            w = cw_ref[:, cols]
            y = (w[2:3] * ext[8:] + w[1:2] * pltpu.roll(ext, 1, 0)[8:] + w[0:1] * pltpu.roll(ext, 2, 0)[8:]
                 + cb_ref[:, cols])
            y_ref[:, cols] = y.astype(BF16)
            return y

        for s in range(D_FF // LANE):
            gs = slice(s * LANE, (s + 1) * LANE)
            g, _ = _gelu_parts(conv(gs))
            act_ref[:, gs] = (g * conv(slice(D_FF + s * LANE, D_FF + (s + 1) * LANE))).astype(BF16)

    t8 = T // 8
    return pl.pallas_call(
        body, name="convglu_fwd", grid=(LP // T,),
        in_specs=[pl.BlockSpec((T, C), lambda i: (i, 0)),
                  pl.BlockSpec((8, C), lambda i: (jnp.maximum(i * t8 - 1, 0), 0)),
                  pl.BlockSpec((FFN_CONV, C), lambda i: (0, 0)), pl.BlockSpec((1, C), lambda i: (0, 0))],
        out_specs=[pl.BlockSpec((T, D_FF), lambda i: (i, 0)), pl.BlockSpec((T, C), lambda i: (i, 0))],
        out_shape=[jax.ShapeDtypeStruct((LP, D_FF), BF16), jax.ShapeDtypeStruct((LP, C), BF16)],
        compiler_params=_cp(("parallel",)),
    )(up, up, conv_w, conv_b)


def _convglu_bwd(up, y, conv_w, dact):
    LP, C = up.shape
    T = _tile(LP, 128)
    TE = T + 8
    nt = LP // T

    def body(x_ref, y_ref, yn_ref, cw_ref, da_ref, dan_ref, dx_ref, dcw_ref, dcb_ref):
        i = pl.program_id(0)

        @pl.when(i == 0)
        def _():
            dcw_ref[...] = jnp.zeros_like(dcw_ref)
            dcb_ref[...] = jnp.zeros_like(dcb_ref)

        last = i == nt - 1

        def back(cols, dy):
            w = cw_ref[:, cols]
            later = [dy[0:T], pltpu.roll(dy, TE - 1, 0)[0:T], pltpu.roll(dy, TE - 2, 0)[0:T]]
            x_t = x_ref[:, cols]
            dcb_ref[:, cols] += _colsum(later[0])
            for j in range(FFN_CONV):
                dcw_ref[j:j + 1, cols] += _colsum(later[FFN_CONV - 1 - j] * x_t)
            dx_ref[:, cols] = (w[2:3] * later[0] + w[1:2] * later[1] + w[0:1] * later[2]).astype(BF16)

        for s in range(D_FF // LANE):
            gs = slice(s * LANE, (s + 1) * LANE)
            vs = slice(D_FF + s * LANE, D_FF + (s + 1) * LANE)
            gate = jnp.concatenate([y_ref[:, gs].astype(F32), yn_ref[0:8, gs].astype(F32)], axis=0)
            val = jnp.concatenate([y_ref[:, vs].astype(F32), yn_ref[0:8, vs].astype(F32)], axis=0)
            g, t = _gelu_parts(gate)
            dg_dx = 0.5 * (1.0 + t) + 0.5 * gate * (1.0 - t * t) * GELU_C * (1.0 + 3.0 * GELU_A * gate * gate)
            da = jnp.concatenate([da_ref[:, gs], jnp.where(last, 0.0, dan_ref[:, gs])], axis=0)
            back(gs, da * val * dg_dx)
            back(vs, da * g)

    t8 = T // 8
    nb8 = LP // 8
    next8 = lambda w: pl.BlockSpec((8, w), lambda i: (jnp.minimum((i + 1) * t8, nb8 - 1), 0))
    row = lambda w: pl.BlockSpec((T, w), lambda i: (i, 0))
    small = lambda r: pl.BlockSpec((r, C), lambda i: (0, 0))
    return pl.pallas_call(
        body, name="convglu_bwd", grid=(nt,),
        in_specs=[row(C), row(C), pl.BlockSpec((16, C), lambda i: (jnp.minimum((i + 1) * (T // 16), LP // 16 - 1), 0)),
                  small(FFN_CONV), row(D_FF), next8(D_FF)],
        out_specs=[row(C), small(FFN_CONV), small(1)],
        out_shape=[jax.ShapeDtypeStruct((LP, C), BF16), jax.ShapeDtypeStruct((FFN_CONV, C), F32),
                   jax.ShapeDtypeStruct((1, C), F32)],
        compiler_params=_cp(("arbitrary",)),
    )(up, y, y, conv_w, dact, dact)


def _final(h1, f, w_post, target, n_real):
    LP, D = h1.shape
    T = _tile(LP, 256)

    def body(h1_ref, f_ref, w_ref, t_ref, loss_ref, dout_ref, df_ref, dw_ref):
        @pl.when(pl.program_id(0) == 0)
        def _():
            loss_ref[...] = jnp.zeros_like(loss_ref)
            dw_ref[...] = jnp.zeros_like(dw_ref)
        rows = pl.program_id(0) * T + _iota2((T, 1), 0)
        real = (rows >= ROW0) & (rows < ROW0 + n_real)
        f = f_ref[...]
        out = h1_ref[...] + f * _rms(f) * w_ref[...]
        err = jnp.where(real, out - t_ref[...], 0.0)
        loss_ref[...] += 0.5 * jnp.sum(_colsum(jnp.mean(err * err, axis=-1, keepdims=True)), axis=-1, keepdims=True)
        dout = err * (1.0 / D)
        dout_ref[...] = dout
        dx, dwn = _rms_bwd(f, w_ref[...], dout)
        df_ref[...] = dx.astype(BF16)
        dw_ref[...] += _colsum(dwn)

    row = pl.BlockSpec((T, D), lambda i: (i, 0))
    vec = pl.BlockSpec((1, D), lambda i: (0, 0))
    return pl.pallas_call(
        body, name="final_loss", grid=(LP // T,),
        in_specs=[row, row, vec, row],
        out_specs=[pl.BlockSpec((1, 128), lambda i: (0, 0)), row, row, vec],
        out_shape=[jax.ShapeDtypeStruct((1, 128), F32), jax.ShapeDtypeStruct((LP, D), F32),
                   jax.ShapeDtypeStruct((LP, D), BF16), jax.ShapeDtypeStruct((1, D), F32)],
        compiler_params=_cp(("arbitrary",)),
    )(h1, f, w_post, target)


ANY_SPEC = pl.BlockSpec(memory_space=pl.ANY)
N_CHIP = 4


def _other_chips(x, y):
    return [(1 - x, y), (x, 1 - y), (1 - x, 1 - y)]


def _gather_direct(arrs, name):
    n = len(arrs)
    npeer = N_DEV - 1

    def body(*refs):
        ins, outs = refs[:n], refs[n:2 * n]
        send_sems, recv_sems, loc_sems = refs[2 * n:]
        x, y, c = lax.axis_index("x"), lax.axis_index("y"), lax.axis_index("c")
        me = 4 * x + 2 * y + c
        copies = []
        for a in range(n):
            for kk in range(1, N_DEV):
                px = 1 - x if kk & 4 else x
                py = 1 - y if kk & 2 else y
                pc = 1 - c if kk & 1 else c
                s = a * npeer + kk - 1
                cp = pltpu.make_async_remote_copy(src_ref=ins[a], dst_ref=outs[a].at[me], send_sem=send_sems.at[s],
                                                  recv_sem=recv_sems.at[s], device_id=(px, py, pc), device_id_type=MESH)
                cp.start()
                copies.append(cp)
            own = pltpu.make_async_copy(ins[a], outs[a].at[me], loc_sems.at[a])
            own.start()
            copies.append(own)
        for cp in copies:
            cp.wait()

    shapes = [jax.ShapeDtypeStruct((N_DEV,) + tuple(a.shape), a.dtype) for a in arrs]
    return pl.pallas_call(
        body, name=name, in_specs=[ANY_SPEC] * n, out_specs=[ANY_SPEC] * n, out_shape=shapes,
        scratch_shapes=[pltpu.SemaphoreType.DMA((n * npeer,)), pltpu.SemaphoreType.DMA((n * npeer,)),
                        pltpu.SemaphoreType.DMA((n,))],
        compiler_params=pltpu.CompilerParams(has_side_effects=True),
    )(*arrs)


class _Exchange:
    def __init__(self, arrs, out_shapes, scratch, start, finish, mid=None):
        self.arrs, self.out_shapes, self.scratch = list(arrs), list(out_shapes), list(scratch)
        self.start, self.finish, self.mid = start, finish, mid

    @property
    def n(self):
        return len(self.arrs)


def _run_exchange(ex, name):
    n = ex.n

    def body(*refs):
        ins, outs, sems = refs[:n], refs[n:2 * n], refs[2 * n:]
        ex.start(ins, outs, sems)
        if ex.mid is not None:
            ex.mid(ins, outs, sems)
        ex.finish(ins, outs, sems)

    return pl.pallas_call(
        body, name=name, in_specs=[ANY_SPEC] * n, out_specs=[ANY_SPEC] * n, out_shape=ex.out_shapes,
        scratch_shapes=ex.scratch, compiler_params=pltpu.CompilerParams(has_side_effects=True),
    )(*ex.arrs)


def _carry_begin(ex, refs, step, nsteps):
    if ex is None:
        return

    @pl.when(step == 0)
    def _():
        ex.start(*refs)

    if ex.mid is not None:
        @pl.when(step == min(nsteps - 1, (3 * nsteps) // 5))
        def _():
            ex.mid(*refs)


def _carry_end(ex, refs, step, nsteps):
    if ex is None:
        return

    @pl.when(step == nsteps - 1)
    def _():
        ex.finish(*refs)


def _gather_two_level(arrs):
    n = len(arrs)
    K = 7

    def env(ins, outs, sems):
        send_sems, recv_sems, loc_sems = sems
        x, y, c = lax.axis_index("x"), lax.axis_index("y"), lax.axis_index("c")

        def cp(a, k, src, slot, to):
            return pltpu.make_async_remote_copy(src_ref=src, dst_ref=outs[a].at[slot], send_sem=send_sems.at[a * K + k],
                                                recv_sem=recv_sems.at[a * K + k], device_id=to, device_id_type=MESH)

        me = 4 * x + 2 * y + c
        owns = [pltpu.make_async_copy(ins[a], outs[a].at[me], loc_sems.at[a]) for a in range(n)]
        first = []
        for a in range(n):
            first.append(cp(a, 0, ins[a], me, (x, y, 1 - c)))
            first += [cp(a, 1 + j, ins[a], me, (px, py, c)) for j, (px, py) in enumerate(_other_chips(x, y))]
        passed = []
        for j, (px, py) in enumerate(_other_chips(x, y)):
            slot = 4 * px + 2 * py + c
            passed += [(cp(a, 1 + j, ins[a], slot, (px, py, c)), cp(a, 4 + j, outs[a].at[slot], slot, (x, y, 1 - c)))
                       for a in range(n)]
        from_sib = []
        for a in range(n):
            from_sib.append(cp(a, 0, ins[a], 4 * x + 2 * y + (1 - c), (x, y, 1 - c)))
            from_sib += [cp(a, 4 + j, ins[a], 4 * px + 2 * py + (1 - c), (x, y, 1 - c))
                         for j, (px, py) in enumerate(_other_chips(x, y))]
        return owns, first, passed, from_sib

    def start(ins, outs, sems):
        owns, first, _, _ = env(ins, outs, sems)
        for cp in owns + first:
            cp.start()

    def mid(ins, outs, sems):
        _, _, passed, _ = env(ins, outs, sems)
        for arrival, fwd in passed:
            arrival.wait_recv()
            fwd.start()

    def finish(ins, outs, sems):
        owns, first, passed, from_sib = env(ins, outs, sems)
        for cp in from_sib:
            cp.wait_recv()
        for cp in first + [fwd for _, fwd in passed]:
            cp.wait_send()
        for cp in owns:
            cp.wait()

    shapes = [jax.ShapeDtypeStruct((N_DEV,) + tuple(a.shape), a.dtype) for a in arrs]
    scratch = [pltpu.SemaphoreType.DMA((n * K,)), pltpu.SemaphoreType.DMA((n * K,)), pltpu.SemaphoreType.DMA((n,))]
    return _Exchange(arrs, shapes, scratch, start, finish, mid)


def _swap_sibling(arrs):
    n = len(arrs)

    def copies(ins, outs, sems):
        send_sems, recv_sems = sems
        x, y, c = lax.axis_index("x"), lax.axis_index("y"), lax.axis_index("c")
        return [pltpu.make_async_remote_copy(src_ref=ins[a], dst_ref=outs[a], send_sem=send_sems.at[a],
                                             recv_sem=recv_sems.at[a], device_id=(x, y, 1 - c), device_id_type=MESH)
                for a in range(n)]

    def start(ins, outs, sems):
        for cp in copies(ins, outs, sems):
            cp.start()

    def finish(ins, outs, sems):
        for cp in copies(ins, outs, sems):
            cp.wait()

    shapes = [jax.ShapeDtypeStruct(tuple(a.shape), a.dtype) for a in arrs]
    return _Exchange(arrs, shapes, [pltpu.SemaphoreType.DMA((n,)), pltpu.SemaphoreType.DMA((n,))], start, finish)


def _exchange_chips(arrs):
    n = len(arrs)
    K = N_CHIP - 1

    def copies(ins, outs, sems):
        send_sems, recv_sems, loc_sems = sems
        x, y, c = lax.axis_index("x"), lax.axis_index("y"), lax.axis_index("c")
        mine = 2 * x + y
        out = []
        for a in range(n):
            out += [pltpu.make_async_remote_copy(src_ref=ins[a].at[2 * px + py], dst_ref=outs[a].at[mine],
                                                 send_sem=send_sems.at[a * K + j], recv_sem=recv_sems.at[a * K + j],
                                                 device_id=(px, py, c), device_id_type=MESH)
                    for j, (px, py) in enumerate(_other_chips(x, y))]
            out.append(pltpu.make_async_copy(ins[a].at[mine], outs[a].at[mine], loc_sems.at[a]))
        return out

    def start(ins, outs, sems):
        for cp in copies(ins, outs, sems):
            cp.start()

    def finish(ins, outs, sems):
        for cp in copies(ins, outs, sems):
            cp.wait()

    shapes = [jax.ShapeDtypeStruct(tuple(a.shape), a.dtype) for a in arrs]
    scratch = [pltpu.SemaphoreType.DMA((n * K,)), pltpu.SemaphoreType.DMA((n * K,)), pltpu.SemaphoreType.DMA((n,))]
    return _Exchange(arrs, shapes, scratch, start, finish)


def _add_halves(mine, theirs, name):
    _, R, C = mine.shape
    cap = max(16, (ELEMWISE_VMEM // (4 * C * 10)) // 16 * 16)
    T = R if R <= cap else _tile(R, cap, 16)

    def body(a_ref, b_ref, o_ref):
        o_ref[...] = (a_ref[...] + b_ref[...].astype(F32)).astype(BF16)

    blk = pl.BlockSpec((N_CHIP, T, C), lambda i: (0, i, 0))
    return pl.pallas_call(
        body, name=name, grid=(R // T,), in_specs=[blk, blk], out_specs=blk,
        out_shape=jax.ShapeDtypeStruct(mine.shape, BF16), compiler_params=_cp(("parallel",)),
    )(mine, theirs)


def _adamw(parts, w, m, v, name):
    R, C = w.shape
    npart = parts.shape[0]
    cap = max(16, (ELEMWISE_VMEM // (4 * C * 12)) // 16 * 16)
    T = R if R <= cap else _tile(R, cap, 16)

    def body(p_ref, w_ref, m_ref, v_ref, g_ref, d_ref, nm_ref, nv_ref):
        g = p_ref[0].astype(F32)
        for k in range(1, npart):
            g = g + p_ref[k].astype(F32)
        mm = ADAM_B1 * m_ref[...] + (1.0 - ADAM_B1) * g
        vv = ADAM_B2 * v_ref[...] + (1.0 - ADAM_B2) * (g * g)
        m_hat = mm / (1.0 - ADAM_B1 ** ADAM_STEP)
        v_hat = vv / (1.0 - ADAM_B2 ** ADAM_STEP)
        g_ref[...] = g
        d_ref[...] = -ADAM_LR * (m_hat / (jnp.sqrt(v_hat) + ADAM_EPS) + ADAM_WD * w_ref[...])
        nm_ref[...] = mm
        nv_ref[...] = vv

    row = pl.BlockSpec((T, C), lambda i: (i, 0))
    out = jax.ShapeDtypeStruct((R, C), F32)
    return pl.pallas_call(
        body, name=name, grid=(R // T,),
        in_specs=[pl.BlockSpec((npart, T, C), lambda i: (0, i, 0)), row, row, row],
        out_specs=[row] * 4, out_shape=[out] * 4,
        compiler_params=_cp(("parallel",)),
    )(parts, w, m, v)


SMALL = ("attn_pre_norm", "gdn_A_log", "gdn_dt_bias", "gdn_norm_w", "sb_norm_w", "attn_post_norm",
         "ffn_pre_norm", "ffn_conv_b", "ffn_post_norm")


def _pack_small(arrs):
    rows = []
    for a in arrs:
        flat = a.reshape(-1).astype(F32)
        n = -(-flat.shape[0] // 128) * 128
        rows.append(jnp.pad(flat, (0, n - flat.shape[0])).reshape(-1, 128))
    slab = jnp.concatenate(rows, axis=0)
    pad = (-slab.shape[0]) % 8
    return jnp.pad(slab, ((0, pad), (0, 0)))


def _unpack_small(slab, shapes):
    out, r = [], 0
    for shp in shapes:
        size = 1
        for s in shp:
            size *= s
        nr = -(-size // 128)
        out.append(slab[r:r + nr].reshape(-1)[:size].reshape(shp))
        r += nr
    return out


def _to_blocks_cols(a):
    R, C = a.shape
    return a.reshape(R, N_DEV, C // N_DEV).transpose(1, 0, 2)


def _from_blocks_cols(a):
    n, R, c = a.shape
    return a.transpose(1, 0, 2).reshape(R, n * c)


def kernel(x, meta_tokens, attn_pre_norm, w_in, gdn_conv_w, gdn_A_log, gdn_dt_bias, gdn_norm_w, sb_norm_w, w_out, attn_post_norm, ffn_pre_norm, w_ffn_up, ffn_conv_w, ffn_conv_b, w_ffn_down, ffn_post_norm, loss_target, m_meta_tokens, m_attn_pre_norm, m_w_in, m_gdn_conv_w, m_gdn_A_log, m_gdn_dt_bias, m_gdn_norm_w, m_sb_norm_w, m_w_out, m_attn_post_norm, m_ffn_pre_norm, m_w_ffn_up, m_ffn_conv_w, m_ffn_conv_b, m_w_ffn_down, m_ffn_post_norm, v_meta_tokens, v_attn_pre_norm, v_w_in, v_gdn_conv_w, v_gdn_A_log, v_gdn_dt_bias, v_gdn_norm_w, v_sb_norm_w, v_w_out, v_attn_post_norm, v_ffn_pre_norm, v_w_ffn_up, v_ffn_conv_w, v_ffn_conv_b, v_w_ffn_down, v_ffn_post_norm):
    args = dict(locals())
    seq = x.shape[1]
    LP = -(-(ROW0 + seq) // LP_ALIGN) * LP_ALIGN
    tail = LP - ROW0 - seq

    meta_f = _from_blocks_cols(_run_exchange(_gather_two_level([meta_tokens]), "gather_meta")[0])

    (h0, target), got = _build_rows(x[0], meta_f, loss_target[0], LP,
                                    carry=_gather_two_level([w_in[0].astype(BF16), gdn_conv_w[0]]))
    (u,), _ = _prenorm_fwd(h0, attn_pre_norm)
    win_o = _from_blocks_cols(got[0])
    o_ab = C_QKV
    o_z = o_ab + 2 * GDN_HEADS
    w_inp = jnp.concatenate([win_o[:, :C_QKV], win_o[:, o_z:o_z + C_Z], win_o[:, o_z + C_Z:],
                             win_o[:, o_ab:o_z], jnp.zeros((D_MODEL, C_AB - 2 * GDN_HEADS), BF16)], axis=1)
    gconv_f = _from_blocks_cols(got[1])
    proj = _mm(u, w_inp, F32, "mm_in")
    (qn, kn, vg, beta_b, g_b), got = _gdn_pre_fwd(
        proj, gconv_f, gdn_A_log, gdn_dt_bias,
        carry=_gather_two_level([w_out[0].astype(BF16), w_ffn_down[0].astype(BF16)]))
    w_out_f = got[0].reshape(D_MODEL, D_MODEL)
    w_down_f = got[1].reshape(D_FF, D_MODEL)
    (cu, cw, cqd, ckd, cqk, ct, cgl), got = _gdn_chunk_fwd(
        qn, kn, vg, beta_b, g_b, carry=_gather_two_level([w_ffn_up[0].astype(BF16), ffn_conv_w[0]]))
    w_up_f = _from_blocks_cols(got[0])
    fconv_f = _from_blocks_cols(got[1])
    og, ssave = _gdn_scan_fwd(cu, cw, cqd, ckd, cqk, cgl)
    osb, ctot, sb_nrun = _sb_fwd(proj)
    snw = sb_norm_w.reshape(1, SB_HEADS * SB_DH)
    y = _attn_norm_fwd(og, proj, osb, gdn_norm_w, snw)
    mix = _mm(y, w_out_f, F32, "mm_out")
    h1, n2 = _resid_fwd(h0, mix, attn_post_norm, ffn_pre_norm)
    up = _mm(n2, w_up_f, F32, "mm_up")
    act, conv_y = _convglu_fwd(up, fconv_f, ffn_conv_b)
    f = _mm(act, w_down_f, F32, "mm_down")
    loss_part, dout, df, d_fpost = _final(h1, f, ffn_post_norm, target, seq)

    d_wdown = _mm_tn(act, df, "mm_dw_down")
    dact = _mm_nt(df, w_down_f, F32, "mm_dact")
    dup, d_fconv, d_fconvb = _convglu_bwd(up, conv_y, fconv_f, dact)
    d_wup = _mm_tn(n2, dup, "mm_dw_up")
    dn2 = _mm_nt(dup, w_up_f, F32, "mm_dn2")
    dh1, dmix, d_fpre, d_apost = _resid_bwd(h1, mix, attn_post_norm, ffn_pre_norm, dout, dn2)
    d_wout = _mm_tn(y, dmix, "mm_dw_out")
    dy = _mm_nt(dmix, w_out_f, F32, "mm_dy")
    my_c = lax.axis_index("c")

    def core_halves(blocks):
        halves = [s.reshape((N_CHIP, 2) + s.shape[1:]) for s in blocks]
        return ([lax.dynamic_index_in_dim(h, my_c, axis=1, keepdims=False) for h in halves],
                [lax.dynamic_index_in_dim(h, 1 - my_c, axis=1, keepdims=False).astype(BF16) for h in halves])

    early_names = ("w_out", "w_ffn_up", "w_ffn_down", "ffn_conv_w")
    e_mine, e_send = core_halves([d_wout.reshape(N_DEV, D_MODEL // N_DEV, D_MODEL), _to_blocks_cols(d_wup),
                                  d_wdown.reshape(N_DEV, D_FF // N_DEV, D_MODEL), _to_blocks_cols(d_fconv)])
    (dog, dz, dos, d_gnw, d_snw), e_theirs = _attn_norm_bwd(og, proj, osb, gdn_norm_w, snw, dy,
                                                            carry=_swap_sibling(e_send))
    e_sums = [_add_halves(a, b, "grads_add_" + nm) for nm, a, b in zip(early_names, e_mine, e_theirs)]
    dqs, dks, dvs = _sb_bwd(proj, ctot, sb_nrun, dos)
    (du_, dw_, dqd_, dkd_, dqk_, dgl_), _ = _gdn_scan_bwd(cu, cw, cqd, ckd, cqk, cgl, ssave, dog)
    dqn, dkn, dvg, dbeta, dg = _gdn_chunk_bwd(qn, kn, vg, beta_b, g_b, ct, du_, dw_, dqd_, dkd_, dqk_, dgl_)
    (dqkv, dab, d_gconv, d_gsc), e_recv = _gdn_pre_bwd(proj, gconv_f, gdn_A_log, gdn_dt_bias, dqn, dkn, dvg, dbeta, dg,
                                                       carry=_exchange_chips(e_sums))
    dpieces = [dqkv, dz, dqs, dks, dvs, dab]
    doffs = [0, OFF_Z, OFF_SB, OFF_SB + 512, OFF_SB + 1024, OFF_AB]
    dw_qkv, dw_ab = _mm_tn_pieces(u, [dqkv, dab], "mm_dw_in_gdn")
    dw_z, dw_qs, dw_ks, dw_vs = _mm_tn_pieces(u, [dz, dqs, dks, dvs], "mm_dw_in_rest")
    du0 = _mm_nt_pieces(dpieces, doffs, w_inp, F32, "mm_du")
    d_win = jnp.concatenate([dw_qkv, dw_ab[:, :2 * GDN_HEADS], dw_z, dw_qs, dw_ks, dw_vs], axis=1)
    late_names = ("w_in", "gdn_conv_w")
    l_mine, l_send = core_halves([_to_blocks_cols(d_win), _to_blocks_cols(d_gconv)])
    l_theirs = _run_exchange(_swap_sibling(l_send), "grads_swap_sibling")
    l_sums = [_add_halves(a, b, "grads_add_" + nm) for nm, a, b in zip(late_names, l_mine, l_theirs)]
    (dh0, d_apre), l_recv = _prenorm_bwd(h0, attn_pre_norm, du0, dh1, carry=_exchange_chips(l_sums))
    grad_x = dh0[ROW0:ROW0 + seq][None]
    d_meta = dh0[PAD_ROWS:ROW0]

    small_grads = [d_apre, d_gsc[0:1, :GDN_HEADS], d_gsc[1:2, :GDN_HEADS], d_gnw, d_snw.reshape(1, SB_HEADS, SB_DH),
                   d_apost, d_fpre, d_fconvb, d_fpost]
    loss_rows = jnp.pad(loss_part, ((0, 7), (0, 0)))
    n_param_rows = _pack_small(small_grads).shape[0]
    n_small_rows = n_param_rows + loss_rows.shape[0]
    slab_parts = _gather_direct(
        [jnp.concatenate([_pack_small(small_grads), loss_rows, d_meta.reshape(-1, LANE)], axis=0)],
        name="gather_small_grads")[0]
    me = 4 * lax.axis_index("x") + 2 * lax.axis_index("y") + my_c
    meta_parts = lax.dynamic_index_in_dim(
        slab_parts[:, n_small_rows:].reshape(N_DEV, N_META, N_DEV, LANE), me, axis=2, keepdims=False)
    slab_parts = slab_parts[:, :n_small_rows]

    res = {}
    for nm, parts in zip(early_names + late_names + ("meta_tokens",), list(e_recv) + list(l_recv) + [meta_parts]):
        wloc = args[nm]
        shp = wloc.shape
        w2 = wloc.reshape(shp[-2], shp[-1])
        outs = _adamw(parts, w2, args["m_" + nm].reshape(w2.shape), args["v_" + nm].reshape(w2.shape), "adamw_" + nm)
        res[nm] = [o.reshape(shp) for o in outs]
    small_shapes = [args[nm].shape for nm in SMALL]
    with_loss_rows = lambda slab: jnp.pad(slab, ((0, n_small_rows - n_param_rows), (0, 0)))
    outs = _adamw(slab_parts, with_loss_rows(_pack_small([args[nm] for nm in SMALL])),
                  with_loss_rows(_pack_small([args["m_" + nm] for nm in SMALL])),
                  with_loss_rows(_pack_small([args["v_" + nm] for nm in SMALL])), "adamw_small")
    loss = outs[0][n_param_rows, 0]
    for k in range(4):
        for nm, val in zip(SMALL, _unpack_small(outs[k], small_shapes)):
            res.setdefault(nm, [None] * 4)[k] = val

    order = ("meta_tokens", "attn_pre_norm", "w_in", "gdn_conv_w", "gdn_A_log", "gdn_dt_bias", "gdn_norm_w",
             "sb_norm_w", "w_out", "attn_post_norm", "ffn_pre_norm", "w_ffn_up", "ffn_conv_w", "ffn_conv_b",
             "w_ffn_down", "ffn_post_norm")
    return (loss, grad_x, *[res[nm][0] for nm in order], *[res[nm][1] for nm in order],
            *[res[nm][2] for nm in order], *[res[nm][3] for nm in order])
```

```python
import functools

import jax
import jax.numpy as jnp
from jax import lax
from jax.experimental import pallas as pl
from jax.experimental.pallas import tpu as pltpu

F32 = jnp.float32
BF16 = jnp.bfloat16

D_MODEL = 1024
N_META = 16
GDN_HEADS = 4
GDN_D = 128
GDN_CHUNK = 64
GDN_CONV = 4
GDN_ROWS = 256
SCAN_CHUNKS = 4
SB_HEADS = 8
SB_DH = 64
SB_BLOCK = 128
D_FF = 2816
FFN_CONV = 3
NORM_EPS = 1e-6
L2_EPS = 1e-6
LANE = 128
N_DEV = 8

PAD_ROWS = SB_BLOCK - N_META
ROW0 = SB_BLOCK
SB_SPAN = 512
SB_DEAD = -104.0
SB_SUB = 256
SB_QTILE = 256
LP_ALIGN = 256

C_QKV = 3 * GDN_HEADS * GDN_D
C_Z = GDN_HEADS * GDN_D
C_SB = 3 * SB_HEADS * SB_DH
C_AB = 256
OFF_Z = C_QKV
OFF_SB = OFF_Z + C_Z
OFF_AB = OFF_SB + C_SB
D_INP = OFF_AB + C_AB
D_IN = C_QKV + 2 * GDN_HEADS + C_Z + C_SB

ADAM_LR = 0.001
ADAM_B1 = 0.9
ADAM_B2 = 0.999
ADAM_EPS = 1e-08
ADAM_WD = 0.01
ADAM_STEP = 10

VMEM_LIMIT = 56 * 1024 * 1024
ELEMWISE_VMEM = 8 * 1024 * 1024
MESH = pl.DeviceIdType.MESH


def _cp(sem=None):
    kw = dict(vmem_limit_bytes=VMEM_LIMIT)
    if sem is not None:
        kw["dimension_semantics"] = sem
    return pltpu.CompilerParams(**kw)


def _tile(n, cap, unit=128):
    best = None
    t = unit
    while t <= min(n, cap):
        if n % t == 0:
            best = t
        t += unit
    assert best is not None, (n, cap, unit)
    return best


def _dot(a, b):
    return jnp.dot(a, b, preferred_element_type=F32)


def _dot_nt(a, b):
    return lax.dot_general(a, b, (((1,), (1,)), ((), ())), preferred_element_type=F32)


def _dot_tn(a, b):
    return lax.dot_general(a, b, (((0,), (0,)), ((), ())), preferred_element_type=F32)


def _split(x):
    hi = x.astype(BF16)
    lo = (x - hi.astype(F32)).astype(BF16)
    return hi, lo


def _dot1(a, b, f=_dot):
    return f(a.astype(BF16), b.astype(BF16))


def _dot3(a, b, f=_dot):
    ah, al = _split(a)
    bh, bl = _split(b)
    return f(ah, bh) + (f(ah, bl) + f(al, bh))


def _dot_exact_l(m_bf16, x, f=_dot):
    xh, xl = _split(x)
    return f(m_bf16, xh) + f(m_bf16, xl)


def _dot_exact_r(x, m_bf16, f=_dot):
    xh, xl = _split(x)
    return f(xh, m_bf16) + f(xl, m_bf16)


def _iota2(shape, dim):
    return lax.broadcasted_iota(jnp.int32, shape, dim)


def _sigmoid(x):
    return 0.5 * jnp.tanh(0.5 * x) + 0.5


def _softplus(x):
    return jnp.maximum(x, 0.0) + jnp.log(1.0 + jnp.exp(-jnp.abs(x)))


def _colsum(x):
    return jnp.sum(x, axis=0, keepdims=True)


def _rowsum(x):
    return jnp.sum(x, axis=-1, keepdims=True)


def _mm(a, b, out_dtype, name):
    M, K = a.shape
    K2, N = b.shape
    assert K == K2
    tm = _tile(M, 768)
    tn = _tile(N, max(128, (6 * 1024 * 1024) // (2 * K)))

    def body(a_ref, b_ref, o_ref):
        o_ref[...] = _dot(a_ref[...].astype(BF16), b_ref[...].astype(BF16)).astype(o_ref.dtype)

    return pl.pallas_call(
        body, name=name, grid=(N // tn, M // tm),
        in_specs=[pl.BlockSpec((tm, K), lambda j, i: (i, 0)), pl.BlockSpec((K, tn), lambda j, i: (0, j))],
        out_specs=pl.BlockSpec((tm, tn), lambda j, i: (i, j)),
        out_shape=jax.ShapeDtypeStruct((M, N), out_dtype),
        compiler_params=_cp(("parallel", "parallel")),
    )(a, b)


def _mm_nt(a, b, out_dtype, name):
    M, K = a.shape
    N, K2 = b.shape
    assert K == K2
    tm = _tile(M, 768)
    tn = _tile(N, max(128, (6 * 1024 * 1024) // (2 * K)))

    def body(a_ref, b_ref, o_ref):
        o_ref[...] = _dot_nt(a_ref[...].astype(BF16), b_ref[...].astype(BF16)).astype(o_ref.dtype)

    return pl.pallas_call(
        body, name=name, grid=(N // tn, M // tm),
        in_specs=[pl.BlockSpec((tm, K), lambda j, i: (i, 0)), pl.BlockSpec((tn, K), lambda j, i: (j, 0))],
        out_specs=pl.BlockSpec((tm, tn), lambda j, i: (i, j)),
        out_shape=jax.ShapeDtypeStruct((M, N), out_dtype),
        compiler_params=_cp(("parallel", "parallel")),
    )(a, b)


def _mm_nt_pieces(pieces, offsets, b, out_dtype, name):
    M = pieces[0].shape[0]
    N = b.shape[0]
    n = len(pieces)
    widths = [p.shape[1] for p in pieces]
    assert all(off % k == 0 for off, k in zip(offsets, widths))
    tm = _tile(M, 768)
    tn = _tile(N, 1024)

    def body(*refs):
        acc = _dot_nt(refs[0][...].astype(BF16), refs[n][...].astype(BF16))
        for p in range(1, n):
            acc = acc + _dot_nt(refs[p][...].astype(BF16), refs[n + p][...].astype(BF16))
        refs[2 * n][...] = acc.astype(out_dtype)

    return pl.pallas_call(
        body, name=name, grid=(N // tn, M // tm),
        in_specs=[pl.BlockSpec((tm, k), lambda j, i: (i, 0)) for k in widths]
        + [pl.BlockSpec((tn, k), functools.partial(lambda j, i, blk: (j, blk), blk=off // k))
           for off, k in zip(offsets, widths)],
        out_specs=pl.BlockSpec((tm, tn), lambda j, i: (i, j)),
        out_shape=jax.ShapeDtypeStruct((M, N), out_dtype),
        compiler_params=_cp(("parallel", "parallel")),
    )(*pieces, *([b] * n))


def _mm_tn_pieces(a, pieces, name):
    M, K = a.shape
    n = len(pieces)
    tm = _tile(M, 768)

    def body(*refs):
        @pl.when(pl.program_id(0) == 0)
        def _():
            for p in range(n):
                refs[1 + n + p][...] = jnp.zeros_like(refs[1 + n + p])
        at = refs[0][...].astype(BF16)
        for p in range(n):
            refs[1 + n + p][...] += _dot_tn(at, refs[1 + p][...].astype(BF16))

    return pl.pallas_call(
        body, name=name, grid=(M // tm,),
        in_specs=[pl.BlockSpec((tm, K), lambda m: (m, 0))] + [pl.BlockSpec((tm, p.shape[1]), lambda m: (m, 0)) for p in pieces],
        out_specs=[pl.BlockSpec((K, p.shape[1]), lambda m: (0, 0)) for p in pieces],
        out_shape=[jax.ShapeDtypeStruct((K, p.shape[1]), F32) for p in pieces],
        compiler_params=_cp(("arbitrary",)),
    )(a, *pieces)


def _mm_tn(a, b, name):
    M, K = a.shape
    M2, N = b.shape
    assert M == M2
    tm = _tile(M, 768)
    tk = _tile(K, 2816)
    tn = _tile(N, 2816)

    def body(a_ref, b_ref, o_ref):
        @pl.when(pl.program_id(2) == 0)
        def _():
            o_ref[...] = jnp.zeros_like(o_ref)
        o_ref[...] += _dot_tn(a_ref[...].astype(BF16), b_ref[...].astype(BF16))

    return pl.pallas_call(
        body, name=name, grid=(K // tk, N // tn, M // tm),
        in_specs=[pl.BlockSpec((tm, tk), lambda i, j, m: (m, i)), pl.BlockSpec((tm, tn), lambda i, j, m: (m, j))],
        out_specs=pl.BlockSpec((tk, tn), lambda i, j, m: (i, j)),
        out_shape=jax.ShapeDtypeStruct((K, N), F32),
        compiler_params=_cp(("parallel", "parallel", "arbitrary")),
    )(a, b)


def _rms(x):
    return lax.rsqrt(jnp.mean(x * x, axis=-1, keepdims=True) + NORM_EPS)


def _rms_bwd(x, w, dy):
    r = _rms(x)
    n = x * r
    dyw = dy * w
    dx = r * (dyw - n * jnp.mean(dyw * n, axis=-1, keepdims=True))
    return dx, dy * n


def _build_rows(x, meta, target, LP, carry=None):
    seq, D = x.shape
    T = SB_BLOCK
    nx = seq // T
    assert seq % T == 0 and meta.shape[0] == N_META

    def body(x_ref, m_ref, t_ref, h_ref, tp_ref):
        i = pl.program_id(0)
        inside = (i >= 1) & (i <= nx)
        head = jnp.concatenate([jnp.zeros((PAD_ROWS, D), F32), m_ref[...]], axis=0)
        h_ref[...] = jnp.where(i == 0, head, jnp.where(inside, x_ref[...], 0.0))
        tp_ref[...] = jnp.where(inside, t_ref[...], 0.0)

    tok = pl.BlockSpec((T, D), lambda i: (jnp.clip(i - 1, 0, nx - 1), 0))
    row = pl.BlockSpec((T, D), lambda i: (i, 0))
    out = jax.ShapeDtypeStruct((LP, D), F32)
    return _call_carrying(
        carry, body, LP // T, name="build_rows",
        in_specs=[tok, pl.BlockSpec((N_META, D), lambda i: (0, 0)), tok], out_specs=[row, row], out_shape=[out, out],
        operands=(x, meta, target))


def _prenorm_fwd(h0, w, carry=None):
    LP, D = h0.shape
    T = _tile(LP, 512)

    def body(h_ref, w_ref, u_ref):
        h = h_ref[...]
        u_ref[...] = (h * _rms(h) * w_ref[...]).astype(BF16)

    return _call_carrying(
        carry, body, LP // T, name="prenorm_fwd",
        in_specs=[pl.BlockSpec((T, D), lambda i: (i, 0)), pl.BlockSpec((1, D), lambda i: (0, 0))],
        out_specs=[pl.BlockSpec((T, D), lambda i: (i, 0))],
        out_shape=[jax.ShapeDtypeStruct((LP, D), BF16)],
        operands=(h0, w))


def _prenorm_bwd(h0, w, du, dh1, carry=None):
    LP, D = h0.shape
    T = _tile(LP, 512)

    def body(h_ref, w_ref, du_ref, dh1_ref, dh0_ref, dw_ref):
        @pl.when(pl.program_id(0) == 0)
        def _():
            dw_ref[...] = jnp.zeros_like(dw_ref)
        dx, dwn = _rms_bwd(h_ref[...], w_ref[...], du_ref[...])
        dh0_ref[...] = dh1_ref[...] + dx
        dw_ref[...] += _colsum(dwn)

    row = pl.BlockSpec((T, D), lambda i: (i, 0))
    vec = pl.BlockSpec((1, D), lambda i: (0, 0))
    return _call_carrying(
        carry, body, LP // T, name="prenorm_bwd",
        in_specs=[row, vec, row, row], out_specs=[row, vec],
        out_shape=[jax.ShapeDtypeStruct((LP, D), F32), jax.ShapeDtypeStruct((1, D), F32)],
        operands=(h0, w, du, dh1))


def _gdn_gate_consts(alog_ref, dtb_ref, h):
    a_coef = -jnp.exp(alog_ref[0:1, h:h + 1])
    return a_coef, dtb_ref[0:1, h:h + 1]


def _gdn_pre_fwd(proj, conv_w, a_log, dt_bias, carry=None):
    LP = proj.shape[0]
    T = _tile(LP, 256)
    C = C_QKV
    H = GDN_HEADS

    def body(x_ref, halo_ref, ab_ref, cw_ref, alog_ref, dtb_ref, q_ref, k_ref, v_ref, beta_ref, g_ref):
        i = pl.program_id(0)

        def conv_silu(cols):
            ext = jnp.concatenate([jnp.where(i > 0, halo_ref[:, cols], 0.0), x_ref[:, cols]], axis=0)
            w = cw_ref[:, cols]
            y = w[GDN_CONV - 1:GDN_CONV] * ext[8:]
            for j in range(GDN_CONV - 1):
                y = y + w[j:j + 1] * pltpu.roll(ext, GDN_CONV - 1 - j, 0)[8:]
            return y * _sigmoid(y)

        for h in range(H):
            sl = slice(h * GDN_D, (h + 1) * GDN_D)
            cq = conv_silu(sl)
            q_ref[:, sl] = cq * lax.rsqrt(_rowsum(cq * cq) + L2_EPS) * (GDN_D ** -0.5)
            ck = conv_silu(slice(512 + h * GDN_D, 512 + (h + 1) * GDN_D))
            k_ref[:, sl] = ck * lax.rsqrt(_rowsum(ck * ck) + L2_EPS)
            v_ref[:, sl] = conv_silu(slice(1024 + h * GDN_D, 1024 + (h + 1) * GDN_D))
        ab = ab_ref[...]
        valid = (i * T + _iota2((T, 1), 0)) >= PAD_ROWS
        for h in range(H):
            sl = slice(h * GDN_D, (h + 1) * GDN_D)
            a_coef, dtb = _gdn_gate_consts(alog_ref, dtb_ref, h)
            g = jnp.where(valid, a_coef * _softplus(ab[:, h:h + 1] + dtb), 0.0)
            beta = jnp.where(valid, _sigmoid(ab[:, H + h:H + h + 1]), 0.0)
            g_ref[:, sl] = jnp.broadcast_to(g, (T, GDN_D))
            beta_ref[:, sl] = jnp.broadcast_to(beta, (T, GDN_D))

    t8 = T // 8
    row512 = pl.BlockSpec((T, 512), lambda i: (i, 0))
    small = lambda r, c: pl.BlockSpec((r, c), lambda i: (0, 0))
    out = jax.ShapeDtypeStruct((LP, 512), F32)
    return _call_carrying(
        carry, body, LP // T, name="gdn_pre_fwd",
        in_specs=[pl.BlockSpec((T, C), lambda i: (i, 0)),
                  pl.BlockSpec((8, C), lambda i: (jnp.maximum(i * t8 - 1, 0), 0)),
                  pl.BlockSpec((T, C_AB), lambda i: (i, OFF_AB // C_AB)),
                  small(GDN_CONV, C), small(1, H), small(1, H)],
        out_specs=[row512] * 5, out_shape=[out] * 5,
        operands=(proj, proj, proj, conv_w, a_log, dt_bias))


def _gdn_pre_bwd(proj, conv_w, a_log, dt_bias, dq, dk, dv, dbeta, dg, carry=None):
    LP = proj.shape[0]
    T = _tile(LP, 256)
    C = C_QKV
    H = GDN_HEADS
    TE = T + 8
    nt = LP // T

    def body(x_ref, xp_ref, xn_ref, ab_ref, cw_ref, alog_ref, dtb_ref,
             dq_ref, dqn_ref, dk_ref, dkn_ref, dv_ref, dvn_ref, dbeta_ref, dg_ref,
             dx_ref, dab_ref, dcw_ref, dsc_ref):
        i = pl.program_id(0)

        @pl.when(i == 0)
        def _():
            dcw_ref[...] = jnp.zeros_like(dcw_ref)
            dsc_ref[...] = jnp.zeros_like(dsc_ref)

        last = i == nt - 1

        def strip(cols, d_ref, dn_ref, dcols, scale):
            ext = jnp.concatenate([jnp.where(i > 0, xp_ref[:, cols], 0.0), x_ref[:, cols],
                                   jnp.where(last, 0.0, xn_ref[:, cols])], axis=0)
            sh = [ext[8:8 + TE]] + [pltpu.roll(ext, s, 0)[8:8 + TE] for s in range(1, GDN_CONV)]
            w = cw_ref[:, cols]
            y = w[GDN_CONV - 1:GDN_CONV] * sh[0]
            for j in range(GDN_CONV - 1):
                y = y + w[j:j + 1] * sh[GDN_CONV - 1 - j]
            sg = _sigmoid(y)
            d = jnp.concatenate([d_ref[:, dcols], jnp.where(last, 0.0, dn_ref[:, dcols])], axis=0)
            if scale is not None:
                c = y * sg
                r = lax.rsqrt(_rowsum(c * c) + L2_EPS)
                n = c * r
                d = scale * r * (d - n * _rowsum(d * n))
            dy = d * (sg * (1.0 + y * (1.0 - sg)))
            dy_t = dy[0:T]
            for j in range(GDN_CONV):
                dcw_ref[j:j + 1, cols] += _colsum(dy_t * sh[GDN_CONV - 1 - j][0:T])
            dx = w[GDN_CONV - 1:GDN_CONV] * dy_t
            for j in range(GDN_CONV - 1):
                dx = dx + w[j:j + 1] * pltpu.roll(dy, TE - (GDN_CONV - 1 - j), 0)[0:T]
            dx_ref[:, cols] = dx.astype(BF16)

        for h in range(H):
            sl = slice(h * GDN_D, (h + 1) * GDN_D)
            strip(sl, dq_ref, dqn_ref, sl, GDN_D ** -0.5)
            strip(slice(512 + h * GDN_D, 512 + (h + 1) * GDN_D), dk_ref, dkn_ref, sl, 1.0)
            strip(slice(1024 + h * GDN_D, 1024 + (h + 1) * GDN_D), dv_ref, dvn_ref, sl, None)
        ab = ab_ref[...]
        valid = (i * T + _iota2((T, 1), 0)) >= PAD_ROWS
        lane = _iota2((T, C_AB), 1)
        lane1 = _iota2((1, 128), 1)
        dab = jnp.zeros((T, C_AB), F32)
        dsc_a = jnp.zeros((1, 128), F32)
        dsc_d = jnp.zeros((1, 128), F32)
        for h in range(H):
            a_coef, dtb = _gdn_gate_consts(alog_ref, dtb_ref, h)
            pre = ab[:, h:h + 1] + dtb
            dgh = jnp.where(valid, dg_ref[:, h * GDN_D:h * GDN_D + 1], 0.0)
            da = dgh * a_coef * _sigmoid(pre)
            beta = _sigmoid(ab[:, H + h:H + h + 1])
            db = jnp.where(valid, dbeta_ref[:, h * GDN_D:h * GDN_D + 1], 0.0) * beta * (1.0 - beta)
            dab = dab + jnp.where(lane == h, da, 0.0) + jnp.where(lane == H + h, db, 0.0)
            dsc_a = dsc_a + jnp.where(lane1 == h, _colsum(dgh * a_coef * _softplus(pre)), 0.0)
            dsc_d = dsc_d + jnp.where(lane1 == h, _colsum(da), 0.0)
        dab_ref[...] = dab.astype(BF16)
        dsc_ref[0:1, :] += dsc_a
        dsc_ref[1:2, :] += dsc_d

    t8 = T // 8
    nb8 = LP // 8
    prev8 = lambda w: pl.BlockSpec((8, w), lambda i: (jnp.maximum(i * t8 - 1, 0), 0))
    next8 = lambda w: pl.BlockSpec((8, w), lambda i: (jnp.minimum((i + 1) * t8, nb8 - 1), 0))
    row = lambda w: pl.BlockSpec((T, w), lambda i: (i, 0))
    small = lambda r, c: pl.BlockSpec((r, c), lambda i: (0, 0))
    return _call_carrying(
        carry, body, nt, name="gdn_pre_bwd",
        in_specs=[row(C), prev8(C), next8(C), pl.BlockSpec((T, C_AB), lambda i: (i, OFF_AB // C_AB)),
                  small(GDN_CONV, C), small(1, H), small(1, H),
                  row(512), next8(512), row(512), next8(512), row(512), next8(512), row(512), row(512)],
        out_specs=[row(C), row(C_AB), small(GDN_CONV, C), small(2, 128)],
        out_shape=[jax.ShapeDtypeStruct((LP, C), BF16), jax.ShapeDtypeStruct((LP, C_AB), BF16),
                   jax.ShapeDtypeStruct((GDN_CONV, C), F32), jax.ShapeDtypeStruct((2, 128), F32)],
        operands=(proj, proj, proj, proj, conv_w, a_log, dt_bias, dq, dq, dk, dk, dv, dv, dbeta, dg))


def _tri_masks():
    r = _iota2((GDN_CHUNK, GDN_CHUNK), 0)
    c = _iota2((GDN_CHUNK, GDN_CHUNK), 1)
    return r >= c, r > c


def _gdn_chunk_common(q, k, v, beta, gb):
    incl, strict = _tri_masks()
    l_incl = incl.astype(BF16)
    gd = _dot_exact_l(l_incl, jnp.where(strict, gb[:, :GDN_CHUNK], 0.0))
    gc = _dot_exact_l(l_incl, gb)
    decay = jnp.where(incl, jnp.exp(jnp.where(incl, gd, 0.0)), 0.0)
    exp_g = jnp.exp(gc)
    g_last = gc[GDN_CHUNK - 1:GDN_CHUNK, :]
    kd_fac = jnp.exp(g_last - gc)
    gl = jnp.exp(g_last)
    kb = k * beta
    kk = _dot1(kb, k, _dot_nt)
    return dict(incl=incl, strict=strict, decay=decay, exp_g=exp_g, kd_fac=kd_fac, gl=gl, kb=kb, kk=kk,
                vb=v * beta, kbg=kb * exp_g)


def _interleave(gens):
    gens = list(gens)
    while gens:
        alive = []
        for g in gens:
            try:
                next(g)
                alive.append(g)
            except StopIteration:
                pass
        gens = alive


def _call_carrying(ex, body, nsteps, *, name, in_specs, out_specs, out_shape, operands, scratch_shapes=()):
    n_in, n_out, n_scr = len(in_specs), len(out_specs), len(scratch_shapes)
    n = ex.n if ex is not None else 0

    def full(*refs):
        o0 = n_in + n
        s0 = o0 + n_out + n
        ex_refs = (refs[n_in:o0], refs[o0 + n_out:s0], refs[s0 + n_scr:])
        step = pl.program_id(0)
        _carry_begin(ex, ex_refs, step, nsteps)
        body(*refs[:n_in], *refs[o0:o0 + n_out], *refs[s0:s0 + n_scr])
        _carry_end(ex, ex_refs, step, nsteps)

    res = pl.pallas_call(
        full, name=name, grid=(nsteps,),
        in_specs=list(in_specs) + [ANY_SPEC] * n, out_specs=list(out_specs) + [ANY_SPEC] * n,
        out_shape=list(out_shape) + (ex.out_shapes if ex is not None else []),
        scratch_shapes=list(scratch_shapes) + (ex.scratch if ex is not None else []),
        compiler_params=pltpu.CompilerParams(dimension_semantics=("arbitrary",), vmem_limit_bytes=VMEM_LIMIT,
                                             has_side_effects=ex is not None),
    )(*operands, *(ex.arrs if ex is not None else []))
    return list(res[:n_out]), list(res[n_out:])


def _gdn_chunk_fwd(qn, kn, v, beta_b, g_b, carry=None):
    LP = qn.shape[0]
    R = GDN_ROWS
    H = GDN_HEADS
    CH = GDN_CHUNK

    def body(q_ref, k_ref, v_ref, b_ref, g_ref, u_ref, w_ref, qd_ref, kd_ref, qk_ref, t_ref, gl_ref):
        def item(cc, h):
            rs = slice(cc * CH, (cc + 1) * CH)
            sl = slice(h * GDN_D, (h + 1) * GDN_D)
            s64 = slice(h * CH, (h + 1) * CH)
            q, k = q_ref[rs, sl], k_ref[rs, sl]
            m = _gdn_chunk_common(q, k, v_ref[rs, sl], b_ref[rs, sl], g_ref[rs, sl])
            qk_raw = _dot1(q, k, _dot_nt)
            yield
            a = jnp.where(m["strict"], m["kk"] * m["decay"], 0.0)
            eye = (_iota2((CH, CH), 0) == _iota2((CH, CH), 1)).astype(F32)
            t = eye - a
            p = _dot3(a, a)
            yield
            for _ in range(4):
                t = t + _dot3(t, p)
                p = _dot3(p, p)
                yield
            t = t + _dot3(t, p)
            yield
            u_ref[rs, sl] = _dot1(t, m["vb"])
            w_ref[rs, sl] = _dot1(t, m["kbg"])
            qk_ref[rs, s64] = qk_raw * m["decay"]
            t_ref[rs, s64] = t
            qd_ref[rs, sl] = q * m["exp_g"]
            kd_ref[rs, sl] = k * m["kd_fac"]
            gl_ref[cc * 8:(cc + 1) * 8, sl] = jnp.broadcast_to(m["gl"], (8, GDN_D))

        _interleave(item(cc, h) for cc in range(R // CH) for h in range(H))

    row = lambda w: pl.BlockSpec((R, w), lambda i: (i, 0))
    o512 = jax.ShapeDtypeStruct((LP, 512), F32)
    o256 = jax.ShapeDtypeStruct((LP, 256), F32)
    return _call_carrying(
        carry, body, LP // R, name="gdn_chunk_fwd",
        in_specs=[row(512)] * 5,
        out_specs=[row(512)] * 4 + [row(256)] * 2 + [pl.BlockSpec((R // 8, 512), lambda i: (i, 0))],
        out_shape=[o512] * 4 + [o256] * 2 + [jax.ShapeDtypeStruct((LP // 8, 512), F32)],
        operands=(qn, kn, v, beta_b, g_b))


def _gdn_chunk_bwd(qn, kn, v, beta_b, g_b, t_all, du, dw, dqd, dkd, dqk, dgl):
    LP = qn.shape[0]
    R = GDN_ROWS
    H = GDN_HEADS
    CH = GDN_CHUNK

    def body(q_ref, k_ref, v_ref, b_ref, g_ref, t_ref, du_ref, dw_ref, dqd_ref, dkd_ref, dqk_ref, dgl_ref,
             dq_ref, dk_ref, dv_ref, db_ref, dg_ref):
        ones = jnp.ones((CH, GDN_D), BF16)

        def item(cc, h):
            rs = slice(cc * CH, (cc + 1) * CH)
            sl = slice(h * GDN_D, (h + 1) * GDN_D)
            s64 = slice(h * CH, (h + 1) * CH)
            q, k, vv, beta = q_ref[rs, sl], k_ref[rs, sl], v_ref[rs, sl], b_ref[rs, sl]
            m = _gdn_chunk_common(q, k, vv, beta, g_ref[rs, sl])
            incl, strict, decay = m["incl"], m["strict"], m["decay"]
            t = t_ref[rs, s64]
            du_, dw_ = du_ref[rs, sl], dw_ref[rs, sl]
            dqd_, dkd_ = dqd_ref[rs, sl], dkd_ref[rs, sl]
            d_t = _dot1(du_, m["vb"], _dot_nt) + _dot1(dw_, m["kbg"], _dot_nt)
            dvb = _dot1(t, du_, _dot_tn)
            dkbg = _dot1(t, dw_, _dot_tn)
            qk_raw = _dot1(q, k, _dot_nt)
            yield
            x1 = _dot3(d_t, t, _dot_nt)
            dkb = dkbg * m["exp_g"]
            d_gi = _rowsum(dkbg * m["kbg"])
            yield
            d_a = jnp.where(strict, -_dot3(t, x1, _dot_tn), 0.0)
            yield
            d_kk = d_a * decay
            dqk_m = jnp.where(incl, dqk_ref[rs, s64], 0.0)
            dqk_raw = dqk_m * decay
            mm = (d_a * m["kk"] + dqk_m * qk_raw) * decay
            dkb = dkb + _dot1(d_kk, k)
            dk_ = _dot1(d_kk, m["kb"], _dot_tn) + _dot1(dqk_raw, q, _dot_tn)
            dq_ = _dot1(dqk_raw, k) + dqd_ * m["exp_g"]
            d_gi = d_gi + (_dot_exact_r(mm, ones) - _dot_exact_r(mm, ones, _dot_tn))
            yield
            d_gi = d_gi + _rowsum(dqd_ * q * m["exp_g"])
            e = _rowsum(dkd_ * k * m["kd_fac"])
            d_gi = d_gi - e
            d_glast = _colsum(jnp.broadcast_to(e, (CH, GDN_D))) + dgl_ref[cc * 8:cc * 8 + 1, sl] * m["gl"]
            dk_ = dk_ + dkd_ * m["kd_fac"] + dkb * beta
            d_gi = d_gi + jnp.where(_iota2((CH, GDN_D), 0) == CH - 1, d_glast, 0.0)
            u_incl = (_iota2((CH, CH), 1) >= _iota2((CH, CH), 0)).astype(BF16)
            dq_ref[rs, sl] = dq_
            dk_ref[rs, sl] = dk_
            dv_ref[rs, sl] = dvb * beta
            db_ref[rs, sl] = jnp.broadcast_to(_rowsum(dvb * vv) + _rowsum(dkb * k), (CH, GDN_D))
            dg_ref[rs, sl] = _dot_exact_l(u_incl, d_gi)

        _interleave(item(cc, h) for cc in range(R // CH) for h in range(H))

    row = lambda w: pl.BlockSpec((R, w), lambda i: (i, 0))
    o512 = jax.ShapeDtypeStruct((LP, 512), F32)
    gl_spec = pl.BlockSpec((R // 8, 512), lambda i: (i, 0))
    return pl.pallas_call(
        body, name="gdn_chunk_bwd", grid=(LP // R,),
        in_specs=[row(512)] * 5 + [row(256)] + [row(512)] * 4 + [row(256), gl_spec],
        out_specs=[row(512)] * 5, out_shape=[o512] * 5,
        compiler_params=_cp(("parallel",)),
    )(qn, kn, v, beta_b, g_b, t_all, du, dw, dqd, dkd, dqk, dgl)


def _gdn_scan_fwd(u, w, qd, kd, qk, gl):
    LP = u.shape[0]
    CH = GDN_CHUNK
    CPS = SCAN_CHUNKS
    N = LP // CH
    NS = N // CPS
    H = GDN_HEADS

    def body(u_ref, w_ref, qd_ref, kd_ref, qk_ref, gl_ref, o_ref, ssave_ref, s_sc):
        @pl.when(pl.program_id(0) == 0)
        def _():
            s_sc[...] = jnp.zeros_like(s_sc)

        for cc in range(CPS):
            rs = slice(cc * CH, (cc + 1) * CH)
            ssave_ref[cc * GDN_D:(cc + 1) * GDN_D, :] = s_sc[...]

            def item(h):
                sl = slice(h * GDN_D, (h + 1) * GDN_D)
                s = s_sc[:, sl]
                v_new = u_ref[rs, sl] - _dot1(w_ref[rs, sl], s)
                o_s = _dot1(qd_ref[rs, sl], s)
                yield
                o_ref[rs, sl] = o_s + _dot1(qk_ref[rs, h * CH:(h + 1) * CH], v_new)
                s_sc[:, sl] = s * gl_ref[cc * 8:cc * 8 + 1, sl] + _dot1(kd_ref[rs, sl], v_new, _dot_tn)

            _interleave(item(h) for h in range(H))

    row = lambda w_: pl.BlockSpec((CPS * CH, w_), lambda n: (n, 0))
    return pl.pallas_call(
        body, name="gdn_scan_fwd", grid=(NS,),
        in_specs=[row(512)] * 4 + [row(256), pl.BlockSpec((CPS * 8, 512), lambda n: (n, 0))],
        out_specs=[row(512), pl.BlockSpec((CPS * GDN_D, 512), lambda n: (n, 0))],
        out_shape=[jax.ShapeDtypeStruct((LP, 512), F32), jax.ShapeDtypeStruct((N * GDN_D, 512), F32)],
        scratch_shapes=[pltpu.VMEM((GDN_D, 512), F32)],
        compiler_params=_cp(("arbitrary",)),
    )(u, w, qd, kd, qk, gl)


def _gdn_scan_bwd(u, w, qd, kd, qk, gl, ssave, do, carry=None):
    LP = u.shape[0]
    CH = GDN_CHUNK
    CPS = SCAN_CHUNKS
    N = LP // CH
    NS = N // CPS
    H = GDN_HEADS

    def body(u_ref, w_ref, qd_ref, kd_ref, qk_ref, gl_ref, s_ref, do_ref,
             du_ref, dw_ref, dqd_ref, dkd_ref, dqk_ref, dgl_ref, ds_sc):
        @pl.when(pl.program_id(0) == 0)
        def _():
            ds_sc[...] = jnp.zeros_like(ds_sc)

        for cc in reversed(range(CPS)):
            rs = slice(cc * CH, (cc + 1) * CH)
            r8 = slice(cc * 8, (cc + 1) * 8)

            def item(h):
                sl = slice(h * GDN_D, (h + 1) * GDN_D)
                s64 = slice(h * CH, (h + 1) * CH)
                s = s_ref[cc * GDN_D:(cc + 1) * GDN_D, sl]
                ds = ds_sc[:, sl]
                do_ = do_ref[rs, sl]
                w_, qd_, kd_, qk_ = w_ref[rs, sl], qd_ref[rs, sl], kd_ref[rs, sl], qk_ref[rs, s64]
                v_new = u_ref[rs, sl] - _dot1(w_, s)
                d_vnew = _dot1(qk_, do_, _dot_tn) + _dot1(kd_, ds)
                dqd_ref[rs, sl] = _dot1(do_, s, _dot_nt)
                ds_new = ds * gl_ref[cc * 8:cc * 8 + 1, sl] + _dot1(qd_, do_, _dot_tn)
                dgl_ref[r8, sl] = jnp.broadcast_to(jnp.sum(_colsum(ds * s), axis=-1, keepdims=True), (8, GDN_D))
                yield
                du_ref[rs, sl] = d_vnew
                dw_ref[rs, sl] = -_dot1(d_vnew, s, _dot_nt)
                dkd_ref[rs, sl] = _dot1(v_new, ds, _dot_nt)
                dqk_ref[rs, s64] = _dot1(do_, v_new, _dot_nt)
                ds_sc[:, sl] = ds_new - _dot1(w_, d_vnew, _dot_tn)

            _interleave(item(h) for h in range(H))

    rev = lambda w_: pl.BlockSpec((CPS * CH, w_), lambda n: (NS - 1 - n, 0))
    rev8 = pl.BlockSpec((CPS * 8, 512), lambda n: (NS - 1 - n, 0))
    o512 = jax.ShapeDtypeStruct((LP, 512), F32)
    return _call_carrying(
        carry, body, NS, name="gdn_scan_bwd",
        in_specs=[rev(512)] * 4 + [rev(256), rev8, pl.BlockSpec((CPS * GDN_D, 512), lambda n: (NS - 1 - n, 0)),
                  rev(512)],
        out_specs=[rev(512)] * 4 + [rev(256), rev8],
        out_shape=[o512] * 4 + [jax.ShapeDtypeStruct((LP, 256), F32), jax.ShapeDtypeStruct((LP // 8, 512), F32)],
        scratch_shapes=[pltpu.VMEM((GDN_D, 512), F32)],
        operands=(u, w, qd, kd, qk, gl, ssave, do))


def _sb_scores(qh, kblk, mask):
    z = _dot_nt(qh, kblk)
    e = jnp.exp(-jnp.abs(z))
    sp = jnp.maximum(z, 0.0) + jnp.log(1.0 + e)
    return z, e, jnp.where(mask, -sp, 0.0), z - sp


def _sb_fwd(proj):
    LP = proj.shape[0]
    B = SB_BLOCK
    W = min(SB_SPAN, LP)
    SUB = SB_SUB
    Q = min(SB_QTILE, LP)
    nq = LP // Q
    nsub = W // SUB
    scale = SB_DH ** -0.5
    qcol, kcol, vcol = OFF_SB // B, (OFF_SB + 512) // B, (OFF_SB + 1024) // B

    def body(q_ref, k_ref, v_ref, tri_ref, o_ref, c_ref, n_ref):
        i = pl.program_id(1)
        lane = _iota2((Q, B), 1)
        head_a = lane < SB_DH
        qs = q_ref[...] * scale
        qh = [jnp.where(head_a, qs, 0.0).astype(BF16), jnp.where(head_a, 0.0, qs).astype(BF16)]
        u_strict = tri_ref[...]
        qpos = i * Q + _iota2((Q, W), 0)
        hi0 = (i + 1) * Q
        nspan = (hi0 + W - 1) // W

        def live(st):
            return (st[0] < nspan) & (st[1] > 0)

        def span(st):
            r, carry = st[0], st[2:]
            hi = hi0 - r * W
            k0 = pl.multiple_of(jnp.maximum(hi - W, 0), B)
            kblk = k_ref[pl.ds(k0, W), :].astype(BF16)
            vblk = v_ref[pl.ds(k0, W), :].astype(BF16)
            kpos = k0 + _iota2((Q, W), 1)
            mask = (kpos < qpos) & (kpos >= PAD_ROWS) & (kpos < hi)
            new = [None] * 4

            def head(h):
                o_acc, c = carry[2 * h], carry[2 * h + 1]
                z, e, l1m, lsg = _sb_scores(qh[h], kblk, mask)
                yield
                subs = [slice(b * SUB, (b + 1) * SUB) for b in range(nsub)]
                suf = [_dot(l1m[:, bs].astype(BF16), u_strict) for bs in subs]
                yield
                parts = [None] * nsub
                for b in reversed(range(nsub)):
                    parts[b] = jnp.where(mask[:, subs[b]], jnp.exp(lsg[:, subs[b]] + suf[b] + c), 0.0)
                    c = c + _rowsum(l1m[:, subs[b]])
                att = jnp.concatenate(parts, axis=1).astype(BF16)
                new[2 * h], new[2 * h + 1] = o_acc + _dot(att, vblk), c

            _interleave(head(h) for h in range(2))
            more = (jnp.maximum(jnp.max(new[1]), jnp.max(new[3])) > SB_DEAD).astype(jnp.int32)
            return (r + 1, more, *new)

        zero_o = jnp.zeros((Q, B), F32)
        zero_c = jnp.zeros((Q, 1), F32)
        nrun, _, o_a, c_a, o_b, c_b = lax.while_loop(
            live, span, (jnp.int32(0), jnp.int32(1), zero_o, zero_c, zero_o, zero_c))
        o_ref[...] = jnp.where(head_a, o_a, o_b)
        c_ref[...] = jnp.where(head_a, c_a, c_b)
        n_ref[pl.program_id(0), i] = nrun

    blk = pl.BlockSpec((Q, B), lambda p, i: (i, p))
    out = jax.ShapeDtypeStruct((LP, 512), F32)
    return pl.pallas_call(
        body, name="sb_fwd", grid=(SB_HEADS // 2, nq),
        in_specs=[pl.BlockSpec((Q, B), lambda p, i: (i, qcol + p)),
                  pl.BlockSpec((LP, B), lambda p, i: (0, kcol + p)),
                  pl.BlockSpec((LP, B), lambda p, i: (0, vcol + p)),
                  pl.BlockSpec((SUB, SUB), lambda p, i: (0, 0))],
        out_specs=[blk, blk, pl.BlockSpec(memory_space=pltpu.SMEM)],
        out_shape=[out, out, jax.ShapeDtypeStruct((SB_HEADS // 2, nq), jnp.int32)],
        compiler_params=_cp(("arbitrary", "arbitrary")),
    )(proj, proj, proj, jnp.tril(jnp.ones((SUB, SUB), BF16), -1))


def _sb_bwd(proj, ctot, nrun_all, do):
    LP = proj.shape[0]
    B = SB_BLOCK
    W = min(SB_SPAN, LP)
    SUB = SB_SUB
    Q = min(SB_QTILE, LP)
    nq = LP // Q
    nsub = W // SUB
    scale = SB_DH ** -0.5
    qcol, kcol, vcol = OFF_SB // B, (OFF_SB + 512) // B, (OFF_SB + 1024) // B

    def body(n_ref, q_ref, k_ref, v_ref, c_ref, do_ref, tril_ref, triu_ref, dq_ref, dk_ref, dv_ref):
        i = pl.program_id(1)

        @pl.when(i == 0)
        def _():
            dk_ref[...] = jnp.zeros_like(dk_ref)
            dv_ref[...] = jnp.zeros_like(dv_ref)

        lane = _iota2((Q, B), 1)
        head_a = lane < SB_DH
        qs = q_ref[...] * scale
        qh = [jnp.where(head_a, qs, 0.0).astype(BF16), jnp.where(head_a, 0.0, qs).astype(BF16)]
        dof = do_ref[...]
        doh = [jnp.where(head_a, dof, 0.0).astype(BF16), jnp.where(head_a, 0.0, dof).astype(BF16)]
        cfull = c_ref[...]
        ctot_h = [cfull[:, 0:1], cfull[:, SB_DH:SB_DH + 1]]
        u_strict = tril_ref[...]
        l_strict = triu_ref[...]
        qpos = i * Q + _iota2((Q, W), 0)
        hi0 = (i + 1) * Q
        nrun = n_ref[pl.program_id(0), i]

        def span(t, carry):
            r = nrun - 1 - t
            hi = hi0 - r * W
            k0 = pl.multiple_of(jnp.maximum(hi - W, 0), B)
            kblk = k_ref[pl.ds(k0, W), :].astype(BF16)
            vblk = v_ref[pl.ds(k0, W), :].astype(BF16)
            kpos = k0 + _iota2((Q, W), 1)
            mask = (kpos < qpos) & (kpos >= PAD_ROWS) & (kpos < hi)
            new = [None] * 6
            dk_add, dv_add = [None, None], [None, None]
            subs = [slice(b * SUB, (b + 1) * SUB) for b in range(nsub)]

            def head(h):
                dq_acc, pre, ecar = carry[3 * h], carry[3 * h + 1], carry[3 * h + 2]
                z, e, l1m, lsg = _sb_scores(qh[h], kblk, mask)
                d_att = _dot_nt(doh[h], vblk)
                yield
                sig = jnp.exp(lsg)
                suf = [_dot(l1m[:, bs].astype(BF16), u_strict) for bs in subs]
                yield
                att_parts, p_parts = [None] * nsub, [None] * nsub
                for b, bs in enumerate(subs):
                    pre = pre + _rowsum(l1m[:, bs])
                    att_parts[b] = jnp.where(mask[:, bs], jnp.exp(lsg[:, bs] + suf[b] + (ctot_h[h] - pre)), 0.0)
                    p_parts[b] = att_parts[b] * d_att[:, bs]
                pcum = [_dot(p.astype(BF16), l_strict) for p in p_parts]
                yield
                dz_parts = [None] * nsub
                for b, bs in enumerate(subs):
                    sg = sig[:, bs]
                    dz_parts[b] = jnp.where(mask[:, bs], p_parts[b] * (1.0 - sg) - sg * (ecar + pcum[b]), 0.0)
                    ecar = ecar + _rowsum(p_parts[b])
                att = jnp.concatenate(att_parts, axis=1).astype(BF16)
                dz = jnp.concatenate(dz_parts, axis=1).astype(BF16)
                new[3 * h:3 * h + 3] = [dq_acc + _dot(dz, kblk), pre, ecar]
                dk_add[h] = _dot_tn(dz, qh[h])
                dv_add[h] = _dot_tn(att, doh[h])

            _interleave(head(h) for h in range(2))
            dk_ref[pl.ds(k0, W), :] += dk_add[0] + dk_add[1]
            dv_ref[pl.ds(k0, W), :] += dv_add[0] + dv_add[1]
            return tuple(new)

        zero_o = jnp.zeros((Q, B), F32)
        zero_c = jnp.zeros((Q, 1), F32)
        res = lax.fori_loop(0, nrun, span, (zero_o, zero_c, zero_c, zero_o, zero_c, zero_c))
        dq_ref[...] = (jnp.where(head_a, res[0], res[3]) * scale).astype(BF16)

    blk = pl.BlockSpec((Q, B), lambda p, i: (i, p))
    col = pl.BlockSpec((LP, B), lambda p, i: (0, p))
    tri = pl.BlockSpec((SUB, SUB), lambda p, i: (0, 0))
    out = jax.ShapeDtypeStruct((LP, 512), F32)
    return pl.pallas_call(
        body, name="sb_bwd", grid=(SB_HEADS // 2, nq),
        in_specs=[pl.BlockSpec(memory_space=pltpu.SMEM),
                  pl.BlockSpec((Q, B), lambda p, i: (i, qcol + p)),
                  pl.BlockSpec((LP, B), lambda p, i: (0, kcol + p)),
                  pl.BlockSpec((LP, B), lambda p, i: (0, vcol + p)),
                  blk, blk, tri, tri],
        out_specs=[blk, col, col], out_shape=[jax.ShapeDtypeStruct((LP, 512), BF16), out, out],
        compiler_params=_cp(("arbitrary", "arbitrary")),
    )(nrun_all, proj, proj, proj, ctot, do, jnp.tril(jnp.ones((SUB, SUB), BF16), -1),
      jnp.triu(jnp.ones((SUB, SUB), BF16), 1))


def _sb_group_mean():
    r = jnp.right_shift(_iota2((512, 512), 0), 6)
    c = jnp.right_shift(_iota2((512, 512), 1), 6)
    return jnp.where(r == c, 1.0 / SB_DH, 0.0).astype(BF16)


def _attn_norm_fwd(og, proj, osb, gnw, snw):
    LP = og.shape[0]
    T = _tile(LP, 256)

    def body(og_ref, z_ref, os_ref, gnw_ref, snw_ref, y_ref):
        valid = (pl.program_id(0) * T + _iota2((T, 1), 0)) >= PAD_ROWS
        z = z_ref[...]
        zg = z * _sigmoid(z)
        for h in range(GDN_HEADS):
            sl = slice(h * GDN_D, (h + 1) * GDN_D)
            o = og_ref[:, sl]
            y = o * _rms(o) * gnw_ref[...] * zg[:, sl]
            y_ref[:, sl] = jnp.where(valid, y, 0.0).astype(BF16)
        o = os_ref[...]
        msq = _dot_exact_r(o * o, _sb_group_mean())
        y = o * lax.rsqrt(msq + NORM_EPS) * snw_ref[...]
        y_ref[:, 512:] = jnp.where(valid, y, 0.0).astype(BF16)

    row = pl.BlockSpec((T, 512), lambda i: (i, 0))
    return pl.pallas_call(
        body, name="attn_norm_fwd", grid=(LP // T,),
        in_specs=[row, pl.BlockSpec((T, 512), lambda i: (i, OFF_Z // 512)), row,
                  pl.BlockSpec((1, GDN_D), lambda i: (0, 0)), pl.BlockSpec((1, 512), lambda i: (0, 0))],
        out_specs=pl.BlockSpec((T, 1024), lambda i: (i, 0)),
        out_shape=jax.ShapeDtypeStruct((LP, 1024), BF16),
        compiler_params=_cp(("parallel",)),
    )(og, proj, osb, gnw, snw)


def _attn_norm_bwd(og, proj, osb, gnw, snw, dy, carry=None):
    LP = og.shape[0]
    T = _tile(LP, 256)

    def body(og_ref, z_ref, os_ref, gnw_ref, snw_ref, dy_ref, dog_ref, dz_ref, dos_ref, dgw_ref, dsw_ref):
        @pl.when(pl.program_id(0) == 0)
        def _():
            dgw_ref[...] = jnp.zeros_like(dgw_ref)
            dsw_ref[...] = jnp.zeros_like(dsw_ref)
        valid = (pl.program_id(0) * T + _iota2((T, 1), 0)) >= PAD_ROWS
        dy = jnp.where(valid, dy_ref[...], 0.0)
        z = z_ref[...]
        sg = _sigmoid(z)
        zg = z * sg
        dgw = jnp.zeros((1, GDN_D), F32)
        for h in range(GDN_HEADS):
            sl = slice(h * GDN_D, (h + 1) * GDN_D)
            o = og_ref[:, sl]
            dyh = dy[:, sl]
            dx, dwn = _rms_bwd(o, gnw_ref[...], dyh * zg[:, sl])
            dog_ref[:, sl] = dx
            dgw = dgw + _colsum(dwn)
            yn = o * _rms(o) * gnw_ref[...]
            dz_ref[:, sl] = (dyh * yn * (sg[:, sl] * (1.0 + z[:, sl] * (1.0 - sg[:, sl])))).astype(BF16)
        dgw_ref[...] += dgw
        o = os_ref[...]
        gm = _sb_group_mean()
        r = lax.rsqrt(_dot_exact_r(o * o, gm) + NORM_EPS)
        n = o * r
        dys = dy[:, 512:]
        dyw = dys * snw_ref[...]
        dos_ref[...] = r * (dyw - n * _dot_exact_r(dyw * n, gm))
        dsw_ref[...] += _colsum(dys * n)

    row = pl.BlockSpec((T, 512), lambda i: (i, 0))
    gw = pl.BlockSpec((1, GDN_D), lambda i: (0, 0))
    sw = pl.BlockSpec((1, 512), lambda i: (0, 0))
    o512 = jax.ShapeDtypeStruct((LP, 512), F32)
    return _call_carrying(
        carry, body, LP // T, name="attn_norm_bwd",
        in_specs=[row, pl.BlockSpec((T, 512), lambda i: (i, OFF_Z // 512)), row, gw, sw,
                  pl.BlockSpec((T, 1024), lambda i: (i, 0))],
        out_specs=[row, row, row, gw, sw],
        out_shape=[o512, jax.ShapeDtypeStruct((LP, 512), BF16), o512, jax.ShapeDtypeStruct((1, GDN_D), F32),
                   jax.ShapeDtypeStruct((1, 512), F32)],
        operands=(og, proj, osb, gnw, snw, dy))


def _resid_fwd(h0, mix, w_post, w_pre):
    LP, D = h0.shape
    T = _tile(LP, 512)

    def body(h0_ref, mix_ref, wp_ref, wf_ref, h1_ref, n2_ref):
        mix = mix_ref[...]
        h1 = h0_ref[...] + mix * _rms(mix) * wp_ref[...]
        h1_ref[...] = h1
        n2_ref[...] = (h1 * _rms(h1) * wf_ref[...]).astype(BF16)

    row = pl.BlockSpec((T, D), lambda i: (i, 0))
    vec = pl.BlockSpec((1, D), lambda i: (0, 0))
    return pl.pallas_call(
        body, name="resid_fwd", grid=(LP // T,),
        in_specs=[row, row, vec, vec], out_specs=[row, row],
        out_shape=[jax.ShapeDtypeStruct((LP, D), F32), jax.ShapeDtypeStruct((LP, D), BF16)],
        compiler_params=_cp(("parallel",)),
    )(h0, mix, w_post, w_pre)


def _resid_bwd(h1, mix, w_post, w_pre, dout, dn2):
    LP, D = h1.shape
    T = _tile(LP, 512)

    def body(h1_ref, mix_ref, wp_ref, wf_ref, dout_ref, dn2_ref, dh1_ref, dmix_ref, dwf_ref, dwp_ref):
        @pl.when(pl.program_id(0) == 0)
        def _():
            dwf_ref[...] = jnp.zeros_like(dwf_ref)
            dwp_ref[...] = jnp.zeros_like(dwp_ref)
        dx, dwn = _rms_bwd(h1_ref[...], wf_ref[...], dn2_ref[...])
        dh1 = dout_ref[...] + dx
        dh1_ref[...] = dh1
        dwf_ref[...] += _colsum(dwn)
        dmix, dwn2 = _rms_bwd(mix_ref[...], wp_ref[...], dh1)
        dmix_ref[...] = dmix.astype(BF16)
        dwp_ref[...] += _colsum(dwn2)

    row = pl.BlockSpec((T, D), lambda i: (i, 0))
    vec = pl.BlockSpec((1, D), lambda i: (0, 0))
    v = jax.ShapeDtypeStruct((1, D), F32)
    return pl.pallas_call(
        body, name="resid_bwd", grid=(LP // T,),
        in_specs=[row, row, vec, vec, row, row], out_specs=[row, row, vec, vec],
        out_shape=[jax.ShapeDtypeStruct((LP, D), F32), jax.ShapeDtypeStruct((LP, D), BF16), v, v],
        compiler_params=_cp(("arbitrary",)),
    )(h1, mix, w_post, w_pre, dout, dn2)


GELU_C = 0.7978845608028654
GELU_A = 0.044715


def _gelu_parts(x):
    t = jnp.tanh(GELU_C * (x + GELU_A * x * x * x))
    return 0.5 * x * (1.0 + t), t


def _convglu_fwd(up, conv_w, conv_b):
    LP, C = up.shape
    T = _tile(LP, 128)

    def body(x_ref, halo_ref, cw_ref, cb_ref, act_ref, y_ref):
        i = pl.program_id(0)

        def conv(cols):
            ext = jnp.concatenate([jnp.where(i > 0, halo_ref[:, cols], 0.0), x_ref[:, cols]], axis=0)
            w = cw_ref[:, cols]
            y = (w[2:3] * ext[8:] + w[1:2] * pltpu.roll(ext, 1, 0)[8:] + w[0:1] * pltpu.roll(ext, 2, 0)[8:]
                 + cb_ref[:, cols])
            y_ref[:, cols] = y.astype(BF16)
            return y

        for s in range(D_FF // LANE):
            gs = slice(s * LANE, (s + 1) * LANE)
            g, _ = _gelu_parts(conv(gs))
            act_ref[:, gs] = (g * conv(slice(D_FF + s * LANE, D_FF + (s + 1) * LANE))).astype(BF16)

    t8 = T // 8
    return pl.pallas_call(
        body, name="convglu_fwd", grid=(LP // T,),
        in_specs=[pl.BlockSpec((T, C), lambda i: (i, 0)),
                  pl.BlockSpec((8, C), lambda i: (jnp.maximum(i * t8 - 1, 0), 0)),
                  pl.BlockSpec((FFN_CONV, C), lambda i: (0, 0)), pl.BlockSpec((1, C), lambda i: (0, 0))],
        out_specs=[pl.BlockSpec((T, D_FF), lambda i: (i, 0)), pl.BlockSpec((T, C), lambda i: (i, 0))],
        out_shape=[jax.ShapeDtypeStruct((LP, D_FF), BF16), jax.ShapeDtypeStruct((LP, C), BF16)],
        compiler_params=_cp(("parallel",)),
    )(up, up, conv_w, conv_b)


def _convglu_bwd(up, y, conv_w, dact):
    LP, C = up.shape
    T = _tile(LP, 128)
    TE = T + 8
    nt = LP // T

    def body(x_ref, y_ref, yn_ref, cw_ref, da_ref, dan_ref, dx_ref, dcw_ref, dcb_ref):
        i = pl.program_id(0)

        @pl.when(i == 0)
        def _():
            dcw_ref[...] = jnp.zeros_like(dcw_ref)
            dcb_ref[...] = jnp.zeros_like(dcb_ref)

        last = i == nt - 1

        def back(cols, dy):
            w = cw_ref[:, cols]
            later = [dy[0:T], pltpu.roll(dy, TE - 1, 0)[0:T], pltpu.roll(dy, TE - 2, 0)[0:T]]
            x_t = x_ref[:, cols]
            dcb_ref[:, cols] += _colsum(later[0])
            for j in range(FFN_CONV):
                dcw_ref[j:j + 1, cols] += _colsum(later[FFN_CONV - 1 - j] * x_t)
            dx_ref[:, cols] = (w[2:3] * later[0] + w[1:2] * later[1] + w[0:1] * later[2]).astype(BF16)

        for s in range(D_FF // LANE):
            gs = slice(s * LANE, (s + 1) * LANE)
            vs = slice(D_FF + s * LANE, D_FF + (s + 1) * LANE)
            gate = jnp.concatenate([y_ref[:, gs].astype(F32), yn_ref[0:8, gs].astype(F32)], axis=0)
            val = jnp.concatenate([y_ref[:, vs].astype(F32), yn_ref[0:8, vs].astype(F32)], axis=0)
            g, t = _gelu_parts(gate)
            dg_dx = 0.5 * (1.0 + t) + 0.5 * gate * (1.0 - t * t) * GELU_C * (1.0 + 3.0 * GELU_A * gate * gate)
            da = jnp.concatenate([da_ref[:, gs], jnp.where(last, 0.0, dan_ref[:, gs])], axis=0)
            back(gs, da * val * dg_dx)
            back(vs, da * g)

    t8 = T // 8
    nb8 = LP // 8
    next8 = lambda w: pl.BlockSpec((8, w), lambda i: (jnp.minimum((i + 1) * t8, nb8 - 1), 0))
    row = lambda w: pl.BlockSpec((T, w), lambda i: (i, 0))
    small = lambda r: pl.BlockSpec((r, C), lambda i: (0, 0))
    return pl.pallas_call(
        body, name="convglu_bwd", grid=(nt,),
        in_specs=[row(C), row(C), pl.BlockSpec((16, C), lambda i: (jnp.minimum((i + 1) * (T // 16), LP // 16 - 1), 0)),
                  small(FFN_CONV), row(D_FF), next8(D_FF)],
        out_specs=[row(C), small(FFN_CONV), small(1)],
        out_shape=[jax.ShapeDtypeStruct((LP, C), BF16), jax.ShapeDtypeStruct((FFN_CONV, C), F32),
                   jax.ShapeDtypeStruct((1, C), F32)],
        compiler_params=_cp(("arbitrary",)),
    )(up, y, y, conv_w, dact, dact)


def _final(h1, f, w_post, target, n_real):
    LP, D = h1.shape
    T = _tile(LP, 256)

    def body(h1_ref, f_ref, w_ref, t_ref, loss_ref, dout_ref, df_ref, dw_ref):
        @pl.when(pl.program_id(0) == 0)
        def _():
            loss_ref[...] = jnp.zeros_like(loss_ref)
            dw_ref[...] = jnp.zeros_like(dw_ref)
        rows = pl.program_id(0) * T + _iota2((T, 1), 0)
        real = (rows >= ROW0) & (rows < ROW0 + n_real)
        f = f_ref[...]
        out = h1_ref[...] + f * _rms(f) * w_ref[...]
        err = jnp.where(real, out - t_ref[...], 0.0)
        loss_ref[...] += 0.5 * jnp.sum(_colsum(jnp.mean(err * err, axis=-1, keepdims=True)), axis=-1, keepdims=True)
        dout = err * (1.0 / D)
        dout_ref[...] = dout
        dx, dwn = _rms_bwd(f, w_ref[...], dout)
        df_ref[...] = dx.astype(BF16)
        dw_ref[...] += _colsum(dwn)

    row = pl.BlockSpec((T, D), lambda i: (i, 0))
    vec = pl.BlockSpec((1, D), lambda i: (0, 0))
    return pl.pallas_call(
        body, name="final_loss", grid=(LP // T,),
        in_specs=[row, row, vec, row],
        out_specs=[pl.BlockSpec((1, 128), lambda i: (0, 0)), row, row, vec],
        out_shape=[jax.ShapeDtypeStruct((1, 128), F32), jax.ShapeDtypeStruct((LP, D), F32),
                   jax.ShapeDtypeStruct((LP, D), BF16), jax.ShapeDtypeStruct((1, D), F32)],
        compiler_params=_cp(("arbitrary",)),
    )(h1, f, w_post, target)


ANY_SPEC = pl.BlockSpec(memory_space=pl.ANY)
N_CHIP = 4


def _other_chips(x, y):
    return [(1 - x, y), (x, 1 - y), (1 - x, 1 - y)]


def _gather_direct(arrs, name):
    n = len(arrs)
    npeer = N_DEV - 1

    def body(*refs):
        ins, outs = refs[:n], refs[n:2 * n]
        send_sems, recv_sems, loc_sems = refs[2 * n:]
        x, y, c = lax.axis_index("x"), lax.axis_index("y"), lax.axis_index("c")
        me = 4 * x + 2 * y + c
        copies = []
        for a in range(n):
            for kk in range(1, N_DEV):
                px = 1 - x if kk & 4 else x
                py = 1 - y if kk & 2 else y
                pc = 1 - c if kk & 1 else c
                s = a * npeer + kk - 1
                cp = pltpu.make_async_remote_copy(src_ref=ins[a], dst_ref=outs[a].at[me], send_sem=send_sems.at[s],
                                                  recv_sem=recv_sems.at[s], device_id=(px, py, pc), device_id_type=MESH)
                cp.start()
                copies.append(cp)
            own = pltpu.make_async_copy(ins[a], outs[a].at[me], loc_sems.at[a])
            own.start()
            copies.append(own)
        for cp in copies:
            cp.wait()

    shapes = [jax.ShapeDtypeStruct((N_DEV,) + tuple(a.shape), a.dtype) for a in arrs]
    return pl.pallas_call(
        body, name=name, in_specs=[ANY_SPEC] * n, out_specs=[ANY_SPEC] * n, out_shape=shapes,
        scratch_shapes=[pltpu.SemaphoreType.DMA((n * npeer,)), pltpu.SemaphoreType.DMA((n * npeer,)),
                        pltpu.SemaphoreType.DMA((n,))],
        compiler_params=pltpu.CompilerParams(has_side_effects=True),
    )(*arrs)


class _Exchange:
    def __init__(self, arrs, out_shapes, scratch, start, finish, mid=None):
        self.arrs, self.out_shapes, self.scratch = list(arrs), list(out_shapes), list(scratch)
        self.start, self.finish, self.mid = start, finish, mid

    @property
    def n(self):
        return len(self.arrs)


def _run_exchange(ex, name):
    n = ex.n

    def body(*refs):
        ins, outs, sems = refs[:n], refs[n:2 * n], refs[2 * n:]
        ex.start(ins, outs, sems)
        if ex.mid is not None:
            ex.mid(ins, outs, sems)
        ex.finish(ins, outs, sems)

    return pl.pallas_call(
        body, name=name, in_specs=[ANY_SPEC] * n, out_specs=[ANY_SPEC] * n, out_shape=ex.out_shapes,
        scratch_shapes=ex.scratch, compiler_params=pltpu.CompilerParams(has_side_effects=True),
    )(*ex.arrs)


def _carry_begin(ex, refs, step, nsteps):
    if ex is None:
        return

    @pl.when(step == 0)
    def _():
        ex.start(*refs)

    if ex.mid is not None:
        @pl.when(step == min(nsteps - 1, (3 * nsteps) // 5))
        def _():
            ex.mid(*refs)


def _carry_end(ex, refs, step, nsteps):
    if ex is None:
        return

    @pl.when(step == nsteps - 1)
    def _():
        ex.finish(*refs)


def _gather_two_level(arrs):
    n = len(arrs)
    K = 7

    def env(ins, outs, sems):
        send_sems, recv_sems, loc_sems = sems
        x, y, c = lax.axis_index("x"), lax.axis_index("y"), lax.axis_index("c")

        def cp(a, k, src, slot, to):
            return pltpu.make_async_remote_copy(src_ref=src, dst_ref=outs[a].at[slot], send_sem=send_sems.at[a * K + k],
                                                recv_sem=recv_sems.at[a * K + k], device_id=to, device_id_type=MESH)

        me = 4 * x + 2 * y + c
        owns = [pltpu.make_async_copy(ins[a], outs[a].at[me], loc_sems.at[a]) for a in range(n)]
        first = []
        for a in range(n):
            first.append(cp(a, 0, ins[a], me, (x, y, 1 - c)))
            first += [cp(a, 1 + j, ins[a], me, (px, py, c)) for j, (px, py) in enumerate(_other_chips(x, y))]
        passed = []
        for j, (px, py) in enumerate(_other_chips(x, y)):
            slot = 4 * px + 2 * py + c
            passed += [(cp(a, 1 + j, ins[a], slot, (px, py, c)), cp(a, 4 + j, outs[a].at[slot], slot, (x, y, 1 - c)))
                       for a in range(n)]
        from_sib = []
        for a in range(n):
            from_sib.append(cp(a, 0, ins[a], 4 * x + 2 * y + (1 - c), (x, y, 1 - c)))
            from_sib += [cp(a, 4 + j, ins[a], 4 * px + 2 * py + (1 - c), (x, y, 1 - c))
                         for j, (px, py) in enumerate(_other_chips(x, y))]
        return owns, first, passed, from_sib

    def start(ins, outs, sems):
        owns, first, _, _ = env(ins, outs, sems)
        for cp in owns + first:
            cp.start()

    def mid(ins, outs, sems):
        _, _, passed, _ = env(ins, outs, sems)
        for arrival, fwd in passed:
            arrival.wait_recv()
            fwd.start()

    def finish(ins, outs, sems):
        owns, first, passed, from_sib = env(ins, outs, sems)
        for cp in from_sib:
            cp.wait_recv()
        for cp in first + [fwd for _, fwd in passed]:
            cp.wait_send()
        for cp in owns:
            cp.wait()

    shapes = [jax.ShapeDtypeStruct((N_DEV,) + tuple(a.shape), a.dtype) for a in arrs]
    scratch = [pltpu.SemaphoreType.DMA((n * K,)), pltpu.SemaphoreType.DMA((n * K,)), pltpu.SemaphoreType.DMA((n,))]
    return _Exchange(arrs, shapes, scratch, start, finish, mid)


def _swap_sibling(arrs):
    n = len(arrs)

    def copies(ins, outs, sems):
        send_sems, recv_sems = sems
        x, y, c = lax.axis_index("x"), lax.axis_index("y"), lax.axis_index("c")
        return [pltpu.make_async_remote_copy(src_ref=ins[a], dst_ref=outs[a], send_sem=send_sems.at[a],
                                             recv_sem=recv_sems.at[a], device_id=(x, y, 1 - c), device_id_type=MESH)
                for a in range(n)]

    def start(ins, outs, sems):
        for cp in copies(ins, outs, sems):
            cp.start()

    def finish(ins, outs, sems):
        for cp in copies(ins, outs, sems):
            cp.wait()

    shapes = [jax.ShapeDtypeStruct(tuple(a.shape), a.dtype) for a in arrs]
    return _Exchange(arrs, shapes, [pltpu.SemaphoreType.DMA((n,)), pltpu.SemaphoreType.DMA((n,))], start, finish)


def _exchange_chips(arrs):
    n = len(arrs)
    K = N_CHIP - 1

    def copies(ins, outs, sems):
        send_sems, recv_sems, loc_sems = sems
        x, y, c = lax.axis_index("x"), lax.axis_index("y"), lax.axis_index("c")
        mine = 2 * x + y
        out = []
        for a in range(n):
            out += [pltpu.make_async_remote_copy(src_ref=ins[a].at[2 * px + py], dst_ref=outs[a].at[mine],
                                                 send_sem=send_sems.at[a * K + j], recv_sem=recv_sems.at[a * K + j],
                                                 device_id=(px, py, c), device_id_type=MESH)
                    for j, (px, py) in enumerate(_other_chips(x, y))]
            out.append(pltpu.make_async_copy(ins[a].at[mine], outs[a].at[mine], loc_sems.at[a]))
        return out

    def start(ins, outs, sems):
        for cp in copies(ins, outs, sems):
            cp.start()

    def finish(ins, outs, sems):
        for cp in copies(ins, outs, sems):
            cp.wait()

    shapes = [jax.ShapeDtypeStruct(tuple(a.shape), a.dtype) for a in arrs]
    scratch = [pltpu.SemaphoreType.DMA((n * K,)), pltpu.SemaphoreType.DMA((n * K,)), pltpu.SemaphoreType.DMA((n,))]
    return _Exchange(arrs, shapes, scratch, start, finish)


def _add_halves(mine, theirs, name):
    _, R, C = mine.shape
    cap = max(16, (ELEMWISE_VMEM // (4 * C * 10)) // 16 * 16)
    T = R if R <= cap else _tile(R, cap, 16)

    def body(a_ref, b_ref, o_ref):
        o_ref[...] = (a_ref[...] + b_ref[...].astype(F32)).astype(BF16)

    blk = pl.BlockSpec((N_CHIP, T, C), lambda i: (0, i, 0))
    return pl.pallas_call(
        body, name=name, grid=(R // T,), in_specs=[blk, blk], out_specs=blk,
        out_shape=jax.ShapeDtypeStruct(mine.shape, BF16), compiler_params=_cp(("parallel",)),
    )(mine, theirs)


def _adamw(parts, w, m, v, name):
    R, C = w.shape
    npart = parts.shape[0]
    cap = max(16, (ELEMWISE_VMEM // (4 * C * 12)) // 16 * 16)
    T = R if R <= cap else _tile(R, cap, 16)

    def body(p_ref, w_ref, m_ref, v_ref, g_ref, d_ref, nm_ref, nv_ref):
        g = p_ref[0].astype(F32)
        for k in range(1, npart):
            g = g + p_ref[k].astype(F32)
        mm = ADAM_B1 * m_ref[...] + (1.0 - ADAM_B1) * g
        vv = ADAM_B2 * v_ref[...] + (1.0 - ADAM_B2) * (g * g)
        m_hat = mm / (1.0 - ADAM_B1 ** ADAM_STEP)
        v_hat = vv / (1.0 - ADAM_B2 ** ADAM_STEP)
        g_ref[...] = g
        d_ref[...] = -ADAM_LR * (m_hat / (jnp.sqrt(v_hat) + ADAM_EPS) + ADAM_WD * w_ref[...])
        nm_ref[...] = mm
        nv_ref[...] = vv

    row = pl.BlockSpec((T, C), lambda i: (i, 0))
    out = jax.ShapeDtypeStruct((R, C), F32)
    return pl.pallas_call(
        body, name=name, grid=(R // T,),
        in_specs=[pl.BlockSpec((npart, T, C), lambda i: (0, i, 0)), row, row, row],
        out_specs=[row] * 4, out_shape=[out] * 4,
        compiler_params=_cp(("parallel",)),
    )(parts, w, m, v)


SMALL = ("attn_pre_norm", "gdn_A_log", "gdn_dt_bias", "gdn_norm_w", "sb_norm_w", "attn_post_norm",
         "ffn_pre_norm", "ffn_conv_b", "ffn_post_norm")


def _pack_small(arrs):
    rows = []
    for a in arrs:
        flat = a.reshape(-1).astype(F32)
        n = -(-flat.shape[0] // 128) * 128
        rows.append(jnp.pad(flat, (0, n - flat.shape[0])).reshape(-1, 128))
    slab = jnp.concatenate(rows, axis=0)
    pad = (-slab.shape[0]) % 8
    return jnp.pad(slab, ((0, pad), (0, 0)))


def _unpack_small(slab, shapes):
    out, r = [], 0
    for shp in shapes:
        size = 1
        for s in shp:
            size *= s
        nr = -(-size // 128)
        out.append(slab[r:r + nr].reshape(-1)[:size].reshape(shp))
        r += nr
    return out


def _to_blocks_cols(a):
    R, C = a.shape
    return a.reshape(R, N_DEV, C // N_DEV).transpose(1, 0, 2)


def _from_blocks_cols(a):
    n, R, c = a.shape
    return a.transpose(1, 0, 2).reshape(R, n * c)


def kernel(x, meta_tokens, attn_pre_norm, w_in, gdn_conv_w, gdn_A_log, gdn_dt_bias, gdn_norm_w, sb_norm_w, w_out, attn_post_norm, ffn_pre_norm, w_ffn_up, ffn_conv_w, ffn_conv_b, w_ffn_down, ffn_post_norm, loss_target, m_meta_tokens, m_attn_pre_norm, m_w_in, m_gdn_conv_w, m_gdn_A_log, m_gdn_dt_bias, m_gdn_norm_w, m_sb_norm_w, m_w_out, m_attn_post_norm, m_ffn_pre_norm, m_w_ffn_up, m_ffn_conv_w, m_ffn_conv_b, m_w_ffn_down, m_ffn_post_norm, v_meta_tokens, v_attn_pre_norm, v_w_in, v_gdn_conv_w, v_gdn_A_log, v_gdn_dt_bias, v_gdn_norm_w, v_sb_norm_w, v_w_out, v_attn_post_norm, v_ffn_pre_norm, v_w_ffn_up, v_ffn_conv_w, v_ffn_conv_b, v_w_ffn_down, v_ffn_post_norm):
    args = dict(locals())
    seq = x.shape[1]
    LP = -(-(ROW0 + seq) // LP_ALIGN) * LP_ALIGN
    tail = LP - ROW0 - seq

    meta_f = _from_blocks_cols(_run_exchange(_gather_two_level([meta_tokens]), "gather_meta")[0])

    (h0, target), got = _build_rows(x[0], meta_f, loss_target[0], LP,
                                    carry=_gather_two_level([w_in[0].astype(BF16), gdn_conv_w[0]]))
    (u,), _ = _prenorm_fwd(h0, attn_pre_norm)
    win_o = _from_blocks_cols(got[0])
    o_ab = C_QKV
    o_z = o_ab + 2 * GDN_HEADS
    w_inp = jnp.concatenate([win_o[:, :C_QKV], win_o[:, o_z:o_z + C_Z], win_o[:, o_z + C_Z:],
                             win_o[:, o_ab:o_z], jnp.zeros((D_MODEL, C_AB - 2 * GDN_HEADS), BF16)], axis=1)
    gconv_f = _from_blocks_cols(got[1])
    proj = _mm(u, w_inp, F32, "mm_in")
    (qn, kn, vg, beta_b, g_b), got = _gdn_pre_fwd(
        proj, gconv_f, gdn_A_log, gdn_dt_bias,
        carry=_gather_two_level([w_out[0].astype(BF16), w_ffn_down[0].astype(BF16)]))
    w_out_f = got[0].reshape(D_MODEL, D_MODEL)
    w_down_f = got[1].reshape(D_FF, D_MODEL)
    (cu, cw, cqd, ckd, cqk, ct, cgl), got = _gdn_chunk_fwd(
        qn, kn, vg, beta_b, g_b, carry=_gather_two_level([w_ffn_up[0].astype(BF16), ffn_conv_w[0]]))
    w_up_f = _from_blocks_cols(got[0])
    fconv_f = _from_blocks_cols(got[1])
    og, ssave = _gdn_scan_fwd(cu, cw, cqd, ckd, cqk, cgl)
    osb, ctot, sb_nrun = _sb_fwd(proj)
    snw = sb_norm_w.reshape(1, SB_HEADS * SB_DH)
    y = _attn_norm_fwd(og, proj, osb, gdn_norm_w, snw)
    mix = _mm(y, w_out_f, F32, "mm_out")
    h1, n2 = _resid_fwd(h0, mix, attn_post_norm, ffn_pre_norm)
    up = _mm(n2, w_up_f, F32, "mm_up")
    act, conv_y = _convglu_fwd(up, fconv_f, ffn_conv_b)
    f = _mm(act, w_down_f, F32, "mm_down")
    loss_part, dout, df, d_fpost = _final(h1, f, ffn_post_norm, target, seq)

    d_wdown = _mm_tn(act, df, "mm_dw_down")
    dact = _mm_nt(df, w_down_f, F32, "mm_dact")
    dup, d_fconv, d_fconvb = _convglu_bwd(up, conv_y, fconv_f, dact)
    d_wup = _mm_tn(n2, dup, "mm_dw_up")
    dn2 = _mm_nt(dup, w_up_f, F32, "mm_dn2")
    dh1, dmix, d_fpre, d_apost = _resid_bwd(h1, mix, attn_post_norm, ffn_pre_norm, dout, dn2)
    d_wout = _mm_tn(y, dmix, "mm_dw_out")
    dy = _mm_nt(dmix, w_out_f, F32, "mm_dy")
    my_c = lax.axis_index("c")

    def core_halves(blocks):
        halves = [s.reshape((N_CHIP, 2) + s.shape[1:]) for s in blocks]
        return ([lax.dynamic_index_in_dim(h, my_c, axis=1, keepdims=False) for h in halves],
                [lax.dynamic_index_in_dim(h, 1 - my_c, axis=1, keepdims=False).astype(BF16) for h in halves])

    early_names = ("w_out", "w_ffn_up", "w_ffn_down", "ffn_conv_w")
    e_mine, e_send = core_halves([d_wout.reshape(N_DEV, D_MODEL // N_DEV, D_MODEL), _to_blocks_cols(d_wup),
                                  d_wdown.reshape(N_DEV, D_FF // N_DEV, D_MODEL), _to_blocks_cols(d_fconv)])
    (dog, dz, dos, d_gnw, d_snw), e_theirs = _attn_norm_bwd(og, proj, osb, gdn_norm_w, snw, dy,
                                                            carry=_swap_sibling(e_send))
    e_sums = [_add_halves(a, b, "grads_add_" + nm) for nm, a, b in zip(early_names, e_mine, e_theirs)]
    dqs, dks, dvs = _sb_bwd(proj, ctot, sb_nrun, dos)
    (du_, dw_, dqd_, dkd_, dqk_, dgl_), _ = _gdn_scan_bwd(cu, cw, cqd, ckd, cqk, cgl, ssave, dog)
    dqn, dkn, dvg, dbeta, dg = _gdn_chunk_bwd(qn, kn, vg, beta_b, g_b, ct, du_, dw_, dqd_, dkd_, dqk_, dgl_)
    (dqkv, dab, d_gconv, d_gsc), e_recv = _gdn_pre_bwd(proj, gconv_f, gdn_A_log, gdn_dt_bias, dqn, dkn, dvg, dbeta, dg,
                                                       carry=_exchange_chips(e_sums))
    dpieces = [dqkv, dz, dqs, dks, dvs, dab]
    doffs = [0, OFF_Z, OFF_SB, OFF_SB + 512, OFF_SB + 1024, OFF_AB]
    dw_qkv, dw_ab = _mm_tn_pieces(u, [dqkv, dab], "mm_dw_in_gdn")
    dw_z, dw_qs, dw_ks, dw_vs = _mm_tn_pieces(u, [dz, dqs, dks, dvs], "mm_dw_in_rest")
    du0 = _mm_nt_pieces(dpieces, doffs, w_inp, F32, "mm_du")
    d_win = jnp.concatenate([dw_qkv, dw_ab[:, :2 * GDN_HEADS], dw_z, dw_qs, dw_ks, dw_vs], axis=1)
    late_names = ("w_in", "gdn_conv_w")
    l_mine, l_send = core_halves([_to_blocks_cols(d_win), _to_blocks_cols(d_gconv)])
    l_theirs = _run_exchange(_swap_sibling(l_send), "grads_swap_sibling")
    l_sums = [_add_halves(a, b, "grads_add_" + nm) for nm, a, b in zip(late_names, l_mine, l_theirs)]
    (dh0, d_apre), l_recv = _prenorm_bwd(h0, attn_pre_norm, du0, dh1, carry=_exchange_chips(l_sums))
    grad_x = dh0[ROW0:ROW0 + seq][None]
    d_meta = dh0[PAD_ROWS:ROW0]

    small_grads = [d_apre, d_gsc[0:1, :GDN_HEADS], d_gsc[1:2, :GDN_HEADS], d_gnw, d_snw.reshape(1, SB_HEADS, SB_DH),
                   d_apost, d_fpre, d_fconvb, d_fpost]
    loss_rows = jnp.pad(loss_part, ((0, 7), (0, 0)))
    n_param_rows = _pack_small(small_grads).shape[0]
    n_small_rows = n_param_rows + loss_rows.shape[0]
    slab_parts = _gather_direct(
        [jnp.concatenate([_pack_small(small_grads), loss_rows, d_meta.reshape(-1, LANE)], axis=0)],
        name="gather_small_grads")[0]
    me = 4 * lax.axis_index("x") + 2 * lax.axis_index("y") + my_c
    meta_parts = lax.dynamic_index_in_dim(
        slab_parts[:, n_small_rows:].reshape(N_DEV, N_META, N_DEV, LANE), me, axis=2, keepdims=False)
    slab_parts = slab_parts[:, :n_small_rows]

    res = {}
    for nm, parts in zip(early_names + late_names + ("meta_tokens",), list(e_recv) + list(l_recv) + [meta_parts]):
        wloc = args[nm]
        shp = wloc.shape
        w2 = wloc.reshape(shp[-2], shp[-1])
        outs = _adamw(parts, w2, args["m_" + nm].reshape(w2.shape), args["v_" + nm].reshape(w2.shape), "adamw_" + nm)
        res[nm] = [o.reshape(shp) for o in outs]
    small_shapes = [args[nm].shape for nm in SMALL]
    with_loss_rows = lambda slab: jnp.pad(slab, ((0, n_small_rows - n_param_rows), (0, 0)))
    outs = _adamw(slab_parts, with_loss_rows(_pack_small([args[nm] for nm in SMALL])),
                  with_loss_rows(_pack_small([args["m_" + nm] for nm in SMALL])),
                  with_loss_rows(_pack_small([args["v_" + nm] for nm in SMALL])), "adamw_small")
    loss = outs[0][n_param_rows, 0]
    for k in range(4):
        for nm, val in zip(SMALL, _unpack_small(outs[k], small_shapes)):
            res.setdefault(nm, [None] * 4)[k] = val

    order = ("meta_tokens", "attn_pre_norm", "w_in", "gdn_conv_w", "gdn_A_log", "gdn_dt_bias", "gdn_norm_w",
             "sb_norm_w", "w_out", "attn_post_norm", "ffn_pre_norm", "w_ffn_up", "ffn_conv_w", "ffn_conv_b",
             "w_ffn_down", "ffn_post_norm")
    return (loss, grad_x, *[res[nm][0] for nm in order], *[res[nm][1] for nm in order],
            *[res[nm][2] for nm in order], *[res[nm][3] for nm in order])
```

```python
import functools

import jax
import jax.numpy as jnp
from jax import lax
from jax.experimental import pallas as pl
from jax.experimental.pallas import tpu as pltpu

F32 = jnp.float32
BF16 = jnp.bfloat16

D_MODEL = 1024
N_META = 16
GDN_HEADS = 4
GDN_D = 128
GDN_CHUNK = 64
GDN_CONV = 4
GDN_ROWS = 256
SCAN_CHUNKS = 4
SB_HEADS = 8
SB_DH = 64
SB_BLOCK = 128
D_FF = 2816
FFN_CONV = 3
NORM_EPS = 1e-6
L2_EPS = 1e-6
LANE = 128
N_DEV = 8

PAD_ROWS = SB_BLOCK - N_META
ROW0 = SB_BLOCK
SB_SPAN = 512
SB_DEAD = -104.0
SB_SUB = 256
SB_QTILE = 256
LP_ALIGN = 256

C_QKV = 3 * GDN_HEADS * GDN_D
C_Z = GDN_HEADS * GDN_D
C_SB = 3 * SB_HEADS * SB_DH
C_AB = 256
OFF_Z = C_QKV
OFF_SB = OFF_Z + C_Z
OFF_AB = OFF_SB + C_SB
D_INP = OFF_AB + C_AB
D_IN = C_QKV + 2 * GDN_HEADS + C_Z + C_SB

ADAM_LR = 0.001
ADAM_B1 = 0.9
ADAM_B2 = 0.999
ADAM_EPS = 1e-08
ADAM_WD = 0.01
ADAM_STEP = 10

VMEM_LIMIT = 56 * 1024 * 1024
ELEMWISE_VMEM = 8 * 1024 * 1024
MESH = pl.DeviceIdType.MESH


def _cp(sem=None):
    kw = dict(vmem_limit_bytes=VMEM_LIMIT)
    if sem is not None:
        kw["dimension_semantics"] = sem
    return pltpu.CompilerParams(**kw)


def _tile(n, cap, unit=128):
    best = None
    t = unit
    while t <= min(n, cap):
        if n % t == 0:
            best = t
        t += unit
    assert best is not None, (n, cap, unit)
    return best


def _dot(a, b):
    return jnp.dot(a, b, preferred_element_type=F32)


def _dot_nt(a, b):
    return lax.dot_general(a, b, (((1,), (1,)), ((), ())), preferred_element_type=F32)


def _dot_tn(a, b):
    return lax.dot_general(a, b, (((0,), (0,)), ((), ())), preferred_element_type=F32)


def _split(x):
    hi = x.astype(BF16)
    lo = (x - hi.astype(F32)).astype(BF16)
    return hi, lo


def _dot1(a, b, f=_dot):
    return f(a.astype(BF16), b.astype(BF16))


def _dot3(a, b, f=_dot):
    ah, al = _split(a)
    bh, bl = _split(b)
    return f(ah, bh) + (f(ah, bl) + f(al, bh))


def _dot_exact_l(m_bf16, x, f=_dot):
    xh, xl = _split(x)
    return f(m_bf16, xh) + f(m_bf16, xl)


def _dot_exact_r(x, m_bf16, f=_dot):
    xh, xl = _split(x)
    return f(xh, m_bf16) + f(xl, m_bf16)


def _iota2(shape, dim):
    return lax.broadcasted_iota(jnp.int32, shape, dim)


def _sigmoid(x):
    return 0.5 * jnp.tanh(0.5 * x) + 0.5


def _softplus(x):
    return jnp.maximum(x, 0.0) + jnp.log(1.0 + jnp.exp(-jnp.abs(x)))


def _colsum(x):
    return jnp.sum(x, axis=0, keepdims=True)


def _rowsum(x):
    return jnp.sum(x, axis=-1, keepdims=True)


def _mm(a, b, out_dtype, name):
    M, K = a.shape
    K2, N = b.shape
    assert K == K2
    tm = _tile(M, 768)
    tn = _tile(N, max(128, (6 * 1024 * 1024) // (2 * K)))

    def body(a_ref, b_ref, o_ref):
        o_ref[...] = _dot(a_ref[...].astype(BF16), b_ref[...].astype(BF16)).astype(o_ref.dtype)

    return pl.pallas_call(
        body, name=name, grid=(N // tn, M // tm),
        in_specs=[pl.BlockSpec((tm, K), lambda j, i: (i, 0)), pl.BlockSpec((K, tn), lambda j, i: (0, j))],
        out_specs=pl.BlockSpec((tm, tn), lambda j, i: (i, j)),
        out_shape=jax.ShapeDtypeStruct((M, N), out_dtype),
        compiler_params=_cp(("parallel", "parallel")),
    )(a, b)


def _mm_nt(a, b, out_dtype, name):
    M, K = a.shape
    N, K2 = b.shape
    assert K == K2
    tm = _tile(M, 768)
    tn = _tile(N, max(128, (6 * 1024 * 1024) // (2 * K)))

    def body(a_ref, b_ref, o_ref):
        o_ref[...] = _dot_nt(a_ref[...].astype(BF16), b_ref[...].astype(BF16)).astype(o_ref.dtype)

    return pl.pallas_call(
        body, name=name, grid=(N // tn, M // tm),
        in_specs=[pl.BlockSpec((tm, K), lambda j, i: (i, 0)), pl.BlockSpec((tn, K), lambda j, i: (j, 0))],
        out_specs=pl.BlockSpec((tm, tn), lambda j, i: (i, j)),
        out_shape=jax.ShapeDtypeStruct((M, N), out_dtype),
        compiler_params=_cp(("parallel", "parallel")),
    )(a, b)


def _mm_nt_pieces(pieces, offsets, b, out_dtype, name):
    M = pieces[0].shape[0]
    N = b.shape[0]
    n = len(pieces)
    widths = [p.shape[1] for p in pieces]
    assert all(off % k == 0 for off, k in zip(offsets, widths))
    tm = _tile(M, 768)
    tn = _tile(N, 1024)

    def body(*refs):
        acc = _dot_nt(refs[0][...].astype(BF16), refs[n][...].astype(BF16))
        for p in range(1, n):
            acc = acc + _dot_nt(refs[p][...].astype(BF16), refs[n + p][...].astype(BF16))
        refs[2 * n][...] = acc.astype(out_dtype)

    return pl.pallas_call(
        body, name=name, grid=(N // tn, M // tm),
        in_specs=[pl.BlockSpec((tm, k), lambda j, i: (i, 0)) for k in widths]
        + [pl.BlockSpec((tn, k), functools.partial(lambda j, i, blk: (j, blk), blk=off // k))
           for off, k in zip(offsets, widths)],
        out_specs=pl.BlockSpec((tm, tn), lambda j, i: (i, j)),
        out_shape=jax.ShapeDtypeStruct((M, N), out_dtype),
        compiler_params=_cp(("parallel", "parallel")),
    )(*pieces, *([b] * n))


def _mm_tn_pieces(a, pieces, name):
    M, K = a.shape
    n = len(pieces)
    tm = _tile(M, 768)

    def body(*refs):
        @pl.when(pl.program_id(0) == 0)
        def _():
            for p in range(n):
                refs[1 + n + p][...] = jnp.zeros_like(refs[1 + n + p])
        at = refs[0][...].astype(BF16)
        for p in range(n):
            refs[1 + n + p][...] += _dot_tn(at, refs[1 + p][...].astype(BF16))

    return pl.pallas_call(
        body, name=name, grid=(M // tm,),
        in_specs=[pl.BlockSpec((tm, K), lambda m: (m, 0))] + [pl.BlockSpec((tm, p.shape[1]), lambda m: (m, 0)) for p in pieces],
        out_specs=[pl.BlockSpec((K, p.shape[1]), lambda m: (0, 0)) for p in pieces],
        out_shape=[jax.ShapeDtypeStruct((K, p.shape[1]), F32) for p in pieces],
        compiler_params=_cp(("arbitrary",)),
    )(a, *pieces)


def _mm_tn(a, b, name):
    M, K = a.shape
    M2, N = b.shape
    assert M == M2
    tm = _tile(M, 768)
    tk = _tile(K, 2816)
    tn = _tile(N, 2816)

    def body(a_ref, b_ref, o_ref):
        @pl.when(pl.program_id(2) == 0)
        def _():
            o_ref[...] = jnp.zeros_like(o_ref)
        o_ref[...] += _dot_tn(a_ref[...].astype(BF16), b_ref[...].astype(BF16))

    return pl.pallas_call(
        body, name=name, grid=(K // tk, N // tn, M // tm),
        in_specs=[pl.BlockSpec((tm, tk), lambda i, j, m: (m, i)), pl.BlockSpec((tm, tn), lambda i, j, m: (m, j))],
        out_specs=pl.BlockSpec((tk, tn), lambda i, j, m: (i, j)),
        out_shape=jax.ShapeDtypeStruct((K, N), F32),
        compiler_params=_cp(("parallel", "parallel", "arbitrary")),
    )(a, b)


def _rms(x):
    return lax.rsqrt(jnp.mean(x * x, axis=-1, keepdims=True) + NORM_EPS)


def _rms_bwd(x, w, dy):
    r = _rms(x)
    n = x * r
    dyw = dy * w
    dx = r * (dyw - n * jnp.mean(dyw * n, axis=-1, keepdims=True))
    return dx, dy * n


def _build_rows(x, meta, target, LP, carry=None):
    seq, D = x.shape
    T = SB_BLOCK
    nx = seq // T
    assert seq % T == 0 and meta.shape[0] == N_META

    def body(x_ref, m_ref, t_ref, h_ref, tp_ref):
        i = pl.program_id(0)
        inside = (i >= 1) & (i <= nx)
        head = jnp.concatenate([jnp.zeros((PAD_ROWS, D), F32), m_ref[...]], axis=0)
        h_ref[...] = jnp.where(i == 0, head, jnp.where(inside, x_ref[...], 0.0))
        tp_ref[...] = jnp.where(inside, t_ref[...], 0.0)

    tok = pl.BlockSpec((T, D), lambda i: (jnp.clip(i - 1, 0, nx - 1), 0))
    row = pl.BlockSpec((T, D), lambda i: (i, 0))
    out = jax.ShapeDtypeStruct((LP, D), F32)
    return _call_carrying(
        carry, body, LP // T, name="build_rows",
        in_specs=[tok, pl.BlockSpec((N_META, D), lambda i: (0, 0)), tok], out_specs=[row, row], out_shape=[out, out],
        operands=(x, meta, target))


def _prenorm_fwd(h0, w, carry=None):
    LP, D = h0.shape
    T = _tile(LP, 512)

    def body(h_ref, w_ref, u_ref):
        h = h_ref[...]
        u_ref[...] = (h * _rms(h) * w_ref[...]).astype(BF16)

    return _call_carrying(
        carry, body, LP // T, name="prenorm_fwd",
        in_specs=[pl.BlockSpec((T, D), lambda i: (i, 0)), pl.BlockSpec((1, D), lambda i: (0, 0))],
        out_specs=[pl.BlockSpec((T, D), lambda i: (i, 0))],
        out_shape=[jax.ShapeDtypeStruct((LP, D), BF16)],
        operands=(h0, w))


def _prenorm_bwd(h0, w, du, dh1, carry=None):
    LP, D = h0.shape
    T = _tile(LP, 512)

    def body(h_ref, w_ref, du_ref, dh1_ref, dh0_ref, dw_ref):
        @pl.when(pl.program_id(0) == 0)
        def _():
            dw_ref[...] = jnp.zeros_like(dw_ref)
        dx, dwn = _rms_bwd(h_ref[...], w_ref[...], du_ref[...])
        dh0_ref[...] = dh1_ref[...] + dx
        dw_ref[...] += _colsum(dwn)

    row = pl.BlockSpec((T, D), lambda i: (i, 0))
    vec = pl.BlockSpec((1, D), lambda i: (0, 0))
    return _call_carrying(
        carry, body, LP // T, name="prenorm_bwd",
        in_specs=[row, vec, row, row], out_specs=[row, vec],
        out_shape=[jax.ShapeDtypeStruct((LP, D), F32), jax.ShapeDtypeStruct((1, D), F32)],
        operands=(h0, w, du, dh1))


def _gdn_gate_consts(alog_ref, dtb_ref, h):
    a_coef = -jnp.exp(alog_ref[0:1, h:h + 1])
    return a_coef, dtb_ref[0:1, h:h + 1]


def _gdn_pre_fwd(proj, conv_w, a_log, dt_bias, carry=None):
    LP = proj.shape[0]
    T = _tile(LP, 256)
    C = C_QKV
    H = GDN_HEADS

    def body(x_ref, halo_ref, ab_ref, cw_ref, alog_ref, dtb_ref, q_ref, k_ref, v_ref, beta_ref, g_ref):
        i = pl.program_id(0)

        def conv_silu(cols):
            ext = jnp.concatenate([jnp.where(i > 0, halo_ref[:, cols], 0.0), x_ref[:, cols]], axis=0)
            w = cw_ref[:, cols]
            y = w[GDN_CONV - 1:GDN_CONV] * ext[8:]
            for j in range(GDN_CONV - 1):
                y = y + w[j:j + 1] * pltpu.roll(ext, GDN_CONV - 1 - j, 0)[8:]
            return y * _sigmoid(y)

        for h in range(H):
            sl = slice(h * GDN_D, (h + 1) * GDN_D)
            cq = conv_silu(sl)
            q_ref[:, sl] = cq * lax.rsqrt(_rowsum(cq * cq) + L2_EPS) * (GDN_D ** -0.5)
            ck = conv_silu(slice(512 + h * GDN_D, 512 + (h + 1) * GDN_D))
            k_ref[:, sl] = ck * lax.rsqrt(_rowsum(ck * ck) + L2_EPS)
            v_ref[:, sl] = conv_silu(slice(1024 + h * GDN_D, 1024 + (h + 1) * GDN_D))
        ab = ab_ref[...]
        valid = (i * T + _iota2((T, 1), 0)) >= PAD_ROWS
        for h in range(H):
            sl = slice(h * GDN_D, (h + 1) * GDN_D)
            a_coef, dtb = _gdn_gate_consts(alog_ref, dtb_ref, h)
            g = jnp.where(valid, a_coef * _softplus(ab[:, h:h + 1] + dtb), 0.0)
            beta = jnp.where(valid, _sigmoid(ab[:, H + h:H + h + 1]), 0.0)
            g_ref[:, sl] = jnp.broadcast_to(g, (T, GDN_D))
            beta_ref[:, sl] = jnp.broadcast_to(beta, (T, GDN_D))

    t8 = T // 8
    row512 = pl.BlockSpec((T, 512), lambda i: (i, 0))
    small = lambda r, c: pl.BlockSpec((r, c), lambda i: (0, 0))
    out = jax.ShapeDtypeStruct((LP, 512), F32)
    return _call_carrying(
        carry, body, LP // T, name="gdn_pre_fwd",
        in_specs=[pl.BlockSpec((T, C), lambda i: (i, 0)),
                  pl.BlockSpec((8, C), lambda i: (jnp.maximum(i * t8 - 1, 0), 0)),
                  pl.BlockSpec((T, C_AB), lambda i: (i, OFF_AB // C_AB)),
                  small(GDN_CONV, C), small(1, H), small(1, H)],
        out_specs=[row512] * 5, out_shape=[out] * 5,
        operands=(proj, proj, proj, conv_w, a_log, dt_bias))


def _gdn_pre_bwd(proj, conv_w, a_log, dt_bias, dq, dk, dv, dbeta, dg, carry=None):
    LP = proj.shape[0]
    T = _tile(LP, 256)
    C = C_QKV
    H = GDN_HEADS
    TE = T + 8
    nt = LP // T

    def body(x_ref, xp_ref, xn_ref, ab_ref, cw_ref, alog_ref, dtb_ref,
             dq_ref, dqn_ref, dk_ref, dkn_ref, dv_ref, dvn_ref, dbeta_ref, dg_ref,
             dx_ref, dab_ref, dcw_ref, dsc_ref):
        i = pl.program_id(0)

        @pl.when(i == 0)
        def _():
            dcw_ref[...] = jnp.zeros_like(dcw_ref)
            dsc_ref[...] = jnp.zeros_like(dsc_ref)

        last = i == nt - 1

        def strip(cols, d_ref, dn_ref, dcols, scale):
            ext = jnp.concatenate([jnp.where(i > 0, xp_ref[:, cols], 0.0), x_ref[:, cols],
                                   jnp.where(last, 0.0, xn_ref[:, cols])], axis=0)
            sh = [ext[8:8 + TE]] + [pltpu.roll(ext, s, 0)[8:8 + TE] for s in range(1, GDN_CONV)]
            w = cw_ref[:, cols]
            y = w[GDN_CONV - 1:GDN_CONV] * sh[0]
            for j in range(GDN_CONV - 1):
                y = y + w[j:j + 1] * sh[GDN_CONV - 1 - j]
            sg = _sigmoid(y)
            d = jnp.concatenate([d_ref[:, dcols], jnp.where(last, 0.0, dn_ref[:, dcols])], axis=0)
            if scale is not None:
                c = y * sg
                r = lax.rsqrt(_rowsum(c * c) + L2_EPS)
                n = c * r
                d = scale * r * (d - n * _rowsum(d * n))
            dy = d * (sg * (1.0 + y * (1.0 - sg)))
            dy_t = dy[0:T]
            for j in range(GDN_CONV):
                dcw_ref[j:j + 1, cols] += _colsum(dy_t * sh[GDN_CONV - 1 - j][0:T])
            dx = w[GDN_CONV - 1:GDN_CONV] * dy_t
            for j in range(GDN_CONV - 1):
                dx = dx + w[j:j + 1] * pltpu.roll(dy, TE - (GDN_CONV - 1 - j), 0)[0:T]
            dx_ref[:, cols] = dx.astype(BF16)

        for h in range(H):
            sl = slice(h * GDN_D, (h + 1) * GDN_D)
            strip(sl, dq_ref, dqn_ref, sl, GDN_D ** -0.5)
            strip(slice(512 + h * GDN_D, 512 + (h + 1) * GDN_D), dk_ref, dkn_ref, sl, 1.0)
            strip(slice(1024 + h * GDN_D, 1024 + (h + 1) * GDN_D), dv_ref, dvn_ref, sl, None)
        ab = ab_ref[...]
        valid = (i * T + _iota2((T, 1), 0)) >= PAD_ROWS
        lane = _iota2((T, C_AB), 1)
        lane1 = _iota2((1, 128), 1)
        dab = jnp.zeros((T, C_AB), F32)
        dsc_a = jnp.zeros((1, 128), F32)
        dsc_d = jnp.zeros((1, 128), F32)
        for h in range(H):
            a_coef, dtb = _gdn_gate_consts(alog_ref, dtb_ref, h)
            pre = ab[:, h:h + 1] + dtb
            dgh = jnp.where(valid, dg_ref[:, h * GDN_D:h * GDN_D + 1], 0.0)
            da = dgh * a_coef * _sigmoid(pre)
            beta = _sigmoid(ab[:, H + h:H + h + 1])
            db = jnp.where(valid, dbeta_ref[:, h * GDN_D:h * GDN_D + 1], 0.0) * beta * (1.0 - beta)
            dab = dab + jnp.where(lane == h, da, 0.0) + jnp.where(lane == H + h, db, 0.0)
            dsc_a = dsc_a + jnp.where(lane1 == h, _colsum(dgh * a_coef * _softplus(pre)), 0.0)
            dsc_d = dsc_d + jnp.where(lane1 == h, _colsum(da), 0.0)
        dab_ref[...] = dab.astype(BF16)
        dsc_ref[0:1, :] += dsc_a
        dsc_ref[1:2, :] += dsc_d

    t8 = T // 8
    nb8 = LP // 8
    prev8 = lambda w: pl.BlockSpec((8, w), lambda i: (jnp.maximum(i * t8 - 1, 0), 0))
    next8 = lambda w: pl.BlockSpec((8, w), lambda i: (jnp.minimum((i + 1) * t8, nb8 - 1), 0))
    row = lambda w: pl.BlockSpec((T, w), lambda i: (i, 0))
    small = lambda r, c: pl.BlockSpec((r, c), lambda i: (0, 0))
    return _call_carrying(
        carry, body, nt, name="gdn_pre_bwd",
        in_specs=[row(C), prev8(C), next8(C), pl.BlockSpec((T, C_AB), lambda i: (i, OFF_AB // C_AB)),
                  small(GDN_CONV, C), small(1, H), small(1, H),
                  row(512), next8(512), row(512), next8(512), row(512), next8(512), row(512), row(512)],
        out_specs=[row(C), row(C_AB), small(GDN_CONV, C), small(2, 128)],
        out_shape=[jax.ShapeDtypeStruct((LP, C), BF16), jax.ShapeDtypeStruct((LP, C_AB), BF16),
                   jax.ShapeDtypeStruct((GDN_CONV, C), F32), jax.ShapeDtypeStruct((2, 128), F32)],
        operands=(proj, proj, proj, proj, conv_w, a_log, dt_bias, dq, dq, dk, dk, dv, dv, dbeta, dg))


def _tri_masks():
    r = _iota2((GDN_CHUNK, GDN_CHUNK), 0)
    c = _iota2((GDN_CHUNK, GDN_CHUNK), 1)
    return r >= c, r > c


def _gdn_chunk_common(q, k, v, beta, gb):
    incl, strict = _tri_masks()
    l_incl = incl.astype(BF16)
    gd = _dot_exact_l(l_incl, jnp.where(strict, gb[:, :GDN_CHUNK], 0.0))
    gc = _dot_exact_l(l_incl, gb)
    decay = jnp.where(incl, jnp.exp(jnp.where(incl, gd, 0.0)), 0.0)
    exp_g = jnp.exp(gc)
    g_last = gc[GDN_CHUNK - 1:GDN_CHUNK, :]
    kd_fac = jnp.exp(g_last - gc)
    gl = jnp.exp(g_last)
    kb = k * beta
    kk = _dot1(kb, k, _dot_nt)
    return dict(incl=incl, strict=strict, decay=decay, exp_g=exp_g, kd_fac=kd_fac, gl=gl, kb=kb, kk=kk,
                vb=v * beta, kbg=kb * exp_g)


def _interleave(gens):
    gens = list(gens)
    while gens:
        alive = []
        for g in gens:
            try:
                next(g)
                alive.append(g)
            except StopIteration:
                pass
        gens = alive


def _call_carrying(ex, body, nsteps, *, name, in_specs, out_specs, out_shape, operands, scratch_shapes=()):
    n_in, n_out, n_scr = len(in_specs), len(out_specs), len(scratch_shapes)
    n = ex.n if ex is not None else 0

    def full(*refs):
        o0 = n_in + n
        s0 = o0 + n_out + n
        ex_refs = (refs[n_in:o0], refs[o0 + n_out:s0], refs[s0 + n_scr:])
        step = pl.program_id(0)
        _carry_begin(ex, ex_refs, step, nsteps)
        body(*refs[:n_in], *refs[o0:o0 + n_out], *refs[s0:s0 + n_scr])
        _carry_end(ex, ex_refs, step, nsteps)

    res = pl.pallas_call(
        full, name=name, grid=(nsteps,),
        in_specs=list(in_specs) + [ANY_SPEC] * n, out_specs=list(out_specs) + [ANY_SPEC] * n,
        out_shape=list(out_shape) + (ex.out_shapes if ex is not None else []),
        scratch_shapes=list(scratch_shapes) + (ex.scratch if ex is not None else []),
        compiler_params=pltpu.CompilerParams(dimension_semantics=("arbitrary",), vmem_limit_bytes=VMEM_LIMIT,
                                             has_side_effects=ex is not None),
    )(*operands, *(ex.arrs if ex is not None else []))
    return list(res[:n_out]), list(res[n_out:])


def _gdn_chunk_fwd(qn, kn, v, beta_b, g_b, carry=None):
    LP = qn.shape[0]
    R = GDN_ROWS
    H = GDN_HEADS
    CH = GDN_CHUNK

    def body(q_ref, k_ref, v_ref, b_ref, g_ref, u_ref, w_ref, qd_ref, kd_ref, qk_ref, t_ref, gl_ref):
        def item(cc, h):
            rs = slice(cc * CH, (cc + 1) * CH)
            sl = slice(h * GDN_D, (h + 1) * GDN_D)
            s64 = slice(h * CH, (h + 1) * CH)
            q, k = q_ref[rs, sl], k_ref[rs, sl]
            m = _gdn_chunk_common(q, k, v_ref[rs, sl], b_ref[rs, sl], g_ref[rs, sl])
            qk_raw = _dot1(q, k, _dot_nt)
            yield
            a = jnp.where(m["strict"], m["kk"] * m["decay"], 0.0)
            eye = (_iota2((CH, CH), 0) == _iota2((CH, CH), 1)).astype(F32)
            t = eye - a
            p = _dot3(a, a)
            yield
            for _ in range(4):
                t = t + _dot3(t, p)
                p = _dot3(p, p)
                yield
            t = t + _dot3(t, p)
            yield
            u_ref[rs, sl] = _dot1(t, m["vb"])
            w_ref[rs, sl] = _dot1(t, m["kbg"])
            qk_ref[rs, s64] = qk_raw * m["decay"]
            t_ref[rs, s64] = t
            qd_ref[rs, sl] = q * m["exp_g"]
            kd_ref[rs, sl] = k * m["kd_fac"]
            gl_ref[cc * 8:(cc + 1) * 8, sl] = jnp.broadcast_to(m["gl"], (8, GDN_D))

        _interleave(item(cc, h) for cc in range(R // CH) for h in range(H))

    row = lambda w: pl.BlockSpec((R, w), lambda i: (i, 0))
    o512 = jax.ShapeDtypeStruct((LP, 512), F32)
    o256 = jax.ShapeDtypeStruct((LP, 256), F32)
    return _call_carrying(
        carry, body, LP // R, name="gdn_chunk_fwd",
        in_specs=[row(512)] * 5,
        out_specs=[row(512)] * 4 + [row(256)] * 2 + [pl.BlockSpec((R // 8, 512), lambda i: (i, 0))],
        out_shape=[o512] * 4 + [o256] * 2 + [jax.ShapeDtypeStruct((LP // 8, 512), F32)],
        operands=(qn, kn, v, beta_b, g_b))


def _gdn_chunk_bwd(qn, kn, v, beta_b, g_b, t_all, du, dw, dqd, dkd, dqk, dgl):
    LP = qn.shape[0]
    R = GDN_ROWS
    H = GDN_HEADS
    CH = GDN_CHUNK

    def body(q_ref, k_ref, v_ref, b_ref, g_ref, t_ref, du_ref, dw_ref, dqd_ref, dkd_ref, dqk_ref, dgl_ref,
             dq_ref, dk_ref, dv_ref, db_ref, dg_ref):
        ones = jnp.ones((CH, GDN_D), BF16)

        def item(cc, h):
            rs = slice(cc * CH, (cc + 1) * CH)
            sl = slice(h * GDN_D, (h + 1) * GDN_D)
            s64 = slice(h * CH, (h + 1) * CH)
            q, k, vv, beta = q_ref[rs, sl], k_ref[rs, sl], v_ref[rs, sl], b_ref[rs, sl]
            m = _gdn_chunk_common(q, k, vv, beta, g_ref[rs, sl])
            incl, strict, decay = m["incl"], m["strict"], m["decay"]
            t = t_ref[rs, s64]
            du_, dw_ = du_ref[rs, sl], dw_ref[rs, sl]
            dqd_, dkd_ = dqd_ref[rs, sl], dkd_ref[rs, sl]
            d_t = _dot1(du_, m["vb"], _dot_nt) + _dot1(dw_, m["kbg"], _dot_nt)
            dvb = _dot1(t, du_, _dot_tn)
            dkbg = _dot1(t, dw_, _dot_tn)
            qk_raw = _dot1(q, k, _dot_nt)
            yield
            x1 = _dot3(d_t, t, _dot_nt)
            dkb = dkbg * m["exp_g"]
            d_gi = _rowsum(dkbg * m["kbg"])
            yield
            d_a = jnp.where(strict, -_dot3(t, x1, _dot_tn), 0.0)
            yield
            d_kk = d_a * decay
            dqk_m = jnp.where(incl, dqk_ref[rs, s64], 0.0)
            dqk_raw = dqk_m * decay
            mm = (d_a * m["kk"] + dqk_m * qk_raw) * decay
            dkb = dkb + _dot1(d_kk, k)
            dk_ = _dot1(d_kk, m["kb"], _dot_tn) + _dot1(dqk_raw, q, _dot_tn)
            dq_ = _dot1(dqk_raw, k) + dqd_ * m["exp_g"]
            d_gi = d_gi + (_dot_exact_r(mm, ones) - _dot_exact_r(mm, ones, _dot_tn))
            yield
            d_gi = d_gi + _rowsum(dqd_ * q * m["exp_g"])
            e = _rowsum(dkd_ * k * m["kd_fac"])
            d_gi = d_gi - e
            d_glast = _colsum(jnp.broadcast_to(e, (CH, GDN_D))) + dgl_ref[cc * 8:cc * 8 + 1, sl] * m["gl"]
            dk_ = dk_ + dkd_ * m["kd_fac"] + dkb * beta
            d_gi = d_gi + jnp.where(_iota2((CH, GDN_D), 0) == CH - 1, d_glast, 0.0)
            u_incl = (_iota2((CH, CH), 1) >= _iota2((CH, CH), 0)).astype(BF16)
            dq_ref[rs, sl] = dq_
            dk_ref[rs, sl] = dk_
            dv_ref[rs, sl] = dvb * beta
            db_ref[rs, sl] = jnp.broadcast_to(_rowsum(dvb * vv) + _rowsum(dkb * k), (CH, GDN_D))
            dg_ref[rs, sl] = _dot_exact_l(u_incl, d_gi)

        _interleave(item(cc, h) for cc in range(R // CH) for h in range(H))

    row = lambda w: pl.BlockSpec((R, w), lambda i: (i, 0))
    o512 = jax.ShapeDtypeStruct((LP, 512), F32)
    gl_spec = pl.BlockSpec((R // 8, 512), lambda i: (i, 0))
    return pl.pallas_call(
        body, name="gdn_chunk_bwd", grid=(LP // R,),
        in_specs=[row(512)] * 5 + [row(256)] + [row(512)] * 4 + [row(256), gl_spec],
        out_specs=[row(512)] * 5, out_shape=[o512] * 5,
        compiler_params=_cp(("parallel",)),
    )(qn, kn, v, beta_b, g_b, t_all, du, dw, dqd, dkd, dqk, dgl)


def _gdn_scan_fwd(u, w, qd, kd, qk, gl):
    LP = u.shape[0]
    CH = GDN_CHUNK
    CPS = SCAN_CHUNKS
    N = LP // CH
    NS = N // CPS
    H = GDN_HEADS

    def body(u_ref, w_ref, qd_ref, kd_ref, qk_ref, gl_ref, o_ref, ssave_ref, s_sc):
        @pl.when(pl.program_id(0) == 0)
        def _():
            s_sc[...] = jnp.zeros_like(s_sc)

        for cc in range(CPS):
            rs = slice(cc * CH, (cc + 1) * CH)
            ssave_ref[cc * GDN_D:(cc + 1) * GDN_D, :] = s_sc[...]

            def item(h):
                sl = slice(h * GDN_D, (h + 1) * GDN_D)
                s = s_sc[:, sl]
                v_new = u_ref[rs, sl] - _dot1(w_ref[rs, sl], s)
                o_s = _dot1(qd_ref[rs, sl], s)
                yield
                o_ref[rs, sl] = o_s + _dot1(qk_ref[rs, h * CH:(h + 1) * CH], v_new)
                s_sc[:, sl] = s * gl_ref[cc * 8:cc * 8 + 1, sl] + _dot1(kd_ref[rs, sl], v_new, _dot_tn)

            _interleave(item(h) for h in range(H))

    row = lambda w_: pl.BlockSpec((CPS * CH, w_), lambda n: (n, 0))
    return pl.pallas_call(
        body, name="gdn_scan_fwd", grid=(NS,),
        in_specs=[row(512)] * 4 + [row(256), pl.BlockSpec((CPS * 8, 512), lambda n: (n, 0))],
        out_specs=[row(512), pl.BlockSpec((CPS * GDN_D, 512), lambda n: (n, 0))],
        out_shape=[jax.ShapeDtypeStruct((LP, 512), F32), jax.ShapeDtypeStruct((N * GDN_D, 512), F32)],
        scratch_shapes=[pltpu.VMEM((GDN_D, 512), F32)],
        compiler_params=_cp(("arbitrary",)),
    )(u, w, qd, kd, qk, gl)


def _gdn_scan_bwd(u, w, qd, kd, qk, gl, ssave, do, carry=None):
    LP = u.shape[0]
    CH = GDN_CHUNK
    CPS = SCAN_CHUNKS
    N = LP // CH
    NS = N // CPS
    H = GDN_HEADS

    def body(u_ref, w_ref, qd_ref, kd_ref, qk_ref, gl_ref, s_ref, do_ref,
             du_ref, dw_ref, dqd_ref, dkd_ref, dqk_ref, dgl_ref, ds_sc):
        @pl.when(pl.program_id(0) == 0)
        def _():
            ds_sc[...] = jnp.zeros_like(ds_sc)

        for cc in reversed(range(CPS)):
            rs = slice(cc * CH, (cc + 1) * CH)
            r8 = slice(cc * 8, (cc + 1) * 8)

            def item(h):
                sl = slice(h * GDN_D, (h + 1) * GDN_D)
                s64 = slice(h * CH, (h + 1) * CH)
                s = s_ref[cc * GDN_D:(cc + 1) * GDN_D, sl]
                ds = ds_sc[:, sl]
                do_ = do_ref[rs, sl]
                w_, qd_, kd_, qk_ = w_ref[rs, sl], qd_ref[rs, sl], kd_ref[rs, sl], qk_ref[rs, s64]
                v_new = u_ref[rs, sl] - _dot1(w_, s)
                d_vnew = _dot1(qk_, do_, _dot_tn) + _dot1(kd_, ds)
                dqd_ref[rs, sl] = _dot1(do_, s, _dot_nt)
                ds_new = ds * gl_ref[cc * 8:cc * 8 + 1, sl] + _dot1(qd_, do_, _dot_tn)
                dgl_ref[r8, sl] = jnp.broadcast_to(jnp.sum(_colsum(ds * s), axis=-1, keepdims=True), (8, GDN_D))
                yield
                du_ref[rs, sl] = d_vnew
                dw_ref[rs, sl] = -_dot1(d_vnew, s, _dot_nt)
                dkd_ref[rs, sl] = _dot1(v_new, ds, _dot_nt)
                dqk_ref[rs, s64] = _dot1(do_, v_new, _dot_nt)
                ds_sc[:, sl] = ds_new - _dot1(w_, d_vnew, _dot_tn)

            _interleave(item(h) for h in range(H))

    rev = lambda w_: pl.BlockSpec((CPS * CH, w_), lambda n: (NS - 1 - n, 0))
    rev8 = pl.BlockSpec((CPS * 8, 512), lambda n: (NS - 1 - n, 0))
    o512 = jax.ShapeDtypeStruct((LP, 512), F32)
    return _call_carrying(
        carry, body, NS, name="gdn_scan_bwd",
        in_specs=[rev(512)] * 4 + [rev(256), rev8, pl.BlockSpec((CPS * GDN_D, 512), lambda n: (NS - 1 - n, 0)),
                  rev(512)],
        out_specs=[rev(512)] * 4 + [rev(256), rev8],
        out_shape=[o512] * 4 + [jax.ShapeDtypeStruct((LP, 256), F32), jax.ShapeDtypeStruct((LP // 8, 512), F32)],
        scratch_shapes=[pltpu.VMEM((GDN_D, 512), F32)],
        operands=(u, w, qd, kd, qk, gl, ssave, do))


def _sb_scores(qh, kblk, mask):
    z = _dot_nt(qh, kblk)
    e = jnp.exp(-jnp.abs(z))
    sp = jnp.maximum(z, 0.0) + jnp.log(1.0 + e)
    return z, e, jnp.where(mask, -sp, 0.0), z - sp


def _sb_fwd(proj):
    LP = proj.shape[0]
    B = SB_BLOCK
    W = min(SB_SPAN, LP)
    SUB = SB_SUB
    Q = min(SB_QTILE, LP)
    nq = LP // Q
    nsub = W // SUB
    scale = SB_DH ** -0.5
    qcol, kcol, vcol = OFF_SB // B, (OFF_SB + 512) // B, (OFF_SB + 1024) // B

    def body(q_ref, k_ref, v_ref, tri_ref, o_ref, c_ref, n_ref):
        i = pl.program_id(1)
        lane = _iota2((Q, B), 1)
        head_a = lane < SB_DH
        qs = q_ref[...] * scale
        qh = [jnp.where(head_a, qs, 0.0).astype(BF16), jnp.where(head_a, 0.0, qs).astype(BF16)]
        u_strict = tri_ref[...]
        qpos = i * Q + _iota2((Q, W), 0)
        hi0 = (i + 1) * Q
        nspan = (hi0 + W - 1) // W

        def live(st):
            return (st[0] < nspan) & (st[1] > 0)

        def span(st):
            r, carry = st[0], st[2:]
            hi = hi0 - r * W
            k0 = pl.multiple_of(jnp.maximum(hi - W, 0), B)
            kblk = k_ref[pl.ds(k0, W), :].astype(BF16)
            vblk = v_ref[pl.ds(k0, W), :].astype(BF16)
            kpos = k0 + _iota2((Q, W), 1)
            mask = (kpos < qpos) & (kpos >= PAD_ROWS) & (kpos < hi)
            new = [None] * 4

            def head(h):
                o_acc, c = carry[2 * h], carry[2 * h + 1]
                z, e, l1m, lsg = _sb_scores(qh[h], kblk, mask)
                yield
                subs = [slice(b * SUB, (b + 1) * SUB) for b in range(nsub)]
                suf = [_dot(l1m[:, bs].astype(BF16), u_strict) for bs in subs]
                yield
                parts = [None] * nsub
                for b in reversed(range(nsub)):
                    parts[b] = jnp.where(mask[:, subs[b]], jnp.exp(lsg[:, subs[b]] + suf[b] + c), 0.0)
                    c = c + _rowsum(l1m[:, subs[b]])
                att = jnp.concatenate(parts, axis=1).astype(BF16)
                new[2 * h], new[2 * h + 1] = o_acc + _dot(att, vblk), c

            _interleave(head(h) for h in range(2))
            more = (jnp.maximum(jnp.max(new[1]), jnp.max(new[3])) > SB_DEAD).astype(jnp.int32)
            return (r + 1, more, *new)

        zero_o = jnp.zeros((Q, B), F32)
        zero_c = jnp.zeros((Q, 1), F32)
        nrun, _, o_a, c_a, o_b, c_b = lax.while_loop(
            live, span, (jnp.int32(0), jnp.int32(1), zero_o, zero_c, zero_o, zero_c))
        o_ref[...] = jnp.where(head_a, o_a, o_b)
        c_ref[...] = jnp.where(head_a, c_a, c_b)
        n_ref[pl.program_id(0), i] = nrun

    blk = pl.BlockSpec((Q, B), lambda p, i: (i, p))
    out = jax.ShapeDtypeStruct((LP, 512), F32)
    return pl.pallas_call(
        body, name="sb_fwd", grid=(SB_HEADS // 2, nq),
        in_specs=[pl.BlockSpec((Q, B), lambda p, i: (i, qcol + p)),
                  pl.BlockSpec((LP, B), lambda p, i: (0, kcol + p)),
                  pl.BlockSpec((LP, B), lambda p, i: (0, vcol + p)),
                  pl.BlockSpec((SUB, SUB), lambda p, i: (0, 0))],
        out_specs=[blk, blk, pl.BlockSpec(memory_space=pltpu.SMEM)],
        out_shape=[out, out, jax.ShapeDtypeStruct((SB_HEADS // 2, nq), jnp.int32)],
        compiler_params=_cp(("arbitrary", "arbitrary")),
    )(proj, proj, proj, jnp.tril(jnp.ones((SUB, SUB), BF16), -1))


def _sb_bwd(proj, ctot, nrun_all, do):
    LP = proj.shape[0]
    B = SB_BLOCK
    W = min(SB_SPAN, LP)
    SUB = SB_SUB
    Q = min(SB_QTILE, LP)
    nq = LP // Q
    nsub = W // SUB
    scale = SB_DH ** -0.5
    qcol, kcol, vcol = OFF_SB // B, (OFF_SB + 512) // B, (OFF_SB + 1024) // B

    def body(n_ref, q_ref, k_ref, v_ref, c_ref, do_ref, tril_ref, triu_ref, dq_ref, dk_ref, dv_ref):
        i = pl.program_id(1)

        @pl.when(i == 0)
        def _():
            dk_ref[...] = jnp.zeros_like(dk_ref)
            dv_ref[...] = jnp.zeros_like(dv_ref)

        lane = _iota2((Q, B), 1)
        head_a = lane < SB_DH
        qs = q_ref[...] * scale
        qh = [jnp.where(head_a, qs, 0.0).astype(BF16), jnp.where(head_a, 0.0, qs).astype(BF16)]
        dof = do_ref[...]
        doh = [jnp.where(head_a, dof, 0.0).astype(BF16), jnp.where(head_a, 0.0, dof).astype(BF16)]
        cfull = c_ref[...]
        ctot_h = [cfull[:, 0:1], cfull[:, SB_DH:SB_DH + 1]]
        u_strict = tril_ref[...]
        l_strict = triu_ref[...]
        qpos = i * Q + _iota2((Q, W), 0)
        hi0 = (i + 1) * Q
        nrun = n_ref[pl.program_id(0), i]

        def span(t, carry):
            r = nrun - 1 - t
            hi = hi0 - r * W
            k0 = pl.multiple_of(jnp.maximum(hi - W, 0), B)
            kblk = k_ref[pl.ds(k0, W), :].astype(BF16)
            vblk = v_ref[pl.ds(k0, W), :].astype(BF16)
            kpos = k0 + _iota2((Q, W), 1)
            mask = (kpos < qpos) & (kpos >= PAD_ROWS) & (kpos < hi)
            new = [None] * 6
            dk_add, dv_add = [None, None], [None, None]
            subs = [slice(b * SUB, (b + 1) * SUB) for b in range(nsub)]

            def head(h):
                dq_acc, pre, ecar = carry[3 * h], carry[3 * h + 1], carry[3 * h + 2]
                z, e, l1m, lsg = _sb_scores(qh[h], kblk, mask)
                d_att = _dot_nt(doh[h], vblk)
                yield
                sig = jnp.exp(lsg)
                suf = [_dot(l1m[:, bs].astype(BF16), u_strict) for bs in subs]
                yield
                att_parts, p_parts = [None] * nsub, [None] * nsub
                for b, bs in enumerate(subs):
                    pre = pre + _rowsum(l1m[:, bs])
                    att_parts[b] = jnp.where(mask[:, bs], jnp.exp(lsg[:, bs] + suf[b] + (ctot_h[h] - pre)), 0.0)
                    p_parts[b] = att_parts[b] * d_att[:, bs]
                pcum = [_dot(p.astype(BF16), l_strict) for p in p_parts]
                yield
                dz_parts = [None] * nsub
                for b, bs in enumerate(subs):
                    sg = sig[:, bs]
                    dz_parts[b] = jnp.where(mask[:, bs], p_parts[b] * (1.0 - sg) - sg * (ecar + pcum[b]), 0.0)
                    ecar = ecar + _rowsum(p_parts[b])
                att = jnp.concatenate(att_parts, axis=1).astype(BF16)
                dz = jnp.concatenate(dz_parts, axis=1).astype(BF16)
                new[3 * h:3 * h + 3] = [dq_acc + _dot(dz, kblk), pre, ecar]
                dk_add[h] = _dot_tn(dz, qh[h])
                dv_add[h] = _dot_tn(att, doh[h])

            _interleave(head(h) for h in range(2))
            dk_ref[pl.ds(k0, W), :] += dk_add[0] + dk_add[1]
            dv_ref[pl.ds(k0, W), :] += dv_add[0] + dv_add[1]
            return tuple(new)

        zero_o = jnp.zeros((Q, B), F32)
        zero_c = jnp.zeros((Q, 1), F32)
        res = lax.fori_loop(0, nrun, span, (zero_o, zero_c, zero_c, zero_o, zero_c, zero_c))
        dq_ref[...] = (jnp.where(head_a, res[0], res[3]) * scale).astype(BF16)

    blk = pl.BlockSpec((Q, B), lambda p, i: (i, p))
    col = pl.BlockSpec((LP, B), lambda p, i: (0, p))
    tri = pl.BlockSpec((SUB, SUB), lambda p, i: (0, 0))
    out = jax.ShapeDtypeStruct((LP, 512), F32)
    return pl.pallas_call(
        body, name="sb_bwd", grid=(SB_HEADS // 2, nq),
        in_specs=[pl.BlockSpec(memory_space=pltpu.SMEM),
                  pl.BlockSpec((Q, B), lambda p, i: (i, qcol + p)),
                  pl.BlockSpec((LP, B), lambda p, i: (0, kcol + p)),
                  pl.BlockSpec((LP, B), lambda p, i: (0, vcol + p)),
                  blk, blk, tri, tri],
        out_specs=[blk, col, col], out_shape=[jax.ShapeDtypeStruct((LP, 512), BF16), out, out],
        compiler_params=_cp(("arbitrary", "arbitrary")),
    )(nrun_all, proj, proj, proj, ctot, do, jnp.tril(jnp.ones((SUB, SUB), BF16), -1),
      jnp.triu(jnp.ones((SUB, SUB), BF16), 1))


def _sb_group_mean():
    return jnp.kron(jnp.eye(SB_HEADS, dtype=F32), jnp.full((SB_DH, SB_DH), 1.0 / SB_DH, F32)).astype(BF16)


GM_SPEC = pl.BlockSpec((SB_HEADS * SB_DH, SB_HEADS * SB_DH), lambda i: (0, 0))


def _attn_norm_fwd(og, proj, osb, gnw, snw):
    LP = og.shape[0]
    T = _tile(LP, 256)

    def body(og_ref, z_ref, os_ref, gnw_ref, snw_ref, gm_ref, y_ref):
        valid = (pl.program_id(0) * T + _iota2((T, 1), 0)) >= PAD_ROWS
        z = z_ref[...]
        zg = z * _sigmoid(z)
        for h in range(GDN_HEADS):
            sl = slice(h * GDN_D, (h + 1) * GDN_D)
            o = og_ref[:, sl]
            y = o * _rms(o) * gnw_ref[...] * zg[:, sl]
            y_ref[:, sl] = jnp.where(valid, y, 0.0).astype(BF16)
        o = os_ref[...]
        msq = _dot_exact_r(o * o, gm_ref[...])
        y = o * lax.rsqrt(msq + NORM_EPS) * snw_ref[...]
        y_ref[:, 512:] = jnp.where(valid, y, 0.0).astype(BF16)

    row = pl.BlockSpec((T, 512), lambda i: (i, 0))
    return pl.pallas_call(
        body, name="attn_norm_fwd", grid=(LP // T,),
        in_specs=[row, pl.BlockSpec((T, 512), lambda i: (i, OFF_Z // 512)), row,
                  pl.BlockSpec((1, GDN_D), lambda i: (0, 0)), pl.BlockSpec((1, 512), lambda i: (0, 0)), GM_SPEC],
        out_specs=pl.BlockSpec((T, 1024), lambda i: (i, 0)),
        out_shape=jax.ShapeDtypeStruct((LP, 1024), BF16),
        compiler_params=_cp(("parallel",)),
    )(og, proj, osb, gnw, snw, _sb_group_mean())


def _attn_norm_bwd(og, proj, osb, gnw, snw, dy, carry=None):
    LP = og.shape[0]
    T = _tile(LP, 256)

    def body(og_ref, z_ref, os_ref, gnw_ref, snw_ref, dy_ref, gm_ref, dog_ref, dz_ref, dos_ref, dgw_ref, dsw_ref):
        @pl.when(pl.program_id(0) == 0)
        def _():
            dgw_ref[...] = jnp.zeros_like(dgw_ref)
            dsw_ref[...] = jnp.zeros_like(dsw_ref)
        valid = (pl.program_id(0) * T + _iota2((T, 1), 0)) >= PAD_ROWS
        dy = jnp.where(valid, dy_ref[...], 0.0)
        z = z_ref[...]
        sg = _sigmoid(z)
        zg = z * sg
        dgw = jnp.zeros((1, GDN_D), F32)
        for h in range(GDN_HEADS):
            sl = slice(h * GDN_D, (h + 1) * GDN_D)
            o = og_ref[:, sl]
            dyh = dy[:, sl]
            dx, dwn = _rms_bwd(o, gnw_ref[...], dyh * zg[:, sl])
            dog_ref[:, sl] = dx
            dgw = dgw + _colsum(dwn)
            yn = o * _rms(o) * gnw_ref[...]
            dz_ref[:, sl] = (dyh * yn * (sg[:, sl] * (1.0 + z[:, sl] * (1.0 - sg[:, sl])))).astype(BF16)
        dgw_ref[...] += dgw
        o = os_ref[...]
        gm = gm_ref[...]
        r = lax.rsqrt(_dot_exact_r(o * o, gm) + NORM_EPS)
        n = o * r
        dys = dy[:, 512:]
        dyw = dys * snw_ref[...]
        dos_ref[...] = r * (dyw - n * _dot_exact_r(dyw * n, gm))
        dsw_ref[...] += _colsum(dys * n)

    row = pl.BlockSpec((T, 512), lambda i: (i, 0))
    gw = pl.BlockSpec((1, GDN_D), lambda i: (0, 0))
    sw = pl.BlockSpec((1, 512), lambda i: (0, 0))
    o512 = jax.ShapeDtypeStruct((LP, 512), F32)
    return _call_carrying(
        carry, body, LP // T, name="attn_norm_bwd",
        in_specs=[row, pl.BlockSpec((T, 512), lambda i: (i, OFF_Z // 512)), row, gw, sw,
                  pl.BlockSpec((T, 1024), lambda i: (i, 0)), GM_SPEC],
        out_specs=[row, row, row, gw, sw],
        out_shape=[o512, jax.ShapeDtypeStruct((LP, 512), BF16), o512, jax.ShapeDtypeStruct((1, GDN_D), F32),
                   jax.ShapeDtypeStruct((1, 512), F32)],
        operands=(og, proj, osb, gnw, snw, dy, _sb_group_mean()))


def _resid_fwd(h0, mix, w_post, w_pre):
    LP, D = h0.shape
    T = _tile(LP, 512)

    def body(h0_ref, mix_ref, wp_ref, wf_ref, h1_ref, n2_ref):
        mix = mix_ref[...]
        h1 = h0_ref[...] + mix * _rms(mix) * wp_ref[...]
        h1_ref[...] = h1
        n2_ref[...] = (h1 * _rms(h1) * wf_ref[...]).astype(BF16)

    row = pl.BlockSpec((T, D), lambda i: (i, 0))
    vec = pl.BlockSpec((1, D), lambda i: (0, 0))
    return pl.pallas_call(
        body, name="resid_fwd", grid=(LP // T,),
        in_specs=[row, row, vec, vec], out_specs=[row, row],
        out_shape=[jax.ShapeDtypeStruct((LP, D), F32), jax.ShapeDtypeStruct((LP, D), BF16)],
        compiler_params=_cp(("parallel",)),
    )(h0, mix, w_post, w_pre)


def _resid_bwd(h1, mix, w_post, w_pre, dout, dn2):
    LP, D = h1.shape
    T = _tile(LP, 512)

    def body(h1_ref, mix_ref, wp_ref, wf_ref, dout_ref, dn2_ref, dh1_ref, dmix_ref, dwf_ref, dwp_ref):
        @pl.when(pl.program_id(0) == 0)
        def _():
            dwf_ref[...] = jnp.zeros_like(dwf_ref)
            dwp_ref[...] = jnp.zeros_like(dwp_ref)
        dx, dwn = _rms_bwd(h1_ref[...], wf_ref[...], dn2_ref[...])
        dh1 = dout_ref[...] + dx
        dh1_ref[...] = dh1
        dwf_ref[...] += _colsum(dwn)
        dmix, dwn2 = _rms_bwd(mix_ref[...], wp_ref[...], dh1)
        dmix_ref[...] = dmix.astype(BF16)
        dwp_ref[...] += _colsum(dwn2)

    row = pl.BlockSpec((T, D), lambda i: (i, 0))
    vec = pl.BlockSpec((1, D), lambda i: (0, 0))
    v = jax.ShapeDtypeStruct((1, D), F32)
    return pl.pallas_call(
        body, name="resid_bwd", grid=(LP // T,),
        in_specs=[row, row, vec, vec, row, row], out_specs=[row, row, vec, vec],
        out_shape=[jax.ShapeDtypeStruct((LP, D), F32), jax.ShapeDtypeStruct((LP, D), BF16), v, v],
        compiler_params=_cp(("arbitrary",)),
    )(h1, mix, w_post, w_pre, dout, dn2)


GELU_C = 0.7978845608028654
GELU_A = 0.044715


def _gelu_parts(x):
    t = jnp.tanh(GELU_C * (x + GELU_A * x * x * x))
    return 0.5 * x * (1.0 + t), t


def _convglu_fwd(up, conv_w, conv_b):
    LP, C = up.shape
    T = _tile(LP, 128)

    def body(x_ref, halo_ref, cw_ref, cb_ref, act_ref, y_ref):
        i = pl.program_id(0)

        def conv(cols):
            ext = jnp.concatenate([jnp.where(i > 0, halo_ref[:, cols], 0.0), x_ref[:, cols]], axis=0)
            w = cw_ref[:, cols]
            y = (w[2:3] * ext[8:] + w[1:2] * pltpu.roll(ext, 1, 0)[8:] + w[0:1] * pltpu.roll(ext, 2, 0)[8:]
                 + cb_ref[:, cols])
            y_ref[:, cols] = y.astype(BF16)
            return y

        for s in range(D_FF // LANE):
            gs = slice(s * LANE, (s + 1) * LANE)
            g, _ = _gelu_parts(conv(gs))
            act_ref[:, gs] = (g * conv(slice(D_FF + s * LANE, D_FF + (s + 1) * LANE))).astype(BF16)

    t8 = T // 8
    return pl.pallas_call(
        body, name="convglu_fwd", grid=(LP // T,),
        in_specs=[pl.BlockSpec((T, C), lambda i: (i, 0)),
                  pl.BlockSpec((8, C), lambda i: (jnp.maximum(i * t8 - 1, 0), 0)),
                  pl.BlockSpec((FFN_CONV, C), lambda i: (0, 0)), pl.BlockSpec((1, C), lambda i: (0, 0))],
        out_specs=[pl.BlockSpec((T, D_FF), lambda i: (i, 0)), pl.BlockSpec((T, C), lambda i: (i, 0))],
        out_shape=[jax.ShapeDtypeStruct((LP, D_FF), BF16), jax.ShapeDtypeStruct((LP, C), BF16)],
        compiler_params=_cp(("parallel",)),
    )(up, up, conv_w, conv_b)


def _convglu_bwd(up, y, conv_w, dact):
    LP, C = up.shape
    T = _tile(LP, 128)
    TE = T + 8
    nt = LP // T

    def body(x_ref, y_ref, yn_ref, cw_ref, da_ref, dan_ref, dx_ref, dcw_ref, dcb_ref):
        i = pl.program_id(0)

        @pl.when(i == 0)
        def _():
            dcw_ref[...] = jnp.zeros_like(dcw_ref)
            dcb_ref[...] = jnp.zeros_like(dcb_ref)

        last = i == nt - 1

        def back(cols, dy):
            w = cw_ref[:, cols]
            later = [dy[0:T], pltpu.roll(dy, TE - 1, 0)[0:T], pltpu.roll(dy, TE - 2, 0)[0:T]]
            x_t = x_ref[:, cols]
            dcb_ref[:, cols] += _colsum(later[0])
            for j in range(FFN_CONV):
                dcw_ref[j:j + 1, cols] += _colsum(later[FFN_CONV - 1 - j] * x_t)
            dx_ref[:, cols] = (w[2:3] * later[0] + w[1:2] * later[1] + w[0:1] * later[2]).astype(BF16)

        for s in range(D_FF // LANE):
            gs = slice(s * LANE, (s + 1) * LANE)
            vs = slice(D_FF + s * LANE, D_FF + (s + 1) * LANE)
            gate = jnp.concatenate([y_ref[:, gs].astype(F32), yn_ref[0:8, gs].astype(F32)], axis=0)
            val = jnp.concatenate([y_ref[:, vs].astype(F32), yn_ref[0:8, vs].astype(F32)], axis=0)
            g, t = _gelu_parts(gate)
            dg_dx = 0.5 * (1.0 + t) + 0.5 * gate * (1.0 - t * t) * GELU_C * (1.0 + 3.0 * GELU_A * gate * gate)
            da = jnp.concatenate([da_ref[:, gs], jnp.where(last, 0.0, dan_ref[:, gs])], axis=0)
            back(gs, da * val * dg_dx)
            back(vs, da * g)

    t8 = T // 8
    nb8 = LP // 8
    next8 = lambda w: pl.BlockSpec((8, w), lambda i: (jnp.minimum((i + 1) * t8, nb8 - 1), 0))
    row = lambda w: pl.BlockSpec((T, w), lambda i: (i, 0))
    small = lambda r: pl.BlockSpec((r, C), lambda i: (0, 0))
    return pl.pallas_call(
        body, name="convglu_bwd", grid=(nt,),
        in_specs=[row(C), row(C), pl.BlockSpec((16, C), lambda i: (jnp.minimum((i + 1) * (T // 16), LP // 16 - 1), 0)),
                  small(FFN_CONV), row(D_FF), next8(D_FF)],
        out_specs=[row(C), small(FFN_CONV), small(1)],
        out_shape=[jax.ShapeDtypeStruct((LP, C), BF16), jax.ShapeDtypeStruct((FFN_CONV, C), F32),
                   jax.ShapeDtypeStruct((1, C), F32)],
        compiler_params=_cp(("arbitrary",)),
    )(up, y, y, conv_w, dact, dact)


def _final(h1, f, w_post, target, n_real):
    LP, D = h1.shape
    T = _tile(LP, 256)

    def body(h1_ref, f_ref, w_ref, t_ref, loss_ref, dout_ref, df_ref, dw_ref):
        @pl.when(pl.program_id(0) == 0)
        def _():
            loss_ref[...] = jnp.zeros_like(loss_ref)
            dw_ref[...] = jnp.zeros_like(dw_ref)
        rows = pl.program_id(0) * T + _iota2((T, 1), 0)
        real = (rows >= ROW0) & (rows < ROW0 + n_real)
        f = f_ref[...]
        out = h1_ref[...] + f * _rms(f) * w_ref[...]
        err = jnp.where(real, out - t_ref[...], 0.0)
        loss_ref[...] += 0.5 * jnp.sum(_colsum(jnp.mean(err * err, axis=-1, keepdims=True)), axis=-1, keepdims=True)
        dout = err * (1.0 / D)
        dout_ref[...] = dout
        dx, dwn = _rms_bwd(f, w_ref[...], dout)
        df_ref[...] = dx.astype(BF16)
        dw_ref[...] += _colsum(dwn)

    row = pl.BlockSpec((T, D), lambda i: (i, 0))
    vec = pl.BlockSpec((1, D), lambda i: (0, 0))
    return pl.pallas_call(
        body, name="final_loss", grid=(LP // T,),
        in_specs=[row, row, vec, row],
        out_specs=[pl.BlockSpec((1, 128), lambda i: (0, 0)), row, row, vec],
        out_shape=[jax.ShapeDtypeStruct((1, 128), F32), jax.ShapeDtypeStruct((LP, D), F32),
                   jax.ShapeDtypeStruct((LP, D), BF16), jax.ShapeDtypeStruct((1, D), F32)],
        compiler_params=_cp(("arbitrary",)),
    )(h1, f, w_post, target)


ANY_SPEC = pl.BlockSpec(memory_space=pl.ANY)
N_CHIP = 4


def _other_chips(x, y):
    return [(1 - x, y), (x, 1 - y), (1 - x, 1 - y)]


def _gather_direct(arrs, name):
    n = len(arrs)
    npeer = N_DEV - 1

    def body(*refs):
        ins, outs = refs[:n], refs[n:2 * n]
        send_sems, recv_sems, loc_sems = refs[2 * n:]
        x, y, c = lax.axis_index("x"), lax.axis_index("y"), lax.axis_index("c")
        me = 4 * x + 2 * y + c
        copies = []
        for a in range(n):
            for kk in range(1, N_DEV):
                px = 1 - x if kk & 4 else x
                py = 1 - y if kk & 2 else y
                pc = 1 - c if kk & 1 else c
                s = a * npeer + kk - 1
                cp = pltpu.make_async_remote_copy(src_ref=ins[a], dst_ref=outs[a].at[me], send_sem=send_sems.at[s],
                                                  recv_sem=recv_sems.at[s], device_id=(px, py, pc), device_id_type=MESH)
                cp.start()
                copies.append(cp)
            own = pltpu.make_async_copy(ins[a], outs[a].at[me], loc_sems.at[a])
            own.start()
            copies.append(own)
        for cp in copies:
            cp.wait()

    shapes = [jax.ShapeDtypeStruct((N_DEV,) + tuple(a.shape), a.dtype) for a in arrs]
    return pl.pallas_call(
        body, name=name, in_specs=[ANY_SPEC] * n, out_specs=[ANY_SPEC] * n, out_shape=shapes,
        scratch_shapes=[pltpu.SemaphoreType.DMA((n * npeer,)), pltpu.SemaphoreType.DMA((n * npeer,)),
                        pltpu.SemaphoreType.DMA((n,))],
        compiler_params=pltpu.CompilerParams(has_side_effects=True),
    )(*arrs)


class _Exchange:
    def __init__(self, arrs, out_shapes, scratch, start, finish, mid=None):
        self.arrs, self.out_shapes, self.scratch = list(arrs), list(out_shapes), list(scratch)
        self.start, self.finish, self.mid = start, finish, mid

    @property
    def n(self):
        return len(self.arrs)


def _run_exchange(ex, name):
    n = ex.n

    def body(*refs):
        ins, outs, sems = refs[:n], refs[n:2 * n], refs[2 * n:]
        ex.start(ins, outs, sems)
        if ex.mid is not None:
            ex.mid(ins, outs, sems)
        ex.finish(ins, outs, sems)

    return pl.pallas_call(
        body, name=name, in_specs=[ANY_SPEC] * n, out_specs=[ANY_SPEC] * n, out_shape=ex.out_shapes,
        scratch_shapes=ex.scratch, compiler_params=pltpu.CompilerParams(has_side_effects=True),
    )(*ex.arrs)


def _carry_begin(ex, refs, step, nsteps):
    if ex is None:
        return

    @pl.when(step == 0)
    def _():
        ex.start(*refs)

    if ex.mid is not None:
        @pl.when(step == min(nsteps - 1, (3 * nsteps) // 5))
        def _():
            ex.mid(*refs)


def _carry_end(ex, refs, step, nsteps):
    if ex is None:
        return

    @pl.when(step == nsteps - 1)
    def _():
        ex.finish(*refs)


def _gather_two_level(arrs):
    n = len(arrs)
    K = 7

    def env(ins, outs, sems):
        send_sems, recv_sems, loc_sems = sems
        x, y, c = lax.axis_index("x"), lax.axis_index("y"), lax.axis_index("c")

        def cp(a, k, src, slot, to):
            return pltpu.make_async_remote_copy(src_ref=src, dst_ref=outs[a].at[slot], send_sem=send_sems.at[a * K + k],
                                                recv_sem=recv_sems.at[a * K + k], device_id=to, device_id_type=MESH)

        me = 4 * x + 2 * y + c
        owns = [pltpu.make_async_copy(ins[a], outs[a].at[me], loc_sems.at[a]) for a in range(n)]
        first = []
        for a in range(n):
            first.append(cp(a, 0, ins[a], me, (x, y, 1 - c)))
            first += [cp(a, 1 + j, ins[a], me, (px, py, c)) for j, (px, py) in enumerate(_other_chips(x, y))]
        passed = []
        for j, (px, py) in enumerate(_other_chips(x, y)):
            slot = 4 * px + 2 * py + c
            passed += [(cp(a, 1 + j, ins[a], slot, (px, py, c)), cp(a, 4 + j, outs[a].at[slot], slot, (x, y, 1 - c)))
                       for a in range(n)]
        from_sib = []
        for a in range(n):
            from_sib.append(cp(a, 0, ins[a], 4 * x + 2 * y + (1 - c), (x, y, 1 - c)))
            from_sib += [cp(a, 4 + j, ins[a], 4 * px + 2 * py + (1 - c), (x, y, 1 - c))
                         for j, (px, py) in enumerate(_other_chips(x, y))]
        return owns, first, passed, from_sib

    def start(ins, outs, sems):
        owns, first, _, _ = env(ins, outs, sems)
        for cp in owns + first:
            cp.start()

    def mid(ins, outs, sems):
        _, _, passed, _ = env(ins, outs, sems)
        for arrival, fwd in passed:
            arrival.wait_recv()
            fwd.start()

    def finish(ins, outs, sems):
        owns, first, passed, from_sib = env(ins, outs, sems)
        for cp in from_sib:
            cp.wait_recv()
        for cp in first + [fwd for _, fwd in passed]:
            cp.wait_send()
        for cp in owns:
            cp.wait()

    shapes = [jax.ShapeDtypeStruct((N_DEV,) + tuple(a.shape), a.dtype) for a in arrs]
    scratch = [pltpu.SemaphoreType.DMA((n * K,)), pltpu.SemaphoreType.DMA((n * K,)), pltpu.SemaphoreType.DMA((n,))]
    return _Exchange(arrs, shapes, scratch, start, finish, mid)


def _swap_sibling(arrs):
    n = len(arrs)

    def copies(ins, outs, sems):
        send_sems, recv_sems = sems
        x, y, c = lax.axis_index("x"), lax.axis_index("y"), lax.axis_index("c")
        return [pltpu.make_async_remote_copy(src_ref=ins[a], dst_ref=outs[a], send_sem=send_sems.at[a],
                                             recv_sem=recv_sems.at[a], device_id=(x, y, 1 - c), device_id_type=MESH)
                for a in range(n)]

    def start(ins, outs, sems):
        for cp in copies(ins, outs, sems):
            cp.start()

    def finish(ins, outs, sems):
        for cp in copies(ins, outs, sems):
            cp.wait()

    shapes = [jax.ShapeDtypeStruct(tuple(a.shape), a.dtype) for a in arrs]
    return _Exchange(arrs, shapes, [pltpu.SemaphoreType.DMA((n,)), pltpu.SemaphoreType.DMA((n,))], start, finish)


def _exchange_chips(arrs):
    n = len(arrs)
    K = N_CHIP - 1

    def copies(ins, outs, sems):
        send_sems, recv_sems, loc_sems = sems
        x, y, c = lax.axis_index("x"), lax.axis_index("y"), lax.axis_index("c")
        mine = 2 * x + y
        out = []
        for a in range(n):
            out += [pltpu.make_async_remote_copy(src_ref=ins[a].at[2 * px + py], dst_ref=outs[a].at[mine],
                                                 send_sem=send_sems.at[a * K + j], recv_sem=recv_sems.at[a * K + j],
                                                 device_id=(px, py, c), device_id_type=MESH)
                    for j, (px, py) in enumerate(_other_chips(x, y))]
            out.append(pltpu.make_async_copy(ins[a].at[mine], outs[a].at[mine], loc_sems.at[a]))
        return out

    def start(ins, outs, sems):
        for cp in copies(ins, outs, sems):
            cp.start()

    def finish(ins, outs, sems):
        for cp in copies(ins, outs, sems):
            cp.wait()

    shapes = [jax.ShapeDtypeStruct(tuple(a.shape), a.dtype) for a in arrs]
    scratch = [pltpu.SemaphoreType.DMA((n * K,)), pltpu.SemaphoreType.DMA((n * K,)), pltpu.SemaphoreType.DMA((n,))]
    return _Exchange(arrs, shapes, scratch, start, finish)


def _add_halves(mine, theirs, name):
    _, R, C = mine.shape
    cap = max(16, (ELEMWISE_VMEM // (4 * C * 10)) // 16 * 16)
    T = R if R <= cap else _tile(R, cap, 16)

    def body(a_ref, b_ref, o_ref):
        o_ref[...] = (a_ref[...] + b_ref[...].astype(F32)).astype(BF16)

    blk = pl.BlockSpec((N_CHIP, T, C), lambda i: (0, i, 0))
    return pl.pallas_call(
        body, name=name, grid=(R // T,), in_specs=[blk, blk], out_specs=blk,
        out_shape=jax.ShapeDtypeStruct(mine.shape, BF16), compiler_params=_cp(("parallel",)),
    )(mine, theirs)


def _adamw(parts, w, m, v, name):
    R, C = w.shape
    npart = parts.shape[0]
    cap = max(16, (ELEMWISE_VMEM // (4 * C * 12)) // 16 * 16)
    T = R if R <= cap else _tile(R, cap, 16)

    def body(p_ref, w_ref, m_ref, v_ref, g_ref, d_ref, nm_ref, nv_ref):
        g = p_ref[0].astype(F32)
        for k in range(1, npart):
            g = g + p_ref[k].astype(F32)
        mm = ADAM_B1 * m_ref[...] + (1.0 - ADAM_B1) * g
        vv = ADAM_B2 * v_ref[...] + (1.0 - ADAM_B2) * (g * g)
        m_hat = mm / (1.0 - ADAM_B1 ** ADAM_STEP)
        v_hat = vv / (1.0 - ADAM_B2 ** ADAM_STEP)
        g_ref[...] = g
        d_ref[...] = -ADAM_LR * (m_hat / (jnp.sqrt(v_hat) + ADAM_EPS) + ADAM_WD * w_ref[...])
        nm_ref[...] = mm
        nv_ref[...] = vv

    row = pl.BlockSpec((T, C), lambda i: (i, 0))
    out = jax.ShapeDtypeStruct((R, C), F32)
    return pl.pallas_call(
        body, name=name, grid=(R // T,),
        in_specs=[pl.BlockSpec((npart, T, C), lambda i: (0, i, 0)), row, row, row],
        out_specs=[row] * 4, out_shape=[out] * 4,
        compiler_params=_cp(("parallel",)),
    )(parts, w, m, v)


SMALL = ("attn_pre_norm", "gdn_A_log", "gdn_dt_bias", "gdn_norm_w", "sb_norm_w", "attn_post_norm",
         "ffn_pre_norm", "ffn_conv_b", "ffn_post_norm")


def _pack_small(arrs):
    rows = []
    for a in arrs:
        flat = a.reshape(-1).astype(F32)
        n = -(-flat.shape[0] // 128) * 128
        rows.append(jnp.pad(flat, (0, n - flat.shape[0])).reshape(-1, 128))
    slab = jnp.concatenate(rows, axis=0)
    pad = (-slab.shape[0]) % 8
    return jnp.pad(slab, ((0, pad), (0, 0)))


def _unpack_small(slab, shapes):
    out, r = [], 0
    for shp in shapes:
        size = 1
        for s in shp:
            size *= s
        nr = -(-size // 128)
        out.append(slab[r:r + nr].reshape(-1)[:size].reshape(shp))
        r += nr
    return out


def _to_blocks_cols(a):
    R, C = a.shape
    return a.reshape(R, N_DEV, C // N_DEV).transpose(1, 0, 2)


def _from_blocks_cols(a):
    n, R, c = a.shape
    return a.transpose(1, 0, 2).reshape(R, n * c)


def kernel(x, meta_tokens, attn_pre_norm, w_in, gdn_conv_w, gdn_A_log, gdn_dt_bias, gdn_norm_w, sb_norm_w, w_out, attn_post_norm, ffn_pre_norm, w_ffn_up, ffn_conv_w, ffn_conv_b, w_ffn_down, ffn_post_norm, loss_target, m_meta_tokens, m_attn_pre_norm, m_w_in, m_gdn_conv_w, m_gdn_A_log, m_gdn_dt_bias, m_gdn_norm_w, m_sb_norm_w, m_w_out, m_attn_post_norm, m_ffn_pre_norm, m_w_ffn_up, m_ffn_conv_w, m_ffn_conv_b, m_w_ffn_down, m_ffn_post_norm, v_meta_tokens, v_attn_pre_norm, v_w_in, v_gdn_conv_w, v_gdn_A_log, v_gdn_dt_bias, v_gdn_norm_w, v_sb_norm_w, v_w_out, v_attn_post_norm, v_ffn_pre_norm, v_w_ffn_up, v_ffn_conv_w, v_ffn_conv_b, v_w_ffn_down, v_ffn_post_norm):
    args = dict(locals())
    seq = x.shape[1]
    LP = -(-(ROW0 + seq) // LP_ALIGN) * LP_ALIGN
    tail = LP - ROW0 - seq

    meta_f = _from_blocks_cols(_run_exchange(_gather_two_level([meta_tokens]), "gather_meta")[0])

    (h0, target), got = _build_rows(x[0], meta_f, loss_target[0], LP,
                                    carry=_gather_two_level([w_in[0].astype(BF16), gdn_conv_w[0]]))
    (u,), _ = _prenorm_fwd(h0, attn_pre_norm)
    win_o = _from_blocks_cols(got[0])
    o_ab = C_QKV
    o_z = o_ab + 2 * GDN_HEADS
    w_inp = jnp.concatenate([win_o[:, :C_QKV], win_o[:, o_z:o_z + C_Z], win_o[:, o_z + C_Z:],
                             win_o[:, o_ab:o_z], jnp.zeros((D_MODEL, C_AB - 2 * GDN_HEADS), BF16)], axis=1)
    gconv_f = _from_blocks_cols(got[1])
    proj = _mm(u, w_inp, F32, "mm_in")
    (qn, kn, vg, beta_b, g_b), got = _gdn_pre_fwd(
        proj, gconv_f, gdn_A_log, gdn_dt_bias,
        carry=_gather_two_level([w_out[0].astype(BF16), w_ffn_down[0].astype(BF16)]))
    w_out_f = got[0].reshape(D_MODEL, D_MODEL)
    w_down_f = got[1].reshape(D_FF, D_MODEL)
    (cu, cw, cqd, ckd, cqk, ct, cgl), got = _gdn_chunk_fwd(
        qn, kn, vg, beta_b, g_b, carry=_gather_two_level([w_ffn_up[0].astype(BF16), ffn_conv_w[0]]))
    w_up_f = _from_blocks_cols(got[0])
    fconv_f = _from_blocks_cols(got[1])
    og, ssave = _gdn_scan_fwd(cu, cw, cqd, ckd, cqk, cgl)
    osb, ctot, sb_nrun = _sb_fwd(proj)
    snw = sb_norm_w.reshape(1, SB_HEADS * SB_DH)
    y = _attn_norm_fwd(og, proj, osb, gdn_norm_w, snw)
    mix = _mm(y, w_out_f, F32, "mm_out")
    h1, n2 = _resid_fwd(h0, mix, attn_post_norm, ffn_pre_norm)
    up = _mm(n2, w_up_f, F32, "mm_up")
    act, conv_y = _convglu_fwd(up, fconv_f, ffn_conv_b)
    f = _mm(act, w_down_f, F32, "mm_down")
    loss_part, dout, df, d_fpost = _final(h1, f, ffn_post_norm, target, seq)

    d_wdown = _mm_tn(act, df, "mm_dw_down")
    dact = _mm_nt(df, w_down_f, F32, "mm_dact")
    dup, d_fconv, d_fconvb = _convglu_bwd(up, conv_y, fconv_f, dact)
    d_wup = _mm_tn(n2, dup, "mm_dw_up")
    dn2 = _mm_nt(dup, w_up_f, F32, "mm_dn2")
    dh1, dmix, d_fpre, d_apost = _resid_bwd(h1, mix, attn_post_norm, ffn_pre_norm, dout, dn2)
    d_wout = _mm_tn(y, dmix, "mm_dw_out")
    dy = _mm_nt(dmix, w_out_f, F32, "mm_dy")
    my_c = lax.axis_index("c")

    def core_halves(blocks):
        halves = [s.reshape((N_CHIP, 2) + s.shape[1:]) for s in blocks]
        return ([lax.dynamic_index_in_dim(h, my_c, axis=1, keepdims=False) for h in halves],
                [lax.dynamic_index_in_dim(h, 1 - my_c, axis=1, keepdims=False).astype(BF16) for h in halves])

    early_names = ("w_out", "w_ffn_up", "w_ffn_down", "ffn_conv_w")
    e_mine, e_send = core_halves([d_wout.reshape(N_DEV, D_MODEL // N_DEV, D_MODEL), _to_blocks_cols(d_wup),
                                  d_wdown.reshape(N_DEV, D_FF // N_DEV, D_MODEL), _to_blocks_cols(d_fconv)])
    (dog, dz, dos, d_gnw, d_snw), e_theirs = _attn_norm_bwd(og, proj, osb, gdn_norm_w, snw, dy,
                                                            carry=_swap_sibling(e_send))
    e_sums = [_add_halves(a, b, "grads_add_" + nm) for nm, a, b in zip(early_names, e_mine, e_theirs)]
    dqs, dks, dvs = _sb_bwd(proj, ctot, sb_nrun, dos)
    (du_, dw_, dqd_, dkd_, dqk_, dgl_), _ = _gdn_scan_bwd(cu, cw, cqd, ckd, cqk, cgl, ssave, dog)
    dqn, dkn, dvg, dbeta, dg = _gdn_chunk_bwd(qn, kn, vg, beta_b, g_b, ct, du_, dw_, dqd_, dkd_, dqk_, dgl_)
    (dqkv, dab, d_gconv, d_gsc), e_recv = _gdn_pre_bwd(proj, gconv_f, gdn_A_log, gdn_dt_bias, dqn, dkn, dvg, dbeta, dg,
                                                       carry=_exchange_chips(e_sums))
    dpieces = [dqkv, dz, dqs, dks, dvs, dab]
    doffs = [0, OFF_Z, OFF_SB, OFF_SB + 512, OFF_SB + 1024, OFF_AB]
    dw_qkv, dw_ab = _mm_tn_pieces(u, [dqkv, dab], "mm_dw_in_gdn")
    dw_z, dw_qs, dw_ks, dw_vs = _mm_tn_pieces(u, [dz, dqs, dks, dvs], "mm_dw_in_rest")
    du0 = _mm_nt_pieces(dpieces, doffs, w_inp, F32, "mm_du")
    d_win = jnp.concatenate([dw_qkv, dw_ab[:, :2 * GDN_HEADS], dw_z, dw_qs, dw_ks, dw_vs], axis=1)
    late_names = ("w_in", "gdn_conv_w")
    l_mine, l_send = core_halves([_to_blocks_cols(d_win), _to_blocks_cols(d_gconv)])
    l_theirs = _run_exchange(_swap_sibling(l_send), "grads_swap_sibling")
    l_sums = [_add_halves(a, b, "grads_add_" + nm) for nm, a, b in zip(late_names, l_mine, l_theirs)]
    (dh0, d_apre), l_recv = _prenorm_bwd(h0, attn_pre_norm, du0, dh1, carry=_exchange_chips(l_sums))
    grad_x = dh0[ROW0:ROW0 + seq][None]
    d_meta = dh0[PAD_ROWS:ROW0]

    small_grads = [d_apre, d_gsc[0:1, :GDN_HEADS], d_gsc[1:2, :GDN_HEADS], d_gnw, d_snw.reshape(1, SB_HEADS, SB_DH),
                   d_apost, d_fpre, d_fconvb, d_fpost]
    loss_rows = jnp.pad(loss_part, ((0, 7), (0, 0)))
    n_param_rows = _pack_small(small_grads).shape[0]
    n_small_rows = n_param_rows + loss_rows.shape[0]
    slab_parts = _gather_direct(
        [jnp.concatenate([_pack_small(small_grads), loss_rows, d_meta.reshape(-1, LANE)], axis=0)],
        name="gather_small_grads")[0]
    me = 4 * lax.axis_index("x") + 2 * lax.axis_index("y") + my_c
    meta_parts = lax.dynamic_index_in_dim(
        slab_parts[:, n_small_rows:].reshape(N_DEV, N_META, N_DEV, LANE), me, axis=2, keepdims=False)
    slab_parts = slab_parts[:, :n_small_rows]

    res = {}
    for nm, parts in zip(early_names + late_names + ("meta_tokens",), list(e_recv) + list(l_recv) + [meta_parts]):
        wloc = args[nm]
        shp = wloc.shape
        w2 = wloc.reshape(shp[-2], shp[-1])
        outs = _adamw(parts, w2, args["m_" + nm].reshape(w2.shape), args["v_" + nm].reshape(w2.shape), "adamw_" + nm)
        res[nm] = [o.reshape(shp) for o in outs]
    small_shapes = [args[nm].shape for nm in SMALL]
    with_loss_rows = lambda slab: jnp.pad(slab, ((0, n_small_rows - n_param_rows), (0, 0)))
    outs = _adamw(slab_parts, with_loss_rows(_pack_small([args[nm] for nm in SMALL])),
                  with_loss_rows(_pack_small([args["m_" + nm] for nm in SMALL])),
                  with_loss_rows(_pack_small([args["v_" + nm] for nm in SMALL])), "adamw_small")
    loss = outs[0][n_param_rows, 0]
    for k in range(4):
        for nm, val in zip(SMALL, _unpack_small(outs[k], small_shapes)):
            res.setdefault(nm, [None] * 4)[k] = val

    order = ("meta_tokens", "attn_pre_norm", "w_in", "gdn_conv_w", "gdn_A_log", "gdn_dt_bias", "gdn_norm_w",
             "sb_norm_w", "w_out", "attn_post_norm", "ffn_pre_norm", "w_ffn_up", "ffn_conv_w", "ffn_conv_b",
             "w_ffn_down", "ffn_post_norm")
    return (loss, grad_x, *[res[nm][0] for nm in order], *[res[nm][1] for nm in order],
            *[res[nm][2] for nm in order], *[res[nm][3] for nm in order])
```

```python
import functools

import jax
import jax.numpy as jnp
from jax import lax
from jax.experimental import pallas as pl
from jax.experimental.pallas import tpu as pltpu

F32 = jnp.float32
BF16 = jnp.bfloat16

D_MODEL = 1024
N_META = 16
GDN_HEADS = 4
GDN_D = 128
GDN_CHUNK = 64
GDN_CONV = 4
GDN_ROWS = 256
SCAN_CHUNKS = 4
SB_HEADS = 8
SB_DH = 64
SB_BLOCK = 128
D_FF = 2816
FFN_CONV = 3
NORM_EPS = 1e-6
L2_EPS = 1e-6
LANE = 128
N_DEV = 8

PAD_ROWS = SB_BLOCK - N_META
ROW0 = SB_BLOCK
SB_SPAN = 512
SB_DEAD = -104.0
SB_SUB = 256
SB_QTILE = 256
LP_ALIGN = 256

C_QKV = 3 * GDN_HEADS * GDN_D
C_Z = GDN_HEADS * GDN_D
C_SB = 3 * SB_HEADS * SB_DH
C_AB = 256
OFF_Z = C_QKV
OFF_SB = OFF_Z + C_Z
OFF_AB = OFF_SB + C_SB
D_INP = OFF_AB + C_AB
D_IN = C_QKV + 2 * GDN_HEADS + C_Z + C_SB

ADAM_LR = 0.001
ADAM_B1 = 0.9
ADAM_B2 = 0.999
ADAM_EPS = 1e-08
ADAM_WD = 0.01
ADAM_STEP = 10

VMEM_LIMIT = 56 * 1024 * 1024
ELEMWISE_VMEM = 8 * 1024 * 1024
MESH = pl.DeviceIdType.MESH


def _cp(sem=None):
    kw = dict(vmem_limit_bytes=VMEM_LIMIT)
    if sem is not None:
        kw["dimension_semantics"] = sem
    return pltpu.CompilerParams(**kw)


def _tile(n, cap, unit=128):
    best = None
    t = unit
    while t <= min(n, cap):
        if n % t == 0:
            best = t
        t += unit
    assert best is not None, (n, cap, unit)
    return best


def _dot(a, b):
    return jnp.dot(a, b, preferred_element_type=F32)


def _dot_nt(a, b):
    return lax.dot_general(a, b, (((1,), (1,)), ((), ())), preferred_element_type=F32)


def _dot_tn(a, b):
    return lax.dot_general(a, b, (((0,), (0,)), ((), ())), preferred_element_type=F32)


def _split(x):
    hi = x.astype(BF16)
    lo = (x - hi.astype(F32)).astype(BF16)
    return hi, lo


def _dot1(a, b, f=_dot):
    return f(a.astype(BF16), b.astype(BF16))


def _dot3(a, b, f=_dot):
    ah, al = _split(a)
    bh, bl = _split(b)
    return f(ah, bh) + (f(ah, bl) + f(al, bh))


def _dot_exact_l(m_bf16, x, f=_dot):
    xh, xl = _split(x)
    return f(m_bf16, xh) + f(m_bf16, xl)


def _dot_exact_r(x, m_bf16, f=_dot):
    xh, xl = _split(x)
    return f(xh, m_bf16) + f(xl, m_bf16)


def _iota2(shape, dim):
    return lax.broadcasted_iota(jnp.int32, shape, dim)


def _sigmoid(x):
    return 0.5 * jnp.tanh(0.5 * x) + 0.5


def _softplus(x):
    return jnp.maximum(x, 0.0) + jnp.log(1.0 + jnp.exp(-jnp.abs(x)))


def _colsum(x):
    return jnp.sum(x, axis=0, keepdims=True)


def _rowsum(x):
    return jnp.sum(x, axis=-1, keepdims=True)


def _mm(a, b, out_dtype, name):
    M, K = a.shape
    K2, N = b.shape
    assert K == K2
    tm = _tile(M, 768)
    tn = _tile(N, max(128, (6 * 1024 * 1024) // (2 * K)))

    def body(a_ref, b_ref, o_ref):
        o_ref[...] = _dot(a_ref[...].astype(BF16), b_ref[...].astype(BF16)).astype(o_ref.dtype)

    return pl.pallas_call(
        body, name=name, grid=(N // tn, M // tm),
        in_specs=[pl.BlockSpec((tm, K), lambda j, i: (i, 0)), pl.BlockSpec((K, tn), lambda j, i: (0, j))],
        out_specs=pl.BlockSpec((tm, tn), lambda j, i: (i, j)),
        out_shape=jax.ShapeDtypeStruct((M, N), out_dtype),
        compiler_params=_cp(("parallel", "parallel")),
    )(a, b)


def _mm_nt(a, b, out_dtype, name):
    M, K = a.shape
    N, K2 = b.shape
    assert K == K2
    tm = _tile(M, 768)
    tn = _tile(N, max(128, (6 * 1024 * 1024) // (2 * K)))

    def body(a_ref, b_ref, o_ref):
        o_ref[...] = _dot_nt(a_ref[...].astype(BF16), b_ref[...].astype(BF16)).astype(o_ref.dtype)

    return pl.pallas_call(
        body, name=name, grid=(N // tn, M // tm),
        in_specs=[pl.BlockSpec((tm, K), lambda j, i: (i, 0)), pl.BlockSpec((tn, K), lambda j, i: (j, 0))],
        out_specs=pl.BlockSpec((tm, tn), lambda j, i: (i, j)),
        out_shape=jax.ShapeDtypeStruct((M, N), out_dtype),
        compiler_params=_cp(("parallel", "parallel")),
    )(a, b)


def _mm_nt_pieces(pieces, offsets, b, out_dtype, name):
    M = pieces[0].shape[0]
    N = b.shape[0]
    n = len(pieces)
    widths = [p.shape[1] for p in pieces]
    assert all(off % k == 0 for off, k in zip(offsets, widths))
    tm = _tile(M, 768)
    tn = _tile(N, 1024)

    def body(*refs):
        acc = _dot_nt(refs[0][...].astype(BF16), refs[n][...].astype(BF16))
        for p in range(1, n):
            acc = acc + _dot_nt(refs[p][...].astype(BF16), refs[n + p][...].astype(BF16))
        refs[2 * n][...] = acc.astype(out_dtype)

    return pl.pallas_call(
        body, name=name, grid=(N // tn, M // tm),
        in_specs=[pl.BlockSpec((tm, k), lambda j, i: (i, 0)) for k in widths]
        + [pl.BlockSpec((tn, k), functools.partial(lambda j, i, blk: (j, blk), blk=off // k))
           for off, k in zip(offsets, widths)],
        out_specs=pl.BlockSpec((tm, tn), lambda j, i: (i, j)),
        out_shape=jax.ShapeDtypeStruct((M, N), out_dtype),
        compiler_params=_cp(("parallel", "parallel")),
    )(*pieces, *([b] * n))


def _mm_tn_pieces(a, pieces, name):
    M, K = a.shape
    n = len(pieces)
    tm = _tile(M, 768)

    def body(*refs):
        @pl.when(pl.program_id(0) == 0)
        def _():
            for p in range(n):
                refs[1 + n + p][...] = jnp.zeros_like(refs[1 + n + p])
        at = refs[0][...].astype(BF16)
        for p in range(n):
            refs[1 + n + p][...] += _dot_tn(at, refs[1 + p][...].astype(BF16))

    return pl.pallas_call(
        body, name=name, grid=(M // tm,),
        in_specs=[pl.BlockSpec((tm, K), lambda m: (m, 0))] + [pl.BlockSpec((tm, p.shape[1]), lambda m: (m, 0)) for p in pieces],
        out_specs=[pl.BlockSpec((K, p.shape[1]), lambda m: (0, 0)) for p in pieces],
        out_shape=[jax.ShapeDtypeStruct((K, p.shape[1]), F32) for p in pieces],
        compiler_params=_cp(("arbitrary",)),
    )(a, *pieces)


def _mm_tn(a, b, name):
    M, K = a.shape
    M2, N = b.shape
    assert M == M2
    tm = _tile(M, 768)
    tk = _tile(K, 2816)
    tn = _tile(N, 2816)

    def body(a_ref, b_ref, o_ref):
        @pl.when(pl.program_id(2) == 0)
        def _():
            o_ref[...] = jnp.zeros_like(o_ref)
        o_ref[...] += _dot_tn(a_ref[...].astype(BF16), b_ref[...].astype(BF16))

    return pl.pallas_call(
        body, name=name, grid=(K // tk, N // tn, M // tm),
        in_specs=[pl.BlockSpec((tm, tk), lambda i, j, m: (m, i)), pl.BlockSpec((tm, tn), lambda i, j, m: (m, j))],
        out_specs=pl.BlockSpec((tk, tn), lambda i, j, m: (i, j)),
        out_shape=jax.ShapeDtypeStruct((K, N), F32),
        compiler_params=_cp(("parallel", "parallel", "arbitrary")),
    )(a, b)


def _rms(x):
    return lax.rsqrt(jnp.mean(x * x, axis=-1, keepdims=True) + NORM_EPS)


def _rms_bwd(x, w, dy):
    r = _rms(x)
    n = x * r
    dyw = dy * w
    dx = r * (dyw - n * jnp.mean(dyw * n, axis=-1, keepdims=True))
    return dx, dy * n


def _build_rows(x, meta, target, LP, carry=None):
    seq, D = x.shape
    T = SB_BLOCK
    tail = LP - ROW0 - seq
    assert seq % T == 0 and tail in (0, T) and meta.shape[0] == N_META
    n = carry.n if carry is not None else 0

    def body(*refs):
        x_ref, m_ref, t_ref = refs[:3]
        h_ref, tp_ref = refs[3 + n:5 + n]
        head_sc, zero_sc, sems = refs[5 + 2 * n:8 + 2 * n]
        ex_refs = (refs[3:3 + n], refs[5 + n:5 + 2 * n], refs[8 + 2 * n:])
        if carry is not None:
            carry.start(*ex_refs)
        zero_sc[...] = jnp.zeros_like(zero_sc)
        head_sc[0:PAD_ROWS, :] = jnp.zeros((PAD_ROWS, D), F32)
        head_sc[PAD_ROWS:, :] = m_ref[...]
        copies = [pltpu.make_async_copy(x_ref, h_ref.at[pl.ds(ROW0, seq)], sems.at[0]),
                  pltpu.make_async_copy(t_ref, tp_ref.at[pl.ds(ROW0, seq)], sems.at[1]),
                  pltpu.make_async_copy(head_sc, h_ref.at[pl.ds(0, T)], sems.at[2]),
                  pltpu.make_async_copy(zero_sc, tp_ref.at[pl.ds(0, T)], sems.at[3])]
        if tail:
            copies += [pltpu.make_async_copy(zero_sc, h_ref.at[pl.ds(ROW0 + seq, T)], sems.at[4]),
                       pltpu.make_async_copy(zero_sc, tp_ref.at[pl.ds(ROW0 + seq, T)], sems.at[5])]
        for cp in copies:
            cp.start()
        if carry is not None and carry.mid is not None:
            carry.mid(*ex_refs)
        for cp in copies:
            cp.wait()
        if carry is not None:
            carry.finish(*ex_refs)

    out = jax.ShapeDtypeStruct((LP, D), F32)
    res = pl.pallas_call(
        body, name="build_rows",
        in_specs=[ANY_SPEC, pl.BlockSpec(memory_space=pltpu.VMEM), ANY_SPEC] + [ANY_SPEC] * n,
        out_specs=[ANY_SPEC, ANY_SPEC] + [ANY_SPEC] * n,
        out_shape=[out, out] + (carry.out_shapes if carry is not None else []),
        scratch_shapes=[pltpu.VMEM((T, D), F32), pltpu.VMEM((T, D), F32), pltpu.SemaphoreType.DMA((6,))]
        + (carry.scratch if carry is not None else []),
        compiler_params=pltpu.CompilerParams(has_side_effects=True),
    )(x, meta, target, *(carry.arrs if carry is not None else []))
    return list(res[:2]), list(res[2:])


def _prenorm_fwd(h0, w, carry=None):
    LP, D = h0.shape
    T = _tile(LP, 512)

    def body(h_ref, w_ref, u_ref):
        h = h_ref[...]
        u_ref[...] = (h * _rms(h) * w_ref[...]).astype(BF16)

    return _call_carrying(
        carry, body, LP // T, name="prenorm_fwd",
        in_specs=[pl.BlockSpec((T, D), lambda i: (i, 0)), pl.BlockSpec((1, D), lambda i: (0, 0))],
        out_specs=[pl.BlockSpec((T, D), lambda i: (i, 0))],
        out_shape=[jax.ShapeDtypeStruct((LP, D), BF16)],
        operands=(h0, w))


def _prenorm_bwd(h0, w, du, dh1, carry=None):
    LP, D = h0.shape
    T = _tile(LP, 512)

    def body(h_ref, w_ref, du_ref, dh1_ref, dh0_ref, dw_ref):
        @pl.when(pl.program_id(0) == 0)
        def _():
            dw_ref[...] = jnp.zeros_like(dw_ref)
        dx, dwn = _rms_bwd(h_ref[...], w_ref[...], du_ref[...])
        dh0_ref[...] = dh1_ref[...] + dx
        dw_ref[...] += _colsum(dwn)

    row = pl.BlockSpec((T, D), lambda i: (i, 0))
    vec = pl.BlockSpec((1, D), lambda i: (0, 0))
    return _call_carrying(
        carry, body, LP // T, name="prenorm_bwd",
        in_specs=[row, vec, row, row], out_specs=[row, vec],
        out_shape=[jax.ShapeDtypeStruct((LP, D), F32), jax.ShapeDtypeStruct((1, D), F32)],
        operands=(h0, w, du, dh1))


def _gdn_gate_consts(alog_ref, dtb_ref, h):
    a_coef = -jnp.exp(alog_ref[0:1, h:h + 1])
    return a_coef, dtb_ref[0:1, h:h + 1]


def _gdn_pre_fwd(proj, conv_w, a_log, dt_bias, carry=None):
    LP = proj.shape[0]
    T = _tile(LP, 256)
    C = C_QKV
    H = GDN_HEADS

    def body(x_ref, halo_ref, ab_ref, cw_ref, alog_ref, dtb_ref, q_ref, k_ref, v_ref, beta_ref, g_ref):
        i = pl.program_id(0)

        def conv_silu(cols):
            ext = jnp.concatenate([jnp.where(i > 0, halo_ref[:, cols], 0.0), x_ref[:, cols]], axis=0)
            w = cw_ref[:, cols]
            y = w[GDN_CONV - 1:GDN_CONV] * ext[8:]
            for j in range(GDN_CONV - 1):
                y = y + w[j:j + 1] * pltpu.roll(ext, GDN_CONV - 1 - j, 0)[8:]
            return y * _sigmoid(y)

        for h in range(H):
            sl = slice(h * GDN_D, (h + 1) * GDN_D)
            cq = conv_silu(sl)
            q_ref[:, sl] = cq * lax.rsqrt(_rowsum(cq * cq) + L2_EPS) * (GDN_D ** -0.5)
            ck = conv_silu(slice(512 + h * GDN_D, 512 + (h + 1) * GDN_D))
            k_ref[:, sl] = ck * lax.rsqrt(_rowsum(ck * ck) + L2_EPS)
            v_ref[:, sl] = conv_silu(slice(1024 + h * GDN_D, 1024 + (h + 1) * GDN_D))
        ab = ab_ref[...]
        valid = (i * T + _iota2((T, 1), 0)) >= PAD_ROWS
        for h in range(H):
            sl = slice(h * GDN_D, (h + 1) * GDN_D)
            a_coef, dtb = _gdn_gate_consts(alog_ref, dtb_ref, h)
            g = jnp.where(valid, a_coef * _softplus(ab[:, h:h + 1] + dtb), 0.0)
            beta = jnp.where(valid, _sigmoid(ab[:, H + h:H + h + 1]), 0.0)
            g_ref[:, sl] = jnp.broadcast_to(g, (T, GDN_D))
            beta_ref[:, sl] = jnp.broadcast_to(beta, (T, GDN_D))

    t8 = T // 8
    row512 = pl.BlockSpec((T, 512), lambda i: (i, 0))
    small = lambda r, c: pl.BlockSpec((r, c), lambda i: (0, 0))
    out = jax.ShapeDtypeStruct((LP, 512), F32)
    return _call_carrying(
        carry, body, LP // T, name="gdn_pre_fwd",
        in_specs=[pl.BlockSpec((T, C), lambda i: (i, 0)),
                  pl.BlockSpec((8, C), lambda i: (jnp.maximum(i * t8 - 1, 0), 0)),
                  pl.BlockSpec((T, C_AB), lambda i: (i, OFF_AB // C_AB)),
                  small(GDN_CONV, C), small(1, H), small(1, H)],
        out_specs=[row512] * 5, out_shape=[out] * 5,
        operands=(proj, proj, proj, conv_w, a_log, dt_bias))


def _gdn_pre_bwd(proj, conv_w, a_log, dt_bias, dq, dk, dv, dbeta, dg, carry=None):
    LP = proj.shape[0]
    T = _tile(LP, 256)
    C = C_QKV
    H = GDN_HEADS
    TE = T + 8
    nt = LP // T

    def body(x_ref, xp_ref, xn_ref, ab_ref, cw_ref, alog_ref, dtb_ref,
             dq_ref, dqn_ref, dk_ref, dkn_ref, dv_ref, dvn_ref, dbeta_ref, dg_ref,
             dx_ref, dab_ref, dcw_ref, dsc_ref):
        i = pl.program_id(0)

        @pl.when(i == 0)
        def _():
            dcw_ref[...] = jnp.zeros_like(dcw_ref)
            dsc_ref[...] = jnp.zeros_like(dsc_ref)

        last = i == nt - 1

        def strip(cols, d_ref, dn_ref, dcols, scale):
            ext = jnp.concatenate([jnp.where(i > 0, xp_ref[:, cols], 0.0), x_ref[:, cols],
                                   jnp.where(last, 0.0, xn_ref[:, cols])], axis=0)
            sh = [ext[8:8 + TE]] + [pltpu.roll(ext, s, 0)[8:8 + TE] for s in range(1, GDN_CONV)]
            w = cw_ref[:, cols]
            y = w[GDN_CONV - 1:GDN_CONV] * sh[0]
            for j in range(GDN_CONV - 1):
                y = y + w[j:j + 1] * sh[GDN_CONV - 1 - j]
            sg = _sigmoid(y)
            d = jnp.concatenate([d_ref[:, dcols], jnp.where(last, 0.0, dn_ref[:, dcols])], axis=0)
            if scale is not None:
                c = y * sg
                r = lax.rsqrt(_rowsum(c * c) + L2_EPS)
                n = c * r
                d = scale * r * (d - n * _rowsum(d * n))
            dy = d * (sg * (1.0 + y * (1.0 - sg)))
            dy_t = dy[0:T]
            for j in range(GDN_CONV):
                dcw_ref[j:j + 1, cols] += _colsum(dy_t * sh[GDN_CONV - 1 - j][0:T])
            dx = w[GDN_CONV - 1:GDN_CONV] * dy_t
            for j in range(GDN_CONV - 1):
                dx = dx + w[j:j + 1] * pltpu.roll(dy, TE - (GDN_CONV - 1 - j), 0)[0:T]
            dx_ref[:, cols] = dx.astype(BF16)

        for h in range(H):
            sl = slice(h * GDN_D, (h + 1) * GDN_D)
            strip(sl, dq_ref, dqn_ref, sl, GDN_D ** -0.5)
            strip(slice(512 + h * GDN_D, 512 + (h + 1) * GDN_D), dk_ref, dkn_ref, sl, 1.0)
            strip(slice(1024 + h * GDN_D, 1024 + (h + 1) * GDN_D), dv_ref, dvn_ref, sl, None)
        ab = ab_ref[...]
        valid = (i * T + _iota2((T, 1), 0)) >= PAD_ROWS
        lane = _iota2((T, C_AB), 1)
        lane1 = _iota2((1, 128), 1)
        dab = jnp.zeros((T, C_AB), F32)
        dsc_a = jnp.zeros((1, 128), F32)
        dsc_d = jnp.zeros((1, 128), F32)
        for h in range(H):
            a_coef, dtb = _gdn_gate_consts(alog_ref, dtb_ref, h)
            pre = ab[:, h:h + 1] + dtb
            dgh = jnp.where(valid, dg_ref[:, h * GDN_D:h * GDN_D + 1], 0.0)
            da = dgh * a_coef * _sigmoid(pre)
            beta = _sigmoid(ab[:, H + h:H + h + 1])
            db = jnp.where(valid, dbeta_ref[:, h * GDN_D:h * GDN_D + 1], 0.0) * beta * (1.0 - beta)
            dab = dab + jnp.where(lane == h, da, 0.0) + jnp.where(lane == H + h, db, 0.0)
            dsc_a = dsc_a + jnp.where(lane1 == h, _colsum(dgh * a_coef * _softplus(pre)), 0.0)
            dsc_d = dsc_d + jnp.where(lane1 == h, _colsum(da), 0.0)
        dab_ref[...] = dab.astype(BF16)
        dsc_ref[0:1, :] += dsc_a
        dsc_ref[1:2, :] += dsc_d

    t8 = T // 8
    nb8 = LP // 8
    prev8 = lambda w: pl.BlockSpec((8, w), lambda i: (jnp.maximum(i * t8 - 1, 0), 0))
    next8 = lambda w: pl.BlockSpec((8, w), lambda i: (jnp.minimum((i + 1) * t8, nb8 - 1), 0))
    row = lambda w: pl.BlockSpec((T, w), lambda i: (i, 0))
    small = lambda r, c: pl.BlockSpec((r, c), lambda i: (0, 0))
    return _call_carrying(
        carry, body, nt, name="gdn_pre_bwd",
        in_specs=[row(C), prev8(C), next8(C), pl.BlockSpec((T, C_AB), lambda i: (i, OFF_AB // C_AB)),
                  small(GDN_CONV, C), small(1, H), small(1, H),
                  row(512), next8(512), row(512), next8(512), row(512), next8(512), row(512), row(512)],
        out_specs=[row(C), row(C_AB), small(GDN_CONV, C), small(2, 128)],
        out_shape=[jax.ShapeDtypeStruct((LP, C), BF16), jax.ShapeDtypeStruct((LP, C_AB), BF16),
                   jax.ShapeDtypeStruct((GDN_CONV, C), F32), jax.ShapeDtypeStruct((2, 128), F32)],
        operands=(proj, proj, proj, proj, conv_w, a_log, dt_bias, dq, dq, dk, dk, dv, dv, dbeta, dg))


def _tri_masks():
    r = _iota2((GDN_CHUNK, GDN_CHUNK), 0)
    c = _iota2((GDN_CHUNK, GDN_CHUNK), 1)
    return r >= c, r > c


def _gdn_chunk_common(q, k, v, beta, gb):
    incl, strict = _tri_masks()
    l_incl = incl.astype(BF16)
    gd = _dot_exact_l(l_incl, jnp.where(strict, gb[:, :GDN_CHUNK], 0.0))
    gc = _dot_exact_l(l_incl, gb)
    decay = jnp.where(incl, jnp.exp(jnp.where(incl, gd, 0.0)), 0.0)
    exp_g = jnp.exp(gc)
    g_last = gc[GDN_CHUNK - 1:GDN_CHUNK, :]
    kd_fac = jnp.exp(g_last - gc)
    gl = jnp.exp(g_last)
    kb = k * beta
    kk = _dot1(kb, k, _dot_nt)
    return dict(incl=incl, strict=strict, decay=decay, exp_g=exp_g, kd_fac=kd_fac, gl=gl, kb=kb, kk=kk,
                vb=v * beta, kbg=kb * exp_g)


def _interleave(gens):
    gens = list(gens)
    while gens:
        alive = []
        for g in gens:
            try:
                next(g)
                alive.append(g)
            except StopIteration:
                pass
        gens = alive


def _call_carrying(ex, body, nsteps, *, name, in_specs, out_specs, out_shape, operands, scratch_shapes=()):
    n_in, n_out, n_scr = len(in_specs), len(out_specs), len(scratch_shapes)
    n = ex.n if ex is not None else 0

    def full(*refs):
        o0 = n_in + n
        s0 = o0 + n_out + n
        ex_refs = (refs[n_in:o0], refs[o0 + n_out:s0], refs[s0 + n_scr:])
        step = pl.program_id(0)
        _carry_begin(ex, ex_refs, step, nsteps)
        body(*refs[:n_in], *refs[o0:o0 + n_out], *refs[s0:s0 + n_scr])
        _carry_end(ex, ex_refs, step, nsteps)

    res = pl.pallas_call(
        full, name=name, grid=(nsteps,),
        in_specs=list(in_specs) + [ANY_SPEC] * n, out_specs=list(out_specs) + [ANY_SPEC] * n,
        out_shape=list(out_shape) + (ex.out_shapes if ex is not None else []),
        scratch_shapes=list(scratch_shapes) + (ex.scratch if ex is not None else []),
        compiler_params=pltpu.CompilerParams(dimension_semantics=("arbitrary",), vmem_limit_bytes=VMEM_LIMIT,
                                             has_side_effects=ex is not None),
    )(*operands, *(ex.arrs if ex is not None else []))
    return list(res[:n_out]), list(res[n_out:])


def _gdn_chunk_fwd(qn, kn, v, beta_b, g_b, carry=None):
    LP = qn.shape[0]
    R = GDN_ROWS
    H = GDN_HEADS
    CH = GDN_CHUNK

    def body(q_ref, k_ref, v_ref, b_ref, g_ref, u_ref, w_ref, qd_ref, kd_ref, qk_ref, t_ref, gl_ref):
        def item(cc, h):
            rs = slice(cc * CH, (cc + 1) * CH)
            sl = slice(h * GDN_D, (h + 1) * GDN_D)
            s64 = slice(h * CH, (h + 1) * CH)
            q, k = q_ref[rs, sl], k_ref[rs, sl]
            m = _gdn_chunk_common(q, k, v_ref[rs, sl], b_ref[rs, sl], g_ref[rs, sl])
            qk_raw = _dot1(q, k, _dot_nt)
            yield
            a = jnp.where(m["strict"], m["kk"] * m["decay"], 0.0)
            eye = (_iota2((CH, CH), 0) == _iota2((CH, CH), 1)).astype(F32)
            t = eye - a
            p = _dot3(a, a)
            yield
            for _ in range(4):
                t = t + _dot3(t, p)
                p = _dot3(p, p)
                yield
            t = t + _dot3(t, p)
            yield
            u_ref[rs, sl] = _dot1(t, m["vb"])
            w_ref[rs, sl] = _dot1(t, m["kbg"])
            qk_ref[rs, s64] = qk_raw * m["decay"]
            t_ref[rs, s64] = t
            qd_ref[rs, sl] = q * m["exp_g"]
            kd_ref[rs, sl] = k * m["kd_fac"]
            gl_ref[cc * 8:(cc + 1) * 8, sl] = jnp.broadcast_to(m["gl"], (8, GDN_D))

        _interleave(item(cc, h) for cc in range(R // CH) for h in range(H))

    row = lambda w: pl.BlockSpec((R, w), lambda i: (i, 0))
    o512 = jax.ShapeDtypeStruct((LP, 512), F32)
    o256 = jax.ShapeDtypeStruct((LP, 256), F32)
    return _call_carrying(
        carry, body, LP // R, name="gdn_chunk_fwd",
        in_specs=[row(512)] * 5,
        out_specs=[row(512)] * 4 + [row(256)] * 2 + [pl.BlockSpec((R // 8, 512), lambda i: (i, 0))],
        out_shape=[o512] * 4 + [o256] * 2 + [jax.ShapeDtypeStruct((LP // 8, 512), F32)],
        operands=(qn, kn, v, beta_b, g_b))


def _gdn_chunk_bwd(qn, kn, v, beta_b, g_b, t_all, du, dw, dqd, dkd, dqk, dgl):
    LP = qn.shape[0]
    R = GDN_ROWS
    H = GDN_HEADS
    CH = GDN_CHUNK

    def body(q_ref, k_ref, v_ref, b_ref, g_ref, t_ref, du_ref, dw_ref, dqd_ref, dkd_ref, dqk_ref, dgl_ref,
             dq_ref, dk_ref, dv_ref, db_ref, dg_ref):
        ones = jnp.ones((CH, GDN_D), BF16)

        def item(cc, h):
            rs = slice(cc * CH, (cc + 1) * CH)
            sl = slice(h * GDN_D, (h + 1) * GDN_D)
            s64 = slice(h * CH, (h + 1) * CH)
            q, k, vv, beta = q_ref[rs, sl], k_ref[rs, sl], v_ref[rs, sl], b_ref[rs, sl]
            m = _gdn_chunk_common(q, k, vv, beta, g_ref[rs, sl])
            incl, strict, decay = m["incl"], m["strict"], m["decay"]
            t = t_ref[rs, s64]
            du_, dw_ = du_ref[rs, sl], dw_ref[rs, sl]
            dqd_, dkd_ = dqd_ref[rs, sl], dkd_ref[rs, sl]
            d_t = _dot1(du_, m["vb"], _dot_nt) + _dot1(dw_, m["kbg"], _dot_nt)
            dvb = _dot1(t, du_, _dot_tn)
            dkbg = _dot1(t, dw_, _dot_tn)
            qk_raw = _dot1(q, k, _dot_nt)
            yield
            x1 = _dot3(d_t, t, _dot_nt)
            dkb = dkbg * m["exp_g"]
            d_gi = _rowsum(dkbg * m["kbg"])
            yield
            d_a = jnp.where(strict, -_dot3(t, x1, _dot_tn), 0.0)
            yield
            d_kk = d_a * decay
            dqk_m = jnp.where(incl, dqk_ref[rs, s64], 0.0)
            dqk_raw = dqk_m * decay
            mm = (d_a * m["kk"] + dqk_m * qk_raw) * decay
            dkb = dkb + _dot1(d_kk, k)
            dk_ = _dot1(d_kk, m["kb"], _dot_tn) + _dot1(dqk_raw, q, _dot_tn)
            dq_ = _dot1(dqk_raw, k) + dqd_ * m["exp_g"]
            d_gi = d_gi + (_dot_exact_r(mm, ones) - _dot_exact_r(mm, ones, _dot_tn))
            yield
            d_gi = d_gi + _rowsum(dqd_ * q * m["exp_g"])
            e = _rowsum(dkd_ * k * m["kd_fac"])
            d_gi = d_gi - e
            d_glast = _colsum(jnp.broadcast_to(e, (CH, GDN_D))) + dgl_ref[cc * 8:cc * 8 + 1, sl] * m["gl"]
            dk_ = dk_ + dkd_ * m["kd_fac"] + dkb * beta
            d_gi = d_gi + jnp.where(_iota2((CH, GDN_D), 0) == CH - 1, d_glast, 0.0)
            u_incl = (_iota2((CH, CH), 1) >= _iota2((CH, CH), 0)).astype(BF16)
            dq_ref[rs, sl] = dq_
            dk_ref[rs, sl] = dk_
            dv_ref[rs, sl] = dvb * beta
            db_ref[rs, sl] = jnp.broadcast_to(_rowsum(dvb * vv) + _rowsum(dkb * k), (CH, GDN_D))
            dg_ref[rs, sl] = _dot_exact_l(u_incl, d_gi)

        _interleave(item(cc, h) for cc in range(R // CH) for h in range(H))

    row = lambda w: pl.BlockSpec((R, w), lambda i: (i, 0))
    o512 = jax.ShapeDtypeStruct((LP, 512), F32)
    gl_spec = pl.BlockSpec((R // 8, 512), lambda i: (i, 0))
    return pl.pallas_call(
        body, name="gdn_chunk_bwd", grid=(LP // R,),
        in_specs=[row(512)] * 5 + [row(256)] + [row(512)] * 4 + [row(256), gl_spec],
        out_specs=[row(512)] * 5, out_shape=[o512] * 5,
        compiler_params=_cp(("parallel",)),
    )(qn, kn, v, beta_b, g_b, t_all, du, dw, dqd, dkd, dqk, dgl)


def _gdn_scan_fwd(u, w, qd, kd, qk, gl):
    LP = u.shape[0]
    CH = GDN_CHUNK
    CPS = SCAN_CHUNKS
    N = LP // CH
    NS = N // CPS
    H = GDN_HEADS

    def body(u_ref, w_ref, qd_ref, kd_ref, qk_ref, gl_ref, o_ref, ssave_ref, s_sc):
        @pl.when(pl.program_id(0) == 0)
        def _():
            s_sc[...] = jnp.zeros_like(s_sc)

        for cc in range(CPS):
            rs = slice(cc * CH, (cc + 1) * CH)
            ssave_ref[cc * GDN_D:(cc + 1) * GDN_D, :] = s_sc[...]

            def item(h):
                sl = slice(h * GDN_D, (h + 1) * GDN_D)
                s = s_sc[:, sl]
                v_new = u_ref[rs, sl] - _dot1(w_ref[rs, sl], s)
                o_s = _dot1(qd_ref[rs, sl], s)
                yield
                o_ref[rs, sl] = o_s + _dot1(qk_ref[rs, h * CH:(h + 1) * CH], v_new)
                s_sc[:, sl] = s * gl_ref[cc * 8:cc * 8 + 1, sl] + _dot1(kd_ref[rs, sl], v_new, _dot_tn)

            _interleave(item(h) for h in range(H))

    row = lambda w_: pl.BlockSpec((CPS * CH, w_), lambda n: (n, 0))
    return pl.pallas_call(
        body, name="gdn_scan_fwd", grid=(NS,),
        in_specs=[row(512)] * 4 + [row(256), pl.BlockSpec((CPS * 8, 512), lambda n: (n, 0))],
        out_specs=[row(512), pl.BlockSpec((CPS * GDN_D, 512), lambda n: (n, 0))],
        out_shape=[jax.ShapeDtypeStruct((LP, 512), F32), jax.ShapeDtypeStruct((N * GDN_D, 512), F32)],
        scratch_shapes=[pltpu.VMEM((GDN_D, 512), F32)],
        compiler_params=_cp(("arbitrary",)),
    )(u, w, qd, kd, qk, gl)


def _gdn_scan_bwd(u, w, qd, kd, qk, gl, ssave, do, carry=None):
    LP = u.shape[0]
    CH = GDN_CHUNK
    CPS = SCAN_CHUNKS
    N = LP // CH
    NS = N // CPS
    H = GDN_HEADS

    def body(u_ref, w_ref, qd_ref, kd_ref, qk_ref, gl_ref, s_ref, do_ref,
             du_ref, dw_ref, dqd_ref, dkd_ref, dqk_ref, dgl_ref, ds_sc):
        @pl.when(pl.program_id(0) == 0)
        def _():
            ds_sc[...] = jnp.zeros_like(ds_sc)

        for cc in reversed(range(CPS)):
            rs = slice(cc * CH, (cc + 1) * CH)
            r8 = slice(cc * 8, (cc + 1) * 8)

            def item(h):
                sl = slice(h * GDN_D, (h + 1) * GDN_D)
                s64 = slice(h * CH, (h + 1) * CH)
                s = s_ref[cc * GDN_D:(cc + 1) * GDN_D, sl]
                ds = ds_sc[:, sl]
                do_ = do_ref[rs, sl]
                w_, qd_, kd_, qk_ = w_ref[rs, sl], qd_ref[rs, sl], kd_ref[rs, sl], qk_ref[rs, s64]
                v_new = u_ref[rs, sl] - _dot1(w_, s)
                d_vnew = _dot1(qk_, do_, _dot_tn) + _dot1(kd_, ds)
                dqd_ref[rs, sl] = _dot1(do_, s, _dot_nt)
                ds_new = ds * gl_ref[cc * 8:cc * 8 + 1, sl] + _dot1(qd_, do_, _dot_tn)
                dgl_ref[r8, sl] = jnp.broadcast_to(jnp.sum(_colsum(ds * s), axis=-1, keepdims=True), (8, GDN_D))
                yield
                du_ref[rs, sl] = d_vnew
                dw_ref[rs, sl] = -_dot1(d_vnew, s, _dot_nt)
                dkd_ref[rs, sl] = _dot1(v_new, ds, _dot_nt)
                dqk_ref[rs, s64] = _dot1(do_, v_new, _dot_nt)
                ds_sc[:, sl] = ds_new - _dot1(w_, d_vnew, _dot_tn)

            _interleave(item(h) for h in range(H))

    rev = lambda w_: pl.BlockSpec((CPS * CH, w_), lambda n: (NS - 1 - n, 0))
    rev8 = pl.BlockSpec((CPS * 8, 512), lambda n: (NS - 1 - n, 0))
    o512 = jax.ShapeDtypeStruct((LP, 512), F32)
    return _call_carrying(
        carry, body, NS, name="gdn_scan_bwd",
        in_specs=[rev(512)] * 4 + [rev(256), rev8, pl.BlockSpec((CPS * GDN_D, 512), lambda n: (NS - 1 - n, 0)),
                  rev(512)],
        out_specs=[rev(512)] * 4 + [rev(256), rev8],
        out_shape=[o512] * 4 + [jax.ShapeDtypeStruct((LP, 256), F32), jax.ShapeDtypeStruct((LP // 8, 512), F32)],
        scratch_shapes=[pltpu.VMEM((GDN_D, 512), F32)],
        operands=(u, w, qd, kd, qk, gl, ssave, do))


def _sb_scores(qh, kblk, mask):
    z = _dot_nt(qh, kblk)
    e = jnp.exp(-jnp.abs(z))
    sp = jnp.maximum(z, 0.0) + jnp.log(1.0 + e)
    return z, e, jnp.where(mask, -sp, 0.0), z - sp


def _sb_fwd(proj):
    LP = proj.shape[0]
    B = SB_BLOCK
    W = min(SB_SPAN, LP)
    SUB = SB_SUB
    Q = min(SB_QTILE, LP)
    nq = LP // Q
    nsub = W // SUB
    scale = SB_DH ** -0.5
    qcol, kcol, vcol = OFF_SB // B, (OFF_SB + 512) // B, (OFF_SB + 1024) // B

    def body(q_ref, k_ref, v_ref, tri_ref, o_ref, c_ref, n_ref):
        i = pl.program_id(1)
        lane = _iota2((Q, B), 1)
        head_a = lane < SB_DH
        qs = q_ref[...] * scale
        qh = [jnp.where(head_a, qs, 0.0).astype(BF16), jnp.where(head_a, 0.0, qs).astype(BF16)]
        u_strict = tri_ref[...]
        qpos = i * Q + _iota2((Q, W), 0)
        hi0 = (i + 1) * Q
        nspan = (hi0 + W - 1) // W

        def live(st):
            return (st[0] < nspan) & (st[1] > 0)

        def span(st):
            r, carry = st[0], st[2:]
            hi = hi0 - r * W
            k0 = pl.multiple_of(jnp.maximum(hi - W, 0), B)
            kblk = k_ref[pl.ds(k0, W), :].astype(BF16)
            vblk = v_ref[pl.ds(k0, W), :].astype(BF16)
            kpos = k0 + _iota2((Q, W), 1)
            mask = (kpos < qpos) & (kpos >= PAD_ROWS) & (kpos < hi)
            new = [None] * 4

            def head(h):
                o_acc, c = carry[2 * h], carry[2 * h + 1]
                z, e, l1m, lsg = _sb_scores(qh[h], kblk, mask)
                yield
                subs = [slice(b * SUB, (b + 1) * SUB) for b in range(nsub)]
                suf = [_dot(l1m[:, bs].astype(BF16), u_strict) for bs in subs]
                yield
                parts = [None] * nsub
                for b in reversed(range(nsub)):
                    parts[b] = jnp.where(mask[:, subs[b]], jnp.exp(lsg[:, subs[b]] + suf[b] + c), 0.0)
                    c = c + _rowsum(l1m[:, subs[b]])
                att = jnp.concatenate(parts, axis=1).astype(BF16)
                new[2 * h], new[2 * h + 1] = o_acc + _dot(att, vblk), c

            _interleave(head(h) for h in range(2))
            more = (jnp.maximum(jnp.max(new[1]), jnp.max(new[3])) > SB_DEAD).astype(jnp.int32)
            return (r + 1, more, *new)

        zero_o = jnp.zeros((Q, B), F32)
        zero_c = jnp.zeros((Q, 1), F32)
        nrun, _, o_a, c_a, o_b, c_b = lax.while_loop(
            live, span, (jnp.int32(0), jnp.int32(1), zero_o, zero_c, zero_o, zero_c))
        o_ref[...] = jnp.where(head_a, o_a, o_b)
        c_ref[...] = jnp.where(head_a, c_a, c_b)
        n_ref[pl.program_id(0), i] = nrun

    blk = pl.BlockSpec((Q, B), lambda p, i: (i, p))
    out = jax.ShapeDtypeStruct((LP, 512), F32)
    return pl.pallas_call(
        body, name="sb_fwd", grid=(SB_HEADS // 2, nq),
        in_specs=[pl.BlockSpec((Q, B), lambda p, i: (i, qcol + p)),
                  pl.BlockSpec((LP, B), lambda p, i: (0, kcol + p)),
                  pl.BlockSpec((LP, B), lambda p, i: (0, vcol + p)),
                  pl.BlockSpec((SUB, SUB), lambda p, i: (0, 0))],
        out_specs=[blk, blk, pl.BlockSpec(memory_space=pltpu.SMEM)],
        out_shape=[out, out, jax.ShapeDtypeStruct((SB_HEADS // 2, nq), jnp.int32)],
        compiler_params=_cp(("arbitrary", "arbitrary")),
    )(proj, proj, proj, jnp.tril(jnp.ones((SUB, SUB), BF16), -1))


def _sb_bwd(proj, ctot, nrun_all, do):
    LP = proj.shape[0]
    B = SB_BLOCK
    W = min(SB_SPAN, LP)
    SUB = SB_SUB
    Q = min(SB_QTILE, LP)
    nq = LP // Q
    nsub = W // SUB
    scale = SB_DH ** -0.5
    qcol, kcol, vcol = OFF_SB // B, (OFF_SB + 512) // B, (OFF_SB + 1024) // B

    def body(n_ref, q_ref, k_ref, v_ref, c_ref, do_ref, tril_ref, triu_ref, dq_ref, dk_ref, dv_ref):
        i = pl.program_id(1)

        @pl.when(i == 0)
        def _():
            dk_ref[...] = jnp.zeros_like(dk_ref)
            dv_ref[...] = jnp.zeros_like(dv_ref)

        lane = _iota2((Q, B), 1)
        head_a = lane < SB_DH
        qs = q_ref[...] * scale
        qh = [jnp.where(head_a, qs, 0.0).astype(BF16), jnp.where(head_a, 0.0, qs).astype(BF16)]
        dof = do_ref[...]
        doh = [jnp.where(head_a, dof, 0.0).astype(BF16), jnp.where(head_a, 0.0, dof).astype(BF16)]
        cfull = c_ref[...]
        ctot_h = [cfull[:, 0:1], cfull[:, SB_DH:SB_DH + 1]]
        u_strict = tril_ref[...]
        l_strict = triu_ref[...]
        qpos = i * Q + _iota2((Q, W), 0)
        hi0 = (i + 1) * Q
        nrun = n_ref[pl.program_id(0), i]

        def span(t, carry):
            r = nrun - 1 - t
            hi = hi0 - r * W
            k0 = pl.multiple_of(jnp.maximum(hi - W, 0), B)
            kblk = k_ref[pl.ds(k0, W), :].astype(BF16)
            vblk = v_ref[pl.ds(k0, W), :].astype(BF16)
            kpos = k0 + _iota2((Q, W), 1)
            mask = (kpos < qpos) & (kpos >= PAD_ROWS) & (kpos < hi)
            new = [None] * 6
            dk_add, dv_add = [None, None], [None, None]
            subs = [slice(b * SUB, (b + 1) * SUB) for b in range(nsub)]

            def head(h):
                dq_acc, pre, ecar = carry[3 * h], carry[3 * h + 1], carry[3 * h + 2]
                z, e, l1m, lsg = _sb_scores(qh[h], kblk, mask)
                d_att = _dot_nt(doh[h], vblk)
                yield
                sig = jnp.exp(lsg)
                suf = [_dot(l1m[:, bs].astype(BF16), u_strict) for bs in subs]
                yield
                att_parts, p_parts = [None] * nsub, [None] * nsub
                for b, bs in enumerate(subs):
                    pre = pre + _rowsum(l1m[:, bs])
                    att_parts[b] = jnp.where(mask[:, bs], jnp.exp(lsg[:, bs] + suf[b] + (ctot_h[h] - pre)), 0.0)
                    p_parts[b] = att_parts[b] * d_att[:, bs]
                pcum = [_dot(p.astype(BF16), l_strict) for p in p_parts]
                yield
                dz_parts = [None] * nsub
                for b, bs in enumerate(subs):
                    sg = sig[:, bs]
                    dz_parts[b] = jnp.where(mask[:, bs], p_parts[b] * (1.0 - sg) - sg * (ecar + pcum[b]), 0.0)
                    ecar = ecar + _rowsum(p_parts[b])
                att = jnp.concatenate(att_parts, axis=1).astype(BF16)
                dz = jnp.concatenate(dz_parts, axis=1).astype(BF16)
                new[3 * h:3 * h + 3] = [dq_acc + _dot(dz, kblk), pre, ecar]
                dk_add[h] = _dot_tn(dz, qh[h])
                dv_add[h] = _dot_tn(att, doh[h])

            _interleave(head(h) for h in range(2))
            dk_ref[pl.ds(k0, W), :] += dk_add[0] + dk_add[1]
            dv_ref[pl.ds(k0, W), :] += dv_add[0] + dv_add[1]
            return tuple(new)

        zero_o = jnp.zeros((Q, B), F32)
        zero_c = jnp.zeros((Q, 1), F32)
        res = lax.fori_loop(0, nrun, span, (zero_o, zero_c, zero_c, zero_o, zero_c, zero_c))
        dq_ref[...] = (jnp.where(head_a, res[0], res[3]) * scale).astype(BF16)

    blk = pl.BlockSpec((Q, B), lambda p, i: (i, p))
    col = pl.BlockSpec((LP, B), lambda p, i: (0, p))
    tri = pl.BlockSpec((SUB, SUB), lambda p, i: (0, 0))
    out = jax.ShapeDtypeStruct((LP, 512), F32)
    return pl.pallas_call(
        body, name="sb_bwd", grid=(SB_HEADS // 2, nq),
        in_specs=[pl.BlockSpec(memory_space=pltpu.SMEM),
                  pl.BlockSpec((Q, B), lambda p, i: (i, qcol + p)),
                  pl.BlockSpec((LP, B), lambda p, i: (0, kcol + p)),
                  pl.BlockSpec((LP, B), lambda p, i: (0, vcol + p)),
                  blk, blk, tri, tri],
        out_specs=[blk, col, col], out_shape=[jax.ShapeDtypeStruct((LP, 512), BF16), out, out],
        compiler_params=_cp(("arbitrary", "arbitrary")),
    )(nrun_all, proj, proj, proj, ctot, do, jnp.tril(jnp.ones((SUB, SUB), BF16), -1),
      jnp.triu(jnp.ones((SUB, SUB), BF16), 1))


def _sb_group_mean():
    return jnp.kron(jnp.eye(SB_HEADS, dtype=F32), jnp.full((SB_DH, SB_DH), 1.0 / SB_DH, F32)).astype(BF16)


GM_SPEC = pl.BlockSpec((SB_HEADS * SB_DH, SB_HEADS * SB_DH), lambda i: (0, 0))


def _attn_norm_fwd(og, proj, osb, gnw, snw):
    LP = og.shape[0]
    T = _tile(LP, 256)

    def body(og_ref, z_ref, os_ref, gnw_ref, snw_ref, gm_ref, y_ref):
        valid = (pl.program_id(0) * T + _iota2((T, 1), 0)) >= PAD_ROWS
        z = z_ref[...]
        zg = z * _sigmoid(z)
        for h in range(GDN_HEADS):
            sl = slice(h * GDN_D, (h + 1) * GDN_D)
            o = og_ref[:, sl]
            y = o * _rms(o) * gnw_ref[...] * zg[:, sl]
            y_ref[:, sl] = jnp.where(valid, y, 0.0).astype(BF16)
        o = os_ref[...]
        msq = _dot_exact_r(o * o, gm_ref[...])
        y = o * lax.rsqrt(msq + NORM_EPS) * snw_ref[...]
        y_ref[:, 512:] = jnp.where(valid, y, 0.0).astype(BF16)

    row = pl.BlockSpec((T, 512), lambda i: (i, 0))
    return pl.pallas_call(
        body, name="attn_norm_fwd", grid=(LP // T,),
        in_specs=[row, pl.BlockSpec((T, 512), lambda i: (i, OFF_Z // 512)), row,
                  pl.BlockSpec((1, GDN_D), lambda i: (0, 0)), pl.BlockSpec((1, 512), lambda i: (0, 0)), GM_SPEC],
        out_specs=pl.BlockSpec((T, 1024), lambda i: (i, 0)),
        out_shape=jax.ShapeDtypeStruct((LP, 1024), BF16),
        compiler_params=_cp(("parallel",)),
    )(og, proj, osb, gnw, snw, _sb_group_mean())


def _attn_norm_bwd(og, proj, osb, gnw, snw, dy, carry=None):
    LP = og.shape[0]
    T = _tile(LP, 256)

    def body(og_ref, z_ref, os_ref, gnw_ref, snw_ref, dy_ref, gm_ref, dog_ref, dz_ref, dos_ref, dgw_ref, dsw_ref):
        @pl.when(pl.program_id(0) == 0)
        def _():
            dgw_ref[...] = jnp.zeros_like(dgw_ref)
            dsw_ref[...] = jnp.zeros_like(dsw_ref)
        valid = (pl.program_id(0) * T + _iota2((T, 1), 0)) >= PAD_ROWS
        dy = jnp.where(valid, dy_ref[...], 0.0)
        z = z_ref[...]
        sg = _sigmoid(z)
        zg = z * sg
        dgw = jnp.zeros((1, GDN_D), F32)
        for h in range(GDN_HEADS):
            sl = slice(h * GDN_D, (h + 1) * GDN_D)
            o = og_ref[:, sl]
            dyh = dy[:, sl]
            dx, dwn = _rms_bwd(o, gnw_ref[...], dyh * zg[:, sl])
            dog_ref[:, sl] = dx
            dgw = dgw + _colsum(dwn)
            yn = o * _rms(o) * gnw_ref[...]
            dz_ref[:, sl] = (dyh * yn * (sg[:, sl] * (1.0 + z[:, sl] * (1.0 - sg[:, sl])))).astype(BF16)
        dgw_ref[...] += dgw
        o = os_ref[...]
        gm = gm_ref[...]
        r = lax.rsqrt(_dot_exact_r(o * o, gm) + NORM_EPS)
        n = o * r
        dys = dy[:, 512:]
        dyw = dys * snw_ref[...]
        dos_ref[...] = r * (dyw - n * _dot_exact_r(dyw * n, gm))
        dsw_ref[...] += _colsum(dys * n)

    row = pl.BlockSpec((T, 512), lambda i: (i, 0))
    gw = pl.BlockSpec((1, GDN_D), lambda i: (0, 0))
    sw = pl.BlockSpec((1, 512), lambda i: (0, 0))
    o512 = jax.ShapeDtypeStruct((LP, 512), F32)
    return _call_carrying(
        carry, body, LP // T, name="attn_norm_bwd",
        in_specs=[row, pl.BlockSpec((T, 512), lambda i: (i, OFF_Z // 512)), row, gw, sw,
                  pl.BlockSpec((T, 1024), lambda i: (i, 0)), GM_SPEC],
        out_specs=[row, row, row, gw, sw],
        out_shape=[o512, jax.ShapeDtypeStruct((LP, 512), BF16), o512, jax.ShapeDtypeStruct((1, GDN_D), F32),
                   jax.ShapeDtypeStruct((1, 512), F32)],
        operands=(og, proj, osb, gnw, snw, dy, _sb_group_mean()))


def _resid_fwd(h0, mix, w_post, w_pre):
    LP, D = h0.shape
    T = _tile(LP, 512)

    def body(h0_ref, mix_ref, wp_ref, wf_ref, h1_ref, n2_ref):
        mix = mix_ref[...]
        h1 = h0_ref[...] + mix * _rms(mix) * wp_ref[...]
        h1_ref[...] = h1
        n2_ref[...] = (h1 * _rms(h1) * wf_ref[...]).astype(BF16)

    row = pl.BlockSpec((T, D), lambda i: (i, 0))
    vec = pl.BlockSpec((1, D), lambda i: (0, 0))
    return pl.pallas_call(
        body, name="resid_fwd", grid=(LP // T,),
        in_specs=[row, row, vec, vec], out_specs=[row, row],
        out_shape=[jax.ShapeDtypeStruct((LP, D), F32), jax.ShapeDtypeStruct((LP, D), BF16)],
        compiler_params=_cp(("parallel",)),
    )(h0, mix, w_post, w_pre)


def _resid_bwd(h1, mix, w_post, w_pre, dout, dn2):
    LP, D = h1.shape
    T = _tile(LP, 512)

    def body(h1_ref, mix_ref, wp_ref, wf_ref, dout_ref, dn2_ref, dh1_ref, dmix_ref, dwf_ref, dwp_ref):
        @pl.when(pl.program_id(0) == 0)
        def _():
            dwf_ref[...] = jnp.zeros_like(dwf_ref)
            dwp_ref[...] = jnp.zeros_like(dwp_ref)
        dx, dwn = _rms_bwd(h1_ref[...], wf_ref[...], dn2_ref[...])
        dh1 = dout_ref[...] + dx
        dh1_ref[...] = dh1
        dwf_ref[...] += _colsum(dwn)
        dmix, dwn2 = _rms_bwd(mix_ref[...], wp_ref[...], dh1)
        dmix_ref[...] = dmix.astype(BF16)
        dwp_ref[...] += _colsum(dwn2)

    row = pl.BlockSpec((T, D), lambda i: (i, 0))
    vec = pl.BlockSpec((1, D), lambda i: (0, 0))
    v = jax.ShapeDtypeStruct((1, D), F32)
    return pl.pallas_call(
        body, name="resid_bwd", grid=(LP // T,),
        in_specs=[row, row, vec, vec, row, row], out_specs=[row, row, vec, vec],
        out_shape=[jax.ShapeDtypeStruct((LP, D), F32), jax.ShapeDtypeStruct((LP, D), BF16), v, v],
        compiler_params=_cp(("arbitrary",)),
    )(h1, mix, w_post, w_pre, dout, dn2)


GELU_C = 0.7978845608028654
GELU_A = 0.044715


def _gelu_parts(x):
    t = jnp.tanh(GELU_C * (x + GELU_A * x * x * x))
    return 0.5 * x * (1.0 + t), t


def _convglu_fwd(up, conv_w, conv_b):
    LP, C = up.shape
    T = _tile(LP, 128)

    def body(x_ref, halo_ref, cw_ref, cb_ref, act_ref, y_ref):
        i = pl.program_id(0)

        def conv(cols):
            ext = jnp.concatenate([jnp.where(i > 0, halo_ref[:, cols], 0.0), x_ref[:, cols]], axis=0)
            w = cw_ref[:, cols]
            y = (w[2:3] * ext[8:] + w[1:2] * pltpu.roll(ext, 1, 0)[8:] + w[0:1] * pltpu.roll(ext, 2, 0)[8:]
                 + cb_ref[:, cols])
            y_ref[:, cols] = y.astype(BF16)
            return y

        for s in range(D_FF // LANE):
            gs = slice(s * LANE, (s + 1) * LANE)
            g, _ = _gelu_parts(conv(gs))
            act_ref[:, gs] = (g * conv(slice(D_FF + s * LANE, D_FF + (s + 1) * LANE))).astype(BF16)

    t8 = T // 8
    return pl.pallas_call(
        body, name="convglu_fwd", grid=(LP // T,),
        in_specs=[pl.BlockSpec((T, C), lambda i: (i, 0)),
                  pl.BlockSpec((8, C), lambda i: (jnp.maximum(i * t8 - 1, 0), 0)),
                  pl.BlockSpec((FFN_CONV, C), lambda i: (0, 0)), pl.BlockSpec((1, C), lambda i: (0, 0))],
        out_specs=[pl.BlockSpec((T, D_FF), lambda i: (i, 0)), pl.BlockSpec((T, C), lambda i: (i, 0))],
        out_shape=[jax.ShapeDtypeStruct((LP, D_FF), BF16), jax.ShapeDtypeStruct((LP, C), BF16)],
        compiler_params=_cp(("parallel",)),
    )(up, up, conv_w, conv_b)


def _convglu_bwd(up, y, conv_w, dact):
    LP, C = up.shape
    T = _tile(LP, 128)
    TE = T + 8
    nt = LP // T

    def body(x_ref, y_ref, yn_ref, cw_ref, da_ref, dan_ref, dx_ref, dcw_ref, dcb_ref):
        i = pl.program_id(0)

        @pl.when(i == 0)
        def _():
            dcw_ref[...] = jnp.zeros_like(dcw_ref)
            dcb_ref[...] = jnp.zeros_like(dcb_ref)

        last = i == nt - 1

        def back(cols, dy):
            w = cw_ref[:, cols]
            later = [dy[0:T], pltpu.roll(dy, TE - 1, 0)[0:T], pltpu.roll(dy, TE - 2, 0)[0:T]]
            x_t = x_ref[:, cols]
            dcb_ref[:, cols] += _colsum(later[0])
            for j in range(FFN_CONV):
                dcw_ref[j:j + 1, cols] += _colsum(later[FFN_CONV - 1 - j] * x_t)
            dx_ref[:, cols] = (w[2:3] * later[0] + w[1:2] * later[1] + w[0:1] * later[2]).astype(BF16)

        for s in range(D_FF // LANE):
            gs = slice(s * LANE, (s + 1) * LANE)
            vs = slice(D_FF + s * LANE, D_FF + (s + 1) * LANE)
            gate = jnp.concatenate([y_ref[:, gs].astype(F32), yn_ref[0:8, gs].astype(F32)], axis=0)
            val = jnp.concatenate([y_ref[:, vs].astype(F32), yn_ref[0:8, vs].astype(F32)], axis=0)
            g, t = _gelu_parts(gate)
            dg_dx = 0.5 * (1.0 + t) + 0.5 * gate * (1.0 - t * t) * GELU_C * (1.0 + 3.0 * GELU_A * gate * gate)
            da = jnp.concatenate([da_ref[:, gs], jnp.where(last, 0.0, dan_ref[:, gs])], axis=0)
            back(gs, da * val * dg_dx)
            back(vs, da * g)

    t8 = T // 8
    nb8 = LP // 8
    next8 = lambda w: pl.BlockSpec((8, w), lambda i: (jnp.minimum((i + 1) * t8, nb8 - 1), 0))
    row = lambda w: pl.BlockSpec((T, w), lambda i: (i, 0))
    small = lambda r: pl.BlockSpec((r, C), lambda i: (0, 0))
    return pl.pallas_call(
        body, name="convglu_bwd", grid=(nt,),
        in_specs=[row(C), row(C), pl.BlockSpec((16, C), lambda i: (jnp.minimum((i + 1) * (T // 16), LP // 16 - 1), 0)),
                  small(FFN_CONV), row(D_FF), next8(D_FF)],
        out_specs=[row(C), small(FFN_CONV), small(1)],
        out_shape=[jax.ShapeDtypeStruct((LP, C), BF16), jax.ShapeDtypeStruct((FFN_CONV, C), F32),
                   jax.ShapeDtypeStruct((1, C), F32)],
        compiler_params=_cp(("arbitrary",)),
    )(up, y, y, conv_w, dact, dact)


def _final(h1, f, w_post, target, n_real):
    LP, D = h1.shape
    T = _tile(LP, 256)

    def body(h1_ref, f_ref, w_ref, t_ref, loss_ref, dout_ref, df_ref, dw_ref):
        @pl.when(pl.program_id(0) == 0)
        def _():
            loss_ref[...] = jnp.zeros_like(loss_ref)
            dw_ref[...] = jnp.zeros_like(dw_ref)
        rows = pl.program_id(0) * T + _iota2((T, 1), 0)
        real = (rows >= ROW0) & (rows < ROW0 + n_real)
        f = f_ref[...]
        out = h1_ref[...] + f * _rms(f) * w_ref[...]
        err = jnp.where(real, out - t_ref[...], 0.0)
        loss_ref[...] += 0.5 * jnp.sum(_colsum(jnp.mean(err * err, axis=-1, keepdims=True)), axis=-1, keepdims=True)
        dout = err * (1.0 / D)
        dout_ref[...] = dout
        dx, dwn = _rms_bwd(f, w_ref[...], dout)
        df_ref[...] = dx.astype(BF16)
        dw_ref[...] += _colsum(dwn)

    row = pl.BlockSpec((T, D), lambda i: (i, 0))
    vec = pl.BlockSpec((1, D), lambda i: (0, 0))
    return pl.pallas_call(
        body, name="final_loss", grid=(LP // T,),
        in_specs=[row, row, vec, row],
        out_specs=[pl.BlockSpec((1, 128), lambda i: (0, 0)), row, row, vec],
        out_shape=[jax.ShapeDtypeStruct((1, 128), F32), jax.ShapeDtypeStruct((LP, D), F32),
                   jax.ShapeDtypeStruct((LP, D), BF16), jax.ShapeDtypeStruct((1, D), F32)],
        compiler_params=_cp(("arbitrary",)),
    )(h1, f, w_post, target)


ANY_SPEC = pl.BlockSpec(memory_space=pl.ANY)
N_CHIP = 4


def _other_chips(x, y):
    return [(1 - x, y), (x, 1 - y), (1 - x, 1 - y)]


def _gather_direct(arrs, name):
    n = len(arrs)
    npeer = N_DEV - 1

    def body(*refs):
        ins, outs = refs[:n], refs[n:2 * n]
        send_sems, recv_sems, loc_sems = refs[2 * n:]
        x, y, c = lax.axis_index("x"), lax.axis_index("y"), lax.axis_index("c")
        me = 4 * x + 2 * y + c
        copies = []
        for a in range(n):
            for kk in range(1, N_DEV):
                px = 1 - x if kk & 4 else x
                py = 1 - y if kk & 2 else y
                pc = 1 - c if kk & 1 else c
                s = a * npeer + kk - 1
                cp = pltpu.make_async_remote_copy(src_ref=ins[a], dst_ref=outs[a].at[me], send_sem=send_sems.at[s],
                                                  recv_sem=recv_sems.at[s], device_id=(px, py, pc), device_id_type=MESH)
                cp.start()
                copies.append(cp)
            own = pltpu.make_async_copy(ins[a], outs[a].at[me], loc_sems.at[a])
            own.start()
            copies.append(own)
        for cp in copies:
            cp.wait()

    shapes = [jax.ShapeDtypeStruct((N_DEV,) + tuple(a.shape), a.dtype) for a in arrs]
    return pl.pallas_call(
        body, name=name, in_specs=[ANY_SPEC] * n, out_specs=[ANY_SPEC] * n, out_shape=shapes,
        scratch_shapes=[pltpu.SemaphoreType.DMA((n * npeer,)), pltpu.SemaphoreType.DMA((n * npeer,)),
                        pltpu.SemaphoreType.DMA((n,))],
        compiler_params=pltpu.CompilerParams(has_side_effects=True),
    )(*arrs)


class _Exchange:
    def __init__(self, arrs, out_shapes, scratch, start, finish, mid=None):
        self.arrs, self.out_shapes, self.scratch = list(arrs), list(out_shapes), list(scratch)
        self.start, self.finish, self.mid = start, finish, mid

    @property
    def n(self):
        return len(self.arrs)


def _run_exchange(ex, name):
    n = ex.n

    def body(*refs):
        ins, outs, sems = refs[:n], refs[n:2 * n], refs[2 * n:]
        ex.start(ins, outs, sems)
        if ex.mid is not None:
            ex.mid(ins, outs, sems)
        ex.finish(ins, outs, sems)

    return pl.pallas_call(
        body, name=name, in_specs=[ANY_SPEC] * n, out_specs=[ANY_SPEC] * n, out_shape=ex.out_shapes,
        scratch_shapes=ex.scratch, compiler_params=pltpu.CompilerParams(has_side_effects=True),
    )(*ex.arrs)


def _carry_begin(ex, refs, step, nsteps):
    if ex is None:
        return

    @pl.when(step == 0)
    def _():
        ex.start(*refs)

    if ex.mid is not None:
        @pl.when(step == min(nsteps - 1, (3 * nsteps) // 5))
        def _():
            ex.mid(*refs)


def _carry_end(ex, refs, step, nsteps):
    if ex is None:
        return

    @pl.when(step == nsteps - 1)
    def _():
        ex.finish(*refs)


def _gather_two_level(arrs):
    n = len(arrs)
    K = 7

    def env(ins, outs, sems):
        send_sems, recv_sems, loc_sems = sems
        x, y, c = lax.axis_index("x"), lax.axis_index("y"), lax.axis_index("c")

        def cp(a, k, src, slot, to):
            return pltpu.make_async_remote_copy(src_ref=src, dst_ref=outs[a].at[slot], send_sem=send_sems.at[a * K + k],
                                                recv_sem=recv_sems.at[a * K + k], device_id=to, device_id_type=MESH)

        me = 4 * x + 2 * y + c
        owns = [pltpu.make_async_copy(ins[a], outs[a].at[me], loc_sems.at[a]) for a in range(n)]
        first = []
        for a in range(n):
            first.append(cp(a, 0, ins[a], me, (x, y, 1 - c)))
            first += [cp(a, 1 + j, ins[a], me, (px, py, c)) for j, (px, py) in enumerate(_other_chips(x, y))]
        passed = []
        for j, (px, py) in enumerate(_other_chips(x, y)):
            slot = 4 * px + 2 * py + c
            passed += [(cp(a, 1 + j, ins[a], slot, (px, py, c)), cp(a, 4 + j, outs[a].at[slot], slot, (x, y, 1 - c)))
                       for a in range(n)]
        from_sib = []
        for a in range(n):
            from_sib.append(cp(a, 0, ins[a], 4 * x + 2 * y + (1 - c), (x, y, 1 - c)))
            from_sib += [cp(a, 4 + j, ins[a], 4 * px + 2 * py + (1 - c), (x, y, 1 - c))
                         for j, (px, py) in enumerate(_other_chips(x, y))]
        return owns, first, passed, from_sib

    def start(ins, outs, sems):
        owns, first, _, _ = env(ins, outs, sems)
        for cp in owns + first:
            cp.start()

    def mid(ins, outs, sems):
        _, _, passed, _ = env(ins, outs, sems)
        for arrival, fwd in passed:
            arrival.wait_recv()
            fwd.start()

    def finish(ins, outs, sems):
        owns, first, passed, from_sib = env(ins, outs, sems)
        for cp in from_sib:
            cp.wait_recv()
        for cp in first + [fwd for _, fwd in passed]:
            cp.wait_send()
        for cp in owns:
            cp.wait()

    shapes = [jax.ShapeDtypeStruct((N_DEV,) + tuple(a.shape), a.dtype) for a in arrs]
    scratch = [pltpu.SemaphoreType.DMA((n * K,)), pltpu.SemaphoreType.DMA((n * K,)), pltpu.SemaphoreType.DMA((n,))]
    return _Exchange(arrs, shapes, scratch, start, finish, mid)


def _swap_sibling(arrs):
    n = len(arrs)

    def copies(ins, outs, sems):
        send_sems, recv_sems = sems
        x, y, c = lax.axis_index("x"), lax.axis_index("y"), lax.axis_index("c")
        return [pltpu.make_async_remote_copy(src_ref=ins[a], dst_ref=outs[a], send_sem=send_sems.at[a],
                                             recv_sem=recv_sems.at[a], device_id=(x, y, 1 - c), device_id_type=MESH)
                for a in range(n)]

    def start(ins, outs, sems):
        for cp in copies(ins, outs, sems):
            cp.start()

    def finish(ins, outs, sems):
        for cp in copies(ins, outs, sems):
            cp.wait()

    shapes = [jax.ShapeDtypeStruct(tuple(a.shape), a.dtype) for a in arrs]
    return _Exchange(arrs, shapes, [pltpu.SemaphoreType.DMA((n,)), pltpu.SemaphoreType.DMA((n,))], start, finish)


def _exchange_chips(arrs):
    n = len(arrs)
    K = N_CHIP - 1

    def copies(ins, outs, sems):
        send_sems, recv_sems, loc_sems = sems
        x, y, c = lax.axis_index("x"), lax.axis_index("y"), lax.axis_index("c")
        mine = 2 * x + y
        out = []
        for a in range(n):
            out += [pltpu.make_async_remote_copy(src_ref=ins[a].at[2 * px + py], dst_ref=outs[a].at[mine],
                                                 send_sem=send_sems.at[a * K + j], recv_sem=recv_sems.at[a * K + j],
                                                 device_id=(px, py, c), device_id_type=MESH)
                    for j, (px, py) in enumerate(_other_chips(x, y))]
            out.append(pltpu.make_async_copy(ins[a].at[mine], outs[a].at[mine], loc_sems.at[a]))
        return out

    def start(ins, outs, sems):
        for cp in copies(ins, outs, sems):
            cp.start()

    def finish(ins, outs, sems):
        for cp in copies(ins, outs, sems):
            cp.wait()

    shapes = [jax.ShapeDtypeStruct(tuple(a.shape), a.dtype) for a in arrs]
    scratch = [pltpu.SemaphoreType.DMA((n * K,)), pltpu.SemaphoreType.DMA((n * K,)), pltpu.SemaphoreType.DMA((n,))]
    return _Exchange(arrs, shapes, scratch, start, finish)


def _add_halves(mine, theirs, name):
    _, R, C = mine.shape
    cap = max(16, (ELEMWISE_VMEM // (4 * C * 10)) // 16 * 16)
    T = R if R <= cap else _tile(R, cap, 16)

    def body(a_ref, b_ref, o_ref):
        o_ref[...] = (a_ref[...] + b_ref[...].astype(F32)).astype(BF16)

    blk = pl.BlockSpec((N_CHIP, T, C), lambda i: (0, i, 0))
    return pl.pallas_call(
        body, name=name, grid=(R // T,), in_specs=[blk, blk], out_specs=blk,
        out_shape=jax.ShapeDtypeStruct(mine.shape, BF16), compiler_params=_cp(("parallel",)),
    )(mine, theirs)


def _adamw(parts, w, m, v, name):
    R, C = w.shape
    npart = parts.shape[0]
    cap = max(16, (ELEMWISE_VMEM // (4 * C * 12)) // 16 * 16)
    T = R if R <= cap else _tile(R, cap, 16)

    def body(p_ref, w_ref, m_ref, v_ref, g_ref, d_ref, nm_ref, nv_ref):
        g = p_ref[0].astype(F32)
        for k in range(1, npart):
            g = g + p_ref[k].astype(F32)
        mm = ADAM_B1 * m_ref[...] + (1.0 - ADAM_B1) * g
        vv = ADAM_B2 * v_ref[...] + (1.0 - ADAM_B2) * (g * g)
        m_hat = mm / (1.0 - ADAM_B1 ** ADAM_STEP)
        v_hat = vv / (1.0 - ADAM_B2 ** ADAM_STEP)
        g_ref[...] = g
        d_ref[...] = -ADAM_LR * (m_hat / (jnp.sqrt(v_hat) + ADAM_EPS) + ADAM_WD * w_ref[...])
        nm_ref[...] = mm
        nv_ref[...] = vv

    row = pl.BlockSpec((T, C), lambda i: (i, 0))
    out = jax.ShapeDtypeStruct((R, C), F32)
    return pl.pallas_call(
        body, name=name, grid=(R // T,),
        in_specs=[pl.BlockSpec((npart, T, C), lambda i: (0, i, 0)), row, row, row],
        out_specs=[row] * 4, out_shape=[out] * 4,
        compiler_params=_cp(("parallel",)),
    )(parts, w, m, v)


SMALL = ("attn_pre_norm", "gdn_A_log", "gdn_dt_bias", "gdn_norm_w", "sb_norm_w", "attn_post_norm",
         "ffn_pre_norm", "ffn_conv_b", "ffn_post_norm")


def _pack_small(arrs):
    rows = []
    for a in arrs:
        flat = a.reshape(-1).astype(F32)
        n = -(-flat.shape[0] // 128) * 128
        rows.append(jnp.pad(flat, (0, n - flat.shape[0])).reshape(-1, 128))
    slab = jnp.concatenate(rows, axis=0)
    pad = (-slab.shape[0]) % 8
    return jnp.pad(slab, ((0, pad), (0, 0)))


def _unpack_small(slab, shapes):
    out, r = [], 0
    for shp in shapes:
        size = 1
        for s in shp:
            size *= s
        nr = -(-size // 128)
        out.append(slab[r:r + nr].reshape(-1)[:size].reshape(shp))
        r += nr
    return out


def _to_blocks_cols(a):
    R, C = a.shape
    return a.reshape(R, N_DEV, C // N_DEV).transpose(1, 0, 2)


def _from_blocks_cols(a):
    n, R, c = a.shape
    return a.transpose(1, 0, 2).reshape(R, n * c)


def kernel(x, meta_tokens, attn_pre_norm, w_in, gdn_conv_w, gdn_A_log, gdn_dt_bias, gdn_norm_w, sb_norm_w, w_out, attn_post_norm, ffn_pre_norm, w_ffn_up, ffn_conv_w, ffn_conv_b, w_ffn_down, ffn_post_norm, loss_target, m_meta_tokens, m_attn_pre_norm, m_w_in, m_gdn_conv_w, m_gdn_A_log, m_gdn_dt_bias, m_gdn_norm_w, m_sb_norm_w, m_w_out, m_attn_post_norm, m_ffn_pre_norm, m_w_ffn_up, m_ffn_conv_w, m_ffn_conv_b, m_w_ffn_down, m_ffn_post_norm, v_meta_tokens, v_attn_pre_norm, v_w_in, v_gdn_conv_w, v_gdn_A_log, v_gdn_dt_bias, v_gdn_norm_w, v_sb_norm_w, v_w_out, v_attn_post_norm, v_ffn_pre_norm, v_w_ffn_up, v_ffn_conv_w, v_ffn_conv_b, v_w_ffn_down, v_ffn_post_norm):
    args = dict(locals())
    seq = x.shape[1]
    LP = -(-(ROW0 + seq) // LP_ALIGN) * LP_ALIGN
    tail = LP - ROW0 - seq

    meta_f = _from_blocks_cols(_run_exchange(_gather_two_level([meta_tokens]), "gather_meta")[0])

    (h0, target), got = _build_rows(x[0], meta_f, loss_target[0], LP,
                                    carry=_gather_two_level([w_in[0].astype(BF16), gdn_conv_w[0]]))
    (u,), _ = _prenorm_fwd(h0, attn_pre_norm)
    win_o = _from_blocks_cols(got[0])
    o_ab = C_QKV
    o_z = o_ab + 2 * GDN_HEADS
    w_inp = jnp.concatenate([win_o[:, :C_QKV], win_o[:, o_z:o_z + C_Z], win_o[:, o_z + C_Z:],
                             win_o[:, o_ab:o_z], jnp.zeros((D_MODEL, C_AB - 2 * GDN_HEADS), BF16)], axis=1)
    gconv_f = _from_blocks_cols(got[1])
    proj = _mm(u, w_inp, F32, "mm_in")
    (qn, kn, vg, beta_b, g_b), got = _gdn_pre_fwd(
        proj, gconv_f, gdn_A_log, gdn_dt_bias,
        carry=_gather_two_level([w_out[0].astype(BF16), w_ffn_down[0].astype(BF16)]))
    w_out_f = got[0].reshape(D_MODEL, D_MODEL)
    w_down_f = got[1].reshape(D_FF, D_MODEL)
    (cu, cw, cqd, ckd, cqk, ct, cgl), got = _gdn_chunk_fwd(
        qn, kn, vg, beta_b, g_b, carry=_gather_two_level([w_ffn_up[0].astype(BF16), ffn_conv_w[0]]))
    w_up_f = _from_blocks_cols(got[0])
    fconv_f = _from_blocks_cols(got[1])
    og, ssave = _gdn_scan_fwd(cu, cw, cqd, ckd, cqk, cgl)
    osb, ctot, sb_nrun = _sb_fwd(proj)
    snw = sb_norm_w.reshape(1, SB_HEADS * SB_DH)
    y = _attn_norm_fwd(og, proj, osb, gdn_norm_w, snw)
    mix = _mm(y, w_out_f, F32, "mm_out")
    h1, n2 = _resid_fwd(h0, mix, attn_post_norm, ffn_pre_norm)
    up = _mm(n2, w_up_f, F32, "mm_up")
    act, conv_y = _convglu_fwd(up, fconv_f, ffn_conv_b)
    f = _mm(act, w_down_f, F32, "mm_down")
    loss_part, dout, df, d_fpost = _final(h1, f, ffn_post_norm, target, seq)

    d_wdown = _mm_tn(act, df, "mm_dw_down")
    dact = _mm_nt(df, w_down_f, F32, "mm_dact")
    dup, d_fconv, d_fconvb = _convglu_bwd(up, conv_y, fconv_f, dact)
    d_wup = _mm_tn(n2, dup, "mm_dw_up")
    dn2 = _mm_nt(dup, w_up_f, F32, "mm_dn2")
    dh1, dmix, d_fpre, d_apost = _resid_bwd(h1, mix, attn_post_norm, ffn_pre_norm, dout, dn2)
    d_wout = _mm_tn(y, dmix, "mm_dw_out")
    dy = _mm_nt(dmix, w_out_f, F32, "mm_dy")
    my_c = lax.axis_index("c")

    def core_halves(blocks):
        halves = [s.reshape((N_CHIP, 2) + s.shape[1:]) for s in blocks]
        return ([lax.dynamic_index_in_dim(h, my_c, axis=1, keepdims=False) for h in halves],
                [lax.dynamic_index_in_dim(h, 1 - my_c, axis=1, keepdims=False).astype(BF16) for h in halves])

    early_names = ("w_out", "w_ffn_up", "w_ffn_down", "ffn_conv_w")
    e_mine, e_send = core_halves([d_wout.reshape(N_DEV, D_MODEL // N_DEV, D_MODEL), _to_blocks_cols(d_wup),
                                  d_wdown.reshape(N_DEV, D_FF // N_DEV, D_MODEL), _to_blocks_cols(d_fconv)])
    (dog, dz, dos, d_gnw, d_snw), e_theirs = _attn_norm_bwd(og, proj, osb, gdn_norm_w, snw, dy,
                                                            carry=_swap_sibling(e_send))
    e_sums = [_add_halves(a, b, "grads_add_" + nm) for nm, a, b in zip(early_names, e_mine, e_theirs)]
    dqs, dks, dvs = _sb_bwd(proj, ctot, sb_nrun, dos)
    (du_, dw_, dqd_, dkd_, dqk_, dgl_), _ = _gdn_scan_bwd(cu, cw, cqd, ckd, cqk, cgl, ssave, dog)
    dqn, dkn, dvg, dbeta, dg = _gdn_chunk_bwd(qn, kn, vg, beta_b, g_b, ct, du_, dw_, dqd_, dkd_, dqk_, dgl_)
    (dqkv, dab, d_gconv, d_gsc), e_recv = _gdn_pre_bwd(proj, gconv_f, gdn_A_log, gdn_dt_bias, dqn, dkn, dvg, dbeta, dg,
                                                       carry=_exchange_chips(e_sums))
    dpieces = [dqkv, dz, dqs, dks, dvs, dab]
    doffs = [0, OFF_Z, OFF_SB, OFF_SB + 512, OFF_SB + 1024, OFF_AB]
    dw_qkv, dw_ab = _mm_tn_pieces(u, [dqkv, dab], "mm_dw_in_gdn")
    dw_z, dw_qs, dw_ks, dw_vs = _mm_tn_pieces(u, [dz, dqs, dks, dvs], "mm_dw_in_rest")
    du0 = _mm_nt_pieces(dpieces, doffs, w_inp, F32, "mm_du")
    d_win = jnp.concatenate([dw_qkv, dw_ab[:, :2 * GDN_HEADS], dw_z, dw_qs, dw_ks, dw_vs], axis=1)
    late_names = ("w_in", "gdn_conv_w")
    l_mine, l_send = core_halves([_to_blocks_cols(d_win), _to_blocks_cols(d_gconv)])
    l_theirs = _run_exchange(_swap_sibling(l_send), "grads_swap_sibling")
    l_sums = [_add_halves(a, b, "grads_add_" + nm) for nm, a, b in zip(late_names, l_mine, l_theirs)]
    (dh0, d_apre), l_recv = _prenorm_bwd(h0, attn_pre_norm, du0, dh1, carry=_exchange_chips(l_sums))
    grad_x = dh0[ROW0:ROW0 + seq][None]
    d_meta = dh0[PAD_ROWS:ROW0]

    small_grads = [d_apre, d_gsc[0:1, :GDN_HEADS], d_gsc[1:2, :GDN_HEADS], d_gnw, d_snw.reshape(1, SB_HEADS, SB_DH),
                   d_apost, d_fpre, d_fconvb, d_fpost]
    loss_rows = jnp.pad(loss_part, ((0, 7), (0, 0)))
    n_param_rows = _pack_small(small_grads).shape[0]
    n_small_rows = n_param_rows + loss_rows.shape[0]
    slab_parts = _gather_direct(
        [jnp.concatenate([_pack_small(small_grads), loss_rows, d_meta.reshape(-1, LANE)], axis=0)],
        name="gather_small_grads")[0]
    me = 4 * lax.axis_index("x") + 2 * lax.axis_index("y") + my_c
    meta_parts = lax.dynamic_index_in_dim(
        slab_parts[:, n_small_rows:].reshape(N_DEV, N_META, N_DEV, LANE), me, axis=2, keepdims=False)
    slab_parts = slab_parts[:, :n_small_rows]

    res = {}
    for nm, parts in zip(early_names + late_names + ("meta_tokens",), list(e_recv) + list(l_recv) + [meta_parts]):
        wloc = args[nm]
        shp = wloc.shape
        w2 = wloc.reshape(shp[-2], shp[-1])
        outs = _adamw(parts, w2, args["m_" + nm].reshape(w2.shape), args["v_" + nm].reshape(w2.shape), "adamw_" + nm)
        res[nm] = [o.reshape(shp) for o in outs]
    small_shapes = [args[nm].shape for nm in SMALL]
    with_loss_rows = lambda slab: jnp.pad(slab, ((0, n_small_rows - n_param_rows), (0, 0)))
    outs = _adamw(slab_parts, with_loss_rows(_pack_small([args[nm] for nm in SMALL])),
                  with_loss_rows(_pack_small([args["m_" + nm] for nm in SMALL])),
                  with_loss_rows(_pack_small([args["v_" + nm] for nm in SMALL])), "adamw_small")
    loss = outs[0][n_param_rows, 0]
    for k in range(4):
        for nm, val in zip(SMALL, _unpack_small(outs[k], small_shapes)):
            res.setdefault(nm, [None] * 4)[k] = val

    order = ("meta_tokens", "attn_pre_norm", "w_in", "gdn_conv_w", "gdn_A_log", "gdn_dt_bias", "gdn_norm_w",
             "sb_norm_w", "w_out", "attn_post_norm", "ffn_pre_norm", "w_ffn_up", "ffn_conv_w", "ffn_conv_b",
             "w_ffn_down", "ffn_post_norm")
    return (loss, grad_x, *[res[nm][0] for nm in order], *[res[nm][1] for nm in order],
            *[res[nm][2] for nm in order], *[res[nm][3] for nm in order])
```
